```python
import jax, jax.numpy as jnp
from jax import lax
import numpy as np

D_MODEL = 1024
BATCH = 8
SEQ = 2048
DEPTH = 2
DEC_BATCH = 128
DEC_SEQ = 4
PAST_LEN = 16384
PAGE_SIZE = 128

N_META = 16
D_MIX = D_MODEL
D_A = 3 * D_MIX // 8
D_B = D_MIX // 4
D_C = D_MIX - D_A - D_B
N_BLK_A = 8
BLK_A = D_A // N_BLK_A
CONV_A = 4
RG_C = 8.0
CONV_B = 31
N_HEADS_C = 4
DV_C = D_C // N_HEADS_C
DK_C = DV_C // 2
D_QK_C = N_HEADS_C * DK_C
GATE_RANK = 16
GATE_TAU = 16.0
CHUNK_C = 64
D_IN = 2 * D_A + 2 * D_B + 2 * D_QK_C + 2 * D_C + GATE_RANK
N_EXPERTS = 32
TOP_K = 4
D_FF = D_MODEL
SWIGLU_LIMIT = 7.0
SWIGLU_ALPHA = 1.702
MOE_BLOCK = 128
DEEPNORM_ALPHA = (2 * DEPTH) ** 0.25
DEEPNORM_BETA = (8 * DEPTH) ** -0.25
LN_EPS = 1e-5

kernel_name = "hymba_rglru_conformer_gla_moe_step"


def layer_norm(x, g, b):
    xf = x.astype(jnp.float32)
    mu = jnp.mean(xf, axis=-1, keepdims=True)
    var = jnp.mean(jnp.square(xf - mu), axis=-1, keepdims=True)
    y = (xf - mu) * lax.rsqrt(var + LN_EPS) * g.astype(jnp.float32) + b.astype(jnp.float32)
    return y.astype(x.dtype)


def causal_depthwise_conv(x, buf, w, b):
    width = w.shape[0]
    xp = jnp.concatenate([buf.astype(x.dtype), x], axis=1)
    y = lax.conv_general_dilated(xp, w.astype(x.dtype)[:, None, :], window_strides=(1,), padding="VALID",
                                 dimension_numbers=("NWC", "WIO", "NWC"), feature_group_count=x.shape[-1])
    return y + b, xp[:, -(width - 1):]


def _linear_recurrence_combine(e1, e2):
    a1, b1 = e1
    a2, b2 = e2
    return a1 * a2, a2 * b1 + b2


def rglru(xc, h0, w_rg, b_rg, w_ig, b_ig, lru_lambda):
    bsz, t, _ = xc.shape
    xh = xc.reshape(bsz, t, N_BLK_A, BLK_A)
    r = jax.nn.sigmoid(jnp.einsum("bthi,hij->bthj", xh, w_rg).reshape(bsz, t, D_A) + b_rg)
    i = jax.nn.sigmoid(jnp.einsum("bthi,hij->bthj", xh, w_ig).reshape(bsz, t, D_A) + b_ig)
    log_a = -RG_C * jax.nn.softplus(-lru_lambda.astype(jnp.float32)) * r.astype(jnp.float32)
    a = jnp.exp(log_a)
    u = jnp.sqrt(-jnp.expm1(2.0 * log_a)) * (i * xc).astype(jnp.float32)
    u = u.at[:, 0].add(a[:, 0] * h0.astype(jnp.float32))
    _, h = lax.associative_scan(_linear_recurrence_combine, (a, u), axis=1)
    return h.astype(xc.dtype), h[:, -1].astype(xc.dtype)


def gla_chunk_scan(q, k, v, g, s0, chunk):
    bsz, t, h, dk = q.shape
    dv = v.shape[-1]
    n = t // chunk

    def to_chunks(z):
        return z.reshape(bsz, n, chunk, *z.shape[2:]).swapaxes(0, 1)

    mask = jnp.tril(jnp.ones((chunk, chunk), dtype=bool))[None, :, :, None, None]

    def step(s, inp):
        qc, kc, vc, gc = inp
        gcum = jnp.cumsum(gc, axis=1)
        g_last = gcum[:, -1]
        o_inter = jnp.einsum("bthd,bhde->bthe", qc * jnp.exp(gcum), s)
        diff = gcum[:, :, None] - gcum[:, None, :]
        decay = jnp.where(mask, jnp.exp(jnp.where(mask, diff, 0.0)), 0.0)
        scores = jnp.einsum("bthd,bshd,btshd->bhts", qc, kc, decay)
        o_intra = jnp.einsum("bhts,bshe->bthe", scores, vc)
        s_new = jnp.exp(g_last)[..., None] * s + jnp.einsum("bshd,bshe->bhde", kc * jnp.exp(g_last[:, None] - gcum), vc)
        return s_new, o_inter + o_intra

    s, o = lax.scan(step, s0, (to_chunks(q), to_chunks(k), to_chunks(v), to_chunks(g)))
    return o.swapaxes(0, 1).reshape(bsz, t, h, dv), s


def mixer(x, state, gla_segments, lp):
    conv_a_buf, h0, conv_b_buf, s0 = state
    bsz, t, _ = x.shape
    f32 = jnp.float32
    sizes = (D_A, D_A, D_B, D_B, D_QK_C, D_QK_C, D_C, D_C)
    xa, ga, vb, gb, qc, kc, vc, rc, zc = jnp.split(x @ lp["w_in"], np.cumsum(sizes).tolist(), axis=-1)
    xa_conv, conv_a_new = causal_depthwise_conv(xa, conv_a_buf, lp["conv_a_w"], lp["conv_a_b"])
    h, h_last = rglru(xa_conv, h0, lp["w_rg"], lp["b_rg"], lp["w_ig"], lp["b_ig"], lp["lru_lambda"])
    y_a = h * jax.nn.gelu(ga, approximate=True)
    u = vb * jax.nn.sigmoid(gb)
    uc, conv_b_new = causal_depthwise_conv(u, conv_b_buf, lp["conv_b_w"], lp["conv_b_b"])
    y_b = jax.nn.silu(layer_norm(uc, lp["ln_b_g"], lp["ln_b_b"]))
    q = qc.reshape(bsz, t, N_HEADS_C, DK_C).astype(f32) * (DK_C ** -0.5)
    k = kc.reshape(bsz, t, N_HEADS_C, DK_C).astype(f32)
    v = vc.reshape(bsz, t, N_HEADS_C, DV_C).astype(f32)
    g = jax.nn.log_sigmoid((zc @ lp["w_gate2"] + lp["b_gate"]).astype(f32)).reshape(bsz, t, N_HEADS_C, DK_C) / GATE_TAU
    s = s0.astype(f32)
    outs = []
    start = 0
    for seg_len, chunk in gla_segments:
        sl = slice(start, start + seg_len)
        o_seg, s = gla_chunk_scan(q[:, sl], k[:, sl], v[:, sl], g[:, sl], s, chunk)
        outs.append(o_seg)
        start += seg_len
    o = jnp.concatenate(outs, axis=1)
    o = o * lax.rsqrt(jnp.mean(jnp.square(o), axis=-1, keepdims=True) + LN_EPS) * lp["gla_norm_g"].astype(f32)
    y_c = o.reshape(bsz, t, D_C).astype(x.dtype) * jax.nn.silu(rc)
    y = jnp.concatenate([y_a, y_b, y_c], axis=-1) @ lp["w_out"]
    return y, (conv_a_new, h_last, conv_b_new, s.astype(x.dtype))


def moe(h, lp):
    bsz, t, d = h.shape
    xf = h.reshape(-1, d)
    n = xf.shape[0]
    m = n * TOP_K
    logits = (xf @ lp["router_w"] + lp["router_b"]).astype(jnp.float32)
    top_val, top_idx = lax.top_k(logits, TOP_K)
    gates = jax.nn.softmax(top_val, axis=-1).astype(h.dtype)
    flat_e = top_idx.reshape(-1)
    order = jnp.argsort(flat_e)
    sorted_e = flat_e[order]
    tok = order // TOP_K
    counts = jnp.bincount(flat_e, length=N_EXPERTS)
    padded = (counts + MOE_BLOCK - 1) // MOE_BLOCK * MOE_BLOCK
    pad_end = jnp.cumsum(padded)
    pad_start = pad_end - padded
    start = jnp.cumsum(counts) - counts
    dest = pad_start[sorted_e] + jnp.arange(m) - start[sorted_e]
    n_blocks = -(-m // MOE_BLOCK) + N_EXPERTS
    buf = jnp.zeros((n_blocks * MOE_BLOCK, d), h.dtype).at[dest].set(xf[tok])
    block_e = jnp.minimum(jnp.searchsorted(pad_end, jnp.arange(n_blocks) * MOE_BLOCK, side="right"), N_EXPERTS - 1)

    def expert_block(args):
        xb, e = args
        hid = xb @ lp["moe_w1"][e] + lp["moe_b1"][e]
        gate = jnp.minimum(hid[:, :D_FF], SWIGLU_LIMIT)
        lin = jnp.clip(hid[:, D_FF:], -SWIGLU_LIMIT, SWIGLU_LIMIT)
        act = gate * jax.nn.sigmoid(SWIGLU_ALPHA * gate) * (lin + 1.0)
        return act @ lp["moe_w2"][e] + lp["moe_b2"][e]

    out = lax.map(expert_block, (buf.reshape(n_blocks, MOE_BLOCK, d), block_e)).reshape(-1, d)
    rows = out[dest] * gates.reshape(-1)[order][:, None]
    y = jax.ops.segment_sum(rows, tok, num_segments=n)
    return y.reshape(bsz, t, d)


def trunk_layer(x, state, gla_segments, lp):
    y_mix, new_state = mixer(x, state, gla_segments, lp)
    h = layer_norm(DEEPNORM_ALPHA * x + y_mix, lp["ln1_g"], lp["ln1_b"])
    out = layer_norm(DEEPNORM_ALPHA * h + moe(h, lp), lp["ln2_g"], lp["ln2_b"])
    return out, new_state


def setup_inputs(seed: int = 0) -> dict:
    key = jax.random.key(seed)
    k = jax.random.split(key, 40)
    f32 = jnp.float32

    def nrm(i, shape, scale):
        return jax.random.normal(k[i], shape, f32) * scale

    a0 = jax.random.uniform(k[16], (DEPTH, D_A), f32, 0.9, 0.999)
    return {
        "x_prompt": nrm(0, (BATCH, SEQ, D_MODEL), 1.0),
        "x_sample": nrm(1, (DEC_BATCH, DEC_SEQ, D_MODEL), 1.0),
        "state_conv_a": nrm(2, (DEPTH, DEC_BATCH, CONV_A - 1, D_A), 1.0),
        "state_rglru": nrm(3, (DEPTH, DEC_BATCH, D_A), 0.5),
        "state_conv_b": nrm(4, (DEPTH, DEC_BATCH, CONV_B - 1, D_B), 1.0),
        "state_gla": nrm(5, (DEPTH, DEC_BATCH, N_HEADS_C, DK_C, DV_C), 0.5),
        "meta_tokens": nrm(6, (N_META, D_MODEL), 1.0),
        "ln0_g": 1.0 + nrm(7, (D_MODEL,), 0.05),
        "ln0_b": nrm(8, (D_MODEL,), 0.02),
        "w_in": nrm(9, (DEPTH, D_MODEL, D_IN), D_MODEL ** -0.5),
        "conv_a_w": nrm(10, (DEPTH, CONV_A, D_A), CONV_A ** -0.5),
        "conv_a_b": nrm(11, (DEPTH, D_A), 0.02),
        "w_rg": nrm(12, (DEPTH, N_BLK_A, BLK_A, BLK_A), BLK_A ** -0.5),
        "b_rg": nrm(13, (DEPTH, D_A), 0.02),
        "w_ig": nrm(14, (DEPTH, N_BLK_A, BLK_A, BLK_A), BLK_A ** -0.5),
        "b_ig": nrm(15, (DEPTH, D_A), 0.02),
        "lru_lambda": jnp.log(a0) - jnp.log1p(-a0),
        "conv_b_w": nrm(17, (DEPTH, CONV_B, D_B), CONV_B ** -0.5),
        "conv_b_b": nrm(18, (DEPTH, D_B), 0.02),
        "ln_b_g": 1.0 + nrm(19, (DEPTH, D_B), 0.05),
        "ln_b_b": nrm(20, (DEPTH, D_B), 0.02),
        "w_gate2": nrm(21, (DEPTH, GATE_RANK, D_QK_C), GATE_RANK ** -0.5),
        "b_gate": nrm(22, (DEPTH, D_QK_C), 0.02),
        "gla_norm_g": 1.0 + nrm(23, (DEPTH, DV_C), 0.05),
        "w_out": nrm(24, (DEPTH, D_MIX, D_MODEL), DEEPNORM_BETA * D_MIX ** -0.5),
        "ln1_g": 1.0 + nrm(25, (DEPTH, D_MODEL), 0.05),
        "ln1_b": nrm(26, (DEPTH, D_MODEL), 0.02),
        "router_w": nrm(27, (DEPTH, D_MODEL, N_EXPERTS), D_MODEL ** -0.5),
        "router_b": nrm(28, (DEPTH, N_EXPERTS), 0.01),
        "moe_w1": nrm(29, (DEPTH, N_EXPERTS, D_MODEL, 2 * D_FF), D_MODEL ** -0.5),
        "moe_b1": nrm(30, (DEPTH, N_EXPERTS, 2 * D_FF), 0.02),
        "moe_w2": nrm(31, (DEPTH, N_EXPERTS, D_FF, D_MODEL), DEEPNORM_BETA * D_FF ** -0.5),
        "moe_b2": nrm(32, (DEPTH, N_EXPERTS, D_MODEL), 0.02),
        "ln2_g": 1.0 + nrm(33, (DEPTH, D_MODEL), 0.05),
        "ln2_b": nrm(34, (DEPTH, D_MODEL), 0.02),
    }


def reference(x_prompt, x_sample, state_conv_a, state_rglru, state_conv_b, state_gla, meta_tokens, ln0_g, ln0_b,
              w_in, conv_a_w, conv_a_b, w_rg, b_rg, w_ig, b_ig, lru_lambda, conv_b_w, conv_b_b, ln_b_g, ln_b_b,
              w_gate2, b_gate, gla_norm_g, w_out, ln1_g, ln1_b, router_w, router_b, moe_w1, moe_b1, moe_w2, moe_b2,
              ln2_g, ln2_b):
    dt = x_prompt.dtype
    bp, seq = x_prompt.shape[0], x_prompt.shape[1]
    dseq = x_sample.shape[1]
    meta = jnp.broadcast_to(meta_tokens.astype(dt)[None], (bp, N_META, D_MODEL))
    hp = layer_norm(jnp.concatenate([meta, x_prompt], axis=1), ln0_g, ln0_b)
    hs = layer_norm(x_sample, ln0_g, ln0_b)
    prompt_segments = ((N_META, N_META), (seq, CHUNK_C))
    sample_segments = ((dseq, dseq),)
    zero_state = (jnp.zeros((bp, CONV_A - 1, D_A), dt), jnp.zeros((bp, D_A), dt),
                  jnp.zeros((bp, CONV_B - 1, D_B), dt), jnp.zeros((bp, N_HEADS_C, DK_C, DV_C), dt))
    new_p = ([], [], [], [])
    new_s = ([], [], [], [])
    for l in range(DEPTH):
        lp = {"w_in": w_in[l], "conv_a_w": conv_a_w[l], "conv_a_b": conv_a_b[l], "w_rg": w_rg[l], "b_rg": b_rg[l],
              "w_ig": w_ig[l], "b_ig": b_ig[l], "lru_lambda": lru_lambda[l], "conv_b_w": conv_b_w[l],
              "conv_b_b": conv_b_b[l], "ln_b_g": ln_b_g[l], "ln_b_b": ln_b_b[l], "w_gate2": w_gate2[l],
              "b_gate": b_gate[l], "gla_norm_g": gla_norm_g[l], "w_out": w_out[l], "ln1_g": ln1_g[l],
              "ln1_b": ln1_b[l], "router_w": router_w[l], "router_b": router_b[l], "moe_w1": moe_w1[l],
              "moe_b1": moe_b1[l], "moe_w2": moe_w2[l], "moe_b2": moe_b2[l], "ln2_g": ln2_g[l], "ln2_b": ln2_b[l]}
        hp, st_p = trunk_layer(hp, zero_state, prompt_segments, lp)
        hs, st_s = trunk_layer(hs, (state_conv_a[l], state_rglru[l], state_conv_b[l], state_gla[l]), sample_segments, lp)
        for j in range(4):
            new_p[j].append(st_p[j])
            new_s[j].append(st_s[j])
    return (hp[:, N_META:], hs,
            jnp.stack(new_p[0]), jnp.stack(new_p[1]), jnp.stack(new_p[2]), jnp.stack(new_p[3]),
            jnp.stack(new_s[0]), jnp.stack(new_s[1]), jnp.stack(new_s[2]), jnp.stack(new_s[3]))
```

```python
import functools

import jax
import jax.numpy as jnp
from jax import lax
from jax.experimental import pallas as pl
from jax.experimental.pallas import tpu as pltpu

F32 = jnp.float32
BF16 = jnp.bfloat16

D_MODEL = 1024
N_META = 16
D_A = 384
D_B = 256
D_C = 384
N_BLK_A = 8
BLK_A = D_A // N_BLK_A
CONV_A = 4
RG_C = 8.0
CONV_B = 31
N_HEADS_C = 4
DV_C = D_C // N_HEADS_C
DK_C = DV_C // 2
D_QK_C = N_HEADS_C * DK_C
GATE_RANK = 16
GATE_TAU = 16.0
N_EXPERTS = 32
TOP_K = 4
D_FF = D_MODEL
SWIGLU_LIMIT = 7.0
SWIGLU_ALPHA = 1.702
LN_EPS = 1e-5

PA_W = 2 * D_A
PB_W = 2 * D_B
PC_Q, PC_Z, PC_K, PC_V, PC_R, PC_W = 0, 192, 256, 512, 896, 1280
P_W = PA_W + PB_W + PC_W

SUBLANES = 8
VMEM_LIMIT_BYTES = 56 * 1024 * 1024
MOE_BM = 256
FF_CHUNK = 256
SAMPLE_PAD_T = 8


def _cparams(*sem):
    return pltpu.CompilerParams(dimension_semantics=sem, vmem_limit_bytes=VMEM_LIMIT_BYTES)


def _sds(shape, dtype=F32):
    return jax.ShapeDtypeStruct(shape, dtype)


def _pick(n, prefs):
    for p in prefs:
        if n % p == 0:
            return p
    raise ValueError(f"no tile for {n} in {prefs}")


def _ln(x, g, b):
    mu = jnp.mean(x, axis=-1, keepdims=True)
    xc = x - mu
    var = jnp.mean(xc * xc, axis=-1, keepdims=True)
    return xc * lax.rsqrt(var + LN_EPS) * g + b


def _sigmoid(x):
    return 1.0 / (1.0 + jnp.exp(-x))


def _split_bf16(x):
    hi = x.astype(BF16)
    lo = (x - hi.astype(F32)).astype(BF16)
    return hi, lo


def _dot(a, b):
    return jnp.dot(a, b, preferred_element_type=F32)


def _inproj_kernel(x_ref, g_ref, b_ref, w_ref, pa_ref, pb_ref, pc_ref, *maybe_xn, apply_ln):
    x = x_ref[...]
    if apply_ln:
        x = _ln(x, g_ref[...], b_ref[...])
        maybe_xn[0][...] = x
    xb = x.astype(BF16)
    pa_ref[...] = _dot(xb, w_ref[:, 0:PA_W])
    pb_ref[...] = _dot(xb, w_ref[:, PA_W:PA_W + PB_W])
    pc_ref[...] = _dot(xb, w_ref[:, PA_W + PB_W:P_W])


def _inproj(x, row0, nrows, ln_g, ln_b, w_packed, apply_ln):
    tm = _pick(nrows, (384, 512, 256, 128, 64, 32, 16, 8))
    while row0 % tm:
        tm //= 2
    off = row0 // tm
    const = lambda i: (0, 0)
    row = lambda i: (i, 0)
    out_shape = [_sds((nrows, PA_W)), _sds((nrows, PB_W)), _sds((nrows, PC_W))]
    out_specs = [pl.BlockSpec((tm, PA_W), row), pl.BlockSpec((tm, PB_W), row), pl.BlockSpec((tm, PC_W), row)]
    if apply_ln:
        out_shape.append(_sds((nrows, D_MODEL)))
        out_specs.append(pl.BlockSpec((tm, D_MODEL), row))
    return pl.pallas_call(
        functools.partial(_inproj_kernel, apply_ln=apply_ln),
        grid=(nrows // tm,),
        in_specs=[pl.BlockSpec((tm, D_MODEL), lambda i: (i + off, 0)),
                  pl.BlockSpec((1, D_MODEL), const), pl.BlockSpec((1, D_MODEL), const),
                  pl.BlockSpec((D_MODEL, P_W), const)],
        out_specs=out_specs, out_shape=out_shape,
        compiler_params=_cparams("parallel"), name="inproj",
    )(x, ln_g, ln_b, w_packed)


def _rglru_kernel(p_ref, cbuf_ref, h0_ref, cw_ref, cb_ref, wg_ref, bg_ref, lam_ref,
                  y_ref, cnew_ref, hlast_ref, xp_scr, a_scr, h_scr, *, T, Tc, Tv, Bb):
    lam = lam_ref[...]
    softplus_neg = jnp.maximum(-lam, 0.0) + jnp.log1p(jnp.exp(-jnp.abs(lam)))
    c_decay = -RG_C * softplus_neg
    cw = cw_ref[...]
    cb = cb_ref[...]
    bg = bg_ref[...]
    sub = lax.broadcasted_iota(jnp.int32, (Tc, D_A), 0) % SUBLANES
    halo = SUBLANES - (CONV_A - 1)
    for b in range(Bb):
        xp_scr[halo:SUBLANES, :] = cbuf_ref[b]
        xp_scr[SUBLANES:SUBLANES + T, :] = p_ref[b, :, 0:D_A]
        cnew_ref[b] = xp_scr[halo + Tv:SUBLANES + Tv, :]

        def chunk(ci, h_b):
            r0 = pl.multiple_of(ci * Tc, SUBLANES)
            win = xp_scr[pl.ds(r0, Tc + SUBLANES), :]
            xc = cb + cw[0:1] * win[halo:halo + Tc]
            for j in range(1, CONV_A):
                xc = xc + cw[j:j + 1] * win[halo + j:halo + j + Tc]
            gates = _dot(xc.astype(BF16), wg_ref[...]) + bg
            r = _sigmoid(gates[:, 0:D_A])
            i = _sigmoid(gates[:, D_A:2 * D_A])
            log_a = c_decay * r
            a = jnp.exp(log_a)
            u = jnp.sqrt(1.0 - a * a) * (i * xc)
            for s in (1, 2, 4):
                keep = sub >= s
                a_prev = pltpu.roll(a, s, 0)
                u_prev = pltpu.roll(u, s, 0)
                u = jnp.where(keep, a * u_prev + u, u)
                a = jnp.where(keep, a * a_prev, a)
            a_scr[...] = a
            h_scr[pl.ds(r0, Tc), :] = u

            def group(gi, h_b):
                c0 = pl.multiple_of(gi * SUBLANES, SUBLANES)
                g0 = pl.multiple_of(r0 + gi * SUBLANES, SUBLANES)
                h8 = a_scr[pl.ds(c0, SUBLANES), :] * h_b + h_scr[pl.ds(g0, SUBLANES), :]
                h_scr[pl.ds(g0, SUBLANES), :] = h8
                return jnp.broadcast_to(h8[SUBLANES - 1:SUBLANES, :], (SUBLANES, D_A))

            h_b = lax.fori_loop(0, Tc // SUBLANES, group, h_b)
            ga = p_ref[b, pl.ds(r0, Tc), D_A:2 * D_A]
            gelu = 0.5 * ga * (1.0 + jnp.tanh(0.7978845608028654 * (ga + 0.044715 * ga * ga * ga)))
            y_ref[b, pl.ds(r0, Tc), :] = h_scr[pl.ds(r0, Tc), :] * gelu
            return h_b

        h_b = jnp.broadcast_to(h0_ref[b], (SUBLANES, D_A))
        lax.fori_loop(0, T // Tc, chunk, h_b)
        hlast_ref[b] = h_scr[Tv - 1:Tv, :]


def _rglru(pa3, cbuf, h0, cw, cb, wg, bg, lam, Tv, Bb):
    B, T, _ = pa3.shape
    Tc = _pick(T, (344, 256, 128, 64, 48, 32, 16, 8))
    const2 = lambda i: (0, 0)
    seq3 = lambda i: (i, 0, 0)
    return pl.pallas_call(
        functools.partial(_rglru_kernel, T=T, Tc=Tc, Tv=Tv, Bb=Bb),
        grid=(B // Bb,),
        in_specs=[pl.BlockSpec((Bb, T, PA_W), seq3), pl.BlockSpec((Bb, CONV_A - 1, D_A), seq3),
                  pl.BlockSpec((Bb, 1, D_A), seq3), pl.BlockSpec((CONV_A, D_A), const2),
                  pl.BlockSpec((1, D_A), const2), pl.BlockSpec((D_A, 2 * D_A), const2),
                  pl.BlockSpec((1, 2 * D_A), const2), pl.BlockSpec((1, D_A), const2)],
        out_specs=[pl.BlockSpec((Bb, T, D_A), seq3), pl.BlockSpec((Bb, CONV_A - 1, D_A), seq3),
                   pl.BlockSpec((Bb, 1, D_A), seq3)],
        out_shape=[_sds((B, T, D_A)), _sds((B, CONV_A - 1, D_A)), _sds((B, 1, D_A))],
        scratch_shapes=[pltpu.VMEM((T + 2 * SUBLANES, D_A), F32), pltpu.VMEM((Tc, D_A), F32),
                        pltpu.VMEM((T, D_A), F32)],
        compiler_params=_cparams("parallel"), name="rglru",
    )(pa3, cbuf, h0, cw, cb, wg, bg, lam)


B_HALO = 32


def _convb_kernel(p_ref, buf_ref, w_ref, cb_ref, g_ref, b_ref, y_ref, bnew_ref, u_scr, *, T, Tc, Tv, Bb):
    w = w_ref[...]
    cb = cb_ref[...]
    g = g_ref[...]
    bb = b_ref[...]
    first = B_HALO - (CONV_B - 1)
    for b in range(Bb):
        u_scr[0:first, :] = jnp.zeros((first, D_B), F32)
        u_scr[first:B_HALO, :] = buf_ref[b]
        u_scr[B_HALO:B_HALO + T, :] = p_ref[b, :, 0:D_B] * _sigmoid(p_ref[b, :, D_B:2 * D_B])
        bnew_ref[b] = u_scr[first + Tv:B_HALO + Tv, :]

        def chunk(ci, carry):
            r0 = pl.multiple_of(ci * Tc, SUBLANES)
            win = u_scr[pl.ds(r0, Tc + B_HALO), :]
            acc = cb + w[0:1] * win[first:first + Tc]
            for j in range(1, CONV_B):
                acc = acc + w[j:j + 1] * win[first + j:first + j + Tc]
            yn = _ln(acc, g, bb)
            y_ref[b, pl.ds(r0, Tc), :] = yn * _sigmoid(yn)
            return carry

        lax.fori_loop(0, T // Tc, chunk, 0)


def _convb(pb3, buf, w, cb, g, b, Tv, Bb):
    B, T, _ = pb3.shape
    Tc = _pick(T, (48, 32, 16, 8))
    const2 = lambda i: (0, 0)
    seq3 = lambda i: (i, 0, 0)
    return pl.pallas_call(
        functools.partial(_convb_kernel, T=T, Tc=Tc, Tv=Tv, Bb=Bb),
        grid=(B // Bb,),
        in_specs=[pl.BlockSpec((Bb, T, PB_W), seq3), pl.BlockSpec((Bb, CONV_B - 1, D_B), seq3),
                  pl.BlockSpec((CONV_B, D_B), const2), pl.BlockSpec((1, D_B), const2),
                  pl.BlockSpec((1, D_B), const2), pl.BlockSpec((1, D_B), const2)],
        out_specs=[pl.BlockSpec((Bb, T, D_B), seq3), pl.BlockSpec((Bb, CONV_B - 1, D_B), seq3)],
        out_shape=[_sds((B, T, D_B)), _sds((B, CONV_B - 1, D_B))],
        scratch_shapes=[pltpu.VMEM((T + B_HALO, D_B), F32)],
        compiler_params=_cparams("parallel"), name="convb",
    )(pb3, buf, w, cb, g, b)


def _gla_kernel(p_ref, s0_ref, wg2_ref, bgate_ref, ng_ref, y_ref, snew_ref, s_scr, *, T, C, Tv, Bb):
    ri = lax.broadcasted_iota(jnp.int32, (C, C), 0)
    ci_ = lax.broadcasted_iota(jnp.int32, (C, C), 1)
    tril = ri >= ci_
    ltri = jnp.where(tril, 1.0, 0.0).astype(BF16)
    lane_k = lax.broadcasted_iota(jnp.int32, (1, D_QK_C), 1)
    lane_v = lax.broadcasted_iota(jnp.int32, (1, D_C), 1)
    hm_k = [(lane_k >= h * DK_C) & (lane_k < (h + 1) * DK_C) for h in range(N_HEADS_C)]
    hm_v = [(lane_v >= h * DV_C) & (lane_v < (h + 1) * DV_C) for h in range(N_HEADS_C)]
    rs = lax.broadcasted_iota(jnp.int32, (D_QK_C, D_C), 0)
    cs = lax.broadcasted_iota(jnp.int32, (D_QK_C, D_C), 1)
    bd = (rs >= 0) & (rs < 0)
    for h in range(N_HEADS_C):
        bd = bd | ((rs >= h * DK_C) & (rs < (h + 1) * DK_C) & (cs >= h * DV_C) & (cs < (h + 1) * DV_C))
    rm = lax.broadcasted_iota(jnp.int32, (D_C, D_C), 0)
    cm = lax.broadcasted_iota(jnp.int32, (D_C, D_C), 1)
    seg = (rm >= 0) & (rm < 0)
    for h in range(N_HEADS_C):
        seg = seg | ((rm >= h * DV_C) & (rm < (h + 1) * DV_C) & (cm >= h * DV_C) & (cm < (h + 1) * DV_C))
    mseg = jnp.where(seg, 1.0, 0.0).astype(BF16)
    ones_cl = jnp.ones((C, 128), BF16)
    wg2 = wg2_ref[...].astype(BF16)
    bgate = bgate_ref[...]
    ng = ng_ref[...]
    rowi = lax.broadcasted_iota(jnp.int32, (C, 1), 0)
    tdims = (((0,), (0,)), ((), ()))

    for b in range(Bb):
        s_scr[...] = jnp.zeros((D_QK_C, D_C), F32)
        for h in range(N_HEADS_C):
            s_scr[h * DK_C:(h + 1) * DK_C, h * DV_C:(h + 1) * DV_C] = s0_ref[b, h]

        def chunk(ci, carry):
            r0 = pl.multiple_of(ci * C, SUBLANES)
            q = p_ref[b, pl.ds(r0, C), PC_Q:PC_Q + D_QK_C] * (DK_C ** -0.5)
            z = p_ref[b, pl.ds(r0, C), PC_Z:PC_Z + GATE_RANK]
            k = p_ref[b, pl.ds(r0, C), PC_K:PC_K + D_QK_C]
            v = p_ref[b, pl.ds(r0, C), PC_V:PC_V + D_C]
            rg = p_ref[b, pl.ds(r0, C), PC_R:PC_R + D_C]
            pre = _dot(z.astype(BF16), wg2) + bgate
            g = (jnp.minimum(pre, 0.0) - jnp.log1p(jnp.exp(-jnp.abs(pre)))) * (1.0 / GATE_TAU)
            valid = (r0 + rowi) < Tv
            g = jnp.where(valid, g, 0.0)
            k = jnp.where(valid, k, 0.0)
            g_hi, g_lo = _split_bf16(g)
            gcum = _dot(ltri, g_hi) + _dot(ltri, g_lo)
            g_last = gcum[C - 1:C, :]
            g_mid = gcum[C // 2 - 1:C // 2, :]
            s_full = s_scr[...]
            vb = v.astype(BF16)
            o = _dot((q * jnp.exp(gcum)).astype(BF16), s_full.astype(BF16))
            qt = q * jnp.exp(gcum - g_mid)
            ktb = (k * jnp.exp(g_mid - gcum)).astype(BF16)
            for h in range(N_HEADS_C):
                qh = jnp.where(hm_k[h], qt, 0.0).astype(BF16)
                sc = lax.dot_general(qh, ktb, (((1,), (1,)), ((), ())), preferred_element_type=F32)
                sc = jnp.where(tril, sc, 0.0)
                oh = _dot(sc.astype(BF16), vb)
                o = o + jnp.where(hm_v[h], oh, 0.0)
            kd = (k * jnp.exp(g_last - gcum)).astype(BF16)
            upd = lax.dot_general(kd, vb, tdims, preferred_element_type=F32)
            gsum = (lax.dot_general(g_hi, ones_cl, tdims, preferred_element_type=F32)
                    + lax.dot_general(g_lo, ones_cl, tdims, preferred_element_type=F32))
            dec = jnp.exp(gsum)
            dec = jnp.concatenate([dec, dec, dec], axis=1)
            s_scr[...] = dec * s_full + jnp.where(bd, upd, 0.0)
            o2_hi, o2_lo = _split_bf16(o * o)
            ms = (_dot(o2_hi, mseg) + _dot(o2_lo, mseg)) * (1.0 / DV_C)
            y = o * lax.rsqrt(ms + LN_EPS) * ng * (rg * _sigmoid(rg))
            y_ref[b, pl.ds(r0, C), :] = y
            return carry

        lax.fori_loop(0, T // C, chunk, 0)
        for h in range(N_HEADS_C):
            snew_ref[b, h] = s_scr[h * DK_C:(h + 1) * DK_C, h * DV_C:(h + 1) * DV_C]


def _gla(pc3, s0, wg2, bgate, ng, Tv, Bb):
    B, T, _ = pc3.shape
    C = _pick(T, (48, 32, 16, 8))
    const2 = lambda i: (0, 0)
    seq3 = lambda i: (i, 0, 0)
    seq4 = lambda i: (i, 0, 0, 0)
    st = (Bb, N_HEADS_C, DK_C, DV_C)
    return pl.pallas_call(
        functools.partial(_gla_kernel, T=T, C=C, Tv=Tv, Bb=Bb),
        grid=(B // Bb,),
        in_specs=[pl.BlockSpec((Bb, T, PC_W), seq3), pl.BlockSpec(st, seq4),
                  pl.BlockSpec((GATE_RANK, D_QK_C), const2), pl.BlockSpec((1, D_QK_C), const2),
                  pl.BlockSpec((1, D_C), const2)],
        out_specs=[pl.BlockSpec((Bb, T, D_C), seq3), pl.BlockSpec(st, seq4)],
        out_shape=[_sds((B, T, D_C)), _sds((B, N_HEADS_C, DK_C, DV_C))],
        scratch_shapes=[pltpu.VMEM((D_QK_C, D_C), F32)],
        compiler_params=_cparams("parallel"), name="gla",
    )(pc3, s0, wg2, bgate, ng)


def _outproj_kernel(ya_ref, yb_ref, yc_ref, x_ref, w_ref, g_ref, b_ref, h_ref, *, alpha):
    y = (_dot(ya_ref[...].astype(BF16), w_ref[0:D_A, :])
         + _dot(yb_ref[...].astype(BF16), w_ref[D_A:D_A + D_B, :])
         + _dot(yc_ref[...].astype(BF16), w_ref[D_A + D_B:D_MODEL, :]))
    h_ref[...] = _ln(alpha * x_ref[...] + y, g_ref[...], b_ref[...])


def _outproj(ya, yb, yc, x, row0, w, g, b, alpha):
    n = ya.shape[0]
    tm = _pick(n, (384, 512, 256, 128, 64, 32, 16, 8))
    while row0 % tm:
        tm //= 2
    off = row0 // tm
    row = lambda i: (i, 0)
    const = lambda i: (0, 0)
    return pl.pallas_call(
        functools.partial(_outproj_kernel, alpha=alpha),
        grid=(n // tm,),
        in_specs=[pl.BlockSpec((tm, D_A), row), pl.BlockSpec((tm, D_B), row), pl.BlockSpec((tm, D_C), row),
                  pl.BlockSpec((tm, D_MODEL), lambda i: (i + off, 0)),
                  pl.BlockSpec((D_MODEL, D_MODEL), const), pl.BlockSpec((1, D_MODEL), const),
                  pl.BlockSpec((1, D_MODEL), const)],
        out_specs=pl.BlockSpec((tm, D_MODEL), row), out_shape=_sds((n, D_MODEL)),
        compiler_params=_cparams("parallel"), name="outproj",
    )(ya, yb, yc, x, w, g, b)


def _router_kernel(h_ref, wt_ref, b_ref, idx_ref, gate_ref):
    nt = (((1,), (1,)), ((), ()))
    hh, hl = _split_bf16(h_ref[...])
    wh, wl = _split_bf16(wt_ref[...])
    logits = (lax.dot_general(wh, hh, nt, preferred_element_type=F32)
              + lax.dot_general(wh, hl, nt, preferred_element_type=F32)
              + lax.dot_general(wl, hh, nt, preferred_element_type=F32)) + b_ref[...]
    eid = lax.broadcasted_iota(jnp.int32, logits.shape, 0)
    vals = []
    for k in range(TOP_K):
        m = jnp.max(logits, axis=0, keepdims=True)
        sel = jnp.min(jnp.where(logits == m, eid, N_EXPERTS), axis=0, keepdims=True)
        idx_ref[k:k + 1, :] = sel
        vals.append(m)
        logits = jnp.where(eid == sel, -jnp.inf, logits)
    es = [jnp.exp(v - vals[0]) for v in vals]
    tot = es[0] + es[1] + es[2] + es[3]
    for k in range(TOP_K):
        gate_ref[k:k + 1, :] = es[k] / tot


def _router(h, wt, b):
    n = h.shape[0]
    tm = _pick(n, (896, 640, 512, 384, 256, 128))
    return pl.pallas_call(
        _router_kernel,
        grid=(n // tm,),
        in_specs=[pl.BlockSpec((tm, D_MODEL), lambda i: (i, 0)),
                  pl.BlockSpec((N_EXPERTS, D_MODEL), lambda i: (0, 0)),
                  pl.BlockSpec((N_EXPERTS, 1), lambda i: (0, 0))],
        out_specs=[pl.BlockSpec((TOP_K, tm), lambda i: (0, i)), pl.BlockSpec((TOP_K, tm), lambda i: (0, i))],
        out_shape=[_sds((TOP_K, n), jnp.int32), _sds((TOP_K, n))],
        compiler_params=_cparams("parallel"), name="router",
    )(h, wt, b)


def _moe_kernel(be_ref, nu_ref, tok_ref, h_hbm, w1_ref, b1_ref, w2_ref, b2_ref, out_ref, xbuf, sem, *, bm):
    i = pl.program_id(0)
    n_used = nu_ref[0]

    def issue(blk, slot):
        def body(j, c):
            t = tok_ref[blk * bm + j]
            pltpu.make_async_copy(h_hbm.at[pl.ds(t, 1), :], xbuf.at[slot, pl.ds(j, 1), :], sem.at[slot]).start()
            return c
        lax.fori_loop(0, bm, body, 0)

    @pl.when(jnp.logical_and(i == 0, n_used > 0))
    def _():
        issue(0, 0)

    @pl.when(i + 1 < n_used)
    def _():
        issue(i + 1, (i + 1) % 2)

    @pl.when(i < n_used)
    def _():
        slot = i % 2
        pltpu.make_async_copy(h_hbm.at[pl.ds(0, bm), :], xbuf.at[slot], sem.at[slot]).wait()
        x = xbuf[slot].astype(BF16)
        acc = jnp.zeros((bm, D_MODEL), F32)
        for c in range(D_FF // FF_CHUNK):
            lo = c * FF_CHUNK
            hg = _dot(x, w1_ref[:, lo:lo + FF_CHUNK]) + b1_ref[:, lo:lo + FF_CHUNK]
            hl = _dot(x, w1_ref[:, D_FF + lo:D_FF + lo + FF_CHUNK]) + b1_ref[:, D_FF + lo:D_FF + lo + FF_CHUNK]
            gate = jnp.minimum(hg, SWIGLU_LIMIT)
            lin = jnp.clip(hl, -SWIGLU_LIMIT, SWIGLU_LIMIT)
            act = gate * _sigmoid(SWIGLU_ALPHA * gate) * (lin + 1.0)
            acc = acc + _dot(act.astype(BF16), w2_ref[lo:lo + FF_CHUNK, :])
        out_ref[...] = acc + b2_ref[...]

    @pl.when(i >= n_used)
    def _():
        out_ref[...] = jnp.zeros((bm, D_MODEL), F32)


def _moe(block_e, n_used, tok_sorted, h, w1, b1, w2, b2, bm):
    nblk = block_e.shape[0]
    wmap = lambda i, be, nu, tok: (be[i], 0, 0)
    return pl.pallas_call(
        functools.partial(_moe_kernel, bm=bm),
        grid_spec=pltpu.PrefetchScalarGridSpec(
            num_scalar_prefetch=3, grid=(nblk,),
            in_specs=[pl.BlockSpec(memory_space=pl.ANY),
                      pl.BlockSpec((None, D_MODEL, 2 * D_FF), wmap), pl.BlockSpec((None, 1, 2 * D_FF), wmap),
                      pl.BlockSpec((None, D_FF, D_MODEL), wmap), pl.BlockSpec((None, 1, D_MODEL), wmap)],
            out_specs=pl.BlockSpec((bm, D_MODEL), lambda i, be, nu, tok: (i, 0)),
            scratch_shapes=[pltpu.VMEM((2, bm, D_MODEL), F32), pltpu.SemaphoreType.DMA((2,))]),
        out_shape=_sds((nblk * bm, D_MODEL)),
        compiler_params=_cparams("arbitrary"), name="moe_experts",
    )(block_e, n_used, tok_sorted, h, w1, b1, w2, b2)


def _combine_kernel(dest_ref, gates_ref, h_ref, o_hbm, g_ref, b_ref, out_ref, gbuf, sem, *, tm, n, alpha):
    i = pl.program_id(0)
    nt = pl.num_programs(0)

    def issue(tile, slot):
        for k in range(TOP_K):
            def body(j, c):
                d = dest_ref[k * n + tile * tm + j]
                pltpu.make_async_copy(o_hbm.at[pl.ds(d, 1), :], gbuf.at[slot, k, pl.ds(j, 1), :],
                                      sem.at[slot]).start()
                return c
            lax.fori_loop(0, tm, body, 0)

    @pl.when(i == 0)
    def _():
        issue(0, 0)

    @pl.when(i + 1 < nt)
    def _():
        issue(i + 1, (i + 1) % 2)

    slot = i % 2
    for k in range(TOP_K):
        pltpu.make_async_copy(o_hbm.at[pl.ds(0, tm), :], gbuf.at[slot, k], sem.at[slot]).wait()
    gates = gates_ref[...]
    y = gbuf[slot, 0] * gates[:, 0:1]
    for k in range(1, TOP_K):
        y = y + gbuf[slot, k] * gates[:, k:k + 1]
    out_ref[...] = _ln(alpha * h_ref[...] + y, g_ref[...], b_ref[...])


def _combine(dest, gates_t, h, out_sorted, g, b, alpha):
    n = h.shape[0]
    tm = _pick(n, (224, 128, 64, 32, 16, 8))
    row = lambda i, d: (i, 0)
    const = lambda i, d: (0, 0)
    return pl.pallas_call(
        functools.partial(_combine_kernel, tm=tm, n=n, alpha=alpha),
        grid_spec=pltpu.PrefetchScalarGridSpec(
            num_scalar_prefetch=1, grid=(n // tm,),
            in_specs=[pl.BlockSpec((tm, TOP_K), row), pl.BlockSpec((tm, D_MODEL), row),
                      pl.BlockSpec(memory_space=pl.ANY),
                      pl.BlockSpec((1, D_MODEL), const), pl.BlockSpec((1, D_MODEL), const)],
            out_specs=pl.BlockSpec((tm, D_MODEL), row),
            scratch_shapes=[pltpu.VMEM((2, TOP_K, tm, D_MODEL), F32), pltpu.SemaphoreType.DMA((2,))]),
        out_shape=_sds((n, D_MODEL)),
        compiler_params=_cparams("arbitrary"), name="moe_combine",
    )(dest, gates_t, h, out_sorted, g, b)


def _route_metadata(top_idx, bm):
    k, n = top_idx.shape
    m = k * n
    e_flat = top_idx.reshape(-1)
    onehot = (e_flat[:, None] == jnp.arange(N_EXPERTS, dtype=jnp.int32)[None, :]).astype(jnp.int32)
    csum = jnp.cumsum(onehot, axis=0)
    rank = jnp.sum(onehot * csum, axis=1) - 1
    counts = csum[-1]
    padded = (counts + bm - 1) // bm * bm
    pad_end = jnp.cumsum(padded)
    pad_start = pad_end - padded
    dest = (pad_start[e_flat] + rank).astype(jnp.int32)
    nblk = -(-m // bm) + N_EXPERTS
    tok = jnp.tile(jnp.arange(n, dtype=jnp.int32), k)
    tok_sorted = jnp.zeros((nblk * bm,), jnp.int32).at[dest].set(tok)
    block_e = jnp.minimum(jnp.searchsorted(pad_end, jnp.arange(nblk, dtype=jnp.int32) * bm, side="right"),
                          N_EXPERTS - 1).astype(jnp.int32)
    n_used = (pad_end[-1] // bm).astype(jnp.int32).reshape(1)
    return dest, tok_sorted, block_e, n_used


def _pack_w_in(w):
    o = 0
    xa, o = w[:, o:o + D_A], o + D_A
    ga, o = w[:, o:o + D_A], o + D_A
    vb, o = w[:, o:o + D_B], o + D_B
    gb, o = w[:, o:o + D_B], o + D_B
    q, o = w[:, o:o + D_QK_C], o + D_QK_C
    k, o = w[:, o:o + D_QK_C], o + D_QK_C
    v, o = w[:, o:o + D_C], o + D_C
    r, o = w[:, o:o + D_C], o + D_C
    z = w[:, o:o + GATE_RANK]
    zq = jnp.zeros((w.shape[0], PC_K - PC_Z - GATE_RANK), w.dtype)
    zk = jnp.zeros((w.shape[0], PC_V - PC_K - D_QK_C), w.dtype)
    return jnp.concatenate([xa, ga, vb, gb, q, z, zq, k, zk, v, r], axis=1).astype(BF16)


def _block_diag(w):
    eye = jnp.eye(N_BLK_A, dtype=w.dtype)
    return jnp.einsum("hij,hg->higj", w, eye).reshape(D_A, D_A)


def kernel(x_prompt, x_sample, state_conv_a, state_rglru, state_conv_b, state_gla, meta_tokens, ln0_g, ln0_b,
           w_in, conv_a_w, conv_a_b, w_rg, b_rg, w_ig, b_ig, lru_lambda, conv_b_w, conv_b_b, ln_b_g, ln_b_b,
           w_gate2, b_gate, gla_norm_g, w_out, ln1_g, ln1_b, router_w, router_b, moe_w1, moe_b1, moe_w2, moe_b2,
           ln2_g, ln2_b):
    bp, seq, _ = x_prompt.shape
    bs, dseq, _ = x_sample.shape
    depth = w_in.shape[0]
    tp = N_META + seq
    ts = SAMPLE_PAD_T
    np_rows = bp * tp
    ns_rows = bs * dseq
    n = np_rows + ns_rows
    alpha = (2 * depth) ** 0.25
    row = lambda a: a.reshape(1, -1)

    meta = jnp.broadcast_to(meta_tokens[None], (bp, N_META, D_MODEL))
    xp_in = jnp.concatenate([meta, x_prompt], axis=1).reshape(np_rows, D_MODEL)
    xs_in = jnp.pad(x_sample, ((0, 0), (0, ts - dseq), (0, 0))).reshape(bs * ts, D_MODEL)
    zeros_p = (jnp.zeros((bp, CONV_A - 1, D_A), F32), jnp.zeros((bp, 1, D_A), F32),
               jnp.zeros((bp, CONV_B - 1, D_B), F32), jnp.zeros((bp, N_HEADS_C, DK_C, DV_C), F32))
    sb = _pick(bs, (8, 4, 2, 1))

    new_p = ([], [], [], [])
    new_s = ([], [], [], [])
    x_all = None
    for l in range(depth):
        w_packed = _pack_w_in(w_in[l])
        wg = jnp.concatenate([_block_diag(w_rg[l]), _block_diag(w_ig[l])], axis=1).astype(BF16)
        bg = jnp.concatenate([b_rg[l], b_ig[l]]).reshape(1, -1)
        w_out_b = w_out[l].astype(BF16)
        ng = row(jnp.tile(gla_norm_g[l], N_HEADS_C))
        first = l == 0
        if first:
            pa_p, pb_p, pc_p, xn_p = _inproj(xp_in, 0, np_rows, row(ln0_g), row(ln0_b), w_packed, True)
            pa_s, pb_s, pc_s, xn_s = _inproj(xs_in, 0, bs * ts, row(ln0_g), row(ln0_b), w_packed, True)
            res_p, res_p_row0 = xn_p, 0
        else:
            pa_p, pb_p, pc_p = _inproj(x_all, 0, np_rows, row(ln0_g), row(ln0_b), w_packed, False)
            xn_s = jnp.pad(x_all[np_rows:].reshape(bs, dseq, D_MODEL),
                           ((0, 0), (0, ts - dseq), (0, 0))).reshape(bs * ts, D_MODEL)
            pa_s, pb_s, pc_s = _inproj(xn_s, 0, bs * ts, row(ln0_g), row(ln0_b), w_packed, False)
            res_p, res_p_row0 = x_all, 0

        outs = []
        for (pa, pb, pc, nb, t, tv, bb, st) in (
                (pa_p, pb_p, pc_p, bp, tp, tp, 1, zeros_p),
                (pa_s, pb_s, pc_s, bs, ts, dseq, sb,
                 (state_conv_a[l], state_rglru[l].reshape(bs, 1, D_A), state_conv_b[l], state_gla[l]))):
            ya, ca_new, h_last = _rglru(pa.reshape(nb, t, PA_W), st[0], st[1], conv_a_w[l], row(conv_a_b[l]),
                                        wg, bg, row(lru_lambda[l]), tv, bb)
            yb, cb_new = _convb(pb.reshape(nb, t, PB_W), st[2], conv_b_w[l], row(conv_b_b[l]),
                                row(ln_b_g[l]), row(ln_b_b[l]), tv, bb)
            yc, s_new = _gla(pc.reshape(nb, t, PC_W), st[3], w_gate2[l], row(b_gate[l]), ng, tv, bb)
            outs.append((ya.reshape(nb * t, D_A), yb.reshape(nb * t, D_B), yc.reshape(nb * t, D_C),
                         ca_new, h_last.reshape(nb, D_A), cb_new, s_new))
        (ya_p, yb_p, yc_p, *st_p), (ya_s, yb_s, yc_s, *st_s) = outs
        for j in range(4):
            new_p[j].append(st_p[j])
            new_s[j].append(st_s[j])

        h_p = _outproj(ya_p, yb_p, yc_p, res_p, res_p_row0, w_out_b, row(ln1_g[l]), row(ln1_b[l]), alpha)
        h_s = _outproj(ya_s, yb_s, yc_s, xn_s, 0, w_out_b, row(ln1_g[l]), row(ln1_b[l]), alpha)
        h_s = h_s.reshape(bs, ts, D_MODEL)[:, :dseq].reshape(ns_rows, D_MODEL)
        h_all = jnp.concatenate([h_p, h_s], axis=0)

        top_idx, gates = _router(h_all, router_w[l].T, router_b[l].reshape(-1, 1))
        dest, tok_sorted, block_e, n_used = _route_metadata(top_idx, MOE_BM)
        out_sorted = _moe(block_e, n_used, tok_sorted, h_all, moe_w1[l].astype(BF16),
                          moe_b1[l].reshape(N_EXPERTS, 1, -1), moe_w2[l].astype(BF16),
                          moe_b2[l].reshape(N_EXPERTS, 1, -1), MOE_BM)
        x_all = _combine(dest, gates.T, h_all, out_sorted, row(ln2_g[l]), row(ln2_b[l]), alpha)

    y_p = x_all[:np_rows].reshape(bp, tp, D_MODEL)[:, N_META:]
    y_s = x_all[np_rows:].reshape(bs, dseq, D_MODEL)
    return (y_p, y_s,
            jnp.stack(new_p[0]), jnp.stack(new_p[1]), jnp.stack(new_p[2]), jnp.stack(new_p[3]),
            jnp.stack(new_s[0]), jnp.stack(new_s[1]), jnp.stack(new_s[2]), jnp.stack(new_s[3]))
```

```python
import functools

import jax
import jax.numpy as jnp
from jax import lax
from jax.experimental import pallas as pl
from jax.experimental.pallas import tpu as pltpu

F32 = jnp.float32
BF16 = jnp.bfloat16

D_MODEL = 1024
N_META = 16
D_A = 384
D_B = 256
D_C = 384
N_BLK_A = 8
BLK_A = D_A // N_BLK_A
CONV_A = 4
RG_C = 8.0
CONV_B = 31
N_HEADS_C = 4
DV_C = D_C // N_HEADS_C
DK_C = DV_C // 2
D_QK_C = N_HEADS_C * DK_C
GATE_RANK = 16
GATE_TAU = 16.0
N_EXPERTS = 32
TOP_K = 4
D_FF = D_MODEL
SWIGLU_LIMIT = 7.0
SWIGLU_ALPHA = 1.702
LN_EPS = 1e-5

PA_W = 2 * D_A
PB_W = 2 * D_B
PC_Q, PC_Z, PC_K, PC_V, PC_R, PC_W = 0, 192, 256, 512, 896, 1280
P_W = PA_W + PB_W + PC_W

SUBLANES = 8
LANES = 128
VMEM_LIMIT_BYTES = 56 * 1024 * 1024
MOE_BM = 256
FF_CHUNK = 256
DMA_GROUP = 8
SAMPLE_PAD_T = 8


def _cparams(*sem):
    return pltpu.CompilerParams(dimension_semantics=sem, vmem_limit_bytes=VMEM_LIMIT_BYTES)


def _sds(shape, dtype=F32):
    return jax.ShapeDtypeStruct(shape, dtype)


def _pick(n, prefs):
    for p in prefs:
        if n % p == 0:
            return p
    raise ValueError(f"no tile for {n} in {prefs}")


def _ln(x, g, b):
    mu = jnp.mean(x, axis=-1, keepdims=True)
    xc = x - mu
    var = jnp.mean(xc * xc, axis=-1, keepdims=True)
    return xc * lax.rsqrt(var + LN_EPS) * g + b


def _sigmoid(x):
    return 1.0 / (1.0 + jnp.exp(-x))


def _split_bf16(x):
    hi = x.astype(BF16)
    lo = (x - hi.astype(F32)).astype(BF16)
    return hi, lo


def _dot(a, b):
    return jnp.dot(a, b, preferred_element_type=F32)


def _inproj_kernel(x_ref, g_ref, b_ref, w_ref, pa_ref, pb_ref, pc_ref, *maybe_xn, apply_ln):
    x = x_ref[...]
    if apply_ln:
        x = _ln(x, g_ref[...], b_ref[...])
        maybe_xn[0][...] = x
    xb = x.astype(BF16)
    pa_ref[...] = _dot(xb, w_ref[:, 0:PA_W])
    pb_ref[...] = _dot(xb, w_ref[:, PA_W:PA_W + PB_W])
    pc_ref[...] = _dot(xb, w_ref[:, PA_W + PB_W:P_W])


def _inproj(x, row0, nrows, ln_g, ln_b, w_packed, apply_ln):
    tm = _pick(nrows, (384, 512, 256, 128, 64, 32, 16, 8))
    while row0 % tm:
        tm //= 2
    off = row0 // tm
    const = lambda i: (0, 0)
    row = lambda i: (i, 0)
    out_shape = [_sds((nrows, PA_W)), _sds((nrows, PB_W)), _sds((nrows, PC_W))]
    out_specs = [pl.BlockSpec((tm, PA_W), row), pl.BlockSpec((tm, PB_W), row), pl.BlockSpec((tm, PC_W), row)]
    if apply_ln:
        out_shape.append(_sds((nrows, D_MODEL)))
        out_specs.append(pl.BlockSpec((tm, D_MODEL), row))
    return pl.pallas_call(
        functools.partial(_inproj_kernel, apply_ln=apply_ln),
        grid=(nrows // tm,),
        in_specs=[pl.BlockSpec((tm, D_MODEL), lambda i: (i + off, 0)),
                  pl.BlockSpec((1, D_MODEL), const), pl.BlockSpec((1, D_MODEL), const),
                  pl.BlockSpec((D_MODEL, P_W), const)],
        out_specs=out_specs, out_shape=out_shape,
        compiler_params=_cparams("parallel"), name="inproj",
    )(x, ln_g, ln_b, w_packed)


def _rglru_kernel(p_ref, cbuf_ref, h0_ref, cw_ref, cb_ref, wg_ref, bg_ref, lam_ref,
                  y_ref, cnew_ref, hlast_ref, xp_scr, a_scr, h_scr, *, T, Tc, Tv, Bb):
    lam = lam_ref[...]
    softplus_neg = jnp.maximum(-lam, 0.0) + jnp.log1p(jnp.exp(-jnp.abs(lam)))
    c_decay = -RG_C * softplus_neg
    cw = cw_ref[...]
    cb = cb_ref[...]
    bg = bg_ref[...]
    sub = lax.broadcasted_iota(jnp.int32, (Tc, D_A), 0) % SUBLANES
    halo = SUBLANES - (CONV_A - 1)
    for b in range(Bb):
        xp_scr[halo:SUBLANES, :] = cbuf_ref[b]
        xp_scr[SUBLANES:SUBLANES + T, :] = p_ref[b, :, 0:D_A]
        cnew_ref[b] = xp_scr[halo + Tv:SUBLANES + Tv, :]

        def chunk(ci, h_b):
            r0 = pl.multiple_of(ci * Tc, SUBLANES)
            win = xp_scr[pl.ds(r0, Tc + SUBLANES), :]
            xc = cb + cw[0:1] * win[halo:halo + Tc]
            for j in range(1, CONV_A):
                xc = xc + cw[j:j + 1] * win[halo + j:halo + j + Tc]
            gates = _dot(xc.astype(BF16), wg_ref[...]) + bg
            r = _sigmoid(gates[:, 0:D_A])
            i = _sigmoid(gates[:, D_A:2 * D_A])
            log_a = c_decay * r
            a = jnp.exp(log_a)
            u = jnp.sqrt(1.0 - a * a) * (i * xc)
            for s in (1, 2, 4):
                keep = sub >= s
                a_prev = pltpu.roll(a, s, 0)
                u_prev = pltpu.roll(u, s, 0)
                u = jnp.where(keep, a * u_prev + u, u)
                a = jnp.where(keep, a * a_prev, a)
            a_scr[...] = a
            h_scr[pl.ds(r0, Tc), :] = u

            def group(gi, h_b):
                c0 = pl.multiple_of(gi * SUBLANES, SUBLANES)
                g0 = pl.multiple_of(r0 + gi * SUBLANES, SUBLANES)
                h8 = a_scr[pl.ds(c0, SUBLANES), :] * h_b + h_scr[pl.ds(g0, SUBLANES), :]
                h_scr[pl.ds(g0, SUBLANES), :] = h8
                return jnp.broadcast_to(h8[SUBLANES - 1:SUBLANES, :], (SUBLANES, D_A))

            h_b = lax.fori_loop(0, Tc // SUBLANES, group, h_b)
            ga = p_ref[b, pl.ds(r0, Tc), D_A:2 * D_A]
            gelu = 0.5 * ga * (1.0 + jnp.tanh(0.7978845608028654 * (ga + 0.044715 * ga * ga * ga)))
            y_ref[b, pl.ds(r0, Tc), :] = h_scr[pl.ds(r0, Tc), :] * gelu
            return h_b

        h_b = jnp.broadcast_to(h0_ref[b], (SUBLANES, D_A))
        lax.fori_loop(0, T // Tc, chunk, h_b)
        hlast_ref[b] = h_scr[Tv - 1:Tv, :]


def _rglru(pa3, cbuf, h0, cw, cb, wg, bg, lam, Tv, Bb):
    B, T, _ = pa3.shape
    Tc = _pick(T, (344, 256, 128, 64, 48, 32, 16, 8))
    const2 = lambda i: (0, 0)
    seq3 = lambda i: (i, 0, 0)
    return pl.pallas_call(
        functools.partial(_rglru_kernel, T=T, Tc=Tc, Tv=Tv, Bb=Bb),
        grid=(B // Bb,),
        in_specs=[pl.BlockSpec((Bb, T, PA_W), seq3), pl.BlockSpec((Bb, CONV_A - 1, D_A), seq3),
                  pl.BlockSpec((Bb, 1, D_A), seq3), pl.BlockSpec((CONV_A, D_A), const2),
                  pl.BlockSpec((1, D_A), const2), pl.BlockSpec((D_A, 2 * D_A), const2),
                  pl.BlockSpec((1, 2 * D_A), const2), pl.BlockSpec((1, D_A), const2)],
        out_specs=[pl.BlockSpec((Bb, T, D_A), seq3), pl.BlockSpec((Bb, CONV_A - 1, D_A), seq3),
                   pl.BlockSpec((Bb, 1, D_A), seq3)],
        out_shape=[_sds((B, T, D_A)), _sds((B, CONV_A - 1, D_A)), _sds((B, 1, D_A))],
        scratch_shapes=[pltpu.VMEM((T + 2 * SUBLANES, D_A), F32), pltpu.VMEM((Tc, D_A), F32),
                        pltpu.VMEM((T, D_A), F32)],
        compiler_params=_cparams("parallel"), name="rglru",
    )(pa3, cbuf, h0, cw, cb, wg, bg, lam)


B_HALO = 32


def _convb_kernel(p_ref, buf_ref, w_ref, cb_ref, g_ref, b_ref, y_ref, bnew_ref, u_scr, *, T, Tc, Tv, Bb):
    w = w_ref[...]
    cb = cb_ref[...]
    g = g_ref[...]
    bb = b_ref[...]
    first = B_HALO - (CONV_B - 1)
    for b in range(Bb):
        u_scr[0:first, :] = jnp.zeros((first, D_B), F32)
        u_scr[first:B_HALO, :] = buf_ref[b]
        u_scr[B_HALO:B_HALO + T, :] = p_ref[b, :, 0:D_B] * _sigmoid(p_ref[b, :, D_B:2 * D_B])
        bnew_ref[b] = u_scr[first + Tv:B_HALO + Tv, :]

        def chunk(ci, carry):
            r0 = pl.multiple_of(ci * Tc, SUBLANES)
            win = u_scr[pl.ds(r0, Tc + B_HALO), :]
            acc = cb + w[0:1] * win[first:first + Tc]
            for j in range(1, CONV_B):
                acc = acc + w[j:j + 1] * win[first + j:first + j + Tc]
            yn = _ln(acc, g, bb)
            y_ref[b, pl.ds(r0, Tc), :] = yn * _sigmoid(yn)
            return carry

        lax.fori_loop(0, T // Tc, chunk, 0)


def _convb(pb3, buf, w, cb, g, b, Tv, Bb):
    B, T, _ = pb3.shape
    Tc = _pick(T, (48, 32, 16, 8))
    const2 = lambda i: (0, 0)
    seq3 = lambda i: (i, 0, 0)
    return pl.pallas_call(
        functools.partial(_convb_kernel, T=T, Tc=Tc, Tv=Tv, Bb=Bb),
        grid=(B // Bb,),
        in_specs=[pl.BlockSpec((Bb, T, PB_W), seq3), pl.BlockSpec((Bb, CONV_B - 1, D_B), seq3),
                  pl.BlockSpec((CONV_B, D_B), const2), pl.BlockSpec((1, D_B), const2),
                  pl.BlockSpec((1, D_B), const2), pl.BlockSpec((1, D_B), const2)],
        out_specs=[pl.BlockSpec((Bb, T, D_B), seq3), pl.BlockSpec((Bb, CONV_B - 1, D_B), seq3)],
        out_shape=[_sds((B, T, D_B)), _sds((B, CONV_B - 1, D_B))],
        scratch_shapes=[pltpu.VMEM((T + B_HALO, D_B), F32)],
        compiler_params=_cparams("parallel"), name="convb",
    )(pb3, buf, w, cb, g, b)


def _gla_kernel(p_ref, s0_ref, wg2_ref, bgate_ref, ng_ref, y_ref, snew_ref, s_scr, *, T, C, Tv, Bb):
    ri = lax.broadcasted_iota(jnp.int32, (C, C), 0)
    ci_ = lax.broadcasted_iota(jnp.int32, (C, C), 1)
    tril = ri >= ci_
    ltri = jnp.where(tril, 1.0, 0.0).astype(BF16)
    lane_k = lax.broadcasted_iota(jnp.int32, (1, D_QK_C), 1)
    lane_v = lax.broadcasted_iota(jnp.int32, (1, D_C), 1)
    hm_k = [(lane_k >= h * DK_C) & (lane_k < (h + 1) * DK_C) for h in range(N_HEADS_C)]
    hm_v = [(lane_v >= h * DV_C) & (lane_v < (h + 1) * DV_C) for h in range(N_HEADS_C)]
    rs = lax.broadcasted_iota(jnp.int32, (D_QK_C, D_C), 0)
    cs = lax.broadcasted_iota(jnp.int32, (D_QK_C, D_C), 1)
    bd = (rs >= 0) & (rs < 0)
    for h in range(N_HEADS_C):
        bd = bd | ((rs >= h * DK_C) & (rs < (h + 1) * DK_C) & (cs >= h * DV_C) & (cs < (h + 1) * DV_C))
    rm = lax.broadcasted_iota(jnp.int32, (D_C, D_C), 0)
    cm = lax.broadcasted_iota(jnp.int32, (D_C, D_C), 1)
    seg = (rm >= 0) & (rm < 0)
    for h in range(N_HEADS_C):
        seg = seg | ((rm >= h * DV_C) & (rm < (h + 1) * DV_C) & (cm >= h * DV_C) & (cm < (h + 1) * DV_C))
    mseg = jnp.where(seg, 1.0, 0.0).astype(BF16)
    ones_cl = jnp.ones((C, 128), BF16)
    wg2 = wg2_ref[...].astype(BF16)
    bgate = bgate_ref[...]
    ng = ng_ref[...]
    rowi = lax.broadcasted_iota(jnp.int32, (C, 1), 0)
    tdims = (((0,), (0,)), ((), ()))

    for b in range(Bb):
        s_scr[...] = jnp.zeros((D_QK_C, D_C), F32)
        for h in range(N_HEADS_C):
            s_scr[h * DK_C:(h + 1) * DK_C, h * DV_C:(h + 1) * DV_C] = s0_ref[b, h]

        def chunk(ci, carry):
            r0 = pl.multiple_of(ci * C, SUBLANES)
            q = p_ref[b, pl.ds(r0, C), PC_Q:PC_Q + D_QK_C] * (DK_C ** -0.5)
            z = p_ref[b, pl.ds(r0, C), PC_Z:PC_Z + GATE_RANK]
            k = p_ref[b, pl.ds(r0, C), PC_K:PC_K + D_QK_C]
            v = p_ref[b, pl.ds(r0, C), PC_V:PC_V + D_C]
            rg = p_ref[b, pl.ds(r0, C), PC_R:PC_R + D_C]
            pre = _dot(z.astype(BF16), wg2) + bgate
            g = (jnp.minimum(pre, 0.0) - jnp.log1p(jnp.exp(-jnp.abs(pre)))) * (1.0 / GATE_TAU)
            valid = (r0 + rowi) < Tv
            g = jnp.where(valid, g, 0.0)
            k = jnp.where(valid, k, 0.0)
            g_hi, g_lo = _split_bf16(g)
            gcum = _dot(ltri, g_hi) + _dot(ltri, g_lo)
            g_last = gcum[C - 1:C, :]
            g_mid = gcum[C // 2 - 1:C // 2, :]
            s_full = s_scr[...]
            vb = v.astype(BF16)
            o = _dot((q * jnp.exp(gcum)).astype(BF16), s_full.astype(BF16))
            qt = q * jnp.exp(gcum - g_mid)
            ktb = (k * jnp.exp(g_mid - gcum)).astype(BF16)
            for h in range(N_HEADS_C):
                qh = jnp.where(hm_k[h], qt, 0.0).astype(BF16)
                sc = lax.dot_general(qh, ktb, (((1,), (1,)), ((), ())), preferred_element_type=F32)
                sc = jnp.where(tril, sc, 0.0)
                oh = _dot(sc.astype(BF16), vb)
                o = o + jnp.where(hm_v[h], oh, 0.0)
            kd = (k * jnp.exp(g_last - gcum)).astype(BF16)
            upd = lax.dot_general(kd, vb, tdims, preferred_element_type=F32)
            gsum = (lax.dot_general(g_hi, ones_cl, tdims, preferred_element_type=F32)
                    + lax.dot_general(g_lo, ones_cl, tdims, preferred_element_type=F32))
            dec = jnp.exp(gsum)
            dec = jnp.concatenate([dec, dec, dec], axis=1)
            s_scr[...] = dec * s_full + jnp.where(bd, upd, 0.0)
            o2_hi, o2_lo = _split_bf16(o * o)
            ms = (_dot(o2_hi, mseg) + _dot(o2_lo, mseg)) * (1.0 / DV_C)
            y = o * lax.rsqrt(ms + LN_EPS) * ng * (rg * _sigmoid(rg))
            y_ref[b, pl.ds(r0, C), :] = y
            return carry

        lax.fori_loop(0, T // C, chunk, 0)
        for h in range(N_HEADS_C):
            snew_ref[b, h] = s_scr[h * DK_C:(h + 1) * DK_C, h * DV_C:(h + 1) * DV_C]


def _gla(pc3, s0, wg2, bgate, ng, Tv, Bb):
    B, T, _ = pc3.shape
    C = _pick(T, (48, 32, 16, 8))
    const2 = lambda i: (0, 0)
    seq3 = lambda i: (i, 0, 0)
    seq4 = lambda i: (i, 0, 0, 0)
    st = (Bb, N_HEADS_C, DK_C, DV_C)
    return pl.pallas_call(
        functools.partial(_gla_kernel, T=T, C=C, Tv=Tv, Bb=Bb),
        grid=(B // Bb,),
        in_specs=[pl.BlockSpec((Bb, T, PC_W), seq3), pl.BlockSpec(st, seq4),
                  pl.BlockSpec((GATE_RANK, D_QK_C), const2), pl.BlockSpec((1, D_QK_C), const2),
                  pl.BlockSpec((1, D_C), const2)],
        out_specs=[pl.BlockSpec((Bb, T, D_C), seq3), pl.BlockSpec(st, seq4)],
        out_shape=[_sds((B, T, D_C)), _sds((B, N_HEADS_C, DK_C, DV_C))],
        scratch_shapes=[pltpu.VMEM((D_QK_C, D_C), F32)],
        compiler_params=_cparams("parallel"), name="gla",
    )(pc3, s0, wg2, bgate, ng)


def _outproj_kernel(ya_ref, yb_ref, yc_ref, x_ref, w_ref, g_ref, b_ref, h_ref, *, alpha):
    y = (_dot(ya_ref[...].astype(BF16), w_ref[0:D_A, :])
         + _dot(yb_ref[...].astype(BF16), w_ref[D_A:D_A + D_B, :])
         + _dot(yc_ref[...].astype(BF16), w_ref[D_A + D_B:D_MODEL, :]))
    h_ref[...] = _ln(alpha * x_ref[...] + y, g_ref[...], b_ref[...])


def _outproj(ya, yb, yc, x, row0, w, g, b, alpha):
    n = ya.shape[0]
    tm = _pick(n, (384, 512, 256, 128, 64, 32, 16, 8))
    while row0 % tm:
        tm //= 2
    off = row0 // tm
    row = lambda i: (i, 0)
    const = lambda i: (0, 0)
    return pl.pallas_call(
        functools.partial(_outproj_kernel, alpha=alpha),
        grid=(n // tm,),
        in_specs=[pl.BlockSpec((tm, D_A), row), pl.BlockSpec((tm, D_B), row), pl.BlockSpec((tm, D_C), row),
                  pl.BlockSpec((tm, D_MODEL), lambda i: (i + off, 0)),
                  pl.BlockSpec((D_MODEL, D_MODEL), const), pl.BlockSpec((1, D_MODEL), const),
                  pl.BlockSpec((1, D_MODEL), const)],
        out_specs=pl.BlockSpec((tm, D_MODEL), row), out_shape=_sds((n, D_MODEL)),
        compiler_params=_cparams("parallel"), name="outproj",
    )(ya, yb, yc, x, w, g, b)


def _router_kernel(h_ref, wt_ref, b_ref, idx_ref, gate_ref, cnt_ref):
    nt = (((1,), (1,)), ((), ()))
    hh, hl = _split_bf16(h_ref[...])
    wh, wl = _split_bf16(wt_ref[...])
    logits = (lax.dot_general(wh, hh, nt, preferred_element_type=F32)
              + lax.dot_general(wh, hl, nt, preferred_element_type=F32)
              + lax.dot_general(wl, hh, nt, preferred_element_type=F32)) + b_ref[...]
    eid = lax.broadcasted_iota(jnp.int32, logits.shape, 0)
    vals = []
    member = jnp.zeros(logits.shape, F32)
    for k in range(TOP_K):
        m = jnp.max(logits, axis=0, keepdims=True)
        sel = jnp.min(jnp.where(logits == m, eid, N_EXPERTS), axis=0, keepdims=True)
        idx_ref[k:k + 1, :] = sel
        vals.append(m)
        hit = eid == sel
        member = jnp.where(hit, 1.0, member)
        logits = jnp.where(hit, -jnp.inf, logits)
    es = [jnp.exp(v - vals[0]) for v in vals]
    tot = es[0] + es[1] + es[2] + es[3]
    for k in range(TOP_K):
        gate_ref[k:k + 1, :] = es[k] / tot
    cnt_ref[...] = jnp.broadcast_to(jnp.sum(member, axis=1, keepdims=True), cnt_ref.shape)


def _router_tile(n):
    return _pick(n, (896, 640, 512, 384, 256, 128))


def _router(h, n, wt, b):
    tm = _router_tile(n)
    nt = n // tm
    return pl.pallas_call(
        _router_kernel,
        grid=(nt,),
        in_specs=[pl.BlockSpec((tm, D_MODEL), lambda i: (i, 0)),
                  pl.BlockSpec((N_EXPERTS, D_MODEL), lambda i: (0, 0)),
                  pl.BlockSpec((N_EXPERTS, 1), lambda i: (0, 0))],
        out_specs=[pl.BlockSpec((TOP_K, tm), lambda i: (0, i)), pl.BlockSpec((TOP_K, tm), lambda i: (0, i)),
                   pl.BlockSpec((N_EXPERTS, LANES), lambda i: (0, i))],
        out_shape=[_sds((TOP_K, n), jnp.int32), _sds((TOP_K, n)), _sds((N_EXPERTS, nt * LANES))],
        compiler_params=_cparams("parallel"), name="router",
    )(h, wt, b)


def _dest_kernel(idx_ref, base_ref, dest_ref):
    tm = idx_ref.shape[1]
    eid = lax.broadcasted_iota(jnp.int32, (N_EXPERTS, tm), 0)
    hits = [eid == idx_ref[k:k + 1, :] for k in range(TOP_K)]
    member = jnp.zeros((N_EXPERTS, tm), F32)
    for k in range(TOP_K):
        member = jnp.where(hits[k], 1.0, member)
    earlier = (lax.broadcasted_iota(jnp.int32, (tm, tm), 0) < lax.broadcasted_iota(jnp.int32, (tm, tm), 1))
    rank = _dot(member.astype(BF16), jnp.where(earlier, 1.0, 0.0).astype(BF16))
    pos = base_ref[...] + rank
    for k in range(TOP_K):
        dest_ref[k:k + 1, :] = jnp.sum(jnp.where(hits[k], pos, 0.0), axis=0, keepdims=True).astype(jnp.int32)


def _dest(top_idx, base):
    n = top_idx.shape[1]
    tm = _router_tile(n)
    return pl.pallas_call(
        _dest_kernel,
        grid=(n // tm,),
        in_specs=[pl.BlockSpec((TOP_K, tm), lambda i: (0, i)),
                  pl.BlockSpec((None, N_EXPERTS, 1), lambda i: (i, 0, 0))],
        out_specs=pl.BlockSpec((TOP_K, tm), lambda i: (0, i)),
        out_shape=_sds((TOP_K, n), jnp.int32),
        compiler_params=_cparams("parallel"), name="moe_dest",
    )(top_idx, base)


def _invert_kernel(vend_ref, pend_ref, dest_hbm, inv_ref, buf, sem, *, n, rc, n_rows):
    rows_per_slot = n // LANES
    chunks_per_slot = rows_per_slot // rc

    def chunk(c, carry):
        cp = pltpu.make_async_copy(dest_hbm.at[pl.ds(c * rc, rc), :], buf, sem.at[0])
        cp.start()
        cp.wait()
        k = c // chunks_per_slot
        t0 = (c - k * chunks_per_slot) * (rc * LANES)

        def row(r, carry):
            for l in range(LANES):
                inv_ref[buf[r, l]] = (t0 + r * LANES + l) * TOP_K + k
            return carry

        return lax.fori_loop(0, rc, row, carry)

    lax.fori_loop(0, TOP_K * chunks_per_slot, chunk, 0)

    def fill(lo, hi, ctr):
        def body(r, ctr):
            inv_ref[r] = ctr
            return ctr + 1
        return lax.fori_loop(lo, hi, body, ctr)

    ctr = jnp.int32(TOP_K * n)
    for e in range(N_EXPERTS):
        ctr = fill(vend_ref[e], pend_ref[e], ctr)
    fill(pend_ref[N_EXPERTS - 1], n_rows, ctr)


def _invert(valid_end, pad_end, dest, n_rows):
    k, n = dest.shape
    rc = _pick(n // LANES, (19, 16, 8, 5, 4, 2, 1))
    return pl.pallas_call(
        functools.partial(_invert_kernel, n=n, rc=rc, n_rows=n_rows),
        grid_spec=pltpu.PrefetchScalarGridSpec(
            num_scalar_prefetch=2, grid=(1,),
            in_specs=[pl.BlockSpec(memory_space=pl.ANY)],
            out_specs=pl.BlockSpec(memory_space=pltpu.SMEM),
            scratch_shapes=[pltpu.SMEM((rc, LANES), jnp.int32), pltpu.SemaphoreType.DMA((1,))]),
        out_shape=_sds((n_rows,), jnp.int32),
        compiler_params=_cparams("arbitrary"), name="moe_invert",
    )(valid_end, pad_end, dest.reshape(k * n // LANES, LANES))


def _moe_kernel(be_ref, nu_ref, inv_ref, h_hbm, w1_ref, b1_ref, w2_ref, b2_ref, y4_hbm,
                xbuf, xb, obuf, gsem, ssem, *, bm):
    i = pl.program_id(0)
    n_used = nu_ref[0]
    last_blk = pl.num_programs(0) - 1
    n_ff = D_FF // FF_CHUNK
    per = bm // n_ff

    def gather_row(blk, slot, j, r=None):
        r = inv_ref[blk * bm + j] if r is None else r
        t = lax.shift_right_logical(r, 2)
        pltpu.make_async_copy(h_hbm.at[pl.ds(t, 1), :], xbuf.at[slot, pl.ds(j, 1), :], gsem.at[slot]).start()

    def scatter_row(blk, slot, j, r=None):
        r = inv_ref[blk * bm + j] if r is None else r
        pltpu.make_async_copy(obuf.at[slot, pl.ds(j, 1), :], y4_hbm.at[pl.ds(r, 1), :], ssem.at[slot]).start()

    def wait_gather(slot):
        pltpu.make_async_copy(h_hbm.at[pl.ds(0, bm), :], xbuf.at[slot], gsem.at[slot]).wait()

    def wait_scatter(slot):
        pltpu.make_async_copy(obuf.at[slot], y4_hbm.at[pl.ds(0, bm), :], ssem.at[slot]).wait()

    def loop_rows(fn, blk, slot):
        def body(j, c):
            fn(blk, slot, j)
            return c
        lax.fori_loop(0, bm, body, 0)

    def step(with_scatter):
        slot = i % 2
        other = 1 - slot
        nxt = jnp.minimum(i + 1, last_blk)
        wait_gather(slot)
        xb[...] = xbuf[slot].astype(BF16)
        acc = jnp.zeros((bm, D_MODEL), F32)
        for c in range(n_ff):
            for j0 in range(c * per, (c + 1) * per, DMA_GROUP):
                js = range(j0, j0 + DMA_GROUP)
                g_ids = [inv_ref[nxt * bm + j] for j in js]
                s_ids = [inv_ref[(i - 1) * bm + j] for j in js] if with_scatter else []
                for j, r in zip(js, g_ids):
                    gather_row(nxt, other, j, r)
                for j, r in zip(js, s_ids):
                    scatter_row(i - 1, other, j, r)
            lo = c * FF_CHUNK
            x = xb[...]
            hg = _dot(x, w1_ref[:, lo:lo + FF_CHUNK]) + b1_ref[:, lo:lo + FF_CHUNK]
            hl = _dot(x, w1_ref[:, D_FF + lo:D_FF + lo + FF_CHUNK]) + b1_ref[:, D_FF + lo:D_FF + lo + FF_CHUNK]
            gate = jnp.minimum(hg, SWIGLU_LIMIT)
            lin = jnp.clip(hl, -SWIGLU_LIMIT, SWIGLU_LIMIT)
            act = gate * _sigmoid(SWIGLU_ALPHA * gate) * (lin + 1.0)
            acc = acc + _dot(act.astype(BF16), w2_ref[lo:lo + FF_CHUNK, :])

        @pl.when(i >= 2)
        def _():
            wait_scatter(slot)
        obuf[slot] = acc + b2_ref[...]

    @pl.when(jnp.logical_and(i == 0, n_used > 0))
    def _():
        loop_rows(gather_row, 0, 0)
        step(False)

    @pl.when(jnp.logical_and(i > 0, i < n_used))
    def _():
        step(True)

    @pl.when(i == n_used - 1)
    def _():
        slot = i % 2
        loop_rows(scatter_row, i, slot)
        wait_scatter(slot)

        @pl.when(i >= 1)
        def _():
            wait_scatter(1 - slot)
        wait_gather(1 - slot)
        obuf[0] = jnp.zeros((bm, D_MODEL), F32)

        def zero_block(blk, c):
            cp = pltpu.make_async_copy(obuf.at[0], y4_hbm.at[pl.ds(pl.multiple_of(blk * bm, bm), bm), :],
                                       ssem.at[0])
            cp.start()
            cp.wait()
            return c
        lax.fori_loop(n_used, last_blk + 1, zero_block, 0)


def _moe(block_e, n_used, inv, h, w1, b1, w2, b2, bm):
    nblk = block_e.shape[0]
    wmap = lambda i, be, nu, iv: (be[i], 0, 0)
    return pl.pallas_call(
        functools.partial(_moe_kernel, bm=bm),
        grid_spec=pltpu.PrefetchScalarGridSpec(
            num_scalar_prefetch=3, grid=(nblk,),
            in_specs=[pl.BlockSpec(memory_space=pl.ANY),
                      pl.BlockSpec((None, D_MODEL, 2 * D_FF), wmap), pl.BlockSpec((None, 1, 2 * D_FF), wmap),
                      pl.BlockSpec((None, D_FF, D_MODEL), wmap), pl.BlockSpec((None, 1, D_MODEL), wmap)],
            out_specs=pl.BlockSpec(memory_space=pl.ANY),
            scratch_shapes=[pltpu.VMEM((2, bm, D_MODEL), F32), pltpu.VMEM((bm, D_MODEL), BF16),
                            pltpu.VMEM((2, bm, D_MODEL), F32),
                            pltpu.SemaphoreType.DMA((2,)), pltpu.SemaphoreType.DMA((2,))]),
        out_shape=_sds((nblk * bm, D_MODEL)),
        compiler_params=_cparams("arbitrary"), name="moe_experts",
    )(block_e, n_used, inv, h, w1, b1, w2, b2)


def _combine_kernel(y4_ref, gates_ref, h_ref, g_ref, b_ref, out_ref, *, alpha):
    gates = gates_ref[...]
    y = y4_ref[:, 0:D_MODEL] * gates[:, 0:1]
    for k in range(1, TOP_K):
        y = y + y4_ref[:, k * D_MODEL:(k + 1) * D_MODEL] * gates[:, k:k + 1]
    out_ref[...] = _ln(alpha * h_ref[...] + y, g_ref[...], b_ref[...])


def _combine(y4, gates_t, h, n, g, b, alpha):
    tm = _pick(n, (224, 128, 64, 32, 16, 8))
    row = lambda i: (i, 0)
    const = lambda i: (0, 0)
    return pl.pallas_call(
        functools.partial(_combine_kernel, alpha=alpha),
        grid=(n // tm,),
        in_specs=[pl.BlockSpec((tm, TOP_K * D_MODEL), row), pl.BlockSpec((tm, TOP_K), row),
                  pl.BlockSpec((tm, D_MODEL), row),
                  pl.BlockSpec((1, D_MODEL), const), pl.BlockSpec((1, D_MODEL), const)],
        out_specs=pl.BlockSpec((tm, D_MODEL), row),
        out_shape=_sds((n, D_MODEL)),
        compiler_params=_cparams("parallel"), name="moe_combine",
    )(y4.reshape(-1, TOP_K * D_MODEL), gates_t, h, g, b)


def _group_layout(cnt_tiles, nblk, bm):
    cnt = cnt_tiles.astype(jnp.int32)
    counts = jnp.sum(cnt, axis=1)
    padded = (counts + bm - 1) // bm * bm
    pad_end = jnp.cumsum(padded)
    pad_start = pad_end - padded
    base = pad_start[:, None] + jnp.cumsum(cnt, axis=1) - cnt
    starts = jnp.arange(nblk, dtype=jnp.int32) * bm
    block_e = jnp.minimum(jnp.sum((pad_end[None, :] <= starts[:, None]).astype(jnp.int32), axis=1),
                          N_EXPERTS - 1).astype(jnp.int32)
    n_used = (pad_end[-1] // bm).astype(jnp.int32).reshape(1)
    valid_end = (pad_start + counts).astype(jnp.int32)
    return base.T.astype(F32)[:, :, None], valid_end, pad_end.astype(jnp.int32), block_e, n_used


def _pack_w_in(w):
    o = 0
    xa, o = w[:, o:o + D_A], o + D_A
    ga, o = w[:, o:o + D_A], o + D_A
    vb, o = w[:, o:o + D_B], o + D_B
    gb, o = w[:, o:o + D_B], o + D_B
    q, o = w[:, o:o + D_QK_C], o + D_QK_C
    k, o = w[:, o:o + D_QK_C], o + D_QK_C
    v, o = w[:, o:o + D_C], o + D_C
    r, o = w[:, o:o + D_C], o + D_C
    z = w[:, o:o + GATE_RANK]
    zq = jnp.zeros((w.shape[0], PC_K - PC_Z - GATE_RANK), w.dtype)
    zk = jnp.zeros((w.shape[0], PC_V - PC_K - D_QK_C), w.dtype)
    return jnp.concatenate([xa, ga, vb, gb, q, z, zq, k, zk, v, r], axis=1).astype(BF16)


def _block_diag(w):
    eye = jnp.eye(N_BLK_A, dtype=w.dtype)
    return jnp.einsum("hij,hg->higj", w, eye).reshape(D_A, D_A)


def kernel(x_prompt, x_sample, state_conv_a, state_rglru, state_conv_b, state_gla, meta_tokens, ln0_g, ln0_b,
           w_in, conv_a_w, conv_a_b, w_rg, b_rg, w_ig, b_ig, lru_lambda, conv_b_w, conv_b_b, ln_b_g, ln_b_b,
           w_gate2, b_gate, gla_norm_g, w_out, ln1_g, ln1_b, router_w, router_b, moe_w1, moe_b1, moe_w2, moe_b2,
           ln2_g, ln2_b):
    bp, seq, _ = x_prompt.shape
    bs, dseq, _ = x_sample.shape
    depth = w_in.shape[0]
    tp = N_META + seq
    ts = SAMPLE_PAD_T
    np_rows = bp * tp
    ns_rows = bs * dseq
    n = np_rows + ns_rows
    alpha = (2 * depth) ** 0.25
    row = lambda a: a.reshape(1, -1)

    meta = jnp.broadcast_to(meta_tokens[None], (bp, N_META, D_MODEL))
    xp_in = jnp.concatenate([meta, x_prompt], axis=1).reshape(np_rows, D_MODEL)
    xs_in = jnp.pad(x_sample, ((0, 0), (0, ts - dseq), (0, 0))).reshape(bs * ts, D_MODEL)
    zeros_p = (jnp.zeros((bp, CONV_A - 1, D_A), F32), jnp.zeros((bp, 1, D_A), F32),
               jnp.zeros((bp, CONV_B - 1, D_B), F32), jnp.zeros((bp, N_HEADS_C, DK_C, DV_C), F32))
    sb = _pick(bs, (8, 4, 2, 1))

    new_p = ([], [], [], [])
    new_s = ([], [], [], [])
    x_all = None
    for l in range(depth):
        w_packed = _pack_w_in(w_in[l])
        wg = jnp.concatenate([_block_diag(w_rg[l]), _block_diag(w_ig[l])], axis=1).astype(BF16)
        bg = jnp.concatenate([b_rg[l], b_ig[l]]).reshape(1, -1)
        w_out_b = w_out[l].astype(BF16)
        ng = row(jnp.tile(gla_norm_g[l], N_HEADS_C))
        first = l == 0
        if first:
            pa_p, pb_p, pc_p, xn_p = _inproj(xp_in, 0, np_rows, row(ln0_g), row(ln0_b), w_packed, True)
            pa_s, pb_s, pc_s, xn_s = _inproj(xs_in, 0, bs * ts, row(ln0_g), row(ln0_b), w_packed, True)
            res_p, res_p_row0 = xn_p, 0
        else:
            pa_p, pb_p, pc_p = _inproj(x_all, 0, np_rows, row(ln0_g), row(ln0_b), w_packed, False)
            xn_s = jnp.pad(x_all[np_rows:].reshape(bs, dseq, D_MODEL),
                           ((0, 0), (0, ts - dseq), (0, 0))).reshape(bs * ts, D_MODEL)
            pa_s, pb_s, pc_s = _inproj(xn_s, 0, bs * ts, row(ln0_g), row(ln0_b), w_packed, False)
            res_p, res_p_row0 = x_all, 0

        outs = []
        for (pa, pb, pc, nb, t, tv, bb, st) in (
                (pa_p, pb_p, pc_p, bp, tp, tp, 1, zeros_p),
                (pa_s, pb_s, pc_s, bs, ts, dseq, sb,
                 (state_conv_a[l], state_rglru[l].reshape(bs, 1, D_A), state_conv_b[l], state_gla[l]))):
            ya, ca_new, h_last = _rglru(pa.reshape(nb, t, PA_W), st[0], st[1], conv_a_w[l], row(conv_a_b[l]),
                                        wg, bg, row(lru_lambda[l]), tv, bb)
            yb, cb_new = _convb(pb.reshape(nb, t, PB_W), st[2], conv_b_w[l], row(conv_b_b[l]),
                                row(ln_b_g[l]), row(ln_b_b[l]), tv, bb)
            yc, s_new = _gla(pc.reshape(nb, t, PC_W), st[3], w_gate2[l], row(b_gate[l]), ng, tv, bb)
            outs.append((ya.reshape(nb * t, D_A), yb.reshape(nb * t, D_B), yc.reshape(nb * t, D_C),
                         ca_new, h_last.reshape(nb, D_A), cb_new, s_new))
        (ya_p, yb_p, yc_p, *st_p), (ya_s, yb_s, yc_s, *st_s) = outs
        for j in range(4):
            new_p[j].append(st_p[j])
            new_s[j].append(st_s[j])

        h_p = _outproj(ya_p, yb_p, yc_p, res_p, res_p_row0, w_out_b, row(ln1_g[l]), row(ln1_b[l]), alpha)
        h_s = _outproj(ya_s, yb_s, yc_s, xn_s, 0, w_out_b, row(ln1_g[l]), row(ln1_b[l]), alpha)
        h_s = h_s.reshape(bs, ts, D_MODEL)[:, :dseq].reshape(ns_rows, D_MODEL)
        nblk = -(-(TOP_K * n) // MOE_BM) + N_EXPERTS
        h_all = jnp.concatenate([h_p, h_s, jnp.zeros((nblk * MOE_BM // TOP_K - n, D_MODEL), F32)], axis=0)

        top_idx, gates, cnt = _router(h_all, n, router_w[l].T, router_b[l].reshape(-1, 1))
        base, valid_end, pad_end, block_e, n_used = _group_layout(cnt[:, ::LANES], nblk, MOE_BM)
        dest = _dest(top_idx, base)
        inv = _invert(valid_end, pad_end, dest, nblk * MOE_BM)
        y4 = _moe(block_e, n_used, inv, h_all, moe_w1[l].astype(BF16),
                  moe_b1[l].reshape(N_EXPERTS, 1, -1), moe_w2[l].astype(BF16),
                  moe_b2[l].reshape(N_EXPERTS, 1, -1), MOE_BM)
        x_all = _combine(y4, gates.T, h_all, n, row(ln2_g[l]), row(ln2_b[l]), alpha)

    y_p = x_all[:np_rows].reshape(bp, tp, D_MODEL)[:, N_META:]
    y_s = x_all[np_rows:].reshape(bs, dseq, D_MODEL)
    return (y_p, y_s,
            jnp.stack(new_p[0]), jnp.stack(new_p[1]), jnp.stack(new_p[2]), jnp.stack(new_p[3]),
            jnp.stack(new_s[0]), jnp.stack(new_s[1]), jnp.stack(new_s[2]), jnp.stack(new_s[3]))
```

```python
import functools

import jax
import jax.numpy as jnp
from jax import lax
from jax.experimental import pallas as pl
from jax.experimental.pallas import tpu as pltpu

F32 = jnp.float32
BF16 = jnp.bfloat16

D_MODEL = 1024
N_META = 16
D_A = 384
D_B = 256
D_C = 384
N_BLK_A = 8
BLK_A = D_A // N_BLK_A
CONV_A = 4
RG_C = 8.0
CONV_B = 31
N_HEADS_C = 4
DV_C = D_C // N_HEADS_C
DK_C = DV_C // 2
D_QK_C = N_HEADS_C * DK_C
GATE_RANK = 16
GATE_TAU = 16.0
N_EXPERTS = 32
TOP_K = 4
D_FF = D_MODEL
SWIGLU_LIMIT = 7.0
SWIGLU_ALPHA = 1.702
LN_EPS = 1e-5

PA_W = 2 * D_A
PB_W = 2 * D_B
PC_Q, PC_Z, PC_K, PC_V, PC_R, PC_W = 0, 192, 256, 512, 896, 1280
P_W = PA_W + PB_W + PC_W

SUBLANES = 8
LANES = 128
ROW_TILE = D_MODEL // LANES
VMEM_LIMIT_BYTES = 56 * 1024 * 1024
MOE_BM = 256
FF_CHUNK = 256
DMA_GROUP = 8
SAMPLE_PAD_T = 8


def _cparams(*sem):
    return pltpu.CompilerParams(dimension_semantics=sem, vmem_limit_bytes=VMEM_LIMIT_BYTES)


def _sds(shape, dtype=F32):
    return jax.ShapeDtypeStruct(shape, dtype)


def _pick(n, prefs):
    for p in prefs:
        if n % p == 0:
            return p
    raise ValueError(f"no tile for {n} in {prefs}")


def _ln(x, g, b):
    mu = jnp.mean(x, axis=-1, keepdims=True)
    xc = x - mu
    var = jnp.mean(xc * xc, axis=-1, keepdims=True)
    return xc * lax.rsqrt(var + LN_EPS) * g + b


def _sigmoid(x):
    return 1.0 / (1.0 + jnp.exp(-x))


def _split_bf16(x):
    hi = x.astype(BF16)
    lo = (x - hi.astype(F32)).astype(BF16)
    return hi, lo


def _dot(a, b):
    return jnp.dot(a, b, preferred_element_type=F32)


def _inproj_kernel(x_ref, g_ref, b_ref, w_ref, pa_ref, pb_ref, pc_ref, *maybe_xn, apply_ln):
    x = x_ref[...]
    if apply_ln:
        x = _ln(x, g_ref[...], b_ref[...])
        maybe_xn[0][...] = x
    xb = x.astype(BF16)
    pa_ref[...] = _dot(xb, w_ref[:, 0:PA_W])
    pb_ref[...] = _dot(xb, w_ref[:, PA_W:PA_W + PB_W])
    pc_ref[...] = _dot(xb, w_ref[:, PA_W + PB_W:P_W])


def _inproj(x, row0, nrows, ln_g, ln_b, w_packed, apply_ln):
    tm = _pick(nrows, (384, 512, 256, 128, 64, 32, 16, 8))
    while row0 % tm:
        tm //= 2
    off = row0 // tm
    const = lambda i: (0, 0)
    row = lambda i: (i, 0)
    out_shape = [_sds((nrows, PA_W)), _sds((nrows, PB_W)), _sds((nrows, PC_W))]
    out_specs = [pl.BlockSpec((tm, PA_W), row), pl.BlockSpec((tm, PB_W), row), pl.BlockSpec((tm, PC_W), row)]
    if apply_ln:
        out_shape.append(_sds((nrows, D_MODEL)))
        out_specs.append(pl.BlockSpec((tm, D_MODEL), row))
    return pl.pallas_call(
        functools.partial(_inproj_kernel, apply_ln=apply_ln),
        grid=(nrows // tm,),
        in_specs=[pl.BlockSpec((tm, D_MODEL), lambda i: (i + off, 0)),
                  pl.BlockSpec((1, D_MODEL), const), pl.BlockSpec((1, D_MODEL), const),
                  pl.BlockSpec((D_MODEL, P_W), const)],
        out_specs=out_specs, out_shape=out_shape,
        compiler_params=_cparams("parallel"), name="inproj",
    )(x, ln_g, ln_b, w_packed)


def _rglru_kernel(p_ref, cbuf_ref, h0_ref, cw_ref, cb_ref, wg_ref, bg_ref, lam_ref,
                  y_ref, cnew_ref, hlast_ref, xp_scr, a_scr, h_scr, *, T, Tc, Tv, Bb):
    lam = lam_ref[...]
    softplus_neg = jnp.maximum(-lam, 0.0) + jnp.log1p(jnp.exp(-jnp.abs(lam)))
    c_decay = -RG_C * softplus_neg
    cw = cw_ref[...]
    cb = cb_ref[...]
    bg = bg_ref[...]
    sub = lax.broadcasted_iota(jnp.int32, (Tc, D_A), 0) % SUBLANES
    halo = SUBLANES - (CONV_A - 1)
    for b in range(Bb):
        xp_scr[halo:SUBLANES, :] = cbuf_ref[b]
        xp_scr[SUBLANES:SUBLANES + T, :] = p_ref[b, :, 0:D_A]
        cnew_ref[b] = xp_scr[halo + Tv:SUBLANES + Tv, :]

        def chunk(ci, h_b):
            r0 = pl.multiple_of(ci * Tc, SUBLANES)
            win = xp_scr[pl.ds(r0, Tc + SUBLANES), :]
            xc = cb + cw[0:1] * win[halo:halo + Tc]
            for j in range(1, CONV_A):
                xc = xc + cw[j:j + 1] * win[halo + j:halo + j + Tc]
            gates = _dot(xc.astype(BF16), wg_ref[...]) + bg
            r = _sigmoid(gates[:, 0:D_A])
            i = _sigmoid(gates[:, D_A:2 * D_A])
            log_a = c_decay * r
            a = jnp.exp(log_a)
            u = jnp.sqrt(1.0 - a * a) * (i * xc)
            for s in (1, 2, 4):
                keep = sub >= s
                a_prev = pltpu.roll(a, s, 0)
                u_prev = pltpu.roll(u, s, 0)
                u = jnp.where(keep, a * u_prev + u, u)
                a = jnp.where(keep, a * a_prev, a)
            a_scr[...] = a
            h_scr[pl.ds(r0, Tc), :] = u

            def group(gi, h_b):
                c0 = pl.multiple_of(gi * SUBLANES, SUBLANES)
                g0 = pl.multiple_of(r0 + gi * SUBLANES, SUBLANES)
                h8 = a_scr[pl.ds(c0, SUBLANES), :] * h_b + h_scr[pl.ds(g0, SUBLANES), :]
                h_scr[pl.ds(g0, SUBLANES), :] = h8
                return jnp.broadcast_to(h8[SUBLANES - 1:SUBLANES, :], (SUBLANES, D_A))

            h_b = lax.fori_loop(0, Tc // SUBLANES, group, h_b)
            ga = p_ref[b, pl.ds(r0, Tc), D_A:2 * D_A]
            gelu = 0.5 * ga * (1.0 + jnp.tanh(0.7978845608028654 * (ga + 0.044715 * ga * ga * ga)))
            y_ref[b, pl.ds(r0, Tc), :] = h_scr[pl.ds(r0, Tc), :] * gelu
            return h_b

        h_b = jnp.broadcast_to(h0_ref[b], (SUBLANES, D_A))
        lax.fori_loop(0, T // Tc, chunk, h_b)
        hlast_ref[b] = h_scr[Tv - 1:Tv, :]


def _rglru(pa3, cbuf, h0, cw, cb, wg, bg, lam, Tv, Bb):
    B, T, _ = pa3.shape
    Tc = _pick(T, (344, 256, 128, 64, 48, 32, 16, 8))
    const2 = lambda i: (0, 0)
    seq3 = lambda i: (i, 0, 0)
    return pl.pallas_call(
        functools.partial(_rglru_kernel, T=T, Tc=Tc, Tv=Tv, Bb=Bb),
        grid=(B // Bb,),
        in_specs=[pl.BlockSpec((Bb, T, PA_W), seq3), pl.BlockSpec((Bb, CONV_A - 1, D_A), seq3),
                  pl.BlockSpec((Bb, 1, D_A), seq3), pl.BlockSpec((CONV_A, D_A), const2),
                  pl.BlockSpec((1, D_A), const2), pl.BlockSpec((D_A, 2 * D_A), const2),
                  pl.BlockSpec((1, 2 * D_A), const2), pl.BlockSpec((1, D_A), const2)],
        out_specs=[pl.BlockSpec((Bb, T, D_A), seq3), pl.BlockSpec((Bb, CONV_A - 1, D_A), seq3),
                   pl.BlockSpec((Bb, 1, D_A), seq3)],
        out_shape=[_sds((B, T, D_A)), _sds((B, CONV_A - 1, D_A)), _sds((B, 1, D_A))],
        scratch_shapes=[pltpu.VMEM((T + 2 * SUBLANES, D_A), F32), pltpu.VMEM((Tc, D_A), F32),
                        pltpu.VMEM((T, D_A), F32)],
        compiler_params=_cparams("parallel"), name="rglru",
    )(pa3, cbuf, h0, cw, cb, wg, bg, lam)


B_HALO = 32


def _convb_kernel(p_ref, buf_ref, w_ref, cb_ref, g_ref, b_ref, y_ref, bnew_ref, u_scr, *, T, Tc, Tv, Bb):
    w = w_ref[...]
    cb = cb_ref[...]
    g = g_ref[...]
    bb = b_ref[...]
    first = B_HALO - (CONV_B - 1)
    for b in range(Bb):
        u_scr[0:first, :] = jnp.zeros((first, D_B), F32)
        u_scr[first:B_HALO, :] = buf_ref[b]
        u_scr[B_HALO:B_HALO + T, :] = p_ref[b, :, 0:D_B] * _sigmoid(p_ref[b, :, D_B:2 * D_B])
        bnew_ref[b] = u_scr[first + Tv:B_HALO + Tv, :]

        def chunk(ci, carry):
            r0 = pl.multiple_of(ci * Tc, SUBLANES)
            win = u_scr[pl.ds(r0, Tc + B_HALO), :]
            acc = cb + w[0:1] * win[first:first + Tc]
            for j in range(1, CONV_B):
                acc = acc + w[j:j + 1] * win[first + j:first + j + Tc]
            yn = _ln(acc, g, bb)
            y_ref[b, pl.ds(r0, Tc), :] = yn * _sigmoid(yn)
            return carry

        lax.fori_loop(0, T // Tc, chunk, 0)


def _convb(pb3, buf, w, cb, g, b, Tv, Bb):
    B, T, _ = pb3.shape
    Tc = _pick(T, (48, 32, 16, 8))
    const2 = lambda i: (0, 0)
    seq3 = lambda i: (i, 0, 0)
    return pl.pallas_call(
        functools.partial(_convb_kernel, T=T, Tc=Tc, Tv=Tv, Bb=Bb),
        grid=(B // Bb,),
        in_specs=[pl.BlockSpec((Bb, T, PB_W), seq3), pl.BlockSpec((Bb, CONV_B - 1, D_B), seq3),
                  pl.BlockSpec((CONV_B, D_B), const2), pl.BlockSpec((1, D_B), const2),
                  pl.BlockSpec((1, D_B), const2), pl.BlockSpec((1, D_B), const2)],
        out_specs=[pl.BlockSpec((Bb, T, D_B), seq3), pl.BlockSpec((Bb, CONV_B - 1, D_B), seq3)],
        out_shape=[_sds((B, T, D_B)), _sds((B, CONV_B - 1, D_B))],
        scratch_shapes=[pltpu.VMEM((T + B_HALO, D_B), F32)],
        compiler_params=_cparams("parallel"), name="convb",
    )(pb3, buf, w, cb, g, b)


def _gla_kernel(p_ref, s0_ref, wg2_ref, bgate_ref, ng_ref, y_ref, snew_ref, s_scr, *, T, C, Tv, Bb):
    ri = lax.broadcasted_iota(jnp.int32, (C, C), 0)
    ci_ = lax.broadcasted_iota(jnp.int32, (C, C), 1)
    tril = ri >= ci_
    ltri = jnp.where(tril, 1.0, 0.0).astype(BF16)
    lane_k = lax.broadcasted_iota(jnp.int32, (1, D_QK_C), 1)
    lane_v = lax.broadcasted_iota(jnp.int32, (1, D_C), 1)
    hm_k = [(lane_k >= h * DK_C) & (lane_k < (h + 1) * DK_C) for h in range(N_HEADS_C)]
    hm_v = [(lane_v >= h * DV_C) & (lane_v < (h + 1) * DV_C) for h in range(N_HEADS_C)]
    rs = lax.broadcasted_iota(jnp.int32, (D_QK_C, D_C), 0)
    cs = lax.broadcasted_iota(jnp.int32, (D_QK_C, D_C), 1)
    bd = (rs >= 0) & (rs < 0)
    for h in range(N_HEADS_C):
        bd = bd | ((rs >= h * DK_C) & (rs < (h + 1) * DK_C) & (cs >= h * DV_C) & (cs < (h + 1) * DV_C))
    rm = lax.broadcasted_iota(jnp.int32, (D_C, D_C), 0)
    cm = lax.broadcasted_iota(jnp.int32, (D_C, D_C), 1)
    seg = (rm >= 0) & (rm < 0)
    for h in range(N_HEADS_C):
        seg = seg | ((rm >= h * DV_C) & (rm < (h + 1) * DV_C) & (cm >= h * DV_C) & (cm < (h + 1) * DV_C))
    mseg = jnp.where(seg, 1.0, 0.0).astype(BF16)
    ones_cl = jnp.ones((C, 128), BF16)
    wg2 = wg2_ref[...].astype(BF16)
    bgate = bgate_ref[...]
    ng = ng_ref[...]
    rowi = lax.broadcasted_iota(jnp.int32, (C, 1), 0)
    tdims = (((0,), (0,)), ((), ()))

    for b in range(Bb):
        s_scr[...] = jnp.zeros((D_QK_C, D_C), F32)
        for h in range(N_HEADS_C):
            s_scr[h * DK_C:(h + 1) * DK_C, h * DV_C:(h + 1) * DV_C] = s0_ref[b, h]

        def chunk(ci, carry):
            r0 = pl.multiple_of(ci * C, SUBLANES)
            q = p_ref[b, pl.ds(r0, C), PC_Q:PC_Q + D_QK_C] * (DK_C ** -0.5)
            z = p_ref[b, pl.ds(r0, C), PC_Z:PC_Z + GATE_RANK]
            k = p_ref[b, pl.ds(r0, C), PC_K:PC_K + D_QK_C]
            v = p_ref[b, pl.ds(r0, C), PC_V:PC_V + D_C]
            rg = p_ref[b, pl.ds(r0, C), PC_R:PC_R + D_C]
            pre = _dot(z.astype(BF16), wg2) + bgate
            g = (jnp.minimum(pre, 0.0) - jnp.log1p(jnp.exp(-jnp.abs(pre)))) * (1.0 / GATE_TAU)
            valid = (r0 + rowi) < Tv
            g = jnp.where(valid, g, 0.0)
            k = jnp.where(valid, k, 0.0)
            g_hi, g_lo = _split_bf16(g)
            gcum = _dot(ltri, g_hi) + _dot(ltri, g_lo)
            g_last = gcum[C - 1:C, :]
            g_mid = gcum[C // 2 - 1:C // 2, :]
            s_full = s_scr[...]
            vb = v.astype(BF16)
            o = _dot((q * jnp.exp(gcum)).astype(BF16), s_full.astype(BF16))
            qt = q * jnp.exp(gcum - g_mid)
            ktb = (k * jnp.exp(g_mid - gcum)).astype(BF16)
            for h in range(N_HEADS_C):
                qh = jnp.where(hm_k[h], qt, 0.0).astype(BF16)
                sc = lax.dot_general(qh, ktb, (((1,), (1,)), ((), ())), preferred_element_type=F32)
                sc = jnp.where(tril, sc, 0.0)
                oh = _dot(sc.astype(BF16), vb)
                o = o + jnp.where(hm_v[h], oh, 0.0)
            kd = (k * jnp.exp(g_last - gcum)).astype(BF16)
            upd = lax.dot_general(kd, vb, tdims, preferred_element_type=F32)
            gsum = (lax.dot_general(g_hi, ones_cl, tdims, preferred_element_type=F32)
                    + lax.dot_general(g_lo, ones_cl, tdims, preferred_element_type=F32))
            dec = jnp.exp(gsum)
            dec = jnp.concatenate([dec, dec, dec], axis=1)
            s_scr[...] = dec * s_full + jnp.where(bd, upd, 0.0)
            o2_hi, o2_lo = _split_bf16(o * o)
            ms = (_dot(o2_hi, mseg) + _dot(o2_lo, mseg)) * (1.0 / DV_C)
            y = o * lax.rsqrt(ms + LN_EPS) * ng * (rg * _sigmoid(rg))
            y_ref[b, pl.ds(r0, C), :] = y
            return carry

        lax.fori_loop(0, T // C, chunk, 0)
        for h in range(N_HEADS_C):
            snew_ref[b, h] = s_scr[h * DK_C:(h + 1) * DK_C, h * DV_C:(h + 1) * DV_C]


def _gla(pc3, s0, wg2, bgate, ng, Tv, Bb):
    B, T, _ = pc3.shape
    C = _pick(T, (48, 32, 16, 8))
    const2 = lambda i: (0, 0)
    seq3 = lambda i: (i, 0, 0)
    seq4 = lambda i: (i, 0, 0, 0)
    st = (Bb, N_HEADS_C, DK_C, DV_C)
    return pl.pallas_call(
        functools.partial(_gla_kernel, T=T, C=C, Tv=Tv, Bb=Bb),
        grid=(B // Bb,),
        in_specs=[pl.BlockSpec((Bb, T, PC_W), seq3), pl.BlockSpec(st, seq4),
                  pl.BlockSpec((GATE_RANK, D_QK_C), const2), pl.BlockSpec((1, D_QK_C), const2),
                  pl.BlockSpec((1, D_C), const2)],
        out_specs=[pl.BlockSpec((Bb, T, D_C), seq3), pl.BlockSpec(st, seq4)],
        out_shape=[_sds((B, T, D_C)), _sds((B, N_HEADS_C, DK_C, DV_C))],
        scratch_shapes=[pltpu.VMEM((D_QK_C, D_C), F32)],
        compiler_params=_cparams("parallel"), name="gla",
    )(pc3, s0, wg2, bgate, ng)


def _tiles_to_rows(ref, lead, start, rows, stride):
    return jnp.concatenate([ref[lead + (pl.ds(start + s, rows, stride=stride), slice(None))]
                            for s in range(ROW_TILE)], axis=1)


def _rows_to_tiles(ref, lead, rows, val):
    for s in range(ROW_TILE):
        ref[lead + (pl.ds(s, rows, stride=ROW_TILE), slice(None))] = val[:, s * LANES:(s + 1) * LANES]


def _outproj_kernel(ya_ref, yb_ref, yc_ref, x_ref, w_ref, g_ref, b_ref, h_ref, *, alpha):
    y = (_dot(ya_ref[...].astype(BF16), w_ref[0:D_A, :])
         + _dot(yb_ref[...].astype(BF16), w_ref[D_A:D_A + D_B, :])
         + _dot(yc_ref[...].astype(BF16), w_ref[D_A + D_B:D_MODEL, :]))
    h = _ln(alpha * x_ref[...] + y, g_ref[...], b_ref[...])
    _rows_to_tiles(h_ref, (), h.shape[0], h)


def _outproj(ya, yb, yc, x, row0, w, g, b, alpha):
    n = ya.shape[0]
    tm = _pick(n, (384, 512, 256, 128, 64, 32, 16, 8))
    while row0 % tm:
        tm //= 2
    off = row0 // tm
    row = lambda i: (i, 0)
    const = lambda i: (0, 0)
    return pl.pallas_call(
        functools.partial(_outproj_kernel, alpha=alpha),
        grid=(n // tm,),
        in_specs=[pl.BlockSpec((tm, D_A), row), pl.BlockSpec((tm, D_B), row), pl.BlockSpec((tm, D_C), row),
                  pl.BlockSpec((tm, D_MODEL), lambda i: (i + off, 0)),
                  pl.BlockSpec((D_MODEL, D_MODEL), const), pl.BlockSpec((1, D_MODEL), const),
                  pl.BlockSpec((1, D_MODEL), const)],
        out_specs=pl.BlockSpec((tm * ROW_TILE, LANES), row), out_shape=_sds((n * ROW_TILE, LANES)),
        compiler_params=_cparams("parallel"), name="outproj",
    )(ya, yb, yc, x, w, g, b)


def _router_kernel(h_ref, wt_ref, b_ref, idx_ref, gate_ref, cnt_ref):
    nt = (((1,), (1,)), ((), ()))
    hh, hl = _split_bf16(_tiles_to_rows(h_ref, (), 0, idx_ref.shape[1], ROW_TILE))
    wh, wl = _split_bf16(wt_ref[...])
    logits = (lax.dot_general(wh, hh, nt, preferred_element_type=F32)
              + lax.dot_general(wh, hl, nt, preferred_element_type=F32)
              + lax.dot_general(wl, hh, nt, preferred_element_type=F32)) + b_ref[...]
    eid = lax.broadcasted_iota(jnp.int32, logits.shape, 0)
    vals = []
    member = jnp.zeros(logits.shape, F32)
    for k in range(TOP_K):
        m = jnp.max(logits, axis=0, keepdims=True)
        sel = jnp.min(jnp.where(logits == m, eid, N_EXPERTS), axis=0, keepdims=True)
        idx_ref[k:k + 1, :] = sel
        vals.append(m)
        hit = eid == sel
        member = jnp.where(hit, 1.0, member)
        logits = jnp.where(hit, -jnp.inf, logits)
    es = [jnp.exp(v - vals[0]) for v in vals]
    tot = es[0] + es[1] + es[2] + es[3]
    for k in range(TOP_K):
        gate_ref[k:k + 1, :] = es[k] / tot
    cnt_ref[...] = jnp.broadcast_to(jnp.sum(member, axis=1, keepdims=True), cnt_ref.shape)


def _router_tile(n):
    return _pick(n, (896, 640, 512, 384, 256, 128))


def _router(h, n, wt, b):
    tm = _router_tile(n)
    nt = n // tm
    return pl.pallas_call(
        _router_kernel,
        grid=(nt,),
        in_specs=[pl.BlockSpec((tm * ROW_TILE, LANES), lambda i: (i, 0)),
                  pl.BlockSpec((N_EXPERTS, D_MODEL), lambda i: (0, 0)),
                  pl.BlockSpec((N_EXPERTS, 1), lambda i: (0, 0))],
        out_specs=[pl.BlockSpec((TOP_K, tm), lambda i: (0, i)), pl.BlockSpec((TOP_K, tm), lambda i: (0, i)),
                   pl.BlockSpec((N_EXPERTS, LANES), lambda i: (0, i))],
        out_shape=[_sds((TOP_K, n), jnp.int32), _sds((TOP_K, n)), _sds((N_EXPERTS, nt * LANES))],
        compiler_params=_cparams("parallel"), name="router",
    )(h, wt, b)


def _dest_kernel(idx_ref, base_ref, dest_ref):
    tm = idx_ref.shape[1]
    eid = lax.broadcasted_iota(jnp.int32, (N_EXPERTS, tm), 0)
    hits = [eid == idx_ref[k:k + 1, :] for k in range(TOP_K)]
    member = jnp.zeros((N_EXPERTS, tm), F32)
    for k in range(TOP_K):
        member = jnp.where(hits[k], 1.0, member)
    earlier = (lax.broadcasted_iota(jnp.int32, (tm, tm), 0) < lax.broadcasted_iota(jnp.int32, (tm, tm), 1))
    rank = _dot(member.astype(BF16), jnp.where(earlier, 1.0, 0.0).astype(BF16))
    pos = base_ref[...] + rank
    for k in range(TOP_K):
        dest_ref[k:k + 1, :] = jnp.sum(jnp.where(hits[k], pos, 0.0), axis=0, keepdims=True).astype(jnp.int32)


def _dest(top_idx, base):
    n = top_idx.shape[1]
    tm = _router_tile(n)
    return pl.pallas_call(
        _dest_kernel,
        grid=(n // tm,),
        in_specs=[pl.BlockSpec((TOP_K, tm), lambda i: (0, i)),
                  pl.BlockSpec((None, N_EXPERTS, 1), lambda i: (i, 0, 0))],
        out_specs=pl.BlockSpec((TOP_K, tm), lambda i: (0, i)),
        out_shape=_sds((TOP_K, n), jnp.int32),
        compiler_params=_cparams("parallel"), name="moe_dest",
    )(top_idx, base)


def _invert_kernel(vend_ref, pend_ref, dest_hbm, inv_ref, buf, sem, *, n, rc, n_rows):
    rows_per_slot = n // LANES
    chunks_per_slot = rows_per_slot // rc

    def chunk(c, carry):
        cp = pltpu.make_async_copy(dest_hbm.at[pl.ds(c * rc, rc), :], buf, sem.at[0])
        cp.start()
        cp.wait()
        k = c // chunks_per_slot
        t0 = (c - k * chunks_per_slot) * (rc * LANES)

        def row(r, carry):
            for l in range(LANES):
                inv_ref[buf[r, l]] = (t0 + r * LANES + l) * TOP_K + k
            return carry

        return lax.fori_loop(0, rc, row, carry)

    lax.fori_loop(0, TOP_K * chunks_per_slot, chunk, 0)

    def fill(lo, hi, ctr):
        def body(r, ctr):
            inv_ref[r] = ctr
            return ctr + 1
        return lax.fori_loop(lo, hi, body, ctr)

    ctr = jnp.int32(TOP_K * n)
    for e in range(N_EXPERTS):
        ctr = fill(vend_ref[e], pend_ref[e], ctr)
    fill(pend_ref[N_EXPERTS - 1], n_rows, ctr)


def _invert(valid_end, pad_end, dest, n_rows):
    k, n = dest.shape
    rc = _pick(n // LANES, (19, 16, 8, 5, 4, 2, 1))
    return pl.pallas_call(
        functools.partial(_invert_kernel, n=n, rc=rc, n_rows=n_rows),
        grid_spec=pltpu.PrefetchScalarGridSpec(
            num_scalar_prefetch=2, grid=(1,),
            in_specs=[pl.BlockSpec(memory_space=pl.ANY)],
            out_specs=pl.BlockSpec(memory_space=pltpu.SMEM),
            scratch_shapes=[pltpu.SMEM((rc, LANES), jnp.int32), pltpu.SemaphoreType.DMA((1,))]),
        out_shape=_sds((n_rows,), jnp.int32),
        compiler_params=_cparams("arbitrary"), name="moe_invert",
    )(valid_end, pad_end, dest.reshape(k * n // LANES, LANES))


def _moe_kernel(be_ref, nu_ref, inv_ref, h_hbm, w1_ref, b1_ref, w2_ref, b2_ref, y4_hbm,
                xbuf, xb, obuf, gsem, ssem, *, bm):
    i = pl.program_id(0)
    n_used = nu_ref[0]
    last_blk = pl.num_programs(0) - 1
    n_ff = D_FF // FF_CHUNK
    per = bm // n_ff

    def tile(r):
        start = r * ROW_TILE
        return pl.ds(start if isinstance(r, int) else pl.multiple_of(start, ROW_TILE), ROW_TILE)

    def gather_row(blk, slot, j, r=None):
        r = inv_ref[blk * bm + j] if r is None else r
        t = lax.shift_right_logical(r, 2)
        pltpu.make_async_copy(h_hbm.at[tile(t), :], xbuf.at[slot, tile(j), :], gsem.at[slot]).start()

    def scatter_row(blk, slot, j, r=None):
        r = inv_ref[blk * bm + j] if r is None else r
        pltpu.make_async_copy(obuf.at[slot, tile(j), :], y4_hbm.at[tile(r), :], ssem.at[slot]).start()

    def wait_gather(slot):
        pltpu.make_async_copy(h_hbm.at[pl.ds(0, bm * ROW_TILE), :], xbuf.at[slot], gsem.at[slot]).wait()

    def wait_scatter(slot):
        pltpu.make_async_copy(obuf.at[slot], y4_hbm.at[pl.ds(0, bm * ROW_TILE), :], ssem.at[slot]).wait()

    def loop_rows(fn, blk, slot):
        def body(j, c):
            fn(blk, slot, j)
            return c
        lax.fori_loop(0, bm, body, 0)

    def step(with_scatter):
        slot = i % 2
        other = 1 - slot
        nxt = jnp.minimum(i + 1, last_blk)
        wait_gather(slot)
        xb[...] = _tiles_to_rows(xbuf, (slot,), 0, bm, ROW_TILE).astype(BF16)
        acc = jnp.zeros((bm, D_MODEL), F32)
        for c in range(n_ff):
            for j0 in range(c * per, (c + 1) * per, DMA_GROUP):
                js = range(j0, j0 + DMA_GROUP)
                g_ids = [inv_ref[nxt * bm + j] for j in js]
                s_ids = [inv_ref[(i - 1) * bm + j] for j in js] if with_scatter else []
                for j, r in zip(js, g_ids):
                    gather_row(nxt, other, j, r)
                for j, r in zip(js, s_ids):
                    scatter_row(i - 1, other, j, r)
            lo = c * FF_CHUNK
            x = xb[...]
            hg = _dot(x, w1_ref[:, lo:lo + FF_CHUNK]) + b1_ref[:, lo:lo + FF_CHUNK]
            hl = _dot(x, w1_ref[:, D_FF + lo:D_FF + lo + FF_CHUNK]) + b1_ref[:, D_FF + lo:D_FF + lo + FF_CHUNK]
            gate = jnp.minimum(hg, SWIGLU_LIMIT)
            lin = jnp.clip(hl, -SWIGLU_LIMIT, SWIGLU_LIMIT)
            act = gate * _sigmoid(SWIGLU_ALPHA * gate) * (lin + 1.0)
            acc = acc + _dot(act.astype(BF16), w2_ref[lo:lo + FF_CHUNK, :])

        @pl.when(i >= 2)
        def _():
            wait_scatter(slot)
        _rows_to_tiles(obuf, (slot,), bm, acc + b2_ref[...])

    @pl.when(jnp.logical_and(i == 0, n_used > 0))
    def _():
        loop_rows(gather_row, 0, 0)
        step(False)

    @pl.when(jnp.logical_and(i > 0, i < n_used))
    def _():
        step(True)

    @pl.when(i == n_used - 1)
    def _():
        slot = i % 2
        loop_rows(scatter_row, i, slot)
        wait_scatter(slot)

        @pl.when(i >= 1)
        def _():
            wait_scatter(1 - slot)
        wait_gather(1 - slot)
        obuf[0] = jnp.zeros((bm * ROW_TILE, LANES), F32)

        def zero_block(blk, c):
            rows = bm * ROW_TILE
            cp = pltpu.make_async_copy(obuf.at[0], y4_hbm.at[pl.ds(pl.multiple_of(blk * rows, rows), rows), :],
                                       ssem.at[0])
            cp.start()
            cp.wait()
            return c
        lax.fori_loop(n_used, last_blk + 1, zero_block, 0)


def _moe(block_e, n_used, inv, h, w1, b1, w2, b2, bm):
    nblk = block_e.shape[0]
    wmap = lambda i, be, nu, iv: (be[i], 0, 0)
    return pl.pallas_call(
        functools.partial(_moe_kernel, bm=bm),
        grid_spec=pltpu.PrefetchScalarGridSpec(
            num_scalar_prefetch=3, grid=(nblk,),
            in_specs=[pl.BlockSpec(memory_space=pl.ANY),
                      pl.BlockSpec((None, D_MODEL, 2 * D_FF), wmap), pl.BlockSpec((None, 1, 2 * D_FF), wmap),
                      pl.BlockSpec((None, D_FF, D_MODEL), wmap), pl.BlockSpec((None, 1, D_MODEL), wmap)],
            out_specs=pl.BlockSpec(memory_space=pl.ANY),
            scratch_shapes=[pltpu.VMEM((2, bm * ROW_TILE, LANES), F32), pltpu.VMEM((bm, D_MODEL), BF16),
                            pltpu.VMEM((2, bm * ROW_TILE, LANES), F32),
                            pltpu.SemaphoreType.DMA((2,)), pltpu.SemaphoreType.DMA((2,))]),
        out_shape=_sds((nblk * bm * ROW_TILE, LANES)),
        compiler_params=_cparams("arbitrary"), name="moe_experts",
    )(block_e, n_used, inv, h, w1, b1, w2, b2)


def _combine_kernel(y4_ref, gates_ref, h_ref, g_ref, b_ref, out_ref, *, alpha):
    tm = out_ref.shape[0]
    gates = gates_ref[...]
    acc = alpha * _tiles_to_rows(h_ref, (), 0, tm, ROW_TILE)
    for k in range(TOP_K):
        acc = acc + _tiles_to_rows(y4_ref, (), k * ROW_TILE, tm, TOP_K * ROW_TILE) * gates[:, k:k + 1]
    out_ref[...] = _ln(acc, g_ref[...], b_ref[...])


def _combine(y4, gates_t, h, n, g, b, alpha):
    tm = _pick(n, (224, 128, 64, 32, 16, 8))
    row = lambda i: (i, 0)
    const = lambda i: (0, 0)
    return pl.pallas_call(
        functools.partial(_combine_kernel, alpha=alpha),
        grid=(n // tm,),
        in_specs=[pl.BlockSpec((TOP_K * tm * ROW_TILE, LANES), row), pl.BlockSpec((tm, TOP_K), row),
                  pl.BlockSpec((tm * ROW_TILE, LANES), row),
                  pl.BlockSpec((1, D_MODEL), const), pl.BlockSpec((1, D_MODEL), const)],
        out_specs=pl.BlockSpec((tm, D_MODEL), row),
        out_shape=_sds((n, D_MODEL)),
        compiler_params=_cparams("parallel"), name="moe_combine",
    )(y4, gates_t, h, g, b)


def _group_layout(cnt_tiles, nblk, bm):
    cnt = cnt_tiles.astype(jnp.int32)
    counts = jnp.sum(cnt, axis=1)
    padded = (counts + bm - 1) // bm * bm
    pad_end = jnp.cumsum(padded)
    pad_start = pad_end - padded
    base = pad_start[:, None] + jnp.cumsum(cnt, axis=1) - cnt
    starts = jnp.arange(nblk, dtype=jnp.int32) * bm
    block_e = jnp.minimum(jnp.sum((pad_end[None, :] <= starts[:, None]).astype(jnp.int32), axis=1),
                          N_EXPERTS - 1).astype(jnp.int32)
    n_used = (pad_end[-1] // bm).astype(jnp.int32).reshape(1)
    valid_end = (pad_start + counts).astype(jnp.int32)
    return base.T.astype(F32)[:, :, None], valid_end, pad_end.astype(jnp.int32), block_e, n_used


def _pack_w_in(w):
    o = 0
    xa, o = w[:, o:o + D_A], o + D_A
    ga, o = w[:, o:o + D_A], o + D_A
    vb, o = w[:, o:o + D_B], o + D_B
    gb, o = w[:, o:o + D_B], o + D_B
    q, o = w[:, o:o + D_QK_C], o + D_QK_C
    k, o = w[:, o:o + D_QK_C], o + D_QK_C
    v, o = w[:, o:o + D_C], o + D_C
    r, o = w[:, o:o + D_C], o + D_C
    z = w[:, o:o + GATE_RANK]
    zq = jnp.zeros((w.shape[0], PC_K - PC_Z - GATE_RANK), w.dtype)
    zk = jnp.zeros((w.shape[0], PC_V - PC_K - D_QK_C), w.dtype)
    return jnp.concatenate([xa, ga, vb, gb, q, z, zq, k, zk, v, r], axis=1).astype(BF16)


def _block_diag(w):
    eye = jnp.eye(N_BLK_A, dtype=w.dtype)
    return jnp.einsum("hij,hg->higj", w, eye).reshape(D_A, D_A)


def kernel(x_prompt, x_sample, state_conv_a, state_rglru, state_conv_b, state_gla, meta_tokens, ln0_g, ln0_b,
           w_in, conv_a_w, conv_a_b, w_rg, b_rg, w_ig, b_ig, lru_lambda, conv_b_w, conv_b_b, ln_b_g, ln_b_b,
           w_gate2, b_gate, gla_norm_g, w_out, ln1_g, ln1_b, router_w, router_b, moe_w1, moe_b1, moe_w2, moe_b2,
           ln2_g, ln2_b):
    bp, seq, _ = x_prompt.shape
    bs, dseq, _ = x_sample.shape
    depth = w_in.shape[0]
    tp = N_META + seq
    ts = SAMPLE_PAD_T
    np_rows = bp * tp
    ns_rows = bs * dseq
    n = np_rows + ns_rows
    alpha = (2 * depth) ** 0.25
    row = lambda a: a.reshape(1, -1)

    meta = jnp.broadcast_to(meta_tokens[None], (bp, N_META, D_MODEL))
    xp_in = jnp.concatenate([meta, x_prompt], axis=1).reshape(np_rows, D_MODEL)
    xs_in = jnp.pad(x_sample, ((0, 0), (0, ts - dseq), (0, 0))).reshape(bs * ts, D_MODEL)
    zeros_p = (jnp.zeros((bp, CONV_A - 1, D_A), F32), jnp.zeros((bp, 1, D_A), F32),
               jnp.zeros((bp, CONV_B - 1, D_B), F32), jnp.zeros((bp, N_HEADS_C, DK_C, DV_C), F32))
    sb = _pick(bs, (8, 4, 2, 1))

    new_p = ([], [], [], [])
    new_s = ([], [], [], [])
    x_all = None
    for l in range(depth):
        w_packed = _pack_w_in(w_in[l])
        wg = jnp.concatenate([_block_diag(w_rg[l]), _block_diag(w_ig[l])], axis=1).astype(BF16)
        bg = jnp.concatenate([b_rg[l], b_ig[l]]).reshape(1, -1)
        w_out_b = w_out[l].astype(BF16)
        ng = row(jnp.tile(gla_norm_g[l], N_HEADS_C))
        first = l == 0
        if first:
            pa_p, pb_p, pc_p, xn_p = _inproj(xp_in, 0, np_rows, row(ln0_g), row(ln0_b), w_packed, True)
            pa_s, pb_s, pc_s, xn_s = _inproj(xs_in, 0, bs * ts, row(ln0_g), row(ln0_b), w_packed, True)
            res_p, res_p_row0 = xn_p, 0
        else:
            pa_p, pb_p, pc_p = _inproj(x_all, 0, np_rows, row(ln0_g), row(ln0_b), w_packed, False)
            xn_s = jnp.pad(x_all[np_rows:].reshape(bs, dseq, D_MODEL),
                           ((0, 0), (0, ts - dseq), (0, 0))).reshape(bs * ts, D_MODEL)
            pa_s, pb_s, pc_s = _inproj(xn_s, 0, bs * ts, row(ln0_g), row(ln0_b), w_packed, False)
            res_p, res_p_row0 = x_all, 0

        outs = []
        for (pa, pb, pc, nb, t, tv, bb, st) in (
                (pa_p, pb_p, pc_p, bp, tp, tp, 1, zeros_p),
                (pa_s, pb_s, pc_s, bs, ts, dseq, sb,
                 (state_conv_a[l], state_rglru[l].reshape(bs, 1, D_A), state_conv_b[l], state_gla[l]))):
            ya, ca_new, h_last = _rglru(pa.reshape(nb, t, PA_W), st[0], st[1], conv_a_w[l], row(conv_a_b[l]),
                                        wg, bg, row(lru_lambda[l]), tv, bb)
            yb, cb_new = _convb(pb.reshape(nb, t, PB_W), st[2], conv_b_w[l], row(conv_b_b[l]),
                                row(ln_b_g[l]), row(ln_b_b[l]), tv, bb)
            yc, s_new = _gla(pc.reshape(nb, t, PC_W), st[3], w_gate2[l], row(b_gate[l]), ng, tv, bb)
            outs.append((ya.reshape(nb * t, D_A), yb.reshape(nb * t, D_B), yc.reshape(nb * t, D_C),
                         ca_new, h_last.reshape(nb, D_A), cb_new, s_new))
        (ya_p, yb_p, yc_p, *st_p), (ya_s, yb_s, yc_s, *st_s) = outs
        for j in range(4):
            new_p[j].append(st_p[j])
            new_s[j].append(st_s[j])

        h_p = _outproj(ya_p, yb_p, yc_p, res_p, res_p_row0, w_out_b, row(ln1_g[l]), row(ln1_b[l]), alpha)
        h_s = _outproj(ya_s, yb_s, yc_s, xn_s, 0, w_out_b, row(ln1_g[l]), row(ln1_b[l]), alpha)
        h_s = h_s.reshape(bs, ts, ROW_TILE, LANES)[:, :dseq].reshape(ns_rows * ROW_TILE, LANES)
        nblk = -(-(TOP_K * n) // MOE_BM) + N_EXPERTS
        h_pad = jnp.zeros(((nblk * MOE_BM // TOP_K - n) * ROW_TILE, LANES), F32)
        h_all = jnp.concatenate([h_p, h_s, h_pad], axis=0)

        top_idx, gates, cnt = _router(h_all, n, router_w[l].T, router_b[l].reshape(-1, 1))
        base, valid_end, pad_end, block_e, n_used = _group_layout(cnt[:, ::LANES], nblk, MOE_BM)
        dest = _dest(top_idx, base)
        inv = _invert(valid_end, pad_end, dest, nblk * MOE_BM)
        y4 = _moe(block_e, n_used, inv, h_all, moe_w1[l].astype(BF16),
                  moe_b1[l].reshape(N_EXPERTS, 1, -1), moe_w2[l].astype(BF16),
                  moe_b2[l].reshape(N_EXPERTS, 1, -1), MOE_BM)
        x_all = _combine(y4, gates.T, h_all, n, row(ln2_g[l]), row(ln2_b[l]), alpha)

    y_p = x_all[:np_rows].reshape(bp, tp, D_MODEL)[:, N_META:]
    y_s = x_all[np_rows:].reshape(bs, dseq, D_MODEL)
    return (y_p, y_s,
            jnp.stack(new_p[0]), jnp.stack(new_p[1]), jnp.stack(new_p[2]), jnp.stack(new_p[3]),
            jnp.stack(new_s[0]), jnp.stack(new_s[1]), jnp.stack(new_s[2]), jnp.stack(new_s[3]))
```

```python
import functools

import jax
import jax.numpy as jnp
from jax import lax
from jax.experimental import pallas as pl
from jax.experimental.pallas import tpu as pltpu

F32 = jnp.float32
BF16 = jnp.bfloat16

D_MODEL = 1024
N_META = 16
D_A = 384
D_B = 256
D_C = 384
N_BLK_A = 8
BLK_A = D_A // N_BLK_A
CONV_A = 4
RG_C = 8.0
CONV_B = 31
N_HEADS_C = 4
DV_C = D_C // N_HEADS_C
DK_C = DV_C // 2
D_QK_C = N_HEADS_C * DK_C
GATE_RANK = 16
GATE_TAU = 16.0
N_EXPERTS = 32
TOP_K = 4
D_FF = D_MODEL
SWIGLU_LIMIT = 7.0
SWIGLU_ALPHA = 1.702
LN_EPS = 1e-5

PA_W = 2 * D_A
PB_W = 2 * D_B
PC_Q, PC_Z, PC_K, PC_V, PC_R, PC_W = 0, 192, 256, 512, 896, 1280
P_W = PA_W + PB_W + PC_W

SUBLANES = 8
LANES = 128
ROW_TILE = D_MODEL // LANES
VMEM_LIMIT_BYTES = 56 * 1024 * 1024
MOE_BM = 256
FF_CHUNK = 256
DMA_GROUP = 8
W_CAST_ROWS = 128
SAMPLE_PAD_T = 8


def _cparams(*sem):
    return pltpu.CompilerParams(dimension_semantics=sem, vmem_limit_bytes=VMEM_LIMIT_BYTES)


def _sds(shape, dtype=F32):
    return jax.ShapeDtypeStruct(shape, dtype)


def _pick(n, prefs):
    for p in prefs:
        if n % p == 0:
            return p
    raise ValueError(f"no tile for {n} in {prefs}")


def _ln(x, g, b):
    mu = jnp.mean(x, axis=-1, keepdims=True)
    xc = x - mu
    var = jnp.mean(xc * xc, axis=-1, keepdims=True)
    return xc * lax.rsqrt(var + LN_EPS) * g + b


def _sigmoid(x):
    return 1.0 / (1.0 + jnp.exp(-x))


def _split_bf16(x):
    hi = x.astype(BF16)
    lo = (x - hi.astype(F32)).astype(BF16)
    return hi, lo


def _dot(a, b):
    return jnp.dot(a, b, preferred_element_type=F32)


def _inproj_kernel(x_ref, g_ref, b_ref, w_ref, pa_ref, pb_ref, pc_ref, *maybe_xn, apply_ln):
    x = x_ref[...]
    if apply_ln:
        x = _ln(x, g_ref[...], b_ref[...])
        maybe_xn[0][...] = x
    xb = x.astype(BF16)
    pa_ref[...] = _dot(xb, w_ref[:, 0:PA_W])
    pb_ref[...] = _dot(xb, w_ref[:, PA_W:PA_W + PB_W])
    pc_ref[...] = _dot(xb, w_ref[:, PA_W + PB_W:P_W])


def _inproj(x, row0, nrows, ln_g, ln_b, w_packed, apply_ln):
    tm = _pick(nrows, (384, 512, 256, 128, 64, 32, 16, 8))
    while row0 % tm:
        tm //= 2
    off = row0 // tm
    const = lambda i: (0, 0)
    row = lambda i: (i, 0)
    out_shape = [_sds((nrows, PA_W)), _sds((nrows, PB_W)), _sds((nrows, PC_W))]
    out_specs = [pl.BlockSpec((tm, PA_W), row), pl.BlockSpec((tm, PB_W), row), pl.BlockSpec((tm, PC_W), row)]
    if apply_ln:
        out_shape.append(_sds((nrows, D_MODEL)))
        out_specs.append(pl.BlockSpec((tm, D_MODEL), row))
    return pl.pallas_call(
        functools.partial(_inproj_kernel, apply_ln=apply_ln),
        grid=(nrows // tm,),
        in_specs=[pl.BlockSpec((tm, D_MODEL), lambda i: (i + off, 0)),
                  pl.BlockSpec((1, D_MODEL), const), pl.BlockSpec((1, D_MODEL), const),
                  pl.BlockSpec((D_MODEL, P_W), const)],
        out_specs=out_specs, out_shape=out_shape,
        compiler_params=_cparams("parallel"), name="inproj",
    )(x, ln_g, ln_b, w_packed)


def _rglru_kernel(p_ref, cbuf_ref, h0_ref, cw_ref, cb_ref, wg_ref, bg_ref, lam_ref,
                  y_ref, cnew_ref, hlast_ref, xp_scr, a_scr, h_scr, *, T, Tc, Tv, Bb):
    lam = lam_ref[...]
    softplus_neg = jnp.maximum(-lam, 0.0) + jnp.log1p(jnp.exp(-jnp.abs(lam)))
    c_decay = -RG_C * softplus_neg
    cw = cw_ref[...]
    cb = cb_ref[...]
    bg = bg_ref[...]
    sub = lax.broadcasted_iota(jnp.int32, (Tc, D_A), 0) % SUBLANES
    halo = SUBLANES - (CONV_A - 1)
    for b in range(Bb):
        xp_scr[halo:SUBLANES, :] = cbuf_ref[b]
        xp_scr[SUBLANES:SUBLANES + T, :] = p_ref[b, :, 0:D_A]
        cnew_ref[b] = xp_scr[halo + Tv:SUBLANES + Tv, :]

        def chunk(ci, h_b):
            r0 = pl.multiple_of(ci * Tc, SUBLANES)
            win = xp_scr[pl.ds(r0, Tc + SUBLANES), :]
            xc = cb + cw[0:1] * win[halo:halo + Tc]
            for j in range(1, CONV_A):
                xc = xc + cw[j:j + 1] * win[halo + j:halo + j + Tc]
            gates = _dot(xc.astype(BF16), wg_ref[...]) + bg
            r = _sigmoid(gates[:, 0:D_A])
            i = _sigmoid(gates[:, D_A:2 * D_A])
            log_a = c_decay * r
            a = jnp.exp(log_a)
            u = jnp.sqrt(1.0 - a * a) * (i * xc)
            for s in (1, 2, 4):
                keep = sub >= s
                a_prev = pltpu.roll(a, s, 0)
                u_prev = pltpu.roll(u, s, 0)
                u = jnp.where(keep, a * u_prev + u, u)
                a = jnp.where(keep, a * a_prev, a)
            a_scr[...] = a
            h_scr[pl.ds(r0, Tc), :] = u

            def group(gi, h_b):
                c0 = pl.multiple_of(gi * SUBLANES, SUBLANES)
                g0 = pl.multiple_of(r0 + gi * SUBLANES, SUBLANES)
                h8 = a_scr[pl.ds(c0, SUBLANES), :] * h_b + h_scr[pl.ds(g0, SUBLANES), :]
                h_scr[pl.ds(g0, SUBLANES), :] = h8
                return jnp.broadcast_to(h8[SUBLANES - 1:SUBLANES, :], (SUBLANES, D_A))

            h_b = lax.fori_loop(0, Tc // SUBLANES, group, h_b)
            ga = p_ref[b, pl.ds(r0, Tc), D_A:2 * D_A]
            gelu = 0.5 * ga * (1.0 + jnp.tanh(0.7978845608028654 * (ga + 0.044715 * ga * ga * ga)))
            y_ref[b, pl.ds(r0, Tc), :] = h_scr[pl.ds(r0, Tc), :] * gelu
            return h_b

        h_b = jnp.broadcast_to(h0_ref[b], (SUBLANES, D_A))
        lax.fori_loop(0, T // Tc, chunk, h_b)
        hlast_ref[b] = h_scr[Tv - 1:Tv, :]


def _rglru(pa3, cbuf, h0, cw, cb, wg, bg, lam, Tv, Bb):
    B, T, _ = pa3.shape
    Tc = _pick(T, (344, 256, 128, 64, 48, 32, 16, 8))
    const2 = lambda i: (0, 0)
    seq3 = lambda i: (i, 0, 0)
    return pl.pallas_call(
        functools.partial(_rglru_kernel, T=T, Tc=Tc, Tv=Tv, Bb=Bb),
        grid=(B // Bb,),
        in_specs=[pl.BlockSpec((Bb, T, PA_W), seq3), pl.BlockSpec((Bb, CONV_A - 1, D_A), seq3),
                  pl.BlockSpec((Bb, 1, D_A), seq3), pl.BlockSpec((CONV_A, D_A), const2),
                  pl.BlockSpec((1, D_A), const2), pl.BlockSpec((D_A, 2 * D_A), const2),
                  pl.BlockSpec((1, 2 * D_A), const2), pl.BlockSpec((1, D_A), const2)],
        out_specs=[pl.BlockSpec((Bb, T, D_A), seq3), pl.BlockSpec((Bb, CONV_A - 1, D_A), seq3),
                   pl.BlockSpec((Bb, 1, D_A), seq3)],
        out_shape=[_sds((B, T, D_A)), _sds((B, CONV_A - 1, D_A)), _sds((B, 1, D_A))],
        scratch_shapes=[pltpu.VMEM((T + 2 * SUBLANES, D_A), F32), pltpu.VMEM((Tc, D_A), F32),
                        pltpu.VMEM((T, D_A), F32)],
        compiler_params=_cparams("parallel"), name="rglru",
    )(pa3, cbuf, h0, cw, cb, wg, bg, lam)


B_HALO = 32


def _convb_kernel(p_ref, buf_ref, w_ref, cb_ref, g_ref, b_ref, y_ref, bnew_ref, u_scr, *, T, Tc, Tv, Bb):
    w = w_ref[...]
    cb = cb_ref[...]
    g = g_ref[...]
    bb = b_ref[...]
    first = B_HALO - (CONV_B - 1)
    for b in range(Bb):
        u_scr[0:first, :] = jnp.zeros((first, D_B), F32)
        u_scr[first:B_HALO, :] = buf_ref[b]
        u_scr[B_HALO:B_HALO + T, :] = p_ref[b, :, 0:D_B] * _sigmoid(p_ref[b, :, D_B:2 * D_B])
        bnew_ref[b] = u_scr[first + Tv:B_HALO + Tv, :]

        def chunk(ci, carry):
            r0 = pl.multiple_of(ci * Tc, SUBLANES)
            win = u_scr[pl.ds(r0, Tc + B_HALO), :]
            acc = cb + w[0:1] * win[first:first + Tc]
            for j in range(1, CONV_B):
                acc = acc + w[j:j + 1] * win[first + j:first + j + Tc]
            yn = _ln(acc, g, bb)
            y_ref[b, pl.ds(r0, Tc), :] = yn * _sigmoid(yn)
            return carry

        lax.fori_loop(0, T // Tc, chunk, 0)


def _convb(pb3, buf, w, cb, g, b, Tv, Bb):
    B, T, _ = pb3.shape
    Tc = _pick(T, (48, 32, 16, 8))
    const2 = lambda i: (0, 0)
    seq3 = lambda i: (i, 0, 0)
    return pl.pallas_call(
        functools.partial(_convb_kernel, T=T, Tc=Tc, Tv=Tv, Bb=Bb),
        grid=(B // Bb,),
        in_specs=[pl.BlockSpec((Bb, T, PB_W), seq3), pl.BlockSpec((Bb, CONV_B - 1, D_B), seq3),
                  pl.BlockSpec((CONV_B, D_B), const2), pl.BlockSpec((1, D_B), const2),
                  pl.BlockSpec((1, D_B), const2), pl.BlockSpec((1, D_B), const2)],
        out_specs=[pl.BlockSpec((Bb, T, D_B), seq3), pl.BlockSpec((Bb, CONV_B - 1, D_B), seq3)],
        out_shape=[_sds((B, T, D_B)), _sds((B, CONV_B - 1, D_B))],
        scratch_shapes=[pltpu.VMEM((T + B_HALO, D_B), F32)],
        compiler_params=_cparams("parallel"), name="convb",
    )(pb3, buf, w, cb, g, b)


def _gla_kernel(p_ref, s0_ref, wg2_ref, bgate_ref, ng_ref, y_ref, snew_ref, s_scr, g_scr, *, T, C, Tb, Tv, Bb):
    ri = lax.broadcasted_iota(jnp.int32, (C, C), 0)
    ci_ = lax.broadcasted_iota(jnp.int32, (C, C), 1)
    tril = ri >= ci_
    lane_k = lax.broadcasted_iota(jnp.int32, (1, D_QK_C), 1)
    lane_v = lax.broadcasted_iota(jnp.int32, (1, D_C), 1)
    hm_k = [(lane_k >= h * DK_C) & (lane_k < (h + 1) * DK_C) for h in range(N_HEADS_C)]
    hm_v = [(lane_v >= h * DV_C) & (lane_v < (h + 1) * DV_C) for h in range(N_HEADS_C)]
    rs = lax.broadcasted_iota(jnp.int32, (D_C, D_QK_C), 0)
    cs = lax.broadcasted_iota(jnp.int32, (D_C, D_QK_C), 1)
    bd_t = (rs >= 0) & (rs < 0)
    for h in range(N_HEADS_C):
        bd_t = bd_t | ((rs >= h * DV_C) & (rs < (h + 1) * DV_C) & (cs >= h * DK_C) & (cs < (h + 1) * DK_C))
    rm = lax.broadcasted_iota(jnp.int32, (D_C, D_C), 0)
    cm = lax.broadcasted_iota(jnp.int32, (D_C, D_C), 1)
    seg = (rm >= 0) & (rm < 0)
    for h in range(N_HEADS_C):
        seg = seg | ((rm >= h * DV_C) & (rm < (h + 1) * DV_C) & (cm >= h * DV_C) & (cm < (h + 1) * DV_C))
    mseg = jnp.where(seg, 1.0, 0.0).astype(BF16)
    wg2 = wg2_ref[...].astype(BF16)
    bgate = bgate_ref[...]
    ng = ng_ref[...]
    rowi = lax.broadcasted_iota(jnp.int32, (C, 1), 0)
    tdims = (((0,), (0,)), ((), ()))

    n_chunks = T // C
    nt_dims = (((1,), (1,)), ((), ()))
    tril4 = jnp.concatenate([tril] * N_HEADS_C, axis=0)
    scan_shifts = [s for s in (1, 2, 4, 8, 16, 32) if s < C]
    scan_keep = [rowi >= s for s in scan_shifts]

    def tile_rows(ti, tb):
        r0 = ti * tb
        return pl.ds(r0 if isinstance(ti, int) else pl.multiple_of(r0, tb), tb)

    for b in range(Bb):
        def gates(ti, carry):
            rows = tile_rows(ti, Tb)
            z = p_ref[b, rows, PC_Z:PC_Z + GATE_RANK]
            pre = _dot(z.astype(BF16), wg2) + bgate
            g = (jnp.minimum(pre, 0.0) - jnp.log1p(jnp.exp(-jnp.abs(pre)))) * (1.0 / GATE_TAU)
            rid = ti * Tb + lax.broadcasted_iota(jnp.int32, (Tb, 1), 0)
            g_scr[rows, :] = jnp.where(rid < Tv, g, 0.0)
            return carry
        lax.fori_loop(0, T // Tb, gates, 0)

        def chunk(ci, carry):
            rows = tile_rows(ci, C)
            q = p_ref[b, rows, PC_Q:PC_Q + D_QK_C] * (DK_C ** -0.5)
            k = p_ref[b, rows, PC_K:PC_K + D_QK_C]
            v = p_ref[b, rows, PC_V:PC_V + D_C]
            k = jnp.where((ci * C + rowi) < Tv, k, 0.0)
            gcum = g_scr[rows, :]
            for s, keep in zip(scan_shifts, scan_keep):
                gcum = gcum + jnp.where(keep, pltpu.roll(gcum, s, 0), 0.0)
            g_last = gcum[C - 1:C, :]
            g_mid = gcum[C // 2 - 1:C // 2, :]
            vb = v.astype(BF16)
            qt = q * jnp.exp(gcum - g_mid)
            ktb = (k * jnp.exp(g_mid - gcum)).astype(BF16)
            q4 = jnp.concatenate([jnp.where(hm_k[h], qt, 0.0) for h in range(N_HEADS_C)], axis=0).astype(BF16)
            sc = lax.dot_general(q4, ktb, nt_dims, preferred_element_type=F32)
            r4 = _dot(jnp.where(tril4, sc, 0.0).astype(BF16), vb)
            o = jnp.where(hm_v[0], r4[0:C], 0.0)
            for h in range(1, N_HEADS_C):
                o = o + jnp.where(hm_v[h], r4[h * C:(h + 1) * C], 0.0)
            kd = (k * jnp.exp(g_last - gcum)).astype(BF16)
            upd_t = lax.dot_general(vb, kd, tdims, preferred_element_type=F32)
            s_in = s_scr[...]
            o = o + lax.dot_general((q * jnp.exp(gcum)).astype(BF16), s_in.astype(BF16), nt_dims,
                                    preferred_element_type=F32)
            s_scr[...] = s_in * jnp.exp(g_last) + jnp.where(bd_t, upd_t, 0.0)
            y_ref[b, rows, :] = o
            return carry

        s_scr[...] = jnp.zeros((D_C, D_QK_C), F32)
        for h in range(N_HEADS_C):
            s_scr[h * DV_C:(h + 1) * DV_C, h * DK_C:(h + 1) * DK_C] = s0_ref[b, h]
        lax.fori_loop(0, n_chunks, chunk, 0)
        for h in range(N_HEADS_C):
            snew_ref[b, h] = s_scr[h * DV_C:(h + 1) * DV_C, h * DK_C:(h + 1) * DK_C]

        def finish(ti, carry):
            rows = tile_rows(ti, Tb)
            o = y_ref[b, rows, :]
            rg = p_ref[b, rows, PC_R:PC_R + D_C]
            o2_hi, o2_lo = _split_bf16(o * o)
            ms = (_dot(o2_hi, mseg) + _dot(o2_lo, mseg)) * (1.0 / DV_C)
            y_ref[b, rows, :] = o * lax.rsqrt(ms + LN_EPS) * ng * (rg * _sigmoid(rg))
            return carry
        lax.fori_loop(0, T // Tb, finish, 0)


def _gla(pc3, s0, wg2, bgate, ng, Tv, Bb):
    B, T, _ = pc3.shape
    C = _pick(T, (48, 32, 16, 8))
    Tb = _pick(T, (344, 256, 128, 64, 48, 32, 16, 8))
    const2 = lambda i: (0, 0)
    seq3 = lambda i: (i, 0, 0)
    seq4 = lambda i: (i, 0, 0, 0)
    st = (Bb, N_HEADS_C, DV_C, DK_C)
    y, s_new_t = pl.pallas_call(
        functools.partial(_gla_kernel, T=T, C=C, Tb=Tb, Tv=Tv, Bb=Bb),
        grid=(B // Bb,),
        in_specs=[pl.BlockSpec((Bb, T, PC_W), seq3), pl.BlockSpec(st, seq4),
                  pl.BlockSpec((GATE_RANK, D_QK_C), const2), pl.BlockSpec((1, D_QK_C), const2),
                  pl.BlockSpec((1, D_C), const2)],
        out_specs=[pl.BlockSpec((Bb, T, D_C), seq3), pl.BlockSpec(st, seq4)],
        out_shape=[_sds((B, T, D_C)), _sds((B, N_HEADS_C, DV_C, DK_C))],
        scratch_shapes=[pltpu.VMEM((D_C, D_QK_C), F32), pltpu.VMEM((T, D_QK_C), F32)],
        compiler_params=_cparams("parallel"), name="gla",
    )(pc3, jnp.swapaxes(s0, 2, 3), wg2, bgate, ng)
    return y, jnp.swapaxes(s_new_t, 2, 3)


def _tiles_to_rows(ref, lead, start, rows, stride):
    return jnp.concatenate([ref[lead + (pl.ds(start + s, rows, stride=stride), slice(None))]
                            for s in range(ROW_TILE)], axis=1)


def _rows_to_tiles(ref, lead, rows, val):
    for s in range(ROW_TILE):
        ref[lead + (pl.ds(s, rows, stride=ROW_TILE), slice(None))] = val[:, s * LANES:(s + 1) * LANES]


def _outproj_kernel(ya_ref, yb_ref, yc_ref, x_ref, w_ref, g_ref, b_ref, h_ref, *, alpha):
    y = (_dot(ya_ref[...].astype(BF16), w_ref[0:D_A, :])
         + _dot(yb_ref[...].astype(BF16), w_ref[D_A:D_A + D_B, :])
         + _dot(yc_ref[...].astype(BF16), w_ref[D_A + D_B:D_MODEL, :]))
    h = _ln(alpha * x_ref[...] + y, g_ref[...], b_ref[...])
    _rows_to_tiles(h_ref, (), h.shape[0], h)


def _outproj(ya, yb, yc, x, row0, w, g, b, alpha):
    n = ya.shape[0]
    tm = _pick(n, (384, 512, 256, 128, 64, 32, 16, 8))
    while row0 % tm:
        tm //= 2
    off = row0 // tm
    row = lambda i: (i, 0)
    const = lambda i: (0, 0)
    return pl.pallas_call(
        functools.partial(_outproj_kernel, alpha=alpha),
        grid=(n // tm,),
        in_specs=[pl.BlockSpec((tm, D_A), row), pl.BlockSpec((tm, D_B), row), pl.BlockSpec((tm, D_C), row),
                  pl.BlockSpec((tm, D_MODEL), lambda i: (i + off, 0)),
                  pl.BlockSpec((D_MODEL, D_MODEL), const), pl.BlockSpec((1, D_MODEL), const),
                  pl.BlockSpec((1, D_MODEL), const)],
        out_specs=pl.BlockSpec((tm * ROW_TILE, LANES), row), out_shape=_sds((n * ROW_TILE, LANES)),
        compiler_params=_cparams("parallel"), name="outproj",
    )(ya, yb, yc, x, w, g, b)


def _router_kernel(h_ref, wt_ref, b_ref, idx_ref, gate_ref, cnt_ref):
    nt = (((1,), (1,)), ((), ()))
    hh, hl = _split_bf16(_tiles_to_rows(h_ref, (), 0, idx_ref.shape[1], ROW_TILE))
    wh, wl = _split_bf16(wt_ref[...])
    logits = (lax.dot_general(wh, hh, nt, preferred_element_type=F32)
              + lax.dot_general(wh, hl, nt, preferred_element_type=F32)
              + lax.dot_general(wl, hh, nt, preferred_element_type=F32)) + b_ref[...]
    eid = lax.broadcasted_iota(jnp.int32, logits.shape, 0)
    vals = []
    member = jnp.zeros(logits.shape, F32)
    for k in range(TOP_K):
        m = jnp.max(logits, axis=0, keepdims=True)
        sel = jnp.min(jnp.where(logits == m, eid, N_EXPERTS), axis=0, keepdims=True)
        idx_ref[k:k + 1, :] = sel
        vals.append(m)
        hit = eid == sel
        member = jnp.where(hit, 1.0, member)
        logits = jnp.where(hit, -jnp.inf, logits)
    es = [jnp.exp(v - vals[0]) for v in vals]
    tot = es[0] + es[1] + es[2] + es[3]
    for k in range(TOP_K):
        gate_ref[k:k + 1, :] = es[k] / tot
    cnt_ref[...] = jnp.broadcast_to(jnp.sum(member, axis=1, keepdims=True), cnt_ref.shape)


def _router_tile(n):
    return _pick(n, (896, 640, 512, 384, 256, 128))


def _router(h, n, wt, b):
    tm = _router_tile(n)
    nt = n // tm
    return pl.pallas_call(
        _router_kernel,
        grid=(nt,),
        in_specs=[pl.BlockSpec((tm * ROW_TILE, LANES), lambda i: (i, 0)),
                  pl.BlockSpec((N_EXPERTS, D_MODEL), lambda i: (0, 0)),
                  pl.BlockSpec((N_EXPERTS, 1), lambda i: (0, 0))],
        out_specs=[pl.BlockSpec((TOP_K, tm), lambda i: (0, i)), pl.BlockSpec((TOP_K, tm), lambda i: (0, i)),
                   pl.BlockSpec((N_EXPERTS, LANES), lambda i: (0, i))],
        out_shape=[_sds((TOP_K, n), jnp.int32), _sds((TOP_K, n)), _sds((N_EXPERTS, nt * LANES))],
        compiler_params=_cparams("parallel"), name="router",
    )(h, wt, b)


def _dest_kernel(idx_ref, base_ref, dest_ref):
    tm = idx_ref.shape[1]
    eid = lax.broadcasted_iota(jnp.int32, (N_EXPERTS, tm), 0)
    hits = [eid == idx_ref[k:k + 1, :] for k in range(TOP_K)]
    member = jnp.zeros((N_EXPERTS, tm), F32)
    for k in range(TOP_K):
        member = jnp.where(hits[k], 1.0, member)
    earlier = (lax.broadcasted_iota(jnp.int32, (tm, tm), 0) < lax.broadcasted_iota(jnp.int32, (tm, tm), 1))
    rank = _dot(member.astype(BF16), jnp.where(earlier, 1.0, 0.0).astype(BF16))
    pos = base_ref[...] + rank
    for k in range(TOP_K):
        dest_ref[k:k + 1, :] = jnp.sum(jnp.where(hits[k], pos, 0.0), axis=0, keepdims=True).astype(jnp.int32)


def _dest(top_idx, base):
    n = top_idx.shape[1]
    tm = _router_tile(n)
    return pl.pallas_call(
        _dest_kernel,
        grid=(n // tm,),
        in_specs=[pl.BlockSpec((TOP_K, tm), lambda i: (0, i)),
                  pl.BlockSpec((None, N_EXPERTS, 1), lambda i: (i, 0, 0))],
        out_specs=pl.BlockSpec((TOP_K, tm), lambda i: (0, i)),
        out_shape=_sds((TOP_K, n), jnp.int32),
        compiler_params=_cparams("parallel"), name="moe_dest",
    )(top_idx, base)


def _invert_kernel(vend_ref, pend_ref, dest_hbm, inv_ref, buf, sem, *, n, rc, n_rows):
    rows_per_slot = n // LANES
    chunks_per_slot = rows_per_slot // rc

    def chunk(c, carry):
        cp = pltpu.make_async_copy(dest_hbm.at[pl.ds(c * rc, rc), :], buf, sem.at[0])
        cp.start()
        cp.wait()
        k = c // chunks_per_slot
        t0 = (c - k * chunks_per_slot) * (rc * LANES)

        def row(r, carry):
            for l in range(LANES):
                inv_ref[buf[r, l]] = (t0 + r * LANES + l) * TOP_K + k
            return carry

        return lax.fori_loop(0, rc, row, carry)

    lax.fori_loop(0, TOP_K * chunks_per_slot, chunk, 0)

    def fill(lo, hi, ctr):
        def body(r, ctr):
            inv_ref[r] = ctr
            return ctr + 1
        return lax.fori_loop(lo, hi, body, ctr)

    ctr = jnp.int32(TOP_K * n)
    for e in range(N_EXPERTS):
        ctr = fill(vend_ref[e], pend_ref[e], ctr)
    fill(pend_ref[N_EXPERTS - 1], n_rows, ctr)


def _invert(valid_end, pad_end, dest, n_rows):
    k, n = dest.shape
    rc = _pick(n // LANES, (19, 16, 8, 5, 4, 2, 1))
    return pl.pallas_call(
        functools.partial(_invert_kernel, n=n, rc=rc, n_rows=n_rows),
        grid_spec=pltpu.PrefetchScalarGridSpec(
            num_scalar_prefetch=2, grid=(1,),
            in_specs=[pl.BlockSpec(memory_space=pl.ANY)],
            out_specs=pl.BlockSpec(memory_space=pltpu.SMEM),
            scratch_shapes=[pltpu.SMEM((rc, LANES), jnp.int32), pltpu.SemaphoreType.DMA((1,))]),
        out_shape=_sds((n_rows,), jnp.int32),
        compiler_params=_cparams("arbitrary"), name="moe_invert",
    )(valid_end, pad_end, dest.reshape(k * n // LANES, LANES))


def _moe_kernel(be_ref, nu_ref, inv_ref, h_hbm, w1f_ref, b1_ref, w2f_ref, b2_ref, y4_hbm,
                xbuf, xb, obuf, w1_ref, w2_ref, gsem, ssem, *, bm):
    i = pl.program_id(0)
    n_used = nu_ref[0]
    last_blk = pl.num_programs(0) - 1
    n_ff = D_FF // FF_CHUNK
    per = bm // n_ff

    def tile(r):
        start = r * ROW_TILE
        return pl.ds(start if isinstance(r, int) else pl.multiple_of(start, ROW_TILE), ROW_TILE)

    def gather_row(blk, slot, j, r=None):
        r = inv_ref[blk * bm + j] if r is None else r
        t = lax.shift_right_logical(r, 2)
        pltpu.make_async_copy(h_hbm.at[tile(t), :], xbuf.at[slot, tile(j), :], gsem.at[slot]).start()

    def scatter_row(blk, slot, j, r=None):
        r = inv_ref[blk * bm + j] if r is None else r
        pltpu.make_async_copy(obuf.at[slot, tile(j), :], y4_hbm.at[tile(r), :], ssem.at[slot]).start()

    def wait_gather(slot):
        pltpu.make_async_copy(h_hbm.at[pl.ds(0, bm * ROW_TILE), :], xbuf.at[slot], gsem.at[slot]).wait()

    def wait_scatter(slot):
        pltpu.make_async_copy(obuf.at[slot], y4_hbm.at[pl.ds(0, bm * ROW_TILE), :], ssem.at[slot]).wait()

    def loop_rows(fn, blk, slot):
        def body(j, c):
            fn(blk, slot, j)
            return c
        lax.fori_loop(0, bm, body, 0)

    def step(with_scatter):
        slot = i % 2
        other = 1 - slot
        nxt = jnp.minimum(i + 1, last_blk)
        wait_gather(slot)
        xb[...] = _tiles_to_rows(xbuf, (slot,), 0, bm, ROW_TILE).astype(BF16)
        acc = jnp.zeros((bm, D_MODEL), F32)
        for c in range(n_ff):
            for j0 in range(c * per, (c + 1) * per, DMA_GROUP):
                js = range(j0, j0 + DMA_GROUP)
                g_ids = [inv_ref[nxt * bm + j] for j in js]
                s_ids = [inv_ref[(i - 1) * bm + j] for j in js] if with_scatter else []
                for j, r in zip(js, g_ids):
                    gather_row(nxt, other, j, r)
                for j, r in zip(js, s_ids):
                    scatter_row(i - 1, other, j, r)
            lo = c * FF_CHUNK
            x = xb[...]
            hg = _dot(x, w1_ref[:, lo:lo + FF_CHUNK]) + b1_ref[:, lo:lo + FF_CHUNK]
            hl = _dot(x, w1_ref[:, D_FF + lo:D_FF + lo + FF_CHUNK]) + b1_ref[:, D_FF + lo:D_FF + lo + FF_CHUNK]
            gate = jnp.minimum(hg, SWIGLU_LIMIT)
            lin = jnp.clip(hl, -SWIGLU_LIMIT, SWIGLU_LIMIT)
            act = gate * _sigmoid(SWIGLU_ALPHA * gate) * (lin + 1.0)
            acc = acc + _dot(act.astype(BF16), w2_ref[lo:lo + FF_CHUNK, :])

        @pl.when(i >= 2)
        def _():
            wait_scatter(slot)
        _rows_to_tiles(obuf, (slot,), bm, acc + b2_ref[...])

    new_expert = jnp.logical_or(i == 0, be_ref[i] != be_ref[jnp.maximum(i - 1, 0)])

    @pl.when(jnp.logical_and(new_expert, i < n_used))
    def _():
        def cast_rows(r, c):
            rows = pl.ds(pl.multiple_of(r * W_CAST_ROWS, W_CAST_ROWS), W_CAST_ROWS)
            w1_ref[rows, :] = w1f_ref[rows, :].astype(BF16)
            w2_ref[rows, :] = w2f_ref[rows, :].astype(BF16)
            return c
        lax.fori_loop(0, D_MODEL // W_CAST_ROWS, cast_rows, 0)

    @pl.when(jnp.logical_and(i == 0, n_used > 0))
    def _():
        loop_rows(gather_row, 0, 0)
        step(False)

    @pl.when(jnp.logical_and(i > 0, i < n_used))
    def _():
        step(True)

    @pl.when(i == n_used - 1)
    def _():
        slot = i % 2
        loop_rows(scatter_row, i, slot)
        wait_scatter(slot)

        @pl.when(i >= 1)
        def _():
            wait_scatter(1 - slot)
        wait_gather(1 - slot)
        obuf[0] = jnp.zeros((bm * ROW_TILE, LANES), F32)

        def zero_block(blk, c):
            rows = bm * ROW_TILE
            cp = pltpu.make_async_copy(obuf.at[0], y4_hbm.at[pl.ds(pl.multiple_of(blk * rows, rows), rows), :],
                                       ssem.at[0])
            cp.start()
            cp.wait()
            return c
        lax.fori_loop(n_used, last_blk + 1, zero_block, 0)


def _moe(block_e, n_used, inv, h, w1, b1, w2, b2, bm):
    nblk = block_e.shape[0]
    wmap = lambda i, be, nu, iv: (be[i], 0, 0)
    return pl.pallas_call(
        functools.partial(_moe_kernel, bm=bm),
        grid_spec=pltpu.PrefetchScalarGridSpec(
            num_scalar_prefetch=3, grid=(nblk,),
            in_specs=[pl.BlockSpec(memory_space=pl.ANY),
                      pl.BlockSpec((None, D_MODEL, 2 * D_FF), wmap), pl.BlockSpec((None, 1, 2 * D_FF), wmap),
                      pl.BlockSpec((None, D_FF, D_MODEL), wmap), pl.BlockSpec((None, 1, D_MODEL), wmap)],
            out_specs=pl.BlockSpec(memory_space=pl.ANY),
            scratch_shapes=[pltpu.VMEM((2, bm * ROW_TILE, LANES), F32), pltpu.VMEM((bm, D_MODEL), BF16),
                            pltpu.VMEM((2, bm * ROW_TILE, LANES), F32),
                            pltpu.VMEM((D_MODEL, 2 * D_FF), BF16), pltpu.VMEM((D_FF, D_MODEL), BF16),
                            pltpu.SemaphoreType.DMA((2,)), pltpu.SemaphoreType.DMA((2,))]),
        out_shape=_sds((nblk * bm * ROW_TILE, LANES)),
        compiler_params=_cparams("arbitrary"), name="moe_experts",
    )(block_e, n_used, inv, h, w1, b1, w2, b2)


def _combine_kernel(y4_ref, gates_ref, h_ref, g_ref, b_ref, out_ref, *, alpha):
    tm = out_ref.shape[0]
    gates = gates_ref[...]
    acc = alpha * _tiles_to_rows(h_ref, (), 0, tm, ROW_TILE)
    for k in range(TOP_K):
        acc = acc + _tiles_to_rows(y4_ref, (), k * ROW_TILE, tm, TOP_K * ROW_TILE) * gates[:, k:k + 1]
    out_ref[...] = _ln(acc, g_ref[...], b_ref[...])


def _combine(y4, gates_t, h, n, g, b, alpha):
    tm = _pick(n, (224, 128, 64, 32, 16, 8))
    row = lambda i: (i, 0)
    const = lambda i: (0, 0)
    return pl.pallas_call(
        functools.partial(_combine_kernel, alpha=alpha),
        grid=(n // tm,),
        in_specs=[pl.BlockSpec((TOP_K * tm * ROW_TILE, LANES), row), pl.BlockSpec((tm, TOP_K), row),
                  pl.BlockSpec((tm * ROW_TILE, LANES), row),
                  pl.BlockSpec((1, D_MODEL), const), pl.BlockSpec((1, D_MODEL), const)],
        out_specs=pl.BlockSpec((tm, D_MODEL), row),
        out_shape=_sds((n, D_MODEL)),
        compiler_params=_cparams("parallel"), name="moe_combine",
    )(y4, gates_t, h, g, b)


def _group_layout(cnt_tiles, nblk, bm):
    cnt = cnt_tiles.astype(jnp.int32)
    counts = jnp.sum(cnt, axis=1)
    padded = (counts + bm - 1) // bm * bm
    pad_end = jnp.cumsum(padded)
    pad_start = pad_end - padded
    base = pad_start[:, None] + jnp.cumsum(cnt, axis=1) - cnt
    starts = jnp.arange(nblk, dtype=jnp.int32) * bm
    block_e = jnp.minimum(jnp.sum((pad_end[None, :] <= starts[:, None]).astype(jnp.int32), axis=1),
                          N_EXPERTS - 1).astype(jnp.int32)
    n_used = (pad_end[-1] // bm).astype(jnp.int32).reshape(1)
    valid_end = (pad_start + counts).astype(jnp.int32)
    return base.T.astype(F32)[:, :, None], valid_end, pad_end.astype(jnp.int32), block_e, n_used


def _pack_w_in(w):
    o = 0
    xa, o = w[:, o:o + D_A], o + D_A
    ga, o = w[:, o:o + D_A], o + D_A
    vb, o = w[:, o:o + D_B], o + D_B
    gb, o = w[:, o:o + D_B], o + D_B
    q, o = w[:, o:o + D_QK_C], o + D_QK_C
    k, o = w[:, o:o + D_QK_C], o + D_QK_C
    v, o = w[:, o:o + D_C], o + D_C
    r, o = w[:, o:o + D_C], o + D_C
    z = w[:, o:o + GATE_RANK]
    zq = jnp.zeros((w.shape[0], PC_K - PC_Z - GATE_RANK), w.dtype)
    zk = jnp.zeros((w.shape[0], PC_V - PC_K - D_QK_C), w.dtype)
    return jnp.concatenate([xa, ga, vb, gb, q, z, zq, k, zk, v, r], axis=1).astype(BF16)


def _block_diag(w):
    eye = jnp.eye(N_BLK_A, dtype=w.dtype)
    return jnp.einsum("hij,hg->higj", w, eye).reshape(D_A, D_A)


def kernel(x_prompt, x_sample, state_conv_a, state_rglru, state_conv_b, state_gla, meta_tokens, ln0_g, ln0_b,
           w_in, conv_a_w, conv_a_b, w_rg, b_rg, w_ig, b_ig, lru_lambda, conv_b_w, conv_b_b, ln_b_g, ln_b_b,
           w_gate2, b_gate, gla_norm_g, w_out, ln1_g, ln1_b, router_w, router_b, moe_w1, moe_b1, moe_w2, moe_b2,
           ln2_g, ln2_b):
    bp, seq, _ = x_prompt.shape
    bs, dseq, _ = x_sample.shape
    depth = w_in.shape[0]
    tp = N_META + seq
    ts = SAMPLE_PAD_T
    np_rows = bp * tp
    ns_rows = bs * dseq
    n = np_rows + ns_rows
    alpha = (2 * depth) ** 0.25
    row = lambda a: a.reshape(1, -1)

    meta = jnp.broadcast_to(meta_tokens[None], (bp, N_META, D_MODEL))
    xp_in = jnp.concatenate([meta, x_prompt], axis=1).reshape(np_rows, D_MODEL)
    xs_in = jnp.pad(x_sample, ((0, 0), (0, ts - dseq), (0, 0))).reshape(bs * ts, D_MODEL)
    zeros_p = (jnp.zeros((bp, CONV_A - 1, D_A), F32), jnp.zeros((bp, 1, D_A), F32),
               jnp.zeros((bp, CONV_B - 1, D_B), F32), jnp.zeros((bp, N_HEADS_C, DK_C, DV_C), F32))
    sb = _pick(bs, (8, 4, 2, 1))

    new_p = ([], [], [], [])
    new_s = ([], [], [], [])
    x_all = None
    for l in range(depth):
        w_packed = _pack_w_in(w_in[l])
        wg = jnp.concatenate([_block_diag(w_rg[l]), _block_diag(w_ig[l])], axis=1).astype(BF16)
        bg = jnp.concatenate([b_rg[l], b_ig[l]]).reshape(1, -1)
        w_out_b = w_out[l].astype(BF16)
        ng = row(jnp.tile(gla_norm_g[l], N_HEADS_C))
        first = l == 0
        if first:
            pa_p, pb_p, pc_p, xn_p = _inproj(xp_in, 0, np_rows, row(ln0_g), row(ln0_b), w_packed, True)
            pa_s, pb_s, pc_s, xn_s = _inproj(xs_in, 0, bs * ts, row(ln0_g), row(ln0_b), w_packed, True)
            res_p, res_p_row0 = xn_p, 0
        else:
            pa_p, pb_p, pc_p = _inproj(x_all, 0, np_rows, row(ln0_g), row(ln0_b), w_packed, False)
            xn_s = jnp.pad(x_all[np_rows:].reshape(bs, dseq, D_MODEL),
                           ((0, 0), (0, ts - dseq), (0, 0))).reshape(bs * ts, D_MODEL)
            pa_s, pb_s, pc_s = _inproj(xn_s, 0, bs * ts, row(ln0_g), row(ln0_b), w_packed, False)
            res_p, res_p_row0 = x_all, 0

        outs = []
        for (pa, pb, pc, nb, t, tv, bb, st) in (
                (pa_p, pb_p, pc_p, bp, tp, tp, 1, zeros_p),
                (pa_s, pb_s, pc_s, bs, ts, dseq, sb,
                 (state_conv_a[l], state_rglru[l].reshape(bs, 1, D_A), state_conv_b[l], state_gla[l]))):
            ya, ca_new, h_last = _rglru(pa.reshape(nb, t, PA_W), st[0], st[1], conv_a_w[l], row(conv_a_b[l]),
                                        wg, bg, row(lru_lambda[l]), tv, bb)
            yb, cb_new = _convb(pb.reshape(nb, t, PB_W), st[2], conv_b_w[l], row(conv_b_b[l]),
                                row(ln_b_g[l]), row(ln_b_b[l]), tv, bb)
            yc, s_new = _gla(pc.reshape(nb, t, PC_W), st[3], w_gate2[l], row(b_gate[l]), ng, tv, bb)
            outs.append((ya.reshape(nb * t, D_A), yb.reshape(nb * t, D_B), yc.reshape(nb * t, D_C),
                         ca_new, h_last.reshape(nb, D_A), cb_new, s_new))
        (ya_p, yb_p, yc_p, *st_p), (ya_s, yb_s, yc_s, *st_s) = outs
        for j in range(4):
            new_p[j].append(st_p[j])
            new_s[j].append(st_s[j])

        h_p = _outproj(ya_p, yb_p, yc_p, res_p, res_p_row0, w_out_b, row(ln1_g[l]), row(ln1_b[l]), alpha)
        h_s = _outproj(ya_s, yb_s, yc_s, xn_s, 0, w_out_b, row(ln1_g[l]), row(ln1_b[l]), alpha)
        h_s = h_s.reshape(bs, ts, ROW_TILE, LANES)[:, :dseq].reshape(ns_rows * ROW_TILE, LANES)
        nblk = -(-(TOP_K * n) // MOE_BM) + N_EXPERTS
        h_pad = jnp.zeros(((nblk * MOE_BM // TOP_K - n) * ROW_TILE, LANES), F32)
        h_all = jnp.concatenate([h_p, h_s, h_pad], axis=0)

        top_idx, gates, cnt = _router(h_all, n, router_w[l].T, router_b[l].reshape(-1, 1))
        base, valid_end, pad_end, block_e, n_used = _group_layout(cnt[:, ::LANES], nblk, MOE_BM)
        dest = _dest(top_idx, base)
        inv = _invert(valid_end, pad_end, dest, nblk * MOE_BM)
        y4 = _moe(block_e, n_used, inv, h_all, moe_w1[l],
                  moe_b1[l].reshape(N_EXPERTS, 1, -1), moe_w2[l],
                  moe_b2[l].reshape(N_EXPERTS, 1, -1), MOE_BM)
        x_all = _combine(y4, gates.T, h_all, n, row(ln2_g[l]), row(ln2_b[l]), alpha)

    y_p = x_all[:np_rows].reshape(bp, tp, D_MODEL)[:, N_META:]
    y_s = x_all[np_rows:].reshape(bs, dseq, D_MODEL)
    return (y_p, y_s,
            jnp.stack(new_p[0]), jnp.stack(new_p[1]), jnp.stack(new_p[2]), jnp.stack(new_p[3]),
            jnp.stack(new_s[0]), jnp.stack(new_s[1]), jnp.stack(new_s[2]), jnp.stack(new_s[3]))
```

```python
import functools
import math

import jax
import jax.numpy as jnp
from jax import lax
from jax.experimental import pallas as pl
from jax.experimental.pallas import tpu as pltpu

F32 = jnp.float32
BF16 = jnp.bfloat16

D_MODEL = 1024
N_META = 16
D_A = 384
D_B = 256
D_C = 384
N_BLK_A = 8
BLK_A = D_A // N_BLK_A
CONV_A = 4
RG_C = 8.0
CONV_B = 31
N_HEADS_C = 4
DV_C = D_C // N_HEADS_C
DK_C = DV_C // 2
D_QK_C = N_HEADS_C * DK_C
GATE_RANK = 16
GATE_TAU = 16.0
N_EXPERTS = 32
TOP_K = 4
D_FF = D_MODEL
SWIGLU_LIMIT = 7.0
SWIGLU_ALPHA = 1.702
LN_EPS = 1e-5

PA_W = 2 * D_A
PB_W = 2 * D_B
PC_Q, PC_Z, PC_K, PC_V, PC_R, PC_W = 0, 192, 256, 512, 896, 1280
P_W = PA_W + PB_W + PC_W

SUBLANES = 8
LANES = 128
ROW_TILE = D_MODEL // LANES
VMEM_LIMIT_BYTES = 56 * 1024 * 1024
MOE_BM = 256
FF_CHUNK = 256
DMA_GROUP = 8
W_CAST_ROWS = 128
SAMPLE_PAD_T = 8


def _cparams(*sem):
    return pltpu.CompilerParams(dimension_semantics=sem, vmem_limit_bytes=VMEM_LIMIT_BYTES)


def _sds(shape, dtype=F32):
    return jax.ShapeDtypeStruct(shape, dtype)


def _pick(n, prefs):
    for p in prefs:
        if n % p == 0:
            return p
    raise ValueError(f"no tile for {n} in {prefs}")


def _ln(x, g, b):
    mu = jnp.mean(x, axis=-1, keepdims=True)
    xc = x - mu
    var = jnp.mean(xc * xc, axis=-1, keepdims=True)
    return xc * lax.rsqrt(var + LN_EPS) * g + b


def _sigmoid(x):
    return 1.0 / (1.0 + jnp.exp(-x))


def _split_bf16(x):
    hi = x.astype(BF16)
    lo = (x - hi.astype(F32)).astype(BF16)
    return hi, lo


def _dot(a, b):
    return jnp.dot(a, b, preferred_element_type=F32)


def _inproj_kernel(x_ref, g_ref, b_ref, w_ref, pa_ref, pb_ref, pc_ref, *maybe_xn, apply_ln):
    x = x_ref[...]
    if apply_ln:
        x = _ln(x, g_ref[...], b_ref[...])
        maybe_xn[0][...] = x
    xb = x.astype(BF16)
    pa_ref[...] = _dot(xb, w_ref[:, 0:PA_W])
    pb_ref[...] = _dot(xb, w_ref[:, PA_W:PA_W + PB_W])
    pc_ref[...] = _dot(xb, w_ref[:, PA_W + PB_W:P_W])


def _inproj(x, row0, nrows, ln_g, ln_b, w_packed, apply_ln):
    tm = _pick(nrows, (384, 512, 256, 128, 64, 32, 16, 8))
    while row0 % tm:
        tm //= 2
    off = row0 // tm
    const = lambda i: (0, 0)
    row = lambda i: (i, 0)
    out_shape = [_sds((nrows, PA_W)), _sds((nrows, PB_W)), _sds((nrows, PC_W))]
    out_specs = [pl.BlockSpec((tm, PA_W), row), pl.BlockSpec((tm, PB_W), row), pl.BlockSpec((tm, PC_W), row)]
    if apply_ln:
        out_shape.append(_sds((nrows, D_MODEL)))
        out_specs.append(pl.BlockSpec((tm, D_MODEL), row))
    return pl.pallas_call(
        functools.partial(_inproj_kernel, apply_ln=apply_ln),
        grid=(nrows // tm,),
        in_specs=[pl.BlockSpec((tm, D_MODEL), lambda i: (i + off, 0)),
                  pl.BlockSpec((1, D_MODEL), const), pl.BlockSpec((1, D_MODEL), const),
                  pl.BlockSpec((D_MODEL, P_W), const)],
        out_specs=out_specs, out_shape=out_shape,
        compiler_params=_cparams("parallel"), name="inproj",
    )(x, ln_g, ln_b, w_packed)


def _rglru_kernel(p_ref, cbuf_ref, h0_ref, cw_ref, cb_ref, wg_ref, bg_ref, lam_ref,
                  y_ref, cnew_ref, hlast_ref, xp_scr, a_scr, h_scr, *, T, Tc, Tv, Bb):
    lam = lam_ref[...]
    softplus_neg = jnp.maximum(-lam, 0.0) + jnp.log1p(jnp.exp(-jnp.abs(lam)))
    c_decay = -RG_C * softplus_neg
    cw = cw_ref[...]
    cb = cb_ref[...]
    bg = bg_ref[...]
    sub = lax.broadcasted_iota(jnp.int32, (Tc, D_A), 0) % SUBLANES
    halo = SUBLANES - (CONV_A - 1)
    for b in range(Bb):
        xp_scr[halo:SUBLANES, :] = cbuf_ref[b]
        xp_scr[SUBLANES:SUBLANES + T, :] = p_ref[b, :, 0:D_A]
        cnew_ref[b] = xp_scr[halo + Tv:SUBLANES + Tv, :]

        def chunk(ci, h_b):
            r0 = pl.multiple_of(ci * Tc, SUBLANES)
            win = xp_scr[pl.ds(r0, Tc + SUBLANES), :]
            xc = cb + cw[0:1] * win[halo:halo + Tc]
            for j in range(1, CONV_A):
                xc = xc + cw[j:j + 1] * win[halo + j:halo + j + Tc]
            gates = _dot(xc.astype(BF16), wg_ref[...]) + bg
            r = _sigmoid(gates[:, 0:D_A])
            i = _sigmoid(gates[:, D_A:2 * D_A])
            log_a = c_decay * r
            a = jnp.exp(log_a)
            u = jnp.sqrt(1.0 - a * a) * (i * xc)
            for s in (1, 2, 4):
                keep = sub >= s
                a_prev = pltpu.roll(a, s, 0)
                u_prev = pltpu.roll(u, s, 0)
                u = jnp.where(keep, a * u_prev + u, u)
                a = jnp.where(keep, a * a_prev, a)
            a_scr[...] = a
            h_scr[pl.ds(r0, Tc), :] = u

            def group(gi, h_b):
                c0 = pl.multiple_of(gi * SUBLANES, SUBLANES)
                g0 = pl.multiple_of(r0 + gi * SUBLANES, SUBLANES)
                h8 = a_scr[pl.ds(c0, SUBLANES), :] * h_b + h_scr[pl.ds(g0, SUBLANES), :]
                h_scr[pl.ds(g0, SUBLANES), :] = h8
                return jnp.broadcast_to(h8[SUBLANES - 1:SUBLANES, :], (SUBLANES, D_A))

            h_b = lax.fori_loop(0, Tc // SUBLANES, group, h_b)
            ga = p_ref[b, pl.ds(r0, Tc), D_A:2 * D_A]
            gelu = 0.5 * ga * (1.0 + jnp.tanh(0.7978845608028654 * (ga + 0.044715 * ga * ga * ga)))
            y_ref[b, pl.ds(r0, Tc), :] = h_scr[pl.ds(r0, Tc), :] * gelu
            return h_b

        h_b = jnp.broadcast_to(h0_ref[b], (SUBLANES, D_A))
        lax.fori_loop(0, T // Tc, chunk, h_b)
        hlast_ref[b] = h_scr[Tv - 1:Tv, :]


def _rglru(pa3, cbuf, h0, cw, cb, wg, bg, lam, Tv, Bb):
    B, T, _ = pa3.shape
    Tc = _pick(T, (344, 256, 128, 64, 48, 32, 16, 8))
    const2 = lambda i: (0, 0)
    seq3 = lambda i: (i, 0, 0)
    return pl.pallas_call(
        functools.partial(_rglru_kernel, T=T, Tc=Tc, Tv=Tv, Bb=Bb),
        grid=(B // Bb,),
        in_specs=[pl.BlockSpec((Bb, T, PA_W), seq3), pl.BlockSpec((Bb, CONV_A - 1, D_A), seq3),
                  pl.BlockSpec((Bb, 1, D_A), seq3), pl.BlockSpec((CONV_A, D_A), const2),
                  pl.BlockSpec((1, D_A), const2), pl.BlockSpec((D_A, 2 * D_A), const2),
                  pl.BlockSpec((1, 2 * D_A), const2), pl.BlockSpec((1, D_A), const2)],
        out_specs=[pl.BlockSpec((Bb, T, D_A), seq3), pl.BlockSpec((Bb, CONV_A - 1, D_A), seq3),
                   pl.BlockSpec((Bb, 1, D_A), seq3)],
        out_shape=[_sds((B, T, D_A)), _sds((B, CONV_A - 1, D_A)), _sds((B, 1, D_A))],
        scratch_shapes=[pltpu.VMEM((T + 2 * SUBLANES, D_A), F32), pltpu.VMEM((Tc, D_A), F32),
                        pltpu.VMEM((T, D_A), F32)],
        compiler_params=_cparams("parallel"), name="rglru",
    )(pa3, cbuf, h0, cw, cb, wg, bg, lam)


B_HALO = 32


def _convb_kernel(p_ref, buf_ref, w_ref, cb_ref, g_ref, b_ref, y_ref, bnew_ref, u_scr, *, T, Tc, Tv, Bb):
    w = w_ref[...]
    cb = cb_ref[...]
    g = g_ref[...]
    bb = b_ref[...]
    first = B_HALO - (CONV_B - 1)
    for b in range(Bb):
        u_scr[0:first, :] = jnp.zeros((first, D_B), F32)
        u_scr[first:B_HALO, :] = buf_ref[b]
        u_scr[B_HALO:B_HALO + T, :] = p_ref[b, :, 0:D_B] * _sigmoid(p_ref[b, :, D_B:2 * D_B])
        bnew_ref[b] = u_scr[first + Tv:B_HALO + Tv, :]

        def chunk(ci, carry):
            r0 = pl.multiple_of(ci * Tc, SUBLANES)
            win = u_scr[pl.ds(r0, Tc + B_HALO), :]
            shifted = [win] + [pltpu.roll(win, Tc + B_HALO - s, 0) for s in range(1, SUBLANES)]
            acc = cb
            for j in range(CONV_B):
                a, s = divmod(first + j, SUBLANES)
                acc = acc + w[j:j + 1] * shifted[s][a * SUBLANES:a * SUBLANES + Tc]
            yn = _ln(acc, g, bb)
            y_ref[b, pl.ds(r0, Tc), :] = yn * _sigmoid(yn)
            return carry

        lax.fori_loop(0, T // Tc, chunk, 0)


def _convb(pb3, buf, w, cb, g, b, Tv, Bb):
    B, T, _ = pb3.shape
    Tc = _pick(T, (48, 32, 16, 8))
    const2 = lambda i: (0, 0)
    seq3 = lambda i: (i, 0, 0)
    return pl.pallas_call(
        functools.partial(_convb_kernel, T=T, Tc=Tc, Tv=Tv, Bb=Bb),
        grid=(B // Bb,),
        in_specs=[pl.BlockSpec((Bb, T, PB_W), seq3), pl.BlockSpec((Bb, CONV_B - 1, D_B), seq3),
                  pl.BlockSpec((CONV_B, D_B), const2), pl.BlockSpec((1, D_B), const2),
                  pl.BlockSpec((1, D_B), const2), pl.BlockSpec((1, D_B), const2)],
        out_specs=[pl.BlockSpec((Bb, T, D_B), seq3), pl.BlockSpec((Bb, CONV_B - 1, D_B), seq3)],
        out_shape=[_sds((B, T, D_B)), _sds((B, CONV_B - 1, D_B))],
        scratch_shapes=[pltpu.VMEM((T + B_HALO, D_B), F32)],
        compiler_params=_cparams("parallel"), name="convb",
    )(pb3, buf, w, cb, g, b)


def _gla_kernel(p_ref, s0_ref, wg2_ref, bgate_ref, ng_ref, y_ref, snew_ref, s_scr, g_scr, *, T, C, Tb, Tv, Bb):
    ri = lax.broadcasted_iota(jnp.int32, (C, C), 0)
    ci_ = lax.broadcasted_iota(jnp.int32, (C, C), 1)
    tril = ri >= ci_
    lane_k = lax.broadcasted_iota(jnp.int32, (1, D_QK_C), 1)
    lane_v = lax.broadcasted_iota(jnp.int32, (1, D_C), 1)
    hm_k = [(lane_k >= h * DK_C) & (lane_k < (h + 1) * DK_C) for h in range(N_HEADS_C)]
    hm_v = [(lane_v >= h * DV_C) & (lane_v < (h + 1) * DV_C) for h in range(N_HEADS_C)]
    rs = lax.broadcasted_iota(jnp.int32, (D_C, D_QK_C), 0)
    cs = lax.broadcasted_iota(jnp.int32, (D_C, D_QK_C), 1)
    bd_t = (rs >= 0) & (rs < 0)
    for h in range(N_HEADS_C):
        bd_t = bd_t | ((rs >= h * DV_C) & (rs < (h + 1) * DV_C) & (cs >= h * DK_C) & (cs < (h + 1) * DK_C))
    rm = lax.broadcasted_iota(jnp.int32, (D_C, D_C), 0)
    cm = lax.broadcasted_iota(jnp.int32, (D_C, D_C), 1)
    seg = (rm >= 0) & (rm < 0)
    for h in range(N_HEADS_C):
        seg = seg | ((rm >= h * DV_C) & (rm < (h + 1) * DV_C) & (cm >= h * DV_C) & (cm < (h + 1) * DV_C))
    mseg = jnp.where(seg, 1.0, 0.0).astype(BF16)
    wg2 = wg2_ref[...].astype(BF16)
    bgate = bgate_ref[...]
    ng = ng_ref[...]
    rowi = lax.broadcasted_iota(jnp.int32, (C, 1), 0)
    tdims = (((0,), (0,)), ((), ()))

    n_chunks = T // C
    nt_dims = (((1,), (1,)), ((), ()))
    tril4 = jnp.concatenate([tril] * N_HEADS_C, axis=0)
    scan_shifts = [s for s in (1, 2, 4, 8, 16, 32) if s < C]
    scan_keep = [rowi >= s for s in scan_shifts]

    def tile_rows(ti, tb):
        r0 = ti * tb
        return pl.ds(r0 if isinstance(ti, int) else pl.multiple_of(r0, tb), tb)

    for b in range(Bb):
        def gates(ti, carry):
            rows = tile_rows(ti, Tb)
            z = p_ref[b, rows, PC_Z:PC_Z + GATE_RANK]
            pre = _dot(z.astype(BF16), wg2) + bgate
            g = (jnp.minimum(pre, 0.0) - jnp.log1p(jnp.exp(-jnp.abs(pre)))) * (1.0 / GATE_TAU)
            rid = ti * Tb + lax.broadcasted_iota(jnp.int32, (Tb, 1), 0)
            g_scr[rows, :] = jnp.where(rid < Tv, g, 0.0)
            return carry
        lax.fori_loop(0, T // Tb, gates, 0)

        def chunk(ci, carry):
            rows = tile_rows(ci, C)
            q = p_ref[b, rows, PC_Q:PC_Q + D_QK_C] * (DK_C ** -0.5)
            k = p_ref[b, rows, PC_K:PC_K + D_QK_C]
            v = p_ref[b, rows, PC_V:PC_V + D_C]
            k = jnp.where((ci * C + rowi) < Tv, k, 0.0)
            gcum = g_scr[rows, :]
            for s, keep in zip(scan_shifts, scan_keep):
                gcum = gcum + jnp.where(keep, pltpu.roll(gcum, s, 0), 0.0)
            g_last = gcum[C - 1:C, :]
            g_mid = gcum[C // 2 - 1:C // 2, :]
            vb = v.astype(BF16)
            qt = q * jnp.exp(gcum - g_mid)
            ktb = (k * jnp.exp(g_mid - gcum)).astype(BF16)
            q4 = jnp.concatenate([jnp.where(hm_k[h], qt, 0.0) for h in range(N_HEADS_C)], axis=0).astype(BF16)
            sc = lax.dot_general(q4, ktb, nt_dims, preferred_element_type=F32)
            r4 = _dot(jnp.where(tril4, sc, 0.0).astype(BF16), vb)
            o = jnp.where(hm_v[0], r4[0:C], 0.0)
            for h in range(1, N_HEADS_C):
                o = o + jnp.where(hm_v[h], r4[h * C:(h + 1) * C], 0.0)
            kd = (k * jnp.exp(g_last - gcum)).astype(BF16)
            upd_t = lax.dot_general(vb, kd, tdims, preferred_element_type=F32)
            s_in = s_scr[...]
            o = o + lax.dot_general((q * jnp.exp(gcum)).astype(BF16), s_in.astype(BF16), nt_dims,
                                    preferred_element_type=F32)
            s_scr[...] = s_in * jnp.exp(g_last) + jnp.where(bd_t, upd_t, 0.0)
            y_ref[b, rows, :] = o
            return carry

        s_scr[...] = jnp.zeros((D_C, D_QK_C), F32)
        for h in range(N_HEADS_C):
            s_scr[h * DV_C:(h + 1) * DV_C, h * DK_C:(h + 1) * DK_C] = s0_ref[b, h]
        lax.fori_loop(0, n_chunks, chunk, 0)
        for h in range(N_HEADS_C):
            snew_ref[b, h] = s_scr[h * DV_C:(h + 1) * DV_C, h * DK_C:(h + 1) * DK_C]

        def finish(ti, carry):
            rows = tile_rows(ti, Tb)
            o = y_ref[b, rows, :]
            rg = p_ref[b, rows, PC_R:PC_R + D_C]
            o2_hi, o2_lo = _split_bf16(o * o)
            ms = (_dot(o2_hi, mseg) + _dot(o2_lo, mseg)) * (1.0 / DV_C)
            y_ref[b, rows, :] = o * lax.rsqrt(ms + LN_EPS) * ng * (rg * _sigmoid(rg))
            return carry
        lax.fori_loop(0, T // Tb, finish, 0)


def _gla(pc3, s0, wg2, bgate, ng, Tv, Bb):
    B, T, _ = pc3.shape
    C = _pick(T, (48, 32, 16, 8))
    Tb = _pick(T, (344, 256, 128, 64, 48, 32, 16, 8))
    const2 = lambda i: (0, 0)
    seq3 = lambda i: (i, 0, 0)
    seq4 = lambda i: (i, 0, 0, 0)
    st = (Bb, N_HEADS_C, DV_C, DK_C)
    y, s_new_t = pl.pallas_call(
        functools.partial(_gla_kernel, T=T, C=C, Tb=Tb, Tv=Tv, Bb=Bb),
        grid=(B // Bb,),
        in_specs=[pl.BlockSpec((Bb, T, PC_W), seq3), pl.BlockSpec(st, seq4),
                  pl.BlockSpec((GATE_RANK, D_QK_C), const2), pl.BlockSpec((1, D_QK_C), const2),
                  pl.BlockSpec((1, D_C), const2)],
        out_specs=[pl.BlockSpec((Bb, T, D_C), seq3), pl.BlockSpec(st, seq4)],
        out_shape=[_sds((B, T, D_C)), _sds((B, N_HEADS_C, DV_C, DK_C))],
        scratch_shapes=[pltpu.VMEM((D_C, D_QK_C), F32), pltpu.VMEM((T, D_QK_C), F32)],
        compiler_params=_cparams("parallel"), name="gla",
    )(pc3, jnp.swapaxes(s0, 2, 3), wg2, bgate, ng)
    return y, jnp.swapaxes(s_new_t, 2, 3)


def _tiles_to_rows(ref, lead, start, rows, stride):
    return jnp.concatenate([ref[lead + (pl.ds(start + s, rows, stride=stride), slice(None))]
                            for s in range(ROW_TILE)], axis=1)


def _rows_to_tiles(ref, lead, rows, val):
    for s in range(ROW_TILE):
        ref[lead + (pl.ds(s, rows, stride=ROW_TILE), slice(None))] = val[:, s * LANES:(s + 1) * LANES]


def _outproj_kernel(ya_ref, yb_ref, yc_ref, x_ref, w_ref, g_ref, b_ref, *rest, alpha, n_blocks):
    h_ref = rest[-1]

    @pl.when(pl.program_id(0) < n_blocks)
    def _():
        y = (_dot(ya_ref[...].astype(BF16), w_ref[0:D_A, :])
             + _dot(yb_ref[...].astype(BF16), w_ref[D_A:D_A + D_B, :])
             + _dot(yc_ref[...].astype(BF16), w_ref[D_A + D_B:D_MODEL, :]))
        h = _ln(alpha * x_ref[...] + y, g_ref[...], b_ref[...])
        _rows_to_tiles(h_ref, (), h.shape[0], h)

    @pl.when(pl.program_id(0) >= n_blocks)
    def _():
        h_ref[...] = jnp.zeros(h_ref.shape, F32)


def _outproj(ya, yb, yc, x, row0, w, g, b, alpha, tm, out_rows, out_row0=0, into=None):
    n = ya.shape[0]
    nb = n // tm
    assert n % tm == 0 and row0 % tm == 0 and out_row0 % tm == 0 and out_rows % tm == 0
    off, out_off = row0 // tm, out_row0 // tm
    grid = nb if into is not None else out_rows // tm
    row = lambda i: (jnp.minimum(i, nb - 1), 0)
    const = lambda i: (0, 0)
    in_specs = [pl.BlockSpec((tm, D_A), row), pl.BlockSpec((tm, D_B), row), pl.BlockSpec((tm, D_C), row),
                pl.BlockSpec((tm, D_MODEL), lambda i: (jnp.minimum(i, nb - 1) + off, 0)),
                pl.BlockSpec((D_MODEL, D_MODEL), const), pl.BlockSpec((1, D_MODEL), const),
                pl.BlockSpec((1, D_MODEL), const)]
    args = [ya, yb, yc, x, w, g, b]
    aliases = {}
    if into is not None:
        in_specs.append(pl.BlockSpec(memory_space=pl.ANY))
        args.append(into)
        aliases = {len(args) - 1: 0}
    return pl.pallas_call(
        functools.partial(_outproj_kernel, alpha=alpha, n_blocks=nb),
        grid=(grid,),
        in_specs=in_specs,
        out_specs=pl.BlockSpec((tm * ROW_TILE, LANES), lambda i: (i + out_off, 0)),
        out_shape=_sds((out_rows * ROW_TILE, LANES)),
        input_output_aliases=aliases,
        compiler_params=_cparams("arbitrary"), name="outproj",
    )(*args)


def _router_kernel(h_ref, wt_ref, b_ref, idx_ref, gate_ref, cnt_ref):
    nt = (((1,), (1,)), ((), ()))
    hh, hl = _split_bf16(_tiles_to_rows(h_ref, (), 0, idx_ref.shape[1], ROW_TILE))
    wh, wl = _split_bf16(wt_ref[...])
    logits = (lax.dot_general(wh, hh, nt, preferred_element_type=F32)
              + lax.dot_general(wh, hl, nt, preferred_element_type=F32)
              + lax.dot_general(wl, hh, nt, preferred_element_type=F32)) + b_ref[...]
    eid = lax.broadcasted_iota(jnp.int32, logits.shape, 0)
    vals = []
    member = jnp.zeros(logits.shape, F32)
    for k in range(TOP_K):
        m = jnp.max(logits, axis=0, keepdims=True)
        sel = jnp.min(jnp.where(logits == m, eid, N_EXPERTS), axis=0, keepdims=True)
        idx_ref[k:k + 1, :] = sel
        vals.append(m)
        hit = eid == sel
        member = jnp.where(hit, 1.0, member)
        logits = jnp.where(hit, -jnp.inf, logits)
    es = [jnp.exp(v - vals[0]) for v in vals]
    tot = es[0] + es[1] + es[2] + es[3]
    for k in range(TOP_K):
        gate_ref[k:k + 1, :] = es[k] / tot
    cnt_ref[...] = jnp.broadcast_to(jnp.sum(member, axis=1, keepdims=True), cnt_ref.shape)


def _router_tile(n):
    return _pick(n, (896, 640, 512, 384, 256, 128))


def _router(h, n, wt, b):
    tm = _router_tile(n)
    nt = n // tm
    return pl.pallas_call(
        _router_kernel,
        grid=(nt,),
        in_specs=[pl.BlockSpec((tm * ROW_TILE, LANES), lambda i: (i, 0)),
                  pl.BlockSpec((N_EXPERTS, D_MODEL), lambda i: (0, 0)),
                  pl.BlockSpec((N_EXPERTS, 1), lambda i: (0, 0))],
        out_specs=[pl.BlockSpec((TOP_K, tm), lambda i: (0, i)), pl.BlockSpec((TOP_K, tm), lambda i: (0, i)),
                   pl.BlockSpec((N_EXPERTS, LANES), lambda i: (0, i))],
        out_shape=[_sds((TOP_K, n), jnp.int32), _sds((TOP_K, n)), _sds((N_EXPERTS, nt * LANES))],
        compiler_params=_cparams("parallel"), name="router",
    )(h, wt, b)


def _dest_kernel(idx_ref, base_ref, dest_ref):
    tm = idx_ref.shape[1]
    eid = lax.broadcasted_iota(jnp.int32, (N_EXPERTS, tm), 0)
    hits = [eid == idx_ref[k:k + 1, :] for k in range(TOP_K)]
    member = jnp.zeros((N_EXPERTS, tm), F32)
    for k in range(TOP_K):
        member = jnp.where(hits[k], 1.0, member)
    earlier = (lax.broadcasted_iota(jnp.int32, (tm, tm), 0) < lax.broadcasted_iota(jnp.int32, (tm, tm), 1))
    rank = _dot(member.astype(BF16), jnp.where(earlier, 1.0, 0.0).astype(BF16))
    pos = base_ref[...] + rank
    for k in range(TOP_K):
        dest_ref[k:k + 1, :] = jnp.sum(jnp.where(hits[k], pos, 0.0), axis=0, keepdims=True).astype(jnp.int32)


def _dest(top_idx, base):
    n = top_idx.shape[1]
    tm = _router_tile(n)
    return pl.pallas_call(
        _dest_kernel,
        grid=(n // tm,),
        in_specs=[pl.BlockSpec((TOP_K, tm), lambda i: (0, i)),
                  pl.BlockSpec((None, N_EXPERTS, 1), lambda i: (i, 0, 0))],
        out_specs=pl.BlockSpec((TOP_K, tm), lambda i: (0, i)),
        out_shape=_sds((TOP_K, n), jnp.int32),
        compiler_params=_cparams("parallel"), name="moe_dest",
    )(top_idx, base)


def _invert_kernel(vend_ref, pend_ref, dest_hbm, inv_ref, buf, sem, *, n, rc, n_rows):
    rows_per_slot = n // LANES
    chunks_per_slot = rows_per_slot // rc

    def chunk(c, carry):
        cp = pltpu.make_async_copy(dest_hbm.at[pl.ds(c * rc, rc), :], buf, sem.at[0])
        cp.start()
        cp.wait()
        k = c // chunks_per_slot
        t0 = (c - k * chunks_per_slot) * (rc * LANES)

        def row(r, carry):
            for l in range(LANES):
                inv_ref[buf[r, l]] = (t0 + r * LANES + l) * TOP_K + k
            return carry

        return lax.fori_loop(0, rc, row, carry)

    lax.fori_loop(0, TOP_K * chunks_per_slot, chunk, 0)

    def fill(lo, hi, ctr):
        def body(r, ctr):
            inv_ref[r] = ctr
            return ctr + 1
        return lax.fori_loop(lo, hi, body, ctr)

    ctr = jnp.int32(TOP_K * n)
    for e in range(N_EXPERTS):
        ctr = fill(vend_ref[e], pend_ref[e], ctr)
    fill(pend_ref[N_EXPERTS - 1], n_rows, ctr)


def _invert(valid_end, pad_end, dest, n_rows):
    k, n = dest.shape
    rc = _pick(n // LANES, (19, 16, 8, 5, 4, 2, 1))
    return pl.pallas_call(
        functools.partial(_invert_kernel, n=n, rc=rc, n_rows=n_rows),
        grid_spec=pltpu.PrefetchScalarGridSpec(
            num_scalar_prefetch=2, grid=(1,),
            in_specs=[pl.BlockSpec(memory_space=pl.ANY)],
            out_specs=pl.BlockSpec(memory_space=pltpu.SMEM),
            scratch_shapes=[pltpu.SMEM((rc, LANES), jnp.int32), pltpu.SemaphoreType.DMA((1,))]),
        out_shape=_sds((n_rows,), jnp.int32),
        compiler_params=_cparams("arbitrary"), name="moe_invert",
    )(valid_end, pad_end, dest.reshape(k * n // LANES, LANES))


def _moe_kernel(be_ref, nu_ref, inv_ref, h_hbm, w1f_ref, b1_ref, w2f_ref, b2_ref, y4_hbm,
                xbuf, xb, obuf, w1_ref, w2_ref, gsem, ssem, *, bm):
    i = pl.program_id(0)
    n_used = nu_ref[0]
    last_blk = pl.num_programs(0) - 1
    n_ff = D_FF // FF_CHUNK
    per = bm // n_ff

    def tile(r):
        start = r * ROW_TILE
        return pl.ds(start if isinstance(r, int) else pl.multiple_of(start, ROW_TILE), ROW_TILE)

    def gather_row(blk, slot, j, r=None):
        r = inv_ref[blk * bm + j] if r is None else r
        t = lax.shift_right_logical(r, 2)
        pltpu.make_async_copy(h_hbm.at[tile(t), :], xbuf.at[slot, tile(j), :], gsem.at[slot]).start()

    def scatter_row(blk, slot, j, r=None):
        r = inv_ref[blk * bm + j] if r is None else r
        pltpu.make_async_copy(obuf.at[slot, tile(j), :], y4_hbm.at[tile(r), :], ssem.at[slot]).start()

    def wait_gather(slot):
        pltpu.make_async_copy(h_hbm.at[pl.ds(0, bm * ROW_TILE), :], xbuf.at[slot], gsem.at[slot]).wait()

    def wait_scatter(slot):
        pltpu.make_async_copy(obuf.at[slot], y4_hbm.at[pl.ds(0, bm * ROW_TILE), :], ssem.at[slot]).wait()

    def loop_rows(fn, blk, slot):
        def body(j, c):
            fn(blk, slot, j)
            return c
        lax.fori_loop(0, bm, body, 0)

    def step(with_scatter):
        slot = i % 2
        other = 1 - slot
        nxt = jnp.minimum(i + 1, last_blk)
        wait_gather(slot)
        xb[...] = _tiles_to_rows(xbuf, (slot,), 0, bm, ROW_TILE).astype(BF16)
        acc = jnp.zeros((bm, D_MODEL), F32)
        for c in range(n_ff):
            for j0 in range(c * per, (c + 1) * per, DMA_GROUP):
                js = range(j0, j0 + DMA_GROUP)
                g_ids = [inv_ref[nxt * bm + j] for j in js]
                s_ids = [inv_ref[(i - 1) * bm + j] for j in js] if with_scatter else []
                for j, r in zip(js, g_ids):
                    gather_row(nxt, other, j, r)
                for j, r in zip(js, s_ids):
                    scatter_row(i - 1, other, j, r)
            lo = c * FF_CHUNK
            x = xb[...]
            hg = _dot(x, w1_ref[:, lo:lo + FF_CHUNK]) + b1_ref[:, lo:lo + FF_CHUNK]
            hl = _dot(x, w1_ref[:, D_FF + lo:D_FF + lo + FF_CHUNK]) + b1_ref[:, D_FF + lo:D_FF + lo + FF_CHUNK]
            gate = jnp.minimum(hg, SWIGLU_LIMIT)
            lin = jnp.clip(hl, -SWIGLU_LIMIT, SWIGLU_LIMIT)
            act = gate * _sigmoid(SWIGLU_ALPHA * gate) * (lin + 1.0)
            acc = acc + _dot(act.astype(BF16), w2_ref[lo:lo + FF_CHUNK, :])

        @pl.when(i >= 2)
        def _():
            wait_scatter(slot)
        _rows_to_tiles(obuf, (slot,), bm, acc + b2_ref[...])

    new_expert = jnp.logical_or(i == 0, be_ref[i] != be_ref[jnp.maximum(i - 1, 0)])

    @pl.when(jnp.logical_and(new_expert, i < n_used))
    def _():
        def cast_rows(r, c):
            rows = pl.ds(pl.multiple_of(r * W_CAST_ROWS, W_CAST_ROWS), W_CAST_ROWS)
            w1_ref[rows, :] = w1f_ref[rows, :].astype(BF16)
            w2_ref[rows, :] = w2f_ref[rows, :].astype(BF16)
            return c
        lax.fori_loop(0, D_MODEL // W_CAST_ROWS, cast_rows, 0)

    @pl.when(jnp.logical_and(i == 0, n_used > 0))
    def _():
        loop_rows(gather_row, 0, 0)
        step(False)

    @pl.when(jnp.logical_and(i > 0, i < n_used))
    def _():
        step(True)

    @pl.when(i == n_used - 1)
    def _():
        slot = i % 2
        loop_rows(scatter_row, i, slot)
        wait_scatter(slot)

        @pl.when(i >= 1)
        def _():
            wait_scatter(1 - slot)
        wait_gather(1 - slot)
        obuf[0] = jnp.zeros((bm * ROW_TILE, LANES), F32)

        def zero_block(blk, c):
            rows = bm * ROW_TILE
            cp = pltpu.make_async_copy(obuf.at[0], y4_hbm.at[pl.ds(pl.multiple_of(blk * rows, rows), rows), :],
                                       ssem.at[0])
            cp.start()
            cp.wait()
            return c
        lax.fori_loop(n_used, last_blk + 1, zero_block, 0)


def _moe(block_e, n_used, inv, h, layer, w1, b1, w2, b2, bm):
    nblk = block_e.shape[0]
    wmap = lambda i, be, nu, iv: (layer, be[i], 0, 0)
    return pl.pallas_call(
        functools.partial(_moe_kernel, bm=bm),
        grid_spec=pltpu.PrefetchScalarGridSpec(
            num_scalar_prefetch=3, grid=(nblk,),
            in_specs=[pl.BlockSpec(memory_space=pl.ANY),
                      pl.BlockSpec((None, None, D_MODEL, 2 * D_FF), wmap),
                      pl.BlockSpec((None, None, 1, 2 * D_FF), wmap),
                      pl.BlockSpec((None, None, D_FF, D_MODEL), wmap),
                      pl.BlockSpec((None, None, 1, D_MODEL), wmap)],
            out_specs=pl.BlockSpec(memory_space=pl.ANY),
            scratch_shapes=[pltpu.VMEM((2, bm * ROW_TILE, LANES), F32), pltpu.VMEM((bm, D_MODEL), BF16),
                            pltpu.VMEM((2, bm * ROW_TILE, LANES), F32),
                            pltpu.VMEM((D_MODEL, 2 * D_FF), BF16), pltpu.VMEM((D_FF, D_MODEL), BF16),
                            pltpu.SemaphoreType.DMA((2,)), pltpu.SemaphoreType.DMA((2,))]),
        out_shape=_sds((nblk * bm * ROW_TILE, LANES)),
        compiler_params=_cparams("arbitrary"), name="moe_experts",
    )(block_e, n_used, inv, h, w1, b1, w2, b2)


def _combine_kernel(y4_ref, gates_ref, h_ref, g_ref, b_ref, out_ref, *, alpha):
    tm = out_ref.shape[0]
    gates = gates_ref[...]
    acc = alpha * _tiles_to_rows(h_ref, (), 0, tm, ROW_TILE)
    for k in range(TOP_K):
        acc = acc + _tiles_to_rows(y4_ref, (), k * ROW_TILE, tm, TOP_K * ROW_TILE) * gates[:, k:k + 1]
    out_ref[...] = _ln(acc, g_ref[...], b_ref[...])


def _combine(y4, gates_t, h, n, g, b, alpha):
    tm = _pick(n, (224, 128, 64, 32, 16, 8))
    row = lambda i: (i, 0)
    const = lambda i: (0, 0)
    return pl.pallas_call(
        functools.partial(_combine_kernel, alpha=alpha),
        grid=(n // tm,),
        in_specs=[pl.BlockSpec((TOP_K * tm * ROW_TILE, LANES), row), pl.BlockSpec((tm, TOP_K), row),
                  pl.BlockSpec((tm * ROW_TILE, LANES), row),
                  pl.BlockSpec((1, D_MODEL), const), pl.BlockSpec((1, D_MODEL), const)],
        out_specs=pl.BlockSpec((tm, D_MODEL), row),
        out_shape=_sds((n, D_MODEL)),
        compiler_params=_cparams("parallel"), name="moe_combine",
    )(y4, gates_t, h, g, b)


def _group_layout(cnt_tiles, nblk, bm):
    cnt = cnt_tiles.astype(jnp.int32)
    counts = jnp.sum(cnt, axis=1)
    padded = (counts + bm - 1) // bm * bm
    pad_end = jnp.cumsum(padded)
    pad_start = pad_end - padded
    base = pad_start[:, None] + jnp.cumsum(cnt, axis=1) - cnt
    starts = jnp.arange(nblk, dtype=jnp.int32) * bm
    block_e = jnp.minimum(jnp.sum((pad_end[None, :] <= starts[:, None]).astype(jnp.int32), axis=1),
                          N_EXPERTS - 1).astype(jnp.int32)
    n_used = (pad_end[-1] // bm).astype(jnp.int32).reshape(1)
    valid_end = (pad_start + counts).astype(jnp.int32)
    return base.T.astype(F32)[:, :, None], valid_end, pad_end.astype(jnp.int32), block_e, n_used


def _pack_w_in(w):
    o = 0
    xa, o = w[:, o:o + D_A], o + D_A
    ga, o = w[:, o:o + D_A], o + D_A
    vb, o = w[:, o:o + D_B], o + D_B
    gb, o = w[:, o:o + D_B], o + D_B
    q, o = w[:, o:o + D_QK_C], o + D_QK_C
    k, o = w[:, o:o + D_QK_C], o + D_QK_C
    v, o = w[:, o:o + D_C], o + D_C
    r, o = w[:, o:o + D_C], o + D_C
    z = w[:, o:o + GATE_RANK]
    zq = jnp.zeros((w.shape[0], PC_K - PC_Z - GATE_RANK), w.dtype)
    zk = jnp.zeros((w.shape[0], PC_V - PC_K - D_QK_C), w.dtype)
    return jnp.concatenate([xa, ga, vb, gb, q, z, zq, k, zk, v, r], axis=1).astype(BF16)


def _block_diag(w):
    eye = jnp.eye(N_BLK_A, dtype=w.dtype)
    return jnp.einsum("hij,hg->higj", w, eye).reshape(D_A, D_A)


def kernel(x_prompt, x_sample, state_conv_a, state_rglru, state_conv_b, state_gla, meta_tokens, ln0_g, ln0_b,
           w_in, conv_a_w, conv_a_b, w_rg, b_rg, w_ig, b_ig, lru_lambda, conv_b_w, conv_b_b, ln_b_g, ln_b_b,
           w_gate2, b_gate, gla_norm_g, w_out, ln1_g, ln1_b, router_w, router_b, moe_w1, moe_b1, moe_w2, moe_b2,
           ln2_g, ln2_b):
    bp, seq, _ = x_prompt.shape
    bs, dseq, _ = x_sample.shape
    depth = w_in.shape[0]
    tp = N_META + seq
    ts = SAMPLE_PAD_T
    np_rows = bp * tp
    ns_rows = bs * dseq
    n = np_rows + ns_rows
    alpha = (2 * depth) ** 0.25
    row = lambda a: a.reshape(1, -1)

    meta = jnp.broadcast_to(meta_tokens[None], (bp, N_META, D_MODEL))
    xp_in = jnp.concatenate([meta, x_prompt], axis=1).reshape(np_rows, D_MODEL)
    xs_in = jnp.pad(x_sample, ((0, 0), (0, ts - dseq), (0, 0))).reshape(bs * ts, D_MODEL)
    zeros_p = (jnp.zeros((bp, CONV_A - 1, D_A), F32), jnp.zeros((bp, 1, D_A), F32),
               jnp.zeros((bp, CONV_B - 1, D_B), F32), jnp.zeros((bp, N_HEADS_C, DK_C, DV_C), F32))
    sb = _pick(bs, (8, 4, 2, 1))
    tm_p = _pick(np_rows, (384, 512, 256, 128, 64, 32, 16, 8))
    tm_s = _pick(math.gcd(ns_rows, np_rows), (128, 64, 32, 16, 8))
    nblk = -(-(TOP_K * n) // MOE_BM) + N_EXPERTS
    while (nblk * MOE_BM // TOP_K) % math.lcm(tm_p, tm_s):
        nblk += 1

    new_p = ([], [], [], [])
    new_s = ([], [], [], [])
    x_all = None
    for l in range(depth):
        w_packed = _pack_w_in(w_in[l])
        wg = jnp.concatenate([_block_diag(w_rg[l]), _block_diag(w_ig[l])], axis=1).astype(BF16)
        bg = jnp.concatenate([b_rg[l], b_ig[l]]).reshape(1, -1)
        w_out_b = w_out[l].astype(BF16)
        ng = row(jnp.tile(gla_norm_g[l], N_HEADS_C))
        first = l == 0
        if first:
            pa_p, pb_p, pc_p, xn_p = _inproj(xp_in, 0, np_rows, row(ln0_g), row(ln0_b), w_packed, True)
            pa_s, pb_s, pc_s, xn_s = _inproj(xs_in, 0, bs * ts, row(ln0_g), row(ln0_b), w_packed, True)
            res_p, res_p_row0 = xn_p, 0
        else:
            pa_p, pb_p, pc_p = _inproj(x_all, 0, np_rows, row(ln0_g), row(ln0_b), w_packed, False)
            xn_s = jnp.pad(x_all[np_rows:].reshape(bs, dseq, D_MODEL),
                           ((0, 0), (0, ts - dseq), (0, 0))).reshape(bs * ts, D_MODEL)
            pa_s, pb_s, pc_s = _inproj(xn_s, 0, bs * ts, row(ln0_g), row(ln0_b), w_packed, False)
            res_p, res_p_row0 = x_all, 0

        outs = []
        for (pa, pb, pc, nb, t, tv, bb, st) in (
                (pa_p, pb_p, pc_p, bp, tp, tp, 1, zeros_p),
                (pa_s, pb_s, pc_s, bs, ts, dseq, sb,
                 (state_conv_a[l], state_rglru[l].reshape(bs, 1, D_A), state_conv_b[l], state_gla[l]))):
            ya, ca_new, h_last = _rglru(pa.reshape(nb, t, PA_W), st[0], st[1], conv_a_w[l], row(conv_a_b[l]),
                                        wg, bg, row(lru_lambda[l]), tv, bb)
            yb, cb_new = _convb(pb.reshape(nb, t, PB_W), st[2], conv_b_w[l], row(conv_b_b[l]),
                                row(ln_b_g[l]), row(ln_b_b[l]), tv, bb)
            yc, s_new = _gla(pc.reshape(nb, t, PC_W), st[3], w_gate2[l], row(b_gate[l]), ng, tv, bb)
            outs.append((ya.reshape(nb * t, D_A), yb.reshape(nb * t, D_B), yc.reshape(nb * t, D_C),
                         ca_new, h_last.reshape(nb, D_A), cb_new, s_new))
        (ya_p, yb_p, yc_p, *st_p), (ya_s, yb_s, yc_s, *st_s) = outs
        for j in range(4):
            new_p[j].append(st_p[j])
            new_s[j].append(st_s[j])

        valid = lambda a: a.reshape(bs, ts, -1)[:, :dseq].reshape(ns_rows, -1)
        h_all = _outproj(ya_p, yb_p, yc_p, res_p, res_p_row0, w_out_b, row(ln1_g[l]), row(ln1_b[l]), alpha,
                         tm_p, nblk * MOE_BM // TOP_K)
        h_all = _outproj(valid(ya_s), valid(yb_s), valid(yc_s), valid(xn_s), 0, w_out_b, row(ln1_g[l]),
                         row(ln1_b[l]), alpha, tm_s, nblk * MOE_BM // TOP_K, out_row0=np_rows, into=h_all)

        top_idx, gates, cnt = _router(h_all, n, router_w[l].T, router_b[l].reshape(-1, 1))
        base, valid_end, pad_end, block_e, n_used = _group_layout(cnt[:, ::LANES], nblk, MOE_BM)
        dest = _dest(top_idx, base)
        inv = _invert(valid_end, pad_end, dest, nblk * MOE_BM)
        y4 = _moe(block_e, n_used, inv, h_all, l, moe_w1, moe_b1[:, :, None, :], moe_w2, moe_b2[:, :, None, :],
                  MOE_BM)
        x_all = _combine(y4, gates.T, h_all, n, row(ln2_g[l]), row(ln2_b[l]), alpha)

    y_p = x_all[:np_rows].reshape(bp, tp, D_MODEL)[:, N_META:]
    y_s = x_all[np_rows:].reshape(bs, dseq, D_MODEL)
    return (y_p, y_s,
            jnp.stack(new_p[0]), jnp.stack(new_p[1]), jnp.stack(new_p[2]), jnp.stack(new_p[3]),
            jnp.stack(new_s[0]), jnp.stack(new_s[1]), jnp.stack(new_s[2]), jnp.stack(new_s[3]))
```

```python
import functools
import math

import jax
import jax.numpy as jnp
from jax import lax
from jax.experimental import pallas as pl
from jax.experimental.pallas import tpu as pltpu

F32 = jnp.float32
BF16 = jnp.bfloat16

D_MODEL = 1024
N_META = 16
D_A = 384
D_B = 256
D_C = 384
N_BLK_A = 8
BLK_A = D_A // N_BLK_A
CONV_A = 4
RG_C = 8.0
CONV_B = 31
N_HEADS_C = 4
DV_C = D_C // N_HEADS_C
DK_C = DV_C // 2
D_QK_C = N_HEADS_C * DK_C
GATE_RANK = 16
GATE_TAU = 16.0
N_EXPERTS = 32
TOP_K = 4
D_FF = D_MODEL
SWIGLU_LIMIT = 7.0
SWIGLU_ALPHA = 1.702
LN_EPS = 1e-5

PA_W = 2 * D_A
PB_W = 2 * D_B
PC_Q, PC_Z, PC_K, PC_V, PC_R, PC_W = 0, 192, 256, 512, 896, 1280
P_W = PA_W + PB_W + PC_W

SUBLANES = 8
LANES = 128
ROW_TILE = D_MODEL // LANES
VMEM_LIMIT_BYTES = 56 * 1024 * 1024
MOE_BM = 256
FF_CHUNK = 256
DMA_GROUP = 8
W_CAST_ROWS = 128
SAMPLE_PAD_T = 8


def _cparams(*sem):
    return pltpu.CompilerParams(dimension_semantics=sem, vmem_limit_bytes=VMEM_LIMIT_BYTES)


def _sds(shape, dtype=F32):
    return jax.ShapeDtypeStruct(shape, dtype)


def _pick(n, prefs):
    for p in prefs:
        if n % p == 0:
            return p
    raise ValueError(f"no tile for {n} in {prefs}")


def _ln(x, g, b):
    mu = jnp.mean(x, axis=-1, keepdims=True)
    xc = x - mu
    var = jnp.mean(xc * xc, axis=-1, keepdims=True)
    return xc * lax.rsqrt(var + LN_EPS) * g + b


def _sigmoid(x):
    return 1.0 / (1.0 + jnp.exp(-x))


def _split_bf16(x):
    hi = x.astype(BF16)
    lo = (x - hi.astype(F32)).astype(BF16)
    return hi, lo


def _dot(a, b):
    return jnp.dot(a, b, preferred_element_type=F32)


def _inproj_kernel(x_ref, g_ref, b_ref, w_ref, pa_ref, pb_ref, pc_ref, *maybe_xn, apply_ln):
    x = x_ref[...]
    if apply_ln:
        x = _ln(x, g_ref[...], b_ref[...])
        maybe_xn[0][...] = x
    xb = x.astype(BF16)
    pa_ref[...] = _dot(xb, w_ref[:, 0:PA_W])
    pb_ref[...] = _dot(xb, w_ref[:, PA_W:PA_W + PB_W])
    pc_ref[...] = _dot(xb, w_ref[:, PA_W + PB_W:P_W])


def _inproj(x, row0, nrows, ln_g, ln_b, w_packed, apply_ln):
    tm = _pick(nrows, (384, 512, 256, 128, 64, 32, 16, 8))
    while row0 % tm:
        tm //= 2
    off = row0 // tm
    const = lambda i: (0, 0)
    row = lambda i: (i, 0)
    out_shape = [_sds((nrows, PA_W)), _sds((nrows, PB_W)), _sds((nrows, PC_W))]
    out_specs = [pl.BlockSpec((tm, PA_W), row), pl.BlockSpec((tm, PB_W), row), pl.BlockSpec((tm, PC_W), row)]
    if apply_ln:
        out_shape.append(_sds((nrows, D_MODEL)))
        out_specs.append(pl.BlockSpec((tm, D_MODEL), row))
    return pl.pallas_call(
        functools.partial(_inproj_kernel, apply_ln=apply_ln),
        grid=(nrows // tm,),
        in_specs=[pl.BlockSpec((tm, D_MODEL), lambda i: (i + off, 0)),
                  pl.BlockSpec((1, D_MODEL), const), pl.BlockSpec((1, D_MODEL), const),
                  pl.BlockSpec((D_MODEL, P_W), const)],
        out_specs=out_specs, out_shape=out_shape,
        compiler_params=_cparams("parallel"), name="inproj",
    )(x, ln_g, ln_b, w_packed)


def _rglru_kernel(p_ref, cbuf_ref, h0_ref, cw_ref, cb_ref, wg_ref, bg_ref, lam_ref,
                  y_ref, cnew_ref, hlast_ref, xp_scr, a_scr, h_scr, *, T, Tc, Tv, Bb):
    lam = lam_ref[...]
    softplus_neg = jnp.maximum(-lam, 0.0) + jnp.log1p(jnp.exp(-jnp.abs(lam)))
    c_decay = -RG_C * softplus_neg
    cw = cw_ref[...]
    cb = cb_ref[...]
    bg = bg_ref[...]
    sub = lax.broadcasted_iota(jnp.int32, (Tc, D_A), 0) % SUBLANES
    halo = SUBLANES - (CONV_A - 1)
    for b in range(Bb):
        xp_scr[halo:SUBLANES, :] = cbuf_ref[b]
        xp_scr[SUBLANES:SUBLANES + T, :] = p_ref[b, :, 0:D_A]
        cnew_ref[b] = xp_scr[halo + Tv:SUBLANES + Tv, :]

        def chunk(ci, h_b):
            r0 = pl.multiple_of(ci * Tc, SUBLANES)
            win = xp_scr[pl.ds(r0, Tc + SUBLANES), :]
            xc = cb + cw[0:1] * win[halo:halo + Tc]
            for j in range(1, CONV_A):
                xc = xc + cw[j:j + 1] * win[halo + j:halo + j + Tc]
            gates = _dot(xc.astype(BF16), wg_ref[...]) + bg
            r = _sigmoid(gates[:, 0:D_A])
            i = _sigmoid(gates[:, D_A:2 * D_A])
            log_a = c_decay * r
            a = jnp.exp(log_a)
            u = jnp.sqrt(1.0 - a * a) * (i * xc)
            for s in (1, 2, 4):
                keep = sub >= s
                a_prev = pltpu.roll(a, s, 0)
                u_prev = pltpu.roll(u, s, 0)
                u = jnp.where(keep, a * u_prev + u, u)
                a = jnp.where(keep, a * a_prev, a)
            a_scr[...] = a
            h_scr[pl.ds(r0, Tc), :] = u

            def group(gi, h_b):
                c0 = pl.multiple_of(gi * SUBLANES, SUBLANES)
                g0 = pl.multiple_of(r0 + gi * SUBLANES, SUBLANES)
                h8 = a_scr[pl.ds(c0, SUBLANES), :] * h_b + h_scr[pl.ds(g0, SUBLANES), :]
                h_scr[pl.ds(g0, SUBLANES), :] = h8
                return jnp.broadcast_to(h8[SUBLANES - 1:SUBLANES, :], (SUBLANES, D_A))

            h_b = lax.fori_loop(0, Tc // SUBLANES, group, h_b)
            ga = p_ref[b, pl.ds(r0, Tc), D_A:2 * D_A]
            gelu = 0.5 * ga * (1.0 + jnp.tanh(0.7978845608028654 * (ga + 0.044715 * ga * ga * ga)))
            y_ref[b, pl.ds(r0, Tc), :] = h_scr[pl.ds(r0, Tc), :] * gelu
            return h_b

        h_b = jnp.broadcast_to(h0_ref[b], (SUBLANES, D_A))
        lax.fori_loop(0, T // Tc, chunk, h_b)
        hlast_ref[b] = h_scr[Tv - 1:Tv, :]


def _rglru(pa3, cbuf, h0, cw, cb, wg, bg, lam, Tv, Bb):
    B, T, _ = pa3.shape
    Tc = _pick(T, (344, 256, 128, 64, 48, 32, 16, 8))
    const2 = lambda i: (0, 0)
    seq3 = lambda i: (i, 0, 0)
    return pl.pallas_call(
        functools.partial(_rglru_kernel, T=T, Tc=Tc, Tv=Tv, Bb=Bb),
        grid=(B // Bb,),
        in_specs=[pl.BlockSpec((Bb, T, PA_W), seq3), pl.BlockSpec((Bb, CONV_A - 1, D_A), seq3),
                  pl.BlockSpec((Bb, 1, D_A), seq3), pl.BlockSpec((CONV_A, D_A), const2),
                  pl.BlockSpec((1, D_A), const2), pl.BlockSpec((D_A, 2 * D_A), const2),
                  pl.BlockSpec((1, 2 * D_A), const2), pl.BlockSpec((1, D_A), const2)],
        out_specs=[pl.BlockSpec((Bb, T, D_A), seq3), pl.BlockSpec((Bb, CONV_A - 1, D_A), seq3),
                   pl.BlockSpec((Bb, 1, D_A), seq3)],
        out_shape=[_sds((B, T, D_A)), _sds((B, CONV_A - 1, D_A)), _sds((B, 1, D_A))],
        scratch_shapes=[pltpu.VMEM((T + 2 * SUBLANES, D_A), F32), pltpu.VMEM((Tc, D_A), F32),
                        pltpu.VMEM((T, D_A), F32)],
        compiler_params=_cparams("parallel"), name="rglru",
    )(pa3, cbuf, h0, cw, cb, wg, bg, lam)


B_HALO = 32


def _convb_kernel(p_ref, buf_ref, w_ref, cb_ref, g_ref, b_ref, y_ref, bnew_ref, u_scr, *, T, Tc, Tv, Bb):
    w = w_ref[...]
    cb = cb_ref[...]
    g = g_ref[...]
    bb = b_ref[...]
    first = B_HALO - (CONV_B - 1)
    for b in range(Bb):
        u_scr[0:first, :] = jnp.zeros((first, D_B), F32)
        u_scr[first:B_HALO, :] = buf_ref[b]
        u_scr[B_HALO:B_HALO + T, :] = p_ref[b, :, 0:D_B] * _sigmoid(p_ref[b, :, D_B:2 * D_B])
        bnew_ref[b] = u_scr[first + Tv:B_HALO + Tv, :]

        def chunk(ci, carry):
            r0 = pl.multiple_of(ci * Tc, SUBLANES)
            win = u_scr[pl.ds(r0, Tc + B_HALO), :]
            shifted = [win] + [pltpu.roll(win, Tc + B_HALO - s, 0) for s in range(1, SUBLANES)]
            acc = cb
            for j in range(CONV_B):
                a, s = divmod(first + j, SUBLANES)
                acc = acc + w[j:j + 1] * shifted[s][a * SUBLANES:a * SUBLANES + Tc]
            yn = _ln(acc, g, bb)
            y_ref[b, pl.ds(r0, Tc), :] = yn * _sigmoid(yn)
            return carry

        lax.fori_loop(0, T // Tc, chunk, 0)


def _convb(pb3, buf, w, cb, g, b, Tv, Bb):
    B, T, _ = pb3.shape
    Tc = _pick(T, (48, 32, 16, 8))
    const2 = lambda i: (0, 0)
    seq3 = lambda i: (i, 0, 0)
    return pl.pallas_call(
        functools.partial(_convb_kernel, T=T, Tc=Tc, Tv=Tv, Bb=Bb),
        grid=(B // Bb,),
        in_specs=[pl.BlockSpec((Bb, T, PB_W), seq3), pl.BlockSpec((Bb, CONV_B - 1, D_B), seq3),
                  pl.BlockSpec((CONV_B, D_B), const2), pl.BlockSpec((1, D_B), const2),
                  pl.BlockSpec((1, D_B), const2), pl.BlockSpec((1, D_B), const2)],
        out_specs=[pl.BlockSpec((Bb, T, D_B), seq3), pl.BlockSpec((Bb, CONV_B - 1, D_B), seq3)],
        out_shape=[_sds((B, T, D_B)), _sds((B, CONV_B - 1, D_B))],
        scratch_shapes=[pltpu.VMEM((T + B_HALO, D_B), F32)],
        compiler_params=_cparams("parallel"), name="convb",
    )(pb3, buf, w, cb, g, b)


def _gla_kernel(p_ref, s0_ref, wg2_ref, bgate_ref, ng_ref, y_ref, snew_ref, s_scr, g_scr, *, T, C, Tb, Tv, Bb):
    ri = lax.broadcasted_iota(jnp.int32, (C, C), 0)
    ci_ = lax.broadcasted_iota(jnp.int32, (C, C), 1)
    tril = ri >= ci_
    lane_k = lax.broadcasted_iota(jnp.int32, (1, D_QK_C), 1)
    lane_v = lax.broadcasted_iota(jnp.int32, (1, D_C), 1)
    hm_k = [(lane_k >= h * DK_C) & (lane_k < (h + 1) * DK_C) for h in range(N_HEADS_C)]
    hm_v = [(lane_v >= h * DV_C) & (lane_v < (h + 1) * DV_C) for h in range(N_HEADS_C)]
    rs = lax.broadcasted_iota(jnp.int32, (D_C, D_QK_C), 0)
    cs = lax.broadcasted_iota(jnp.int32, (D_C, D_QK_C), 1)
    bd_t = (rs >= 0) & (rs < 0)
    for h in range(N_HEADS_C):
        bd_t = bd_t | ((rs >= h * DV_C) & (rs < (h + 1) * DV_C) & (cs >= h * DK_C) & (cs < (h + 1) * DK_C))
    rm = lax.broadcasted_iota(jnp.int32, (D_C, D_C), 0)
    cm = lax.broadcasted_iota(jnp.int32, (D_C, D_C), 1)
    seg = (rm >= 0) & (rm < 0)
    for h in range(N_HEADS_C):
        seg = seg | ((rm >= h * DV_C) & (rm < (h + 1) * DV_C) & (cm >= h * DV_C) & (cm < (h + 1) * DV_C))
    mseg = jnp.where(seg, 1.0, 0.0).astype(BF16)
    wg2 = wg2_ref[...].astype(BF16)
    bgate = bgate_ref[...]
    ng = ng_ref[...]
    rowi = lax.broadcasted_iota(jnp.int32, (C, 1), 0)
    tdims = (((0,), (0,)), ((), ()))

    n_chunks = T // C
    nt_dims = (((1,), (1,)), ((), ()))
    tril4 = jnp.concatenate([tril] * N_HEADS_C, axis=0)
    scan_shifts = [s for s in (1, 2, 4, 8, 16, 32) if s < C]
    scan_keep = [rowi >= s for s in scan_shifts]

    def tile_rows(ti, tb):
        r0 = ti * tb
        return pl.ds(r0 if isinstance(ti, int) else pl.multiple_of(r0, tb), tb)

    for b in range(Bb):
        def gates(ti, carry):
            rows = tile_rows(ti, Tb)
            z = p_ref[b, rows, PC_Z:PC_Z + GATE_RANK]
            pre = _dot(z.astype(BF16), wg2) + bgate
            g = (jnp.minimum(pre, 0.0) - jnp.log1p(jnp.exp(-jnp.abs(pre)))) * (1.0 / GATE_TAU)
            rid = ti * Tb + lax.broadcasted_iota(jnp.int32, (Tb, 1), 0)
            g_scr[rows, :] = jnp.where(rid < Tv, g, 0.0)
            return carry
        lax.fori_loop(0, T // Tb, gates, 0)

        def chunk(ci, carry):
            rows = tile_rows(ci, C)
            q = p_ref[b, rows, PC_Q:PC_Q + D_QK_C] * (DK_C ** -0.5)
            k = p_ref[b, rows, PC_K:PC_K + D_QK_C]
            v = p_ref[b, rows, PC_V:PC_V + D_C]
            k = jnp.where((ci * C + rowi) < Tv, k, 0.0)
            gcum = g_scr[rows, :]
            for s, keep in zip(scan_shifts, scan_keep):
                gcum = gcum + jnp.where(keep, pltpu.roll(gcum, s, 0), 0.0)
            g_last = gcum[C - 1:C, :]
            g_mid = gcum[C // 2 - 1:C // 2, :]
            vb = v.astype(BF16)
            qt = q * jnp.exp(gcum - g_mid)
            ktb = (k * jnp.exp(g_mid - gcum)).astype(BF16)
            q4 = jnp.concatenate([jnp.where(hm_k[h], qt, 0.0) for h in range(N_HEADS_C)], axis=0).astype(BF16)
            sc = lax.dot_general(q4, ktb, nt_dims, preferred_element_type=F32)
            r4 = _dot(jnp.where(tril4, sc, 0.0).astype(BF16), vb)
            o = jnp.where(hm_v[0], r4[0:C], 0.0)
            for h in range(1, N_HEADS_C):
                o = o + jnp.where(hm_v[h], r4[h * C:(h + 1) * C], 0.0)
            kd = (k * jnp.exp(g_last - gcum)).astype(BF16)
            upd_t = lax.dot_general(vb, kd, tdims, preferred_element_type=F32)
            s_in = s_scr[...]
            o = o + lax.dot_general((q * jnp.exp(gcum)).astype(BF16), s_in.astype(BF16), nt_dims,
                                    preferred_element_type=F32)
            s_scr[...] = s_in * jnp.exp(g_last) + jnp.where(bd_t, upd_t, 0.0)
            y_ref[b, rows, :] = o
            return carry

        s_scr[...] = jnp.zeros((D_C, D_QK_C), F32)
        for h in range(N_HEADS_C):
            s_scr[h * DV_C:(h + 1) * DV_C, h * DK_C:(h + 1) * DK_C] = s0_ref[b, h]
        lax.fori_loop(0, n_chunks, chunk, 0)
        for h in range(N_HEADS_C):
            snew_ref[b, h] = s_scr[h * DV_C:(h + 1) * DV_C, h * DK_C:(h + 1) * DK_C]

        def finish(ti, carry):
            rows = tile_rows(ti, Tb)
            o = y_ref[b, rows, :]
            rg = p_ref[b, rows, PC_R:PC_R + D_C]
            o2_hi, o2_lo = _split_bf16(o * o)
            ms = (_dot(o2_hi, mseg) + _dot(o2_lo, mseg)) * (1.0 / DV_C)
            y_ref[b, rows, :] = o * lax.rsqrt(ms + LN_EPS) * ng * (rg * _sigmoid(rg))
            return carry
        lax.fori_loop(0, T // Tb, finish, 0)


def _gla(pc3, s0, wg2, bgate, ng, Tv, Bb):
    B, T, _ = pc3.shape
    C = _pick(T, (48, 32, 16, 8))
    Tb = _pick(T, (344, 256, 128, 64, 48, 32, 16, 8))
    const2 = lambda i: (0, 0)
    seq3 = lambda i: (i, 0, 0)
    seq4 = lambda i: (i, 0, 0, 0)
    st = (Bb, N_HEADS_C, DV_C, DK_C)
    y, s_new_t = pl.pallas_call(
        functools.partial(_gla_kernel, T=T, C=C, Tb=Tb, Tv=Tv, Bb=Bb),
        grid=(B // Bb,),
        in_specs=[pl.BlockSpec((Bb, T, PC_W), seq3), pl.BlockSpec(st, seq4),
                  pl.BlockSpec((GATE_RANK, D_QK_C), const2), pl.BlockSpec((1, D_QK_C), const2),
                  pl.BlockSpec((1, D_C), const2)],
        out_specs=[pl.BlockSpec((Bb, T, D_C), seq3), pl.BlockSpec(st, seq4)],
        out_shape=[_sds((B, T, D_C)), _sds((B, N_HEADS_C, DV_C, DK_C))],
        scratch_shapes=[pltpu.VMEM((D_C, D_QK_C), F32), pltpu.VMEM((T, D_QK_C), F32)],
        compiler_params=_cparams("parallel"), name="gla",
    )(pc3, jnp.swapaxes(s0, 2, 3), wg2, bgate, ng)
    return y, jnp.swapaxes(s_new_t, 2, 3)


def _tiles_to_rows(ref, lead, start, rows, stride):
    return jnp.concatenate([ref[lead + (pl.ds(start + s, rows, stride=stride), slice(None))]
                            for s in range(ROW_TILE)], axis=1)


def _rows_to_tiles(ref, lead, rows, val):
    for s in range(ROW_TILE):
        ref[lead + (pl.ds(s, rows, stride=ROW_TILE), slice(None))] = val[:, s * LANES:(s + 1) * LANES]


def _outproj_kernel(ya_ref, yb_ref, yc_ref, x_ref, w_ref, g_ref, b_ref, *rest, alpha, n_blocks):
    h_ref = rest[-1]

    @pl.when(pl.program_id(0) < n_blocks)
    def _():
        y = (_dot(ya_ref[...].astype(BF16), w_ref[0:D_A, :])
             + _dot(yb_ref[...].astype(BF16), w_ref[D_A:D_A + D_B, :])
             + _dot(yc_ref[...].astype(BF16), w_ref[D_A + D_B:D_MODEL, :]))
        h = _ln(alpha * x_ref[...] + y, g_ref[...], b_ref[...])
        _rows_to_tiles(h_ref, (), h.shape[0], h)

    @pl.when(pl.program_id(0) >= n_blocks)
    def _():
        h_ref[...] = jnp.zeros(h_ref.shape, F32)


def _outproj(ya, yb, yc, x, row0, w, g, b, alpha, tm, out_rows, out_row0=0, into=None):
    n = ya.shape[0]
    nb = n // tm
    assert n % tm == 0 and row0 % tm == 0 and out_row0 % tm == 0 and out_rows % tm == 0
    off, out_off = row0 // tm, out_row0 // tm
    grid = nb if into is not None else out_rows // tm
    row = lambda i: (jnp.minimum(i, nb - 1), 0)
    const = lambda i: (0, 0)
    in_specs = [pl.BlockSpec((tm, D_A), row), pl.BlockSpec((tm, D_B), row), pl.BlockSpec((tm, D_C), row),
                pl.BlockSpec((tm, D_MODEL), lambda i: (jnp.minimum(i, nb - 1) + off, 0)),
                pl.BlockSpec((D_MODEL, D_MODEL), const), pl.BlockSpec((1, D_MODEL), const),
                pl.BlockSpec((1, D_MODEL), const)]
    args = [ya, yb, yc, x, w, g, b]
    aliases = {}
    if into is not None:
        in_specs.append(pl.BlockSpec(memory_space=pl.ANY))
        args.append(into)
        aliases = {len(args) - 1: 0}
    return pl.pallas_call(
        functools.partial(_outproj_kernel, alpha=alpha, n_blocks=nb),
        grid=(grid,),
        in_specs=in_specs,
        out_specs=pl.BlockSpec((tm * ROW_TILE, LANES), lambda i: (i + out_off, 0)),
        out_shape=_sds((out_rows * ROW_TILE, LANES)),
        input_output_aliases=aliases,
        compiler_params=_cparams("arbitrary"), name="outproj",
    )(*args)


def _router_kernel(h_ref, wt_ref, b_ref, idx_ref, gate_ref, cnt_ref):
    nt = (((1,), (1,)), ((), ()))
    hh, hl = _split_bf16(_tiles_to_rows(h_ref, (), 0, idx_ref.shape[1], ROW_TILE))
    wh, wl = _split_bf16(wt_ref[...])
    logits = (lax.dot_general(wh, hh, nt, preferred_element_type=F32)
              + lax.dot_general(wh, hl, nt, preferred_element_type=F32)
              + lax.dot_general(wl, hh, nt, preferred_element_type=F32)) + b_ref[...]
    eid = lax.broadcasted_iota(jnp.int32, logits.shape, 0)
    vals = []
    member = jnp.zeros(logits.shape, F32)
    for k in range(TOP_K):
        m = jnp.max(logits, axis=0, keepdims=True)
        sel = jnp.min(jnp.where(logits == m, eid, N_EXPERTS), axis=0, keepdims=True)
        idx_ref[k:k + 1, :] = sel
        vals.append(m)
        hit = eid == sel
        member = jnp.where(hit, 1.0, member)
        logits = jnp.where(hit, -jnp.inf, logits)
    es = [jnp.exp(v - vals[0]) for v in vals]
    tot = es[0] + es[1] + es[2] + es[3]
    for k in range(TOP_K):
        gate_ref[k:k + 1, :] = es[k] / tot
    cnt_ref[...] = jnp.broadcast_to(jnp.sum(member, axis=1, keepdims=True), cnt_ref.shape)


def _router_tile(n):
    return _pick(n, (896, 640, 512, 384, 256, 128))


def _router(h, n, wt, b):
    tm = _router_tile(n)
    nt = n // tm
    return pl.pallas_call(
        _router_kernel,
        grid=(nt,),
        in_specs=[pl.BlockSpec((tm * ROW_TILE, LANES), lambda i: (i, 0)),
                  pl.BlockSpec((N_EXPERTS, D_MODEL), lambda i: (0, 0)),
                  pl.BlockSpec((N_EXPERTS, 1), lambda i: (0, 0))],
        out_specs=[pl.BlockSpec((TOP_K, tm), lambda i: (0, i)), pl.BlockSpec((TOP_K, tm), lambda i: (0, i)),
                   pl.BlockSpec((N_EXPERTS, LANES), lambda i: (0, i))],
        out_shape=[_sds((TOP_K, n), jnp.int32), _sds((TOP_K, n)), _sds((N_EXPERTS, nt * LANES))],
        compiler_params=_cparams("parallel"), name="router",
    )(h, wt, b)


def _dest_kernel(idx_ref, base_ref, dest_ref):
    tm = idx_ref.shape[1]
    eid = lax.broadcasted_iota(jnp.int32, (N_EXPERTS, tm), 0)
    hits = [eid == idx_ref[k:k + 1, :] for k in range(TOP_K)]
    member = jnp.zeros((N_EXPERTS, tm), F32)
    for k in range(TOP_K):
        member = jnp.where(hits[k], 1.0, member)
    earlier = (lax.broadcasted_iota(jnp.int32, (tm, tm), 0) < lax.broadcasted_iota(jnp.int32, (tm, tm), 1))
    rank = _dot(member.astype(BF16), jnp.where(earlier, 1.0, 0.0).astype(BF16))
    pos = base_ref[...] + rank
    for k in range(TOP_K):
        dest_ref[k:k + 1, :] = jnp.sum(jnp.where(hits[k], pos, 0.0), axis=0, keepdims=True).astype(jnp.int32)


def _dest(top_idx, base):
    n = top_idx.shape[1]
    tm = _router_tile(n)
    return pl.pallas_call(
        _dest_kernel,
        grid=(n // tm,),
        in_specs=[pl.BlockSpec((TOP_K, tm), lambda i: (0, i)),
                  pl.BlockSpec((None, N_EXPERTS, 1), lambda i: (i, 0, 0))],
        out_specs=pl.BlockSpec((TOP_K, tm), lambda i: (0, i)),
        out_shape=_sds((TOP_K, n), jnp.int32),
        compiler_params=_cparams("parallel"), name="moe_dest",
    )(top_idx, base)


def _invert_kernel(vend_ref, pend_ref, dest_hbm, inv_ref, buf, sem, *, n, rc, n_rows):
    rows_per_slot = n // LANES
    chunks_per_slot = rows_per_slot // rc

    def chunk(c, carry):
        cp = pltpu.make_async_copy(dest_hbm.at[pl.ds(c * rc, rc), :], buf, sem.at[0])
        cp.start()
        cp.wait()
        k = c // chunks_per_slot
        t0 = (c - k * chunks_per_slot) * (rc * LANES)

        def row(r, carry):
            for l in range(LANES):
                inv_ref[buf[r, l]] = (t0 + r * LANES + l) * TOP_K + k
            return carry

        return lax.fori_loop(0, rc, row, carry)

    lax.fori_loop(0, TOP_K * chunks_per_slot, chunk, 0)

    def fill(lo, hi, ctr):
        def body(r, ctr):
            inv_ref[r] = ctr
            return ctr + 1
        return lax.fori_loop(lo, hi, body, ctr)

    ctr = jnp.int32(TOP_K * n)
    for e in range(N_EXPERTS):
        ctr = fill(vend_ref[e], pend_ref[e], ctr)
    fill(pend_ref[N_EXPERTS - 1], n_rows, ctr)


def _invert(valid_end, pad_end, dest, n_rows):
    k, n = dest.shape
    rc = _pick(n // LANES, (19, 16, 8, 5, 4, 2, 1))
    return pl.pallas_call(
        functools.partial(_invert_kernel, n=n, rc=rc, n_rows=n_rows),
        grid_spec=pltpu.PrefetchScalarGridSpec(
            num_scalar_prefetch=2, grid=(1,),
            in_specs=[pl.BlockSpec(memory_space=pl.ANY)],
            out_specs=pl.BlockSpec(memory_space=pltpu.SMEM),
            scratch_shapes=[pltpu.SMEM((rc, LANES), jnp.int32), pltpu.SemaphoreType.DMA((1,))]),
        out_shape=_sds((n_rows,), jnp.int32),
        compiler_params=_cparams("arbitrary"), name="moe_invert",
    )(valid_end, pad_end, dest.reshape(k * n // LANES, LANES))


def _moe_kernel(be_ref, nu_ref, inv_ref, h_hbm, w1f_ref, b1_ref, w2f_ref, b2_ref, y4_hbm,
                xbuf, xb, obuf, w1_ref, w2_ref, gsem, ssem, *, bm):
    i = pl.program_id(0)
    n_used = nu_ref[0]
    last_blk = pl.num_programs(0) - 1
    n_ff = D_FF // FF_CHUNK

    def tile(r):
        start = r * ROW_TILE
        return pl.ds(start if isinstance(r, int) else pl.multiple_of(start, ROW_TILE), ROW_TILE)

    def gather_row(blk, slot, j, r=None):
        r = inv_ref[blk * bm + j] if r is None else r
        t = lax.shift_right_logical(r, 2)
        pltpu.make_async_copy(h_hbm.at[tile(t), :], xbuf.at[slot, tile(j), :], gsem.at[slot]).start()

    def scatter_row(blk, slot, j, r=None):
        r = inv_ref[blk * bm + j] if r is None else r
        pltpu.make_async_copy(obuf.at[slot, tile(j), :], y4_hbm.at[tile(r), :], ssem.at[slot]).start()

    def wait_gather(slot):
        pltpu.make_async_copy(h_hbm.at[pl.ds(0, bm * ROW_TILE), :], xbuf.at[slot], gsem.at[slot]).wait()

    def wait_scatter(slot):
        pltpu.make_async_copy(obuf.at[slot], y4_hbm.at[pl.ds(0, bm * ROW_TILE), :], ssem.at[slot]).wait()

    def loop_rows(fn, blk, slot):
        def body(j, c):
            fn(blk, slot, j)
            return c
        lax.fori_loop(0, bm, body, 0)

    def issue_rows(fn, blk, slot):
        for j0 in range(0, bm, DMA_GROUP):
            js = range(j0, j0 + DMA_GROUP)
            ids = [inv_ref[blk * bm + j] for j in js]
            for j, r in zip(js, ids):
                fn(blk, slot, j, r)

    def step():
        slot = i % 2
        issue_rows(gather_row, jnp.minimum(i + 1, last_blk), 1 - slot)
        wait_gather(slot)
        xb[...] = _tiles_to_rows(xbuf, (slot,), 0, bm, ROW_TILE).astype(BF16)
        acc = jnp.zeros((bm, D_MODEL), F32)
        for c in range(n_ff):
            lo = c * FF_CHUNK
            x = xb[...]
            hg = _dot(x, w1_ref[:, lo:lo + FF_CHUNK]) + b1_ref[:, lo:lo + FF_CHUNK]
            hl = _dot(x, w1_ref[:, D_FF + lo:D_FF + lo + FF_CHUNK]) + b1_ref[:, D_FF + lo:D_FF + lo + FF_CHUNK]
            gate = jnp.minimum(hg, SWIGLU_LIMIT)
            lin = jnp.clip(hl, -SWIGLU_LIMIT, SWIGLU_LIMIT)
            act = gate * _sigmoid(SWIGLU_ALPHA * gate) * (lin + 1.0)
            acc = acc + _dot(act.astype(BF16), w2_ref[lo:lo + FF_CHUNK, :])

        @pl.when(i >= 2)
        def _():
            wait_scatter(slot)
        _rows_to_tiles(obuf, (slot,), bm, acc + b2_ref[...])
        issue_rows(scatter_row, i, slot)

    new_expert = jnp.logical_or(i == 0, be_ref[i] != be_ref[jnp.maximum(i - 1, 0)])

    @pl.when(jnp.logical_and(new_expert, i < n_used))
    def _():
        def cast_rows(r, c):
            rows = pl.ds(pl.multiple_of(r * W_CAST_ROWS, W_CAST_ROWS), W_CAST_ROWS)
            w1_ref[rows, :] = w1f_ref[rows, :].astype(BF16)
            w2_ref[rows, :] = w2f_ref[rows, :].astype(BF16)
            return c
        lax.fori_loop(0, D_MODEL // W_CAST_ROWS, cast_rows, 0)

    @pl.when(jnp.logical_and(i == 0, n_used > 0))
    def _():
        loop_rows(gather_row, 0, 0)

    @pl.when(i < n_used)
    def _():
        step()

    @pl.when(i == n_used - 1)
    def _():
        slot = i % 2
        wait_scatter(slot)

        @pl.when(i >= 1)
        def _():
            wait_scatter(1 - slot)
        wait_gather(1 - slot)
        obuf[0] = jnp.zeros((bm * ROW_TILE, LANES), F32)

        def zero_block(blk, c):
            rows = bm * ROW_TILE
            cp = pltpu.make_async_copy(obuf.at[0], y4_hbm.at[pl.ds(pl.multiple_of(blk * rows, rows), rows), :],
                                       ssem.at[0])
            cp.start()
            cp.wait()
            return c
        lax.fori_loop(n_used, last_blk + 1, zero_block, 0)


def _moe(block_e, n_used, inv, h, layer, w1, b1, w2, b2, bm):
    nblk = block_e.shape[0]
    wmap = lambda i, be, nu, iv: (layer, be[i], 0, 0)
    return pl.pallas_call(
        functools.partial(_moe_kernel, bm=bm),
        grid_spec=pltpu.PrefetchScalarGridSpec(
            num_scalar_prefetch=3, grid=(nblk,),
            in_specs=[pl.BlockSpec(memory_space=pl.ANY),
                      pl.BlockSpec((None, None, D_MODEL, 2 * D_FF), wmap),
                      pl.BlockSpec((None, None, 1, 2 * D_FF), wmap),
                      pl.BlockSpec((None, None, D_FF, D_MODEL), wmap),
                      pl.BlockSpec((None, None, 1, D_MODEL), wmap)],
            out_specs=pl.BlockSpec(memory_space=pl.ANY),
            scratch_shapes=[pltpu.VMEM((2, bm * ROW_TILE, LANES), F32), pltpu.VMEM((bm, D_MODEL), BF16),
                            pltpu.VMEM((2, bm * ROW_TILE, LANES), F32),
                            pltpu.VMEM((D_MODEL, 2 * D_FF), BF16), pltpu.VMEM((D_FF, D_MODEL), BF16),
                            pltpu.SemaphoreType.DMA((2,)), pltpu.SemaphoreType.DMA((2,))]),
        out_shape=_sds((nblk * bm * ROW_TILE, LANES)),
        compiler_params=_cparams("arbitrary"), name="moe_experts",
    )(block_e, n_used, inv, h, w1, b1, w2, b2)


def _combine_kernel(y4_ref, gates_ref, h_ref, g_ref, b_ref, out_ref, *, alpha):
    tm = out_ref.shape[0]
    gates = gates_ref[...]
    acc = alpha * _tiles_to_rows(h_ref, (), 0, tm, ROW_TILE)
    for k in range(TOP_K):
        acc = acc + _tiles_to_rows(y4_ref, (), k * ROW_TILE, tm, TOP_K * ROW_TILE) * gates[:, k:k + 1]
    out_ref[...] = _ln(acc, g_ref[...], b_ref[...])


def _combine(y4, gates_t, h, n, g, b, alpha):
    tm = _pick(n, (224, 128, 64, 32, 16, 8))
    row = lambda i: (i, 0)
    const = lambda i: (0, 0)
    return pl.pallas_call(
        functools.partial(_combine_kernel, alpha=alpha),
        grid=(n // tm,),
        in_specs=[pl.BlockSpec((TOP_K * tm * ROW_TILE, LANES), row), pl.BlockSpec((tm, TOP_K), row),
                  pl.BlockSpec((tm * ROW_TILE, LANES), row),
                  pl.BlockSpec((1, D_MODEL), const), pl.BlockSpec((1, D_MODEL), const)],
        out_specs=pl.BlockSpec((tm, D_MODEL), row),
        out_shape=_sds((n, D_MODEL)),
        compiler_params=_cparams("parallel"), name="moe_combine",
    )(y4, gates_t, h, g, b)


def _group_layout(cnt_tiles, nblk, bm):
    cnt = cnt_tiles.astype(jnp.int32)
    counts = jnp.sum(cnt, axis=1)
    padded = (counts + bm - 1) // bm * bm
    pad_end = jnp.cumsum(padded)
    pad_start = pad_end - padded
    base = pad_start[:, None] + jnp.cumsum(cnt, axis=1) - cnt
    starts = jnp.arange(nblk, dtype=jnp.int32) * bm
    block_e = jnp.minimum(jnp.sum((pad_end[None, :] <= starts[:, None]).astype(jnp.int32), axis=1),
                          N_EXPERTS - 1).astype(jnp.int32)
    n_used = (pad_end[-1] // bm).astype(jnp.int32).reshape(1)
    valid_end = (pad_start + counts).astype(jnp.int32)
    return base.T.astype(F32)[:, :, None], valid_end, pad_end.astype(jnp.int32), block_e, n_used


def _pack_w_in(w):
    o = 0
    xa, o = w[:, o:o + D_A], o + D_A
    ga, o = w[:, o:o + D_A], o + D_A
    vb, o = w[:, o:o + D_B], o + D_B
    gb, o = w[:, o:o + D_B], o + D_B
    q, o = w[:, o:o + D_QK_C], o + D_QK_C
    k, o = w[:, o:o + D_QK_C], o + D_QK_C
    v, o = w[:, o:o + D_C], o + D_C
    r, o = w[:, o:o + D_C], o + D_C
    z = w[:, o:o + GATE_RANK]
    zq = jnp.zeros((w.shape[0], PC_K - PC_Z - GATE_RANK), w.dtype)
    zk = jnp.zeros((w.shape[0], PC_V - PC_K - D_QK_C), w.dtype)
    return jnp.concatenate([xa, ga, vb, gb, q, z, zq, k, zk, v, r], axis=1).astype(BF16)


def _block_diag(w):
    eye = jnp.eye(N_BLK_A, dtype=w.dtype)
    return jnp.einsum("hij,hg->higj", w, eye).reshape(D_A, D_A)


def kernel(x_prompt, x_sample, state_conv_a, state_rglru, state_conv_b, state_gla, meta_tokens, ln0_g, ln0_b,
           w_in, conv_a_w, conv_a_b, w_rg, b_rg, w_ig, b_ig, lru_lambda, conv_b_w, conv_b_b, ln_b_g, ln_b_b,
           w_gate2, b_gate, gla_norm_g, w_out, ln1_g, ln1_b, router_w, router_b, moe_w1, moe_b1, moe_w2, moe_b2,
           ln2_g, ln2_b):
    bp, seq, _ = x_prompt.shape
    bs, dseq, _ = x_sample.shape
    depth = w_in.shape[0]
    tp = N_META + seq
    ts = SAMPLE_PAD_T
    np_rows = bp * tp
    ns_rows = bs * dseq
    n = np_rows + ns_rows
    alpha = (2 * depth) ** 0.25
    row = lambda a: a.reshape(1, -1)

    meta = jnp.broadcast_to(meta_tokens[None], (bp, N_META, D_MODEL))
    xp_in = jnp.concatenate([meta, x_prompt], axis=1).reshape(np_rows, D_MODEL)
    xs_in = jnp.pad(x_sample, ((0, 0), (0, ts - dseq), (0, 0))).reshape(bs * ts, D_MODEL)
    zeros_p = (jnp.zeros((bp, CONV_A - 1, D_A), F32), jnp.zeros((bp, 1, D_A), F32),
               jnp.zeros((bp, CONV_B - 1, D_B), F32), jnp.zeros((bp, N_HEADS_C, DK_C, DV_C), F32))
    sb = _pick(bs, (8, 4, 2, 1))
    tm_p = _pick(np_rows, (384, 512, 256, 128, 64, 32, 16, 8))
    tm_s = _pick(math.gcd(ns_rows, np_rows), (128, 64, 32, 16, 8))
    nblk = -(-(TOP_K * n) // MOE_BM) + N_EXPERTS
    while (nblk * MOE_BM // TOP_K) % math.lcm(tm_p, tm_s):
        nblk += 1

    new_p = ([], [], [], [])
    new_s = ([], [], [], [])
    x_all = None
    for l in range(depth):
        w_packed = _pack_w_in(w_in[l])
        wg = jnp.concatenate([_block_diag(w_rg[l]), _block_diag(w_ig[l])], axis=1).astype(BF16)
        bg = jnp.concatenate([b_rg[l], b_ig[l]]).reshape(1, -1)
        w_out_b = w_out[l].astype(BF16)
        ng = row(jnp.tile(gla_norm_g[l], N_HEADS_C))
        first = l == 0
        if first:
            pa_p, pb_p, pc_p, xn_p = _inproj(xp_in, 0, np_rows, row(ln0_g), row(ln0_b), w_packed, True)
            pa_s, pb_s, pc_s, xn_s = _inproj(xs_in, 0, bs * ts, row(ln0_g), row(ln0_b), w_packed, True)
            res_p, res_p_row0 = xn_p, 0
        else:
            pa_p, pb_p, pc_p = _inproj(x_all, 0, np_rows, row(ln0_g), row(ln0_b), w_packed, False)
            xn_s = jnp.pad(x_all[np_rows:].reshape(bs, dseq, D_MODEL),
                           ((0, 0), (0, ts - dseq), (0, 0))).reshape(bs * ts, D_MODEL)
            pa_s, pb_s, pc_s = _inproj(xn_s, 0, bs * ts, row(ln0_g), row(ln0_b), w_packed, False)
            res_p, res_p_row0 = x_all, 0

        outs = []
        for (pa, pb, pc, nb, t, tv, bb, st) in (
                (pa_p, pb_p, pc_p, bp, tp, tp, 1, zeros_p),
                (pa_s, pb_s, pc_s, bs, ts, dseq, sb,
                 (state_conv_a[l], state_rglru[l].reshape(bs, 1, D_A), state_conv_b[l], state_gla[l]))):
            ya, ca_new, h_last = _rglru(pa.reshape(nb, t, PA_W), st[0], st[1], conv_a_w[l], row(conv_a_b[l]),
                                        wg, bg, row(lru_lambda[l]), tv, bb)
            yb, cb_new = _convb(pb.reshape(nb, t, PB_W), st[2], conv_b_w[l], row(conv_b_b[l]),
                                row(ln_b_g[l]), row(ln_b_b[l]), tv, bb)
            yc, s_new = _gla(pc.reshape(nb, t, PC_W), st[3], w_gate2[l], row(b_gate[l]), ng, tv, bb)
            outs.append((ya.reshape(nb * t, D_A), yb.reshape(nb * t, D_B), yc.reshape(nb * t, D_C),
                         ca_new, h_last.reshape(nb, D_A), cb_new, s_new))
        (ya_p, yb_p, yc_p, *st_p), (ya_s, yb_s, yc_s, *st_s) = outs
        for j in range(4):
            new_p[j].append(st_p[j])
            new_s[j].append(st_s[j])

        valid = lambda a: a.reshape(bs, ts, -1)[:, :dseq].reshape(ns_rows, -1)
        h_all = _outproj(ya_p, yb_p, yc_p, res_p, res_p_row0, w_out_b, row(ln1_g[l]), row(ln1_b[l]), alpha,
                         tm_p, nblk * MOE_BM // TOP_K)
        h_all = _outproj(valid(ya_s), valid(yb_s), valid(yc_s), valid(xn_s), 0, w_out_b, row(ln1_g[l]),
                         row(ln1_b[l]), alpha, tm_s, nblk * MOE_BM // TOP_K, out_row0=np_rows, into=h_all)

        top_idx, gates, cnt = _router(h_all, n, router_w[l].T, router_b[l].reshape(-1, 1))
        base, valid_end, pad_end, block_e, n_used = _group_layout(cnt[:, ::LANES], nblk, MOE_BM)
        dest = _dest(top_idx, base)
        inv = _invert(valid_end, pad_end, dest, nblk * MOE_BM)
        y4 = _moe(block_e, n_used, inv, h_all, l, moe_w1, moe_b1[:, :, None, :], moe_w2, moe_b2[:, :, None, :],
                  MOE_BM)
        x_all = _combine(y4, gates.T, h_all, n, row(ln2_g[l]), row(ln2_b[l]), alpha)

    y_p = x_all[:np_rows].reshape(bp, tp, D_MODEL)[:, N_META:]
    y_s = x_all[np_rows:].reshape(bs, dseq, D_MODEL)
    return (y_p, y_s,
            jnp.stack(new_p[0]), jnp.stack(new_p[1]), jnp.stack(new_p[2]), jnp.stack(new_p[3]),
            jnp.stack(new_s[0]), jnp.stack(new_s[1]), jnp.stack(new_s[2]), jnp.stack(new_s[3]))
```

```python
import functools
import math

import jax
import jax.numpy as jnp
from jax import lax
from jax.experimental import pallas as pl
from jax.experimental.pallas import tpu as pltpu

F32 = jnp.float32
BF16 = jnp.bfloat16

D_MODEL = 1024
N_META = 16
D_A = 384
D_B = 256
D_C = 384
N_BLK_A = 8
BLK_A = D_A // N_BLK_A
CONV_A = 4
RG_C = 8.0
CONV_B = 31
N_HEADS_C = 4
DV_C = D_C // N_HEADS_C
DK_C = DV_C // 2
D_QK_C = N_HEADS_C * DK_C
GATE_RANK = 16
GATE_TAU = 16.0
N_EXPERTS = 32
TOP_K = 4
D_FF = D_MODEL
SWIGLU_LIMIT = 7.0
SWIGLU_ALPHA = 1.702
LN_EPS = 1e-5

PA_W = 2 * D_A
PB_W = 2 * D_B
PC_Q, PC_Z, PC_K, PC_V, PC_R, PC_W = 0, 192, 256, 512, 896, 1280
P_W = PA_W + PB_W + PC_W

SUBLANES = 8
LANES = 128
ROW_TILE = D_MODEL // LANES
VMEM_LIMIT_BYTES = 56 * 1024 * 1024
MOE_BM = 256
FF_CHUNK = 256
DMA_GROUP = 8
W_CAST_ROWS = 128
SCATTER_DMA_PRIORITY = 1
SAMPLE_PAD_T = 8


def _cparams(*sem):
    return pltpu.CompilerParams(dimension_semantics=sem, vmem_limit_bytes=VMEM_LIMIT_BYTES)


def _sds(shape, dtype=F32):
    return jax.ShapeDtypeStruct(shape, dtype)


def _pick(n, prefs):
    for p in prefs:
        if n % p == 0:
            return p
    raise ValueError(f"no tile for {n} in {prefs}")


def _ln(x, g, b):
    mu = jnp.mean(x, axis=-1, keepdims=True)
    xc = x - mu
    var = jnp.mean(xc * xc, axis=-1, keepdims=True)
    return xc * lax.rsqrt(var + LN_EPS) * g + b


def _sigmoid(x):
    return 1.0 / (1.0 + jnp.exp(-x))


def _split_bf16(x):
    hi = x.astype(BF16)
    lo = (x - hi.astype(F32)).astype(BF16)
    return hi, lo


def _dot(a, b):
    return jnp.dot(a, b, preferred_element_type=F32)


def _inproj_kernel(x_ref, g_ref, b_ref, w_ref, pa_ref, pb_ref, pc_ref, *maybe_xn, apply_ln):
    x = x_ref[...]
    if apply_ln:
        x = _ln(x, g_ref[...], b_ref[...])
        maybe_xn[0][...] = x
    xb = x.astype(BF16)
    pa_ref[...] = _dot(xb, w_ref[:, 0:PA_W])
    pb_ref[...] = _dot(xb, w_ref[:, PA_W:PA_W + PB_W])
    pc_ref[...] = _dot(xb, w_ref[:, PA_W + PB_W:P_W])


def _inproj(x, row0, nrows, ln_g, ln_b, w_packed, apply_ln):
    tm = _pick(nrows, (384, 512, 256, 128, 64, 32, 16, 8))
    while row0 % tm:
        tm //= 2
    off = row0 // tm
    const = lambda i: (0, 0)
    row = lambda i: (i, 0)
    out_shape = [_sds((nrows, PA_W)), _sds((nrows, PB_W)), _sds((nrows, PC_W))]
    out_specs = [pl.BlockSpec((tm, PA_W), row), pl.BlockSpec((tm, PB_W), row), pl.BlockSpec((tm, PC_W), row)]
    if apply_ln:
        out_shape.append(_sds((nrows, D_MODEL)))
        out_specs.append(pl.BlockSpec((tm, D_MODEL), row))
    return pl.pallas_call(
        functools.partial(_inproj_kernel, apply_ln=apply_ln),
        grid=(nrows // tm,),
        in_specs=[pl.BlockSpec((tm, D_MODEL), lambda i: (i + off, 0)),
                  pl.BlockSpec((1, D_MODEL), const), pl.BlockSpec((1, D_MODEL), const),
                  pl.BlockSpec((D_MODEL, P_W), const)],
        out_specs=out_specs, out_shape=out_shape,
        compiler_params=_cparams("parallel"), name="inproj",
    )(x, ln_g, ln_b, w_packed)


def _rglru_kernel(p_ref, cbuf_ref, h0_ref, cw_ref, cb_ref, wg_ref, bg_ref, lam_ref,
                  y_ref, cnew_ref, hlast_ref, xp_scr, a_scr, h_scr, *, T, Tc, Tv, Bb):
    lam = lam_ref[...]
    softplus_neg = jnp.maximum(-lam, 0.0) + jnp.log1p(jnp.exp(-jnp.abs(lam)))
    c_decay = -RG_C * softplus_neg
    cw = cw_ref[...]
    cb = cb_ref[...]
    bg = bg_ref[...]
    sub = lax.broadcasted_iota(jnp.int32, (Tc, D_A), 0) % SUBLANES
    halo = SUBLANES - (CONV_A - 1)
    for b in range(Bb):
        xp_scr[halo:SUBLANES, :] = cbuf_ref[b]
        xp_scr[SUBLANES:SUBLANES + T, :] = p_ref[b, :, 0:D_A]
        cnew_ref[b] = xp_scr[halo + Tv:SUBLANES + Tv, :]

        def chunk(ci, h_b):
            r0 = pl.multiple_of(ci * Tc, SUBLANES)
            win = xp_scr[pl.ds(r0, Tc + SUBLANES), :]
            xc = cb + cw[0:1] * win[halo:halo + Tc]
            for j in range(1, CONV_A):
                xc = xc + cw[j:j + 1] * win[halo + j:halo + j + Tc]
            gates = _dot(xc.astype(BF16), wg_ref[...]) + bg
            r = _sigmoid(gates[:, 0:D_A])
            i = _sigmoid(gates[:, D_A:2 * D_A])
            log_a = c_decay * r
            a = jnp.exp(log_a)
            u = jnp.sqrt(1.0 - a * a) * (i * xc)
            for s in (1, 2, 4):
                keep = sub >= s
                a_prev = pltpu.roll(a, s, 0)
                u_prev = pltpu.roll(u, s, 0)
                u = jnp.where(keep, a * u_prev + u, u)
                a = jnp.where(keep, a * a_prev, a)
            a_scr[...] = a
            h_scr[pl.ds(r0, Tc), :] = u

            def group(gi, h_b):
                c0 = pl.multiple_of(gi * SUBLANES, SUBLANES)
                g0 = pl.multiple_of(r0 + gi * SUBLANES, SUBLANES)
                h8 = a_scr[pl.ds(c0, SUBLANES), :] * h_b + h_scr[pl.ds(g0, SUBLANES), :]
                h_scr[pl.ds(g0, SUBLANES), :] = h8
                return jnp.broadcast_to(h8[SUBLANES - 1:SUBLANES, :], (SUBLANES, D_A))

            h_b = lax.fori_loop(0, Tc // SUBLANES, group, h_b)
            ga = p_ref[b, pl.ds(r0, Tc), D_A:2 * D_A]
            gelu = 0.5 * ga * (1.0 + jnp.tanh(0.7978845608028654 * (ga + 0.044715 * ga * ga * ga)))
            y_ref[b, pl.ds(r0, Tc), :] = h_scr[pl.ds(r0, Tc), :] * gelu
            return h_b

        h_b = jnp.broadcast_to(h0_ref[b], (SUBLANES, D_A))
        lax.fori_loop(0, T // Tc, chunk, h_b)
        hlast_ref[b] = h_scr[Tv - 1:Tv, :]


def _rglru(pa3, cbuf, h0, cw, cb, wg, bg, lam, Tv, Bb):
    B, T, _ = pa3.shape
    Tc = _pick(T, (344, 256, 128, 64, 48, 32, 16, 8))
    const2 = lambda i: (0, 0)
    seq3 = lambda i: (i, 0, 0)
    return pl.pallas_call(
        functools.partial(_rglru_kernel, T=T, Tc=Tc, Tv=Tv, Bb=Bb),
        grid=(B // Bb,),
        in_specs=[pl.BlockSpec((Bb, T, PA_W), seq3), pl.BlockSpec((Bb, CONV_A - 1, D_A), seq3),
                  pl.BlockSpec((Bb, 1, D_A), seq3), pl.BlockSpec((CONV_A, D_A), const2),
                  pl.BlockSpec((1, D_A), const2), pl.BlockSpec((D_A, 2 * D_A), const2),
                  pl.BlockSpec((1, 2 * D_A), const2), pl.BlockSpec((1, D_A), const2)],
        out_specs=[pl.BlockSpec((Bb, T, D_A), seq3), pl.BlockSpec((Bb, CONV_A - 1, D_A), seq3),
                   pl.BlockSpec((Bb, 1, D_A), seq3)],
        out_shape=[_sds((B, T, D_A)), _sds((B, CONV_A - 1, D_A)), _sds((B, 1, D_A))],
        scratch_shapes=[pltpu.VMEM((T + 2 * SUBLANES, D_A), F32), pltpu.VMEM((Tc, D_A), F32),
                        pltpu.VMEM((T, D_A), F32)],
        compiler_params=_cparams("parallel"), name="rglru",
    )(pa3, cbuf, h0, cw, cb, wg, bg, lam)


B_HALO = 32


def _convb_kernel(p_ref, buf_ref, w_ref, cb_ref, g_ref, b_ref, y_ref, bnew_ref, u_scr, *, T, Tc, Tv, Bb):
    w = w_ref[...]
    cb = cb_ref[...]
    g = g_ref[...]
    bb = b_ref[...]
    first = B_HALO - (CONV_B - 1)
    for b in range(Bb):
        u_scr[0:first, :] = jnp.zeros((first, D_B), F32)
        u_scr[first:B_HALO, :] = buf_ref[b]
        u_scr[B_HALO:B_HALO + T, :] = p_ref[b, :, 0:D_B] * _sigmoid(p_ref[b, :, D_B:2 * D_B])
        bnew_ref[b] = u_scr[first + Tv:B_HALO + Tv, :]

        def chunk(ci, carry):
            r0 = pl.multiple_of(ci * Tc, SUBLANES)
            win = u_scr[pl.ds(r0, Tc + B_HALO), :]
            shifted = [win] + [pltpu.roll(win, Tc + B_HALO - s, 0) for s in range(1, SUBLANES)]
            acc = cb
            for j in range(CONV_B):
                a, s = divmod(first + j, SUBLANES)
                acc = acc + w[j:j + 1] * shifted[s][a * SUBLANES:a * SUBLANES + Tc]
            yn = _ln(acc, g, bb)
            y_ref[b, pl.ds(r0, Tc), :] = yn * _sigmoid(yn)
            return carry

        lax.fori_loop(0, T // Tc, chunk, 0)


def _convb(pb3, buf, w, cb, g, b, Tv, Bb):
    B, T, _ = pb3.shape
    Tc = _pick(T, (48, 32, 16, 8))
    const2 = lambda i: (0, 0)
    seq3 = lambda i: (i, 0, 0)
    return pl.pallas_call(
        functools.partial(_convb_kernel, T=T, Tc=Tc, Tv=Tv, Bb=Bb),
        grid=(B // Bb,),
        in_specs=[pl.BlockSpec((Bb, T, PB_W), seq3), pl.BlockSpec((Bb, CONV_B - 1, D_B), seq3),
                  pl.BlockSpec((CONV_B, D_B), const2), pl.BlockSpec((1, D_B), const2),
                  pl.BlockSpec((1, D_B), const2), pl.BlockSpec((1, D_B), const2)],
        out_specs=[pl.BlockSpec((Bb, T, D_B), seq3), pl.BlockSpec((Bb, CONV_B - 1, D_B), seq3)],
        out_shape=[_sds((B, T, D_B)), _sds((B, CONV_B - 1, D_B))],
        scratch_shapes=[pltpu.VMEM((T + B_HALO, D_B), F32)],
        compiler_params=_cparams("parallel"), name="convb",
    )(pb3, buf, w, cb, g, b)


def _gla_kernel(p_ref, s0_ref, wg2_ref, bgate_ref, ng_ref, y_ref, snew_ref, s_scr, g_scr, *, T, C, Tb, Tv, Bb):
    ri = lax.broadcasted_iota(jnp.int32, (C, C), 0)
    ci_ = lax.broadcasted_iota(jnp.int32, (C, C), 1)
    tril = ri >= ci_
    lane_k = lax.broadcasted_iota(jnp.int32, (1, D_QK_C), 1)
    lane_v = lax.broadcasted_iota(jnp.int32, (1, D_C), 1)
    hm_k = [(lane_k >= h * DK_C) & (lane_k < (h + 1) * DK_C) for h in range(N_HEADS_C)]
    hm_v = [(lane_v >= h * DV_C) & (lane_v < (h + 1) * DV_C) for h in range(N_HEADS_C)]
    rs = lax.broadcasted_iota(jnp.int32, (D_C, D_QK_C), 0)
    cs = lax.broadcasted_iota(jnp.int32, (D_C, D_QK_C), 1)
    bd_t = (rs >= 0) & (rs < 0)
    for h in range(N_HEADS_C):
        bd_t = bd_t | ((rs >= h * DV_C) & (rs < (h + 1) * DV_C) & (cs >= h * DK_C) & (cs < (h + 1) * DK_C))
    rm = lax.broadcasted_iota(jnp.int32, (D_C, D_C), 0)
    cm = lax.broadcasted_iota(jnp.int32, (D_C, D_C), 1)
    seg = (rm >= 0) & (rm < 0)
    for h in range(N_HEADS_C):
        seg = seg | ((rm >= h * DV_C) & (rm < (h + 1) * DV_C) & (cm >= h * DV_C) & (cm < (h + 1) * DV_C))
    mseg = jnp.where(seg, 1.0, 0.0).astype(BF16)
    wg2 = wg2_ref[...].astype(BF16)
    bgate = bgate_ref[...]
    ng = ng_ref[...]
    rowi = lax.broadcasted_iota(jnp.int32, (C, 1), 0)
    tdims = (((0,), (0,)), ((), ()))

    n_chunks = T // C
    nt_dims = (((1,), (1,)), ((), ()))
    tril4 = jnp.concatenate([tril] * N_HEADS_C, axis=0)
    scan_shifts = [s for s in (1, 2, 4, 8, 16, 32) if s < C]
    scan_keep = [rowi >= s for s in scan_shifts]

    def tile_rows(ti, tb):
        r0 = ti * tb
        return pl.ds(r0 if isinstance(ti, int) else pl.multiple_of(r0, tb), tb)

    for b in range(Bb):
        def gates(ti, carry):
            rows = tile_rows(ti, Tb)
            z = p_ref[b, rows, PC_Z:PC_Z + GATE_RANK]
            pre = _dot(z.astype(BF16), wg2) + bgate
            g = (jnp.minimum(pre, 0.0) - jnp.log1p(jnp.exp(-jnp.abs(pre)))) * (1.0 / GATE_TAU)
            rid = ti * Tb + lax.broadcasted_iota(jnp.int32, (Tb, 1), 0)
            g_scr[rows, :] = jnp.where(rid < Tv, g, 0.0)
            return carry
        lax.fori_loop(0, T // Tb, gates, 0)

        def chunk(ci, carry):
            rows = tile_rows(ci, C)
            q = p_ref[b, rows, PC_Q:PC_Q + D_QK_C] * (DK_C ** -0.5)
            k = p_ref[b, rows, PC_K:PC_K + D_QK_C]
            v = p_ref[b, rows, PC_V:PC_V + D_C]
            k = jnp.where((ci * C + rowi) < Tv, k, 0.0)
            gcum = g_scr[rows, :]
            for s, keep in zip(scan_shifts, scan_keep):
                gcum = gcum + jnp.where(keep, pltpu.roll(gcum, s, 0), 0.0)
            g_last = gcum[C - 1:C, :]
            g_mid = gcum[C // 2 - 1:C // 2, :]
            vb = v.astype(BF16)
            qt = q * jnp.exp(gcum - g_mid)
            ktb = (k * jnp.exp(g_mid - gcum)).astype(BF16)
            q4 = jnp.concatenate([jnp.where(hm_k[h], qt, 0.0) for h in range(N_HEADS_C)], axis=0).astype(BF16)
            sc = lax.dot_general(q4, ktb, nt_dims, preferred_element_type=F32)
            r4 = _dot(jnp.where(tril4, sc, 0.0).astype(BF16), vb)
            o = jnp.where(hm_v[0], r4[0:C], 0.0)
            for h in range(1, N_HEADS_C):
                o = o + jnp.where(hm_v[h], r4[h * C:(h + 1) * C], 0.0)
            kd = (k * jnp.exp(g_last - gcum)).astype(BF16)
            upd_t = lax.dot_general(vb, kd, tdims, preferred_element_type=F32)
            s_in = s_scr[...]
            o = o + lax.dot_general((q * jnp.exp(gcum)).astype(BF16), s_in.astype(BF16), nt_dims,
                                    preferred_element_type=F32)
            s_scr[...] = s_in * jnp.exp(g_last) + jnp.where(bd_t, upd_t, 0.0)
            y_ref[b, rows, :] = o
            return carry

        s_scr[...] = jnp.zeros((D_C, D_QK_C), F32)
        for h in range(N_HEADS_C):
            s_scr[h * DV_C:(h + 1) * DV_C, h * DK_C:(h + 1) * DK_C] = s0_ref[b, h]
        lax.fori_loop(0, n_chunks, chunk, 0)
        for h in range(N_HEADS_C):
            snew_ref[b, h] = s_scr[h * DV_C:(h + 1) * DV_C, h * DK_C:(h + 1) * DK_C]

        def finish(ti, carry):
            rows = tile_rows(ti, Tb)
            o = y_ref[b, rows, :]
            rg = p_ref[b, rows, PC_R:PC_R + D_C]
            o2_hi, o2_lo = _split_bf16(o * o)
            ms = (_dot(o2_hi, mseg) + _dot(o2_lo, mseg)) * (1.0 / DV_C)
            y_ref[b, rows, :] = o * lax.rsqrt(ms + LN_EPS) * ng * (rg * _sigmoid(rg))
            return carry
        lax.fori_loop(0, T // Tb, finish, 0)


def _gla(pc3, s0, wg2, bgate, ng, Tv, Bb):
    B, T, _ = pc3.shape
    C = _pick(T, (48, 32, 16, 8))
    Tb = _pick(T, (344, 256, 128, 64, 48, 32, 16, 8))
    const2 = lambda i: (0, 0)
    seq3 = lambda i: (i, 0, 0)
    seq4 = lambda i: (i, 0, 0, 0)
    st = (Bb, N_HEADS_C, DV_C, DK_C)
    y, s_new_t = pl.pallas_call(
        functools.partial(_gla_kernel, T=T, C=C, Tb=Tb, Tv=Tv, Bb=Bb),
        grid=(B // Bb,),
        in_specs=[pl.BlockSpec((Bb, T, PC_W), seq3), pl.BlockSpec(st, seq4),
                  pl.BlockSpec((GATE_RANK, D_QK_C), const2), pl.BlockSpec((1, D_QK_C), const2),
                  pl.BlockSpec((1, D_C), const2)],
        out_specs=[pl.BlockSpec((Bb, T, D_C), seq3), pl.BlockSpec(st, seq4)],
        out_shape=[_sds((B, T, D_C)), _sds((B, N_HEADS_C, DV_C, DK_C))],
        scratch_shapes=[pltpu.VMEM((D_C, D_QK_C), F32), pltpu.VMEM((T, D_QK_C), F32)],
        compiler_params=_cparams("parallel"), name="gla",
    )(pc3, jnp.swapaxes(s0, 2, 3), wg2, bgate, ng)
    return y, jnp.swapaxes(s_new_t, 2, 3)


def _tiles_to_rows(ref, lead, start, rows, stride):
    return jnp.concatenate([ref[lead + (pl.ds(start + s, rows, stride=stride), slice(None))]
                            for s in range(ROW_TILE)], axis=1)


def _rows_to_tiles(ref, lead, rows, val):
    for s in range(ROW_TILE):
        ref[lead + (pl.ds(s, rows, stride=ROW_TILE), slice(None))] = val[:, s * LANES:(s + 1) * LANES]


def _outproj_kernel(ya_ref, yb_ref, yc_ref, x_ref, w_ref, g_ref, b_ref, *rest, alpha, n_blocks):
    h_ref = rest[-1]

    @pl.when(pl.program_id(0) < n_blocks)
    def _():
        y = (_dot(ya_ref[...].astype(BF16), w_ref[0:D_A, :])
             + _dot(yb_ref[...].astype(BF16), w_ref[D_A:D_A + D_B, :])
             + _dot(yc_ref[...].astype(BF16), w_ref[D_A + D_B:D_MODEL, :]))
        h = _ln(alpha * x_ref[...] + y, g_ref[...], b_ref[...])
        _rows_to_tiles(h_ref, (), h.shape[0], h)

    @pl.when(pl.program_id(0) >= n_blocks)
    def _():
        h_ref[...] = jnp.zeros(h_ref.shape, F32)


def _outproj(ya, yb, yc, x, row0, w, g, b, alpha, tm, out_rows, out_row0=0, into=None):
    n = ya.shape[0]
    nb = n // tm
    assert n % tm == 0 and row0 % tm == 0 and out_row0 % tm == 0 and out_rows % tm == 0
    off, out_off = row0 // tm, out_row0 // tm
    grid = nb if into is not None else out_rows // tm
    row = lambda i: (jnp.minimum(i, nb - 1), 0)
    const = lambda i: (0, 0)
    in_specs = [pl.BlockSpec((tm, D_A), row), pl.BlockSpec((tm, D_B), row), pl.BlockSpec((tm, D_C), row),
                pl.BlockSpec((tm, D_MODEL), lambda i: (jnp.minimum(i, nb - 1) + off, 0)),
                pl.BlockSpec((D_MODEL, D_MODEL), const), pl.BlockSpec((1, D_MODEL), const),
                pl.BlockSpec((1, D_MODEL), const)]
    args = [ya, yb, yc, x, w, g, b]
    aliases = {}
    if into is not None:
        in_specs.append(pl.BlockSpec(memory_space=pl.ANY))
        args.append(into)
        aliases = {len(args) - 1: 0}
    return pl.pallas_call(
        functools.partial(_outproj_kernel, alpha=alpha, n_blocks=nb),
        grid=(grid,),
        in_specs=in_specs,
        out_specs=pl.BlockSpec((tm * ROW_TILE, LANES), lambda i: (i + out_off, 0)),
        out_shape=_sds((out_rows * ROW_TILE, LANES)),
        input_output_aliases=aliases,
        compiler_params=_cparams("arbitrary"), name="outproj",
    )(*args)


def _router_kernel(h_ref, wt_ref, b_ref, idx_ref, gate_ref, cnt_ref):
    nt = (((1,), (1,)), ((), ()))
    hh, hl = _split_bf16(_tiles_to_rows(h_ref, (), 0, idx_ref.shape[1], ROW_TILE))
    wh, wl = _split_bf16(wt_ref[...])
    logits = (lax.dot_general(wh, hh, nt, preferred_element_type=F32)
              + lax.dot_general(wh, hl, nt, preferred_element_type=F32)
              + lax.dot_general(wl, hh, nt, preferred_element_type=F32)) + b_ref[...]
    eid = lax.broadcasted_iota(jnp.int32, logits.shape, 0)
    vals = []
    member = jnp.zeros(logits.shape, F32)
    for k in range(TOP_K):
        m = jnp.max(logits, axis=0, keepdims=True)
        sel = jnp.min(jnp.where(logits == m, eid, N_EXPERTS), axis=0, keepdims=True)
        idx_ref[k:k + 1, :] = sel
        vals.append(m)
        hit = eid == sel
        member = jnp.where(hit, 1.0, member)
        logits = jnp.where(hit, -jnp.inf, logits)
    es = [jnp.exp(v - vals[0]) for v in vals]
    tot = es[0] + es[1] + es[2] + es[3]
    for k in range(TOP_K):
        gate_ref[k:k + 1, :] = es[k] / tot
    cnt_ref[...] = jnp.broadcast_to(jnp.sum(member, axis=1, keepdims=True), cnt_ref.shape)


def _router_tile(n):
    return _pick(n, (896, 640, 512, 384, 256, 128))


def _router(h, n, wt, b):
    tm = _router_tile(n)
    nt = n // tm
    return pl.pallas_call(
        _router_kernel,
        grid=(nt,),
        in_specs=[pl.BlockSpec((tm * ROW_TILE, LANES), lambda i: (i, 0)),
                  pl.BlockSpec((N_EXPERTS, D_MODEL), lambda i: (0, 0)),
                  pl.BlockSpec((N_EXPERTS, 1), lambda i: (0, 0))],
        out_specs=[pl.BlockSpec((TOP_K, tm), lambda i: (0, i)), pl.BlockSpec((TOP_K, tm), lambda i: (0, i)),
                   pl.BlockSpec((N_EXPERTS, LANES), lambda i: (0, i))],
        out_shape=[_sds((TOP_K, n), jnp.int32), _sds((TOP_K, n)), _sds((N_EXPERTS, nt * LANES))],
        compiler_params=_cparams("parallel"), name="router",
    )(h, wt, b)


def _dest_kernel(idx_ref, base_ref, dest_ref):
    tm = idx_ref.shape[1]
    eid = lax.broadcasted_iota(jnp.int32, (N_EXPERTS, tm), 0)
    hits = [eid == idx_ref[k:k + 1, :] for k in range(TOP_K)]
    member = jnp.zeros((N_EXPERTS, tm), F32)
    for k in range(TOP_K):
        member = jnp.where(hits[k], 1.0, member)
    earlier = (lax.broadcasted_iota(jnp.int32, (tm, tm), 0) < lax.broadcasted_iota(jnp.int32, (tm, tm), 1))
    rank = _dot(member.astype(BF16), jnp.where(earlier, 1.0, 0.0).astype(BF16))
    pos = base_ref[...] + rank
    for k in range(TOP_K):
        dest_ref[k:k + 1, :] = jnp.sum(jnp.where(hits[k], pos, 0.0), axis=0, keepdims=True).astype(jnp.int32)


def _dest(top_idx, base):
    n = top_idx.shape[1]
    tm = _router_tile(n)
    return pl.pallas_call(
        _dest_kernel,
        grid=(n // tm,),
        in_specs=[pl.BlockSpec((TOP_K, tm), lambda i: (0, i)),
                  pl.BlockSpec((None, N_EXPERTS, 1), lambda i: (i, 0, 0))],
        out_specs=pl.BlockSpec((TOP_K, tm), lambda i: (0, i)),
        out_shape=_sds((TOP_K, n), jnp.int32),
        compiler_params=_cparams("parallel"), name="moe_dest",
    )(top_idx, base)


def _invert_kernel(vend_ref, pend_ref, dest_hbm, inv_ref, buf, sem, *, n, rc, n_rows):
    rows_per_slot = n // LANES
    chunks_per_slot = rows_per_slot // rc

    def chunk(c, carry):
        cp = pltpu.make_async_copy(dest_hbm.at[pl.ds(c * rc, rc), :], buf, sem.at[0])
        cp.start()
        cp.wait()
        k = c // chunks_per_slot
        t0 = (c - k * chunks_per_slot) * (rc * LANES)

        def row(r, carry):
            for l in range(LANES):
                inv_ref[buf[r, l]] = (t0 + r * LANES + l) * TOP_K + k
            return carry

        return lax.fori_loop(0, rc, row, carry)

    lax.fori_loop(0, TOP_K * chunks_per_slot, chunk, 0)

    def fill(lo, hi, ctr):
        def body(r, ctr):
            inv_ref[r] = ctr
            return ctr + 1
        return lax.fori_loop(lo, hi, body, ctr)

    ctr = jnp.int32(TOP_K * n)
    for e in range(N_EXPERTS):
        ctr = fill(vend_ref[e], pend_ref[e], ctr)
    fill(pend_ref[N_EXPERTS - 1], n_rows, ctr)


def _invert(valid_end, pad_end, dest, n_rows):
    k, n = dest.shape
    rc = _pick(n // LANES, (19, 16, 8, 5, 4, 2, 1))
    return pl.pallas_call(
        functools.partial(_invert_kernel, n=n, rc=rc, n_rows=n_rows),
        grid_spec=pltpu.PrefetchScalarGridSpec(
            num_scalar_prefetch=2, grid=(1,),
            in_specs=[pl.BlockSpec(memory_space=pl.ANY)],
            out_specs=pl.BlockSpec(memory_space=pltpu.SMEM),
            scratch_shapes=[pltpu.SMEM((rc, LANES), jnp.int32), pltpu.SemaphoreType.DMA((1,))]),
        out_shape=_sds((n_rows,), jnp.int32),
        compiler_params=_cparams("arbitrary"), name="moe_invert",
    )(valid_end, pad_end, dest.reshape(k * n // LANES, LANES))


def _moe_kernel(be_ref, nu_ref, inv_ref, h_hbm, w1f_ref, b1_ref, w2f_ref, b2_ref, y4_hbm,
                xbuf, xb, obuf, w1_ref, w2_ref, gsem, ssem, *, bm):
    i = pl.program_id(0)
    n_used = nu_ref[0]
    last_blk = pl.num_programs(0) - 1
    n_ff = D_FF // FF_CHUNK

    def tile(r):
        start = r * ROW_TILE
        return pl.ds(start if isinstance(r, int) else pl.multiple_of(start, ROW_TILE), ROW_TILE)

    def gather_row(blk, slot, j, r=None):
        r = inv_ref[blk * bm + j] if r is None else r
        t = lax.shift_right_logical(r, 2)
        pltpu.make_async_copy(h_hbm.at[tile(t), :], xbuf.at[slot, tile(j), :], gsem.at[slot]).start()

    def scatter_row(blk, slot, j, r=None):
        r = inv_ref[blk * bm + j] if r is None else r
        pltpu.make_async_copy(obuf.at[slot, tile(j), :], y4_hbm.at[tile(r), :],
                              ssem.at[slot]).start(priority=SCATTER_DMA_PRIORITY)

    def wait_gather(slot):
        pltpu.make_async_copy(h_hbm.at[pl.ds(0, bm * ROW_TILE), :], xbuf.at[slot], gsem.at[slot]).wait()

    def wait_scatter(slot):
        pltpu.make_async_copy(obuf.at[slot], y4_hbm.at[pl.ds(0, bm * ROW_TILE), :], ssem.at[slot]).wait()

    def loop_rows(fn, blk, slot):
        def body(j, c):
            fn(blk, slot, j)
            return c
        lax.fori_loop(0, bm, body, 0)

    def issue_rows(fn, blk, slot):
        for j0 in range(0, bm, DMA_GROUP):
            js = range(j0, j0 + DMA_GROUP)
            ids = [inv_ref[blk * bm + j] for j in js]
            for j, r in zip(js, ids):
                fn(blk, slot, j, r)

    def step():
        slot = i % 2
        issue_rows(gather_row, jnp.minimum(i + 1, last_blk), 1 - slot)
        wait_gather(slot)
        xb[...] = _tiles_to_rows(xbuf, (slot,), 0, bm, ROW_TILE).astype(BF16)
        acc = jnp.zeros((bm, D_MODEL), F32)
        for c in range(n_ff):
            lo = c * FF_CHUNK
            x = xb[...]
            hg = _dot(x, w1_ref[:, lo:lo + FF_CHUNK]) + b1_ref[:, lo:lo + FF_CHUNK]
            hl = _dot(x, w1_ref[:, D_FF + lo:D_FF + lo + FF_CHUNK]) + b1_ref[:, D_FF + lo:D_FF + lo + FF_CHUNK]
            gate = jnp.minimum(hg, SWIGLU_LIMIT)
            lin = jnp.clip(hl, -SWIGLU_LIMIT, SWIGLU_LIMIT)
            act = gate * _sigmoid(SWIGLU_ALPHA * gate) * (lin + 1.0)
            acc = acc + _dot(act.astype(BF16), w2_ref[lo:lo + FF_CHUNK, :])

        @pl.when(i >= 2)
        def _():
            wait_scatter(slot)
        _rows_to_tiles(obuf, (slot,), bm, acc + b2_ref[...])
        issue_rows(scatter_row, i, slot)

    new_expert = jnp.logical_or(i == 0, be_ref[i] != be_ref[jnp.maximum(i - 1, 0)])

    @pl.when(jnp.logical_and(new_expert, i < n_used))
    def _():
        def cast_rows(r, c):
            rows = pl.ds(pl.multiple_of(r * W_CAST_ROWS, W_CAST_ROWS), W_CAST_ROWS)
            w1_ref[rows, :] = w1f_ref[rows, :].astype(BF16)
            w2_ref[rows, :] = w2f_ref[rows, :].astype(BF16)
            return c
        lax.fori_loop(0, D_MODEL // W_CAST_ROWS, cast_rows, 0)

    @pl.when(jnp.logical_and(i == 0, n_used > 0))
    def _():
        loop_rows(gather_row, 0, 0)

    @pl.when(i < n_used)
    def _():
        step()

    @pl.when(i == n_used - 1)
    def _():
        slot = i % 2
        wait_scatter(slot)

        @pl.when(i >= 1)
        def _():
            wait_scatter(1 - slot)
        wait_gather(1 - slot)
        obuf[0] = jnp.zeros((bm * ROW_TILE, LANES), F32)

        def zero_block(blk, c):
            rows = bm * ROW_TILE
            cp = pltpu.make_async_copy(obuf.at[0], y4_hbm.at[pl.ds(pl.multiple_of(blk * rows, rows), rows), :],
                                       ssem.at[0])
            cp.start()
            cp.wait()
            return c
        lax.fori_loop(n_used, last_blk + 1, zero_block, 0)


def _moe(block_e, n_used, inv, h, layer, w1, b1, w2, b2, bm):
    nblk = block_e.shape[0]
    wmap = lambda i, be, nu, iv: (layer, be[i], 0, 0)
    return pl.pallas_call(
        functools.partial(_moe_kernel, bm=bm),
        grid_spec=pltpu.PrefetchScalarGridSpec(
            num_scalar_prefetch=3, grid=(nblk,),
            in_specs=[pl.BlockSpec(memory_space=pl.ANY),
                      pl.BlockSpec((None, None, D_MODEL, 2 * D_FF), wmap),
                      pl.BlockSpec((None, None, 1, 2 * D_FF), wmap),
                      pl.BlockSpec((None, None, D_FF, D_MODEL), wmap),
                      pl.BlockSpec((None, None, 1, D_MODEL), wmap)],
            out_specs=pl.BlockSpec(memory_space=pl.ANY),
            scratch_shapes=[pltpu.VMEM((2, bm * ROW_TILE, LANES), F32), pltpu.VMEM((bm, D_MODEL), BF16),
                            pltpu.VMEM((2, bm * ROW_TILE, LANES), F32),
                            pltpu.VMEM((D_MODEL, 2 * D_FF), BF16), pltpu.VMEM((D_FF, D_MODEL), BF16),
                            pltpu.SemaphoreType.DMA((2,)), pltpu.SemaphoreType.DMA((2,))]),
        out_shape=_sds((nblk * bm * ROW_TILE, LANES)),
        compiler_params=_cparams("arbitrary"), name="moe_experts",
    )(block_e, n_used, inv, h, w1, b1, w2, b2)


def _combine_kernel(y4_ref, gates_ref, h_ref, g_ref, b_ref, out_ref, *, alpha):
    tm = out_ref.shape[0]
    gates = gates_ref[...]
    acc = alpha * _tiles_to_rows(h_ref, (), 0, tm, ROW_TILE)
    for k in range(TOP_K):
        acc = acc + _tiles_to_rows(y4_ref, (), k * ROW_TILE, tm, TOP_K * ROW_TILE) * gates[:, k:k + 1]
    out_ref[...] = _ln(acc, g_ref[...], b_ref[...])


def _combine(y4, gates_t, h, n, g, b, alpha):
    tm = _pick(n, (224, 128, 64, 32, 16, 8))
    row = lambda i: (i, 0)
    const = lambda i: (0, 0)
    return pl.pallas_call(
        functools.partial(_combine_kernel, alpha=alpha),
        grid=(n // tm,),
        in_specs=[pl.BlockSpec((TOP_K * tm * ROW_TILE, LANES), row), pl.BlockSpec((tm, TOP_K), row),
                  pl.BlockSpec((tm * ROW_TILE, LANES), row),
                  pl.BlockSpec((1, D_MODEL), const), pl.BlockSpec((1, D_MODEL), const)],
        out_specs=pl.BlockSpec((tm, D_MODEL), row),
        out_shape=_sds((n, D_MODEL)),
        compiler_params=_cparams("parallel"), name="moe_combine",
    )(y4, gates_t, h, g, b)


def _group_layout(cnt_tiles, nblk, bm):
    cnt = cnt_tiles.astype(jnp.int32)
    counts = jnp.sum(cnt, axis=1)
    padded = (counts + bm - 1) // bm * bm
    pad_end = jnp.cumsum(padded)
    pad_start = pad_end - padded
    base = pad_start[:, None] + jnp.cumsum(cnt, axis=1) - cnt
    starts = jnp.arange(nblk, dtype=jnp.int32) * bm
    block_e = jnp.minimum(jnp.sum((pad_end[None, :] <= starts[:, None]).astype(jnp.int32), axis=1),
                          N_EXPERTS - 1).astype(jnp.int32)
    n_used = (pad_end[-1] // bm).astype(jnp.int32).reshape(1)
    valid_end = (pad_start + counts).astype(jnp.int32)
    return base.T.astype(F32)[:, :, None], valid_end, pad_end.astype(jnp.int32), block_e, n_used


def _pack_w_in(w):
    o = 0
    xa, o = w[:, o:o + D_A], o + D_A
    ga, o = w[:, o:o + D_A], o + D_A
    vb, o = w[:, o:o + D_B], o + D_B
    gb, o = w[:, o:o + D_B], o + D_B
    q, o = w[:, o:o + D_QK_C], o + D_QK_C
    k, o = w[:, o:o + D_QK_C], o + D_QK_C
    v, o = w[:, o:o + D_C], o + D_C
    r, o = w[:, o:o + D_C], o + D_C
    z = w[:, o:o + GATE_RANK]
    zq = jnp.zeros((w.shape[0], PC_K - PC_Z - GATE_RANK), w.dtype)
    zk = jnp.zeros((w.shape[0], PC_V - PC_K - D_QK_C), w.dtype)
    return jnp.concatenate([xa, ga, vb, gb, q, z, zq, k, zk, v, r], axis=1).astype(BF16)


def _block_diag(w):
    eye = jnp.eye(N_BLK_A, dtype=w.dtype)
    return jnp.einsum("hij,hg->higj", w, eye).reshape(D_A, D_A)


def kernel(x_prompt, x_sample, state_conv_a, state_rglru, state_conv_b, state_gla, meta_tokens, ln0_g, ln0_b,
           w_in, conv_a_w, conv_a_b, w_rg, b_rg, w_ig, b_ig, lru_lambda, conv_b_w, conv_b_b, ln_b_g, ln_b_b,
           w_gate2, b_gate, gla_norm_g, w_out, ln1_g, ln1_b, router_w, router_b, moe_w1, moe_b1, moe_w2, moe_b2,
           ln2_g, ln2_b):
    bp, seq, _ = x_prompt.shape
    bs, dseq, _ = x_sample.shape
    depth = w_in.shape[0]
    tp = N_META + seq
    ts = SAMPLE_PAD_T
    np_rows = bp * tp
    ns_rows = bs * dseq
    n = np_rows + ns_rows
    alpha = (2 * depth) ** 0.25
    row = lambda a: a.reshape(1, -1)

    meta = jnp.broadcast_to(meta_tokens[None], (bp, N_META, D_MODEL))
    xp_in = jnp.concatenate([meta, x_prompt], axis=1).reshape(np_rows, D_MODEL)
    xs_in = jnp.pad(x_sample, ((0, 0), (0, ts - dseq), (0, 0))).reshape(bs * ts, D_MODEL)
    zeros_p = (jnp.zeros((bp, CONV_A - 1, D_A), F32), jnp.zeros((bp, 1, D_A), F32),
               jnp.zeros((bp, CONV_B - 1, D_B), F32), jnp.zeros((bp, N_HEADS_C, DK_C, DV_C), F32))
    sb = _pick(bs, (8, 4, 2, 1))
    tm_p = _pick(np_rows, (384, 512, 256, 128, 64, 32, 16, 8))
    tm_s = _pick(math.gcd(ns_rows, np_rows), (128, 64, 32, 16, 8))
    nblk = -(-(TOP_K * n) // MOE_BM) + N_EXPERTS
    while (nblk * MOE_BM // TOP_K) % math.lcm(tm_p, tm_s):
        nblk += 1

    new_p = ([], [], [], [])
    new_s = ([], [], [], [])
    x_all = None
    for l in range(depth):
        w_packed = _pack_w_in(w_in[l])
        wg = jnp.concatenate([_block_diag(w_rg[l]), _block_diag(w_ig[l])], axis=1).astype(BF16)
        bg = jnp.concatenate([b_rg[l], b_ig[l]]).reshape(1, -1)
        w_out_b = w_out[l].astype(BF16)
        ng = row(jnp.tile(gla_norm_g[l], N_HEADS_C))
        first = l == 0
        if first:
            pa_p, pb_p, pc_p, xn_p = _inproj(xp_in, 0, np_rows, row(ln0_g), row(ln0_b), w_packed, True)
            pa_s, pb_s, pc_s, xn_s = _inproj(xs_in, 0, bs * ts, row(ln0_g), row(ln0_b), w_packed, True)
            res_p, res_p_row0 = xn_p, 0
        else:
            pa_p, pb_p, pc_p = _inproj(x_all, 0, np_rows, row(ln0_g), row(ln0_b), w_packed, False)
            xn_s = jnp.pad(x_all[np_rows:].reshape(bs, dseq, D_MODEL),
                           ((0, 0), (0, ts - dseq), (0, 0))).reshape(bs * ts, D_MODEL)
            pa_s, pb_s, pc_s = _inproj(xn_s, 0, bs * ts, row(ln0_g), row(ln0_b), w_packed, False)
            res_p, res_p_row0 = x_all, 0

        outs = []
        for (pa, pb, pc, nb, t, tv, bb, st) in (
                (pa_p, pb_p, pc_p, bp, tp, tp, 1, zeros_p),
                (pa_s, pb_s, pc_s, bs, ts, dseq, sb,
                 (state_conv_a[l], state_rglru[l].reshape(bs, 1, D_A), state_conv_b[l], state_gla[l]))):
            ya, ca_new, h_last = _rglru(pa.reshape(nb, t, PA_W), st[0], st[1], conv_a_w[l], row(conv_a_b[l]),
                                        wg, bg, row(lru_lambda[l]), tv, bb)
            yb, cb_new = _convb(pb.reshape(nb, t, PB_W), st[2], conv_b_w[l], row(conv_b_b[l]),
                                row(ln_b_g[l]), row(ln_b_b[l]), tv, bb)
            yc, s_new = _gla(pc.reshape(nb, t, PC_W), st[3], w_gate2[l], row(b_gate[l]), ng, tv, bb)
            outs.append((ya.reshape(nb * t, D_A), yb.reshape(nb * t, D_B), yc.reshape(nb * t, D_C),
                         ca_new, h_last.reshape(nb, D_A), cb_new, s_new))
        (ya_p, yb_p, yc_p, *st_p), (ya_s, yb_s, yc_s, *st_s) = outs
        for j in range(4):
            new_p[j].append(st_p[j])
            new_s[j].append(st_s[j])

        valid = lambda a: a.reshape(bs, ts, -1)[:, :dseq].reshape(ns_rows, -1)
        h_all = _outproj(ya_p, yb_p, yc_p, res_p, res_p_row0, w_out_b, row(ln1_g[l]), row(ln1_b[l]), alpha,
                         tm_p, nblk * MOE_BM // TOP_K)
        h_all = _outproj(valid(ya_s), valid(yb_s), valid(yc_s), valid(xn_s), 0, w_out_b, row(ln1_g[l]),
                         row(ln1_b[l]), alpha, tm_s, nblk * MOE_BM // TOP_K, out_row0=np_rows, into=h_all)

        top_idx, gates, cnt = _router(h_all, n, router_w[l].T, router_b[l].reshape(-1, 1))
        base, valid_end, pad_end, block_e, n_used = _group_layout(cnt[:, ::LANES], nblk, MOE_BM)
        dest = _dest(top_idx, base)
        inv = _invert(valid_end, pad_end, dest, nblk * MOE_BM)
        y4 = _moe(block_e, n_used, inv, h_all, l, moe_w1, moe_b1[:, :, None, :], moe_w2, moe_b2[:, :, None, :],
                  MOE_BM)
        x_all = _combine(y4, gates.T, h_all, n, row(ln2_g[l]), row(ln2_b[l]), alpha)

    y_p = x_all[:np_rows].reshape(bp, tp, D_MODEL)[:, N_META:]
    y_s = x_all[np_rows:].reshape(bs, dseq, D_MODEL)
    return (y_p, y_s,
            jnp.stack(new_p[0]), jnp.stack(new_p[1]), jnp.stack(new_p[2]), jnp.stack(new_p[3]),
            jnp.stack(new_s[0]), jnp.stack(new_s[1]), jnp.stack(new_s[2]), jnp.stack(new_s[3]))
```

```python
import functools
import math

import jax
import jax.numpy as jnp
from jax import lax
from jax.experimental import pallas as pl
from jax.experimental.pallas import tpu as pltpu

F32 = jnp.float32
BF16 = jnp.bfloat16

D_MODEL = 1024
N_META = 16
D_A = 384
D_B = 256
D_C = 384
N_BLK_A = 8
BLK_A = D_A // N_BLK_A
CONV_A = 4
RG_C = 8.0
CONV_B = 31
N_HEADS_C = 4
DV_C = D_C // N_HEADS_C
DK_C = DV_C // 2
D_QK_C = N_HEADS_C * DK_C
GATE_RANK = 16
GATE_TAU = 16.0
N_EXPERTS = 32
TOP_K = 4
D_FF = D_MODEL
SWIGLU_LIMIT = 7.0
SWIGLU_ALPHA = 1.702
LN_EPS = 1e-5

PA_W = 2 * D_A
PB_W = 2 * D_B
PC_Q, PC_Z, PC_K, PC_V, PC_R, PC_W = 0, 192, 256, 512, 896, 1280
P_W = PA_W + PB_W + PC_W

SUBLANES = 8
LANES = 128
ROW_TILE = D_MODEL // LANES
VMEM_LIMIT_BYTES = 56 * 1024 * 1024
MOE_BM = 256
FF_CHUNK = 256
DMA_GROUP = 8
W_CAST_ROWS = 128
SAMPLE_PAD_T = 8


def _cparams(*sem):
    return pltpu.CompilerParams(dimension_semantics=sem, vmem_limit_bytes=VMEM_LIMIT_BYTES)


def _sds(shape, dtype=F32):
    return jax.ShapeDtypeStruct(shape, dtype)


def _pick(n, prefs):
    for p in prefs:
        if n % p == 0:
            return p
    raise ValueError(f"no tile for {n} in {prefs}")


def _ln(x, g, b):
    mu = jnp.mean(x, axis=-1, keepdims=True)
    xc = x - mu
    var = jnp.mean(xc * xc, axis=-1, keepdims=True)
    return xc * lax.rsqrt(var + LN_EPS) * g + b


def _sigmoid(x):
    return 1.0 / (1.0 + jnp.exp(-x))


def _split_bf16(x):
    hi = x.astype(BF16)
    lo = (x - hi.astype(F32)).astype(BF16)
    return hi, lo


def _dot(a, b):
    return jnp.dot(a, b, preferred_element_type=F32)


def _inproj_kernel(x_ref, g_ref, b_ref, w_ref, pa_ref, pb_ref, pc_ref, *maybe_xn, apply_ln):
    x = x_ref[...]
    if apply_ln:
        x = _ln(x, g_ref[...], b_ref[...])
        maybe_xn[0][...] = x
    xb = x.astype(BF16)
    pa_ref[...] = _dot(xb, w_ref[:, 0:PA_W])
    pb_ref[...] = _dot(xb, w_ref[:, PA_W:PA_W + PB_W])
    pc_ref[...] = _dot(xb, w_ref[:, PA_W + PB_W:P_W])


def _inproj(x, row0, nrows, ln_g, ln_b, w_packed, apply_ln):
    tm = _pick(nrows, (384, 512, 256, 128, 64, 32, 16, 8))
    while row0 % tm:
        tm //= 2
    off = row0 // tm
    const = lambda i: (0, 0)
    row = lambda i: (i, 0)
    out_shape = [_sds((nrows, PA_W)), _sds((nrows, PB_W)), _sds((nrows, PC_W))]
    out_specs = [pl.BlockSpec((tm, PA_W), row), pl.BlockSpec((tm, PB_W), row), pl.BlockSpec((tm, PC_W), row)]
    if apply_ln:
        out_shape.append(_sds((nrows, D_MODEL)))
        out_specs.append(pl.BlockSpec((tm, D_MODEL), row))
    return pl.pallas_call(
        functools.partial(_inproj_kernel, apply_ln=apply_ln),
        grid=(nrows // tm,),
        in_specs=[pl.BlockSpec((tm, D_MODEL), lambda i: (i + off, 0)),
                  pl.BlockSpec((1, D_MODEL), const), pl.BlockSpec((1, D_MODEL), const),
                  pl.BlockSpec((D_MODEL, P_W), const)],
        out_specs=out_specs, out_shape=out_shape,
        compiler_params=_cparams("parallel"), name="inproj",
    )(x, ln_g, ln_b, w_packed)


def _rglru_kernel(p_ref, cbuf_ref, h0_ref, cw_ref, cb_ref, wg_ref, bg_ref, lam_ref,
                  y_ref, cnew_ref, hlast_ref, xp_scr, a_scr, h_scr, *, T, Tc, Tv, Bb):
    lam = lam_ref[...]
    softplus_neg = jnp.maximum(-lam, 0.0) + jnp.log1p(jnp.exp(-jnp.abs(lam)))
    c_decay = -RG_C * softplus_neg
    cw = cw_ref[...]
    cb = cb_ref[...]
    bg = bg_ref[...]
    sub = lax.broadcasted_iota(jnp.int32, (Tc, D_A), 0) % SUBLANES
    halo = SUBLANES - (CONV_A - 1)
    for b in range(Bb):
        xp_scr[halo:SUBLANES, :] = cbuf_ref[b]
        xp_scr[SUBLANES:SUBLANES + T, :] = p_ref[b, :, 0:D_A]
        cnew_ref[b] = xp_scr[halo + Tv:SUBLANES + Tv, :]

        def chunk(ci, h_b):
            r0 = pl.multiple_of(ci * Tc, SUBLANES)
            win = xp_scr[pl.ds(r0, Tc + SUBLANES), :]
            xc = cb + cw[0:1] * win[halo:halo + Tc]
            for j in range(1, CONV_A):
                xc = xc + cw[j:j + 1] * win[halo + j:halo + j + Tc]
            gates = _dot(xc.astype(BF16), wg_ref[...]) + bg
            r = _sigmoid(gates[:, 0:D_A])
            i = _sigmoid(gates[:, D_A:2 * D_A])
            log_a = c_decay * r
            a = jnp.exp(log_a)
            u = jnp.sqrt(1.0 - a * a) * (i * xc)
            for s in (1, 2, 4):
                keep = sub >= s
                a_prev = pltpu.roll(a, s, 0)
                u_prev = pltpu.roll(u, s, 0)
                u = jnp.where(keep, a * u_prev + u, u)
                a = jnp.where(keep, a * a_prev, a)
            a_scr[...] = a
            h_scr[pl.ds(r0, Tc), :] = u

            def group(gi, h_b):
                c0 = pl.multiple_of(gi * SUBLANES, SUBLANES)
                g0 = pl.multiple_of(r0 + gi * SUBLANES, SUBLANES)
                h8 = a_scr[pl.ds(c0, SUBLANES), :] * h_b + h_scr[pl.ds(g0, SUBLANES), :]
                h_scr[pl.ds(g0, SUBLANES), :] = h8
                return jnp.broadcast_to(h8[SUBLANES - 1:SUBLANES, :], (SUBLANES, D_A))

            h_b = lax.fori_loop(0, Tc // SUBLANES, group, h_b)
            ga = p_ref[b, pl.ds(r0, Tc), D_A:2 * D_A]
            gelu = 0.5 * ga * (1.0 + jnp.tanh(0.7978845608028654 * (ga + 0.044715 * ga * ga * ga)))
            y_ref[b, pl.ds(r0, Tc), :] = h_scr[pl.ds(r0, Tc), :] * gelu
            return h_b

        h_b = jnp.broadcast_to(h0_ref[b], (SUBLANES, D_A))
        lax.fori_loop(0, T // Tc, chunk, h_b)
        hlast_ref[b] = h_scr[Tv - 1:Tv, :]


def _rglru(pa3, cbuf, h0, cw, cb, wg, bg, lam, Tv, Bb):
    B, T, _ = pa3.shape
    Tc = _pick(T, (344, 256, 128, 64, 48, 32, 16, 8))
    const2 = lambda i: (0, 0)
    seq3 = lambda i: (i, 0, 0)
    return pl.pallas_call(
        functools.partial(_rglru_kernel, T=T, Tc=Tc, Tv=Tv, Bb=Bb),
        grid=(B // Bb,),
        in_specs=[pl.BlockSpec((Bb, T, PA_W), seq3), pl.BlockSpec((Bb, CONV_A - 1, D_A), seq3),
                  pl.BlockSpec((Bb, 1, D_A), seq3), pl.BlockSpec((CONV_A, D_A), const2),
                  pl.BlockSpec((1, D_A), const2), pl.BlockSpec((D_A, 2 * D_A), const2),
                  pl.BlockSpec((1, 2 * D_A), const2), pl.BlockSpec((1, D_A), const2)],
        out_specs=[pl.BlockSpec((Bb, T, D_A), seq3), pl.BlockSpec((Bb, CONV_A - 1, D_A), seq3),
                   pl.BlockSpec((Bb, 1, D_A), seq3)],
        out_shape=[_sds((B, T, D_A)), _sds((B, CONV_A - 1, D_A)), _sds((B, 1, D_A))],
        scratch_shapes=[pltpu.VMEM((T + 2 * SUBLANES, D_A), F32), pltpu.VMEM((Tc, D_A), F32),
                        pltpu.VMEM((T, D_A), F32)],
        compiler_params=_cparams("parallel"), name="rglru",
    )(pa3, cbuf, h0, cw, cb, wg, bg, lam)


B_HALO = 32


def _convb_kernel(p_ref, buf_ref, w_ref, cb_ref, g_ref, b_ref, y_ref, bnew_ref, u_scr, *, T, Tc, Tv, Bb):
    w = w_ref[...]
    cb = cb_ref[...]
    g = g_ref[...]
    bb = b_ref[...]
    first = B_HALO - (CONV_B - 1)
    for b in range(Bb):
        u_scr[0:first, :] = jnp.zeros((first, D_B), F32)
        u_scr[first:B_HALO, :] = buf_ref[b]
        u_scr[B_HALO:B_HALO + T, :] = p_ref[b, :, 0:D_B] * _sigmoid(p_ref[b, :, D_B:2 * D_B])
        bnew_ref[b] = u_scr[first + Tv:B_HALO + Tv, :]

        def chunk(ci, carry):
            r0 = pl.multiple_of(ci * Tc, SUBLANES)
            win = u_scr[pl.ds(r0, Tc + B_HALO), :]
            shifted = [win] + [pltpu.roll(win, Tc + B_HALO - s, 0) for s in range(1, SUBLANES)]
            acc = cb
            for j in range(CONV_B):
                a, s = divmod(first + j, SUBLANES)
                acc = acc + w[j:j + 1] * shifted[s][a * SUBLANES:a * SUBLANES + Tc]
            yn = _ln(acc, g, bb)
            y_ref[b, pl.ds(r0, Tc), :] = yn * _sigmoid(yn)
            return carry

        lax.fori_loop(0, T // Tc, chunk, 0)


def _convb(pb3, buf, w, cb, g, b, Tv, Bb):
    B, T, _ = pb3.shape
    Tc = _pick(T, (48, 32, 16, 8))
    const2 = lambda i: (0, 0)
    seq3 = lambda i: (i, 0, 0)
    return pl.pallas_call(
        functools.partial(_convb_kernel, T=T, Tc=Tc, Tv=Tv, Bb=Bb),
        grid=(B // Bb,),
        in_specs=[pl.BlockSpec((Bb, T, PB_W), seq3), pl.BlockSpec((Bb, CONV_B - 1, D_B), seq3),
                  pl.BlockSpec((CONV_B, D_B), const2), pl.BlockSpec((1, D_B), const2),
                  pl.BlockSpec((1, D_B), const2), pl.BlockSpec((1, D_B), const2)],
        out_specs=[pl.BlockSpec((Bb, T, D_B), seq3), pl.BlockSpec((Bb, CONV_B - 1, D_B), seq3)],
        out_shape=[_sds((B, T, D_B)), _sds((B, CONV_B - 1, D_B))],
        scratch_shapes=[pltpu.VMEM((T + B_HALO, D_B), F32)],
        compiler_params=_cparams("parallel"), name="convb",
    )(pb3, buf, w, cb, g, b)


def _gla_kernel(p_ref, s0_ref, wg2_ref, bgate_ref, ng_ref, y_ref, snew_ref, s_scr, g_scr, *, T, C, Tb, Tv, Bb):
    ri = lax.broadcasted_iota(jnp.int32, (C, C), 0)
    ci_ = lax.broadcasted_iota(jnp.int32, (C, C), 1)
    tril = ri >= ci_
    lane_k = lax.broadcasted_iota(jnp.int32, (1, D_QK_C), 1)
    lane_v = lax.broadcasted_iota(jnp.int32, (1, D_C), 1)
    hm_k = [(lane_k >= h * DK_C) & (lane_k < (h + 1) * DK_C) for h in range(N_HEADS_C)]
    hm_v = [(lane_v >= h * DV_C) & (lane_v < (h + 1) * DV_C) for h in range(N_HEADS_C)]
    rs = lax.broadcasted_iota(jnp.int32, (D_C, D_QK_C), 0)
    cs = lax.broadcasted_iota(jnp.int32, (D_C, D_QK_C), 1)
    bd_t = (rs >= 0) & (rs < 0)
    for h in range(N_HEADS_C):
        bd_t = bd_t | ((rs >= h * DV_C) & (rs < (h + 1) * DV_C) & (cs >= h * DK_C) & (cs < (h + 1) * DK_C))
    rm = lax.broadcasted_iota(jnp.int32, (D_C, D_C), 0)
    cm = lax.broadcasted_iota(jnp.int32, (D_C, D_C), 1)
    seg = (rm >= 0) & (rm < 0)
    for h in range(N_HEADS_C):
        seg = seg | ((rm >= h * DV_C) & (rm < (h + 1) * DV_C) & (cm >= h * DV_C) & (cm < (h + 1) * DV_C))
    mseg = jnp.where(seg, 1.0, 0.0).astype(BF16)
    wg2 = wg2_ref[...].astype(BF16)
    bgate = bgate_ref[...]
    ng = ng_ref[...]
    rowi = lax.broadcasted_iota(jnp.int32, (C, 1), 0)
    tdims = (((0,), (0,)), ((), ()))

    n_chunks = T // C
    nt_dims = (((1,), (1,)), ((), ()))
    tril4 = jnp.concatenate([tril] * N_HEADS_C, axis=0)
    scan_shifts = [s for s in (1, 2, 4, 8, 16, 32) if s < C]
    scan_keep = [rowi >= s for s in scan_shifts]

    def tile_rows(ti, tb):
        r0 = ti * tb
        return pl.ds(r0 if isinstance(ti, int) else pl.multiple_of(r0, tb), tb)

    for b in range(Bb):
        def gates(ti, carry):
            rows = tile_rows(ti, Tb)
            z = p_ref[b, rows, PC_Z:PC_Z + GATE_RANK]
            pre = _dot(z.astype(BF16), wg2) + bgate
            g = (jnp.minimum(pre, 0.0) - jnp.log1p(jnp.exp(-jnp.abs(pre)))) * (1.0 / GATE_TAU)
            rid = ti * Tb + lax.broadcasted_iota(jnp.int32, (Tb, 1), 0)
            g_scr[rows, :] = jnp.where(rid < Tv, g, 0.0)
            return carry
        lax.fori_loop(0, T // Tb, gates, 0)

        def chunk(ci, carry):
            rows = tile_rows(ci, C)
            q = p_ref[b, rows, PC_Q:PC_Q + D_QK_C] * (DK_C ** -0.5)
            k = p_ref[b, rows, PC_K:PC_K + D_QK_C]
            v = p_ref[b, rows, PC_V:PC_V + D_C]
            k = jnp.where((ci * C + rowi) < Tv, k, 0.0)
            gcum = g_scr[rows, :]
            for s, keep in zip(scan_shifts, scan_keep):
                gcum = gcum + jnp.where(keep, pltpu.roll(gcum, s, 0), 0.0)
            g_last = gcum[C - 1:C, :]
            g_mid = gcum[C // 2 - 1:C // 2, :]
            vb = v.astype(BF16)
            qt = q * jnp.exp(gcum - g_mid)
            ktb = (k * jnp.exp(g_mid - gcum)).astype(BF16)
            q4 = jnp.concatenate([jnp.where(hm_k[h], qt, 0.0) for h in range(N_HEADS_C)], axis=0).astype(BF16)
            sc = lax.dot_general(q4, ktb, nt_dims, preferred_element_type=F32)
            r4 = _dot(jnp.where(tril4, sc, 0.0).astype(BF16), vb)
            o = jnp.where(hm_v[0], r4[0:C], 0.0)
            for h in range(1, N_HEADS_C):
                o = o + jnp.where(hm_v[h], r4[h * C:(h + 1) * C], 0.0)
            kd = (k * jnp.exp(g_last - gcum)).astype(BF16)
            upd_t = lax.dot_general(vb, kd, tdims, preferred_element_type=F32)
            s_in = s_scr[...]
            o = o + lax.dot_general((q * jnp.exp(gcum)).astype(BF16), s_in.astype(BF16), nt_dims,
                                    preferred_element_type=F32)
            s_scr[...] = s_in * jnp.exp(g_last) + jnp.where(bd_t, upd_t, 0.0)
            y_ref[b, rows, :] = o
            return carry

        s_scr[...] = jnp.zeros((D_C, D_QK_C), F32)
        for h in range(N_HEADS_C):
            s_scr[h * DV_C:(h + 1) * DV_C, h * DK_C:(h + 1) * DK_C] = s0_ref[b, h]
        lax.fori_loop(0, n_chunks, chunk, 0)
        for h in range(N_HEADS_C):
            snew_ref[b, h] = s_scr[h * DV_C:(h + 1) * DV_C, h * DK_C:(h + 1) * DK_C]

        def finish(ti, carry):
            rows = tile_rows(ti, Tb)
            o = y_ref[b, rows, :]
            rg = p_ref[b, rows, PC_R:PC_R + D_C]
            o2_hi, o2_lo = _split_bf16(o * o)
            ms = (_dot(o2_hi, mseg) + _dot(o2_lo, mseg)) * (1.0 / DV_C)
            y_ref[b, rows, :] = o * lax.rsqrt(ms + LN_EPS) * ng * (rg * _sigmoid(rg))
            return carry
        lax.fori_loop(0, T // Tb, finish, 0)


def _gla(pc3, s0, wg2, bgate, ng, Tv, Bb):
    B, T, _ = pc3.shape
    C = _pick(T, (48, 32, 16, 8))
    Tb = _pick(T, (344, 256, 128, 64, 48, 32, 16, 8))
    const2 = lambda i: (0, 0)
    seq3 = lambda i: (i, 0, 0)
    seq4 = lambda i: (i, 0, 0, 0)
    st = (Bb, N_HEADS_C, DV_C, DK_C)
    y, s_new_t = pl.pallas_call(
        functools.partial(_gla_kernel, T=T, C=C, Tb=Tb, Tv=Tv, Bb=Bb),
        grid=(B // Bb,),
        in_specs=[pl.BlockSpec((Bb, T, PC_W), seq3), pl.BlockSpec(st, seq4),
                  pl.BlockSpec((GATE_RANK, D_QK_C), const2), pl.BlockSpec((1, D_QK_C), const2),
                  pl.BlockSpec((1, D_C), const2)],
        out_specs=[pl.BlockSpec((Bb, T, D_C), seq3), pl.BlockSpec(st, seq4)],
        out_shape=[_sds((B, T, D_C)), _sds((B, N_HEADS_C, DV_C, DK_C))],
        scratch_shapes=[pltpu.VMEM((D_C, D_QK_C), F32), pltpu.VMEM((T, D_QK_C), F32)],
        compiler_params=_cparams("parallel"), name="gla",
    )(pc3, jnp.swapaxes(s0, 2, 3), wg2, bgate, ng)
    return y, jnp.swapaxes(s_new_t, 2, 3)


def _tiles_to_rows(ref, lead, start, rows, stride):
    return jnp.concatenate([ref[lead + (pl.ds(start + s, rows, stride=stride), slice(None))]
                            for s in range(ROW_TILE)], axis=1)


def _rows_to_tiles(ref, lead, rows, val):
    for s in range(ROW_TILE):
        ref[lead + (pl.ds(s, rows, stride=ROW_TILE), slice(None))] = val[:, s * LANES:(s + 1) * LANES]


def _outproj_kernel(ya_ref, yb_ref, yc_ref, x_ref, w_ref, g_ref, b_ref, *rest, alpha, n_blocks):
    h_ref = rest[-1]

    @pl.when(pl.program_id(0) < n_blocks)
    def _():
        y = (_dot(ya_ref[...].astype(BF16), w_ref[0:D_A, :])
             + _dot(yb_ref[...].astype(BF16), w_ref[D_A:D_A + D_B, :])
             + _dot(yc_ref[...].astype(BF16), w_ref[D_A + D_B:D_MODEL, :]))
        h = _ln(alpha * x_ref[...] + y, g_ref[...], b_ref[...])
        _rows_to_tiles(h_ref, (), h.shape[0], h)

    @pl.when(pl.program_id(0) >= n_blocks)
    def _():
        h_ref[...] = jnp.zeros(h_ref.shape, F32)


def _outproj(ya, yb, yc, x, row0, w, g, b, alpha, tm, out_rows, out_row0=0, into=None):
    n = ya.shape[0]
    nb = n // tm
    assert n % tm == 0 and row0 % tm == 0 and out_row0 % tm == 0 and out_rows % tm == 0
    off, out_off = row0 // tm, out_row0 // tm
    grid = nb if into is not None else out_rows // tm
    row = lambda i: (jnp.minimum(i, nb - 1), 0)
    const = lambda i: (0, 0)
    in_specs = [pl.BlockSpec((tm, D_A), row), pl.BlockSpec((tm, D_B), row), pl.BlockSpec((tm, D_C), row),
                pl.BlockSpec((tm, D_MODEL), lambda i: (jnp.minimum(i, nb - 1) + off, 0)),
                pl.BlockSpec((D_MODEL, D_MODEL), const), pl.BlockSpec((1, D_MODEL), const),
                pl.BlockSpec((1, D_MODEL), const)]
    args = [ya, yb, yc, x, w, g, b]
    aliases = {}
    if into is not None:
        in_specs.append(pl.BlockSpec(memory_space=pl.ANY))
        args.append(into)
        aliases = {len(args) - 1: 0}
    return pl.pallas_call(
        functools.partial(_outproj_kernel, alpha=alpha, n_blocks=nb),
        grid=(grid,),
        in_specs=in_specs,
        out_specs=pl.BlockSpec((tm * ROW_TILE, LANES), lambda i: (i + out_off, 0)),
        out_shape=_sds((out_rows * ROW_TILE, LANES)),
        input_output_aliases=aliases,
        compiler_params=_cparams("arbitrary"), name="outproj",
    )(*args)


def _router_kernel(h_ref, wt_ref, b_ref, idx_ref, gate_ref, cnt_ref):
    nt = (((1,), (1,)), ((), ()))
    hh, hl = _split_bf16(_tiles_to_rows(h_ref, (), 0, idx_ref.shape[1], ROW_TILE))
    wh, wl = _split_bf16(wt_ref[...])
    logits = (lax.dot_general(wh, hh, nt, preferred_element_type=F32)
              + lax.dot_general(wh, hl, nt, preferred_element_type=F32)
              + lax.dot_general(wl, hh, nt, preferred_element_type=F32)) + b_ref[...]
    eid = lax.broadcasted_iota(jnp.int32, logits.shape, 0)
    vals = []
    member = jnp.zeros(logits.shape, F32)
    for k in range(TOP_K):
        m = jnp.max(logits, axis=0, keepdims=True)
        sel = jnp.min(jnp.where(logits == m, eid, N_EXPERTS), axis=0, keepdims=True)
        idx_ref[k:k + 1, :] = sel
        vals.append(m)
        hit = eid == sel
        member = jnp.where(hit, 1.0, member)
        logits = jnp.where(hit, -jnp.inf, logits)
    es = [jnp.exp(v - vals[0]) for v in vals]
    tot = es[0] + es[1] + es[2] + es[3]
    for k in range(TOP_K):
        gate_ref[k:k + 1, :] = es[k] / tot
    cnt_ref[...] = jnp.broadcast_to(jnp.sum(member, axis=1, keepdims=True), cnt_ref.shape)


def _router_tile(n):
    return _pick(n, (896, 640, 512, 384, 256, 128))


def _router(h, n, wt, b):
    tm = _router_tile(n)
    nt = n // tm
    return pl.pallas_call(
        _router_kernel,
        grid=(nt,),
        in_specs=[pl.BlockSpec((tm * ROW_TILE, LANES), lambda i: (i, 0)),
                  pl.BlockSpec((N_EXPERTS, D_MODEL), lambda i: (0, 0)),
                  pl.BlockSpec((N_EXPERTS, 1), lambda i: (0, 0))],
        out_specs=[pl.BlockSpec((TOP_K, tm), lambda i: (0, i)), pl.BlockSpec((TOP_K, tm), lambda i: (0, i)),
                   pl.BlockSpec((N_EXPERTS, LANES), lambda i: (0, i))],
        out_shape=[_sds((TOP_K, n), jnp.int32), _sds((TOP_K, n)), _sds((N_EXPERTS, nt * LANES))],
        compiler_params=_cparams("parallel"), name="router",
    )(h, wt, b)


def _dest_kernel(idx_ref, base_ref, dest_ref):
    tm = idx_ref.shape[1]
    eid = lax.broadcasted_iota(jnp.int32, (N_EXPERTS, tm), 0)
    hits = [eid == idx_ref[k:k + 1, :] for k in range(TOP_K)]
    member = jnp.zeros((N_EXPERTS, tm), F32)
    for k in range(TOP_K):
        member = jnp.where(hits[k], 1.0, member)
    earlier = (lax.broadcasted_iota(jnp.int32, (tm, tm), 0) < lax.broadcasted_iota(jnp.int32, (tm, tm), 1))
    rank = _dot(member.astype(BF16), jnp.where(earlier, 1.0, 0.0).astype(BF16))
    pos = base_ref[...] + rank
    for k in range(TOP_K):
        dest_ref[k:k + 1, :] = jnp.sum(jnp.where(hits[k], pos, 0.0), axis=0, keepdims=True).astype(jnp.int32)


def _dest(top_idx, base):
    n = top_idx.shape[1]
    tm = _router_tile(n)
    return pl.pallas_call(
        _dest_kernel,
        grid=(n // tm,),
        in_specs=[pl.BlockSpec((TOP_K, tm), lambda i: (0, i)),
                  pl.BlockSpec((None, N_EXPERTS, 1), lambda i: (i, 0, 0))],
        out_specs=pl.BlockSpec((TOP_K, tm), lambda i: (0, i)),
        out_shape=_sds((TOP_K, n), jnp.int32),
        compiler_params=_cparams("parallel"), name="moe_dest",
    )(top_idx, base)


def _invert_kernel(vend_ref, pend_ref, dest_hbm, inv_ref, buf, sem, *, n, rc, n_rows):
    rows_per_slot = n // LANES
    chunks_per_slot = rows_per_slot // rc

    def chunk(c, carry):
        cp = pltpu.make_async_copy(dest_hbm.at[pl.ds(c * rc, rc), :], buf, sem.at[0])
        cp.start()
        cp.wait()
        k = c // chunks_per_slot
        t0 = (c - k * chunks_per_slot) * (rc * LANES)

        def row(r, carry):
            for l in range(LANES):
                inv_ref[buf[r, l]] = (t0 + r * LANES + l) * TOP_K + k
            return carry

        return lax.fori_loop(0, rc, row, carry)

    lax.fori_loop(0, TOP_K * chunks_per_slot, chunk, 0)

    def fill(lo, hi, ctr):
        def body(r, ctr):
            inv_ref[r] = ctr
            return ctr + 1
        return lax.fori_loop(lo, hi, body, ctr)

    ctr = jnp.int32(TOP_K * n)
    for e in range(N_EXPERTS):
        ctr = fill(vend_ref[e], pend_ref[e], ctr)
    fill(pend_ref[N_EXPERTS - 1], n_rows, ctr)


def _invert(valid_end, pad_end, dest, n_rows):
    k, n = dest.shape
    rc = _pick(n // LANES, (19, 16, 8, 5, 4, 2, 1))
    return pl.pallas_call(
        functools.partial(_invert_kernel, n=n, rc=rc, n_rows=n_rows),
        grid_spec=pltpu.PrefetchScalarGridSpec(
            num_scalar_prefetch=2, grid=(1,),
            in_specs=[pl.BlockSpec(memory_space=pl.ANY)],
            out_specs=pl.BlockSpec(memory_space=pltpu.SMEM),
            scratch_shapes=[pltpu.SMEM((rc, LANES), jnp.int32), pltpu.SemaphoreType.DMA((1,))]),
        out_shape=_sds((n_rows,), jnp.int32),
        compiler_params=_cparams("arbitrary"), name="moe_invert",
    )(valid_end, pad_end, dest.reshape(k * n // LANES, LANES))


def _moe_kernel(be_ref, nu_ref, inv_ref, h_hbm, w1f_ref, b1_ref, w2f_ref, b2_ref, y4_hbm,
                xbuf, xb, obuf, w1_ref, w2_ref, gsem, ssem, *, bm):
    i = pl.program_id(0)
    n_used = nu_ref[0]
    last_blk = pl.num_programs(0) - 1
    n_ff = D_FF // FF_CHUNK

    def tile(r):
        start = r * ROW_TILE
        return pl.ds(start if isinstance(r, int) else pl.multiple_of(start, ROW_TILE), ROW_TILE)

    def gather_row(blk, slot, j, r=None):
        r = inv_ref[blk * bm + j] if r is None else r
        t = lax.shift_right_logical(r, 2)
        pltpu.make_async_copy(h_hbm.at[tile(t), :], xbuf.at[slot, tile(j), :], gsem.at[slot]).start()

    def scatter_row(blk, slot, j, r=None):
        r = inv_ref[blk * bm + j] if r is None else r
        pltpu.make_async_copy(obuf.at[slot, tile(j), :], y4_hbm.at[tile(r), :], ssem.at[slot]).start()

    def wait_gather(slot):
        pltpu.make_async_copy(h_hbm.at[pl.ds(0, bm * ROW_TILE), :], xbuf.at[slot], gsem.at[slot]).wait()

    def wait_scatter(slot):
        pltpu.make_async_copy(obuf.at[slot], y4_hbm.at[pl.ds(0, bm * ROW_TILE), :], ssem.at[slot]).wait()

    def loop_rows(fn, blk, slot):
        def body(j, c):
            fn(blk, slot, j)
            return c
        lax.fori_loop(0, bm, body, 0)

    def issue_rows(fn, blk, slot, j_lo, j_hi):
        for j0 in range(j_lo, j_hi, DMA_GROUP):
            js = range(j0, j0 + DMA_GROUP)
            ids = [inv_ref[blk * bm + j] for j in js]
            for j, r in zip(js, ids):
                fn(blk, slot, j, r)

    def step():
        slot = i % 2
        other = 1 - slot
        per = bm // n_ff
        wait_gather(slot)
        xb[...] = _tiles_to_rows(xbuf, (slot,), 0, bm, ROW_TILE).astype(BF16)
        acc = jnp.zeros((bm, D_MODEL), F32)
        for c in range(n_ff):
            @pl.when(i >= 1)
            def _():
                issue_rows(scatter_row, i - 1, other, c * per, (c + 1) * per)

            @pl.when(i + 1 < n_used)
            def _():
                issue_rows(gather_row, i + 1, other, c * per, (c + 1) * per)
            lo = c * FF_CHUNK
            x = xb[...]
            hg = _dot(x, w1_ref[:, lo:lo + FF_CHUNK]) + b1_ref[:, lo:lo + FF_CHUNK]
            hl = _dot(x, w1_ref[:, D_FF + lo:D_FF + lo + FF_CHUNK]) + b1_ref[:, D_FF + lo:D_FF + lo + FF_CHUNK]
            gate = jnp.minimum(hg, SWIGLU_LIMIT)
            lin = jnp.clip(hl, -SWIGLU_LIMIT, SWIGLU_LIMIT)
            act = gate * _sigmoid(SWIGLU_ALPHA * gate) * (lin + 1.0)
            acc = acc + _dot(act.astype(BF16), w2_ref[lo:lo + FF_CHUNK, :])

        @pl.when(i >= 2)
        def _():
            wait_scatter(slot)
        _rows_to_tiles(obuf, (slot,), bm, acc + b2_ref[...])

    new_expert = jnp.logical_or(i == 0, be_ref[i] != be_ref[jnp.maximum(i - 1, 0)])

    @pl.when(jnp.logical_and(new_expert, i < n_used))
    def _():
        def cast_rows(r, c):
            rows = pl.ds(pl.multiple_of(r * W_CAST_ROWS, W_CAST_ROWS), W_CAST_ROWS)
            w1_ref[rows, :] = w1f_ref[rows, :].astype(BF16)
            w2_ref[rows, :] = w2f_ref[rows, :].astype(BF16)
            return c
        lax.fori_loop(0, D_MODEL // W_CAST_ROWS, cast_rows, 0)

    @pl.when(jnp.logical_and(i == 0, n_used > 0))
    def _():
        loop_rows(gather_row, 0, 0)

    @pl.when(i < n_used)
    def _():
        step()

    @pl.when(i == n_used - 1)
    def _():
        slot = i % 2
        loop_rows(scatter_row, i, slot)
        wait_scatter(slot)

        @pl.when(i >= 1)
        def _():
            wait_scatter(1 - slot)
        obuf[0] = jnp.zeros((bm * ROW_TILE, LANES), F32)

        def zero_block(blk, c):
            rows = bm * ROW_TILE
            cp = pltpu.make_async_copy(obuf.at[0], y4_hbm.at[pl.ds(pl.multiple_of(blk * rows, rows), rows), :],
                                       ssem.at[0])
            cp.start()
            cp.wait()
            return c
        lax.fori_loop(n_used, last_blk + 1, zero_block, 0)


def _moe(block_e, n_used, inv, h, layer, w1, b1, w2, b2, bm):
    nblk = block_e.shape[0]
    wmap = lambda i, be, nu, iv: (layer, be[i], 0, 0)
    return pl.pallas_call(
        functools.partial(_moe_kernel, bm=bm),
        grid_spec=pltpu.PrefetchScalarGridSpec(
            num_scalar_prefetch=3, grid=(nblk,),
            in_specs=[pl.BlockSpec(memory_space=pl.ANY),
                      pl.BlockSpec((None, None, D_MODEL, 2 * D_FF), wmap),
                      pl.BlockSpec((None, None, 1, 2 * D_FF), wmap),
                      pl.BlockSpec((None, None, D_FF, D_MODEL), wmap),
                      pl.BlockSpec((None, None, 1, D_MODEL), wmap)],
            out_specs=pl.BlockSpec(memory_space=pl.ANY),
            scratch_shapes=[pltpu.VMEM((2, bm * ROW_TILE, LANES), F32), pltpu.VMEM((bm, D_MODEL), BF16),
                            pltpu.VMEM((2, bm * ROW_TILE, LANES), F32),
                            pltpu.VMEM((D_MODEL, 2 * D_FF), BF16), pltpu.VMEM((D_FF, D_MODEL), BF16),
                            pltpu.SemaphoreType.DMA((2,)), pltpu.SemaphoreType.DMA((2,))]),
        out_shape=_sds((nblk * bm * ROW_TILE, LANES)),
        compiler_params=_cparams("arbitrary"), name="moe_experts",
    )(block_e, n_used, inv, h, w1, b1, w2, b2)


def _combine_kernel(y4_ref, gates_ref, h_ref, g_ref, b_ref, out_ref, *, alpha):
    tm = out_ref.shape[0]
    gates = gates_ref[...]
    acc = alpha * _tiles_to_rows(h_ref, (), 0, tm, ROW_TILE)
    for k in range(TOP_K):
        acc = acc + _tiles_to_rows(y4_ref, (), k * ROW_TILE, tm, TOP_K * ROW_TILE) * gates[:, k:k + 1]
    out_ref[...] = _ln(acc, g_ref[...], b_ref[...])


def _combine(y4, gates_t, h, n, g, b, alpha):
    tm = _pick(n, (224, 128, 64, 32, 16, 8))
    row = lambda i: (i, 0)
    const = lambda i: (0, 0)
    return pl.pallas_call(
        functools.partial(_combine_kernel, alpha=alpha),
        grid=(n // tm,),
        in_specs=[pl.BlockSpec((TOP_K * tm * ROW_TILE, LANES), row), pl.BlockSpec((tm, TOP_K), row),
                  pl.BlockSpec((tm * ROW_TILE, LANES), row),
                  pl.BlockSpec((1, D_MODEL), const), pl.BlockSpec((1, D_MODEL), const)],
        out_specs=pl.BlockSpec((tm, D_MODEL), row),
        out_shape=_sds((n, D_MODEL)),
        compiler_params=_cparams("parallel"), name="moe_combine",
    )(y4, gates_t, h, g, b)


def _group_layout(cnt_tiles, nblk, bm):
    cnt = cnt_tiles.astype(jnp.int32)
    counts = jnp.sum(cnt, axis=1)
    padded = (counts + bm - 1) // bm * bm
    pad_end = jnp.cumsum(padded)
    pad_start = pad_end - padded
    base = pad_start[:, None] + jnp.cumsum(cnt, axis=1) - cnt
    starts = jnp.arange(nblk, dtype=jnp.int32) * bm
    block_e = jnp.minimum(jnp.sum((pad_end[None, :] <= starts[:, None]).astype(jnp.int32), axis=1),
                          N_EXPERTS - 1).astype(jnp.int32)
    n_used = (pad_end[-1] // bm).astype(jnp.int32).reshape(1)
    valid_end = (pad_start + counts).astype(jnp.int32)
    return base.T.astype(F32)[:, :, None], valid_end, pad_end.astype(jnp.int32), block_e, n_used


def _pack_w_in(w):
    o = 0
    xa, o = w[:, o:o + D_A], o + D_A
    ga, o = w[:, o:o + D_A], o + D_A
    vb, o = w[:, o:o + D_B], o + D_B
    gb, o = w[:, o:o + D_B], o + D_B
    q, o = w[:, o:o + D_QK_C], o + D_QK_C
    k, o = w[:, o:o + D_QK_C], o + D_QK_C
    v, o = w[:, o:o + D_C], o + D_C
    r, o = w[:, o:o + D_C], o + D_C
    z = w[:, o:o + GATE_RANK]
    zq = jnp.zeros((w.shape[0], PC_K - PC_Z - GATE_RANK), w.dtype)
    zk = jnp.zeros((w.shape[0], PC_V - PC_K - D_QK_C), w.dtype)
    return jnp.concatenate([xa, ga, vb, gb, q, z, zq, k, zk, v, r], axis=1).astype(BF16)


def _block_diag(w):
    eye = jnp.eye(N_BLK_A, dtype=w.dtype)
    return jnp.einsum("hij,hg->higj", w, eye).reshape(D_A, D_A)


def kernel(x_prompt, x_sample, state_conv_a, state_rglru, state_conv_b, state_gla, meta_tokens, ln0_g, ln0_b,
           w_in, conv_a_w, conv_a_b, w_rg, b_rg, w_ig, b_ig, lru_lambda, conv_b_w, conv_b_b, ln_b_g, ln_b_b,
           w_gate2, b_gate, gla_norm_g, w_out, ln1_g, ln1_b, router_w, router_b, moe_w1, moe_b1, moe_w2, moe_b2,
           ln2_g, ln2_b):
    bp, seq, _ = x_prompt.shape
    bs, dseq, _ = x_sample.shape
    depth = w_in.shape[0]
    tp = N_META + seq
    ts = SAMPLE_PAD_T
    np_rows = bp * tp
    ns_rows = bs * dseq
    n = np_rows + ns_rows
    alpha = (2 * depth) ** 0.25
    row = lambda a: a.reshape(1, -1)

    meta = jnp.broadcast_to(meta_tokens[None], (bp, N_META, D_MODEL))
    xp_in = jnp.concatenate([meta, x_prompt], axis=1).reshape(np_rows, D_MODEL)
    xs_in = jnp.pad(x_sample, ((0, 0), (0, ts - dseq), (0, 0))).reshape(bs * ts, D_MODEL)
    zeros_p = (jnp.zeros((bp, CONV_A - 1, D_A), F32), jnp.zeros((bp, 1, D_A), F32),
               jnp.zeros((bp, CONV_B - 1, D_B), F32), jnp.zeros((bp, N_HEADS_C, DK_C, DV_C), F32))
    sb = _pick(bs, (8, 4, 2, 1))
    tm_p = _pick(np_rows, (384, 512, 256, 128, 64, 32, 16, 8))
    tm_s = _pick(math.gcd(ns_rows, np_rows), (128, 64, 32, 16, 8))
    nblk = -(-(TOP_K * n) // MOE_BM) + N_EXPERTS
    while (nblk * MOE_BM // TOP_K) % math.lcm(tm_p, tm_s):
        nblk += 1

    new_p = ([], [], [], [])
    new_s = ([], [], [], [])
    x_all = None
    for l in range(depth):
        w_packed = _pack_w_in(w_in[l])
        wg = jnp.concatenate([_block_diag(w_rg[l]), _block_diag(w_ig[l])], axis=1).astype(BF16)
        bg = jnp.concatenate([b_rg[l], b_ig[l]]).reshape(1, -1)
        w_out_b = w_out[l].astype(BF16)
        ng = row(jnp.tile(gla_norm_g[l], N_HEADS_C))
        first = l == 0
        if first:
            pa_p, pb_p, pc_p, xn_p = _inproj(xp_in, 0, np_rows, row(ln0_g), row(ln0_b), w_packed, True)
            pa_s, pb_s, pc_s, xn_s = _inproj(xs_in, 0, bs * ts, row(ln0_g), row(ln0_b), w_packed, True)
            res_p, res_p_row0 = xn_p, 0
        else:
            pa_p, pb_p, pc_p = _inproj(x_all, 0, np_rows, row(ln0_g), row(ln0_b), w_packed, False)
            xn_s = jnp.pad(x_all[np_rows:].reshape(bs, dseq, D_MODEL),
                           ((0, 0), (0, ts - dseq), (0, 0))).reshape(bs * ts, D_MODEL)
            pa_s, pb_s, pc_s = _inproj(xn_s, 0, bs * ts, row(ln0_g), row(ln0_b), w_packed, False)
            res_p, res_p_row0 = x_all, 0

        outs = []
        for (pa, pb, pc, nb, t, tv, bb, st) in (
                (pa_p, pb_p, pc_p, bp, tp, tp, 1, zeros_p),
                (pa_s, pb_s, pc_s, bs, ts, dseq, sb,
                 (state_conv_a[l], state_rglru[l].reshape(bs, 1, D_A), state_conv_b[l], state_gla[l]))):
            ya, ca_new, h_last = _rglru(pa.reshape(nb, t, PA_W), st[0], st[1], conv_a_w[l], row(conv_a_b[l]),
                                        wg, bg, row(lru_lambda[l]), tv, bb)
            yb, cb_new = _convb(pb.reshape(nb, t, PB_W), st[2], conv_b_w[l], row(conv_b_b[l]),
                                row(ln_b_g[l]), row(ln_b_b[l]), tv, bb)
            yc, s_new = _gla(pc.reshape(nb, t, PC_W), st[3], w_gate2[l], row(b_gate[l]), ng, tv, bb)
            outs.append((ya.reshape(nb * t, D_A), yb.reshape(nb * t, D_B), yc.reshape(nb * t, D_C),
                         ca_new, h_last.reshape(nb, D_A), cb_new, s_new))
        (ya_p, yb_p, yc_p, *st_p), (ya_s, yb_s, yc_s, *st_s) = outs
        for j in range(4):
            new_p[j].append(st_p[j])
            new_s[j].append(st_s[j])

        valid = lambda a: a.reshape(bs, ts, -1)[:, :dseq].reshape(ns_rows, -1)
        h_all = _outproj(ya_p, yb_p, yc_p, res_p, res_p_row0, w_out_b, row(ln1_g[l]), row(ln1_b[l]), alpha,
                         tm_p, nblk * MOE_BM // TOP_K)
        h_all = _outproj(valid(ya_s), valid(yb_s), valid(yc_s), valid(xn_s), 0, w_out_b, row(ln1_g[l]),
                         row(ln1_b[l]), alpha, tm_s, nblk * MOE_BM // TOP_K, out_row0=np_rows, into=h_all)

        top_idx, gates, cnt = _router(h_all, n, router_w[l].T, router_b[l].reshape(-1, 1))
        base, valid_end, pad_end, block_e, n_used = _group_layout(cnt[:, ::LANES], nblk, MOE_BM)
        dest = _dest(top_idx, base)
        inv = _invert(valid_end, pad_end, dest, nblk * MOE_BM)
        y4 = _moe(block_e, n_used, inv, h_all, l, moe_w1, moe_b1[:, :, None, :], moe_w2, moe_b2[:, :, None, :],
                  MOE_BM)
        x_all = _combine(y4, gates.T, h_all, n, row(ln2_g[l]), row(ln2_b[l]), alpha)

    y_p = x_all[:np_rows].reshape(bp, tp, D_MODEL)[:, N_META:]
    y_s = x_all[np_rows:].reshape(bs, dseq, D_MODEL)
    return (y_p, y_s,
            jnp.stack(new_p[0]), jnp.stack(new_p[1]), jnp.stack(new_p[2]), jnp.stack(new_p[3]),
            jnp.stack(new_s[0]), jnp.stack(new_s[1]), jnp.stack(new_s[2]), jnp.stack(new_s[3]))
```

```python
import functools
import math

import jax
import jax.numpy as jnp
from jax import lax
from jax.experimental import pallas as pl
from jax.experimental.pallas import tpu as pltpu

F32 = jnp.float32
BF16 = jnp.bfloat16

D_MODEL = 1024
N_META = 16
D_A = 384
D_B = 256
D_C = 384
N_BLK_A = 8
BLK_A = D_A // N_BLK_A
CONV_A = 4
RG_C = 8.0
CONV_B = 31
N_HEADS_C = 4
DV_C = D_C // N_HEADS_C
DK_C = DV_C // 2
D_QK_C = N_HEADS_C * DK_C
GATE_RANK = 16
GATE_TAU = 16.0
N_EXPERTS = 32
TOP_K = 4
D_FF = D_MODEL
SWIGLU_LIMIT = 7.0
SWIGLU_ALPHA = 1.702
LN_EPS = 1e-5

PA_W = 2 * D_A
PB_W = 2 * D_B
PC_Q, PC_Z, PC_K, PC_V, PC_R, PC_W = 0, 192, 256, 512, 896, 1280
P_W = PA_W + PB_W + PC_W

SUBLANES = 8
LANES = 128
ROW_TILE = D_MODEL // LANES
VMEM_LIMIT_BYTES = 56 * 1024 * 1024
MOE_BM = 256
FF_CHUNK = 256
DMA_GROUP = 8
W_CAST_ROWS = 128
SAMPLE_PAD_T = 8


def _cparams(*sem):
    return pltpu.CompilerParams(dimension_semantics=sem, vmem_limit_bytes=VMEM_LIMIT_BYTES)


def _sds(shape, dtype=F32):
    return jax.ShapeDtypeStruct(shape, dtype)


def _pick(n, prefs):
    for p in prefs:
        if n % p == 0:
            return p
    raise ValueError(f"no tile for {n} in {prefs}")


def _ln(x, g, b):
    mu = jnp.mean(x, axis=-1, keepdims=True)
    xc = x - mu
    var = jnp.mean(xc * xc, axis=-1, keepdims=True)
    return xc * lax.rsqrt(var + LN_EPS) * g + b


def _sigmoid(x):
    return 1.0 / (1.0 + jnp.exp(-x))


def _split_bf16(x):
    hi = x.astype(BF16)
    lo = (x - hi.astype(F32)).astype(BF16)
    return hi, lo


def _dot(a, b):
    return jnp.dot(a, b, preferred_element_type=F32)


def _inproj_kernel(x_ref, g_ref, b_ref, w_ref, pa_ref, pb_ref, pc_ref, *maybe_xn, apply_ln):
    x = x_ref[...]
    if apply_ln:
        x = _ln(x, g_ref[...], b_ref[...])
        maybe_xn[0][...] = x
    xb = x.astype(BF16)
    pa_ref[...] = _dot(xb, w_ref[:, 0:PA_W])
    pb_ref[...] = _dot(xb, w_ref[:, PA_W:PA_W + PB_W])
    pc_ref[...] = _dot(xb, w_ref[:, PA_W + PB_W:P_W])


def _inproj(x, row0, nrows, ln_g, ln_b, w_packed, apply_ln):
    tm = _pick(nrows, (384, 512, 256, 128, 64, 32, 16, 8))
    while row0 % tm:
        tm //= 2
    off = row0 // tm
    const = lambda i: (0, 0)
    row = lambda i: (i, 0)
    out_shape = [_sds((nrows, PA_W)), _sds((nrows, PB_W)), _sds((nrows, PC_W))]
    out_specs = [pl.BlockSpec((tm, PA_W), row), pl.BlockSpec((tm, PB_W), row), pl.BlockSpec((tm, PC_W), row)]
    if apply_ln:
        out_shape.append(_sds((nrows, D_MODEL)))
        out_specs.append(pl.BlockSpec((tm, D_MODEL), row))
    return pl.pallas_call(
        functools.partial(_inproj_kernel, apply_ln=apply_ln),
        grid=(nrows // tm,),
        in_specs=[pl.BlockSpec((tm, D_MODEL), lambda i: (i + off, 0)),
                  pl.BlockSpec((1, D_MODEL), const), pl.BlockSpec((1, D_MODEL), const),
                  pl.BlockSpec((D_MODEL, P_W), const)],
        out_specs=out_specs, out_shape=out_shape,
        compiler_params=_cparams("parallel"), name="inproj",
    )(x, ln_g, ln_b, w_packed)


def _rglru_kernel(p_ref, cbuf_ref, h0_ref, cw_ref, cb_ref, wg_ref, bg_ref, lam_ref,
                  y_ref, cnew_ref, hlast_ref, xp_scr, a_scr, h_scr, *, T, Tc, Tv, Bb):
    lam = lam_ref[...]
    softplus_neg = jnp.maximum(-lam, 0.0) + jnp.log1p(jnp.exp(-jnp.abs(lam)))
    c_decay = -RG_C * softplus_neg
    cw = cw_ref[...]
    cb = cb_ref[...]
    bg = bg_ref[...]
    sub = lax.broadcasted_iota(jnp.int32, (Tc, D_A), 0) % SUBLANES
    halo = SUBLANES - (CONV_A - 1)
    for b in range(Bb):
        xp_scr[halo:SUBLANES, :] = cbuf_ref[b]
        xp_scr[SUBLANES:SUBLANES + T, :] = p_ref[b, :, 0:D_A]
        cnew_ref[b] = xp_scr[halo + Tv:SUBLANES + Tv, :]

        def chunk(ci, h_b):
            r0 = pl.multiple_of(ci * Tc, SUBLANES)
            win = xp_scr[pl.ds(r0, Tc + SUBLANES), :]
            xc = cb + cw[0:1] * win[halo:halo + Tc]
            for j in range(1, CONV_A):
                xc = xc + cw[j:j + 1] * win[halo + j:halo + j + Tc]
            gates = _dot(xc.astype(BF16), wg_ref[...]) + bg
            r = _sigmoid(gates[:, 0:D_A])
            i = _sigmoid(gates[:, D_A:2 * D_A])
            log_a = c_decay * r
            a = jnp.exp(log_a)
            u = jnp.sqrt(1.0 - a * a) * (i * xc)
            for s in (1, 2, 4):
                keep = sub >= s
                a_prev = pltpu.roll(a, s, 0)
                u_prev = pltpu.roll(u, s, 0)
                u = jnp.where(keep, a * u_prev + u, u)
                a = jnp.where(keep, a * a_prev, a)
            a_scr[...] = a
            h_scr[pl.ds(r0, Tc), :] = u

            def group(gi, h_b):
                c0 = pl.multiple_of(gi * SUBLANES, SUBLANES)
                g0 = pl.multiple_of(r0 + gi * SUBLANES, SUBLANES)
                h8 = a_scr[pl.ds(c0, SUBLANES), :] * h_b + h_scr[pl.ds(g0, SUBLANES), :]
                h_scr[pl.ds(g0, SUBLANES), :] = h8
                return jnp.broadcast_to(h8[SUBLANES - 1:SUBLANES, :], (SUBLANES, D_A))

            h_b = lax.fori_loop(0, Tc // SUBLANES, group, h_b)
            ga = p_ref[b, pl.ds(r0, Tc), D_A:2 * D_A]
            gelu = 0.5 * ga * (1.0 + jnp.tanh(0.7978845608028654 * (ga + 0.044715 * ga * ga * ga)))
            y_ref[b, pl.ds(r0, Tc), :] = h_scr[pl.ds(r0, Tc), :] * gelu
            return h_b

        h_b = jnp.broadcast_to(h0_ref[b], (SUBLANES, D_A))
        lax.fori_loop(0, T // Tc, chunk, h_b)
        hlast_ref[b] = h_scr[Tv - 1:Tv, :]


def _rglru(pa3, cbuf, h0, cw, cb, wg, bg, lam, Tv, Bb):
    B, T, _ = pa3.shape
    Tc = _pick(T, (344, 256, 128, 64, 48, 32, 16, 8))
    const2 = lambda i: (0, 0)
    seq3 = lambda i: (i, 0, 0)
    return pl.pallas_call(
        functools.partial(_rglru_kernel, T=T, Tc=Tc, Tv=Tv, Bb=Bb),
        grid=(B // Bb,),
        in_specs=[pl.BlockSpec((Bb, T, PA_W), seq3), pl.BlockSpec((Bb, CONV_A - 1, D_A), seq3),
                  pl.BlockSpec((Bb, 1, D_A), seq3), pl.BlockSpec((CONV_A, D_A), const2),
                  pl.BlockSpec((1, D_A), const2), pl.BlockSpec((D_A, 2 * D_A), const2),
                  pl.BlockSpec((1, 2 * D_A), const2), pl.BlockSpec((1, D_A), const2)],
        out_specs=[pl.BlockSpec((Bb, T, D_A), seq3), pl.BlockSpec((Bb, CONV_A - 1, D_A), seq3),
                   pl.BlockSpec((Bb, 1, D_A), seq3)],
        out_shape=[_sds((B, T, D_A)), _sds((B, CONV_A - 1, D_A)), _sds((B, 1, D_A))],
        scratch_shapes=[pltpu.VMEM((T + 2 * SUBLANES, D_A), F32), pltpu.VMEM((Tc, D_A), F32),
                        pltpu.VMEM((T, D_A), F32)],
        compiler_params=_cparams("parallel"), name="rglru",
    )(pa3, cbuf, h0, cw, cb, wg, bg, lam)


B_HALO = 32


def _convb_kernel(p_ref, buf_ref, w_ref, cb_ref, g_ref, b_ref, y_ref, bnew_ref, u_scr, *, T, Tc, Tv, Bb):
    w = w_ref[...]
    cb = cb_ref[...]
    g = g_ref[...]
    bb = b_ref[...]
    first = B_HALO - (CONV_B - 1)
    for b in range(Bb):
        u_scr[0:first, :] = jnp.zeros((first, D_B), F32)
        u_scr[first:B_HALO, :] = buf_ref[b]
        u_scr[B_HALO:B_HALO + T, :] = p_ref[b, :, 0:D_B] * _sigmoid(p_ref[b, :, D_B:2 * D_B])
        bnew_ref[b] = u_scr[first + Tv:B_HALO + Tv, :]

        def chunk(ci, carry):
            r0 = pl.multiple_of(ci * Tc, SUBLANES)
            win = u_scr[pl.ds(r0, Tc + B_HALO), :]
            shifted = [win] + [pltpu.roll(win, Tc + B_HALO - s, 0) for s in range(1, SUBLANES)]
            acc = cb
            for j in range(CONV_B):
                a, s = divmod(first + j, SUBLANES)
                acc = acc + w[j:j + 1] * shifted[s][a * SUBLANES:a * SUBLANES + Tc]
            yn = _ln(acc, g, bb)
            y_ref[b, pl.ds(r0, Tc), :] = yn * _sigmoid(yn)
            return carry

        lax.fori_loop(0, T // Tc, chunk, 0)


def _convb(pb3, buf, w, cb, g, b, Tv, Bb):
    B, T, _ = pb3.shape
    Tc = _pick(T, (48, 32, 16, 8))
    const2 = lambda i: (0, 0)
    seq3 = lambda i: (i, 0, 0)
    return pl.pallas_call(
        functools.partial(_convb_kernel, T=T, Tc=Tc, Tv=Tv, Bb=Bb),
        grid=(B // Bb,),
        in_specs=[pl.BlockSpec((Bb, T, PB_W), seq3), pl.BlockSpec((Bb, CONV_B - 1, D_B), seq3),
                  pl.BlockSpec((CONV_B, D_B), const2), pl.BlockSpec((1, D_B), const2),
                  pl.BlockSpec((1, D_B), const2), pl.BlockSpec((1, D_B), const2)],
        out_specs=[pl.BlockSpec((Bb, T, D_B), seq3), pl.BlockSpec((Bb, CONV_B - 1, D_B), seq3)],
        out_shape=[_sds((B, T, D_B)), _sds((B, CONV_B - 1, D_B))],
        scratch_shapes=[pltpu.VMEM((T + B_HALO, D_B), F32)],
        compiler_params=_cparams("parallel"), name="convb",
    )(pb3, buf, w, cb, g, b)


def _gla_kernel(p_ref, s0_ref, wg2_ref, bgate_ref, ng_ref, y_ref, snew_ref, s_scr, g_scr, *, T, C, Tb, Tv, Bb):
    ri = lax.broadcasted_iota(jnp.int32, (C, C), 0)
    ci_ = lax.broadcasted_iota(jnp.int32, (C, C), 1)
    tril = ri >= ci_
    lane_k = lax.broadcasted_iota(jnp.int32, (1, D_QK_C), 1)
    lane_v = lax.broadcasted_iota(jnp.int32, (1, D_C), 1)
    hm_k = [(lane_k >= h * DK_C) & (lane_k < (h + 1) * DK_C) for h in range(N_HEADS_C)]
    hm_v = [(lane_v >= h * DV_C) & (lane_v < (h + 1) * DV_C) for h in range(N_HEADS_C)]
    rs = lax.broadcasted_iota(jnp.int32, (D_C, D_QK_C), 0)
    cs = lax.broadcasted_iota(jnp.int32, (D_C, D_QK_C), 1)
    bd_t = (rs >= 0) & (rs < 0)
    for h in range(N_HEADS_C):
        bd_t = bd_t | ((rs >= h * DV_C) & (rs < (h + 1) * DV_C) & (cs >= h * DK_C) & (cs < (h + 1) * DK_C))
    rm = lax.broadcasted_iota(jnp.int32, (D_C, D_C), 0)
    cm = lax.broadcasted_iota(jnp.int32, (D_C, D_C), 1)
    seg = (rm >= 0) & (rm < 0)
    for h in range(N_HEADS_C):
        seg = seg | ((rm >= h * DV_C) & (rm < (h + 1) * DV_C) & (cm >= h * DV_C) & (cm < (h + 1) * DV_C))
    mseg = jnp.where(seg, 1.0, 0.0).astype(BF16)
    wg2 = wg2_ref[...].astype(BF16)
    bgate = bgate_ref[...]
    ng = ng_ref[...]
    rowi = lax.broadcasted_iota(jnp.int32, (C, 1), 0)
    tdims = (((0,), (0,)), ((), ()))

    n_chunks = T // C
    nt_dims = (((1,), (1,)), ((), ()))
    tril4 = jnp.concatenate([tril] * N_HEADS_C, axis=0)
    scan_shifts = [s for s in (1, 2, 4, 8, 16, 32) if s < C]
    scan_keep = [rowi >= s for s in scan_shifts]

    def tile_rows(ti, tb, base=0):
        r0 = ti * tb
        return pl.ds(base + (r0 if isinstance(ti, int) else pl.multiple_of(r0, tb)), tb)

    def gates(ti, carry):
        rows = tile_rows(ti, Tb)
        z = p_ref[rows, PC_Z:PC_Z + GATE_RANK]
        pre = _dot(z.astype(BF16), wg2) + bgate
        g = (jnp.minimum(pre, 0.0) - jnp.log1p(jnp.exp(-jnp.abs(pre)))) * (1.0 / GATE_TAU)
        rid = ti * Tb + lax.broadcasted_iota(jnp.int32, (Tb, 1), 0)
        if Bb > 1:
            rid = rid & (T - 1)
        g_scr[rows, :] = jnp.where(rid < Tv, g, 0.0)
        return carry
    lax.fori_loop(0, Bb * T // Tb, gates, 0)

    for b in range(Bb):
        def chunk(ci, carry):
            rows = tile_rows(ci, C, b * T)
            q = p_ref[rows, PC_Q:PC_Q + D_QK_C] * (DK_C ** -0.5)
            k = p_ref[rows, PC_K:PC_K + D_QK_C]
            v = p_ref[rows, PC_V:PC_V + D_C]
            k = jnp.where((ci * C + rowi) < Tv, k, 0.0)
            gcum = g_scr[rows, :]
            for s, keep in zip(scan_shifts, scan_keep):
                gcum = gcum + jnp.where(keep, pltpu.roll(gcum, s, 0), 0.0)
            g_last = gcum[C - 1:C, :]
            g_mid = gcum[C // 2 - 1:C // 2, :]
            vb = v.astype(BF16)
            qt = q * jnp.exp(gcum - g_mid)
            ktb = (k * jnp.exp(g_mid - gcum)).astype(BF16)
            q4 = jnp.concatenate([jnp.where(hm_k[h], qt, 0.0) for h in range(N_HEADS_C)], axis=0).astype(BF16)
            sc = lax.dot_general(q4, ktb, nt_dims, preferred_element_type=F32)
            r4 = _dot(jnp.where(tril4, sc, 0.0).astype(BF16), vb)
            o = jnp.where(hm_v[0], r4[0:C], 0.0)
            for h in range(1, N_HEADS_C):
                o = o + jnp.where(hm_v[h], r4[h * C:(h + 1) * C], 0.0)
            kd = (k * jnp.exp(g_last - gcum)).astype(BF16)
            upd_t = lax.dot_general(vb, kd, tdims, preferred_element_type=F32)
            s_in = s_scr[...]
            o = o + lax.dot_general((q * jnp.exp(gcum)).astype(BF16), s_in.astype(BF16), nt_dims,
                                    preferred_element_type=F32)
            s_scr[...] = s_in * jnp.exp(g_last) + jnp.where(bd_t, upd_t, 0.0)
            y_ref[rows, :] = o
            return carry

        s_scr[...] = jnp.zeros((D_C, D_QK_C), F32)
        for h in range(N_HEADS_C):
            s_scr[h * DV_C:(h + 1) * DV_C, h * DK_C:(h + 1) * DK_C] = s0_ref[b, h]
        lax.fori_loop(0, n_chunks, chunk, 0)
        for h in range(N_HEADS_C):
            snew_ref[b, h] = s_scr[h * DV_C:(h + 1) * DV_C, h * DK_C:(h + 1) * DK_C]

    def finish(ti, carry):
        rows = tile_rows(ti, Tb)
        o = y_ref[rows, :]
        rg = p_ref[rows, PC_R:PC_R + D_C]
        o2_hi, o2_lo = _split_bf16(o * o)
        ms = (_dot(o2_hi, mseg) + _dot(o2_lo, mseg)) * (1.0 / DV_C)
        y_ref[rows, :] = o * lax.rsqrt(ms + LN_EPS) * ng * (rg * _sigmoid(rg))
        return carry
    lax.fori_loop(0, Bb * T // Tb, finish, 0)


def _gla(pc, B, T, s0, wg2, bgate, ng, Tv, Bb):
    assert Bb == 1 or T & (T - 1) == 0
    C = _pick(T, (48, 32, 16, 8))
    Tb = _pick(Bb * T, (344, 256, 128, 64, 48, 32, 16, 8))
    const2 = lambda i: (0, 0)
    row2 = lambda i: (i, 0)
    seq4 = lambda i: (i, 0, 0, 0)
    st = (Bb, N_HEADS_C, DV_C, DK_C)
    y, s_new_t = pl.pallas_call(
        functools.partial(_gla_kernel, T=T, C=C, Tb=Tb, Tv=Tv, Bb=Bb),
        grid=(B // Bb,),
        in_specs=[pl.BlockSpec((Bb * T, PC_W), row2), pl.BlockSpec(st, seq4),
                  pl.BlockSpec((GATE_RANK, D_QK_C), const2), pl.BlockSpec((1, D_QK_C), const2),
                  pl.BlockSpec((1, D_C), const2)],
        out_specs=[pl.BlockSpec((Bb * T, D_C), row2), pl.BlockSpec(st, seq4)],
        out_shape=[_sds((B * T, D_C)), _sds((B, N_HEADS_C, DV_C, DK_C))],
        scratch_shapes=[pltpu.VMEM((D_C, D_QK_C), F32), pltpu.VMEM((Bb * T, D_QK_C), F32)],
        compiler_params=_cparams("parallel"), name="gla",
    )(pc, jnp.swapaxes(s0, 2, 3), wg2, bgate, ng)
    return y, jnp.swapaxes(s_new_t, 2, 3)


def _tiles_to_rows(ref, lead, start, rows, stride):
    return jnp.concatenate([ref[lead + (pl.ds(start + s, rows, stride=stride), slice(None))]
                            for s in range(ROW_TILE)], axis=1)


def _rows_to_tiles(ref, lead, rows, val):
    for s in range(ROW_TILE):
        ref[lead + (pl.ds(s, rows, stride=ROW_TILE), slice(None))] = val[:, s * LANES:(s + 1) * LANES]


def _outproj_kernel(ya_ref, yb_ref, yc_ref, x_ref, w_ref, g_ref, b_ref, *rest, alpha, n_blocks):
    h_ref = rest[-1]

    @pl.when(pl.program_id(0) < n_blocks)
    def _():
        y = (_dot(ya_ref[...].astype(BF16), w_ref[0:D_A, :])
             + _dot(yb_ref[...].astype(BF16), w_ref[D_A:D_A + D_B, :])
             + _dot(yc_ref[...].astype(BF16), w_ref[D_A + D_B:D_MODEL, :]))
        h = _ln(alpha * x_ref[...] + y, g_ref[...], b_ref[...])
        _rows_to_tiles(h_ref, (), h.shape[0], h)

    @pl.when(pl.program_id(0) >= n_blocks)
    def _():
        h_ref[...] = jnp.zeros(h_ref.shape, F32)


def _outproj(ya, yb, yc, x, row0, w, g, b, alpha, tm, out_rows, out_row0=0, into=None):
    n = ya.shape[0]
    nb = n // tm
    assert n % tm == 0 and row0 % tm == 0 and out_row0 % tm == 0 and out_rows % tm == 0
    off, out_off = row0 // tm, out_row0 // tm
    grid = nb if into is not None else out_rows // tm
    row = lambda i: (jnp.minimum(i, nb - 1), 0)
    const = lambda i: (0, 0)
    in_specs = [pl.BlockSpec((tm, D_A), row), pl.BlockSpec((tm, D_B), row), pl.BlockSpec((tm, D_C), row),
                pl.BlockSpec((tm, D_MODEL), lambda i: (jnp.minimum(i, nb - 1) + off, 0)),
                pl.BlockSpec((D_MODEL, D_MODEL), const), pl.BlockSpec((1, D_MODEL), const),
                pl.BlockSpec((1, D_MODEL), const)]
    args = [ya, yb, yc, x, w, g, b]
    aliases = {}
    if into is not None:
        in_specs.append(pl.BlockSpec(memory_space=pl.ANY))
        args.append(into)
        aliases = {len(args) - 1: 0}
    return pl.pallas_call(
        functools.partial(_outproj_kernel, alpha=alpha, n_blocks=nb),
        grid=(grid,),
        in_specs=in_specs,
        out_specs=pl.BlockSpec((tm * ROW_TILE, LANES), lambda i: (i + out_off, 0)),
        out_shape=_sds((out_rows * ROW_TILE, LANES)),
        input_output_aliases=aliases,
        compiler_params=_cparams("arbitrary"), name="outproj",
    )(*args)


def _router_kernel(h_ref, wt_ref, b_ref, idx_ref, gate_ref, cnt_ref):
    nt = (((1,), (1,)), ((), ()))
    hh, hl = _split_bf16(_tiles_to_rows(h_ref, (), 0, idx_ref.shape[1], ROW_TILE))
    wh, wl = _split_bf16(wt_ref[...])
    logits = (lax.dot_general(wh, hh, nt, preferred_element_type=F32)
              + lax.dot_general(wh, hl, nt, preferred_element_type=F32)
              + lax.dot_general(wl, hh, nt, preferred_element_type=F32)) + b_ref[...]
    eid = lax.broadcasted_iota(jnp.int32, logits.shape, 0)
    vals = []
    member = jnp.zeros(logits.shape, F32)
    for k in range(TOP_K):
        m = jnp.max(logits, axis=0, keepdims=True)
        sel = jnp.min(jnp.where(logits == m, eid, N_EXPERTS), axis=0, keepdims=True)
        idx_ref[k:k + 1, :] = sel
        vals.append(m)
        hit = eid == sel
        member = jnp.where(hit, 1.0, member)
        logits = jnp.where(hit, -jnp.inf, logits)
    es = [jnp.exp(v - vals[0]) for v in vals]
    tot = es[0] + es[1] + es[2] + es[3]
    for k in range(TOP_K):
        gate_ref[k:k + 1, :] = es[k] / tot
    cnt_ref[...] = jnp.broadcast_to(jnp.sum(member, axis=1, keepdims=True), cnt_ref.shape)


def _router_tile(n):
    return _pick(n, (896, 640, 512, 384, 256, 128))


def _router(h, n, wt, b):
    tm = _router_tile(n)
    nt = n // tm
    return pl.pallas_call(
        _router_kernel,
        grid=(nt,),
        in_specs=[pl.BlockSpec((tm * ROW_TILE, LANES), lambda i: (i, 0)),
                  pl.BlockSpec((N_EXPERTS, D_MODEL), lambda i: (0, 0)),
                  pl.BlockSpec((N_EXPERTS, 1), lambda i: (0, 0))],
        out_specs=[pl.BlockSpec((TOP_K, tm), lambda i: (0, i)), pl.BlockSpec((TOP_K, tm), lambda i: (0, i)),
                   pl.BlockSpec((N_EXPERTS, LANES), lambda i: (0, i))],
        out_shape=[_sds((TOP_K, n), jnp.int32), _sds((TOP_K, n)), _sds((N_EXPERTS, nt * LANES))],
        compiler_params=_cparams("parallel"), name="router",
    )(h, wt, b)


def _dest_kernel(idx_ref, base_ref, dest_ref):
    tm = idx_ref.shape[1]
    eid = lax.broadcasted_iota(jnp.int32, (N_EXPERTS, tm), 0)
    hits = [eid == idx_ref[k:k + 1, :] for k in range(TOP_K)]
    member = jnp.zeros((N_EXPERTS, tm), F32)
    for k in range(TOP_K):
        member = jnp.where(hits[k], 1.0, member)
    earlier = (lax.broadcasted_iota(jnp.int32, (tm, tm), 0) < lax.broadcasted_iota(jnp.int32, (tm, tm), 1))
    rank = _dot(member.astype(BF16), jnp.where(earlier, 1.0, 0.0).astype(BF16))
    pos = base_ref[...] + rank
    for k in range(TOP_K):
        dest_ref[k:k + 1, :] = jnp.sum(jnp.where(hits[k], pos, 0.0), axis=0, keepdims=True).astype(jnp.int32)


def _dest(top_idx, base):
    n = top_idx.shape[1]
    tm = _router_tile(n)
    return pl.pallas_call(
        _dest_kernel,
        grid=(n // tm,),
        in_specs=[pl.BlockSpec((TOP_K, tm), lambda i: (0, i)),
                  pl.BlockSpec((None, N_EXPERTS, 1), lambda i: (i, 0, 0))],
        out_specs=pl.BlockSpec((TOP_K, tm), lambda i: (0, i)),
        out_shape=_sds((TOP_K, n), jnp.int32),
        compiler_params=_cparams("parallel"), name="moe_dest",
    )(top_idx, base)


def _invert_kernel(vend_ref, pend_ref, dest_hbm, inv_ref, buf, sem, *, n, rc, n_rows):
    rows_per_slot = n // LANES
    chunks_per_slot = rows_per_slot // rc

    def chunk(c, carry):
        cp = pltpu.make_async_copy(dest_hbm.at[pl.ds(c * rc, rc), :], buf, sem.at[0])
        cp.start()
        cp.wait()
        k = c // chunks_per_slot
        t0 = (c - k * chunks_per_slot) * (rc * LANES)

        def row(r, carry):
            first = (t0 + r * LANES) * TOP_K + k
            for l in range(LANES):
                inv_ref[buf[r, l]] = first + l * TOP_K
            return carry

        return lax.fori_loop(0, rc, row, carry)

    lax.fori_loop(0, TOP_K * chunks_per_slot, chunk, 0)

    def fill(lo, hi, ctr):
        def body(r, ctr):
            inv_ref[r] = ctr
            return ctr + 1
        return lax.fori_loop(lo, hi, body, ctr)

    ctr = jnp.int32(TOP_K * n)
    for e in range(N_EXPERTS):
        ctr = fill(vend_ref[e], pend_ref[e], ctr)
    fill(pend_ref[N_EXPERTS - 1], n_rows, ctr)


def _invert(valid_end, pad_end, dest, n_rows):
    k, n = dest.shape
    rc = _pick(n // LANES, (19, 16, 8, 5, 4, 2, 1))
    return pl.pallas_call(
        functools.partial(_invert_kernel, n=n, rc=rc, n_rows=n_rows),
        grid_spec=pltpu.PrefetchScalarGridSpec(
            num_scalar_prefetch=2, grid=(1,),
            in_specs=[pl.BlockSpec(memory_space=pl.ANY)],
            out_specs=pl.BlockSpec(memory_space=pltpu.SMEM),
            scratch_shapes=[pltpu.SMEM((rc, LANES), jnp.int32), pltpu.SemaphoreType.DMA((1,))]),
        out_shape=_sds((n_rows,), jnp.int32),
        compiler_params=_cparams("arbitrary"), name="moe_invert",
    )(valid_end, pad_end, dest.reshape(k * n // LANES, LANES))


def _moe_kernel(be_ref, nu_ref, inv_ref, h_hbm, w1f_ref, b1_ref, w2f_ref, b2_ref, y4_hbm,
                xbuf, xb, obuf, w1_ref, w2_ref, gsem, ssem, *, bm):
    i = pl.program_id(0)
    n_used = nu_ref[0]
    last_blk = pl.num_programs(0) - 1
    n_ff = D_FF // FF_CHUNK

    def tile(r):
        start = r * ROW_TILE
        return pl.ds(start if isinstance(r, int) else pl.multiple_of(start, ROW_TILE), ROW_TILE)

    def gather_row(blk, slot, j, r=None):
        r = inv_ref[blk * bm + j] if r is None else r
        t = lax.shift_right_logical(r, 2)
        pltpu.make_async_copy(h_hbm.at[tile(t), :], xbuf.at[slot, tile(j), :], gsem.at[slot]).start()

    def scatter_row(blk, slot, j, r=None):
        r = inv_ref[blk * bm + j] if r is None else r
        pltpu.make_async_copy(obuf.at[slot, tile(j), :], y4_hbm.at[tile(r), :], ssem.at[slot]).start()

    def wait_gather(slot):
        pltpu.make_async_copy(h_hbm.at[pl.ds(0, bm * ROW_TILE), :], xbuf.at[slot], gsem.at[slot]).wait()

    def wait_scatter(slot):
        pltpu.make_async_copy(obuf.at[slot], y4_hbm.at[pl.ds(0, bm * ROW_TILE), :], ssem.at[slot]).wait()

    def loop_rows(fn, blk, slot):
        def body(j, c):
            fn(blk, slot, j)
            return c
        lax.fori_loop(0, bm, body, 0)

    def issue_rows(fn, blk, slot, j_lo, j_hi):
        for j0 in range(j_lo, j_hi, DMA_GROUP):
            js = range(j0, j0 + DMA_GROUP)
            ids = [inv_ref[blk * bm + j] for j in js]
            for j, r in zip(js, ids):
                fn(blk, slot, j, r)

    def step():
        slot = i % 2

        @pl.when(i + 1 < n_used)
        def _():
            issue_rows(gather_row, i + 1, 1 - slot, 0, bm)
        wait_gather(slot)
        xb[...] = _tiles_to_rows(xbuf, (slot,), 0, bm, ROW_TILE).astype(BF16)
        acc = jnp.zeros((bm, D_MODEL), F32)
        for c in range(n_ff):
            lo = c * FF_CHUNK
            x = xb[...]
            hg = _dot(x, w1_ref[:, lo:lo + FF_CHUNK]) + b1_ref[:, lo:lo + FF_CHUNK]
            hl = _dot(x, w1_ref[:, D_FF + lo:D_FF + lo + FF_CHUNK]) + b1_ref[:, D_FF + lo:D_FF + lo + FF_CHUNK]
            gate = jnp.minimum(hg, SWIGLU_LIMIT)
            lin = jnp.clip(hl, -SWIGLU_LIMIT, SWIGLU_LIMIT)
            act = gate * _sigmoid(SWIGLU_ALPHA * gate) * (lin + 1.0)
            acc = acc + _dot(act.astype(BF16), w2_ref[lo:lo + FF_CHUNK, :])

        @pl.when(i >= 2)
        def _():
            wait_scatter(slot)
        _rows_to_tiles(obuf, (slot,), bm, acc + b2_ref[...])
        issue_rows(scatter_row, i, slot, 0, bm)

    new_expert = jnp.logical_or(i == 0, be_ref[i] != be_ref[jnp.maximum(i - 1, 0)])

    @pl.when(jnp.logical_and(new_expert, i < n_used))
    def _():
        def cast_rows(r, c):
            rows = pl.ds(pl.multiple_of(r * W_CAST_ROWS, W_CAST_ROWS), W_CAST_ROWS)
            w1_ref[rows, :] = w1f_ref[rows, :].astype(BF16)
            w2_ref[rows, :] = w2f_ref[rows, :].astype(BF16)
            return c
        lax.fori_loop(0, D_MODEL // W_CAST_ROWS, cast_rows, 0)

    @pl.when(jnp.logical_and(i == 0, n_used > 0))
    def _():
        loop_rows(gather_row, 0, 0)

    @pl.when(i < n_used)
    def _():
        step()

    @pl.when(i == n_used - 1)
    def _():
        slot = i % 2
        wait_scatter(slot)

        @pl.when(i >= 1)
        def _():
            wait_scatter(1 - slot)
        obuf[0] = jnp.zeros((bm * ROW_TILE, LANES), F32)

        def zero_block(blk, c):
            rows = bm * ROW_TILE
            cp = pltpu.make_async_copy(obuf.at[0], y4_hbm.at[pl.ds(pl.multiple_of(blk * rows, rows), rows), :],
                                       ssem.at[0])
            cp.start()
            cp.wait()
            return c
        lax.fori_loop(n_used, last_blk + 1, zero_block, 0)


def _moe(block_e, n_used, inv, h, layer, w1, b1, w2, b2, bm):
    nblk = block_e.shape[0]
    wmap = lambda i, be, nu, iv: (layer, be[i], 0, 0)
    return pl.pallas_call(
        functools.partial(_moe_kernel, bm=bm),
        grid_spec=pltpu.PrefetchScalarGridSpec(
            num_scalar_prefetch=3, grid=(nblk,),
            in_specs=[pl.BlockSpec(memory_space=pl.ANY),
                      pl.BlockSpec((None, None, D_MODEL, 2 * D_FF), wmap),
                      pl.BlockSpec((None, None, 1, 2 * D_FF), wmap),
                      pl.BlockSpec((None, None, D_FF, D_MODEL), wmap),
                      pl.BlockSpec((None, None, 1, D_MODEL), wmap)],
            out_specs=pl.BlockSpec(memory_space=pl.ANY),
            scratch_shapes=[pltpu.VMEM((2, bm * ROW_TILE, LANES), F32), pltpu.VMEM((bm, D_MODEL), BF16),
                            pltpu.VMEM((2, bm * ROW_TILE, LANES), F32),
                            pltpu.VMEM((D_MODEL, 2 * D_FF), BF16), pltpu.VMEM((D_FF, D_MODEL), BF16),
                            pltpu.SemaphoreType.DMA((2,)), pltpu.SemaphoreType.DMA((2,))]),
        out_shape=_sds((nblk * bm * ROW_TILE, LANES)),
        compiler_params=_cparams("arbitrary"), name="moe_experts",
    )(block_e, n_used, inv, h, w1, b1, w2, b2)


def _combine_kernel(y4_ref, gates_ref, h_ref, g_ref, b_ref, out_ref, *, alpha):
    tm = out_ref.shape[0]
    gates = gates_ref[...]
    acc = alpha * _tiles_to_rows(h_ref, (), 0, tm, ROW_TILE)
    for k in range(TOP_K):
        acc = acc + _tiles_to_rows(y4_ref, (), k * ROW_TILE, tm, TOP_K * ROW_TILE) * gates[:, k:k + 1]
    out_ref[...] = _ln(acc, g_ref[...], b_ref[...])


def _combine(y4, gates_t, h, n, g, b, alpha):
    tm = _pick(n, (224, 128, 64, 32, 16, 8))
    row = lambda i: (i, 0)
    const = lambda i: (0, 0)
    return pl.pallas_call(
        functools.partial(_combine_kernel, alpha=alpha),
        grid=(n // tm,),
        in_specs=[pl.BlockSpec((TOP_K * tm * ROW_TILE, LANES), row), pl.BlockSpec((tm, TOP_K), row),
                  pl.BlockSpec((tm * ROW_TILE, LANES), row),
                  pl.BlockSpec((1, D_MODEL), const), pl.BlockSpec((1, D_MODEL), const)],
        out_specs=pl.BlockSpec((tm, D_MODEL), row),
        out_shape=_sds((n, D_MODEL)),
        compiler_params=_cparams("parallel"), name="moe_combine",
    )(y4, gates_t, h, g, b)


def _group_layout(cnt_tiles, nblk, bm):
    cnt = cnt_tiles.astype(jnp.int32)
    counts = jnp.sum(cnt, axis=1)
    padded = (counts + bm - 1) // bm * bm
    pad_end = jnp.cumsum(padded)
    pad_start = pad_end - padded
    base = pad_start[:, None] + jnp.cumsum(cnt, axis=1) - cnt
    starts = jnp.arange(nblk, dtype=jnp.int32) * bm
    block_e = jnp.minimum(jnp.sum((pad_end[None, :] <= starts[:, None]).astype(jnp.int32), axis=1),
                          N_EXPERTS - 1).astype(jnp.int32)
    n_used = (pad_end[-1] // bm).astype(jnp.int32).reshape(1)
    valid_end = (pad_start + counts).astype(jnp.int32)
    return base.T.astype(F32)[:, :, None], valid_end, pad_end.astype(jnp.int32), block_e, n_used


def _pack_w_in(w):
    o = 0
    xa, o = w[:, o:o + D_A], o + D_A
    ga, o = w[:, o:o + D_A], o + D_A
    vb, o = w[:, o:o + D_B], o + D_B
    gb, o = w[:, o:o + D_B], o + D_B
    q, o = w[:, o:o + D_QK_C], o + D_QK_C
    k, o = w[:, o:o + D_QK_C], o + D_QK_C
    v, o = w[:, o:o + D_C], o + D_C
    r, o = w[:, o:o + D_C], o + D_C
    z = w[:, o:o + GATE_RANK]
    zq = jnp.zeros((w.shape[0], PC_K - PC_Z - GATE_RANK), w.dtype)
    zk = jnp.zeros((w.shape[0], PC_V - PC_K - D_QK_C), w.dtype)
    return jnp.concatenate([xa, ga, vb, gb, q, z, zq, k, zk, v, r], axis=1).astype(BF16)


def _block_diag(w):
    eye = jnp.eye(N_BLK_A, dtype=w.dtype)
    return jnp.einsum("hij,hg->higj", w, eye).reshape(D_A, D_A)


def kernel(x_prompt, x_sample, state_conv_a, state_rglru, state_conv_b, state_gla, meta_tokens, ln0_g, ln0_b,
           w_in, conv_a_w, conv_a_b, w_rg, b_rg, w_ig, b_ig, lru_lambda, conv_b_w, conv_b_b, ln_b_g, ln_b_b,
           w_gate2, b_gate, gla_norm_g, w_out, ln1_g, ln1_b, router_w, router_b, moe_w1, moe_b1, moe_w2, moe_b2,
           ln2_g, ln2_b):
    bp, seq, _ = x_prompt.shape
    bs, dseq, _ = x_sample.shape
    depth = w_in.shape[0]
    tp = N_META + seq
    ts = SAMPLE_PAD_T
    np_rows = bp * tp
    ns_rows = bs * dseq
    n = np_rows + ns_rows
    alpha = (2 * depth) ** 0.25
    row = lambda a: a.reshape(1, -1)

    meta = jnp.broadcast_to(meta_tokens[None], (bp, N_META, D_MODEL))
    xp_in = jnp.concatenate([meta, x_prompt], axis=1).reshape(np_rows, D_MODEL)
    xs_in = jnp.pad(x_sample, ((0, 0), (0, ts - dseq), (0, 0))).reshape(bs * ts, D_MODEL)
    zeros_p = (jnp.zeros((bp, CONV_A - 1, D_A), F32), jnp.zeros((bp, 1, D_A), F32),
               jnp.zeros((bp, CONV_B - 1, D_B), F32), jnp.zeros((bp, N_HEADS_C, DK_C, DV_C), F32))
    sb = _pick(bs, (8, 4, 2, 1))
    tm_p = _pick(np_rows, (384, 512, 256, 128, 64, 32, 16, 8))
    tm_s = _pick(math.gcd(ns_rows, np_rows), (128, 64, 32, 16, 8))
    nblk = -(-(TOP_K * n) // MOE_BM) + N_EXPERTS
    while (nblk * MOE_BM // TOP_K) % math.lcm(tm_p, tm_s):
        nblk += 1

    new_p = ([], [], [], [])
    new_s = ([], [], [], [])
    x_all = None
    for l in range(depth):
        w_packed = _pack_w_in(w_in[l])
        wg = jnp.concatenate([_block_diag(w_rg[l]), _block_diag(w_ig[l])], axis=1).astype(BF16)
        bg = jnp.concatenate([b_rg[l], b_ig[l]]).reshape(1, -1)
        w_out_b = w_out[l].astype(BF16)
        ng = row(jnp.tile(gla_norm_g[l], N_HEADS_C))
        first = l == 0
        if first:
            pa_p, pb_p, pc_p, xn_p = _inproj(xp_in, 0, np_rows, row(ln0_g), row(ln0_b), w_packed, True)
            pa_s, pb_s, pc_s, xn_s = _inproj(xs_in, 0, bs * ts, row(ln0_g), row(ln0_b), w_packed, True)
            res_p, res_p_row0 = xn_p, 0
        else:
            pa_p, pb_p, pc_p = _inproj(x_all, 0, np_rows, row(ln0_g), row(ln0_b), w_packed, False)
            xn_s = jnp.pad(x_all[np_rows:].reshape(bs, dseq, D_MODEL),
                           ((0, 0), (0, ts - dseq), (0, 0))).reshape(bs * ts, D_MODEL)
            pa_s, pb_s, pc_s = _inproj(xn_s, 0, bs * ts, row(ln0_g), row(ln0_b), w_packed, False)
            res_p, res_p_row0 = x_all, 0

        outs = []
        for (pa, pb, pc, nb, t, tv, bb, st) in (
                (pa_p, pb_p, pc_p, bp, tp, tp, 1, zeros_p),
                (pa_s, pb_s, pc_s, bs, ts, dseq, sb,
                 (state_conv_a[l], state_rglru[l].reshape(bs, 1, D_A), state_conv_b[l], state_gla[l]))):
            ya, ca_new, h_last = _rglru(pa.reshape(nb, t, PA_W), st[0], st[1], conv_a_w[l], row(conv_a_b[l]),
                                        wg, bg, row(lru_lambda[l]), tv, bb)
            yb, cb_new = _convb(pb.reshape(nb, t, PB_W), st[2], conv_b_w[l], row(conv_b_b[l]),
                                row(ln_b_g[l]), row(ln_b_b[l]), tv, bb)
            yc, s_new = _gla(pc, nb, t, st[3], w_gate2[l], row(b_gate[l]), ng, tv, bb)
            outs.append((ya.reshape(nb * t, D_A), yb.reshape(nb * t, D_B), yc,
                         ca_new, h_last.reshape(nb, D_A), cb_new, s_new))
        (ya_p, yb_p, yc_p, *st_p), (ya_s, yb_s, yc_s, *st_s) = outs
        for j in range(4):
            new_p[j].append(st_p[j])
            new_s[j].append(st_s[j])

        valid = lambda a: a.reshape(bs, ts, -1)[:, :dseq].reshape(ns_rows, -1)
        h_all = _outproj(ya_p, yb_p, yc_p, res_p, res_p_row0, w_out_b, row(ln1_g[l]), row(ln1_b[l]), alpha,
                         tm_p, nblk * MOE_BM // TOP_K)
        h_all = _outproj(valid(ya_s), valid(yb_s), valid(yc_s), valid(xn_s), 0, w_out_b, row(ln1_g[l]),
                         row(ln1_b[l]), alpha, tm_s, nblk * MOE_BM // TOP_K, out_row0=np_rows, into=h_all)

        top_idx, gates, cnt = _router(h_all, n, router_w[l].T, router_b[l].reshape(-1, 1))
        base, valid_end, pad_end, block_e, n_used = _group_layout(cnt[:, ::LANES], nblk, MOE_BM)
        dest = _dest(top_idx, base)
        inv = _invert(valid_end, pad_end, dest, nblk * MOE_BM)
        y4 = _moe(block_e, n_used, inv, h_all, l, moe_w1, moe_b1[:, :, None, :], moe_w2, moe_b2[:, :, None, :],
                  MOE_BM)
        x_all = _combine(y4, gates.T, h_all, n, row(ln2_g[l]), row(ln2_b[l]), alpha)

    y_p = x_all[:np_rows].reshape(bp, tp, D_MODEL)[:, N_META:]
    y_s = x_all[np_rows:].reshape(bs, dseq, D_MODEL)
    return (y_p, y_s,
            jnp.stack(new_p[0]), jnp.stack(new_p[1]), jnp.stack(new_p[2]), jnp.stack(new_p[3]),
            jnp.stack(new_s[0]), jnp.stack(new_s[1]), jnp.stack(new_s[2]), jnp.stack(new_s[3]))
```

```python
import functools
import math

import jax
import jax.numpy as jnp
from jax import lax
from jax.experimental import pallas as pl
from jax.experimental.pallas import tpu as pltpu

F32 = jnp.float32
BF16 = jnp.bfloat16

D_MODEL = 1024
N_META = 16
D_A = 384
D_B = 256
D_C = 384
N_BLK_A = 8
BLK_A = D_A // N_BLK_A
CONV_A = 4
RG_C = 8.0
CONV_B = 31
N_HEADS_C = 4
DV_C = D_C // N_HEADS_C
DK_C = DV_C // 2
D_QK_C = N_HEADS_C * DK_C
GATE_RANK = 16
GATE_TAU = 16.0
N_EXPERTS = 32
TOP_K = 4
D_FF = D_MODEL
SWIGLU_LIMIT = 7.0
SWIGLU_ALPHA = 1.702
LN_EPS = 1e-5

PA_W = 2 * D_A
PB_W = 2 * D_B
PC_Q, PC_Z, PC_K, PC_V, PC_R, PC_W = 0, 192, 256, 512, 896, 1280
P_W = PA_W + PB_W + PC_W

SUBLANES = 8
LANES = 128
ROW_TILE = D_MODEL // LANES
VMEM_LIMIT_BYTES = 56 * 1024 * 1024
MOE_BM = 256
FF_CHUNK = 256
DMA_GROUP = 8
W_CAST_ROWS = 128
SAMPLE_PAD_T = 8


def _cparams(*sem):
    return pltpu.CompilerParams(dimension_semantics=sem, vmem_limit_bytes=VMEM_LIMIT_BYTES)


def _sds(shape, dtype=F32):
    return jax.ShapeDtypeStruct(shape, dtype)


def _pick(n, prefs):
    for p in prefs:
        if n % p == 0:
            return p
    raise ValueError(f"no tile for {n} in {prefs}")


def _ln(x, g, b):
    mu = jnp.mean(x, axis=-1, keepdims=True)
    xc = x - mu
    var = jnp.mean(xc * xc, axis=-1, keepdims=True)
    return xc * lax.rsqrt(var + LN_EPS) * g + b


def _sigmoid(x):
    return 1.0 / (1.0 + jnp.exp(-x))


def _split_bf16(x):
    hi = x.astype(BF16)
    lo = (x - hi.astype(F32)).astype(BF16)
    return hi, lo


def _dot(a, b):
    return jnp.dot(a, b, preferred_element_type=F32)


def _inproj_kernel(x_ref, g_ref, b_ref, w_ref, pa_ref, pb_ref, pc_ref, *maybe_xn, apply_ln):
    x = x_ref[...]
    if apply_ln:
        x = _ln(x, g_ref[...], b_ref[...])
        maybe_xn[0][...] = x
    xb = x.astype(BF16)
    pa_ref[...] = _dot(xb, w_ref[:, 0:PA_W])
    pb_ref[...] = _dot(xb, w_ref[:, PA_W:PA_W + PB_W])
    pc_ref[...] = _dot(xb, w_ref[:, PA_W + PB_W:P_W])


def _inproj(x, row0, nrows, ln_g, ln_b, w_packed, apply_ln):
    tm = _pick(nrows, (384, 512, 256, 128, 64, 32, 16, 8))
    while row0 % tm:
        tm //= 2
    off = row0 // tm
    const = lambda i: (0, 0)
    row = lambda i: (i, 0)
    out_shape = [_sds((nrows, PA_W)), _sds((nrows, PB_W)), _sds((nrows, PC_W))]
    out_specs = [pl.BlockSpec((tm, PA_W), row), pl.BlockSpec((tm, PB_W), row), pl.BlockSpec((tm, PC_W), row)]
    if apply_ln:
        out_shape.append(_sds((nrows, D_MODEL)))
        out_specs.append(pl.BlockSpec((tm, D_MODEL), row))
    return pl.pallas_call(
        functools.partial(_inproj_kernel, apply_ln=apply_ln),
        grid=(nrows // tm,),
        in_specs=[pl.BlockSpec((tm, D_MODEL), lambda i: (i + off, 0)),
                  pl.BlockSpec((1, D_MODEL), const), pl.BlockSpec((1, D_MODEL), const),
                  pl.BlockSpec((D_MODEL, P_W), const)],
        out_specs=out_specs, out_shape=out_shape,
        compiler_params=_cparams("parallel"), name="inproj",
    )(x, ln_g, ln_b, w_packed)


def _rglru_kernel(p_ref, cbuf_ref, h0_ref, cw_ref, cb_ref, wg_ref, bg_ref, lam_ref,
                  y_ref, cnew_ref, hlast_ref, xp_scr, a_scr, h_scr, *, T, Tc, Tv, Bb):
    lam = lam_ref[...]
    softplus_neg = jnp.maximum(-lam, 0.0) + jnp.log1p(jnp.exp(-jnp.abs(lam)))
    c_decay = -RG_C * softplus_neg
    cw = cw_ref[...]
    cb = cb_ref[...]
    bg = bg_ref[...]
    sub = lax.broadcasted_iota(jnp.int32, (Tc, D_A), 0) % SUBLANES
    halo = SUBLANES - (CONV_A - 1)
    for b in range(Bb):
        xp_scr[halo:SUBLANES, :] = cbuf_ref[b]
        xp_scr[SUBLANES:SUBLANES + T, :] = p_ref[b, :, 0:D_A]
        cnew_ref[b] = xp_scr[halo + Tv:SUBLANES + Tv, :]

        def chunk(ci, h_b):
            r0 = pl.multiple_of(ci * Tc, SUBLANES)
            win = xp_scr[pl.ds(r0, Tc + SUBLANES), :]
            xc = cb + cw[0:1] * win[halo:halo + Tc]
            for j in range(1, CONV_A):
                xc = xc + cw[j:j + 1] * win[halo + j:halo + j + Tc]
            gates = _dot(xc.astype(BF16), wg_ref[...]) + bg
            r = _sigmoid(gates[:, 0:D_A])
            i = _sigmoid(gates[:, D_A:2 * D_A])
            log_a = c_decay * r
            a = jnp.exp(log_a)
            u = jnp.sqrt(1.0 - a * a) * (i * xc)
            for s in (1, 2, 4):
                keep = sub >= s
                a_prev = pltpu.roll(a, s, 0)
                u_prev = pltpu.roll(u, s, 0)
                u = jnp.where(keep, a * u_prev + u, u)
                a = jnp.where(keep, a * a_prev, a)
            a_scr[...] = a
            h_scr[pl.ds(r0, Tc), :] = u

            def group(gi, h_b):
                c0 = pl.multiple_of(gi * SUBLANES, SUBLANES)
                g0 = pl.multiple_of(r0 + gi * SUBLANES, SUBLANES)
                h8 = a_scr[pl.ds(c0, SUBLANES), :] * h_b + h_scr[pl.ds(g0, SUBLANES), :]
                h_scr[pl.ds(g0, SUBLANES), :] = h8
                return jnp.broadcast_to(h8[SUBLANES - 1:SUBLANES, :], (SUBLANES, D_A))

            h_b = lax.fori_loop(0, Tc // SUBLANES, group, h_b)
            ga = p_ref[b, pl.ds(r0, Tc), D_A:2 * D_A]
            gelu = 0.5 * ga * (1.0 + jnp.tanh(0.7978845608028654 * (ga + 0.044715 * ga * ga * ga)))
            y_ref[b, pl.ds(r0, Tc), :] = h_scr[pl.ds(r0, Tc), :] * gelu
            return h_b

        h_b = jnp.broadcast_to(h0_ref[b], (SUBLANES, D_A))
        lax.fori_loop(0, T // Tc, chunk, h_b)
        hlast_ref[b] = h_scr[Tv - 1:Tv, :]


def _rglru(pa3, cbuf, h0, cw, cb, wg, bg, lam, Tv, Bb):
    B, T, _ = pa3.shape
    Tc = _pick(T, (344, 256, 128, 64, 48, 32, 16, 8))
    const2 = lambda i: (0, 0)
    seq3 = lambda i: (i, 0, 0)
    return pl.pallas_call(
        functools.partial(_rglru_kernel, T=T, Tc=Tc, Tv=Tv, Bb=Bb),
        grid=(B // Bb,),
        in_specs=[pl.BlockSpec((Bb, T, PA_W), seq3), pl.BlockSpec((Bb, CONV_A - 1, D_A), seq3),
                  pl.BlockSpec((Bb, 1, D_A), seq3), pl.BlockSpec((CONV_A, D_A), const2),
                  pl.BlockSpec((1, D_A), const2), pl.BlockSpec((D_A, 2 * D_A), const2),
                  pl.BlockSpec((1, 2 * D_A), const2), pl.BlockSpec((1, D_A), const2)],
        out_specs=[pl.BlockSpec((Bb, T, D_A), seq3), pl.BlockSpec((Bb, CONV_A - 1, D_A), seq3),
                   pl.BlockSpec((Bb, 1, D_A), seq3)],
        out_shape=[_sds((B, T, D_A)), _sds((B, CONV_A - 1, D_A)), _sds((B, 1, D_A))],
        scratch_shapes=[pltpu.VMEM((T + 2 * SUBLANES, D_A), F32), pltpu.VMEM((Tc, D_A), F32),
                        pltpu.VMEM((T, D_A), F32)],
        compiler_params=_cparams("parallel"), name="rglru",
    )(pa3, cbuf, h0, cw, cb, wg, bg, lam)


B_HALO = 32


def _convb_kernel(p_ref, buf_ref, w_ref, cb_ref, g_ref, b_ref, y_ref, bnew_ref, u_scr, *, T, Tc, Tv, Bb):
    w = w_ref[...]
    cb = cb_ref[...]
    g = g_ref[...]
    bb = b_ref[...]
    first = B_HALO - (CONV_B - 1)
    for b in range(Bb):
        u_scr[0:first, :] = jnp.zeros((first, D_B), F32)
        u_scr[first:B_HALO, :] = buf_ref[b]
        u_scr[B_HALO:B_HALO + T, :] = p_ref[b, :, 0:D_B] * _sigmoid(p_ref[b, :, D_B:2 * D_B])
        bnew_ref[b] = u_scr[first + Tv:B_HALO + Tv, :]

        def chunk(ci, carry):
            r0 = pl.multiple_of(ci * Tc, SUBLANES)
            win = u_scr[pl.ds(r0, Tc + B_HALO), :]
            shifted = [win] + [pltpu.roll(win, Tc + B_HALO - s, 0) for s in range(1, SUBLANES)]
            acc = cb
            for j in range(CONV_B):
                a, s = divmod(first + j, SUBLANES)
                acc = acc + w[j:j + 1] * shifted[s][a * SUBLANES:a * SUBLANES + Tc]
            yn = _ln(acc, g, bb)
            y_ref[b, pl.ds(r0, Tc), :] = yn * _sigmoid(yn)
            return carry

        lax.fori_loop(0, T // Tc, chunk, 0)


def _convb(pb3, buf, w, cb, g, b, Tv, Bb):
    B, T, _ = pb3.shape
    Tc = _pick(T, (48, 32, 16, 8))
    const2 = lambda i: (0, 0)
    seq3 = lambda i: (i, 0, 0)
    return pl.pallas_call(
        functools.partial(_convb_kernel, T=T, Tc=Tc, Tv=Tv, Bb=Bb),
        grid=(B // Bb,),
        in_specs=[pl.BlockSpec((Bb, T, PB_W), seq3), pl.BlockSpec((Bb, CONV_B - 1, D_B), seq3),
                  pl.BlockSpec((CONV_B, D_B), const2), pl.BlockSpec((1, D_B), const2),
                  pl.BlockSpec((1, D_B), const2), pl.BlockSpec((1, D_B), const2)],
        out_specs=[pl.BlockSpec((Bb, T, D_B), seq3), pl.BlockSpec((Bb, CONV_B - 1, D_B), seq3)],
        out_shape=[_sds((B, T, D_B)), _sds((B, CONV_B - 1, D_B))],
        scratch_shapes=[pltpu.VMEM((T + B_HALO, D_B), F32)],
        compiler_params=_cparams("parallel"), name="convb",
    )(pb3, buf, w, cb, g, b)


def _gla_kernel(p_ref, s0_ref, wg2_ref, bgate_ref, ng_ref, y_ref, snew_ref, s_scr, g_scr, *, T, C, Tb, Tv, Bb):
    ri = lax.broadcasted_iota(jnp.int32, (C, C), 0)
    ci_ = lax.broadcasted_iota(jnp.int32, (C, C), 1)
    tril = ri >= ci_
    lane_k = lax.broadcasted_iota(jnp.int32, (1, D_QK_C), 1)
    lane_v = lax.broadcasted_iota(jnp.int32, (1, D_C), 1)
    hm_k = [(lane_k >= h * DK_C) & (lane_k < (h + 1) * DK_C) for h in range(N_HEADS_C)]
    hm_v = [(lane_v >= h * DV_C) & (lane_v < (h + 1) * DV_C) for h in range(N_HEADS_C)]
    rs = lax.broadcasted_iota(jnp.int32, (D_C, D_QK_C), 0)
    cs = lax.broadcasted_iota(jnp.int32, (D_C, D_QK_C), 1)
    bd_t = (rs >= 0) & (rs < 0)
    for h in range(N_HEADS_C):
        bd_t = bd_t | ((rs >= h * DV_C) & (rs < (h + 1) * DV_C) & (cs >= h * DK_C) & (cs < (h + 1) * DK_C))
    rm = lax.broadcasted_iota(jnp.int32, (D_C, D_C), 0)
    cm = lax.broadcasted_iota(jnp.int32, (D_C, D_C), 1)
    seg = (rm >= 0) & (rm < 0)
    for h in range(N_HEADS_C):
        seg = seg | ((rm >= h * DV_C) & (rm < (h + 1) * DV_C) & (cm >= h * DV_C) & (cm < (h + 1) * DV_C))
    mseg = jnp.where(seg, 1.0, 0.0).astype(BF16)
    wg2 = wg2_ref[...].astype(BF16)
    bgate = bgate_ref[...]
    ng = ng_ref[...]
    rowi = lax.broadcasted_iota(jnp.int32, (C, 1), 0)
    tdims = (((0,), (0,)), ((), ()))

    n_chunks = T // C
    nt_dims = (((1,), (1,)), ((), ()))
    tril4 = jnp.concatenate([tril] * N_HEADS_C, axis=0)
    scan_shifts = [s for s in (1, 2, 4, 8, 16, 32) if s < C]
    scan_keep = [rowi >= s for s in scan_shifts]

    def tile_rows(ti, tb, base=0):
        r0 = ti * tb
        return pl.ds(base + (r0 if isinstance(ti, int) else pl.multiple_of(r0, tb)), tb)

    def gates(ti, carry):
        rows = tile_rows(ti, Tb)
        z = p_ref[rows, PC_Z:PC_Z + GATE_RANK]
        pre = _dot(z.astype(BF16), wg2) + bgate
        g = (jnp.minimum(pre, 0.0) - jnp.log1p(jnp.exp(-jnp.abs(pre)))) * (1.0 / GATE_TAU)
        rid = ti * Tb + lax.broadcasted_iota(jnp.int32, (Tb, 1), 0)
        if Bb > 1:
            rid = rid & (T - 1)
        g_scr[rows, :] = jnp.where(rid < Tv, g, 0.0)
        return carry
    lax.fori_loop(0, Bb * T // Tb, gates, 0)

    for b in range(Bb):
        def chunk(ci, carry):
            rows = tile_rows(ci, C, b * T)
            q = p_ref[rows, PC_Q:PC_Q + D_QK_C] * (DK_C ** -0.5)
            k = p_ref[rows, PC_K:PC_K + D_QK_C]
            v = p_ref[rows, PC_V:PC_V + D_C]
            k = jnp.where((ci * C + rowi) < Tv, k, 0.0)
            gcum = g_scr[rows, :]
            for s, keep in zip(scan_shifts, scan_keep):
                gcum = gcum + jnp.where(keep, pltpu.roll(gcum, s, 0), 0.0)
            g_last = gcum[C - 1:C, :]
            g_mid = gcum[C // 2 - 1:C // 2, :]
            vb = v.astype(BF16)
            qt = q * jnp.exp(gcum - g_mid)
            ktb = (k * jnp.exp(g_mid - gcum)).astype(BF16)
            q4 = jnp.concatenate([jnp.where(hm_k[h], qt, 0.0) for h in range(N_HEADS_C)], axis=0).astype(BF16)
            sc = lax.dot_general(q4, ktb, nt_dims, preferred_element_type=F32)
            r4 = _dot(jnp.where(tril4, sc, 0.0).astype(BF16), vb)
            o = jnp.where(hm_v[0], r4[0:C], 0.0)
            for h in range(1, N_HEADS_C):
                o = o + jnp.where(hm_v[h], r4[h * C:(h + 1) * C], 0.0)
            kd = (k * jnp.exp(g_last - gcum)).astype(BF16)
            upd_t = lax.dot_general(vb, kd, tdims, preferred_element_type=F32)
            s_in = s_scr[...]
            o = o + lax.dot_general((q * jnp.exp(gcum)).astype(BF16), s_in.astype(BF16), nt_dims,
                                    preferred_element_type=F32)
            s_scr[...] = s_in * jnp.exp(g_last) + jnp.where(bd_t, upd_t, 0.0)
            y_ref[rows, :] = o
            return carry

        s_scr[...] = jnp.zeros((D_C, D_QK_C), F32)
        for h in range(N_HEADS_C):
            s_scr[h * DV_C:(h + 1) * DV_C, h * DK_C:(h + 1) * DK_C] = s0_ref[b, h]
        lax.fori_loop(0, n_chunks, chunk, 0)
        for h in range(N_HEADS_C):
            snew_ref[b, h] = s_scr[h * DV_C:(h + 1) * DV_C, h * DK_C:(h + 1) * DK_C]

    def finish(ti, carry):
        rows = tile_rows(ti, Tb)
        o = y_ref[rows, :]
        rg = p_ref[rows, PC_R:PC_R + D_C]
        o2_hi, o2_lo = _split_bf16(o * o)
        ms = (_dot(o2_hi, mseg) + _dot(o2_lo, mseg)) * (1.0 / DV_C)
        y_ref[rows, :] = o * lax.rsqrt(ms + LN_EPS) * ng * (rg * _sigmoid(rg))
        return carry
    lax.fori_loop(0, Bb * T // Tb, finish, 0)


def _gla(pc, B, T, s0, wg2, bgate, ng, Tv, Bb):
    assert Bb == 1 or T & (T - 1) == 0
    C = _pick(T, (48, 32, 16, 8))
    Tb = _pick(Bb * T, (344, 256, 128, 64, 48, 32, 16, 8))
    const2 = lambda i: (0, 0)
    row2 = lambda i: (i, 0)
    seq4 = lambda i: (i, 0, 0, 0)
    st = (Bb, N_HEADS_C, DV_C, DK_C)
    y, s_new_t = pl.pallas_call(
        functools.partial(_gla_kernel, T=T, C=C, Tb=Tb, Tv=Tv, Bb=Bb),
        grid=(B // Bb,),
        in_specs=[pl.BlockSpec((Bb * T, PC_W), row2), pl.BlockSpec(st, seq4),
                  pl.BlockSpec((GATE_RANK, D_QK_C), const2), pl.BlockSpec((1, D_QK_C), const2),
                  pl.BlockSpec((1, D_C), const2)],
        out_specs=[pl.BlockSpec((Bb * T, D_C), row2), pl.BlockSpec(st, seq4)],
        out_shape=[_sds((B * T, D_C)), _sds((B, N_HEADS_C, DV_C, DK_C))],
        scratch_shapes=[pltpu.VMEM((D_C, D_QK_C), F32), pltpu.VMEM((Bb * T, D_QK_C), F32)],
        compiler_params=_cparams("parallel"), name="gla",
    )(pc, jnp.swapaxes(s0, 2, 3), wg2, bgate, ng)
    return y, jnp.swapaxes(s_new_t, 2, 3)


def _tiles_to_rows(ref, lead, start, rows, stride):
    return jnp.concatenate([ref[lead + (pl.ds(start + s, rows, stride=stride), slice(None))]
                            for s in range(ROW_TILE)], axis=1)


def _rows_to_tiles(ref, lead, rows, val):
    for s in range(ROW_TILE):
        ref[lead + (pl.ds(s, rows, stride=ROW_TILE), slice(None))] = val[:, s * LANES:(s + 1) * LANES]


def _outproj_kernel(ya_ref, yb_ref, yc_ref, x_ref, w_ref, g_ref, b_ref, *rest, alpha, n_blocks):
    h_ref = rest[-1]

    @pl.when(pl.program_id(0) < n_blocks)
    def _():
        y = (_dot(ya_ref[...].astype(BF16), w_ref[0:D_A, :])
             + _dot(yb_ref[...].astype(BF16), w_ref[D_A:D_A + D_B, :])
             + _dot(yc_ref[...].astype(BF16), w_ref[D_A + D_B:D_MODEL, :]))
        h = _ln(alpha * x_ref[...] + y, g_ref[...], b_ref[...])
        _rows_to_tiles(h_ref, (), h.shape[0], h)

    @pl.when(pl.program_id(0) >= n_blocks)
    def _():
        h_ref[...] = jnp.zeros(h_ref.shape, F32)


def _outproj(ya, yb, yc, x, row0, w, g, b, alpha, tm, out_rows, out_row0=0, into=None):
    n = ya.shape[0]
    nb = n // tm
    assert n % tm == 0 and row0 % tm == 0 and out_row0 % tm == 0 and out_rows % tm == 0
    off, out_off = row0 // tm, out_row0 // tm
    grid = nb if into is not None else out_rows // tm
    row = lambda i: (jnp.minimum(i, nb - 1), 0)
    const = lambda i: (0, 0)
    in_specs = [pl.BlockSpec((tm, D_A), row), pl.BlockSpec((tm, D_B), row), pl.BlockSpec((tm, D_C), row),
                pl.BlockSpec((tm, D_MODEL), lambda i: (jnp.minimum(i, nb - 1) + off, 0)),
                pl.BlockSpec((D_MODEL, D_MODEL), const), pl.BlockSpec((1, D_MODEL), const),
                pl.BlockSpec((1, D_MODEL), const)]
    args = [ya, yb, yc, x, w, g, b]
    aliases = {}
    if into is not None:
        in_specs.append(pl.BlockSpec(memory_space=pl.ANY))
        args.append(into)
        aliases = {len(args) - 1: 0}
    return pl.pallas_call(
        functools.partial(_outproj_kernel, alpha=alpha, n_blocks=nb),
        grid=(grid,),
        in_specs=in_specs,
        out_specs=pl.BlockSpec((tm * ROW_TILE, LANES), lambda i: (i + out_off, 0)),
        out_shape=_sds((out_rows * ROW_TILE, LANES)),
        input_output_aliases=aliases,
        compiler_params=_cparams("arbitrary"), name="outproj",
    )(*args)


def _router_kernel(h_ref, wt_ref, b_ref, idx_ref, gate_ref, cnt_ref):
    nt = (((1,), (1,)), ((), ()))
    hh, hl = _split_bf16(_tiles_to_rows(h_ref, (), 0, idx_ref.shape[1], ROW_TILE))
    wh, wl = _split_bf16(wt_ref[...])
    logits = (lax.dot_general(wh, hh, nt, preferred_element_type=F32)
              + lax.dot_general(wh, hl, nt, preferred_element_type=F32)
              + lax.dot_general(wl, hh, nt, preferred_element_type=F32)) + b_ref[...]
    eid = lax.broadcasted_iota(jnp.int32, logits.shape, 0)
    vals = []
    member = jnp.zeros(logits.shape, F32)
    for k in range(TOP_K):
        m = jnp.max(logits, axis=0, keepdims=True)
        sel = jnp.min(jnp.where(logits == m, eid, N_EXPERTS), axis=0, keepdims=True)
        idx_ref[k:k + 1, :] = sel
        vals.append(m)
        hit = eid == sel
        member = jnp.where(hit, 1.0, member)
        logits = jnp.where(hit, -jnp.inf, logits)
    es = [jnp.exp(v - vals[0]) for v in vals]
    tot = es[0] + es[1] + es[2] + es[3]
    for k in range(TOP_K):
        gate_ref[k:k + 1, :] = es[k] / tot
    cnt_ref[...] = jnp.broadcast_to(jnp.sum(member, axis=1, keepdims=True), cnt_ref.shape)


def _router_tile(n):
    return _pick(n, (896, 640, 512, 384, 256, 128))


def _router(h, n, wt, b):
    tm = _router_tile(n)
    nt = n // tm
    return pl.pallas_call(
        _router_kernel,
        grid=(nt,),
        in_specs=[pl.BlockSpec((tm * ROW_TILE, LANES), lambda i: (i, 0)),
                  pl.BlockSpec((N_EXPERTS, D_MODEL), lambda i: (0, 0)),
                  pl.BlockSpec((N_EXPERTS, 1), lambda i: (0, 0))],
        out_specs=[pl.BlockSpec((TOP_K, tm), lambda i: (0, i)), pl.BlockSpec((TOP_K, tm), lambda i: (0, i)),
                   pl.BlockSpec((N_EXPERTS, LANES), lambda i: (0, i))],
        out_shape=[_sds((TOP_K, n), jnp.int32), _sds((TOP_K, n)), _sds((N_EXPERTS, nt * LANES))],
        compiler_params=_cparams("parallel"), name="router",
    )(h, wt, b)


def _dest_kernel(idx_ref, base_ref, dest_ref):
    tm = idx_ref.shape[1]
    eid = lax.broadcasted_iota(jnp.int32, (N_EXPERTS, tm), 0)
    hits = [eid == idx_ref[k:k + 1, :] for k in range(TOP_K)]
    member = jnp.zeros((N_EXPERTS, tm), F32)
    for k in range(TOP_K):
        member = jnp.where(hits[k], 1.0, member)
    earlier = (lax.broadcasted_iota(jnp.int32, (tm, tm), 0) < lax.broadcasted_iota(jnp.int32, (tm, tm), 1))
    rank = _dot(member.astype(BF16), jnp.where(earlier, 1.0, 0.0).astype(BF16))
    pos = base_ref[...] + rank
    for k in range(TOP_K):
        dest_ref[k:k + 1, :] = jnp.sum(jnp.where(hits[k], pos, 0.0), axis=0, keepdims=True).astype(jnp.int32)


def _dest(top_idx, base):
    n = top_idx.shape[1]
    tm = _router_tile(n)
    return pl.pallas_call(
        _dest_kernel,
        grid=(n // tm,),
        in_specs=[pl.BlockSpec((TOP_K, tm), lambda i: (0, i)),
                  pl.BlockSpec((None, N_EXPERTS, 1), lambda i: (i, 0, 0))],
        out_specs=pl.BlockSpec((TOP_K, tm), lambda i: (0, i)),
        out_shape=_sds((TOP_K, n), jnp.int32),
        compiler_params=_cparams("parallel"), name="moe_dest",
    )(top_idx, base)


def _invert_kernel(vend_ref, pend_ref, dest_hbm, inv_ref, buf, sem, *, n, rc, n_rows):
    rows_per_slot = n // LANES
    chunks_per_slot = rows_per_slot // rc

    def chunk(c, carry):
        cp = pltpu.make_async_copy(dest_hbm.at[pl.ds(c * rc, rc), :], buf, sem.at[0])
        cp.start()
        cp.wait()
        k = c // chunks_per_slot
        t0 = (c - k * chunks_per_slot) * (rc * LANES)

        def row(r, carry):
            first = (t0 + r * LANES) * TOP_K + k
            for l in range(LANES):
                inv_ref[buf[r, l]] = first + l * TOP_K
            return carry

        return lax.fori_loop(0, rc, row, carry)

    lax.fori_loop(0, TOP_K * chunks_per_slot, chunk, 0)

    def fill(lo, hi, ctr):
        def body(r, ctr):
            inv_ref[r] = ctr
            return ctr + 1
        return lax.fori_loop(lo, hi, body, ctr)

    ctr = jnp.int32(TOP_K * n)
    for e in range(N_EXPERTS):
        ctr = fill(vend_ref[e], pend_ref[e], ctr)
    fill(pend_ref[N_EXPERTS - 1], n_rows, ctr)


def _invert(valid_end, pad_end, dest, n_rows):
    k, n = dest.shape
    rc = _pick(n // LANES, (19, 16, 8, 5, 4, 2, 1))
    return pl.pallas_call(
        functools.partial(_invert_kernel, n=n, rc=rc, n_rows=n_rows),
        grid_spec=pltpu.PrefetchScalarGridSpec(
            num_scalar_prefetch=2, grid=(1,),
            in_specs=[pl.BlockSpec(memory_space=pl.ANY)],
            out_specs=pl.BlockSpec(memory_space=pltpu.SMEM),
            scratch_shapes=[pltpu.SMEM((rc, LANES), jnp.int32), pltpu.SemaphoreType.DMA((1,))]),
        out_shape=_sds((n_rows,), jnp.int32),
        compiler_params=_cparams("arbitrary"), name="moe_invert",
    )(valid_end, pad_end, dest.reshape(k * n // LANES, LANES))


def _moe_kernel(be_ref, nu_ref, inv_ref, h_hbm, w1f_ref, b1_ref, w2f_ref, b2_ref, y4_hbm,
                xbuf, xb, obuf, w1_ref, w2_ref, gsem, ssem, *, bm):
    i = pl.program_id(0)
    n_used = nu_ref[0]
    last_blk = pl.num_programs(0) - 1
    n_ff = D_FF // FF_CHUNK

    def tile(r):
        start = r * ROW_TILE
        return pl.ds(start if isinstance(r, int) else pl.multiple_of(start, ROW_TILE), ROW_TILE)

    def gather_row(blk, slot, j, r=None):
        r = inv_ref[blk * bm + j] if r is None else r
        t = lax.shift_right_logical(r, 2)
        pltpu.make_async_copy(h_hbm.at[tile(t), :], xbuf.at[slot, tile(j), :], gsem.at[slot]).start()

    def scatter_row(blk, slot, j, r=None):
        r = inv_ref[blk * bm + j] if r is None else r
        pltpu.make_async_copy(obuf.at[slot, tile(j), :], y4_hbm.at[tile(r), :], ssem.at[slot]).start()

    def wait_gather(slot):
        pltpu.make_async_copy(h_hbm.at[pl.ds(0, bm * ROW_TILE), :], xbuf.at[slot], gsem.at[slot]).wait()

    def wait_scatter(slot):
        pltpu.make_async_copy(obuf.at[slot], y4_hbm.at[pl.ds(0, bm * ROW_TILE), :], ssem.at[slot]).wait()

    def loop_rows(fn, blk, slot):
        def body(j, c):
            fn(blk, slot, j)
            return c
        lax.fori_loop(0, bm, body, 0)

    def issue_rows(fn, blk, slot, j_lo, j_hi):
        for j0 in range(j_lo, j_hi, DMA_GROUP):
            js = range(j0, j0 + DMA_GROUP)
            ids = [inv_ref[blk * bm + j] for j in js]
            for j, r in zip(js, ids):
                fn(blk, slot, j, r)

    def step():
        slot = i % 2

        @pl.when(i + 1 < n_used)
        def _():
            issue_rows(gather_row, i + 1, 1 - slot, 0, bm)
        wait_gather(slot)
        xb[...] = _tiles_to_rows(xbuf, (slot,), 0, bm, ROW_TILE).astype(BF16)
        acc = jnp.zeros((bm, D_MODEL), F32)
        for c in range(n_ff):
            lo = c * FF_CHUNK
            x = xb[...]
            hg = _dot(x, w1_ref[:, lo:lo + FF_CHUNK]) + b1_ref[:, lo:lo + FF_CHUNK]
            hl = _dot(x, w1_ref[:, D_FF + lo:D_FF + lo + FF_CHUNK]) + b1_ref[:, D_FF + lo:D_FF + lo + FF_CHUNK]
            gate = jnp.minimum(hg, SWIGLU_LIMIT)
            lin = jnp.clip(hl, -SWIGLU_LIMIT, SWIGLU_LIMIT)
            act = gate * _sigmoid(SWIGLU_ALPHA * gate) * (lin + 1.0)
            acc = acc + _dot(act.astype(BF16), w2_ref[lo:lo + FF_CHUNK, :])

        @pl.when(i >= 2)
        def _():
            wait_scatter(slot)
        _rows_to_tiles(obuf, (slot,), bm, acc + b2_ref[...])
        issue_rows(scatter_row, i, slot, 0, bm)

    new_expert = jnp.logical_or(i == 0, be_ref[i] != be_ref[jnp.maximum(i - 1, 0)])

    @pl.when(jnp.logical_and(new_expert, i < n_used))
    def _():
        def cast_rows(r, c):
            rows = pl.ds(pl.multiple_of(r * W_CAST_ROWS, W_CAST_ROWS), W_CAST_ROWS)
            w1_ref[rows, :] = w1f_ref[rows, :].astype(BF16)
            w2_ref[rows, :] = w2f_ref[rows, :].astype(BF16)
            return c
        lax.fori_loop(0, D_MODEL // W_CAST_ROWS, cast_rows, 0)

    @pl.when(jnp.logical_and(i == 0, n_used > 0))
    def _():
        loop_rows(gather_row, 0, 0)

    @pl.when(i < n_used)
    def _():
        step()

    @pl.when(i == n_used - 1)
    def _():
        slot = i % 2
        wait_scatter(slot)

        @pl.when(i >= 1)
        def _():
            wait_scatter(1 - slot)
        obuf[0] = jnp.zeros((bm * ROW_TILE, LANES), F32)

        def zero_block(blk, c):
            rows = bm * ROW_TILE
            cp = pltpu.make_async_copy(obuf.at[0], y4_hbm.at[pl.ds(pl.multiple_of(blk * rows, rows), rows), :],
                                       ssem.at[0])
            cp.start()
            cp.wait()
            return c
        lax.fori_loop(n_used, last_blk + 1, zero_block, 0)


def _moe(block_e, n_used, inv, h, layer, w1, b1, w2, b2, bm):
    nblk = block_e.shape[0]
    wmap = lambda i, be, nu, iv: (layer, be[i], 0, 0)
    return pl.pallas_call(
        functools.partial(_moe_kernel, bm=bm),
        grid_spec=pltpu.PrefetchScalarGridSpec(
            num_scalar_prefetch=3, grid=(nblk,),
            in_specs=[pl.BlockSpec(memory_space=pl.ANY),
                      pl.BlockSpec((None, None, D_MODEL, 2 * D_FF), wmap),
                      pl.BlockSpec((None, None, 1, 2 * D_FF), wmap),
                      pl.BlockSpec((None, None, D_FF, D_MODEL), wmap),
                      pl.BlockSpec((None, None, 1, D_MODEL), wmap)],
            out_specs=pl.BlockSpec(memory_space=pl.ANY),
            scratch_shapes=[pltpu.VMEM((2, bm * ROW_TILE, LANES), F32), pltpu.VMEM((bm, D_MODEL), BF16),
                            pltpu.VMEM((2, bm * ROW_TILE, LANES), F32),
                            pltpu.VMEM((D_MODEL, 2 * D_FF), BF16), pltpu.VMEM((D_FF, D_MODEL), BF16),
                            pltpu.SemaphoreType.DMA((2,)), pltpu.SemaphoreType.DMA((2,))]),
        out_shape=_sds((nblk * bm * ROW_TILE, LANES)),
        compiler_params=_cparams("arbitrary"), name="moe_experts",
    )(block_e, n_used, inv, h, w1, b1, w2, b2)


def _combine_kernel(gates_ref, y4_ref, h_ref, g_ref, b_ref, out_ref, acc_scr, *, alpha):
    tm = out_ref.shape[0]
    tok0 = pl.program_id(0) * tm

    def group(gi, carry):
        t0 = pl.multiple_of(gi * SUBLANES, SUBLANES)
        tiles = []
        for u in range(SUBLANES):
            t = t0 + u
            tile = alpha * h_ref[pl.ds(pl.multiple_of(t * ROW_TILE, ROW_TILE), ROW_TILE), :]
            for k in range(TOP_K):
                r = (t * TOP_K + k) * ROW_TILE
                tile = tile + (gates_ref[(tok0 + t) * TOP_K + k]
                               * y4_ref[pl.ds(pl.multiple_of(r, ROW_TILE), ROW_TILE), :])
            tiles.append(tile.reshape(1, ROW_TILE, LANES))
        rows = jnp.swapaxes(jnp.concatenate(tiles, axis=0), 0, 1)
        for s in range(ROW_TILE):
            acc_scr[pl.ds(t0, SUBLANES), s * LANES:(s + 1) * LANES] = rows[s]
        return carry
    lax.fori_loop(0, tm // SUBLANES, group, 0)
    out_ref[...] = _ln(acc_scr[...], g_ref[...], b_ref[...])


def _combine(y4, gates_flat, h, n, g, b, alpha):
    tm = _pick(n, (224, 128, 64, 32, 16, 8))
    row = lambda i, gt: (i, 0)
    const = lambda i, gt: (0, 0)
    return pl.pallas_call(
        functools.partial(_combine_kernel, alpha=alpha),
        grid_spec=pltpu.PrefetchScalarGridSpec(
            num_scalar_prefetch=1, grid=(n // tm,),
            in_specs=[pl.BlockSpec((TOP_K * tm * ROW_TILE, LANES), row),
                      pl.BlockSpec((tm * ROW_TILE, LANES), row),
                      pl.BlockSpec((1, D_MODEL), const), pl.BlockSpec((1, D_MODEL), const)],
            out_specs=pl.BlockSpec((tm, D_MODEL), row),
            scratch_shapes=[pltpu.VMEM((tm, D_MODEL), F32)]),
        out_shape=_sds((n, D_MODEL)),
        compiler_params=_cparams("arbitrary"), name="moe_combine",
    )(gates_flat, y4, h, g, b)


def _group_layout(cnt_tiles, nblk, bm):
    cnt = cnt_tiles.astype(jnp.int32)
    counts = jnp.sum(cnt, axis=1)
    padded = (counts + bm - 1) // bm * bm
    pad_end = jnp.cumsum(padded)
    pad_start = pad_end - padded
    base = pad_start[:, None] + jnp.cumsum(cnt, axis=1) - cnt
    starts = jnp.arange(nblk, dtype=jnp.int32) * bm
    block_e = jnp.minimum(jnp.sum((pad_end[None, :] <= starts[:, None]).astype(jnp.int32), axis=1),
                          N_EXPERTS - 1).astype(jnp.int32)
    n_used = (pad_end[-1] // bm).astype(jnp.int32).reshape(1)
    valid_end = (pad_start + counts).astype(jnp.int32)
    return base.T.astype(F32)[:, :, None], valid_end, pad_end.astype(jnp.int32), block_e, n_used


def _pack_w_in(w):
    o = 0
    xa, o = w[:, o:o + D_A], o + D_A
    ga, o = w[:, o:o + D_A], o + D_A
    vb, o = w[:, o:o + D_B], o + D_B
    gb, o = w[:, o:o + D_B], o + D_B
    q, o = w[:, o:o + D_QK_C], o + D_QK_C
    k, o = w[:, o:o + D_QK_C], o + D_QK_C
    v, o = w[:, o:o + D_C], o + D_C
    r, o = w[:, o:o + D_C], o + D_C
    z = w[:, o:o + GATE_RANK]
    zq = jnp.zeros((w.shape[0], PC_K - PC_Z - GATE_RANK), w.dtype)
    zk = jnp.zeros((w.shape[0], PC_V - PC_K - D_QK_C), w.dtype)
    return jnp.concatenate([xa, ga, vb, gb, q, z, zq, k, zk, v, r], axis=1).astype(BF16)


def _block_diag(w):
    eye = jnp.eye(N_BLK_A, dtype=w.dtype)
    return jnp.einsum("hij,hg->higj", w, eye).reshape(D_A, D_A)


def kernel(x_prompt, x_sample, state_conv_a, state_rglru, state_conv_b, state_gla, meta_tokens, ln0_g, ln0_b,
           w_in, conv_a_w, conv_a_b, w_rg, b_rg, w_ig, b_ig, lru_lambda, conv_b_w, conv_b_b, ln_b_g, ln_b_b,
           w_gate2, b_gate, gla_norm_g, w_out, ln1_g, ln1_b, router_w, router_b, moe_w1, moe_b1, moe_w2, moe_b2,
           ln2_g, ln2_b):
    bp, seq, _ = x_prompt.shape
    bs, dseq, _ = x_sample.shape
    depth = w_in.shape[0]
    tp = N_META + seq
    ts = SAMPLE_PAD_T
    np_rows = bp * tp
    ns_rows = bs * dseq
    n = np_rows + ns_rows
    alpha = (2 * depth) ** 0.25
    row = lambda a: a.reshape(1, -1)

    meta = jnp.broadcast_to(meta_tokens[None], (bp, N_META, D_MODEL))
    xp_in = jnp.concatenate([meta, x_prompt], axis=1).reshape(np_rows, D_MODEL)
    xs_in = jnp.pad(x_sample, ((0, 0), (0, ts - dseq), (0, 0))).reshape(bs * ts, D_MODEL)
    zeros_p = (jnp.zeros((bp, CONV_A - 1, D_A), F32), jnp.zeros((bp, 1, D_A), F32),
               jnp.zeros((bp, CONV_B - 1, D_B), F32), jnp.zeros((bp, N_HEADS_C, DK_C, DV_C), F32))
    sb = _pick(bs, (8, 4, 2, 1))
    tm_p = _pick(np_rows, (384, 512, 256, 128, 64, 32, 16, 8))
    tm_s = _pick(math.gcd(ns_rows, np_rows), (128, 64, 32, 16, 8))
    nblk = -(-(TOP_K * n) // MOE_BM) + N_EXPERTS
    while (nblk * MOE_BM // TOP_K) % math.lcm(tm_p, tm_s):
        nblk += 1

    new_p = ([], [], [], [])
    new_s = ([], [], [], [])
    x_all = None
    for l in range(depth):
        w_packed = _pack_w_in(w_in[l])
        wg = jnp.concatenate([_block_diag(w_rg[l]), _block_diag(w_ig[l])], axis=1).astype(BF16)
        bg = jnp.concatenate([b_rg[l], b_ig[l]]).reshape(1, -1)
        w_out_b = w_out[l].astype(BF16)
        ng = row(jnp.tile(gla_norm_g[l], N_HEADS_C))
        first = l == 0
        if first:
            pa_p, pb_p, pc_p, xn_p = _inproj(xp_in, 0, np_rows, row(ln0_g), row(ln0_b), w_packed, True)
            pa_s, pb_s, pc_s, xn_s = _inproj(xs_in, 0, bs * ts, row(ln0_g), row(ln0_b), w_packed, True)
            res_p, res_p_row0 = xn_p, 0
        else:
            pa_p, pb_p, pc_p = _inproj(x_all, 0, np_rows, row(ln0_g), row(ln0_b), w_packed, False)
            xn_s = jnp.pad(x_all[np_rows:].reshape(bs, dseq, D_MODEL),
                           ((0, 0), (0, ts - dseq), (0, 0))).reshape(bs * ts, D_MODEL)
            pa_s, pb_s, pc_s = _inproj(xn_s, 0, bs * ts, row(ln0_g), row(ln0_b), w_packed, False)
            res_p, res_p_row0 = x_all, 0

        outs = []
        for (pa, pb, pc, nb, t, tv, bb, st) in (
                (pa_p, pb_p, pc_p, bp, tp, tp, 1, zeros_p),
                (pa_s, pb_s, pc_s, bs, ts, dseq, sb,
                 (state_conv_a[l], state_rglru[l].reshape(bs, 1, D_A), state_conv_b[l], state_gla[l]))):
            ya, ca_new, h_last = _rglru(pa.reshape(nb, t, PA_W), st[0], st[1], conv_a_w[l], row(conv_a_b[l]),
                                        wg, bg, row(lru_lambda[l]), tv, bb)
            yb, cb_new = _convb(pb.reshape(nb, t, PB_W), st[2], conv_b_w[l], row(conv_b_b[l]),
                                row(ln_b_g[l]), row(ln_b_b[l]), tv, bb)
            yc, s_new = _gla(pc, nb, t, st[3], w_gate2[l], row(b_gate[l]), ng, tv, bb)
            outs.append((ya.reshape(nb * t, D_A), yb.reshape(nb * t, D_B), yc,
                         ca_new, h_last.reshape(nb, D_A), cb_new, s_new))
        (ya_p, yb_p, yc_p, *st_p), (ya_s, yb_s, yc_s, *st_s) = outs
        for j in range(4):
            new_p[j].append(st_p[j])
            new_s[j].append(st_s[j])

        valid = lambda a: a.reshape(bs, ts, -1)[:, :dseq].reshape(ns_rows, -1)
        h_all = _outproj(ya_p, yb_p, yc_p, res_p, res_p_row0, w_out_b, row(ln1_g[l]), row(ln1_b[l]), alpha,
                         tm_p, nblk * MOE_BM // TOP_K)
        h_all = _outproj(valid(ya_s), valid(yb_s), valid(yc_s), valid(xn_s), 0, w_out_b, row(ln1_g[l]),
                         row(ln1_b[l]), alpha, tm_s, nblk * MOE_BM // TOP_K, out_row0=np_rows, into=h_all)

        top_idx, gates, cnt = _router(h_all, n, router_w[l].T, router_b[l].reshape(-1, 1))
        base, valid_end, pad_end, block_e, n_used = _group_layout(cnt[:, ::LANES], nblk, MOE_BM)
        dest = _dest(top_idx, base)
        inv = _invert(valid_end, pad_end, dest, nblk * MOE_BM)
        y4 = _moe(block_e, n_used, inv, h_all, l, moe_w1, moe_b1[:, :, None, :], moe_w2, moe_b2[:, :, None, :],
                  MOE_BM)
        x_all = _combine(y4, gates.T.reshape(-1), h_all, n, row(ln2_g[l]), row(ln2_b[l]), alpha)

    y_p = x_all[:np_rows].reshape(bp, tp, D_MODEL)[:, N_META:]
    y_s = x_all[np_rows:].reshape(bs, dseq, D_MODEL)
    return (y_p, y_s,
            jnp.stack(new_p[0]), jnp.stack(new_p[1]), jnp.stack(new_p[2]), jnp.stack(new_p[3]),
            jnp.stack(new_s[0]), jnp.stack(new_s[1]), jnp.stack(new_s[2]), jnp.stack(new_s[3]))
```

```python
import functools
import math

import jax
import jax.numpy as jnp
from jax import lax
from jax.experimental import pallas as pl
from jax.experimental.pallas import tpu as pltpu
from jax.experimental.pallas import tpu_sc as plsc

F32 = jnp.float32
BF16 = jnp.bfloat16

D_MODEL = 1024
N_META = 16
D_A = 384
D_B = 256
D_C = 384
N_BLK_A = 8
BLK_A = D_A // N_BLK_A
CONV_A = 4
RG_C = 8.0
CONV_B = 31
N_HEADS_C = 4
DV_C = D_C // N_HEADS_C
DK_C = DV_C // 2
D_QK_C = N_HEADS_C * DK_C
GATE_RANK = 16
GATE_TAU = 16.0
N_EXPERTS = 32
TOP_K = 4
D_FF = D_MODEL
SWIGLU_LIMIT = 7.0
SWIGLU_ALPHA = 1.702
LN_EPS = 1e-5

PA_W = 2 * D_A
PB_W = 2 * D_B
PC_Q, PC_Z, PC_K, PC_V, PC_R, PC_W = 0, 192, 256, 512, 896, 1280
P_W = PA_W + PB_W + PC_W

SUBLANES = 8
LANES = 128
SC_LANES = 16
SC_CORES, SC_SUBCORES = 2, 16
ROW_TILE = D_MODEL // LANES
VMEM_LIMIT_BYTES = 56 * 1024 * 1024
MOE_BM = 256
FF_CHUNK = 256
DMA_GROUP = 8
W_CAST_ROWS = 128
SAMPLE_PAD_T = 8


def _cparams(*sem):
    return pltpu.CompilerParams(dimension_semantics=sem, vmem_limit_bytes=VMEM_LIMIT_BYTES)


def _sds(shape, dtype=F32):
    return jax.ShapeDtypeStruct(shape, dtype)


def _pick(n, prefs):
    for p in prefs:
        if n % p == 0:
            return p
    raise ValueError(f"no tile for {n} in {prefs}")


def _ln(x, g, b):
    mu = jnp.mean(x, axis=-1, keepdims=True)
    xc = x - mu
    var = jnp.mean(xc * xc, axis=-1, keepdims=True)
    return xc * lax.rsqrt(var + LN_EPS) * g + b


def _sigmoid(x):
    return 1.0 / (1.0 + jnp.exp(-x))


def _split_bf16(x):
    hi = x.astype(BF16)
    lo = (x - hi.astype(F32)).astype(BF16)
    return hi, lo


def _dot(a, b):
    return jnp.dot(a, b, preferred_element_type=F32)


def _inproj_kernel(x_ref, g_ref, b_ref, w_ref, pa_ref, pb_ref, pc_ref, *maybe_xn, apply_ln):
    x = x_ref[...]
    if apply_ln:
        x = _ln(x, g_ref[...], b_ref[...])
        maybe_xn[0][...] = x
    xb = x.astype(BF16)
    pa_ref[...] = _dot(xb, w_ref[:, 0:PA_W])
    pb_ref[...] = _dot(xb, w_ref[:, PA_W:PA_W + PB_W])
    pc_ref[...] = _dot(xb, w_ref[:, PA_W + PB_W:P_W])


def _inproj(x, row0, nrows, ln_g, ln_b, w_packed, apply_ln):
    tm = _pick(nrows, (384, 512, 256, 128, 64, 32, 16, 8))
    while row0 % tm:
        tm //= 2
    off = row0 // tm
    const = lambda i: (0, 0)
    row = lambda i: (i, 0)
    out_shape = [_sds((nrows, PA_W)), _sds((nrows, PB_W)), _sds((nrows, PC_W))]
    out_specs = [pl.BlockSpec((tm, PA_W), row), pl.BlockSpec((tm, PB_W), row), pl.BlockSpec((tm, PC_W), row)]
    if apply_ln:
        out_shape.append(_sds((nrows, D_MODEL)))
        out_specs.append(pl.BlockSpec((tm, D_MODEL), row))
    return pl.pallas_call(
        functools.partial(_inproj_kernel, apply_ln=apply_ln),
        grid=(nrows // tm,),
        in_specs=[pl.BlockSpec((tm, D_MODEL), lambda i: (i + off, 0)),
                  pl.BlockSpec((1, D_MODEL), const), pl.BlockSpec((1, D_MODEL), const),
                  pl.BlockSpec((D_MODEL, P_W), const)],
        out_specs=out_specs, out_shape=out_shape,
        compiler_params=_cparams("parallel"), name="inproj",
    )(x, ln_g, ln_b, w_packed)


def _rglru_kernel(p_ref, cbuf_ref, h0_ref, cw_ref, cb_ref, wg_ref, bg_ref, lam_ref,
                  y_ref, cnew_ref, hlast_ref, xp_scr, a_scr, h_scr, *, T, Tc, Tv, Bb):
    lam = lam_ref[...]
    softplus_neg = jnp.maximum(-lam, 0.0) + jnp.log1p(jnp.exp(-jnp.abs(lam)))
    c_decay = -RG_C * softplus_neg
    cw = cw_ref[...]
    cb = cb_ref[...]
    bg = bg_ref[...]
    sub = lax.broadcasted_iota(jnp.int32, (Tc, D_A), 0) % SUBLANES
    halo = SUBLANES - (CONV_A - 1)
    for b in range(Bb):
        xp_scr[halo:SUBLANES, :] = cbuf_ref[b]
        xp_scr[SUBLANES:SUBLANES + T, :] = p_ref[b, :, 0:D_A]
        cnew_ref[b] = xp_scr[halo + Tv:SUBLANES + Tv, :]

        def chunk(ci, h_b):
            r0 = pl.multiple_of(ci * Tc, SUBLANES)
            win = xp_scr[pl.ds(r0, Tc + SUBLANES), :]
            xc = cb + cw[0:1] * win[halo:halo + Tc]
            for j in range(1, CONV_A):
                xc = xc + cw[j:j + 1] * win[halo + j:halo + j + Tc]
            gates = _dot(xc.astype(BF16), wg_ref[...]) + bg
            r = _sigmoid(gates[:, 0:D_A])
            i = _sigmoid(gates[:, D_A:2 * D_A])
            log_a = c_decay * r
            a = jnp.exp(log_a)
            u = jnp.sqrt(1.0 - a * a) * (i * xc)
            for s in (1, 2, 4):
                keep = sub >= s
                a_prev = pltpu.roll(a, s, 0)
                u_prev = pltpu.roll(u, s, 0)
                u = jnp.where(keep, a * u_prev + u, u)
                a = jnp.where(keep, a * a_prev, a)
            a_scr[...] = a
            h_scr[pl.ds(r0, Tc), :] = u

            def group(gi, h_b):
                c0 = pl.multiple_of(gi * SUBLANES, SUBLANES)
                g0 = pl.multiple_of(r0 + gi * SUBLANES, SUBLANES)
                h8 = a_scr[pl.ds(c0, SUBLANES), :] * h_b + h_scr[pl.ds(g0, SUBLANES), :]
                h_scr[pl.ds(g0, SUBLANES), :] = h8
                return jnp.broadcast_to(h8[SUBLANES - 1:SUBLANES, :], (SUBLANES, D_A))

            h_b = lax.fori_loop(0, Tc // SUBLANES, group, h_b)
            ga = p_ref[b, pl.ds(r0, Tc), D_A:2 * D_A]
            gelu = 0.5 * ga * (1.0 + jnp.tanh(0.7978845608028654 * (ga + 0.044715 * ga * ga * ga)))
            y_ref[b, pl.ds(r0, Tc), :] = h_scr[pl.ds(r0, Tc), :] * gelu
            return h_b

        h_b = jnp.broadcast_to(h0_ref[b], (SUBLANES, D_A))
        lax.fori_loop(0, T // Tc, chunk, h_b)
        hlast_ref[b] = h_scr[Tv - 1:Tv, :]


def _rglru(pa3, cbuf, h0, cw, cb, wg, bg, lam, Tv, Bb):
    B, T, _ = pa3.shape
    Tc = _pick(T, (344, 256, 128, 64, 48, 32, 16, 8))
    const2 = lambda i: (0, 0)
    seq3 = lambda i: (i, 0, 0)
    return pl.pallas_call(
        functools.partial(_rglru_kernel, T=T, Tc=Tc, Tv=Tv, Bb=Bb),
        grid=(B // Bb,),
        in_specs=[pl.BlockSpec((Bb, T, PA_W), seq3), pl.BlockSpec((Bb, CONV_A - 1, D_A), seq3),
                  pl.BlockSpec((Bb, 1, D_A), seq3), pl.BlockSpec((CONV_A, D_A), const2),
                  pl.BlockSpec((1, D_A), const2), pl.BlockSpec((D_A, 2 * D_A), const2),
                  pl.BlockSpec((1, 2 * D_A), const2), pl.BlockSpec((1, D_A), const2)],
        out_specs=[pl.BlockSpec((Bb, T, D_A), seq3), pl.BlockSpec((Bb, CONV_A - 1, D_A), seq3),
                   pl.BlockSpec((Bb, 1, D_A), seq3)],
        out_shape=[_sds((B, T, D_A)), _sds((B, CONV_A - 1, D_A)), _sds((B, 1, D_A))],
        scratch_shapes=[pltpu.VMEM((T + 2 * SUBLANES, D_A), F32), pltpu.VMEM((Tc, D_A), F32),
                        pltpu.VMEM((T, D_A), F32)],
        compiler_params=_cparams("parallel"), name="rglru",
    )(pa3, cbuf, h0, cw, cb, wg, bg, lam)


B_HALO = 32


def _convb_kernel(p_ref, buf_ref, w_ref, cb_ref, g_ref, b_ref, y_ref, bnew_ref, u_scr, *, T, Tc, Tv, Bb):
    w = w_ref[...]
    cb = cb_ref[...]
    g = g_ref[...]
    bb = b_ref[...]
    first = B_HALO - (CONV_B - 1)
    for b in range(Bb):
        u_scr[0:first, :] = jnp.zeros((first, D_B), F32)
        u_scr[first:B_HALO, :] = buf_ref[b]
        u_scr[B_HALO:B_HALO + T, :] = p_ref[b, :, 0:D_B] * _sigmoid(p_ref[b, :, D_B:2 * D_B])
        bnew_ref[b] = u_scr[first + Tv:B_HALO + Tv, :]

        def chunk(ci, carry):
            r0 = pl.multiple_of(ci * Tc, SUBLANES)
            win = u_scr[pl.ds(r0, Tc + B_HALO), :]
            shifted = [win] + [pltpu.roll(win, Tc + B_HALO - s, 0) for s in range(1, SUBLANES)]
            acc = cb
            for j in range(CONV_B):
                a, s = divmod(first + j, SUBLANES)
                acc = acc + w[j:j + 1] * shifted[s][a * SUBLANES:a * SUBLANES + Tc]
            yn = _ln(acc, g, bb)
            y_ref[b, pl.ds(r0, Tc), :] = yn * _sigmoid(yn)
            return carry

        lax.fori_loop(0, T // Tc, chunk, 0)


def _convb(pb3, buf, w, cb, g, b, Tv, Bb):
    B, T, _ = pb3.shape
    Tc = _pick(T, (48, 32, 16, 8))
    const2 = lambda i: (0, 0)
    seq3 = lambda i: (i, 0, 0)
    return pl.pallas_call(
        functools.partial(_convb_kernel, T=T, Tc=Tc, Tv=Tv, Bb=Bb),
        grid=(B // Bb,),
        in_specs=[pl.BlockSpec((Bb, T, PB_W), seq3), pl.BlockSpec((Bb, CONV_B - 1, D_B), seq3),
                  pl.BlockSpec((CONV_B, D_B), const2), pl.BlockSpec((1, D_B), const2),
                  pl.BlockSpec((1, D_B), const2), pl.BlockSpec((1, D_B), const2)],
        out_specs=[pl.BlockSpec((Bb, T, D_B), seq3), pl.BlockSpec((Bb, CONV_B - 1, D_B), seq3)],
        out_shape=[_sds((B, T, D_B)), _sds((B, CONV_B - 1, D_B))],
        scratch_shapes=[pltpu.VMEM((T + B_HALO, D_B), F32)],
        compiler_params=_cparams("parallel"), name="convb",
    )(pb3, buf, w, cb, g, b)


def _gla_kernel(p_ref, s0_ref, wg2_ref, bgate_ref, ng_ref, y_ref, snew_ref, s_scr, g_scr, *, T, C, Tb, Tv, Bb):
    ri = lax.broadcasted_iota(jnp.int32, (C, C), 0)
    ci_ = lax.broadcasted_iota(jnp.int32, (C, C), 1)
    tril = ri >= ci_
    lane_k = lax.broadcasted_iota(jnp.int32, (1, D_QK_C), 1)
    lane_v = lax.broadcasted_iota(jnp.int32, (1, D_C), 1)
    hm_k = [(lane_k >= h * DK_C) & (lane_k < (h + 1) * DK_C) for h in range(N_HEADS_C)]
    hm_v = [(lane_v >= h * DV_C) & (lane_v < (h + 1) * DV_C) for h in range(N_HEADS_C)]
    rs = lax.broadcasted_iota(jnp.int32, (D_C, D_QK_C), 0)
    cs = lax.broadcasted_iota(jnp.int32, (D_C, D_QK_C), 1)
    bd_t = (rs >= 0) & (rs < 0)
    for h in range(N_HEADS_C):
        bd_t = bd_t | ((rs >= h * DV_C) & (rs < (h + 1) * DV_C) & (cs >= h * DK_C) & (cs < (h + 1) * DK_C))
    rm = lax.broadcasted_iota(jnp.int32, (D_C, D_C), 0)
    cm = lax.broadcasted_iota(jnp.int32, (D_C, D_C), 1)
    seg = (rm >= 0) & (rm < 0)
    for h in range(N_HEADS_C):
        seg = seg | ((rm >= h * DV_C) & (rm < (h + 1) * DV_C) & (cm >= h * DV_C) & (cm < (h + 1) * DV_C))
    mseg = jnp.where(seg, 1.0, 0.0).astype(BF16)
    wg2 = wg2_ref[...].astype(BF16)
    bgate = bgate_ref[...]
    ng = ng_ref[...]
    rowi = lax.broadcasted_iota(jnp.int32, (C, 1), 0)
    tdims = (((0,), (0,)), ((), ()))

    n_chunks = T // C
    nt_dims = (((1,), (1,)), ((), ()))
    tril4 = jnp.concatenate([tril] * N_HEADS_C, axis=0)
    scan_shifts = [s for s in (1, 2, 4, 8, 16, 32) if s < C]
    scan_keep = [rowi >= s for s in scan_shifts]

    def tile_rows(ti, tb, base=0):
        r0 = ti * tb
        return pl.ds(base + (r0 if isinstance(ti, int) else pl.multiple_of(r0, tb)), tb)

    def gates(ti, carry):
        rows = tile_rows(ti, Tb)
        z = p_ref[rows, PC_Z:PC_Z + GATE_RANK]
        pre = _dot(z.astype(BF16), wg2) + bgate
        g = (jnp.minimum(pre, 0.0) - jnp.log1p(jnp.exp(-jnp.abs(pre)))) * (1.0 / GATE_TAU)
        rid = ti * Tb + lax.broadcasted_iota(jnp.int32, (Tb, 1), 0)
        if Bb > 1:
            rid = rid & (T - 1)
        g_scr[rows, :] = jnp.where(rid < Tv, g, 0.0)
        return carry
    lax.fori_loop(0, Bb * T // Tb, gates, 0)

    for b in range(Bb):
        def chunk(ci, carry):
            rows = tile_rows(ci, C, b * T)
            q = p_ref[rows, PC_Q:PC_Q + D_QK_C] * (DK_C ** -0.5)
            k = p_ref[rows, PC_K:PC_K + D_QK_C]
            v = p_ref[rows, PC_V:PC_V + D_C]
            k = jnp.where((ci * C + rowi) < Tv, k, 0.0)
            gcum = g_scr[rows, :]
            for s, keep in zip(scan_shifts, scan_keep):
                gcum = gcum + jnp.where(keep, pltpu.roll(gcum, s, 0), 0.0)
            g_last = gcum[C - 1:C, :]
            g_mid = gcum[C // 2 - 1:C // 2, :]
            vb = v.astype(BF16)
            qt = q * jnp.exp(gcum - g_mid)
            ktb = (k * jnp.exp(g_mid - gcum)).astype(BF16)
            q4 = jnp.concatenate([jnp.where(hm_k[h], qt, 0.0) for h in range(N_HEADS_C)], axis=0).astype(BF16)
            sc = lax.dot_general(q4, ktb, nt_dims, preferred_element_type=F32)
            r4 = _dot(jnp.where(tril4, sc, 0.0).astype(BF16), vb)
            o = jnp.where(hm_v[0], r4[0:C], 0.0)
            for h in range(1, N_HEADS_C):
                o = o + jnp.where(hm_v[h], r4[h * C:(h + 1) * C], 0.0)
            kd = (k * jnp.exp(g_last - gcum)).astype(BF16)
            upd_t = lax.dot_general(vb, kd, tdims, preferred_element_type=F32)
            s_in = s_scr[...]
            o = o + lax.dot_general((q * jnp.exp(gcum)).astype(BF16), s_in.astype(BF16), nt_dims,
                                    preferred_element_type=F32)
            s_scr[...] = s_in * jnp.exp(g_last) + jnp.where(bd_t, upd_t, 0.0)
            y_ref[rows, :] = o
            return carry

        s_scr[...] = jnp.zeros((D_C, D_QK_C), F32)
        for h in range(N_HEADS_C):
            s_scr[h * DV_C:(h + 1) * DV_C, h * DK_C:(h + 1) * DK_C] = s0_ref[b, h]
        lax.fori_loop(0, n_chunks, chunk, 0)
        for h in range(N_HEADS_C):
            snew_ref[b, h] = s_scr[h * DV_C:(h + 1) * DV_C, h * DK_C:(h + 1) * DK_C]

    def finish(ti, carry):
        rows = tile_rows(ti, Tb)
        o = y_ref[rows, :]
        rg = p_ref[rows, PC_R:PC_R + D_C]
        o2_hi, o2_lo = _split_bf16(o * o)
        ms = (_dot(o2_hi, mseg) + _dot(o2_lo, mseg)) * (1.0 / DV_C)
        y_ref[rows, :] = o * lax.rsqrt(ms + LN_EPS) * ng * (rg * _sigmoid(rg))
        return carry
    lax.fori_loop(0, Bb * T // Tb, finish, 0)


def _gla(pc, B, T, s0, wg2, bgate, ng, Tv, Bb):
    assert Bb == 1 or T & (T - 1) == 0
    C = _pick(T, (48, 32, 16, 8))
    Tb = _pick(Bb * T, (344, 256, 128, 64, 48, 32, 16, 8))
    const2 = lambda i: (0, 0)
    row2 = lambda i: (i, 0)
    seq4 = lambda i: (i, 0, 0, 0)
    st = (Bb, N_HEADS_C, DV_C, DK_C)
    y, s_new_t = pl.pallas_call(
        functools.partial(_gla_kernel, T=T, C=C, Tb=Tb, Tv=Tv, Bb=Bb),
        grid=(B // Bb,),
        in_specs=[pl.BlockSpec((Bb * T, PC_W), row2), pl.BlockSpec(st, seq4),
                  pl.BlockSpec((GATE_RANK, D_QK_C), const2), pl.BlockSpec((1, D_QK_C), const2),
                  pl.BlockSpec((1, D_C), const2)],
        out_specs=[pl.BlockSpec((Bb * T, D_C), row2), pl.BlockSpec(st, seq4)],
        out_shape=[_sds((B * T, D_C)), _sds((B, N_HEADS_C, DV_C, DK_C))],
        scratch_shapes=[pltpu.VMEM((D_C, D_QK_C), F32), pltpu.VMEM((Bb * T, D_QK_C), F32)],
        compiler_params=_cparams("parallel"), name="gla",
    )(pc, jnp.swapaxes(s0, 2, 3), wg2, bgate, ng)
    return y, jnp.swapaxes(s_new_t, 2, 3)


def _tiles_to_rows(ref, lead, start, rows, stride):
    return jnp.concatenate([ref[lead + (pl.ds(start + s, rows, stride=stride), slice(None))]
                            for s in range(ROW_TILE)], axis=1)


def _rows_to_tiles(ref, lead, rows, val):
    for s in range(ROW_TILE):
        ref[lead + (pl.ds(s, rows, stride=ROW_TILE), slice(None))] = val[:, s * LANES:(s + 1) * LANES]


def _outproj_kernel(ya_ref, yb_ref, yc_ref, x_ref, w_ref, g_ref, b_ref, *rest, alpha, n_blocks):
    h_ref = rest[-1]

    @pl.when(pl.program_id(0) < n_blocks)
    def _():
        y = (_dot(ya_ref[...].astype(BF16), w_ref[0:D_A, :])
             + _dot(yb_ref[...].astype(BF16), w_ref[D_A:D_A + D_B, :])
             + _dot(yc_ref[...].astype(BF16), w_ref[D_A + D_B:D_MODEL, :]))
        h = _ln(alpha * x_ref[...] + y, g_ref[...], b_ref[...])
        _rows_to_tiles(h_ref, (), h.shape[0], h)

    @pl.when(pl.program_id(0) >= n_blocks)
    def _():
        h_ref[...] = jnp.zeros(h_ref.shape, F32)


def _outproj(ya, yb, yc, x, row0, w, g, b, alpha, tm, out_rows, out_row0=0, into=None):
    n = ya.shape[0]
    nb = n // tm
    assert n % tm == 0 and row0 % tm == 0 and out_row0 % tm == 0 and out_rows % tm == 0
    off, out_off = row0 // tm, out_row0 // tm
    grid = nb if into is not None else out_rows // tm
    row = lambda i: (jnp.minimum(i, nb - 1), 0)
    const = lambda i: (0, 0)
    in_specs = [pl.BlockSpec((tm, D_A), row), pl.BlockSpec((tm, D_B), row), pl.BlockSpec((tm, D_C), row),
                pl.BlockSpec((tm, D_MODEL), lambda i: (jnp.minimum(i, nb - 1) + off, 0)),
                pl.BlockSpec((D_MODEL, D_MODEL), const), pl.BlockSpec((1, D_MODEL), const),
                pl.BlockSpec((1, D_MODEL), const)]
    args = [ya, yb, yc, x, w, g, b]
    aliases = {}
    if into is not None:
        in_specs.append(pl.BlockSpec(memory_space=pl.ANY))
        args.append(into)
        aliases = {len(args) - 1: 0}
    return pl.pallas_call(
        functools.partial(_outproj_kernel, alpha=alpha, n_blocks=nb),
        grid=(grid,),
        in_specs=in_specs,
        out_specs=pl.BlockSpec((tm * ROW_TILE, LANES), lambda i: (i + out_off, 0)),
        out_shape=_sds((out_rows * ROW_TILE, LANES)),
        input_output_aliases=aliases,
        compiler_params=_cparams("arbitrary"), name="outproj",
    )(*args)


def _router_kernel(h_ref, wt_ref, b_ref, idx_ref, gate_ref, cnt_ref):
    nt = (((1,), (1,)), ((), ()))
    hh, hl = _split_bf16(_tiles_to_rows(h_ref, (), 0, idx_ref.shape[1], ROW_TILE))
    wh, wl = _split_bf16(wt_ref[...])
    logits = (lax.dot_general(wh, hh, nt, preferred_element_type=F32)
              + lax.dot_general(wh, hl, nt, preferred_element_type=F32)
              + lax.dot_general(wl, hh, nt, preferred_element_type=F32)) + b_ref[...]
    eid = lax.broadcasted_iota(jnp.int32, logits.shape, 0)
    vals = []
    member = jnp.zeros(logits.shape, F32)
    for k in range(TOP_K):
        m = jnp.max(logits, axis=0, keepdims=True)
        sel = jnp.min(jnp.where(logits == m, eid, N_EXPERTS), axis=0, keepdims=True)
        idx_ref[k:k + 1, :] = sel
        vals.append(m)
        hit = eid == sel
        member = jnp.where(hit, 1.0, member)
        logits = jnp.where(hit, -jnp.inf, logits)
    es = [jnp.exp(v - vals[0]) for v in vals]
    tot = es[0] + es[1] + es[2] + es[3]
    for k in range(TOP_K):
        gate_ref[k:k + 1, :] = es[k] / tot
    cnt_ref[...] = jnp.broadcast_to(jnp.sum(member, axis=1, keepdims=True), cnt_ref.shape)


def _router_tile(n):
    return _pick(n, (896, 640, 512, 384, 256, 128))


def _router(h, n, wt, b):
    tm = _router_tile(n)
    nt = n // tm
    return pl.pallas_call(
        _router_kernel,
        grid=(nt,),
        in_specs=[pl.BlockSpec((tm * ROW_TILE, LANES), lambda i: (i, 0)),
                  pl.BlockSpec((N_EXPERTS, D_MODEL), lambda i: (0, 0)),
                  pl.BlockSpec((N_EXPERTS, 1), lambda i: (0, 0))],
        out_specs=[pl.BlockSpec((TOP_K, tm), lambda i: (0, i)), pl.BlockSpec((TOP_K, tm), lambda i: (0, i)),
                   pl.BlockSpec((N_EXPERTS, LANES), lambda i: (0, i))],
        out_shape=[_sds((TOP_K, n), jnp.int32), _sds((TOP_K, n)), _sds((N_EXPERTS, nt * LANES))],
        compiler_params=_cparams("parallel"), name="router",
    )(h, wt, b)


def _dest_kernel(idx_ref, base_ref, dest_ref):
    tm = idx_ref.shape[1]
    eid = lax.broadcasted_iota(jnp.int32, (N_EXPERTS, tm), 0)
    hits = [eid == idx_ref[k:k + 1, :] for k in range(TOP_K)]
    member = jnp.zeros((N_EXPERTS, tm), F32)
    for k in range(TOP_K):
        member = jnp.where(hits[k], 1.0, member)
    earlier = (lax.broadcasted_iota(jnp.int32, (tm, tm), 0) < lax.broadcasted_iota(jnp.int32, (tm, tm), 1))
    rank = _dot(member.astype(BF16), jnp.where(earlier, 1.0, 0.0).astype(BF16))
    pos = base_ref[...] + rank
    for k in range(TOP_K):
        dest_ref[k:k + 1, :] = jnp.sum(jnp.where(hits[k], pos, 0.0), axis=0, keepdims=True).astype(jnp.int32)


def _dest(top_idx, base):
    n = top_idx.shape[1]
    tm = _router_tile(n)
    return pl.pallas_call(
        _dest_kernel,
        grid=(n // tm,),
        in_specs=[pl.BlockSpec((TOP_K, tm), lambda i: (0, i)),
                  pl.BlockSpec((None, N_EXPERTS, 1), lambda i: (i, 0, 0))],
        out_specs=pl.BlockSpec((TOP_K, tm), lambda i: (0, i)),
        out_shape=_sds((TOP_K, n), jnp.int32),
        compiler_params=_cparams("parallel"), name="moe_dest",
    )(top_idx, base)


def _invert_kernel(vend_ref, pend_ref, dest_hbm, inv_ref, buf, sem, *, n, rc, n_rows):
    rows_per_slot = n // LANES
    chunks_per_slot = rows_per_slot // rc

    def chunk(c, carry):
        cp = pltpu.make_async_copy(dest_hbm.at[pl.ds(c * rc, rc), :], buf, sem.at[0])
        cp.start()
        cp.wait()
        k = c // chunks_per_slot
        t0 = (c - k * chunks_per_slot) * (rc * LANES)

        def row(r, carry):
            first = (t0 + r * LANES) * TOP_K + k
            for l in range(LANES):
                inv_ref[buf[r, l]] = first + l * TOP_K
            return carry

        return lax.fori_loop(0, rc, row, carry)

    lax.fori_loop(0, TOP_K * chunks_per_slot, chunk, 0)

    def fill(lo, hi, ctr):
        def body(r, ctr):
            inv_ref[r] = ctr
            return ctr + 1
        return lax.fori_loop(lo, hi, body, ctr)

    ctr = jnp.int32(TOP_K * n)
    for e in range(N_EXPERTS):
        ctr = fill(vend_ref[e], pend_ref[e], ctr)
    fill(pend_ref[N_EXPERTS - 1], n_rows, ctr)


def _invert(valid_end, pad_end, dest, n_rows):
    k, n = dest.shape
    rc = _pick(n // LANES, (19, 16, 8, 5, 4, 2, 1))
    return pl.pallas_call(
        functools.partial(_invert_kernel, n=n, rc=rc, n_rows=n_rows),
        grid_spec=pltpu.PrefetchScalarGridSpec(
            num_scalar_prefetch=2, grid=(1,),
            in_specs=[pl.BlockSpec(memory_space=pl.ANY)],
            out_specs=pl.BlockSpec(memory_space=pltpu.SMEM),
            scratch_shapes=[pltpu.SMEM((rc, LANES), jnp.int32), pltpu.SemaphoreType.DMA((1,))]),
        out_shape=_sds((n_rows,), jnp.int32),
        compiler_params=_cparams("arbitrary"), name="moe_invert",
    )(valid_end, pad_end, dest.reshape(k * n // LANES, LANES))


def _invert_sc(valid_cum, pad_end, dest, n_rows):
    k, n = dest.shape
    ch = next(c for c in range(min(n, 4096) // 16 * 16, 0, -16) if n % c == 0)
    pos = jnp.arange(n_rows, dtype=jnp.int32)
    e_of = jnp.sum((pad_end[None, :] <= pos[:, None]).astype(jnp.int32), axis=1)
    fill = pos + k * n - jnp.concatenate([valid_cum, jnp.full((1,), k * n, jnp.int32)])[e_of]
    mesh = plsc.VectorSubcoreMesh(core_axis_name="c", subcore_axis_name="s", num_cores=SC_CORES,
                                  num_subcores=SC_SUBCORES)

    @functools.partial(
        pl.kernel, mesh=mesh, out_type=_sds((n_rows,), jnp.int32),
        scratch_types=[pltpu.VMEM((n_rows,), jnp.int32), pltpu.VMEM((ch,), jnp.int32)],
        compiler_params=pltpu.CompilerParams(needs_layout_passes=False), name="moe_invert_sc")
    def run(fill_hbm, dest_hbm, inv_hbm, inv_v, d_v):
        first = jnp.logical_and(lax.axis_index("c") == 0, lax.axis_index("s") == 0)

        @pl.when(first)
        def _():
            pltpu.sync_copy(fill_hbm, inv_v)
            lanes = lax.iota(jnp.int32, SC_LANES)
            for slot in range(k):
                def chunk(c, carry):
                    pltpu.sync_copy(dest_hbm.at[pl.ds(slot * n + c * ch, ch)], d_v)

                    def vec(v, carry):
                        idx = d_v[pl.ds(v * SC_LANES, SC_LANES)]
                        plsc.store_scatter(inv_v, [idx], (c * ch + v * SC_LANES + lanes) * k + slot)
                        return carry
                    return lax.fori_loop(0, ch // SC_LANES, vec, carry)
                lax.fori_loop(0, n // ch, chunk, 0)
            pltpu.sync_copy(inv_v, inv_hbm)

    return run(fill, dest.reshape(-1))


def _moe_kernel(be_ref, nu_ref, inv_ref, h_hbm, w1f_ref, b1_ref, w2f_ref, b2_ref, y4_hbm,
                xbuf, xb, obuf, w1_ref, w2_ref, gsem, ssem, *, bm):
    i = pl.program_id(0)
    n_used = nu_ref[0]
    last_blk = pl.num_programs(0) - 1
    n_ff = D_FF // FF_CHUNK

    def tile(r):
        start = r * ROW_TILE
        return pl.ds(start if isinstance(r, int) else pl.multiple_of(start, ROW_TILE), ROW_TILE)

    def gather_row(blk, slot, j, r=None):
        r = inv_ref[blk * bm + j] if r is None else r
        t = lax.shift_right_logical(r, 2)
        pltpu.make_async_copy(h_hbm.at[tile(t), :], xbuf.at[slot, tile(j), :], gsem.at[slot]).start()

    def scatter_row(blk, slot, j, r=None):
        r = inv_ref[blk * bm + j] if r is None else r
        pltpu.make_async_copy(obuf.at[slot, tile(j), :], y4_hbm.at[tile(r), :], ssem.at[slot]).start()

    def wait_gather(slot):
        pltpu.make_async_copy(h_hbm.at[pl.ds(0, bm * ROW_TILE), :], xbuf.at[slot], gsem.at[slot]).wait()

    def wait_scatter(slot):
        pltpu.make_async_copy(obuf.at[slot], y4_hbm.at[pl.ds(0, bm * ROW_TILE), :], ssem.at[slot]).wait()

    def loop_rows(fn, blk, slot):
        def body(j, c):
            fn(blk, slot, j)
            return c
        lax.fori_loop(0, bm, body, 0)

    def issue_rows(fn, blk, slot, j_lo, j_hi):
        for j0 in range(j_lo, j_hi, DMA_GROUP):
            js = range(j0, j0 + DMA_GROUP)
            ids = [inv_ref[blk * bm + j] for j in js]
            for j, r in zip(js, ids):
                fn(blk, slot, j, r)

    def step():
        slot = i % 2

        @pl.when(i + 1 < n_used)
        def _():
            issue_rows(gather_row, i + 1, 1 - slot, 0, bm)
        wait_gather(slot)
        xb[...] = _tiles_to_rows(xbuf, (slot,), 0, bm, ROW_TILE).astype(BF16)
        acc = jnp.zeros((bm, D_MODEL), F32)
        for c in range(n_ff):
            lo = c * FF_CHUNK
            x = xb[...]
            hg = _dot(x, w1_ref[:, lo:lo + FF_CHUNK]) + b1_ref[:, lo:lo + FF_CHUNK]
            hl = _dot(x, w1_ref[:, D_FF + lo:D_FF + lo + FF_CHUNK]) + b1_ref[:, D_FF + lo:D_FF + lo + FF_CHUNK]
            gate = jnp.minimum(hg, SWIGLU_LIMIT)
            lin = jnp.clip(hl, -SWIGLU_LIMIT, SWIGLU_LIMIT)
            act = gate * _sigmoid(SWIGLU_ALPHA * gate) * (lin + 1.0)
            acc = acc + _dot(act.astype(BF16), w2_ref[lo:lo + FF_CHUNK, :])

        @pl.when(i >= 2)
        def _():
            wait_scatter(slot)
        _rows_to_tiles(obuf, (slot,), bm, acc + b2_ref[...])
        issue_rows(scatter_row, i, slot, 0, bm)

    new_expert = jnp.logical_or(i == 0, be_ref[i] != be_ref[jnp.maximum(i - 1, 0)])

    @pl.when(jnp.logical_and(new_expert, i < n_used))
    def _():
        def cast_rows(r, c):
            rows = pl.ds(pl.multiple_of(r * W_CAST_ROWS, W_CAST_ROWS), W_CAST_ROWS)
            w1_ref[rows, :] = w1f_ref[rows, :].astype(BF16)
            w2_ref[rows, :] = w2f_ref[rows, :].astype(BF16)
            return c
        lax.fori_loop(0, D_MODEL // W_CAST_ROWS, cast_rows, 0)

    @pl.when(jnp.logical_and(i == 0, n_used > 0))
    def _():
        loop_rows(gather_row, 0, 0)

    @pl.when(i < n_used)
    def _():
        step()

    @pl.when(i == n_used - 1)
    def _():
        slot = i % 2
        wait_scatter(slot)

        @pl.when(i >= 1)
        def _():
            wait_scatter(1 - slot)
        obuf[0] = jnp.zeros((bm * ROW_TILE, LANES), F32)

        def zero_block(blk, c):
            rows = bm * ROW_TILE
            cp = pltpu.make_async_copy(obuf.at[0], y4_hbm.at[pl.ds(pl.multiple_of(blk * rows, rows), rows), :],
                                       ssem.at[0])
            cp.start()
            cp.wait()
            return c
        lax.fori_loop(n_used, last_blk + 1, zero_block, 0)


def _moe(block_e, n_used, inv, h, layer, w1, b1, w2, b2, bm):
    nblk = block_e.shape[0]
    wmap = lambda i, be, nu, iv: (layer, be[i], 0, 0)
    return pl.pallas_call(
        functools.partial(_moe_kernel, bm=bm),
        grid_spec=pltpu.PrefetchScalarGridSpec(
            num_scalar_prefetch=3, grid=(nblk,),
            in_specs=[pl.BlockSpec(memory_space=pl.ANY),
                      pl.BlockSpec((None, None, D_MODEL, 2 * D_FF), wmap),
                      pl.BlockSpec((None, None, 1, 2 * D_FF), wmap),
                      pl.BlockSpec((None, None, D_FF, D_MODEL), wmap),
                      pl.BlockSpec((None, None, 1, D_MODEL), wmap)],
            out_specs=pl.BlockSpec(memory_space=pl.ANY),
            scratch_shapes=[pltpu.VMEM((2, bm * ROW_TILE, LANES), F32), pltpu.VMEM((bm, D_MODEL), BF16),
                            pltpu.VMEM((2, bm * ROW_TILE, LANES), F32),
                            pltpu.VMEM((D_MODEL, 2 * D_FF), BF16), pltpu.VMEM((D_FF, D_MODEL), BF16),
                            pltpu.SemaphoreType.DMA((2,)), pltpu.SemaphoreType.DMA((2,))]),
        out_shape=_sds((nblk * bm * ROW_TILE, LANES)),
        compiler_params=_cparams("arbitrary"), name="moe_experts",
    )(block_e, n_used, inv, h, w1, b1, w2, b2)


def _combine_kernel(gates_ref, y4_ref, h_ref, g_ref, b_ref, out_ref, acc_scr, *, alpha):
    tm = out_ref.shape[0]
    tok0 = pl.program_id(0) * tm

    def group(gi, carry):
        t0 = pl.multiple_of(gi * SUBLANES, SUBLANES)
        tiles = []
        for u in range(SUBLANES):
            t = t0 + u
            tile = alpha * h_ref[pl.ds(pl.multiple_of(t * ROW_TILE, ROW_TILE), ROW_TILE), :]
            for k in range(TOP_K):
                r = (t * TOP_K + k) * ROW_TILE
                tile = tile + (gates_ref[(tok0 + t) * TOP_K + k]
                               * y4_ref[pl.ds(pl.multiple_of(r, ROW_TILE), ROW_TILE), :])
            tiles.append(tile.reshape(1, ROW_TILE, LANES))
        rows = jnp.swapaxes(jnp.concatenate(tiles, axis=0), 0, 1)
        for s in range(ROW_TILE):
            acc_scr[pl.ds(t0, SUBLANES), s * LANES:(s + 1) * LANES] = rows[s]
        return carry
    lax.fori_loop(0, tm // SUBLANES, group, 0)
    out_ref[...] = _ln(acc_scr[...], g_ref[...], b_ref[...])


def _combine(y4, gates_flat, h, n, g, b, alpha):
    tm = _pick(n, (224, 128, 64, 32, 16, 8))
    row = lambda i, gt: (i, 0)
    const = lambda i, gt: (0, 0)
    return pl.pallas_call(
        functools.partial(_combine_kernel, alpha=alpha),
        grid_spec=pltpu.PrefetchScalarGridSpec(
            num_scalar_prefetch=1, grid=(n // tm,),
            in_specs=[pl.BlockSpec((TOP_K * tm * ROW_TILE, LANES), row),
                      pl.BlockSpec((tm * ROW_TILE, LANES), row),
                      pl.BlockSpec((1, D_MODEL), const), pl.BlockSpec((1, D_MODEL), const)],
            out_specs=pl.BlockSpec((tm, D_MODEL), row),
            scratch_shapes=[pltpu.VMEM((tm, D_MODEL), F32)]),
        out_shape=_sds((n, D_MODEL)),
        compiler_params=_cparams("arbitrary"), name="moe_combine",
    )(gates_flat, y4, h, g, b)


def _group_layout(cnt_tiles, nblk, bm):
    cnt = cnt_tiles.astype(jnp.int32)
    counts = jnp.sum(cnt, axis=1)
    padded = (counts + bm - 1) // bm * bm
    pad_end = jnp.cumsum(padded)
    pad_start = pad_end - padded
    base = pad_start[:, None] + jnp.cumsum(cnt, axis=1) - cnt
    starts = jnp.arange(nblk, dtype=jnp.int32) * bm
    block_e = jnp.minimum(jnp.sum((pad_end[None, :] <= starts[:, None]).astype(jnp.int32), axis=1),
                          N_EXPERTS - 1).astype(jnp.int32)
    n_used = (pad_end[-1] // bm).astype(jnp.int32).reshape(1)
    valid_end = (pad_start + counts).astype(jnp.int32)
    return base.T.astype(F32)[:, :, None], valid_end, pad_end.astype(jnp.int32), block_e, n_used


def _pack_w_in(w):
    o = 0
    xa, o = w[:, o:o + D_A], o + D_A
    ga, o = w[:, o:o + D_A], o + D_A
    vb, o = w[:, o:o + D_B], o + D_B
    gb, o = w[:, o:o + D_B], o + D_B
    q, o = w[:, o:o + D_QK_C], o + D_QK_C
    k, o = w[:, o:o + D_QK_C], o + D_QK_C
    v, o = w[:, o:o + D_C], o + D_C
    r, o = w[:, o:o + D_C], o + D_C
    z = w[:, o:o + GATE_RANK]
    zq = jnp.zeros((w.shape[0], PC_K - PC_Z - GATE_RANK), w.dtype)
    zk = jnp.zeros((w.shape[0], PC_V - PC_K - D_QK_C), w.dtype)
    return jnp.concatenate([xa, ga, vb, gb, q, z, zq, k, zk, v, r], axis=1).astype(BF16)


def _block_diag(w):
    eye = jnp.eye(N_BLK_A, dtype=w.dtype)
    return jnp.einsum("hij,hg->higj", w, eye).reshape(D_A, D_A)


def kernel(x_prompt, x_sample, state_conv_a, state_rglru, state_conv_b, state_gla, meta_tokens, ln0_g, ln0_b,
           w_in, conv_a_w, conv_a_b, w_rg, b_rg, w_ig, b_ig, lru_lambda, conv_b_w, conv_b_b, ln_b_g, ln_b_b,
           w_gate2, b_gate, gla_norm_g, w_out, ln1_g, ln1_b, router_w, router_b, moe_w1, moe_b1, moe_w2, moe_b2,
           ln2_g, ln2_b):
    bp, seq, _ = x_prompt.shape
    bs, dseq, _ = x_sample.shape
    depth = w_in.shape[0]
    tp = N_META + seq
    ts = SAMPLE_PAD_T
    np_rows = bp * tp
    ns_rows = bs * dseq
    n = np_rows + ns_rows
    alpha = (2 * depth) ** 0.25
    row = lambda a: a.reshape(1, -1)

    meta = jnp.broadcast_to(meta_tokens[None], (bp, N_META, D_MODEL))
    xp_in = jnp.concatenate([meta, x_prompt], axis=1).reshape(np_rows, D_MODEL)
    xs_in = jnp.pad(x_sample, ((0, 0), (0, ts - dseq), (0, 0))).reshape(bs * ts, D_MODEL)
    zeros_p = (jnp.zeros((bp, CONV_A - 1, D_A), F32), jnp.zeros((bp, 1, D_A), F32),
               jnp.zeros((bp, CONV_B - 1, D_B), F32), jnp.zeros((bp, N_HEADS_C, DK_C, DV_C), F32))
    sb = _pick(bs, (8, 4, 2, 1))
    tm_p = _pick(np_rows, (384, 512, 256, 128, 64, 32, 16, 8))
    tm_s = _pick(math.gcd(ns_rows, np_rows), (128, 64, 32, 16, 8))
    nblk = -(-(TOP_K * n) // MOE_BM) + N_EXPERTS
    while (nblk * MOE_BM // TOP_K) % math.lcm(tm_p, tm_s):
        nblk += 1

    new_p = ([], [], [], [])
    new_s = ([], [], [], [])
    x_all = None
    for l in range(depth):
        w_packed = _pack_w_in(w_in[l])
        wg = jnp.concatenate([_block_diag(w_rg[l]), _block_diag(w_ig[l])], axis=1).astype(BF16)
        bg = jnp.concatenate([b_rg[l], b_ig[l]]).reshape(1, -1)
        w_out_b = w_out[l].astype(BF16)
        ng = row(jnp.tile(gla_norm_g[l], N_HEADS_C))
        first = l == 0
        if first:
            pa_p, pb_p, pc_p, xn_p = _inproj(xp_in, 0, np_rows, row(ln0_g), row(ln0_b), w_packed, True)
            pa_s, pb_s, pc_s, xn_s = _inproj(xs_in, 0, bs * ts, row(ln0_g), row(ln0_b), w_packed, True)
            res_p, res_p_row0 = xn_p, 0
        else:
            pa_p, pb_p, pc_p = _inproj(x_all, 0, np_rows, row(ln0_g), row(ln0_b), w_packed, False)
            xn_s = jnp.pad(x_all[np_rows:].reshape(bs, dseq, D_MODEL),
                           ((0, 0), (0, ts - dseq), (0, 0))).reshape(bs * ts, D_MODEL)
            pa_s, pb_s, pc_s = _inproj(xn_s, 0, bs * ts, row(ln0_g), row(ln0_b), w_packed, False)
            res_p, res_p_row0 = x_all, 0

        outs = []
        for (pa, pb, pc, nb, t, tv, bb, st) in (
                (pa_p, pb_p, pc_p, bp, tp, tp, 1, zeros_p),
                (pa_s, pb_s, pc_s, bs, ts, dseq, sb,
                 (state_conv_a[l], state_rglru[l].reshape(bs, 1, D_A), state_conv_b[l], state_gla[l]))):
            ya, ca_new, h_last = _rglru(pa.reshape(nb, t, PA_W), st[0], st[1], conv_a_w[l], row(conv_a_b[l]),
                                        wg, bg, row(lru_lambda[l]), tv, bb)
            yb, cb_new = _convb(pb.reshape(nb, t, PB_W), st[2], conv_b_w[l], row(conv_b_b[l]),
                                row(ln_b_g[l]), row(ln_b_b[l]), tv, bb)
            yc, s_new = _gla(pc, nb, t, st[3], w_gate2[l], row(b_gate[l]), ng, tv, bb)
            outs.append((ya.reshape(nb * t, D_A), yb.reshape(nb * t, D_B), yc,
                         ca_new, h_last.reshape(nb, D_A), cb_new, s_new))
        (ya_p, yb_p, yc_p, *st_p), (ya_s, yb_s, yc_s, *st_s) = outs
        for j in range(4):
            new_p[j].append(st_p[j])
            new_s[j].append(st_s[j])

        valid = lambda a: a.reshape(bs, ts, -1)[:, :dseq].reshape(ns_rows, -1)
        h_all = _outproj(ya_p, yb_p, yc_p, res_p, res_p_row0, w_out_b, row(ln1_g[l]), row(ln1_b[l]), alpha,
                         tm_p, nblk * MOE_BM // TOP_K)
        h_all = _outproj(valid(ya_s), valid(yb_s), valid(yc_s), valid(xn_s), 0, w_out_b, row(ln1_g[l]),
                         row(ln1_b[l]), alpha, tm_s, nblk * MOE_BM // TOP_K, out_row0=np_rows, into=h_all)

        top_idx, gates, cnt = _router(h_all, n, router_w[l].T, router_b[l].reshape(-1, 1))
        base, valid_end, pad_end, block_e, n_used = _group_layout(cnt[:, ::LANES], nblk, MOE_BM)
        dest = _dest(top_idx, base)
        inv = _invert_sc(jnp.cumsum(jnp.sum(cnt[:, ::LANES].astype(jnp.int32), axis=1)), pad_end, dest,
                         nblk * MOE_BM)
        y4 = _moe(block_e, n_used, inv, h_all, l, moe_w1, moe_b1[:, :, None, :], moe_w2, moe_b2[:, :, None, :],
                  MOE_BM)
        x_all = _combine(y4, gates.T.reshape(-1), h_all, n, row(ln2_g[l]), row(ln2_b[l]), alpha)

    y_p = x_all[:np_rows].reshape(bp, tp, D_MODEL)[:, N_META:]
    y_s = x_all[np_rows:].reshape(bs, dseq, D_MODEL)
    return (y_p, y_s,
            jnp.stack(new_p[0]), jnp.stack(new_p[1]), jnp.stack(new_p[2]), jnp.stack(new_p[3]),
            jnp.stack(new_s[0]), jnp.stack(new_s[1]), jnp.stack(new_s[2]), jnp.stack(new_s[3]))
```

```python
import functools
import math

import jax
import jax.numpy as jnp
from jax import lax
from jax.experimental import pallas as pl
from jax.experimental.pallas import tpu as pltpu
from jax.experimental.pallas import tpu_sc as plsc

F32 = jnp.float32
BF16 = jnp.bfloat16

D_MODEL = 1024
N_META = 16
D_A = 384
D_B = 256
D_C = 384
N_BLK_A = 8
BLK_A = D_A // N_BLK_A
CONV_A = 4
RG_C = 8.0
CONV_B = 31
N_HEADS_C = 4
DV_C = D_C // N_HEADS_C
DK_C = DV_C // 2
D_QK_C = N_HEADS_C * DK_C
GATE_RANK = 16
GATE_TAU = 16.0
N_EXPERTS = 32
TOP_K = 4
D_FF = D_MODEL
SWIGLU_LIMIT = 7.0
SWIGLU_ALPHA = 1.702
LN_EPS = 1e-5

PA_W = 2 * D_A
PB_W = 2 * D_B
PC_Q, PC_Z, PC_K, PC_V, PC_R, PC_W = 0, 192, 256, 512, 896, 1280
P_W = PA_W + PB_W + PC_W

SUBLANES = 8
LANES = 128
SC_LANES = 16
SC_CORES, SC_SUBCORES = 2, 16
ROW_TILE = D_MODEL // LANES
VMEM_LIMIT_BYTES = 56 * 1024 * 1024
MOE_BM = 256
FF_CHUNK = 256
DMA_GROUP = 8
W_CAST_ROWS = 128
SAMPLE_PAD_T = 8


def _cparams(*sem):
    return pltpu.CompilerParams(dimension_semantics=sem, vmem_limit_bytes=VMEM_LIMIT_BYTES)


def _sds(shape, dtype=F32):
    return jax.ShapeDtypeStruct(shape, dtype)


def _pick(n, prefs):
    for p in prefs:
        if n % p == 0:
            return p
    raise ValueError(f"no tile for {n} in {prefs}")


def _ln(x, g, b):
    mu = jnp.mean(x, axis=-1, keepdims=True)
    xc = x - mu
    var = jnp.mean(xc * xc, axis=-1, keepdims=True)
    return xc * lax.rsqrt(var + LN_EPS) * g + b


def _sigmoid(x):
    return 1.0 / (1.0 + jnp.exp(-x))


def _split_bf16(x):
    hi = x.astype(BF16)
    lo = (x - hi.astype(F32)).astype(BF16)
    return hi, lo


def _dot(a, b):
    return jnp.dot(a, b, preferred_element_type=F32)


def _inproj_kernel(x_ref, g_ref, b_ref, w_ref, pa_ref, pb_ref, pc_ref, *maybe_xn, apply_ln):
    x = x_ref[...]
    if apply_ln:
        x = _ln(x, g_ref[...], b_ref[...])
        maybe_xn[0][...] = x
    xb = x.astype(BF16)
    pa_ref[...] = _dot(xb, w_ref[:, 0:PA_W])
    pb_ref[...] = _dot(xb, w_ref[:, PA_W:PA_W + PB_W])
    pc_ref[...] = _dot(xb, w_ref[:, PA_W + PB_W:P_W])


def _inproj(x, row0, nrows, ln_g, ln_b, w_packed, apply_ln):
    tm = _pick(nrows, (384, 512, 256, 128, 64, 32, 16, 8))
    while row0 % tm:
        tm //= 2
    off = row0 // tm
    const = lambda i: (0, 0)
    row = lambda i: (i, 0)
    out_shape = [_sds((nrows, PA_W)), _sds((nrows, PB_W)), _sds((nrows, PC_W))]
    out_specs = [pl.BlockSpec((tm, PA_W), row), pl.BlockSpec((tm, PB_W), row), pl.BlockSpec((tm, PC_W), row)]
    if apply_ln:
        out_shape.append(_sds((nrows, D_MODEL)))
        out_specs.append(pl.BlockSpec((tm, D_MODEL), row))
    return pl.pallas_call(
        functools.partial(_inproj_kernel, apply_ln=apply_ln),
        grid=(nrows // tm,),
        in_specs=[pl.BlockSpec((tm, D_MODEL), lambda i: (i + off, 0)),
                  pl.BlockSpec((1, D_MODEL), const), pl.BlockSpec((1, D_MODEL), const),
                  pl.BlockSpec((D_MODEL, P_W), const)],
        out_specs=out_specs, out_shape=out_shape,
        compiler_params=_cparams("parallel"), name="inproj",
    )(x, ln_g, ln_b, w_packed)


def _rglru_kernel(p_ref, cbuf_ref, h0_ref, cw_ref, cb_ref, wg_ref, bg_ref, lam_ref,
                  y_ref, cnew_ref, hlast_ref, xp_scr, a_scr, h_scr, *, T, Tc, Tv, Bb):
    lam = lam_ref[...]
    softplus_neg = jnp.maximum(-lam, 0.0) + jnp.log1p(jnp.exp(-jnp.abs(lam)))
    c_decay = -RG_C * softplus_neg
    cw = cw_ref[...]
    cb = cb_ref[...]
    bg = bg_ref[...]
    sub = lax.broadcasted_iota(jnp.int32, (Tc, D_A), 0) % SUBLANES
    halo = SUBLANES - (CONV_A - 1)
    for b in range(Bb):
        xp_scr[halo:SUBLANES, :] = cbuf_ref[b]
        xp_scr[SUBLANES:SUBLANES + T, :] = p_ref[b, :, 0:D_A]
        cnew_ref[b] = xp_scr[halo + Tv:SUBLANES + Tv, :]

        def chunk(ci, h_b):
            r0 = pl.multiple_of(ci * Tc, SUBLANES)
            win = xp_scr[pl.ds(r0, Tc + SUBLANES), :]
            xc = cb + cw[0:1] * win[halo:halo + Tc]
            for j in range(1, CONV_A):
                xc = xc + cw[j:j + 1] * win[halo + j:halo + j + Tc]
            gates = _dot(xc.astype(BF16), wg_ref[...]) + bg
            r = _sigmoid(gates[:, 0:D_A])
            i = _sigmoid(gates[:, D_A:2 * D_A])
            log_a = c_decay * r
            a = jnp.exp(log_a)
            u = jnp.sqrt(1.0 - a * a) * (i * xc)
            for s in (1, 2, 4):
                keep = sub >= s
                a_prev = pltpu.roll(a, s, 0)
                u_prev = pltpu.roll(u, s, 0)
                u = jnp.where(keep, a * u_prev + u, u)
                a = jnp.where(keep, a * a_prev, a)
            a_scr[...] = a
            h_scr[pl.ds(r0, Tc), :] = u

            def group(gi, h_b):
                c0 = pl.multiple_of(gi * SUBLANES, SUBLANES)
                g0 = pl.multiple_of(r0 + gi * SUBLANES, SUBLANES)
                h8 = a_scr[pl.ds(c0, SUBLANES), :] * h_b + h_scr[pl.ds(g0, SUBLANES), :]
                h_scr[pl.ds(g0, SUBLANES), :] = h8
                return jnp.broadcast_to(h8[SUBLANES - 1:SUBLANES, :], (SUBLANES, D_A))

            h_b = lax.fori_loop(0, Tc // SUBLANES, group, h_b)
            ga = p_ref[b, pl.ds(r0, Tc), D_A:2 * D_A]
            gelu = 0.5 * ga * (1.0 + jnp.tanh(0.7978845608028654 * (ga + 0.044715 * ga * ga * ga)))
            y_ref[b, pl.ds(r0, Tc), :] = h_scr[pl.ds(r0, Tc), :] * gelu
            return h_b

        h_b = jnp.broadcast_to(h0_ref[b], (SUBLANES, D_A))
        lax.fori_loop(0, T // Tc, chunk, h_b)
        hlast_ref[b] = h_scr[Tv - 1:Tv, :]


def _rglru(pa3, cbuf, h0, cw, cb, wg, bg, lam, Tv, Bb):
    B, T, _ = pa3.shape
    Tc = _pick(T, (344, 256, 128, 64, 48, 32, 16, 8))
    const2 = lambda i: (0, 0)
    seq3 = lambda i: (i, 0, 0)
    return pl.pallas_call(
        functools.partial(_rglru_kernel, T=T, Tc=Tc, Tv=Tv, Bb=Bb),
        grid=(B // Bb,),
        in_specs=[pl.BlockSpec((Bb, T, PA_W), seq3), pl.BlockSpec((Bb, CONV_A - 1, D_A), seq3),
                  pl.BlockSpec((Bb, 1, D_A), seq3), pl.BlockSpec((CONV_A, D_A), const2),
                  pl.BlockSpec((1, D_A), const2), pl.BlockSpec((D_A, 2 * D_A), const2),
                  pl.BlockSpec((1, 2 * D_A), const2), pl.BlockSpec((1, D_A), const2)],
        out_specs=[pl.BlockSpec((Bb, T, D_A), seq3), pl.BlockSpec((Bb, CONV_A - 1, D_A), seq3),
                   pl.BlockSpec((Bb, 1, D_A), seq3)],
        out_shape=[_sds((B, T, D_A)), _sds((B, CONV_A - 1, D_A)), _sds((B, 1, D_A))],
        scratch_shapes=[pltpu.VMEM((T + 2 * SUBLANES, D_A), F32), pltpu.VMEM((Tc, D_A), F32),
                        pltpu.VMEM((T, D_A), F32)],
        compiler_params=_cparams("parallel"), name="rglru",
    )(pa3, cbuf, h0, cw, cb, wg, bg, lam)


B_HALO = 32


def _convb_kernel(p_ref, buf_ref, w_ref, cb_ref, g_ref, b_ref, y_ref, bnew_ref, u_scr, *, T, Tc, Tv, Bb):
    w = w_ref[...]
    cb = cb_ref[...]
    g = g_ref[...]
    bb = b_ref[...]
    first = B_HALO - (CONV_B - 1)
    for b in range(Bb):
        u_scr[0:first, :] = jnp.zeros((first, D_B), F32)
        u_scr[first:B_HALO, :] = buf_ref[b]
        u_scr[B_HALO:B_HALO + T, :] = p_ref[b, :, 0:D_B] * _sigmoid(p_ref[b, :, D_B:2 * D_B])
        bnew_ref[b] = u_scr[first + Tv:B_HALO + Tv, :]

        def chunk(ci, carry):
            r0 = pl.multiple_of(ci * Tc, SUBLANES)
            win = u_scr[pl.ds(r0, Tc + B_HALO), :]
            shifted = [win] + [pltpu.roll(win, Tc + B_HALO - s, 0) for s in range(1, SUBLANES)]
            acc = cb
            for j in range(CONV_B):
                a, s = divmod(first + j, SUBLANES)
                acc = acc + w[j:j + 1] * shifted[s][a * SUBLANES:a * SUBLANES + Tc]
            yn = _ln(acc, g, bb)
            y_ref[b, pl.ds(r0, Tc), :] = yn * _sigmoid(yn)
            return carry

        lax.fori_loop(0, T // Tc, chunk, 0)


def _convb(pb3, buf, w, cb, g, b, Tv, Bb):
    B, T, _ = pb3.shape
    Tc = _pick(T, (48, 32, 16, 8))
    const2 = lambda i: (0, 0)
    seq3 = lambda i: (i, 0, 0)
    return pl.pallas_call(
        functools.partial(_convb_kernel, T=T, Tc=Tc, Tv=Tv, Bb=Bb),
        grid=(B // Bb,),
        in_specs=[pl.BlockSpec((Bb, T, PB_W), seq3), pl.BlockSpec((Bb, CONV_B - 1, D_B), seq3),
                  pl.BlockSpec((CONV_B, D_B), const2), pl.BlockSpec((1, D_B), const2),
                  pl.BlockSpec((1, D_B), const2), pl.BlockSpec((1, D_B), const2)],
        out_specs=[pl.BlockSpec((Bb, T, D_B), seq3), pl.BlockSpec((Bb, CONV_B - 1, D_B), seq3)],
        out_shape=[_sds((B, T, D_B)), _sds((B, CONV_B - 1, D_B))],
        scratch_shapes=[pltpu.VMEM((T + B_HALO, D_B), F32)],
        compiler_params=_cparams("parallel"), name="convb",
    )(pb3, buf, w, cb, g, b)


def _gla_kernel(p_ref, s0_ref, wg2_ref, bgate_ref, ng_ref, y_ref, snew_ref, s_scr, g_scr, *, T, C, Tb, Tv, Bb):
    ri = lax.broadcasted_iota(jnp.int32, (C, C), 0)
    ci_ = lax.broadcasted_iota(jnp.int32, (C, C), 1)
    tril = ri >= ci_
    lane_k = lax.broadcasted_iota(jnp.int32, (1, D_QK_C), 1)
    lane_v = lax.broadcasted_iota(jnp.int32, (1, D_C), 1)
    hm_k = [(lane_k >= h * DK_C) & (lane_k < (h + 1) * DK_C) for h in range(N_HEADS_C)]
    hm_v = [(lane_v >= h * DV_C) & (lane_v < (h + 1) * DV_C) for h in range(N_HEADS_C)]
    rs = lax.broadcasted_iota(jnp.int32, (D_C, D_QK_C), 0)
    cs = lax.broadcasted_iota(jnp.int32, (D_C, D_QK_C), 1)
    bd_t = (rs >= 0) & (rs < 0)
    for h in range(N_HEADS_C):
        bd_t = bd_t | ((rs >= h * DV_C) & (rs < (h + 1) * DV_C) & (cs >= h * DK_C) & (cs < (h + 1) * DK_C))
    rm = lax.broadcasted_iota(jnp.int32, (D_C, D_C), 0)
    cm = lax.broadcasted_iota(jnp.int32, (D_C, D_C), 1)
    seg = (rm >= 0) & (rm < 0)
    for h in range(N_HEADS_C):
        seg = seg | ((rm >= h * DV_C) & (rm < (h + 1) * DV_C) & (cm >= h * DV_C) & (cm < (h + 1) * DV_C))
    mseg = jnp.where(seg, 1.0, 0.0).astype(BF16)
    wg2 = wg2_ref[...].astype(BF16)
    bgate = bgate_ref[...]
    ng = ng_ref[...]
    rowi = lax.broadcasted_iota(jnp.int32, (C, 1), 0)
    tdims = (((0,), (0,)), ((), ()))

    n_chunks = T // C
    nt_dims = (((1,), (1,)), ((), ()))
    tril4 = jnp.concatenate([tril] * N_HEADS_C, axis=0)
    scan_shifts = [s for s in (1, 2, 4, 8, 16, 32) if s < C]
    scan_keep = [rowi >= s for s in scan_shifts]

    def tile_rows(ti, tb, base=0):
        r0 = ti * tb
        return pl.ds(base + (r0 if isinstance(ti, int) else pl.multiple_of(r0, tb)), tb)

    def gates(ti, carry):
        rows = tile_rows(ti, Tb)
        z = p_ref[rows, PC_Z:PC_Z + GATE_RANK]
        pre = _dot(z.astype(BF16), wg2) + bgate
        g = (jnp.minimum(pre, 0.0) - jnp.log1p(jnp.exp(-jnp.abs(pre)))) * (1.0 / GATE_TAU)
        rid = ti * Tb + lax.broadcasted_iota(jnp.int32, (Tb, 1), 0)
        if Bb > 1:
            rid = rid & (T - 1)
        g_scr[rows, :] = jnp.where(rid < Tv, g, 0.0)
        return carry
    lax.fori_loop(0, Bb * T // Tb, gates, 0)

    for b in range(Bb):
        def chunk(ci, carry):
            rows = tile_rows(ci, C, b * T)
            q = p_ref[rows, PC_Q:PC_Q + D_QK_C] * (DK_C ** -0.5)
            k = p_ref[rows, PC_K:PC_K + D_QK_C]
            v = p_ref[rows, PC_V:PC_V + D_C]
            k = jnp.where((ci * C + rowi) < Tv, k, 0.0)
            gcum = g_scr[rows, :]
            for s, keep in zip(scan_shifts, scan_keep):
                gcum = gcum + jnp.where(keep, pltpu.roll(gcum, s, 0), 0.0)
            g_last = gcum[C - 1:C, :]
            g_mid = gcum[C // 2 - 1:C // 2, :]
            vb = v.astype(BF16)
            qt = q * jnp.exp(gcum - g_mid)
            ktb = (k * jnp.exp(g_mid - gcum)).astype(BF16)
            q4 = jnp.concatenate([jnp.where(hm_k[h], qt, 0.0) for h in range(N_HEADS_C)], axis=0).astype(BF16)
            sc = lax.dot_general(q4, ktb, nt_dims, preferred_element_type=F32)
            r4 = _dot(jnp.where(tril4, sc, 0.0).astype(BF16), vb)
            o = jnp.where(hm_v[0], r4[0:C], 0.0)
            for h in range(1, N_HEADS_C):
                o = o + jnp.where(hm_v[h], r4[h * C:(h + 1) * C], 0.0)
            kd = (k * jnp.exp(g_last - gcum)).astype(BF16)
            upd_t = lax.dot_general(vb, kd, tdims, preferred_element_type=F32)
            s_in = s_scr[...]
            o = o + lax.dot_general((q * jnp.exp(gcum)).astype(BF16), s_in.astype(BF16), nt_dims,
                                    preferred_element_type=F32)
            s_scr[...] = s_in * jnp.exp(g_last) + jnp.where(bd_t, upd_t, 0.0)
            y_ref[rows, :] = o
            return carry

        s_scr[...] = jnp.zeros((D_C, D_QK_C), F32)
        for h in range(N_HEADS_C):
            s_scr[h * DV_C:(h + 1) * DV_C, h * DK_C:(h + 1) * DK_C] = s0_ref[b, h]
        lax.fori_loop(0, n_chunks, chunk, 0)
        for h in range(N_HEADS_C):
            snew_ref[b, h] = s_scr[h * DV_C:(h + 1) * DV_C, h * DK_C:(h + 1) * DK_C]

    def finish(ti, carry):
        rows = tile_rows(ti, Tb)
        o = y_ref[rows, :]
        rg = p_ref[rows, PC_R:PC_R + D_C]
        o2_hi, o2_lo = _split_bf16(o * o)
        ms = (_dot(o2_hi, mseg) + _dot(o2_lo, mseg)) * (1.0 / DV_C)
        y_ref[rows, :] = o * lax.rsqrt(ms + LN_EPS) * ng * (rg * _sigmoid(rg))
        return carry
    lax.fori_loop(0, Bb * T // Tb, finish, 0)


def _gla(pc, B, T, s0, wg2, bgate, ng, Tv, Bb):
    assert Bb == 1 or T & (T - 1) == 0
    C = _pick(T, (48, 32, 16, 8))
    Tb = _pick(Bb * T, (344, 256, 128, 64, 48, 32, 16, 8))
    const2 = lambda i: (0, 0)
    row2 = lambda i: (i, 0)
    seq4 = lambda i: (i, 0, 0, 0)
    st = (Bb, N_HEADS_C, DV_C, DK_C)
    y, s_new_t = pl.pallas_call(
        functools.partial(_gla_kernel, T=T, C=C, Tb=Tb, Tv=Tv, Bb=Bb),
        grid=(B // Bb,),
        in_specs=[pl.BlockSpec((Bb * T, PC_W), row2), pl.BlockSpec(st, seq4),
                  pl.BlockSpec((GATE_RANK, D_QK_C), const2), pl.BlockSpec((1, D_QK_C), const2),
                  pl.BlockSpec((1, D_C), const2)],
        out_specs=[pl.BlockSpec((Bb * T, D_C), row2), pl.BlockSpec(st, seq4)],
        out_shape=[_sds((B * T, D_C)), _sds((B, N_HEADS_C, DV_C, DK_C))],
        scratch_shapes=[pltpu.VMEM((D_C, D_QK_C), F32), pltpu.VMEM((Bb * T, D_QK_C), F32)],
        compiler_params=_cparams("parallel"), name="gla",
    )(pc, jnp.swapaxes(s0, 2, 3), wg2, bgate, ng)
    return y, jnp.swapaxes(s_new_t, 2, 3)


def _tiles_to_rows(ref, lead, start, rows, stride):
    return jnp.concatenate([ref[lead + (pl.ds(start + s, rows, stride=stride), slice(None))]
                            for s in range(ROW_TILE)], axis=1)


def _rows_to_tiles(ref, lead, rows, val):
    for s in range(ROW_TILE):
        ref[lead + (pl.ds(s, rows, stride=ROW_TILE), slice(None))] = val[:, s * LANES:(s + 1) * LANES]


def _outproj_kernel(ya_ref, yb_ref, yc_ref, x_ref, w_ref, g_ref, b_ref, *rest, alpha, n_blocks):
    h_ref = rest[-1]

    @pl.when(pl.program_id(0) < n_blocks)
    def _():
        y = (_dot(ya_ref[...].astype(BF16), w_ref[0:D_A, :])
             + _dot(yb_ref[...].astype(BF16), w_ref[D_A:D_A + D_B, :])
             + _dot(yc_ref[...].astype(BF16), w_ref[D_A + D_B:D_MODEL, :]))
        h = _ln(alpha * x_ref[...] + y, g_ref[...], b_ref[...])
        _rows_to_tiles(h_ref, (), h.shape[0], h)

    @pl.when(pl.program_id(0) >= n_blocks)
    def _():
        h_ref[...] = jnp.zeros(h_ref.shape, F32)


def _outproj(ya, yb, yc, x, row0, w, g, b, alpha, tm, out_rows, out_row0=0, into=None):
    n = ya.shape[0]
    nb = n // tm
    assert n % tm == 0 and row0 % tm == 0 and out_row0 % tm == 0 and out_rows % tm == 0
    off, out_off = row0 // tm, out_row0 // tm
    grid = nb if into is not None else out_rows // tm
    row = lambda i: (jnp.minimum(i, nb - 1), 0)
    const = lambda i: (0, 0)
    in_specs = [pl.BlockSpec((tm, D_A), row), pl.BlockSpec((tm, D_B), row), pl.BlockSpec((tm, D_C), row),
                pl.BlockSpec((tm, D_MODEL), lambda i: (jnp.minimum(i, nb - 1) + off, 0)),
                pl.BlockSpec((D_MODEL, D_MODEL), const), pl.BlockSpec((1, D_MODEL), const),
                pl.BlockSpec((1, D_MODEL), const)]
    args = [ya, yb, yc, x, w, g, b]
    aliases = {}
    if into is not None:
        in_specs.append(pl.BlockSpec(memory_space=pl.ANY))
        args.append(into)
        aliases = {len(args) - 1: 0}
    return pl.pallas_call(
        functools.partial(_outproj_kernel, alpha=alpha, n_blocks=nb),
        grid=(grid,),
        in_specs=in_specs,
        out_specs=pl.BlockSpec((tm * ROW_TILE, LANES), lambda i: (i + out_off, 0)),
        out_shape=_sds((out_rows * ROW_TILE, LANES)),
        input_output_aliases=aliases,
        compiler_params=_cparams("arbitrary"), name="outproj",
    )(*args)


def _router_kernel(h_ref, wt_ref, b_ref, idx_ref, gate_ref, cnt_ref):
    nt = (((1,), (1,)), ((), ()))
    hh, hl = _split_bf16(_tiles_to_rows(h_ref, (), 0, idx_ref.shape[1], ROW_TILE))
    wh, wl = _split_bf16(wt_ref[...])
    logits = (lax.dot_general(wh, hh, nt, preferred_element_type=F32)
              + lax.dot_general(wh, hl, nt, preferred_element_type=F32)
              + lax.dot_general(wl, hh, nt, preferred_element_type=F32)) + b_ref[...]
    eid = lax.broadcasted_iota(jnp.int32, logits.shape, 0)
    vals = []
    member = jnp.zeros(logits.shape, F32)
    for k in range(TOP_K):
        m = jnp.max(logits, axis=0, keepdims=True)
        sel = jnp.min(jnp.where(logits == m, eid, N_EXPERTS), axis=0, keepdims=True)
        idx_ref[k:k + 1, :] = sel
        vals.append(m)
        hit = eid == sel
        member = jnp.where(hit, 1.0, member)
        logits = jnp.where(hit, -jnp.inf, logits)
    es = [jnp.exp(v - vals[0]) for v in vals]
    tot = es[0] + es[1] + es[2] + es[3]
    for k in range(TOP_K):
        gate_ref[k:k + 1, :] = es[k] / tot
    cnt_ref[...] = jnp.broadcast_to(jnp.sum(member, axis=1, keepdims=True), cnt_ref.shape)


def _router_tile(n):
    return _pick(n, (896, 640, 512, 384, 256, 128))


def _router(h, n, wt, b):
    tm = _router_tile(n)
    nt = n // tm
    return pl.pallas_call(
        _router_kernel,
        grid=(nt,),
        in_specs=[pl.BlockSpec((tm * ROW_TILE, LANES), lambda i: (i, 0)),
                  pl.BlockSpec((N_EXPERTS, D_MODEL), lambda i: (0, 0)),
                  pl.BlockSpec((N_EXPERTS, 1), lambda i: (0, 0))],
        out_specs=[pl.BlockSpec((TOP_K, tm), lambda i: (0, i)), pl.BlockSpec((TOP_K, tm), lambda i: (0, i)),
                   pl.BlockSpec((N_EXPERTS, LANES), lambda i: (0, i))],
        out_shape=[_sds((TOP_K, n), jnp.int32), _sds((TOP_K, n)), _sds((N_EXPERTS, nt * LANES))],
        compiler_params=_cparams("parallel"), name="router",
    )(h, wt, b)


def _dest_kernel(idx_ref, base_ref, dest_ref):
    tm = idx_ref.shape[1]
    eid = lax.broadcasted_iota(jnp.int32, (N_EXPERTS, tm), 0)
    hits = [eid == idx_ref[k:k + 1, :] for k in range(TOP_K)]
    member = jnp.zeros((N_EXPERTS, tm), F32)
    for k in range(TOP_K):
        member = jnp.where(hits[k], 1.0, member)
    earlier = (lax.broadcasted_iota(jnp.int32, (tm, tm), 0) < lax.broadcasted_iota(jnp.int32, (tm, tm), 1))
    rank = _dot(member.astype(BF16), jnp.where(earlier, 1.0, 0.0).astype(BF16))
    pos = base_ref[...] + rank
    for k in range(TOP_K):
        dest_ref[k:k + 1, :] = jnp.sum(jnp.where(hits[k], pos, 0.0), axis=0, keepdims=True).astype(jnp.int32)


def _dest(top_idx, base):
    n = top_idx.shape[1]
    tm = _router_tile(n)
    return pl.pallas_call(
        _dest_kernel,
        grid=(n // tm,),
        in_specs=[pl.BlockSpec((TOP_K, tm), lambda i: (0, i)),
                  pl.BlockSpec((None, N_EXPERTS, 1), lambda i: (i, 0, 0))],
        out_specs=pl.BlockSpec((TOP_K, tm), lambda i: (0, i)),
        out_shape=_sds((TOP_K, n), jnp.int32),
        compiler_params=_cparams("parallel"), name="moe_dest",
    )(top_idx, base)


def _invert(pad_shift, dest, bm):
    k, n = dest.shape
    n_rows = pad_shift.shape[0] * bm
    ch = next(c for c in range(min(n, 4096) // SC_LANES * SC_LANES, 0, -SC_LANES) if n % c == 0)
    fill = (jnp.arange(n_rows, dtype=jnp.int32).reshape(-1, bm) + pad_shift[:, None]).reshape(-1)
    mesh = plsc.VectorSubcoreMesh(core_axis_name="c", subcore_axis_name="s", num_cores=SC_CORES,
                                  num_subcores=SC_SUBCORES)

    @functools.partial(
        pl.kernel, mesh=mesh, out_type=_sds((n_rows,), jnp.int32),
        scratch_types=[pltpu.VMEM((n_rows,), jnp.int32), pltpu.VMEM((ch,), jnp.int32)],
        compiler_params=pltpu.CompilerParams(needs_layout_passes=False), name="moe_invert_sc")
    def run(fill_hbm, dest_hbm, inv_hbm, inv_v, d_v):
        first = jnp.logical_and(lax.axis_index("c") == 0, lax.axis_index("s") == 0)

        @pl.when(first)
        def _():
            pltpu.sync_copy(fill_hbm, inv_v)
            lanes = lax.iota(jnp.int32, SC_LANES)
            for slot in range(k):
                def chunk(c, carry):
                    pltpu.sync_copy(dest_hbm.at[pl.ds(slot * n + c * ch, ch)], d_v)

                    def vec(v, carry):
                        idx = d_v[pl.ds(v * SC_LANES, SC_LANES)]
                        plsc.store_scatter(inv_v, [idx], (c * ch + v * SC_LANES + lanes) * k + slot)
                        return carry
                    return lax.fori_loop(0, ch // SC_LANES, vec, carry)
                lax.fori_loop(0, n // ch, chunk, 0)
            pltpu.sync_copy(inv_v, inv_hbm)

    return run(fill, dest.reshape(-1))


def _moe_kernel(be_ref, nu_ref, inv_ref, h_hbm, w1f_ref, b1_ref, w2f_ref, b2_ref, y4_hbm,
                xbuf, xb, obuf, w1_ref, w2_ref, gsem, ssem, *, bm):
    i = pl.program_id(0)
    n_used = nu_ref[0]
    last_blk = pl.num_programs(0) - 1
    n_ff = D_FF // FF_CHUNK

    def tile(r):
        start = r * ROW_TILE
        return pl.ds(start if isinstance(r, int) else pl.multiple_of(start, ROW_TILE), ROW_TILE)

    def gather_row(blk, slot, j, r=None):
        r = inv_ref[blk * bm + j] if r is None else r
        t = lax.shift_right_logical(r, 2)
        pltpu.make_async_copy(h_hbm.at[tile(t), :], xbuf.at[slot, tile(j), :], gsem.at[slot]).start()

    def scatter_row(blk, slot, j, r=None):
        r = inv_ref[blk * bm + j] if r is None else r
        pltpu.make_async_copy(obuf.at[slot, tile(j), :], y4_hbm.at[tile(r), :], ssem.at[slot]).start()

    def wait_gather(slot):
        pltpu.make_async_copy(h_hbm.at[pl.ds(0, bm * ROW_TILE), :], xbuf.at[slot], gsem.at[slot]).wait()

    def wait_scatter(slot):
        pltpu.make_async_copy(obuf.at[slot], y4_hbm.at[pl.ds(0, bm * ROW_TILE), :], ssem.at[slot]).wait()

    def loop_rows(fn, blk, slot):
        def body(j, c):
            fn(blk, slot, j)
            return c
        lax.fori_loop(0, bm, body, 0)

    def issue_rows(fn, blk, slot, j_lo, j_hi):
        for j0 in range(j_lo, j_hi, DMA_GROUP):
            js = range(j0, j0 + DMA_GROUP)
            ids = [inv_ref[blk * bm + j] for j in js]
            for j, r in zip(js, ids):
                fn(blk, slot, j, r)

    def step():
        slot = i % 2

        @pl.when(i + 1 < n_used)
        def _():
            issue_rows(gather_row, i + 1, 1 - slot, 0, bm)
        wait_gather(slot)
        xb[...] = _tiles_to_rows(xbuf, (slot,), 0, bm, ROW_TILE).astype(BF16)
        acc = jnp.zeros((bm, D_MODEL), F32)
        for c in range(n_ff):
            lo = c * FF_CHUNK
            x = xb[...]
            hg = _dot(x, w1_ref[:, lo:lo + FF_CHUNK]) + b1_ref[:, lo:lo + FF_CHUNK]
            hl = _dot(x, w1_ref[:, D_FF + lo:D_FF + lo + FF_CHUNK]) + b1_ref[:, D_FF + lo:D_FF + lo + FF_CHUNK]
            gate = jnp.minimum(hg, SWIGLU_LIMIT)
            lin = jnp.clip(hl, -SWIGLU_LIMIT, SWIGLU_LIMIT)
            act = gate * _sigmoid(SWIGLU_ALPHA * gate) * (lin + 1.0)
            acc = acc + _dot(act.astype(BF16), w2_ref[lo:lo + FF_CHUNK, :])

        @pl.when(i >= 2)
        def _():
            wait_scatter(slot)
        _rows_to_tiles(obuf, (slot,), bm, acc + b2_ref[...])
        issue_rows(scatter_row, i, slot, 0, bm)

    new_expert = jnp.logical_or(i == 0, be_ref[i] != be_ref[jnp.maximum(i - 1, 0)])

    @pl.when(jnp.logical_and(new_expert, i < n_used))
    def _():
        def cast_rows(r, c):
            rows = pl.ds(pl.multiple_of(r * W_CAST_ROWS, W_CAST_ROWS), W_CAST_ROWS)
            w1_ref[rows, :] = w1f_ref[rows, :].astype(BF16)
            w2_ref[rows, :] = w2f_ref[rows, :].astype(BF16)
            return c
        lax.fori_loop(0, D_MODEL // W_CAST_ROWS, cast_rows, 0)

    @pl.when(jnp.logical_and(i == 0, n_used > 0))
    def _():
        loop_rows(gather_row, 0, 0)

    @pl.when(i < n_used)
    def _():
        step()

    @pl.when(i == n_used - 1)
    def _():
        slot = i % 2
        wait_scatter(slot)

        @pl.when(i >= 1)
        def _():
            wait_scatter(1 - slot)
        obuf[0] = jnp.zeros((bm * ROW_TILE, LANES), F32)

        def zero_block(blk, c):
            rows = bm * ROW_TILE
            cp = pltpu.make_async_copy(obuf.at[0], y4_hbm.at[pl.ds(pl.multiple_of(blk * rows, rows), rows), :],
                                       ssem.at[0])
            cp.start()
            cp.wait()
            return c
        lax.fori_loop(n_used, last_blk + 1, zero_block, 0)


def _moe(block_e, n_used, inv, h, layer, w1, b1, w2, b2, bm):
    nblk = block_e.shape[0]
    wmap = lambda i, be, nu, iv: (layer, be[i], 0, 0)
    return pl.pallas_call(
        functools.partial(_moe_kernel, bm=bm),
        grid_spec=pltpu.PrefetchScalarGridSpec(
            num_scalar_prefetch=3, grid=(nblk,),
            in_specs=[pl.BlockSpec(memory_space=pl.ANY),
                      pl.BlockSpec((None, None, D_MODEL, 2 * D_FF), wmap),
                      pl.BlockSpec((None, None, 1, 2 * D_FF), wmap),
                      pl.BlockSpec((None, None, D_FF, D_MODEL), wmap),
                      pl.BlockSpec((None, None, 1, D_MODEL), wmap)],
            out_specs=pl.BlockSpec(memory_space=pl.ANY),
            scratch_shapes=[pltpu.VMEM((2, bm * ROW_TILE, LANES), F32), pltpu.VMEM((bm, D_MODEL), BF16),
                            pltpu.VMEM((2, bm * ROW_TILE, LANES), F32),
                            pltpu.VMEM((D_MODEL, 2 * D_FF), BF16), pltpu.VMEM((D_FF, D_MODEL), BF16),
                            pltpu.SemaphoreType.DMA((2,)), pltpu.SemaphoreType.DMA((2,))]),
        out_shape=_sds((nblk * bm * ROW_TILE, LANES)),
        compiler_params=_cparams("arbitrary"), name="moe_experts",
    )(block_e, n_used, inv, h, w1, b1, w2, b2)


def _combine_kernel(gates_ref, y4_ref, h_ref, g_ref, b_ref, out_ref, acc_scr, *, alpha):
    tm = out_ref.shape[0]
    tok0 = pl.program_id(0) * tm

    def group(gi, carry):
        t0 = pl.multiple_of(gi * SUBLANES, SUBLANES)
        tiles = []
        for u in range(SUBLANES):
            t = t0 + u
            tile = alpha * h_ref[pl.ds(pl.multiple_of(t * ROW_TILE, ROW_TILE), ROW_TILE), :]
            for k in range(TOP_K):
                r = (t * TOP_K + k) * ROW_TILE
                tile = tile + (gates_ref[(tok0 + t) * TOP_K + k]
                               * y4_ref[pl.ds(pl.multiple_of(r, ROW_TILE), ROW_TILE), :])
            tiles.append(tile.reshape(1, ROW_TILE, LANES))
        rows = jnp.swapaxes(jnp.concatenate(tiles, axis=0), 0, 1)
        for s in range(ROW_TILE):
            acc_scr[pl.ds(t0, SUBLANES), s * LANES:(s + 1) * LANES] = rows[s]
        return carry
    lax.fori_loop(0, tm // SUBLANES, group, 0)
    out_ref[...] = _ln(acc_scr[...], g_ref[...], b_ref[...])


def _combine(y4, gates_flat, h, n, g, b, alpha):
    tm = _pick(n, (224, 128, 64, 32, 16, 8))
    row = lambda i, gt: (i, 0)
    const = lambda i, gt: (0, 0)
    return pl.pallas_call(
        functools.partial(_combine_kernel, alpha=alpha),
        grid_spec=pltpu.PrefetchScalarGridSpec(
            num_scalar_prefetch=1, grid=(n // tm,),
            in_specs=[pl.BlockSpec((TOP_K * tm * ROW_TILE, LANES), row),
                      pl.BlockSpec((tm * ROW_TILE, LANES), row),
                      pl.BlockSpec((1, D_MODEL), const), pl.BlockSpec((1, D_MODEL), const)],
            out_specs=pl.BlockSpec((tm, D_MODEL), row),
            scratch_shapes=[pltpu.VMEM((tm, D_MODEL), F32)]),
        out_shape=_sds((n, D_MODEL)),
        compiler_params=_cparams("arbitrary"), name="moe_combine",
    )(gates_flat, y4, h, g, b)


def _group_layout(cnt_tiles, nblk, bm):
    cnt = cnt_tiles.astype(jnp.int32)
    counts = jnp.sum(cnt, axis=1)
    padded = (counts + bm - 1) // bm * bm
    pad_end = jnp.cumsum(padded)
    pad_start = pad_end - padded
    base = pad_start[:, None] + jnp.cumsum(cnt, axis=1) - cnt
    starts = jnp.arange(nblk, dtype=jnp.int32) * bm
    groups_before = jnp.sum((pad_end[None, :] <= starts[:, None]).astype(jnp.int32), axis=1)
    block_e = jnp.minimum(groups_before, N_EXPERTS - 1).astype(jnp.int32)
    n_used = (pad_end[-1] // bm).astype(jnp.int32).reshape(1)
    total = jnp.sum(counts)
    real_before = jnp.concatenate([jnp.cumsum(counts), total.reshape(1)])[groups_before]
    pad_shift = (total - real_before).astype(jnp.int32)
    return base.T.astype(F32)[:, :, None], pad_shift, block_e, n_used


def _pack_w_in(w):
    o = 0
    xa, o = w[:, o:o + D_A], o + D_A
    ga, o = w[:, o:o + D_A], o + D_A
    vb, o = w[:, o:o + D_B], o + D_B
    gb, o = w[:, o:o + D_B], o + D_B
    q, o = w[:, o:o + D_QK_C], o + D_QK_C
    k, o = w[:, o:o + D_QK_C], o + D_QK_C
    v, o = w[:, o:o + D_C], o + D_C
    r, o = w[:, o:o + D_C], o + D_C
    z = w[:, o:o + GATE_RANK]
    zq = jnp.zeros((w.shape[0], PC_K - PC_Z - GATE_RANK), w.dtype)
    zk = jnp.zeros((w.shape[0], PC_V - PC_K - D_QK_C), w.dtype)
    return jnp.concatenate([xa, ga, vb, gb, q, z, zq, k, zk, v, r], axis=1).astype(BF16)


def _block_diag(w):
    eye = jnp.eye(N_BLK_A, dtype=w.dtype)
    return jnp.einsum("hij,hg->higj", w, eye).reshape(D_A, D_A)


def kernel(x_prompt, x_sample, state_conv_a, state_rglru, state_conv_b, state_gla, meta_tokens, ln0_g, ln0_b,
           w_in, conv_a_w, conv_a_b, w_rg, b_rg, w_ig, b_ig, lru_lambda, conv_b_w, conv_b_b, ln_b_g, ln_b_b,
           w_gate2, b_gate, gla_norm_g, w_out, ln1_g, ln1_b, router_w, router_b, moe_w1, moe_b1, moe_w2, moe_b2,
           ln2_g, ln2_b):
    bp, seq, _ = x_prompt.shape
    bs, dseq, _ = x_sample.shape
    depth = w_in.shape[0]
    tp = N_META + seq
    ts = SAMPLE_PAD_T
    np_rows = bp * tp
    ns_rows = bs * dseq
    n = np_rows + ns_rows
    alpha = (2 * depth) ** 0.25
    row = lambda a: a.reshape(1, -1)

    meta = jnp.broadcast_to(meta_tokens[None], (bp, N_META, D_MODEL))
    xp_in = jnp.concatenate([meta, x_prompt], axis=1).reshape(np_rows, D_MODEL)
    xs_in = jnp.pad(x_sample, ((0, 0), (0, ts - dseq), (0, 0))).reshape(bs * ts, D_MODEL)
    zeros_p = (jnp.zeros((bp, CONV_A - 1, D_A), F32), jnp.zeros((bp, 1, D_A), F32),
               jnp.zeros((bp, CONV_B - 1, D_B), F32), jnp.zeros((bp, N_HEADS_C, DK_C, DV_C), F32))
    sb = _pick(bs, (8, 4, 2, 1))
    tm_p = _pick(np_rows, (384, 512, 256, 128, 64, 32, 16, 8))
    tm_s = _pick(math.gcd(ns_rows, np_rows), (128, 64, 32, 16, 8))
    nblk = -(-(TOP_K * n) // MOE_BM) + N_EXPERTS
    while (nblk * MOE_BM // TOP_K) % math.lcm(tm_p, tm_s):
        nblk += 1

    new_p = ([], [], [], [])
    new_s = ([], [], [], [])
    x_all = None
    for l in range(depth):
        w_packed = _pack_w_in(w_in[l])
        wg = jnp.concatenate([_block_diag(w_rg[l]), _block_diag(w_ig[l])], axis=1).astype(BF16)
        bg = jnp.concatenate([b_rg[l], b_ig[l]]).reshape(1, -1)
        w_out_b = w_out[l].astype(BF16)
        ng = row(jnp.tile(gla_norm_g[l], N_HEADS_C))
        first = l == 0
        if first:
            pa_p, pb_p, pc_p, xn_p = _inproj(xp_in, 0, np_rows, row(ln0_g), row(ln0_b), w_packed, True)
            pa_s, pb_s, pc_s, xn_s = _inproj(xs_in, 0, bs * ts, row(ln0_g), row(ln0_b), w_packed, True)
            res_p, res_p_row0 = xn_p, 0
        else:
            pa_p, pb_p, pc_p = _inproj(x_all, 0, np_rows, row(ln0_g), row(ln0_b), w_packed, False)
            xn_s = jnp.pad(x_all[np_rows:].reshape(bs, dseq, D_MODEL),
                           ((0, 0), (0, ts - dseq), (0, 0))).reshape(bs * ts, D_MODEL)
            pa_s, pb_s, pc_s = _inproj(xn_s, 0, bs * ts, row(ln0_g), row(ln0_b), w_packed, False)
            res_p, res_p_row0 = x_all, 0

        outs = []
        for (pa, pb, pc, nb, t, tv, bb, st) in (
                (pa_p, pb_p, pc_p, bp, tp, tp, 1, zeros_p),
                (pa_s, pb_s, pc_s, bs, ts, dseq, sb,
                 (state_conv_a[l], state_rglru[l].reshape(bs, 1, D_A), state_conv_b[l], state_gla[l]))):
            ya, ca_new, h_last = _rglru(pa.reshape(nb, t, PA_W), st[0], st[1], conv_a_w[l], row(conv_a_b[l]),
                                        wg, bg, row(lru_lambda[l]), tv, bb)
            yb, cb_new = _convb(pb.reshape(nb, t, PB_W), st[2], conv_b_w[l], row(conv_b_b[l]),
                                row(ln_b_g[l]), row(ln_b_b[l]), tv, bb)
            yc, s_new = _gla(pc, nb, t, st[3], w_gate2[l], row(b_gate[l]), ng, tv, bb)
            outs.append((ya.reshape(nb * t, D_A), yb.reshape(nb * t, D_B), yc,
                         ca_new, h_last.reshape(nb, D_A), cb_new, s_new))
        (ya_p, yb_p, yc_p, *st_p), (ya_s, yb_s, yc_s, *st_s) = outs
        for j in range(4):
            new_p[j].append(st_p[j])
            new_s[j].append(st_s[j])

        valid = lambda a: a.reshape(bs, ts, -1)[:, :dseq].reshape(ns_rows, -1)
        h_all = _outproj(ya_p, yb_p, yc_p, res_p, res_p_row0, w_out_b, row(ln1_g[l]), row(ln1_b[l]), alpha,
                         tm_p, nblk * MOE_BM // TOP_K)
        h_all = _outproj(valid(ya_s), valid(yb_s), valid(yc_s), valid(xn_s), 0, w_out_b, row(ln1_g[l]),
                         row(ln1_b[l]), alpha, tm_s, nblk * MOE_BM // TOP_K, out_row0=np_rows, into=h_all)

        top_idx, gates, cnt = _router(h_all, n, router_w[l].T, router_b[l].reshape(-1, 1))
        base, pad_shift, block_e, n_used = _group_layout(cnt[:, ::LANES], nblk, MOE_BM)
        dest = _dest(top_idx, base)
        inv = _invert(pad_shift, dest, MOE_BM)
        y4 = _moe(block_e, n_used, inv, h_all, l, moe_w1, moe_b1[:, :, None, :], moe_w2, moe_b2[:, :, None, :],
                  MOE_BM)
        x_all = _combine(y4, gates.T.reshape(-1), h_all, n, row(ln2_g[l]), row(ln2_b[l]), alpha)

    y_p = x_all[:np_rows].reshape(bp, tp, D_MODEL)[:, N_META:]
    y_s = x_all[np_rows:].reshape(bs, dseq, D_MODEL)
    return (y_p, y_s,
            jnp.stack(new_p[0]), jnp.stack(new_p[1]), jnp.stack(new_p[2]), jnp.stack(new_p[3]),
            jnp.stack(new_s[0]), jnp.stack(new_s[1]), jnp.stack(new_s[2]), jnp.stack(new_s[3]))
```

```python
import functools
import math

import jax
import jax.numpy as jnp
from jax import lax
from jax.experimental import pallas as pl
from jax.experimental.pallas import tpu as pltpu
from jax.experimental.pallas import tpu_sc as plsc

F32 = jnp.float32
BF16 = jnp.bfloat16

D_MODEL = 1024
N_META = 16
D_A = 384
D_B = 256
D_C = 384
N_BLK_A = 8
BLK_A = D_A // N_BLK_A
CONV_A = 4
RG_C = 8.0
CONV_B = 31
N_HEADS_C = 4
DV_C = D_C // N_HEADS_C
DK_C = DV_C // 2
D_QK_C = N_HEADS_C * DK_C
GATE_RANK = 16
GATE_TAU = 16.0
N_EXPERTS = 32
TOP_K = 4
D_FF = D_MODEL
SWIGLU_LIMIT = 7.0
SWIGLU_ALPHA = 1.702
LN_EPS = 1e-5

PA_W = 2 * D_A
PB_W = 2 * D_B
PC_Q, PC_Z, PC_K, PC_V, PC_R, PC_W = 0, 192, 256, 512, 896, 1280
P_W = PA_W + PB_W + PC_W

SUBLANES = 8
LANES = 128
SC_LANES = 16
SC_CORES, SC_SUBCORES = 2, 16
ROW_TILE = D_MODEL // LANES
VMEM_LIMIT_BYTES = 56 * 1024 * 1024
MOE_BM = 256
FF_CHUNK = 512
DMA_GROUP = 8
W_CAST_ROWS = 128
SAMPLE_PAD_T = 8


def _cparams(*sem):
    return pltpu.CompilerParams(dimension_semantics=sem, vmem_limit_bytes=VMEM_LIMIT_BYTES)


def _sds(shape, dtype=F32):
    return jax.ShapeDtypeStruct(shape, dtype)


def _pick(n, prefs):
    for p in prefs:
        if n % p == 0:
            return p
    raise ValueError(f"no tile for {n} in {prefs}")


def _ln(x, g, b):
    mu = jnp.mean(x, axis=-1, keepdims=True)
    xc = x - mu
    var = jnp.mean(xc * xc, axis=-1, keepdims=True)
    return xc * lax.rsqrt(var + LN_EPS) * g + b


def _sigmoid(x):
    return 1.0 / (1.0 + jnp.exp(-x))


def _split_bf16(x):
    hi = x.astype(BF16)
    lo = (x - hi.astype(F32)).astype(BF16)
    return hi, lo


def _dot(a, b):
    return jnp.dot(a, b, preferred_element_type=F32)


def _inproj_kernel(x_ref, g_ref, b_ref, w_ref, pa_ref, pb_ref, pc_ref, *maybe_xn, apply_ln):
    x = x_ref[...]
    if apply_ln:
        x = _ln(x, g_ref[...], b_ref[...])
        maybe_xn[0][...] = x
    xb = x.astype(BF16)
    pa_ref[...] = _dot(xb, w_ref[:, 0:PA_W])
    pb_ref[...] = _dot(xb, w_ref[:, PA_W:PA_W + PB_W])
    pc_ref[...] = _dot(xb, w_ref[:, PA_W + PB_W:P_W])


def _inproj(x, row0, nrows, ln_g, ln_b, w_packed, apply_ln):
    tm = _pick(nrows, (384, 512, 256, 128, 64, 32, 16, 8))
    while row0 % tm:
        tm //= 2
    off = row0 // tm
    const = lambda i: (0, 0)
    row = lambda i: (i, 0)
    out_shape = [_sds((nrows, PA_W)), _sds((nrows, PB_W)), _sds((nrows, PC_W))]
    out_specs = [pl.BlockSpec((tm, PA_W), row), pl.BlockSpec((tm, PB_W), row), pl.BlockSpec((tm, PC_W), row)]
    if apply_ln:
        out_shape.append(_sds((nrows, D_MODEL)))
        out_specs.append(pl.BlockSpec((tm, D_MODEL), row))
    return pl.pallas_call(
        functools.partial(_inproj_kernel, apply_ln=apply_ln),
        grid=(nrows // tm,),
        in_specs=[pl.BlockSpec((tm, D_MODEL), lambda i: (i + off, 0)),
                  pl.BlockSpec((1, D_MODEL), const), pl.BlockSpec((1, D_MODEL), const),
                  pl.BlockSpec((D_MODEL, P_W), const)],
        out_specs=out_specs, out_shape=out_shape,
        compiler_params=_cparams("parallel"), name="inproj",
    )(x, ln_g, ln_b, w_packed)


def _rglru_kernel(p_ref, cbuf_ref, h0_ref, cw_ref, cb_ref, wg_ref, bg_ref, lam_ref,
                  y_ref, cnew_ref, hlast_ref, xp_scr, a_scr, h_scr, *, T, Tc, Tv, Bb):
    lam = lam_ref[...]
    softplus_neg = jnp.maximum(-lam, 0.0) + jnp.log1p(jnp.exp(-jnp.abs(lam)))
    c_decay = -RG_C * softplus_neg
    cw = cw_ref[...]
    cb = cb_ref[...]
    bg = bg_ref[...]
    sub = lax.broadcasted_iota(jnp.int32, (Tc, D_A), 0) % SUBLANES
    halo = SUBLANES - (CONV_A - 1)
    for b in range(Bb):
        xp_scr[halo:SUBLANES, :] = cbuf_ref[b]
        xp_scr[SUBLANES:SUBLANES + T, :] = p_ref[b, :, 0:D_A]
        cnew_ref[b] = xp_scr[halo + Tv:SUBLANES + Tv, :]

        def chunk(ci, h_b):
            r0 = pl.multiple_of(ci * Tc, SUBLANES)
            win = xp_scr[pl.ds(r0, Tc + SUBLANES), :]
            xc = cb + cw[0:1] * win[halo:halo + Tc]
            for j in range(1, CONV_A):
                xc = xc + cw[j:j + 1] * win[halo + j:halo + j + Tc]
            gates = _dot(xc.astype(BF16), wg_ref[...]) + bg
            r = _sigmoid(gates[:, 0:D_A])
            i = _sigmoid(gates[:, D_A:2 * D_A])
            log_a = c_decay * r
            a = jnp.exp(log_a)
            u = jnp.sqrt(1.0 - a * a) * (i * xc)
            for s in (1, 2, 4):
                keep = sub >= s
                a_prev = pltpu.roll(a, s, 0)
                u_prev = pltpu.roll(u, s, 0)
                u = jnp.where(keep, a * u_prev + u, u)
                a = jnp.where(keep, a * a_prev, a)
            a_scr[...] = a
            h_scr[pl.ds(r0, Tc), :] = u

            def group(gi, h_b):
                c0 = pl.multiple_of(gi * SUBLANES, SUBLANES)
                g0 = pl.multiple_of(r0 + gi * SUBLANES, SUBLANES)
                h8 = a_scr[pl.ds(c0, SUBLANES), :] * h_b + h_scr[pl.ds(g0, SUBLANES), :]
                h_scr[pl.ds(g0, SUBLANES), :] = h8
                return jnp.broadcast_to(h8[SUBLANES - 1:SUBLANES, :], (SUBLANES, D_A))

            h_b = lax.fori_loop(0, Tc // SUBLANES, group, h_b)
            ga = p_ref[b, pl.ds(r0, Tc), D_A:2 * D_A]
            gelu = 0.5 * ga * (1.0 + jnp.tanh(0.7978845608028654 * (ga + 0.044715 * ga * ga * ga)))
            y_ref[b, pl.ds(r0, Tc), :] = h_scr[pl.ds(r0, Tc), :] * gelu
            return h_b

        h_b = jnp.broadcast_to(h0_ref[b], (SUBLANES, D_A))
        lax.fori_loop(0, T // Tc, chunk, h_b)
        hlast_ref[b] = h_scr[Tv - 1:Tv, :]


def _rglru(pa3, cbuf, h0, cw, cb, wg, bg, lam, Tv, Bb):
    B, T, _ = pa3.shape
    Tc = _pick(T, (344, 256, 128, 64, 48, 32, 16, 8))
    const2 = lambda i: (0, 0)
    seq3 = lambda i: (i, 0, 0)
    return pl.pallas_call(
        functools.partial(_rglru_kernel, T=T, Tc=Tc, Tv=Tv, Bb=Bb),
        grid=(B // Bb,),
        in_specs=[pl.BlockSpec((Bb, T, PA_W), seq3), pl.BlockSpec((Bb, CONV_A - 1, D_A), seq3),
                  pl.BlockSpec((Bb, 1, D_A), seq3), pl.BlockSpec((CONV_A, D_A), const2),
                  pl.BlockSpec((1, D_A), const2), pl.BlockSpec((D_A, 2 * D_A), const2),
                  pl.BlockSpec((1, 2 * D_A), const2), pl.BlockSpec((1, D_A), const2)],
        out_specs=[pl.BlockSpec((Bb, T, D_A), seq3), pl.BlockSpec((Bb, CONV_A - 1, D_A), seq3),
                   pl.BlockSpec((Bb, 1, D_A), seq3)],
        out_shape=[_sds((B, T, D_A)), _sds((B, CONV_A - 1, D_A)), _sds((B, 1, D_A))],
        scratch_shapes=[pltpu.VMEM((T + 2 * SUBLANES, D_A), F32), pltpu.VMEM((Tc, D_A), F32),
                        pltpu.VMEM((T, D_A), F32)],
        compiler_params=_cparams("parallel"), name="rglru",
    )(pa3, cbuf, h0, cw, cb, wg, bg, lam)


B_HALO = 32


def _convb_kernel(p_ref, buf_ref, w_ref, cb_ref, g_ref, b_ref, y_ref, bnew_ref, u_scr, *, T, Tc, Tv, Bb):
    w = w_ref[...]
    cb = cb_ref[...]
    g = g_ref[...]
    bb = b_ref[...]
    first = B_HALO - (CONV_B - 1)
    for b in range(Bb):
        u_scr[0:first, :] = jnp.zeros((first, D_B), F32)
        u_scr[first:B_HALO, :] = buf_ref[b]
        u_scr[B_HALO:B_HALO + T, :] = p_ref[b, :, 0:D_B] * _sigmoid(p_ref[b, :, D_B:2 * D_B])
        bnew_ref[b] = u_scr[first + Tv:B_HALO + Tv, :]

        def chunk(ci, carry):
            r0 = pl.multiple_of(ci * Tc, SUBLANES)
            win = u_scr[pl.ds(r0, Tc + B_HALO), :]
            shifted = [win] + [pltpu.roll(win, Tc + B_HALO - s, 0) for s in range(1, SUBLANES)]
            acc = cb
            for j in range(CONV_B):
                a, s = divmod(first + j, SUBLANES)
                acc = acc + w[j:j + 1] * shifted[s][a * SUBLANES:a * SUBLANES + Tc]
            yn = _ln(acc, g, bb)
            y_ref[b, pl.ds(r0, Tc), :] = yn * _sigmoid(yn)
            return carry

        lax.fori_loop(0, T // Tc, chunk, 0)


def _convb(pb3, buf, w, cb, g, b, Tv, Bb):
    B, T, _ = pb3.shape
    Tc = _pick(T, (48, 32, 16, 8))
    const2 = lambda i: (0, 0)
    seq3 = lambda i: (i, 0, 0)
    return pl.pallas_call(
        functools.partial(_convb_kernel, T=T, Tc=Tc, Tv=Tv, Bb=Bb),
        grid=(B // Bb,),
        in_specs=[pl.BlockSpec((Bb, T, PB_W), seq3), pl.BlockSpec((Bb, CONV_B - 1, D_B), seq3),
                  pl.BlockSpec((CONV_B, D_B), const2), pl.BlockSpec((1, D_B), const2),
                  pl.BlockSpec((1, D_B), const2), pl.BlockSpec((1, D_B), const2)],
        out_specs=[pl.BlockSpec((Bb, T, D_B), seq3), pl.BlockSpec((Bb, CONV_B - 1, D_B), seq3)],
        out_shape=[_sds((B, T, D_B)), _sds((B, CONV_B - 1, D_B))],
        scratch_shapes=[pltpu.VMEM((T + B_HALO, D_B), F32)],
        compiler_params=_cparams("parallel"), name="convb",
    )(pb3, buf, w, cb, g, b)


def _gla_kernel(p_ref, s0_ref, wg2_ref, bgate_ref, ng_ref, y_ref, snew_ref, s_scr, g_scr, *, T, C, Tb, Tv, Bb):
    ri = lax.broadcasted_iota(jnp.int32, (C, C), 0)
    ci_ = lax.broadcasted_iota(jnp.int32, (C, C), 1)
    tril = ri >= ci_
    lane_k = lax.broadcasted_iota(jnp.int32, (1, D_QK_C), 1)
    lane_v = lax.broadcasted_iota(jnp.int32, (1, D_C), 1)
    hm_k = [(lane_k >= h * DK_C) & (lane_k < (h + 1) * DK_C) for h in range(N_HEADS_C)]
    hm_v = [(lane_v >= h * DV_C) & (lane_v < (h + 1) * DV_C) for h in range(N_HEADS_C)]
    rs = lax.broadcasted_iota(jnp.int32, (D_C, D_QK_C), 0)
    cs = lax.broadcasted_iota(jnp.int32, (D_C, D_QK_C), 1)
    bd_t = (rs >= 0) & (rs < 0)
    for h in range(N_HEADS_C):
        bd_t = bd_t | ((rs >= h * DV_C) & (rs < (h + 1) * DV_C) & (cs >= h * DK_C) & (cs < (h + 1) * DK_C))
    rm = lax.broadcasted_iota(jnp.int32, (D_C, D_C), 0)
    cm = lax.broadcasted_iota(jnp.int32, (D_C, D_C), 1)
    seg = (rm >= 0) & (rm < 0)
    for h in range(N_HEADS_C):
        seg = seg | ((rm >= h * DV_C) & (rm < (h + 1) * DV_C) & (cm >= h * DV_C) & (cm < (h + 1) * DV_C))
    mseg = jnp.where(seg, 1.0, 0.0).astype(BF16)
    wg2 = wg2_ref[...].astype(BF16)
    bgate = bgate_ref[...]
    ng = ng_ref[...]
    rowi = lax.broadcasted_iota(jnp.int32, (C, 1), 0)
    tdims = (((0,), (0,)), ((), ()))

    n_chunks = T // C
    nt_dims = (((1,), (1,)), ((), ()))
    tril4 = jnp.concatenate([tril] * N_HEADS_C, axis=0)
    scan_shifts = [s for s in (1, 2, 4, 8, 16, 32) if s < C]
    scan_keep = [rowi >= s for s in scan_shifts]

    def tile_rows(ti, tb, base=0):
        r0 = ti * tb
        return pl.ds(base + (r0 if isinstance(ti, int) else pl.multiple_of(r0, tb)), tb)

    def gates(ti, carry):
        rows = tile_rows(ti, Tb)
        z = p_ref[rows, PC_Z:PC_Z + GATE_RANK]
        pre = _dot(z.astype(BF16), wg2) + bgate
        g = (jnp.minimum(pre, 0.0) - jnp.log1p(jnp.exp(-jnp.abs(pre)))) * (1.0 / GATE_TAU)
        rid = ti * Tb + lax.broadcasted_iota(jnp.int32, (Tb, 1), 0)
        if Bb > 1:
            rid = rid & (T - 1)
        g_scr[rows, :] = jnp.where(rid < Tv, g, 0.0)
        return carry
    lax.fori_loop(0, Bb * T // Tb, gates, 0)

    for b in range(Bb):
        def chunk(ci, carry):
            rows = tile_rows(ci, C, b * T)
            q = p_ref[rows, PC_Q:PC_Q + D_QK_C] * (DK_C ** -0.5)
            k = p_ref[rows, PC_K:PC_K + D_QK_C]
            v = p_ref[rows, PC_V:PC_V + D_C]
            k = jnp.where((ci * C + rowi) < Tv, k, 0.0)
            gcum = g_scr[rows, :]
            for s, keep in zip(scan_shifts, scan_keep):
                gcum = gcum + jnp.where(keep, pltpu.roll(gcum, s, 0), 0.0)
            g_last = gcum[C - 1:C, :]
            g_mid = gcum[C // 2 - 1:C // 2, :]
            vb = v.astype(BF16)
            qt = q * jnp.exp(gcum - g_mid)
            ktb = (k * jnp.exp(g_mid - gcum)).astype(BF16)
            q4 = jnp.concatenate([jnp.where(hm_k[h], qt, 0.0) for h in range(N_HEADS_C)], axis=0).astype(BF16)
            sc = lax.dot_general(q4, ktb, nt_dims, preferred_element_type=F32)
            r4 = _dot(jnp.where(tril4, sc, 0.0).astype(BF16), vb)
            o = jnp.where(hm_v[0], r4[0:C], 0.0)
            for h in range(1, N_HEADS_C):
                o = o + jnp.where(hm_v[h], r4[h * C:(h + 1) * C], 0.0)
            kd = (k * jnp.exp(g_last - gcum)).astype(BF16)
            upd_t = lax.dot_general(vb, kd, tdims, preferred_element_type=F32)
            s_in = s_scr[...]
            o = o + lax.dot_general((q * jnp.exp(gcum)).astype(BF16), s_in.astype(BF16), nt_dims,
                                    preferred_element_type=F32)
            s_scr[...] = s_in * jnp.exp(g_last) + jnp.where(bd_t, upd_t, 0.0)
            y_ref[rows, :] = o
            return carry

        s_scr[...] = jnp.zeros((D_C, D_QK_C), F32)
        for h in range(N_HEADS_C):
            s_scr[h * DV_C:(h + 1) * DV_C, h * DK_C:(h + 1) * DK_C] = s0_ref[b, h]
        lax.fori_loop(0, n_chunks, chunk, 0)
        for h in range(N_HEADS_C):
            snew_ref[b, h] = s_scr[h * DV_C:(h + 1) * DV_C, h * DK_C:(h + 1) * DK_C]

    def finish(ti, carry):
        rows = tile_rows(ti, Tb)
        o = y_ref[rows, :]
        rg = p_ref[rows, PC_R:PC_R + D_C]
        o2_hi, o2_lo = _split_bf16(o * o)
        ms = (_dot(o2_hi, mseg) + _dot(o2_lo, mseg)) * (1.0 / DV_C)
        y_ref[rows, :] = o * lax.rsqrt(ms + LN_EPS) * ng * (rg * _sigmoid(rg))
        return carry
    lax.fori_loop(0, Bb * T // Tb, finish, 0)


def _gla(pc, B, T, s0, wg2, bgate, ng, Tv, Bb):
    assert Bb == 1 or T & (T - 1) == 0
    C = _pick(T, (48, 32, 16, 8))
    Tb = _pick(Bb * T, (344, 256, 128, 64, 48, 32, 16, 8))
    const2 = lambda i: (0, 0)
    row2 = lambda i: (i, 0)
    seq4 = lambda i: (i, 0, 0, 0)
    st = (Bb, N_HEADS_C, DV_C, DK_C)
    y, s_new_t = pl.pallas_call(
        functools.partial(_gla_kernel, T=T, C=C, Tb=Tb, Tv=Tv, Bb=Bb),
        grid=(B // Bb,),
        in_specs=[pl.BlockSpec((Bb * T, PC_W), row2), pl.BlockSpec(st, seq4),
                  pl.BlockSpec((GATE_RANK, D_QK_C), const2), pl.BlockSpec((1, D_QK_C), const2),
                  pl.BlockSpec((1, D_C), const2)],
        out_specs=[pl.BlockSpec((Bb * T, D_C), row2), pl.BlockSpec(st, seq4)],
        out_shape=[_sds((B * T, D_C)), _sds((B, N_HEADS_C, DV_C, DK_C))],
        scratch_shapes=[pltpu.VMEM((D_C, D_QK_C), F32), pltpu.VMEM((Bb * T, D_QK_C), F32)],
        compiler_params=_cparams("parallel"), name="gla",
    )(pc, jnp.swapaxes(s0, 2, 3), wg2, bgate, ng)
    return y, jnp.swapaxes(s_new_t, 2, 3)


def _tiles_to_rows(ref, lead, start, rows, stride):
    return jnp.concatenate([ref[lead + (pl.ds(start + s, rows, stride=stride), slice(None))]
                            for s in range(ROW_TILE)], axis=1)


def _rows_to_tiles(ref, lead, rows, val):
    for s in range(ROW_TILE):
        ref[lead + (pl.ds(s, rows, stride=ROW_TILE), slice(None))] = val[:, s * LANES:(s + 1) * LANES]


def _outproj_kernel(ya_ref, yb_ref, yc_ref, x_ref, w_ref, g_ref, b_ref, *rest, alpha, n_blocks):
    h_ref = rest[-1]

    @pl.when(pl.program_id(0) < n_blocks)
    def _():
        y = (_dot(ya_ref[...].astype(BF16), w_ref[0:D_A, :])
             + _dot(yb_ref[...].astype(BF16), w_ref[D_A:D_A + D_B, :])
             + _dot(yc_ref[...].astype(BF16), w_ref[D_A + D_B:D_MODEL, :]))
        h = _ln(alpha * x_ref[...] + y, g_ref[...], b_ref[...])
        _rows_to_tiles(h_ref, (), h.shape[0], h)

    @pl.when(pl.program_id(0) >= n_blocks)
    def _():
        h_ref[...] = jnp.zeros(h_ref.shape, F32)


def _outproj(ya, yb, yc, x, row0, w, g, b, alpha, tm, out_rows, out_row0=0, into=None):
    n = ya.shape[0]
    nb = n // tm
    assert n % tm == 0 and row0 % tm == 0 and out_row0 % tm == 0 and out_rows % tm == 0
    off, out_off = row0 // tm, out_row0 // tm
    grid = nb if into is not None else out_rows // tm
    row = lambda i: (jnp.minimum(i, nb - 1), 0)
    const = lambda i: (0, 0)
    in_specs = [pl.BlockSpec((tm, D_A), row), pl.BlockSpec((tm, D_B), row), pl.BlockSpec((tm, D_C), row),
                pl.BlockSpec((tm, D_MODEL), lambda i: (jnp.minimum(i, nb - 1) + off, 0)),
                pl.BlockSpec((D_MODEL, D_MODEL), const), pl.BlockSpec((1, D_MODEL), const),
                pl.BlockSpec((1, D_MODEL), const)]
    args = [ya, yb, yc, x, w, g, b]
    aliases = {}
    if into is not None:
        in_specs.append(pl.BlockSpec(memory_space=pl.ANY))
        args.append(into)
        aliases = {len(args) - 1: 0}
    return pl.pallas_call(
        functools.partial(_outproj_kernel, alpha=alpha, n_blocks=nb),
        grid=(grid,),
        in_specs=in_specs,
        out_specs=pl.BlockSpec((tm * ROW_TILE, LANES), lambda i: (i + out_off, 0)),
        out_shape=_sds((out_rows * ROW_TILE, LANES)),
        input_output_aliases=aliases,
        compiler_params=_cparams("arbitrary"), name="outproj",
    )(*args)


def _router_kernel(h_ref, wt_ref, b_ref, idx_ref, gate_ref, cnt_ref):
    nt = (((1,), (1,)), ((), ()))
    hh, hl = _split_bf16(_tiles_to_rows(h_ref, (), 0, idx_ref.shape[1], ROW_TILE))
    wh, wl = _split_bf16(wt_ref[...])
    logits = (lax.dot_general(wh, hh, nt, preferred_element_type=F32)
              + lax.dot_general(wh, hl, nt, preferred_element_type=F32)
              + lax.dot_general(wl, hh, nt, preferred_element_type=F32)) + b_ref[...]
    eid = lax.broadcasted_iota(jnp.int32, logits.shape, 0)
    vals = []
    member = jnp.zeros(logits.shape, F32)
    for k in range(TOP_K):
        m = jnp.max(logits, axis=0, keepdims=True)
        sel = jnp.min(jnp.where(logits == m, eid, N_EXPERTS), axis=0, keepdims=True)
        idx_ref[k:k + 1, :] = sel
        vals.append(m)
        hit = eid == sel
        member = jnp.where(hit, 1.0, member)
        logits = jnp.where(hit, -jnp.inf, logits)
    es = [jnp.exp(v - vals[0]) for v in vals]
    tot = es[0] + es[1] + es[2] + es[3]
    for k in range(TOP_K):
        gate_ref[k:k + 1, :] = es[k] / tot
    cnt_ref[...] = jnp.broadcast_to(jnp.sum(member, axis=1, keepdims=True), cnt_ref.shape)


def _router_tile(n):
    return _pick(n, (896, 640, 512, 384, 256, 128))


def _router(h, n, wt, b):
    tm = _router_tile(n)
    nt = n // tm
    return pl.pallas_call(
        _router_kernel,
        grid=(nt,),
        in_specs=[pl.BlockSpec((tm * ROW_TILE, LANES), lambda i: (i, 0)),
                  pl.BlockSpec((N_EXPERTS, D_MODEL), lambda i: (0, 0)),
                  pl.BlockSpec((N_EXPERTS, 1), lambda i: (0, 0))],
        out_specs=[pl.BlockSpec((TOP_K, tm), lambda i: (0, i)), pl.BlockSpec((TOP_K, tm), lambda i: (0, i)),
                   pl.BlockSpec((N_EXPERTS, LANES), lambda i: (0, i))],
        out_shape=[_sds((TOP_K, n), jnp.int32), _sds((TOP_K, n)), _sds((N_EXPERTS, nt * LANES))],
        compiler_params=_cparams("parallel"), name="router",
    )(h, wt, b)


def _dest_kernel(idx_ref, base_ref, dest_ref):
    tm = idx_ref.shape[1]
    eid = lax.broadcasted_iota(jnp.int32, (N_EXPERTS, tm), 0)
    hits = [eid == idx_ref[k:k + 1, :] for k in range(TOP_K)]
    member = jnp.zeros((N_EXPERTS, tm), F32)
    for k in range(TOP_K):
        member = jnp.where(hits[k], 1.0, member)
    earlier = (lax.broadcasted_iota(jnp.int32, (tm, tm), 0) < lax.broadcasted_iota(jnp.int32, (tm, tm), 1))
    rank = _dot(member.astype(BF16), jnp.where(earlier, 1.0, 0.0).astype(BF16))
    pos = base_ref[...] + rank
    for k in range(TOP_K):
        dest_ref[k:k + 1, :] = jnp.sum(jnp.where(hits[k], pos, 0.0), axis=0, keepdims=True).astype(jnp.int32)


def _dest(top_idx, base):
    n = top_idx.shape[1]
    tm = _router_tile(n)
    return pl.pallas_call(
        _dest_kernel,
        grid=(n // tm,),
        in_specs=[pl.BlockSpec((TOP_K, tm), lambda i: (0, i)),
                  pl.BlockSpec((None, N_EXPERTS, 1), lambda i: (i, 0, 0))],
        out_specs=pl.BlockSpec((TOP_K, tm), lambda i: (0, i)),
        out_shape=_sds((TOP_K, n), jnp.int32),
        compiler_params=_cparams("parallel"), name="moe_dest",
    )(top_idx, base)


def _invert(pad_shift, dest, bm):
    k, n = dest.shape
    n_rows = pad_shift.shape[0] * bm
    ch = next(c for c in range(min(n, 4096) // SC_LANES * SC_LANES, 0, -SC_LANES) if n % c == 0)
    fill = (jnp.arange(n_rows, dtype=jnp.int32).reshape(-1, bm) + pad_shift[:, None]).reshape(-1)
    mesh = plsc.VectorSubcoreMesh(core_axis_name="c", subcore_axis_name="s", num_cores=SC_CORES,
                                  num_subcores=SC_SUBCORES)

    @functools.partial(
        pl.kernel, mesh=mesh, out_type=_sds((n_rows,), jnp.int32),
        scratch_types=[pltpu.VMEM((n_rows,), jnp.int32), pltpu.VMEM((ch,), jnp.int32)],
        compiler_params=pltpu.CompilerParams(needs_layout_passes=False), name="moe_invert_sc")
    def run(fill_hbm, dest_hbm, inv_hbm, inv_v, d_v):
        first = jnp.logical_and(lax.axis_index("c") == 0, lax.axis_index("s") == 0)

        @pl.when(first)
        def _():
            pltpu.sync_copy(fill_hbm, inv_v)
            lanes = lax.iota(jnp.int32, SC_LANES)
            for slot in range(k):
                def chunk(c, carry):
                    pltpu.sync_copy(dest_hbm.at[pl.ds(slot * n + c * ch, ch)], d_v)

                    def vec(v, carry):
                        idx = d_v[pl.ds(v * SC_LANES, SC_LANES)]
                        plsc.store_scatter(inv_v, [idx], (c * ch + v * SC_LANES + lanes) * k + slot)
                        return carry
                    return lax.fori_loop(0, ch // SC_LANES, vec, carry)
                lax.fori_loop(0, n // ch, chunk, 0)
            pltpu.sync_copy(inv_v, inv_hbm)

    return run(fill, dest.reshape(-1))


def _moe_kernel(be_ref, nu_ref, inv_ref, h_hbm, w1f_ref, b1_ref, w2f_ref, b2_ref, y4_hbm,
                xbuf, xb, obuf, w1_ref, w2_ref, gsem, ssem, *, bm):
    i = pl.program_id(0)
    n_used = nu_ref[0]
    last_blk = pl.num_programs(0) - 1
    n_ff = D_FF // FF_CHUNK

    def tile(r):
        start = r * ROW_TILE
        return pl.ds(start if isinstance(r, int) else pl.multiple_of(start, ROW_TILE), ROW_TILE)

    def gather_row(blk, slot, j, r=None):
        r = inv_ref[blk * bm + j] if r is None else r
        t = lax.shift_right_logical(r, 2)
        pltpu.make_async_copy(h_hbm.at[tile(t), :], xbuf.at[slot, tile(j), :], gsem.at[slot]).start()

    def scatter_row(blk, slot, j, r=None):
        r = inv_ref[blk * bm + j] if r is None else r
        pltpu.make_async_copy(obuf.at[slot, tile(j), :], y4_hbm.at[tile(r), :], ssem.at[slot]).start()

    def wait_gather(slot):
        pltpu.make_async_copy(h_hbm.at[pl.ds(0, bm * ROW_TILE), :], xbuf.at[slot], gsem.at[slot]).wait()

    def wait_scatter(slot):
        pltpu.make_async_copy(obuf.at[slot], y4_hbm.at[pl.ds(0, bm * ROW_TILE), :], ssem.at[slot]).wait()

    def loop_rows(fn, blk, slot):
        def body(j, c):
            fn(blk, slot, j)
            return c
        lax.fori_loop(0, bm, body, 0)

    def issue_rows(fn, blk, slot, j_lo, j_hi):
        for j0 in range(j_lo, j_hi, DMA_GROUP):
            js = range(j0, j0 + DMA_GROUP)
            ids = [inv_ref[blk * bm + j] for j in js]
            for j, r in zip(js, ids):
                fn(blk, slot, j, r)

    def step():
        slot = i % 2

        @pl.when(i + 1 < n_used)
        def _():
            issue_rows(gather_row, i + 1, 1 - slot, 0, bm)
        wait_gather(slot)
        xb[...] = _tiles_to_rows(xbuf, (slot,), 0, bm, ROW_TILE).astype(BF16)
        acc = jnp.zeros((bm, D_MODEL), F32)
        for c in range(n_ff):
            lo = c * FF_CHUNK
            x = xb[...]
            hg = _dot(x, w1_ref[:, lo:lo + FF_CHUNK]) + b1_ref[:, lo:lo + FF_CHUNK]
            hl = _dot(x, w1_ref[:, D_FF + lo:D_FF + lo + FF_CHUNK]) + b1_ref[:, D_FF + lo:D_FF + lo + FF_CHUNK]
            gate = jnp.minimum(hg, SWIGLU_LIMIT)
            lin = jnp.clip(hl, -SWIGLU_LIMIT, SWIGLU_LIMIT)
            act = gate * _sigmoid(SWIGLU_ALPHA * gate) * (lin + 1.0)
            acc = acc + _dot(act.astype(BF16), w2_ref[lo:lo + FF_CHUNK, :])

        @pl.when(i >= 2)
        def _():
            wait_scatter(slot)
        _rows_to_tiles(obuf, (slot,), bm, acc + b2_ref[...])
        issue_rows(scatter_row, i, slot, 0, bm)

    new_expert = jnp.logical_or(i == 0, be_ref[i] != be_ref[jnp.maximum(i - 1, 0)])

    @pl.when(jnp.logical_and(new_expert, i < n_used))
    def _():
        def cast_rows(r, c):
            rows = pl.ds(pl.multiple_of(r * W_CAST_ROWS, W_CAST_ROWS), W_CAST_ROWS)
            w1_ref[rows, :] = w1f_ref[rows, :].astype(BF16)
            w2_ref[rows, :] = w2f_ref[rows, :].astype(BF16)
            return c
        lax.fori_loop(0, D_MODEL // W_CAST_ROWS, cast_rows, 0)

    @pl.when(jnp.logical_and(i == 0, n_used > 0))
    def _():
        loop_rows(gather_row, 0, 0)

    @pl.when(i < n_used)
    def _():
        step()

    @pl.when(i == n_used - 1)
    def _():
        slot = i % 2
        wait_scatter(slot)

        @pl.when(i >= 1)
        def _():
            wait_scatter(1 - slot)
        obuf[0] = jnp.zeros((bm * ROW_TILE, LANES), F32)

        def zero_block(blk, c):
            rows = bm * ROW_TILE
            cp = pltpu.make_async_copy(obuf.at[0], y4_hbm.at[pl.ds(pl.multiple_of(blk * rows, rows), rows), :],
                                       ssem.at[0])
            cp.start()
            cp.wait()
            return c
        lax.fori_loop(n_used, last_blk + 1, zero_block, 0)


def _moe(block_e, n_used, inv, h, layer, w1, b1, w2, b2, bm):
    nblk = block_e.shape[0]
    wmap = lambda i, be, nu, iv: (layer, be[i], 0, 0)
    return pl.pallas_call(
        functools.partial(_moe_kernel, bm=bm),
        grid_spec=pltpu.PrefetchScalarGridSpec(
            num_scalar_prefetch=3, grid=(nblk,),
            in_specs=[pl.BlockSpec(memory_space=pl.ANY),
                      pl.BlockSpec((None, None, D_MODEL, 2 * D_FF), wmap),
                      pl.BlockSpec((None, None, 1, 2 * D_FF), wmap),
                      pl.BlockSpec((None, None, D_FF, D_MODEL), wmap),
                      pl.BlockSpec((None, None, 1, D_MODEL), wmap)],
            out_specs=pl.BlockSpec(memory_space=pl.ANY),
            scratch_shapes=[pltpu.VMEM((2, bm * ROW_TILE, LANES), F32), pltpu.VMEM((bm, D_MODEL), BF16),
                            pltpu.VMEM((2, bm * ROW_TILE, LANES), F32),
                            pltpu.VMEM((D_MODEL, 2 * D_FF), BF16), pltpu.VMEM((D_FF, D_MODEL), BF16),
                            pltpu.SemaphoreType.DMA((2,)), pltpu.SemaphoreType.DMA((2,))]),
        out_shape=_sds((nblk * bm * ROW_TILE, LANES)),
        compiler_params=_cparams("arbitrary"), name="moe_experts",
    )(block_e, n_used, inv, h, w1, b1, w2, b2)


def _combine_kernel(gates_ref, y4_ref, h_ref, g_ref, b_ref, out_ref, acc_scr, *, alpha):
    tm = out_ref.shape[0]
    tok0 = pl.program_id(0) * tm

    def group(gi, carry):
        t0 = pl.multiple_of(gi * SUBLANES, SUBLANES)
        tiles = []
        for u in range(SUBLANES):
            t = t0 + u
            tile = alpha * h_ref[pl.ds(pl.multiple_of(t * ROW_TILE, ROW_TILE), ROW_TILE), :]
            for k in range(TOP_K):
                r = (t * TOP_K + k) * ROW_TILE
                tile = tile + (gates_ref[(tok0 + t) * TOP_K + k]
                               * y4_ref[pl.ds(pl.multiple_of(r, ROW_TILE), ROW_TILE), :])
            tiles.append(tile.reshape(1, ROW_TILE, LANES))
        rows = jnp.swapaxes(jnp.concatenate(tiles, axis=0), 0, 1)
        for s in range(ROW_TILE):
            acc_scr[pl.ds(t0, SUBLANES), s * LANES:(s + 1) * LANES] = rows[s]
        return carry
    lax.fori_loop(0, tm // SUBLANES, group, 0)
    out_ref[...] = _ln(acc_scr[...], g_ref[...], b_ref[...])


def _combine(y4, gates_flat, h, n, g, b, alpha):
    tm = _pick(n, (448, 224, 128, 64, 32, 16, 8))
    row = lambda i, gt: (i, 0)
    const = lambda i, gt: (0, 0)
    return pl.pallas_call(
        functools.partial(_combine_kernel, alpha=alpha),
        grid_spec=pltpu.PrefetchScalarGridSpec(
            num_scalar_prefetch=1, grid=(n // tm,),
            in_specs=[pl.BlockSpec((TOP_K * tm * ROW_TILE, LANES), row),
                      pl.BlockSpec((tm * ROW_TILE, LANES), row),
                      pl.BlockSpec((1, D_MODEL), const), pl.BlockSpec((1, D_MODEL), const)],
            out_specs=pl.BlockSpec((tm, D_MODEL), row),
            scratch_shapes=[pltpu.VMEM((tm, D_MODEL), F32)]),
        out_shape=_sds((n, D_MODEL)),
        compiler_params=_cparams("arbitrary"), name="moe_combine",
    )(gates_flat, y4, h, g, b)


def _group_layout(cnt_tiles, nblk, bm):
    cnt = cnt_tiles.astype(jnp.int32)
    counts = jnp.sum(cnt, axis=1)
    padded = (counts + bm - 1) // bm * bm
    pad_end = jnp.cumsum(padded)
    pad_start = pad_end - padded
    base = pad_start[:, None] + jnp.cumsum(cnt, axis=1) - cnt
    starts = jnp.arange(nblk, dtype=jnp.int32) * bm
    groups_before = jnp.sum((pad_end[None, :] <= starts[:, None]).astype(jnp.int32), axis=1)
    block_e = jnp.minimum(groups_before, N_EXPERTS - 1).astype(jnp.int32)
    n_used = (pad_end[-1] // bm).astype(jnp.int32).reshape(1)
    total = jnp.sum(counts)
    real_before = jnp.concatenate([jnp.cumsum(counts), total.reshape(1)])[groups_before]
    pad_shift = (total - real_before).astype(jnp.int32)
    return base.T.astype(F32)[:, :, None], pad_shift, block_e, n_used


def _pack_w_in(w):
    o = 0
    xa, o = w[:, o:o + D_A], o + D_A
    ga, o = w[:, o:o + D_A], o + D_A
    vb, o = w[:, o:o + D_B], o + D_B
    gb, o = w[:, o:o + D_B], o + D_B
    q, o = w[:, o:o + D_QK_C], o + D_QK_C
    k, o = w[:, o:o + D_QK_C], o + D_QK_C
    v, o = w[:, o:o + D_C], o + D_C
    r, o = w[:, o:o + D_C], o + D_C
    z = w[:, o:o + GATE_RANK]
    zq = jnp.zeros((w.shape[0], PC_K - PC_Z - GATE_RANK), w.dtype)
    zk = jnp.zeros((w.shape[0], PC_V - PC_K - D_QK_C), w.dtype)
    return jnp.concatenate([xa, ga, vb, gb, q, z, zq, k, zk, v, r], axis=1).astype(BF16)


def _block_diag(w):
    eye = jnp.eye(N_BLK_A, dtype=w.dtype)
    return jnp.einsum("hij,hg->higj", w, eye).reshape(D_A, D_A)


def kernel(x_prompt, x_sample, state_conv_a, state_rglru, state_conv_b, state_gla, meta_tokens, ln0_g, ln0_b,
           w_in, conv_a_w, conv_a_b, w_rg, b_rg, w_ig, b_ig, lru_lambda, conv_b_w, conv_b_b, ln_b_g, ln_b_b,
           w_gate2, b_gate, gla_norm_g, w_out, ln1_g, ln1_b, router_w, router_b, moe_w1, moe_b1, moe_w2, moe_b2,
           ln2_g, ln2_b):
    bp, seq, _ = x_prompt.shape
    bs, dseq, _ = x_sample.shape
    depth = w_in.shape[0]
    tp = N_META + seq
    ts = SAMPLE_PAD_T
    np_rows = bp * tp
    ns_rows = bs * dseq
    n = np_rows + ns_rows
    alpha = (2 * depth) ** 0.25
    row = lambda a: a.reshape(1, -1)

    meta = jnp.broadcast_to(meta_tokens[None], (bp, N_META, D_MODEL))
    xp_in = jnp.concatenate([meta, x_prompt], axis=1).reshape(np_rows, D_MODEL)
    xs_in = jnp.pad(x_sample, ((0, 0), (0, ts - dseq), (0, 0))).reshape(bs * ts, D_MODEL)
    zeros_p = (jnp.zeros((bp, CONV_A - 1, D_A), F32), jnp.zeros((bp, 1, D_A), F32),
               jnp.zeros((bp, CONV_B - 1, D_B), F32), jnp.zeros((bp, N_HEADS_C, DK_C, DV_C), F32))
    sb = _pick(bs, (8, 4, 2, 1))
    tm_p = _pick(np_rows, (384, 512, 256, 128, 64, 32, 16, 8))
    tm_s = _pick(math.gcd(ns_rows, np_rows), (128, 64, 32, 16, 8))
    nblk = -(-(TOP_K * n) // MOE_BM) + N_EXPERTS
    while (nblk * MOE_BM // TOP_K) % math.lcm(tm_p, tm_s):
        nblk += 1

    new_p = ([], [], [], [])
    new_s = ([], [], [], [])
    x_all = None
    for l in range(depth):
        w_packed = _pack_w_in(w_in[l])
        wg = jnp.concatenate([_block_diag(w_rg[l]), _block_diag(w_ig[l])], axis=1).astype(BF16)
        bg = jnp.concatenate([b_rg[l], b_ig[l]]).reshape(1, -1)
        w_out_b = w_out[l].astype(BF16)
        ng = row(jnp.tile(gla_norm_g[l], N_HEADS_C))
        first = l == 0
        if first:
            pa_p, pb_p, pc_p, xn_p = _inproj(xp_in, 0, np_rows, row(ln0_g), row(ln0_b), w_packed, True)
            pa_s, pb_s, pc_s, xn_s = _inproj(xs_in, 0, bs * ts, row(ln0_g), row(ln0_b), w_packed, True)
            res_p, res_p_row0 = xn_p, 0
        else:
            pa_p, pb_p, pc_p = _inproj(x_all, 0, np_rows, row(ln0_g), row(ln0_b), w_packed, False)
            xn_s = jnp.pad(x_all[np_rows:].reshape(bs, dseq, D_MODEL),
                           ((0, 0), (0, ts - dseq), (0, 0))).reshape(bs * ts, D_MODEL)
            pa_s, pb_s, pc_s = _inproj(xn_s, 0, bs * ts, row(ln0_g), row(ln0_b), w_packed, False)
            res_p, res_p_row0 = x_all, 0

        outs = []
        for (pa, pb, pc, nb, t, tv, bb, st) in (
                (pa_p, pb_p, pc_p, bp, tp, tp, 1, zeros_p),
                (pa_s, pb_s, pc_s, bs, ts, dseq, sb,
                 (state_conv_a[l], state_rglru[l].reshape(bs, 1, D_A), state_conv_b[l], state_gla[l]))):
            ya, ca_new, h_last = _rglru(pa.reshape(nb, t, PA_W), st[0], st[1], conv_a_w[l], row(conv_a_b[l]),
                                        wg, bg, row(lru_lambda[l]), tv, bb)
            yb, cb_new = _convb(pb.reshape(nb, t, PB_W), st[2], conv_b_w[l], row(conv_b_b[l]),
                                row(ln_b_g[l]), row(ln_b_b[l]), tv, bb)
            yc, s_new = _gla(pc, nb, t, st[3], w_gate2[l], row(b_gate[l]), ng, tv, bb)
            outs.append((ya.reshape(nb * t, D_A), yb.reshape(nb * t, D_B), yc,
                         ca_new, h_last.reshape(nb, D_A), cb_new, s_new))
        (ya_p, yb_p, yc_p, *st_p), (ya_s, yb_s, yc_s, *st_s) = outs
        for j in range(4):
            new_p[j].append(st_p[j])
            new_s[j].append(st_s[j])

        valid = lambda a: a.reshape(bs, ts, -1)[:, :dseq].reshape(ns_rows, -1)
        h_all = _outproj(ya_p, yb_p, yc_p, res_p, res_p_row0, w_out_b, row(ln1_g[l]), row(ln1_b[l]), alpha,
                         tm_p, nblk * MOE_BM // TOP_K)
        h_all = _outproj(valid(ya_s), valid(yb_s), valid(yc_s), valid(xn_s), 0, w_out_b, row(ln1_g[l]),
                         row(ln1_b[l]), alpha, tm_s, nblk * MOE_BM // TOP_K, out_row0=np_rows, into=h_all)

        top_idx, gates, cnt = _router(h_all, n, router_w[l].T, router_b[l].reshape(-1, 1))
        base, pad_shift, block_e, n_used = _group_layout(cnt[:, ::LANES], nblk, MOE_BM)
        dest = _dest(top_idx, base)
        inv = _invert(pad_shift, dest, MOE_BM)
        y4 = _moe(block_e, n_used, inv, h_all, l, moe_w1, moe_b1[:, :, None, :], moe_w2, moe_b2[:, :, None, :],
                  MOE_BM)
        x_all = _combine(y4, gates.T.reshape(-1), h_all, n, row(ln2_g[l]), row(ln2_b[l]), alpha)

    y_p = x_all[:np_rows].reshape(bp, tp, D_MODEL)[:, N_META:]
    y_s = x_all[np_rows:].reshape(bs, dseq, D_MODEL)
    return (y_p, y_s,
            jnp.stack(new_p[0]), jnp.stack(new_p[1]), jnp.stack(new_p[2]), jnp.stack(new_p[3]),
            jnp.stack(new_s[0]), jnp.stack(new_s[1]), jnp.stack(new_s[2]), jnp.stack(new_s[3]))
```

```python
import functools
import math

import jax
import jax.numpy as jnp
from jax import lax
from jax.experimental import pallas as pl
from jax.experimental.pallas import tpu as pltpu
from jax.experimental.pallas import tpu_sc as plsc

F32 = jnp.float32
BF16 = jnp.bfloat16

D_MODEL = 1024
N_META = 16
D_A = 384
D_B = 256
D_C = 384
N_BLK_A = 8
BLK_A = D_A // N_BLK_A
CONV_A = 4
RG_C = 8.0
CONV_B = 31
N_HEADS_C = 4
DV_C = D_C // N_HEADS_C
DK_C = DV_C // 2
D_QK_C = N_HEADS_C * DK_C
GATE_RANK = 16
GATE_TAU = 16.0
N_EXPERTS = 32
TOP_K = 4
D_FF = D_MODEL
SWIGLU_LIMIT = 7.0
SWIGLU_ALPHA = 1.702
LN_EPS = 1e-5

PA_W = 2 * D_A
PB_W = 2 * D_B
PC_Q, PC_Z, PC_K, PC_V, PC_R, PC_W = 0, 192, 256, 512, 896, 1280
P_W = PA_W + PB_W + PC_W

SUBLANES = 8
LANES = 128
SC_LANES = 16
SC_CORES, SC_SUBCORES = 2, 16
ROW_TILE = D_MODEL // LANES
VMEM_LIMIT_BYTES = 56 * 1024 * 1024
MOE_BM = 256
FF_CHUNK = 512
DMA_GROUP = 8
W_CAST_ROWS = 128
SAMPLE_PAD_T = 8


def _cparams(*sem):
    return pltpu.CompilerParams(dimension_semantics=sem, vmem_limit_bytes=VMEM_LIMIT_BYTES)


def _sds(shape, dtype=F32):
    return jax.ShapeDtypeStruct(shape, dtype)


def _pick(n, prefs):
    for p in prefs:
        if n % p == 0:
            return p
    raise ValueError(f"no tile for {n} in {prefs}")


def _ln(x, g, b):
    mu = jnp.mean(x, axis=-1, keepdims=True)
    xc = x - mu
    var = jnp.mean(xc * xc, axis=-1, keepdims=True)
    return xc * lax.rsqrt(var + LN_EPS) * g + b


def _sigmoid(x):
    return 1.0 / (1.0 + jnp.exp(-x))


def _split_bf16(x):
    hi = x.astype(BF16)
    lo = (x - hi.astype(F32)).astype(BF16)
    return hi, lo


def _dot(a, b):
    return jnp.dot(a, b, preferred_element_type=F32)


def _inproj_kernel(x_ref, g_ref, b_ref, w_ref, pa_ref, pb_ref, pc_ref, *maybe_xn, apply_ln):
    x = x_ref[...]
    if apply_ln:
        x = _ln(x, g_ref[...], b_ref[...])
        maybe_xn[0][...] = x
    xb = x.astype(BF16)
    pa_ref[...] = _dot(xb, w_ref[:, 0:PA_W])
    pb_ref[...] = _dot(xb, w_ref[:, PA_W:PA_W + PB_W])
    pc_ref[...] = _dot(xb, w_ref[:, PA_W + PB_W:P_W])


def _inproj(x, row0, nrows, ln_g, ln_b, w_packed, apply_ln):
    tm = _pick(nrows, (384, 512, 256, 128, 64, 32, 16, 8))
    while row0 % tm:
        tm //= 2
    off = row0 // tm
    const = lambda i: (0, 0)
    row = lambda i: (i, 0)
    out_shape = [_sds((nrows, PA_W)), _sds((nrows, PB_W)), _sds((nrows, PC_W))]
    out_specs = [pl.BlockSpec((tm, PA_W), row), pl.BlockSpec((tm, PB_W), row), pl.BlockSpec((tm, PC_W), row)]
    if apply_ln:
        out_shape.append(_sds((nrows, D_MODEL)))
        out_specs.append(pl.BlockSpec((tm, D_MODEL), row))
    return pl.pallas_call(
        functools.partial(_inproj_kernel, apply_ln=apply_ln),
        grid=(nrows // tm,),
        in_specs=[pl.BlockSpec((tm, D_MODEL), lambda i: (i + off, 0)),
                  pl.BlockSpec((1, D_MODEL), const), pl.BlockSpec((1, D_MODEL), const),
                  pl.BlockSpec((D_MODEL, P_W), const)],
        out_specs=out_specs, out_shape=out_shape,
        compiler_params=_cparams("parallel"), name="inproj",
    )(x, ln_g, ln_b, w_packed)


def _rglru_kernel(p_ref, cbuf_ref, h0_ref, cw_ref, cb_ref, wg_ref, bg_ref, lam_ref,
                  y_ref, cnew_ref, hlast_ref, xp_scr, a_scr, h_scr, *, T, Tc, Tv, Bb):
    lam = lam_ref[...]
    softplus_neg = jnp.maximum(-lam, 0.0) + jnp.log1p(jnp.exp(-jnp.abs(lam)))
    c_decay = -RG_C * softplus_neg
    cw = cw_ref[...]
    cb = cb_ref[...]
    bg = bg_ref[...]
    sub = lax.broadcasted_iota(jnp.int32, (Tc, D_A), 0) % SUBLANES
    halo = SUBLANES - (CONV_A - 1)
    for b in range(Bb):
        xp_scr[halo:SUBLANES, :] = cbuf_ref[b]
        xp_scr[SUBLANES:SUBLANES + T, :] = p_ref[b, :, 0:D_A]
        cnew_ref[b] = xp_scr[halo + Tv:SUBLANES + Tv, :]

        def chunk(ci, h_b):
            r0 = pl.multiple_of(ci * Tc, SUBLANES)
            win = xp_scr[pl.ds(r0, Tc + SUBLANES), :]
            xc = cb + cw[0:1] * win[halo:halo + Tc]
            for j in range(1, CONV_A):
                xc = xc + cw[j:j + 1] * win[halo + j:halo + j + Tc]
            gates = _dot(xc.astype(BF16), wg_ref[...]) + bg
            r = _sigmoid(gates[:, 0:D_A])
            i = _sigmoid(gates[:, D_A:2 * D_A])
            log_a = c_decay * r
            a = jnp.exp(log_a)
            u = jnp.sqrt(1.0 - a * a) * (i * xc)
            for s in (1, 2, 4):
                keep = sub >= s
                a_prev = pltpu.roll(a, s, 0)
                u_prev = pltpu.roll(u, s, 0)
                u = jnp.where(keep, a * u_prev + u, u)
                a = jnp.where(keep, a * a_prev, a)
            a_scr[...] = a
            h_scr[pl.ds(r0, Tc), :] = u

            def group(gi, h_b):
                c0 = pl.multiple_of(gi * SUBLANES, SUBLANES)
                g0 = pl.multiple_of(r0 + gi * SUBLANES, SUBLANES)
                h8 = a_scr[pl.ds(c0, SUBLANES), :] * h_b + h_scr[pl.ds(g0, SUBLANES), :]
                h_scr[pl.ds(g0, SUBLANES), :] = h8
                return jnp.broadcast_to(h8[SUBLANES - 1:SUBLANES, :], (SUBLANES, D_A))

            h_b = lax.fori_loop(0, Tc // SUBLANES, group, h_b)
            ga = p_ref[b, pl.ds(r0, Tc), D_A:2 * D_A]
            gelu = 0.5 * ga * (1.0 + jnp.tanh(0.7978845608028654 * (ga + 0.044715 * ga * ga * ga)))
            y_ref[b, pl.ds(r0, Tc), :] = h_scr[pl.ds(r0, Tc), :] * gelu
            return h_b

        h_b = jnp.broadcast_to(h0_ref[b], (SUBLANES, D_A))
        lax.fori_loop(0, T // Tc, chunk, h_b)
        hlast_ref[b] = h_scr[Tv - 1:Tv, :]


def _rglru(pa3, cbuf, h0, cw, cb, wg, bg, lam, Tv, Bb):
    B, T, _ = pa3.shape
    Tc = _pick(T, (344, 256, 128, 64, 48, 32, 16, 8))
    const2 = lambda i: (0, 0)
    seq3 = lambda i: (i, 0, 0)
    return pl.pallas_call(
        functools.partial(_rglru_kernel, T=T, Tc=Tc, Tv=Tv, Bb=Bb),
        grid=(B // Bb,),
        in_specs=[pl.BlockSpec((Bb, T, PA_W), seq3), pl.BlockSpec((Bb, CONV_A - 1, D_A), seq3),
                  pl.BlockSpec((Bb, 1, D_A), seq3), pl.BlockSpec((CONV_A, D_A), const2),
                  pl.BlockSpec((1, D_A), const2), pl.BlockSpec((D_A, 2 * D_A), const2),
                  pl.BlockSpec((1, 2 * D_A), const2), pl.BlockSpec((1, D_A), const2)],
        out_specs=[pl.BlockSpec((Bb, T, D_A), seq3), pl.BlockSpec((Bb, CONV_A - 1, D_A), seq3),
                   pl.BlockSpec((Bb, 1, D_A), seq3)],
        out_shape=[_sds((B, T, D_A)), _sds((B, CONV_A - 1, D_A)), _sds((B, 1, D_A))],
        scratch_shapes=[pltpu.VMEM((T + 2 * SUBLANES, D_A), F32), pltpu.VMEM((Tc, D_A), F32),
                        pltpu.VMEM((T, D_A), F32)],
        compiler_params=_cparams("parallel"), name="rglru",
    )(pa3, cbuf, h0, cw, cb, wg, bg, lam)


B_HALO = 32


def _convb_kernel(p_ref, buf_ref, w_ref, cb_ref, g_ref, b_ref, y_ref, bnew_ref, u_scr, *, T, Tc, Tv, Bb):
    w = w_ref[...]
    cb = cb_ref[...]
    g = g_ref[...]
    bb = b_ref[...]
    first = B_HALO - (CONV_B - 1)
    for b in range(Bb):
        u_scr[0:first, :] = jnp.zeros((first, D_B), F32)
        u_scr[first:B_HALO, :] = buf_ref[b]
        u_scr[B_HALO:B_HALO + T, :] = p_ref[b, :, 0:D_B] * _sigmoid(p_ref[b, :, D_B:2 * D_B])
        bnew_ref[b] = u_scr[first + Tv:B_HALO + Tv, :]

        def chunk(ci, carry):
            r0 = pl.multiple_of(ci * Tc, SUBLANES)
            win = u_scr[pl.ds(r0, Tc + B_HALO), :]
            shifted = [win] + [pltpu.roll(win, Tc + B_HALO - s, 0) for s in range(1, SUBLANES)]
            acc = cb
            for j in range(CONV_B):
                a, s = divmod(first + j, SUBLANES)
                acc = acc + w[j:j + 1] * shifted[s][a * SUBLANES:a * SUBLANES + Tc]
            yn = _ln(acc, g, bb)
            y_ref[b, pl.ds(r0, Tc), :] = yn * _sigmoid(yn)
            return carry

        lax.fori_loop(0, T // Tc, chunk, 0)


def _convb(pb3, buf, w, cb, g, b, Tv, Bb):
    B, T, _ = pb3.shape
    Tc = _pick(T, (48, 32, 16, 8))
    const2 = lambda i: (0, 0)
    seq3 = lambda i: (i, 0, 0)
    return pl.pallas_call(
        functools.partial(_convb_kernel, T=T, Tc=Tc, Tv=Tv, Bb=Bb),
        grid=(B // Bb,),
        in_specs=[pl.BlockSpec((Bb, T, PB_W), seq3), pl.BlockSpec((Bb, CONV_B - 1, D_B), seq3),
                  pl.BlockSpec((CONV_B, D_B), const2), pl.BlockSpec((1, D_B), const2),
                  pl.BlockSpec((1, D_B), const2), pl.BlockSpec((1, D_B), const2)],
        out_specs=[pl.BlockSpec((Bb, T, D_B), seq3), pl.BlockSpec((Bb, CONV_B - 1, D_B), seq3)],
        out_shape=[_sds((B, T, D_B)), _sds((B, CONV_B - 1, D_B))],
        scratch_shapes=[pltpu.VMEM((T + B_HALO, D_B), F32)],
        compiler_params=_cparams("parallel"), name="convb",
    )(pb3, buf, w, cb, g, b)


def _gla_kernel(p_ref, s0_ref, wg2_ref, bgate_ref, ng_ref, y_ref, snew_ref, s_scr, g_scr, *, T, C, Tb, Tv, Bb):
    ri = lax.broadcasted_iota(jnp.int32, (C, C), 0)
    ci_ = lax.broadcasted_iota(jnp.int32, (C, C), 1)
    tril = ri >= ci_
    lane_k = lax.broadcasted_iota(jnp.int32, (1, D_QK_C), 1)
    lane_v = lax.broadcasted_iota(jnp.int32, (1, D_C), 1)
    hm_k = [(lane_k >= h * DK_C) & (lane_k < (h + 1) * DK_C) for h in range(N_HEADS_C)]
    hm_v = [(lane_v >= h * DV_C) & (lane_v < (h + 1) * DV_C) for h in range(N_HEADS_C)]
    rs = lax.broadcasted_iota(jnp.int32, (D_C, D_QK_C), 0)
    cs = lax.broadcasted_iota(jnp.int32, (D_C, D_QK_C), 1)
    bd_t = (rs >= 0) & (rs < 0)
    for h in range(N_HEADS_C):
        bd_t = bd_t | ((rs >= h * DV_C) & (rs < (h + 1) * DV_C) & (cs >= h * DK_C) & (cs < (h + 1) * DK_C))
    rm = lax.broadcasted_iota(jnp.int32, (D_C, D_C), 0)
    cm = lax.broadcasted_iota(jnp.int32, (D_C, D_C), 1)
    seg = (rm >= 0) & (rm < 0)
    for h in range(N_HEADS_C):
        seg = seg | ((rm >= h * DV_C) & (rm < (h + 1) * DV_C) & (cm >= h * DV_C) & (cm < (h + 1) * DV_C))
    mseg = jnp.where(seg, 1.0, 0.0).astype(BF16)
    wg2 = wg2_ref[...].astype(BF16)
    bgate = bgate_ref[...]
    ng = ng_ref[...]
    rowi = lax.broadcasted_iota(jnp.int32, (C, 1), 0)
    tdims = (((0,), (0,)), ((), ()))

    n_chunks = T // C
    nt_dims = (((1,), (1,)), ((), ()))
    tril4 = jnp.concatenate([tril] * N_HEADS_C, axis=0)
    scan_shifts = [s for s in (1, 2, 4, 8, 16, 32) if s < C]
    scan_keep = [rowi >= s for s in scan_shifts]

    def tile_rows(ti, tb, base=0):
        r0 = ti * tb
        return pl.ds(base + (r0 if isinstance(ti, int) else pl.multiple_of(r0, tb)), tb)

    def gates(ti, carry):
        rows = tile_rows(ti, Tb)
        z = p_ref[rows, PC_Z:PC_Z + GATE_RANK]
        pre = _dot(z.astype(BF16), wg2) + bgate
        g = (jnp.minimum(pre, 0.0) - jnp.log1p(jnp.exp(-jnp.abs(pre)))) * (1.0 / GATE_TAU)
        rid = ti * Tb + lax.broadcasted_iota(jnp.int32, (Tb, 1), 0)
        if Bb > 1:
            rid = rid & (T - 1)
        g_scr[rows, :] = jnp.where(rid < Tv, g, 0.0)
        return carry
    lax.fori_loop(0, Bb * T // Tb, gates, 0)

    for b in range(Bb):
        def chunk(ci, carry):
            rows = tile_rows(ci, C, b * T)
            q = p_ref[rows, PC_Q:PC_Q + D_QK_C] * (DK_C ** -0.5)
            k = p_ref[rows, PC_K:PC_K + D_QK_C]
            v = p_ref[rows, PC_V:PC_V + D_C]
            k = jnp.where((ci * C + rowi) < Tv, k, 0.0)
            gcum = g_scr[rows, :]
            for s, keep in zip(scan_shifts, scan_keep):
                gcum = gcum + jnp.where(keep, pltpu.roll(gcum, s, 0), 0.0)
            g_last = gcum[C - 1:C, :]
            g_mid = gcum[C // 2 - 1:C // 2, :]
            vb = v.astype(BF16)
            qt = q * jnp.exp(gcum - g_mid)
            ktb = (k * jnp.exp(g_mid - gcum)).astype(BF16)
            q4 = jnp.concatenate([jnp.where(hm_k[h], qt, 0.0) for h in range(N_HEADS_C)], axis=0).astype(BF16)
            sc = lax.dot_general(q4, ktb, nt_dims, preferred_element_type=F32)
            r4 = _dot(jnp.where(tril4, sc, 0.0).astype(BF16), vb)
            o = jnp.where(hm_v[0], r4[0:C], 0.0)
            for h in range(1, N_HEADS_C):
                o = o + jnp.where(hm_v[h], r4[h * C:(h + 1) * C], 0.0)
            kd = (k * jnp.exp(g_last - gcum)).astype(BF16)
            upd_t = lax.dot_general(vb, kd, tdims, preferred_element_type=F32)
            s_in = s_scr[...]
            o = o + lax.dot_general((q * jnp.exp(gcum)).astype(BF16), s_in.astype(BF16), nt_dims,
                                    preferred_element_type=F32)
            s_scr[...] = s_in * jnp.exp(g_last) + jnp.where(bd_t, upd_t, 0.0)
            y_ref[rows, :] = o
            return carry

        s_scr[...] = jnp.zeros((D_C, D_QK_C), F32)
        for h in range(N_HEADS_C):
            s_scr[h * DV_C:(h + 1) * DV_C, h * DK_C:(h + 1) * DK_C] = s0_ref[b, h]
        lax.fori_loop(0, n_chunks, chunk, 0)
        for h in range(N_HEADS_C):
            snew_ref[b, h] = s_scr[h * DV_C:(h + 1) * DV_C, h * DK_C:(h + 1) * DK_C]

    def finish(ti, carry):
        rows = tile_rows(ti, Tb)
        o = y_ref[rows, :]
        rg = p_ref[rows, PC_R:PC_R + D_C]
        o2_hi, o2_lo = _split_bf16(o * o)
        ms = (_dot(o2_hi, mseg) + _dot(o2_lo, mseg)) * (1.0 / DV_C)
        y_ref[rows, :] = o * lax.rsqrt(ms + LN_EPS) * ng * (rg * _sigmoid(rg))
        return carry
    lax.fori_loop(0, Bb * T // Tb, finish, 0)


def _gla(pc, B, T, s0, wg2, bgate, ng, Tv, Bb):
    assert Bb == 1 or T & (T - 1) == 0
    C = _pick(T, (48, 32, 16, 8))
    Tb = _pick(Bb * T, (344, 256, 128, 64, 48, 32, 16, 8))
    const2 = lambda i: (0, 0)
    row2 = lambda i: (i, 0)
    seq4 = lambda i: (i, 0, 0, 0)
    st = (Bb, N_HEADS_C, DV_C, DK_C)
    y, s_new_t = pl.pallas_call(
        functools.partial(_gla_kernel, T=T, C=C, Tb=Tb, Tv=Tv, Bb=Bb),
        grid=(B // Bb,),
        in_specs=[pl.BlockSpec((Bb * T, PC_W), row2), pl.BlockSpec(st, seq4),
                  pl.BlockSpec((GATE_RANK, D_QK_C), const2), pl.BlockSpec((1, D_QK_C), const2),
                  pl.BlockSpec((1, D_C), const2)],
        out_specs=[pl.BlockSpec((Bb * T, D_C), row2), pl.BlockSpec(st, seq4)],
        out_shape=[_sds((B * T, D_C)), _sds((B, N_HEADS_C, DV_C, DK_C))],
        scratch_shapes=[pltpu.VMEM((D_C, D_QK_C), F32), pltpu.VMEM((Bb * T, D_QK_C), F32)],
        compiler_params=_cparams("parallel"), name="gla",
    )(pc, jnp.swapaxes(s0, 2, 3), wg2, bgate, ng)
    return y, jnp.swapaxes(s_new_t, 2, 3)


def _tiles_to_rows(ref, lead, rows):
    return jnp.concatenate([ref[lead + (pl.ds(s, rows, stride=ROW_TILE), slice(None))]
                            for s in range(ROW_TILE)], axis=1)


def _rows_to_tiles(ref, lead, rows, val):
    for s in range(ROW_TILE):
        ref[lead + (pl.ds(s, rows, stride=ROW_TILE), slice(None))] = val[:, s * LANES:(s + 1) * LANES]


def _outproj_kernel(ya_ref, yb_ref, yc_ref, x_ref, w_ref, g_ref, b_ref, *rest, alpha, n_blocks):
    h_ref = rest[-1]

    @pl.when(pl.program_id(0) < n_blocks)
    def _():
        y_cat = jnp.concatenate([ya_ref[...].astype(BF16), yb_ref[...].astype(BF16), yc_ref[...].astype(BF16)],
                                axis=1)
        y = _dot(y_cat, w_ref[...])
        h = _ln(alpha * x_ref[...] + y, g_ref[...], b_ref[...])
        _rows_to_tiles(h_ref, (), h.shape[0], h)

    @pl.when(pl.program_id(0) >= n_blocks)
    def _():
        h_ref[...] = jnp.zeros(h_ref.shape, F32)


def _outproj(ya, yb, yc, x, row0, w, g, b, alpha, tm, out_rows, out_row0=0, into=None):
    n = ya.shape[0]
    nb = n // tm
    assert n % tm == 0 and row0 % tm == 0 and out_row0 % tm == 0 and out_rows % tm == 0
    off, out_off = row0 // tm, out_row0 // tm
    grid = nb if into is not None else out_rows // tm
    row = lambda i: (jnp.minimum(i, nb - 1), 0)
    const = lambda i: (0, 0)
    in_specs = [pl.BlockSpec((tm, D_A), row), pl.BlockSpec((tm, D_B), row), pl.BlockSpec((tm, D_C), row),
                pl.BlockSpec((tm, D_MODEL), lambda i: (jnp.minimum(i, nb - 1) + off, 0)),
                pl.BlockSpec((D_MODEL, D_MODEL), const), pl.BlockSpec((1, D_MODEL), const),
                pl.BlockSpec((1, D_MODEL), const)]
    args = [ya, yb, yc, x, w, g, b]
    aliases = {}
    if into is not None:
        in_specs.append(pl.BlockSpec(memory_space=pl.ANY))
        args.append(into)
        aliases = {len(args) - 1: 0}
    return pl.pallas_call(
        functools.partial(_outproj_kernel, alpha=alpha, n_blocks=nb),
        grid=(grid,),
        in_specs=in_specs,
        out_specs=pl.BlockSpec((tm * ROW_TILE, LANES), lambda i: (i + out_off, 0)),
        out_shape=_sds((out_rows * ROW_TILE, LANES)),
        input_output_aliases=aliases,
        compiler_params=_cparams("arbitrary"), name="outproj",
    )(*args)


def _router_kernel(h_ref, wt_ref, b_ref, idx_ref, gate_ref, cnt_ref):
    nt = (((1,), (1,)), ((), ()))
    hh, hl = _split_bf16(_tiles_to_rows(h_ref, (), idx_ref.shape[1]))
    wh, wl = _split_bf16(wt_ref[...])
    logits = (lax.dot_general(wh, hh, nt, preferred_element_type=F32)
              + lax.dot_general(wh, hl, nt, preferred_element_type=F32)
              + lax.dot_general(wl, hh, nt, preferred_element_type=F32)) + b_ref[...]
    eid = lax.broadcasted_iota(jnp.int32, logits.shape, 0)
    vals = []
    member = jnp.zeros(logits.shape, F32)
    for k in range(TOP_K):
        m = jnp.max(logits, axis=0, keepdims=True)
        sel = jnp.min(jnp.where(logits == m, eid, N_EXPERTS), axis=0, keepdims=True)
        idx_ref[k:k + 1, :] = sel
        vals.append(m)
        hit = eid == sel
        member = jnp.where(hit, 1.0, member)
        logits = jnp.where(hit, -jnp.inf, logits)
    es = [jnp.exp(v - vals[0]) for v in vals]
    tot = es[0] + es[1] + es[2] + es[3]
    for k in range(TOP_K):
        gate_ref[k:k + 1, :] = es[k] / tot
    cnt_ref[...] = jnp.broadcast_to(jnp.sum(member, axis=1, keepdims=True), cnt_ref.shape)


def _router_tile(n):
    return _pick(n, (896, 640, 512, 384, 256, 128))


def _router(h, n, wt, b):
    tm = _router_tile(n)
    nt = n // tm
    return pl.pallas_call(
        _router_kernel,
        grid=(nt,),
        in_specs=[pl.BlockSpec((tm * ROW_TILE, LANES), lambda i: (i, 0)),
                  pl.BlockSpec((N_EXPERTS, D_MODEL), lambda i: (0, 0)),
                  pl.BlockSpec((N_EXPERTS, 1), lambda i: (0, 0))],
        out_specs=[pl.BlockSpec((TOP_K, tm), lambda i: (0, i)), pl.BlockSpec((TOP_K, tm), lambda i: (0, i)),
                   pl.BlockSpec((N_EXPERTS, LANES), lambda i: (0, i))],
        out_shape=[_sds((TOP_K, n), jnp.int32), _sds((TOP_K, n)), _sds((N_EXPERTS, nt * LANES))],
        compiler_params=_cparams("parallel"), name="router",
    )(h, wt, b)


def _dest_kernel(idx_ref, base_ref, dest_ref):
    tm = idx_ref.shape[1]
    eid = lax.broadcasted_iota(jnp.int32, (N_EXPERTS, tm), 0)
    hits = [eid == idx_ref[k:k + 1, :] for k in range(TOP_K)]
    member = jnp.zeros((N_EXPERTS, tm), F32)
    for k in range(TOP_K):
        member = jnp.where(hits[k], 1.0, member)
    earlier = (lax.broadcasted_iota(jnp.int32, (tm, tm), 0) < lax.broadcasted_iota(jnp.int32, (tm, tm), 1))
    rank = _dot(member.astype(BF16), jnp.where(earlier, 1.0, 0.0).astype(BF16))
    pos = base_ref[...] + rank
    for k in range(TOP_K):
        dest_ref[k:k + 1, :] = jnp.sum(jnp.where(hits[k], pos, 0.0), axis=0, keepdims=True).astype(jnp.int32)


def _dest(top_idx, base):
    n = top_idx.shape[1]
    tm = _router_tile(n)
    return pl.pallas_call(
        _dest_kernel,
        grid=(n // tm,),
        in_specs=[pl.BlockSpec((TOP_K, tm), lambda i: (0, i)),
                  pl.BlockSpec((None, N_EXPERTS, 1), lambda i: (i, 0, 0))],
        out_specs=pl.BlockSpec((TOP_K, tm), lambda i: (0, i)),
        out_shape=_sds((TOP_K, n), jnp.int32),
        compiler_params=_cparams("parallel"), name="moe_dest",
    )(top_idx, base)


def _invert(pad_shift, dest, bm):
    k, n = dest.shape
    n_rows = pad_shift.shape[0] * bm
    ch = next(c for c in range(min(n, 4096) // SC_LANES * SC_LANES, 0, -SC_LANES) if n % c == 0)
    fill = (jnp.arange(n_rows, dtype=jnp.int32).reshape(-1, bm) + pad_shift[:, None]).reshape(-1)
    mesh = plsc.VectorSubcoreMesh(core_axis_name="c", subcore_axis_name="s", num_cores=SC_CORES,
                                  num_subcores=SC_SUBCORES)

    @functools.partial(
        pl.kernel, mesh=mesh, out_type=_sds((n_rows,), jnp.int32),
        scratch_types=[pltpu.VMEM((n_rows,), jnp.int32), pltpu.VMEM((ch,), jnp.int32)],
        compiler_params=pltpu.CompilerParams(needs_layout_passes=False), name="moe_invert_sc")
    def run(fill_hbm, dest_hbm, inv_hbm, inv_v, d_v):
        first = jnp.logical_and(lax.axis_index("c") == 0, lax.axis_index("s") == 0)

        @pl.when(first)
        def _():
            pltpu.sync_copy(fill_hbm, inv_v)
            lanes = lax.iota(jnp.int32, SC_LANES)
            for slot in range(k):
                def chunk(c, carry):
                    pltpu.sync_copy(dest_hbm.at[pl.ds(slot * n + c * ch, ch)], d_v)

                    def vec(v, carry):
                        idx = d_v[pl.ds(v * SC_LANES, SC_LANES)]
                        plsc.store_scatter(inv_v, [idx], (c * ch + v * SC_LANES + lanes) * k + slot)
                        return carry
                    return lax.fori_loop(0, ch // SC_LANES, vec, carry)
                lax.fori_loop(0, n // ch, chunk, 0)
            pltpu.sync_copy(inv_v, inv_hbm)

    return run(fill, dest.reshape(-1))


def _moe_kernel(be_ref, nu_ref, inv_ref, h_hbm, w1f_ref, b1_ref, w2f_ref, b2_ref, y4_hbm,
                xbuf, xb, obuf, w1_ref, w2_ref, gsem, ssem, *, bm):
    i = pl.program_id(0)
    n_used = nu_ref[0]
    last_blk = pl.num_programs(0) - 1
    n_ff = D_FF // FF_CHUNK

    def tile(r):
        start = r * ROW_TILE
        return pl.ds(start if isinstance(r, int) else pl.multiple_of(start, ROW_TILE), ROW_TILE)

    def gather_row(blk, slot, j, r=None):
        r = inv_ref[blk * bm + j] if r is None else r
        t = lax.shift_right_logical(r, 2)
        pltpu.make_async_copy(h_hbm.at[tile(t), :], xbuf.at[slot, tile(j), :], gsem.at[slot]).start()

    def scatter_row(blk, slot, j, r=None):
        r = inv_ref[blk * bm + j] if r is None else r
        pltpu.make_async_copy(obuf.at[slot, tile(j), :], y4_hbm.at[tile(r), :], ssem.at[slot]).start()

    def wait_gather(slot):
        pltpu.make_async_copy(h_hbm.at[pl.ds(0, bm * ROW_TILE), :], xbuf.at[slot], gsem.at[slot]).wait()

    def wait_scatter(slot):
        pltpu.make_async_copy(obuf.at[slot], y4_hbm.at[pl.ds(0, bm * ROW_TILE), :], ssem.at[slot]).wait()

    def loop_rows(fn, blk, slot):
        def body(j, c):
            fn(blk, slot, j)
            return c
        lax.fori_loop(0, bm, body, 0)

    def issue_rows(fn, blk, slot, j_lo, j_hi):
        for j0 in range(j_lo, j_hi, DMA_GROUP):
            js = range(j0, j0 + DMA_GROUP)
            ids = [inv_ref[blk * bm + j] for j in js]
            for j, r in zip(js, ids):
                fn(blk, slot, j, r)

    def step():
        slot = i % 2

        @pl.when(i + 1 < n_used)
        def _():
            issue_rows(gather_row, i + 1, 1 - slot, 0, bm)
        wait_gather(slot)
        xb[...] = _tiles_to_rows(xbuf, (slot,), bm).astype(BF16)
        acc = jnp.zeros((bm, D_MODEL), F32)
        for c in range(n_ff):
            lo = c * FF_CHUNK
            x = xb[...]
            hg = _dot(x, w1_ref[:, lo:lo + FF_CHUNK]) + b1_ref[:, lo:lo + FF_CHUNK]
            hl = _dot(x, w1_ref[:, D_FF + lo:D_FF + lo + FF_CHUNK]) + b1_ref[:, D_FF + lo:D_FF + lo + FF_CHUNK]
            gate = jnp.minimum(hg, SWIGLU_LIMIT)
            lin = jnp.clip(hl, -SWIGLU_LIMIT, SWIGLU_LIMIT)
            act = gate * _sigmoid(SWIGLU_ALPHA * gate) * (lin + 1.0)
            acc = acc + _dot(act.astype(BF16), w2_ref[lo:lo + FF_CHUNK, :])

        @pl.when(i >= 2)
        def _():
            wait_scatter(slot)
        _rows_to_tiles(obuf, (slot,), bm, acc + b2_ref[...])
        issue_rows(scatter_row, i, slot, 0, bm)

    new_expert = jnp.logical_or(i == 0, be_ref[i] != be_ref[jnp.maximum(i - 1, 0)])

    @pl.when(jnp.logical_and(new_expert, i < n_used))
    def _():
        def cast_rows(r, c):
            rows = pl.ds(pl.multiple_of(r * W_CAST_ROWS, W_CAST_ROWS), W_CAST_ROWS)
            w1_ref[rows, :] = w1f_ref[rows, :].astype(BF16)
            w2_ref[rows, :] = w2f_ref[rows, :].astype(BF16)
            return c
        lax.fori_loop(0, D_MODEL // W_CAST_ROWS, cast_rows, 0)

    @pl.when(jnp.logical_and(i == 0, n_used > 0))
    def _():
        loop_rows(gather_row, 0, 0)

    @pl.when(i < n_used)
    def _():
        step()

    @pl.when(i == n_used - 1)
    def _():
        slot = i % 2
        wait_scatter(slot)

        @pl.when(i >= 1)
        def _():
            wait_scatter(1 - slot)
        obuf[0] = jnp.zeros((bm * ROW_TILE, LANES), F32)

        def zero_block(blk, c):
            rows = bm * ROW_TILE
            cp = pltpu.make_async_copy(obuf.at[0], y4_hbm.at[pl.ds(pl.multiple_of(blk * rows, rows), rows), :],
                                       ssem.at[0])
            cp.start()
            cp.wait()
            return c
        lax.fori_loop(n_used, last_blk + 1, zero_block, 0)


def _moe(block_e, n_used, inv, h, layer, w1, b1, w2, b2, bm):
    nblk = block_e.shape[0]
    wmap = lambda i, be, nu, iv: (layer, be[i], 0, 0)
    return pl.pallas_call(
        functools.partial(_moe_kernel, bm=bm),
        grid_spec=pltpu.PrefetchScalarGridSpec(
            num_scalar_prefetch=3, grid=(nblk,),
            in_specs=[pl.BlockSpec(memory_space=pl.ANY),
                      pl.BlockSpec((None, None, D_MODEL, 2 * D_FF), wmap),
                      pl.BlockSpec((None, None, 1, 2 * D_FF), wmap),
                      pl.BlockSpec((None, None, D_FF, D_MODEL), wmap),
                      pl.BlockSpec((None, None, 1, D_MODEL), wmap)],
            out_specs=pl.BlockSpec(memory_space=pl.ANY),
            scratch_shapes=[pltpu.VMEM((2, bm * ROW_TILE, LANES), F32), pltpu.VMEM((bm, D_MODEL), BF16),
                            pltpu.VMEM((2, bm * ROW_TILE, LANES), F32),
                            pltpu.VMEM((D_MODEL, 2 * D_FF), BF16), pltpu.VMEM((D_FF, D_MODEL), BF16),
                            pltpu.SemaphoreType.DMA((2,)), pltpu.SemaphoreType.DMA((2,))]),
        out_shape=_sds((nblk * bm * ROW_TILE, LANES)),
        compiler_params=_cparams("arbitrary"), name="moe_experts",
    )(block_e, n_used, inv, h, w1, b1, w2, b2)


def _combine_kernel(gates_ref, y4_ref, h_ref, g_ref, b_ref, out_ref, acc_scr, *, alpha):
    tm = out_ref.shape[0]
    tok0 = pl.program_id(0) * tm

    def group(gi, carry):
        t0 = pl.multiple_of(gi * SUBLANES, SUBLANES)
        tiles = []
        for u in range(SUBLANES):
            t = t0 + u
            tile = alpha * h_ref[pl.ds(pl.multiple_of(t * ROW_TILE, ROW_TILE), ROW_TILE), :]
            for k in range(TOP_K):
                r = (t * TOP_K + k) * ROW_TILE
                tile = tile + (gates_ref[(tok0 + t) * TOP_K + k]
                               * y4_ref[pl.ds(pl.multiple_of(r, ROW_TILE), ROW_TILE), :])
            tiles.append(tile.reshape(1, ROW_TILE, LANES))
        rows = jnp.swapaxes(jnp.concatenate(tiles, axis=0), 0, 1)
        for s in range(ROW_TILE):
            acc_scr[pl.ds(t0, SUBLANES), s * LANES:(s + 1) * LANES] = rows[s]
        return carry
    lax.fori_loop(0, tm // SUBLANES, group, 0)
    out_ref[...] = _ln(acc_scr[...], g_ref[...], b_ref[...])


def _combine(y4, gates_flat, h, n, g, b, alpha):
    tm = _pick(n, (448, 224, 128, 64, 32, 16, 8))
    row = lambda i, gt: (i, 0)
    const = lambda i, gt: (0, 0)
    return pl.pallas_call(
        functools.partial(_combine_kernel, alpha=alpha),
        grid_spec=pltpu.PrefetchScalarGridSpec(
            num_scalar_prefetch=1, grid=(n // tm,),
            in_specs=[pl.BlockSpec((TOP_K * tm * ROW_TILE, LANES), row),
                      pl.BlockSpec((tm * ROW_TILE, LANES), row),
                      pl.BlockSpec((1, D_MODEL), const), pl.BlockSpec((1, D_MODEL), const)],
            out_specs=pl.BlockSpec((tm, D_MODEL), row),
            scratch_shapes=[pltpu.VMEM((tm, D_MODEL), F32)]),
        out_shape=_sds((n, D_MODEL)),
        compiler_params=_cparams("arbitrary"), name="moe_combine",
    )(gates_flat, y4, h, g, b)


def _group_layout(cnt_tiles, nblk, bm):
    cnt = cnt_tiles.astype(jnp.int32)
    counts = jnp.sum(cnt, axis=1)
    padded = (counts + bm - 1) // bm * bm
    pad_end = jnp.cumsum(padded)
    pad_start = pad_end - padded
    base = pad_start[:, None] + jnp.cumsum(cnt, axis=1) - cnt
    starts = jnp.arange(nblk, dtype=jnp.int32) * bm
    groups_before = jnp.sum((pad_end[None, :] <= starts[:, None]).astype(jnp.int32), axis=1)
    block_e = jnp.minimum(groups_before, N_EXPERTS - 1).astype(jnp.int32)
    n_used = (pad_end[-1] // bm).astype(jnp.int32).reshape(1)
    total = jnp.sum(counts)
    real_before = jnp.concatenate([jnp.cumsum(counts), total.reshape(1)])[groups_before]
    pad_shift = (total - real_before).astype(jnp.int32)
    return base.T.astype(F32)[:, :, None], pad_shift, block_e, n_used


def _pack_w_in(w):
    o = 0
    xa, o = w[:, o:o + D_A], o + D_A
    ga, o = w[:, o:o + D_A], o + D_A
    vb, o = w[:, o:o + D_B], o + D_B
    gb, o = w[:, o:o + D_B], o + D_B
    q, o = w[:, o:o + D_QK_C], o + D_QK_C
    k, o = w[:, o:o + D_QK_C], o + D_QK_C
    v, o = w[:, o:o + D_C], o + D_C
    r, o = w[:, o:o + D_C], o + D_C
    z = w[:, o:o + GATE_RANK]
    zq = jnp.zeros((w.shape[0], PC_K - PC_Z - GATE_RANK), w.dtype)
    zk = jnp.zeros((w.shape[0], PC_V - PC_K - D_QK_C), w.dtype)
    return jnp.concatenate([xa, ga, vb, gb, q, z, zq, k, zk, v, r], axis=1).astype(BF16)


def _block_diag(w):
    eye = jnp.eye(N_BLK_A, dtype=w.dtype)
    return jnp.einsum("hij,hg->higj", w, eye).reshape(D_A, D_A)


def kernel(x_prompt, x_sample, state_conv_a, state_rglru, state_conv_b, state_gla, meta_tokens, ln0_g, ln0_b,
           w_in, conv_a_w, conv_a_b, w_rg, b_rg, w_ig, b_ig, lru_lambda, conv_b_w, conv_b_b, ln_b_g, ln_b_b,
           w_gate2, b_gate, gla_norm_g, w_out, ln1_g, ln1_b, router_w, router_b, moe_w1, moe_b1, moe_w2, moe_b2,
           ln2_g, ln2_b):
    bp, seq, _ = x_prompt.shape
    bs, dseq, _ = x_sample.shape
    depth = w_in.shape[0]
    tp = N_META + seq
    ts = SAMPLE_PAD_T
    np_rows = bp * tp
    ns_rows = bs * dseq
    n = np_rows + ns_rows
    alpha = (2 * depth) ** 0.25
    row = lambda a: a.reshape(1, -1)

    meta = jnp.broadcast_to(meta_tokens[None], (bp, N_META, D_MODEL))
    xp_in = jnp.concatenate([meta, x_prompt], axis=1).reshape(np_rows, D_MODEL)
    xs_in = jnp.pad(x_sample, ((0, 0), (0, ts - dseq), (0, 0))).reshape(bs * ts, D_MODEL)
    zeros_p = (jnp.zeros((bp, CONV_A - 1, D_A), F32), jnp.zeros((bp, 1, D_A), F32),
               jnp.zeros((bp, CONV_B - 1, D_B), F32), jnp.zeros((bp, N_HEADS_C, DK_C, DV_C), F32))
    sb = _pick(bs, (8, 4, 2, 1))
    tm_p = _pick(np_rows, (384, 512, 256, 128, 64, 32, 16, 8))
    tm_s = _pick(math.gcd(ns_rows, np_rows), (128, 64, 32, 16, 8))
    nblk = -(-(TOP_K * n) // MOE_BM) + N_EXPERTS
    while (nblk * MOE_BM // TOP_K) % math.lcm(tm_p, tm_s):
        nblk += 1

    new_p = ([], [], [], [])
    new_s = ([], [], [], [])
    x_all = None
    for l in range(depth):
        w_packed = _pack_w_in(w_in[l])
        wg = jnp.concatenate([_block_diag(w_rg[l]), _block_diag(w_ig[l])], axis=1).astype(BF16)
        bg = jnp.concatenate([b_rg[l], b_ig[l]]).reshape(1, -1)
        w_out_b = w_out[l].astype(BF16)
        ng = row(jnp.tile(gla_norm_g[l], N_HEADS_C))
        first = l == 0
        if first:
            pa_p, pb_p, pc_p, xn_p = _inproj(xp_in, 0, np_rows, row(ln0_g), row(ln0_b), w_packed, True)
            pa_s, pb_s, pc_s, xn_s = _inproj(xs_in, 0, bs * ts, row(ln0_g), row(ln0_b), w_packed, True)
            res_p, res_p_row0 = xn_p, 0
        else:
            pa_p, pb_p, pc_p = _inproj(x_all, 0, np_rows, row(ln0_g), row(ln0_b), w_packed, False)
            xn_s = jnp.pad(x_all[np_rows:].reshape(bs, dseq, D_MODEL),
                           ((0, 0), (0, ts - dseq), (0, 0))).reshape(bs * ts, D_MODEL)
            pa_s, pb_s, pc_s = _inproj(xn_s, 0, bs * ts, row(ln0_g), row(ln0_b), w_packed, False)
            res_p, res_p_row0 = x_all, 0

        outs = []
        for (pa, pb, pc, nb, t, tv, bb, st) in (
                (pa_p, pb_p, pc_p, bp, tp, tp, 1, zeros_p),
                (pa_s, pb_s, pc_s, bs, ts, dseq, sb,
                 (state_conv_a[l], state_rglru[l].reshape(bs, 1, D_A), state_conv_b[l], state_gla[l]))):
            ya, ca_new, h_last = _rglru(pa.reshape(nb, t, PA_W), st[0], st[1], conv_a_w[l], row(conv_a_b[l]),
                                        wg, bg, row(lru_lambda[l]), tv, bb)
            yb, cb_new = _convb(pb.reshape(nb, t, PB_W), st[2], conv_b_w[l], row(conv_b_b[l]),
                                row(ln_b_g[l]), row(ln_b_b[l]), tv, bb)
            yc, s_new = _gla(pc, nb, t, st[3], w_gate2[l], row(b_gate[l]), ng, tv, bb)
            outs.append((ya.reshape(nb * t, D_A), yb.reshape(nb * t, D_B), yc,
                         ca_new, h_last.reshape(nb, D_A), cb_new, s_new))
        (ya_p, yb_p, yc_p, *st_p), (ya_s, yb_s, yc_s, *st_s) = outs
        for j in range(4):
            new_p[j].append(st_p[j])
            new_s[j].append(st_s[j])

        valid = lambda a: a.reshape(bs, ts, -1)[:, :dseq].reshape(ns_rows, -1)
        h_all = _outproj(ya_p, yb_p, yc_p, res_p, res_p_row0, w_out_b, row(ln1_g[l]), row(ln1_b[l]), alpha,
                         tm_p, nblk * MOE_BM // TOP_K)
        h_all = _outproj(valid(ya_s), valid(yb_s), valid(yc_s), valid(xn_s), 0, w_out_b, row(ln1_g[l]),
                         row(ln1_b[l]), alpha, tm_s, nblk * MOE_BM // TOP_K, out_row0=np_rows, into=h_all)

        top_idx, gates, cnt = _router(h_all, n, router_w[l].T, router_b[l].reshape(-1, 1))
        base, pad_shift, block_e, n_used = _group_layout(cnt[:, ::LANES], nblk, MOE_BM)
        dest = _dest(top_idx, base)
        inv = _invert(pad_shift, dest, MOE_BM)
        y4 = _moe(block_e, n_used, inv, h_all, l, moe_w1, moe_b1[:, :, None, :], moe_w2, moe_b2[:, :, None, :],
                  MOE_BM)
        x_all = _combine(y4, gates.T.reshape(-1), h_all, n, row(ln2_g[l]), row(ln2_b[l]), alpha)

    y_p = x_all[:np_rows].reshape(bp, tp, D_MODEL)[:, N_META:]
    y_s = x_all[np_rows:].reshape(bs, dseq, D_MODEL)
    return (y_p, y_s,
            jnp.stack(new_p[0]), jnp.stack(new_p[1]), jnp.stack(new_p[2]), jnp.stack(new_p[3]),
            jnp.stack(new_s[0]), jnp.stack(new_s[1]), jnp.stack(new_s[2]), jnp.stack(new_s[3]))
```

```python
import functools
import math

import jax
import jax.numpy as jnp
from jax import lax
from jax.experimental import pallas as pl
from jax.experimental.pallas import tpu as pltpu
from jax.experimental.pallas import tpu_sc as plsc

F32 = jnp.float32
BF16 = jnp.bfloat16

D_MODEL = 1024
N_META = 16
D_A = 384
D_B = 256
D_C = 384
N_BLK_A = 8
BLK_A = D_A // N_BLK_A
CONV_A = 4
RG_C = 8.0
CONV_B = 31
N_HEADS_C = 4
DV_C = D_C // N_HEADS_C
DK_C = DV_C // 2
D_QK_C = N_HEADS_C * DK_C
GATE_RANK = 16
GATE_TAU = 16.0
N_EXPERTS = 32
TOP_K = 4
D_FF = D_MODEL
SWIGLU_LIMIT = 7.0
SWIGLU_ALPHA = 1.702
LN_EPS = 1e-5

PA_W = 2 * D_A
PB_W = 2 * D_B
PC_Q, PC_Z, PC_K, PC_V, PC_R, PC_W = 0, 192, 256, 512, 896, 1280
P_W = PA_W + PB_W + PC_W

SUBLANES = 8
LANES = 128
SC_LANES = 16
SC_CORES, SC_SUBCORES = 2, 16
ROW_TILE = D_MODEL // LANES
VMEM_LIMIT_BYTES = 56 * 1024 * 1024
MOE_BM = 256
FF_CHUNK = 512
DMA_GROUP = 8
W_CAST_ROWS = 128
GLA_TRIP = 14
CONV_TRIP = 3
SAMPLE_PAD_T = 8


def _cparams(*sem):
    return pltpu.CompilerParams(dimension_semantics=sem, vmem_limit_bytes=VMEM_LIMIT_BYTES)


def _sds(shape, dtype=F32):
    return jax.ShapeDtypeStruct(shape, dtype)


def _pick(n, prefs):
    for p in prefs:
        if n % p == 0:
            return p
    raise ValueError(f"no tile for {n} in {prefs}")


def _ln(x, g, b):
    mu = jnp.mean(x, axis=-1, keepdims=True)
    xc = x - mu
    var = jnp.mean(xc * xc, axis=-1, keepdims=True)
    return xc * lax.rsqrt(var + LN_EPS) * g + b


def _sigmoid(x):
    return 1.0 / (1.0 + jnp.exp(-x))


def _split_bf16(x):
    hi = x.astype(BF16)
    lo = (x - hi.astype(F32)).astype(BF16)
    return hi, lo


def _dot(a, b):
    return jnp.dot(a, b, preferred_element_type=F32)


def _inproj_kernel(x_ref, g_ref, b_ref, w_ref, pa_ref, pb_ref, pc_ref, *maybe_xn, apply_ln):
    x = x_ref[...]
    if apply_ln:
        x = _ln(x, g_ref[...], b_ref[...])
        maybe_xn[0][...] = x
    xb = x.astype(BF16)
    pa_ref[...] = _dot(xb, w_ref[:, 0:PA_W])
    pb_ref[...] = _dot(xb, w_ref[:, PA_W:PA_W + PB_W])
    pc_ref[...] = _dot(xb, w_ref[:, PA_W + PB_W:P_W])


def _inproj(x, row0, nrows, ln_g, ln_b, w_packed, apply_ln):
    tm = _pick(nrows, (384, 512, 256, 128, 64, 32, 16, 8))
    while row0 % tm:
        tm //= 2
    off = row0 // tm
    const = lambda i: (0, 0)
    row = lambda i: (i, 0)
    out_shape = [_sds((nrows, PA_W)), _sds((nrows, PB_W)), _sds((nrows, PC_W))]
    out_specs = [pl.BlockSpec((tm, PA_W), row), pl.BlockSpec((tm, PB_W), row), pl.BlockSpec((tm, PC_W), row)]
    if apply_ln:
        out_shape.append(_sds((nrows, D_MODEL)))
        out_specs.append(pl.BlockSpec((tm, D_MODEL), row))
    return pl.pallas_call(
        functools.partial(_inproj_kernel, apply_ln=apply_ln),
        grid=(nrows // tm,),
        in_specs=[pl.BlockSpec((tm, D_MODEL), lambda i: (i + off, 0)),
                  pl.BlockSpec((1, D_MODEL), const), pl.BlockSpec((1, D_MODEL), const),
                  pl.BlockSpec((D_MODEL, P_W), const)],
        out_specs=out_specs, out_shape=out_shape,
        compiler_params=_cparams("parallel"), name="inproj",
    )(x, ln_g, ln_b, w_packed)


def _rglru_kernel(p_ref, cbuf_ref, h0_ref, cw_ref, cb_ref, wg_ref, bg_ref, lam_ref,
                  y_ref, cnew_ref, hlast_ref, xp_scr, a_scr, h_scr, *, T, Tc, Tv, Bb):
    lam = lam_ref[...]
    softplus_neg = jnp.maximum(-lam, 0.0) + jnp.log1p(jnp.exp(-jnp.abs(lam)))
    c_decay = -RG_C * softplus_neg
    cw = cw_ref[...]
    cb = cb_ref[...]
    bg = bg_ref[...]
    sub = lax.broadcasted_iota(jnp.int32, (Tc, D_A), 0) % SUBLANES
    halo = SUBLANES - (CONV_A - 1)
    for b in range(Bb):
        xp_scr[halo:SUBLANES, :] = cbuf_ref[b]
        xp_scr[SUBLANES:SUBLANES + T, :] = p_ref[b, :, 0:D_A]
        cnew_ref[b] = xp_scr[halo + Tv:SUBLANES + Tv, :]

        def chunk(ci, h_b):
            r0 = pl.multiple_of(ci * Tc, SUBLANES)
            win = xp_scr[pl.ds(r0, Tc + SUBLANES), :]
            xc = cb + cw[0:1] * win[halo:halo + Tc]
            for j in range(1, CONV_A):
                xc = xc + cw[j:j + 1] * win[halo + j:halo + j + Tc]
            gates = _dot(xc.astype(BF16), wg_ref[...]) + bg
            r = _sigmoid(gates[:, 0:D_A])
            i = _sigmoid(gates[:, D_A:2 * D_A])
            log_a = c_decay * r
            a = jnp.exp(log_a)
            u = jnp.sqrt(1.0 - a * a) * (i * xc)
            for s in (1, 2, 4):
                keep = sub >= s
                a_prev = pltpu.roll(a, s, 0)
                u_prev = pltpu.roll(u, s, 0)
                u = jnp.where(keep, a * u_prev + u, u)
                a = jnp.where(keep, a * a_prev, a)
            a_scr[...] = a
            h_scr[pl.ds(r0, Tc), :] = u

            def group(gi, h_b):
                c0 = pl.multiple_of(gi * SUBLANES, SUBLANES)
                g0 = pl.multiple_of(r0 + gi * SUBLANES, SUBLANES)
                h8 = a_scr[pl.ds(c0, SUBLANES), :] * h_b + h_scr[pl.ds(g0, SUBLANES), :]
                h_scr[pl.ds(g0, SUBLANES), :] = h8
                return jnp.broadcast_to(h8[SUBLANES - 1:SUBLANES, :], (SUBLANES, D_A))

            h_b = lax.fori_loop(0, Tc // SUBLANES, group, h_b)
            ga = p_ref[b, pl.ds(r0, Tc), D_A:2 * D_A]
            gelu = 0.5 * ga * (1.0 + jnp.tanh(0.7978845608028654 * (ga + 0.044715 * ga * ga * ga)))
            y_ref[b, pl.ds(r0, Tc), :] = h_scr[pl.ds(r0, Tc), :] * gelu
            return h_b

        h_b = jnp.broadcast_to(h0_ref[b], (SUBLANES, D_A))
        lax.fori_loop(0, T // Tc, chunk, h_b)
        hlast_ref[b] = h_scr[Tv - 1:Tv, :]


def _rglru(pa3, cbuf, h0, cw, cb, wg, bg, lam, Tv, Bb):
    B, T, _ = pa3.shape
    Tc = _pick(T, (344, 256, 128, 64, 48, 32, 16, 8))
    const2 = lambda i: (0, 0)
    seq3 = lambda i: (i, 0, 0)
    return pl.pallas_call(
        functools.partial(_rglru_kernel, T=T, Tc=Tc, Tv=Tv, Bb=Bb),
        grid=(B // Bb,),
        in_specs=[pl.BlockSpec((Bb, T, PA_W), seq3), pl.BlockSpec((Bb, CONV_A - 1, D_A), seq3),
                  pl.BlockSpec((Bb, 1, D_A), seq3), pl.BlockSpec((CONV_A, D_A), const2),
                  pl.BlockSpec((1, D_A), const2), pl.BlockSpec((D_A, 2 * D_A), const2),
                  pl.BlockSpec((1, 2 * D_A), const2), pl.BlockSpec((1, D_A), const2)],
        out_specs=[pl.BlockSpec((Bb, T, D_A), seq3), pl.BlockSpec((Bb, CONV_A - 1, D_A), seq3),
                   pl.BlockSpec((Bb, 1, D_A), seq3)],
        out_shape=[_sds((B, T, D_A)), _sds((B, CONV_A - 1, D_A)), _sds((B, 1, D_A))],
        scratch_shapes=[pltpu.VMEM((T + 2 * SUBLANES, D_A), F32), pltpu.VMEM((Tc, D_A), F32),
                        pltpu.VMEM((T, D_A), F32)],
        compiler_params=_cparams("parallel"), name="rglru",
    )(pa3, cbuf, h0, cw, cb, wg, bg, lam)


B_HALO = 32


def _convb_kernel(p_ref, buf_ref, w_ref, cb_ref, g_ref, b_ref, y_ref, bnew_ref, u_scr, *, T, Tc, Tv, Bb):
    w = w_ref[...]
    cb = cb_ref[...]
    g = g_ref[...]
    bb = b_ref[...]
    first = B_HALO - (CONV_B - 1)
    for b in range(Bb):
        u_scr[0:first, :] = jnp.zeros((first, D_B), F32)
        u_scr[first:B_HALO, :] = buf_ref[b]
        u_scr[B_HALO:B_HALO + T, :] = p_ref[b, :, 0:D_B] * _sigmoid(p_ref[b, :, D_B:2 * D_B])
        bnew_ref[b] = u_scr[first + Tv:B_HALO + Tv, :]

        def chunk(ci):
            r0 = ci * Tc if isinstance(ci, int) else pl.multiple_of(ci * Tc, SUBLANES)
            win = u_scr[pl.ds(r0, Tc + B_HALO), :]
            shifted = [win] + [pltpu.roll(win, Tc + B_HALO - s, 0) for s in range(1, SUBLANES)]
            acc = cb
            for j in range(CONV_B):
                a, s = divmod(first + j, SUBLANES)
                acc = acc + w[j:j + 1] * shifted[s][a * SUBLANES:a * SUBLANES + Tc]
            yn = _ln(acc, g, bb)
            y_ref[b, pl.ds(r0, Tc), :] = yn * _sigmoid(yn)

        def trip(ti, carry):
            for u in range(CONV_TRIP):
                chunk(peel + CONV_TRIP * ti + u)
            return carry

        peel = (T // Tc) % CONV_TRIP
        for ci in range(peel):
            chunk(ci)
        lax.fori_loop(0, (T // Tc) // CONV_TRIP, trip, 0)


def _convb(pb3, buf, w, cb, g, b, Tv, Bb):
    B, T, _ = pb3.shape
    Tc = _pick(T, (48, 32, 16, 8))
    const2 = lambda i: (0, 0)
    seq3 = lambda i: (i, 0, 0)
    return pl.pallas_call(
        functools.partial(_convb_kernel, T=T, Tc=Tc, Tv=Tv, Bb=Bb),
        grid=(B // Bb,),
        in_specs=[pl.BlockSpec((Bb, T, PB_W), seq3), pl.BlockSpec((Bb, CONV_B - 1, D_B), seq3),
                  pl.BlockSpec((CONV_B, D_B), const2), pl.BlockSpec((1, D_B), const2),
                  pl.BlockSpec((1, D_B), const2), pl.BlockSpec((1, D_B), const2)],
        out_specs=[pl.BlockSpec((Bb, T, D_B), seq3), pl.BlockSpec((Bb, CONV_B - 1, D_B), seq3)],
        out_shape=[_sds((B, T, D_B)), _sds((B, CONV_B - 1, D_B))],
        scratch_shapes=[pltpu.VMEM((T + B_HALO, D_B), F32)],
        compiler_params=_cparams("parallel"), name="convb",
    )(pb3, buf, w, cb, g, b)


def _gla_kernel(p_ref, s0_ref, wg2_ref, bgate_ref, ng_ref, y_ref, snew_ref, s_scr, g_scr, *, T, C, Tb, Tv, Bb):
    ri = lax.broadcasted_iota(jnp.int32, (C, C), 0)
    ci_ = lax.broadcasted_iota(jnp.int32, (C, C), 1)
    tril = ri >= ci_
    lane_k = lax.broadcasted_iota(jnp.int32, (1, D_QK_C), 1)
    lane_v = lax.broadcasted_iota(jnp.int32, (1, D_C), 1)
    hm_k = [(lane_k >= h * DK_C) & (lane_k < (h + 1) * DK_C) for h in range(N_HEADS_C)]
    hm_v = [(lane_v >= h * DV_C) & (lane_v < (h + 1) * DV_C) for h in range(N_HEADS_C)]
    rs = lax.broadcasted_iota(jnp.int32, (D_C, D_QK_C), 0)
    cs = lax.broadcasted_iota(jnp.int32, (D_C, D_QK_C), 1)
    bd_t = (rs >= 0) & (rs < 0)
    for h in range(N_HEADS_C):
        bd_t = bd_t | ((rs >= h * DV_C) & (rs < (h + 1) * DV_C) & (cs >= h * DK_C) & (cs < (h + 1) * DK_C))
    rm = lax.broadcasted_iota(jnp.int32, (D_C, D_C), 0)
    cm = lax.broadcasted_iota(jnp.int32, (D_C, D_C), 1)
    seg = (rm >= 0) & (rm < 0)
    for h in range(N_HEADS_C):
        seg = seg | ((rm >= h * DV_C) & (rm < (h + 1) * DV_C) & (cm >= h * DV_C) & (cm < (h + 1) * DV_C))
    mseg = jnp.where(seg, 1.0, 0.0).astype(BF16)
    wg2 = wg2_ref[...].astype(BF16)
    bgate = bgate_ref[...]
    ng = ng_ref[...]
    rowi = lax.broadcasted_iota(jnp.int32, (C, 1), 0)
    tdims = (((0,), (0,)), ((), ()))

    n_chunks = T // C
    nt_dims = (((1,), (1,)), ((), ()))
    tril4 = jnp.concatenate([tril] * N_HEADS_C, axis=0)
    scan_shifts = [s for s in (1, 2, 4, 8, 16, 32) if s < C]
    scan_keep = [rowi >= s for s in scan_shifts]

    def tile_rows(ti, tb, base=0):
        r0 = ti * tb
        return pl.ds(base + (r0 if isinstance(ti, int) else pl.multiple_of(r0, tb)), tb)

    def gates(ti, carry):
        rows = tile_rows(ti, Tb)
        z = p_ref[rows, PC_Z:PC_Z + GATE_RANK]
        pre = _dot(z.astype(BF16), wg2) + bgate
        g = (jnp.minimum(pre, 0.0) - jnp.log1p(jnp.exp(-jnp.abs(pre)))) * (1.0 / GATE_TAU)
        rid = ti * Tb + lax.broadcasted_iota(jnp.int32, (Tb, 1), 0)
        if Bb > 1:
            rid = rid & (T - 1)
        g_scr[rows, :] = jnp.where(rid < Tv, g, 0.0)
        return carry
    lax.fori_loop(0, Bb * T // Tb, gates, 0)

    for b in range(Bb):
        def chunk(ci):
            rows = tile_rows(ci, C, b * T)
            q = p_ref[rows, PC_Q:PC_Q + D_QK_C] * (DK_C ** -0.5)
            k = p_ref[rows, PC_K:PC_K + D_QK_C]
            v = p_ref[rows, PC_V:PC_V + D_C]
            k = jnp.where((ci * C + rowi) < Tv, k, 0.0)
            gcum = g_scr[rows, :]
            for s, keep in zip(scan_shifts, scan_keep):
                gcum = gcum + jnp.where(keep, pltpu.roll(gcum, s, 0), 0.0)
            g_last = gcum[C - 1:C, :]
            g_mid = gcum[C // 2 - 1:C // 2, :]
            vb = v.astype(BF16)
            qt = q * jnp.exp(gcum - g_mid)
            ktb = (k * jnp.exp(g_mid - gcum)).astype(BF16)
            q4 = jnp.concatenate([jnp.where(hm_k[h], qt, 0.0) for h in range(N_HEADS_C)], axis=0).astype(BF16)
            sc = lax.dot_general(q4, ktb, nt_dims, preferred_element_type=F32)
            r4 = _dot(jnp.where(tril4, sc, 0.0).astype(BF16), vb)
            o = jnp.where(hm_v[0], r4[0:C], 0.0)
            for h in range(1, N_HEADS_C):
                o = o + jnp.where(hm_v[h], r4[h * C:(h + 1) * C], 0.0)
            kd = (k * jnp.exp(g_last - gcum)).astype(BF16)
            upd_t = lax.dot_general(vb, kd, tdims, preferred_element_type=F32)
            return rows, o, (q * jnp.exp(gcum)).astype(BF16), jnp.exp(g_last), jnp.where(bd_t, upd_t, 0.0)

        def apply_state(rows, o, qg, dec, upd):
            s_in = s_scr[...]
            y_ref[rows, :] = o + lax.dot_general(qg, s_in.astype(BF16), nt_dims, preferred_element_type=F32)
            s_scr[...] = s_in * dec + upd

        def trip(ti, carry):
            parts = [chunk(peel + GLA_TRIP * ti + u) for u in range(GLA_TRIP)]
            for part in parts:
                apply_state(*part)
            return carry

        s_scr[...] = jnp.zeros((D_C, D_QK_C), F32)
        for h in range(N_HEADS_C):
            s_scr[h * DV_C:(h + 1) * DV_C, h * DK_C:(h + 1) * DK_C] = s0_ref[b, h]
        peel = n_chunks % GLA_TRIP
        for ci in range(peel):
            apply_state(*chunk(ci))
        lax.fori_loop(0, n_chunks // GLA_TRIP, trip, 0)
        for h in range(N_HEADS_C):
            snew_ref[b, h] = s_scr[h * DV_C:(h + 1) * DV_C, h * DK_C:(h + 1) * DK_C]

    def finish(ti, carry):
        rows = tile_rows(ti, Tb)
        o = y_ref[rows, :]
        rg = p_ref[rows, PC_R:PC_R + D_C]
        o2_hi, o2_lo = _split_bf16(o * o)
        ms = (_dot(o2_hi, mseg) + _dot(o2_lo, mseg)) * (1.0 / DV_C)
        y_ref[rows, :] = o * lax.rsqrt(ms + LN_EPS) * ng * (rg * _sigmoid(rg))
        return carry
    lax.fori_loop(0, Bb * T // Tb, finish, 0)


def _gla(pc, B, T, s0, wg2, bgate, ng, Tv, Bb):
    assert Bb == 1 or T & (T - 1) == 0
    C = _pick(T, (48, 32, 16, 8))
    Tb = _pick(Bb * T, (344, 256, 128, 64, 48, 32, 16, 8))
    const2 = lambda i: (0, 0)
    row2 = lambda i: (i, 0)
    seq4 = lambda i: (i, 0, 0, 0)
    st = (Bb, N_HEADS_C, DV_C, DK_C)
    y, s_new_t = pl.pallas_call(
        functools.partial(_gla_kernel, T=T, C=C, Tb=Tb, Tv=Tv, Bb=Bb),
        grid=(B // Bb,),
        in_specs=[pl.BlockSpec((Bb * T, PC_W), row2), pl.BlockSpec(st, seq4),
                  pl.BlockSpec((GATE_RANK, D_QK_C), const2), pl.BlockSpec((1, D_QK_C), const2),
                  pl.BlockSpec((1, D_C), const2)],
        out_specs=[pl.BlockSpec((Bb * T, D_C), row2), pl.BlockSpec(st, seq4)],
        out_shape=[_sds((B * T, D_C)), _sds((B, N_HEADS_C, DV_C, DK_C))],
        scratch_shapes=[pltpu.VMEM((D_C, D_QK_C), F32), pltpu.VMEM((Bb * T, D_QK_C), F32)],
        compiler_params=_cparams("parallel"), name="gla",
    )(pc, jnp.swapaxes(s0, 2, 3), wg2, bgate, ng)
    return y, jnp.swapaxes(s_new_t, 2, 3)


def _tiles_to_rows(ref, lead, rows):
    return jnp.concatenate([ref[lead + (pl.ds(s, rows, stride=ROW_TILE), slice(None))]
                            for s in range(ROW_TILE)], axis=1)


def _rows_to_tiles(ref, lead, rows, val):
    for s in range(ROW_TILE):
        ref[lead + (pl.ds(s, rows, stride=ROW_TILE), slice(None))] = val[:, s * LANES:(s + 1) * LANES]


def _outproj_kernel(ya_ref, yb_ref, yc_ref, x_ref, w_ref, g_ref, b_ref, *rest, alpha, n_blocks):
    h_ref = rest[-1]

    @pl.when(pl.program_id(0) < n_blocks)
    def _():
        y_cat = jnp.concatenate([ya_ref[...].astype(BF16), yb_ref[...].astype(BF16), yc_ref[...].astype(BF16)],
                                axis=1)
        y = _dot(y_cat, w_ref[...])
        h = _ln(alpha * x_ref[...] + y, g_ref[...], b_ref[...])
        _rows_to_tiles(h_ref, (), h.shape[0], h)

    @pl.when(pl.program_id(0) >= n_blocks)
    def _():
        h_ref[...] = jnp.zeros(h_ref.shape, F32)


def _outproj(ya, yb, yc, x, row0, w, g, b, alpha, tm, out_rows, out_row0=0, into=None):
    n = ya.shape[0]
    nb = n // tm
    assert n % tm == 0 and row0 % tm == 0 and out_row0 % tm == 0 and out_rows % tm == 0
    off, out_off = row0 // tm, out_row0 // tm
    grid = nb if into is not None else out_rows // tm
    row = lambda i: (jnp.minimum(i, nb - 1), 0)
    const = lambda i: (0, 0)
    in_specs = [pl.BlockSpec((tm, D_A), row), pl.BlockSpec((tm, D_B), row), pl.BlockSpec((tm, D_C), row),
                pl.BlockSpec((tm, D_MODEL), lambda i: (jnp.minimum(i, nb - 1) + off, 0)),
                pl.BlockSpec((D_MODEL, D_MODEL), const), pl.BlockSpec((1, D_MODEL), const),
                pl.BlockSpec((1, D_MODEL), const)]
    args = [ya, yb, yc, x, w, g, b]
    aliases = {}
    if into is not None:
        in_specs.append(pl.BlockSpec(memory_space=pl.ANY))
        args.append(into)
        aliases = {len(args) - 1: 0}
    return pl.pallas_call(
        functools.partial(_outproj_kernel, alpha=alpha, n_blocks=nb),
        grid=(grid,),
        in_specs=in_specs,
        out_specs=pl.BlockSpec((tm * ROW_TILE, LANES), lambda i: (i + out_off, 0)),
        out_shape=_sds((out_rows * ROW_TILE, LANES)),
        input_output_aliases=aliases,
        compiler_params=_cparams("arbitrary"), name="outproj",
    )(*args)


def _router_kernel(h_ref, wt_ref, b_ref, idx_ref, gate_ref, cnt_ref):
    nt = (((1,), (1,)), ((), ()))
    hh, hl = _split_bf16(_tiles_to_rows(h_ref, (), idx_ref.shape[1]))
    wh, wl = _split_bf16(wt_ref[...])
    logits = (lax.dot_general(wh, hh, nt, preferred_element_type=F32)
              + lax.dot_general(wh, hl, nt, preferred_element_type=F32)
              + lax.dot_general(wl, hh, nt, preferred_element_type=F32)) + b_ref[...]
    eid = lax.broadcasted_iota(jnp.int32, logits.shape, 0)
    vals = []
    member = jnp.zeros(logits.shape, F32)
    for k in range(TOP_K):
        m = jnp.max(logits, axis=0, keepdims=True)
        sel = jnp.min(jnp.where(logits == m, eid, N_EXPERTS), axis=0, keepdims=True)
        idx_ref[k:k + 1, :] = sel
        vals.append(m)
        hit = eid == sel
        member = jnp.where(hit, 1.0, member)
        logits = jnp.where(hit, -jnp.inf, logits)
    es = [jnp.exp(v - vals[0]) for v in vals]
    tot = es[0] + es[1] + es[2] + es[3]
    for k in range(TOP_K):
        gate_ref[k:k + 1, :] = es[k] / tot
    cnt_ref[...] = jnp.broadcast_to(jnp.sum(member, axis=1, keepdims=True), cnt_ref.shape)


def _router_tile(n):
    return _pick(n, (896, 640, 512, 384, 256, 128))


def _router(h, n, wt, b):
    tm = _router_tile(n)
    nt = n // tm
    return pl.pallas_call(
        _router_kernel,
        grid=(nt,),
        in_specs=[pl.BlockSpec((tm * ROW_TILE, LANES), lambda i: (i, 0)),
                  pl.BlockSpec((N_EXPERTS, D_MODEL), lambda i: (0, 0)),
                  pl.BlockSpec((N_EXPERTS, 1), lambda i: (0, 0))],
        out_specs=[pl.BlockSpec((TOP_K, tm), lambda i: (0, i)), pl.BlockSpec((TOP_K, tm), lambda i: (0, i)),
                   pl.BlockSpec((N_EXPERTS, LANES), lambda i: (0, i))],
        out_shape=[_sds((TOP_K, n), jnp.int32), _sds((TOP_K, n)), _sds((N_EXPERTS, nt * LANES))],
        compiler_params=_cparams("parallel"), name="router",
    )(h, wt, b)


def _dest_kernel(idx_ref, base_ref, dest_ref):
    tm = idx_ref.shape[1]
    eid = lax.broadcasted_iota(jnp.int32, (N_EXPERTS, tm), 0)
    hits = [eid == idx_ref[k:k + 1, :] for k in range(TOP_K)]
    member = jnp.zeros((N_EXPERTS, tm), F32)
    for k in range(TOP_K):
        member = jnp.where(hits[k], 1.0, member)
    earlier = (lax.broadcasted_iota(jnp.int32, (tm, tm), 0) < lax.broadcasted_iota(jnp.int32, (tm, tm), 1))
    rank = _dot(member.astype(BF16), jnp.where(earlier, 1.0, 0.0).astype(BF16))
    pos = base_ref[...] + rank
    for k in range(TOP_K):
        dest_ref[k:k + 1, :] = jnp.sum(jnp.where(hits[k], pos, 0.0), axis=0, keepdims=True).astype(jnp.int32)


def _dest(top_idx, base):
    n = top_idx.shape[1]
    tm = _router_tile(n)
    return pl.pallas_call(
        _dest_kernel,
        grid=(n // tm,),
        in_specs=[pl.BlockSpec((TOP_K, tm), lambda i: (0, i)),
                  pl.BlockSpec((None, N_EXPERTS, 1), lambda i: (i, 0, 0))],
        out_specs=pl.BlockSpec((TOP_K, tm), lambda i: (0, i)),
        out_shape=_sds((TOP_K, n), jnp.int32),
        compiler_params=_cparams("parallel"), name="moe_dest",
    )(top_idx, base)


def _invert(pad_shift, dest, bm):
    k, n = dest.shape
    n_rows = pad_shift.shape[0] * bm
    ch = next(c for c in range(min(n, 4096) // SC_LANES * SC_LANES, 0, -SC_LANES) if n % c == 0)
    fill = (jnp.arange(n_rows, dtype=jnp.int32).reshape(-1, bm) + pad_shift[:, None]).reshape(-1)
    mesh = plsc.VectorSubcoreMesh(core_axis_name="c", subcore_axis_name="s", num_cores=SC_CORES,
                                  num_subcores=SC_SUBCORES)

    @functools.partial(
        pl.kernel, mesh=mesh, out_type=_sds((n_rows,), jnp.int32),
        scratch_types=[pltpu.VMEM((n_rows,), jnp.int32), pltpu.VMEM((ch,), jnp.int32)],
        compiler_params=pltpu.CompilerParams(needs_layout_passes=False), name="moe_invert_sc")
    def run(fill_hbm, dest_hbm, inv_hbm, inv_v, d_v):
        first = jnp.logical_and(lax.axis_index("c") == 0, lax.axis_index("s") == 0)

        @pl.when(first)
        def _():
            pltpu.sync_copy(fill_hbm, inv_v)
            lanes = lax.iota(jnp.int32, SC_LANES)
            for slot in range(k):
                def chunk(c, carry):
                    pltpu.sync_copy(dest_hbm.at[pl.ds(slot * n + c * ch, ch)], d_v)

                    def vec(v, carry):
                        idx = d_v[pl.ds(v * SC_LANES, SC_LANES)]
                        plsc.store_scatter(inv_v, [idx], (c * ch + v * SC_LANES + lanes) * k + slot)
                        return carry
                    return lax.fori_loop(0, ch // SC_LANES, vec, carry)
                lax.fori_loop(0, n // ch, chunk, 0)
            pltpu.sync_copy(inv_v, inv_hbm)

    return run(fill, dest.reshape(-1))


def _moe_kernel(be_ref, nu_ref, inv_ref, h_hbm, w1f_ref, b1_ref, w2f_ref, b2_ref, y4_hbm,
                xbuf, xb, obuf, w1_ref, w2_ref, gsem, ssem, *, bm):
    i = pl.program_id(0)
    n_used = nu_ref[0]
    last_blk = pl.num_programs(0) - 1
    n_ff = D_FF // FF_CHUNK

    def tile(r):
        start = r * ROW_TILE
        return pl.ds(start if isinstance(r, int) else pl.multiple_of(start, ROW_TILE), ROW_TILE)

    def gather_row(blk, slot, j, r=None):
        r = inv_ref[blk * bm + j] if r is None else r
        t = lax.shift_right_logical(r, 2)
        pltpu.make_async_copy(h_hbm.at[tile(t), :], xbuf.at[slot, tile(j), :], gsem.at[slot]).start()

    def scatter_row(blk, slot, j, r=None):
        r = inv_ref[blk * bm + j] if r is None else r
        pltpu.make_async_copy(obuf.at[slot, tile(j), :], y4_hbm.at[tile(r), :], ssem.at[slot]).start()

    def wait_gather(slot):
        pltpu.make_async_copy(h_hbm.at[pl.ds(0, bm * ROW_TILE), :], xbuf.at[slot], gsem.at[slot]).wait()

    def wait_scatter(slot):
        pltpu.make_async_copy(obuf.at[slot], y4_hbm.at[pl.ds(0, bm * ROW_TILE), :], ssem.at[slot]).wait()

    def loop_rows(fn, blk, slot):
        def body(j, c):
            fn(blk, slot, j)
            return c
        lax.fori_loop(0, bm, body, 0)

    def issue_rows(fn, blk, slot, j_lo, j_hi):
        for j0 in range(j_lo, j_hi, DMA_GROUP):
            js = range(j0, j0 + DMA_GROUP)
            ids = [inv_ref[blk * bm + j] for j in js]
            for j, r in zip(js, ids):
                fn(blk, slot, j, r)

    def step():
        slot = i % 2

        @pl.when(i + 1 < n_used)
        def _():
            issue_rows(gather_row, i + 1, 1 - slot, 0, bm)
        wait_gather(slot)
        xb[...] = _tiles_to_rows(xbuf, (slot,), bm).astype(BF16)
        acc = jnp.zeros((bm, D_MODEL), F32)
        for c in range(n_ff):
            lo = c * FF_CHUNK
            x = xb[...]
            hg = _dot(x, w1_ref[:, lo:lo + FF_CHUNK]) + b1_ref[:, lo:lo + FF_CHUNK]
            hl = _dot(x, w1_ref[:, D_FF + lo:D_FF + lo + FF_CHUNK]) + b1_ref[:, D_FF + lo:D_FF + lo + FF_CHUNK]
            gate = jnp.minimum(hg, SWIGLU_LIMIT)
            lin = jnp.clip(hl, -SWIGLU_LIMIT, SWIGLU_LIMIT)
            act = gate * _sigmoid(SWIGLU_ALPHA * gate) * (lin + 1.0)
            acc = acc + _dot(act.astype(BF16), w2_ref[lo:lo + FF_CHUNK, :])

        @pl.when(i >= 2)
        def _():
            wait_scatter(slot)
        _rows_to_tiles(obuf, (slot,), bm, acc + b2_ref[...])
        issue_rows(scatter_row, i, slot, 0, bm)

    new_expert = jnp.logical_or(i == 0, be_ref[i] != be_ref[jnp.maximum(i - 1, 0)])

    @pl.when(jnp.logical_and(new_expert, i < n_used))
    def _():
        def cast_rows(r, c):
            rows = pl.ds(pl.multiple_of(r * W_CAST_ROWS, W_CAST_ROWS), W_CAST_ROWS)
            w1_ref[rows, :] = w1f_ref[rows, :].astype(BF16)
            w2_ref[rows, :] = w2f_ref[rows, :].astype(BF16)
            return c
        lax.fori_loop(0, D_MODEL // W_CAST_ROWS, cast_rows, 0)

    @pl.when(jnp.logical_and(i == 0, n_used > 0))
    def _():
        loop_rows(gather_row, 0, 0)

    @pl.when(i < n_used)
    def _():
        step()

    @pl.when(i == n_used - 1)
    def _():
        slot = i % 2
        wait_scatter(slot)

        @pl.when(i >= 1)
        def _():
            wait_scatter(1 - slot)
        obuf[0] = jnp.zeros((bm * ROW_TILE, LANES), F32)

        def zero_block(blk, c):
            rows = bm * ROW_TILE
            cp = pltpu.make_async_copy(obuf.at[0], y4_hbm.at[pl.ds(pl.multiple_of(blk * rows, rows), rows), :],
                                       ssem.at[0])
            cp.start()
            cp.wait()
            return c
        lax.fori_loop(n_used, last_blk + 1, zero_block, 0)


def _moe(block_e, n_used, inv, h, layer, w1, b1, w2, b2, bm):
    nblk = block_e.shape[0]
    wmap = lambda i, be, nu, iv: (layer, be[i], 0, 0)
    return pl.pallas_call(
        functools.partial(_moe_kernel, bm=bm),
        grid_spec=pltpu.PrefetchScalarGridSpec(
            num_scalar_prefetch=3, grid=(nblk,),
            in_specs=[pl.BlockSpec(memory_space=pl.ANY),
                      pl.BlockSpec((None, None, D_MODEL, 2 * D_FF), wmap),
                      pl.BlockSpec((None, None, 1, 2 * D_FF), wmap),
                      pl.BlockSpec((None, None, D_FF, D_MODEL), wmap),
                      pl.BlockSpec((None, None, 1, D_MODEL), wmap)],
            out_specs=pl.BlockSpec(memory_space=pl.ANY),
            scratch_shapes=[pltpu.VMEM((2, bm * ROW_TILE, LANES), F32), pltpu.VMEM((bm, D_MODEL), BF16),
                            pltpu.VMEM((2, bm * ROW_TILE, LANES), F32),
                            pltpu.VMEM((D_MODEL, 2 * D_FF), BF16), pltpu.VMEM((D_FF, D_MODEL), BF16),
                            pltpu.SemaphoreType.DMA((2,)), pltpu.SemaphoreType.DMA((2,))]),
        out_shape=_sds((nblk * bm * ROW_TILE, LANES)),
        compiler_params=_cparams("arbitrary"), name="moe_experts",
    )(block_e, n_used, inv, h, w1, b1, w2, b2)


def _combine_kernel(gates_ref, y4_ref, h_ref, g_ref, b_ref, out_ref, acc_scr, *, alpha, tok_off):
    tm = out_ref.shape[0]
    tok0 = tok_off + pl.program_id(0) * tm

    def group(gi, carry):
        t0 = pl.multiple_of(gi * SUBLANES, SUBLANES)
        tiles = []
        for u in range(SUBLANES):
            t = t0 + u
            tile = alpha * h_ref[pl.ds(pl.multiple_of(t * ROW_TILE, ROW_TILE), ROW_TILE), :]
            for k in range(TOP_K):
                r = (t * TOP_K + k) * ROW_TILE
                tile = tile + (gates_ref[(tok0 + t) * TOP_K + k]
                               * y4_ref[pl.ds(pl.multiple_of(r, ROW_TILE), ROW_TILE), :])
            tiles.append(tile.reshape(1, ROW_TILE, LANES))
        rows = jnp.swapaxes(jnp.concatenate(tiles, axis=0), 0, 1)
        for s in range(ROW_TILE):
            acc_scr[pl.ds(t0, SUBLANES), s * LANES:(s + 1) * LANES] = rows[s]
        return carry
    lax.fori_loop(0, tm // SUBLANES, group, 0)
    out_ref[...] = _ln(acc_scr[...], g_ref[...], b_ref[...])


def _combine(y4, gates_flat, h, row0, n, g, b, alpha):
    tm = _pick(math.gcd(n, row0) if row0 else n, (448, 384, 224, 128, 64, 32, 16, 8))
    off = row0 // tm
    row = lambda i, gt: (i + off, 0)
    const = lambda i, gt: (0, 0)
    return pl.pallas_call(
        functools.partial(_combine_kernel, alpha=alpha, tok_off=row0),
        grid_spec=pltpu.PrefetchScalarGridSpec(
            num_scalar_prefetch=1, grid=(n // tm,),
            in_specs=[pl.BlockSpec((TOP_K * tm * ROW_TILE, LANES), row),
                      pl.BlockSpec((tm * ROW_TILE, LANES), row),
                      pl.BlockSpec((1, D_MODEL), const), pl.BlockSpec((1, D_MODEL), const)],
            out_specs=pl.BlockSpec((tm, D_MODEL), lambda i, gt: (i, 0)),
            scratch_shapes=[pltpu.VMEM((tm, D_MODEL), F32)]),
        out_shape=_sds((n, D_MODEL)),
        compiler_params=_cparams("arbitrary"), name="moe_combine",
    )(gates_flat, y4, h, g, b)


def _group_layout(cnt_tiles, nblk, bm):
    cnt = cnt_tiles.astype(jnp.int32)
    counts = jnp.sum(cnt, axis=1)
    padded = (counts + bm - 1) // bm * bm
    pad_end = jnp.cumsum(padded)
    pad_start = pad_end - padded
    base = pad_start[:, None] + jnp.cumsum(cnt, axis=1) - cnt
    starts = jnp.arange(nblk, dtype=jnp.int32) * bm
    groups_before = jnp.sum((pad_end[None, :] <= starts[:, None]).astype(jnp.int32), axis=1)
    block_e = jnp.minimum(groups_before, N_EXPERTS - 1).astype(jnp.int32)
    n_used = (pad_end[-1] // bm).astype(jnp.int32).reshape(1)
    total = jnp.sum(counts)
    real_before = jnp.concatenate([jnp.cumsum(counts), total.reshape(1)])[groups_before]
    pad_shift = (total - real_before).astype(jnp.int32)
    return base.T.astype(F32)[:, :, None], pad_shift, block_e, n_used


def _pack_w_in(w):
    o = 0
    xa, o = w[:, o:o + D_A], o + D_A
    ga, o = w[:, o:o + D_A], o + D_A
    vb, o = w[:, o:o + D_B], o + D_B
    gb, o = w[:, o:o + D_B], o + D_B
    q, o = w[:, o:o + D_QK_C], o + D_QK_C
    k, o = w[:, o:o + D_QK_C], o + D_QK_C
    v, o = w[:, o:o + D_C], o + D_C
    r, o = w[:, o:o + D_C], o + D_C
    z = w[:, o:o + GATE_RANK]
    zq = jnp.zeros((w.shape[0], PC_K - PC_Z - GATE_RANK), w.dtype)
    zk = jnp.zeros((w.shape[0], PC_V - PC_K - D_QK_C), w.dtype)
    return jnp.concatenate([xa, ga, vb, gb, q, z, zq, k, zk, v, r], axis=1).astype(BF16)


def _block_diag(w):
    eye = jnp.eye(N_BLK_A, dtype=w.dtype)
    return jnp.einsum("hij,hg->higj", w, eye).reshape(D_A, D_A)


def kernel(x_prompt, x_sample, state_conv_a, state_rglru, state_conv_b, state_gla, meta_tokens, ln0_g, ln0_b,
           w_in, conv_a_w, conv_a_b, w_rg, b_rg, w_ig, b_ig, lru_lambda, conv_b_w, conv_b_b, ln_b_g, ln_b_b,
           w_gate2, b_gate, gla_norm_g, w_out, ln1_g, ln1_b, router_w, router_b, moe_w1, moe_b1, moe_w2, moe_b2,
           ln2_g, ln2_b):
    bp, seq, _ = x_prompt.shape
    bs, dseq, _ = x_sample.shape
    depth = w_in.shape[0]
    tp = N_META + seq
    ts = SAMPLE_PAD_T
    np_rows = bp * tp
    ns_rows = bs * dseq
    n = np_rows + ns_rows
    alpha = (2 * depth) ** 0.25
    row = lambda a: a.reshape(1, -1)

    meta = jnp.broadcast_to(meta_tokens[None], (bp, N_META, D_MODEL))
    xp_in = jnp.concatenate([meta, x_prompt], axis=1).reshape(np_rows, D_MODEL)
    xs_in = jnp.pad(x_sample, ((0, 0), (0, ts - dseq), (0, 0))).reshape(bs * ts, D_MODEL)
    zeros_p = (jnp.zeros((bp, CONV_A - 1, D_A), F32), jnp.zeros((bp, 1, D_A), F32),
               jnp.zeros((bp, CONV_B - 1, D_B), F32), jnp.zeros((bp, N_HEADS_C, DK_C, DV_C), F32))
    sb = _pick(bs, (8, 4, 2, 1))
    tm_p = _pick(np_rows, (384, 512, 256, 128, 64, 32, 16, 8))
    tm_s = _pick(math.gcd(ns_rows, np_rows), (128, 64, 32, 16, 8))
    nblk = -(-(TOP_K * n) // MOE_BM) + N_EXPERTS
    while (nblk * MOE_BM // TOP_K) % math.lcm(tm_p, tm_s):
        nblk += 1

    new_p = ([], [], [], [])
    new_s = ([], [], [], [])
    x_all = None
    for l in range(depth):
        w_packed = _pack_w_in(w_in[l])
        wg = jnp.concatenate([_block_diag(w_rg[l]), _block_diag(w_ig[l])], axis=1).astype(BF16)
        bg = jnp.concatenate([b_rg[l], b_ig[l]]).reshape(1, -1)
        w_out_b = w_out[l].astype(BF16)
        ng = row(jnp.tile(gla_norm_g[l], N_HEADS_C))
        first = l == 0
        if first:
            pa_p, pb_p, pc_p, xn_p = _inproj(xp_in, 0, np_rows, row(ln0_g), row(ln0_b), w_packed, True)
            pa_s, pb_s, pc_s, xn_s = _inproj(xs_in, 0, bs * ts, row(ln0_g), row(ln0_b), w_packed, True)
            res_p, res_p_row0 = xn_p, 0
        else:
            pa_p, pb_p, pc_p = _inproj(x_all, 0, np_rows, row(ln0_g), row(ln0_b), w_packed, False)
            xn_s = jnp.pad(x_all[np_rows:].reshape(bs, dseq, D_MODEL),
                           ((0, 0), (0, ts - dseq), (0, 0))).reshape(bs * ts, D_MODEL)
            pa_s, pb_s, pc_s = _inproj(xn_s, 0, bs * ts, row(ln0_g), row(ln0_b), w_packed, False)
            res_p, res_p_row0 = x_all, 0

        outs = []
        for (pa, pb, pc, nb, t, tv, bb, st) in (
                (pa_p, pb_p, pc_p, bp, tp, tp, 1, zeros_p),
                (pa_s, pb_s, pc_s, bs, ts, dseq, sb,
                 (state_conv_a[l], state_rglru[l].reshape(bs, 1, D_A), state_conv_b[l], state_gla[l]))):
            ya, ca_new, h_last = _rglru(pa.reshape(nb, t, PA_W), st[0], st[1], conv_a_w[l], row(conv_a_b[l]),
                                        wg, bg, row(lru_lambda[l]), tv, bb)
            yb, cb_new = _convb(pb.reshape(nb, t, PB_W), st[2], conv_b_w[l], row(conv_b_b[l]),
                                row(ln_b_g[l]), row(ln_b_b[l]), tv, bb)
            yc, s_new = _gla(pc, nb, t, st[3], w_gate2[l], row(b_gate[l]), ng, tv, bb)
            outs.append((ya.reshape(nb * t, D_A), yb.reshape(nb * t, D_B), yc,
                         ca_new, h_last.reshape(nb, D_A), cb_new, s_new))
        (ya_p, yb_p, yc_p, *st_p), (ya_s, yb_s, yc_s, *st_s) = outs
        for j in range(4):
            new_p[j].append(st_p[j])
            new_s[j].append(st_s[j])

        valid = lambda a: a.reshape(bs, ts, -1)[:, :dseq].reshape(ns_rows, -1)
        h_all = _outproj(ya_p, yb_p, yc_p, res_p, res_p_row0, w_out_b, row(ln1_g[l]), row(ln1_b[l]), alpha,
                         tm_p, nblk * MOE_BM // TOP_K)
        h_all = _outproj(valid(ya_s), valid(yb_s), valid(yc_s), valid(xn_s), 0, w_out_b, row(ln1_g[l]),
                         row(ln1_b[l]), alpha, tm_s, nblk * MOE_BM // TOP_K, out_row0=np_rows, into=h_all)

        top_idx, gates, cnt = _router(h_all, n, router_w[l].T, router_b[l].reshape(-1, 1))
        base, pad_shift, block_e, n_used = _group_layout(cnt[:, ::LANES], nblk, MOE_BM)
        dest = _dest(top_idx, base)
        inv = _invert(pad_shift, dest, MOE_BM)
        y4 = _moe(block_e, n_used, inv, h_all, l, moe_w1, moe_b1[:, :, None, :], moe_w2, moe_b2[:, :, None, :],
                  MOE_BM)
        comb = functools.partial(_combine, y4, gates.T.reshape(-1), h_all, g=row(ln2_g[l]), b=row(ln2_b[l]),
                                 alpha=alpha)
        if l + 1 < depth:
            x_all = comb(0, n)
        else:
            out_p, out_s = comb(0, np_rows), comb(np_rows, ns_rows)

    y_p = out_p.reshape(bp, tp, D_MODEL)[:, N_META:]
    y_s = out_s.reshape(bs, dseq, D_MODEL)
    return (y_p, y_s,
            jnp.stack(new_p[0]), jnp.stack(new_p[1]), jnp.stack(new_p[2]), jnp.stack(new_p[3]),
            jnp.stack(new_s[0]), jnp.stack(new_s[1]), jnp.stack(new_s[2]), jnp.stack(new_s[3]))
```

```python
import functools
import math

import jax
import jax.numpy as jnp
from jax import lax
from jax.experimental import pallas as pl
from jax.experimental.pallas import tpu as pltpu
from jax.experimental.pallas import tpu_sc as plsc

F32 = jnp.float32
BF16 = jnp.bfloat16

D_MODEL = 1024
N_META = 16
D_A = 384
D_B = 256
D_C = 384
N_BLK_A = 8
BLK_A = D_A // N_BLK_A
CONV_A = 4
RG_C = 8.0
CONV_B = 31
N_HEADS_C = 4
DV_C = D_C // N_HEADS_C
DK_C = DV_C // 2
D_QK_C = N_HEADS_C * DK_C
GATE_RANK = 16
GATE_TAU = 16.0
N_EXPERTS = 32
TOP_K = 4
D_FF = D_MODEL
SWIGLU_LIMIT = 7.0
SWIGLU_ALPHA = 1.702
LN_EPS = 1e-5

PA_W = 2 * D_A
PB_W = 2 * D_B
PC_Q, PC_Z, PC_K, PC_V, PC_R, PC_W = 0, 192, 256, 512, 896, 1280
P_W = PA_W + PB_W + PC_W

SUBLANES = 8
LANES = 128
SC_LANES = 16
SC_CORES, SC_SUBCORES = 2, 16
ROW_TILE = D_MODEL // LANES
VMEM_LIMIT_BYTES = 56 * 1024 * 1024
MOE_BM = 512
FF_CHUNK = 512
DMA_GROUP = 8
W_CAST_ROWS = 128
GLA_TRIP = 14
CONV_TRIP = 7
SAMPLE_PAD_T = 8


def _cparams(*sem):
    return pltpu.CompilerParams(dimension_semantics=sem, vmem_limit_bytes=VMEM_LIMIT_BYTES)


def _sds(shape, dtype=F32):
    return jax.ShapeDtypeStruct(shape, dtype)


def _pick(n, prefs):
    for p in prefs:
        if n % p == 0:
            return p
    raise ValueError(f"no tile for {n} in {prefs}")


def _ln(x, g, b):
    mu = jnp.mean(x, axis=-1, keepdims=True)
    xc = x - mu
    var = jnp.mean(xc * xc, axis=-1, keepdims=True)
    return xc * lax.rsqrt(var + LN_EPS) * g + b


def _sigmoid(x):
    return 1.0 / (1.0 + jnp.exp(-x))


def _split_bf16(x):
    hi = x.astype(BF16)
    lo = (x - hi.astype(F32)).astype(BF16)
    return hi, lo


def _dot(a, b):
    return jnp.dot(a, b, preferred_element_type=F32)


def _inproj_kernel(x_ref, g_ref, b_ref, w_ref, pa_ref, pb_ref, pc_ref, *maybe_xn, apply_ln):
    x = x_ref[...]
    if apply_ln:
        x = _ln(x, g_ref[...], b_ref[...])
        maybe_xn[0][...] = x
    xb = x.astype(BF16)
    pa_ref[...] = _dot(xb, w_ref[:, 0:PA_W])
    pb_ref[...] = _dot(xb, w_ref[:, PA_W:PA_W + PB_W])
    pc_ref[...] = _dot(xb, w_ref[:, PA_W + PB_W:P_W])


def _inproj(x, row0, nrows, ln_g, ln_b, w_packed, apply_ln):
    tm = _pick(nrows, (384, 512, 256, 128, 64, 32, 16, 8))
    while row0 % tm:
        tm //= 2
    off = row0 // tm
    const = lambda i: (0, 0)
    row = lambda i: (i, 0)
    out_shape = [_sds((nrows, PA_W)), _sds((nrows, PB_W)), _sds((nrows, PC_W))]
    out_specs = [pl.BlockSpec((tm, PA_W), row), pl.BlockSpec((tm, PB_W), row), pl.BlockSpec((tm, PC_W), row)]
    if apply_ln:
        out_shape.append(_sds((nrows, D_MODEL)))
        out_specs.append(pl.BlockSpec((tm, D_MODEL), row))
    return pl.pallas_call(
        functools.partial(_inproj_kernel, apply_ln=apply_ln),
        grid=(nrows // tm,),
        in_specs=[pl.BlockSpec((tm, D_MODEL), lambda i: (i + off, 0)),
                  pl.BlockSpec((1, D_MODEL), const), pl.BlockSpec((1, D_MODEL), const),
                  pl.BlockSpec((D_MODEL, P_W), const)],
        out_specs=out_specs, out_shape=out_shape,
        compiler_params=_cparams("parallel"), name="inproj",
    )(x, ln_g, ln_b, w_packed)


def _rglru_kernel(p_ref, cbuf_ref, h0_ref, cw_ref, cb_ref, wg_ref, bg_ref, lam_ref,
                  y_ref, cnew_ref, hlast_ref, xp_scr, a_scr, h_scr, *, T, Tc, Tv, Bb):
    lam = lam_ref[...]
    softplus_neg = jnp.maximum(-lam, 0.0) + jnp.log1p(jnp.exp(-jnp.abs(lam)))
    c_decay = -RG_C * softplus_neg
    cw = cw_ref[...]
    cb = cb_ref[...]
    bg = bg_ref[...]
    sub = lax.broadcasted_iota(jnp.int32, (Tc, D_A), 0) % SUBLANES
    halo = SUBLANES - (CONV_A - 1)
    for b in range(Bb):
        xp_scr[halo:SUBLANES, :] = cbuf_ref[b]
        xp_scr[SUBLANES:SUBLANES + T, :] = p_ref[b, :, 0:D_A]
        cnew_ref[b] = xp_scr[halo + Tv:SUBLANES + Tv, :]

        def chunk(ci, h_b):
            r0 = pl.multiple_of(ci * Tc, SUBLANES)
            win = xp_scr[pl.ds(r0, Tc + SUBLANES), :]
            xc = cb + cw[0:1] * win[halo:halo + Tc]
            for j in range(1, CONV_A):
                xc = xc + cw[j:j + 1] * win[halo + j:halo + j + Tc]
            gates = _dot(xc.astype(BF16), wg_ref[...]) + bg
            r = _sigmoid(gates[:, 0:D_A])
            i = _sigmoid(gates[:, D_A:2 * D_A])
            log_a = c_decay * r
            a = jnp.exp(log_a)
            u = jnp.sqrt(1.0 - a * a) * (i * xc)
            for s in (1, 2, 4):
                keep = sub >= s
                a_prev = pltpu.roll(a, s, 0)
                u_prev = pltpu.roll(u, s, 0)
                u = jnp.where(keep, a * u_prev + u, u)
                a = jnp.where(keep, a * a_prev, a)
            a_scr[...] = a
            h_scr[pl.ds(r0, Tc), :] = u

            def group(gi, h_b):
                c0 = pl.multiple_of(gi * SUBLANES, SUBLANES)
                g0 = pl.multiple_of(r0 + gi * SUBLANES, SUBLANES)
                h8 = a_scr[pl.ds(c0, SUBLANES), :] * h_b + h_scr[pl.ds(g0, SUBLANES), :]
                h_scr[pl.ds(g0, SUBLANES), :] = h8
                return jnp.broadcast_to(h8[SUBLANES - 1:SUBLANES, :], (SUBLANES, D_A))

            h_b = lax.fori_loop(0, Tc // SUBLANES, group, h_b)
            ga = p_ref[b, pl.ds(r0, Tc), D_A:2 * D_A]
            gelu = 0.5 * ga * (1.0 + jnp.tanh(0.7978845608028654 * (ga + 0.044715 * ga * ga * ga)))
            y_ref[b, pl.ds(r0, Tc), :] = h_scr[pl.ds(r0, Tc), :] * gelu
            return h_b

        h_b = jnp.broadcast_to(h0_ref[b], (SUBLANES, D_A))
        lax.fori_loop(0, T // Tc, chunk, h_b)
        hlast_ref[b] = h_scr[Tv - 1:Tv, :]


def _rglru(pa3, cbuf, h0, cw, cb, wg, bg, lam, Tv, Bb):
    B, T, _ = pa3.shape
    Tc = _pick(T, (344, 256, 128, 64, 48, 32, 16, 8))
    const2 = lambda i: (0, 0)
    seq3 = lambda i: (i, 0, 0)
    return pl.pallas_call(
        functools.partial(_rglru_kernel, T=T, Tc=Tc, Tv=Tv, Bb=Bb),
        grid=(B // Bb,),
        in_specs=[pl.BlockSpec((Bb, T, PA_W), seq3), pl.BlockSpec((Bb, CONV_A - 1, D_A), seq3),
                  pl.BlockSpec((Bb, 1, D_A), seq3), pl.BlockSpec((CONV_A, D_A), const2),
                  pl.BlockSpec((1, D_A), const2), pl.BlockSpec((D_A, 2 * D_A), const2),
                  pl.BlockSpec((1, 2 * D_A), const2), pl.BlockSpec((1, D_A), const2)],
        out_specs=[pl.BlockSpec((Bb, T, D_A), seq3), pl.BlockSpec((Bb, CONV_A - 1, D_A), seq3),
                   pl.BlockSpec((Bb, 1, D_A), seq3)],
        out_shape=[_sds((B, T, D_A)), _sds((B, CONV_A - 1, D_A)), _sds((B, 1, D_A))],
        scratch_shapes=[pltpu.VMEM((T + 2 * SUBLANES, D_A), F32), pltpu.VMEM((Tc, D_A), F32),
                        pltpu.VMEM((T, D_A), F32)],
        compiler_params=_cparams("parallel"), name="rglru",
    )(pa3, cbuf, h0, cw, cb, wg, bg, lam)


B_HALO = 32


def _convb_kernel(p_ref, buf_ref, w_ref, cb_ref, g_ref, b_ref, y_ref, bnew_ref, u_scr, *, T, Tc, Tv, Bb):
    w = w_ref[...]
    cb = cb_ref[...]
    g = g_ref[...]
    bb = b_ref[...]
    first = B_HALO - (CONV_B - 1)
    for b in range(Bb):
        u_scr[0:first, :] = jnp.zeros((first, D_B), F32)
        u_scr[first:B_HALO, :] = buf_ref[b]
        u_scr[B_HALO:B_HALO + T, :] = p_ref[b, :, 0:D_B] * _sigmoid(p_ref[b, :, D_B:2 * D_B])
        bnew_ref[b] = u_scr[first + Tv:B_HALO + Tv, :]

        def chunk(ci):
            r0 = ci * Tc if isinstance(ci, int) else pl.multiple_of(ci * Tc, SUBLANES)
            win = u_scr[pl.ds(r0, Tc + B_HALO), :]
            shifted = [win] + [pltpu.roll(win, Tc + B_HALO - s, 0) for s in range(1, SUBLANES)]
            acc = cb
            for j in range(CONV_B):
                a, s = divmod(first + j, SUBLANES)
                acc = acc + w[j:j + 1] * shifted[s][a * SUBLANES:a * SUBLANES + Tc]
            yn = _ln(acc, g, bb)
            y_ref[b, pl.ds(r0, Tc), :] = yn * _sigmoid(yn)

        def trip(ti, carry):
            for u in range(CONV_TRIP):
                chunk(peel + CONV_TRIP * ti + u)
            return carry

        peel = (T // Tc) % CONV_TRIP
        for ci in range(peel):
            chunk(ci)
        lax.fori_loop(0, (T // Tc) // CONV_TRIP, trip, 0)


def _convb(pb3, buf, w, cb, g, b, Tv, Bb):
    B, T, _ = pb3.shape
    Tc = _pick(T, (48, 32, 16, 8))
    const2 = lambda i: (0, 0)
    seq3 = lambda i: (i, 0, 0)
    return pl.pallas_call(
        functools.partial(_convb_kernel, T=T, Tc=Tc, Tv=Tv, Bb=Bb),
        grid=(B // Bb,),
        in_specs=[pl.BlockSpec((Bb, T, PB_W), seq3), pl.BlockSpec((Bb, CONV_B - 1, D_B), seq3),
                  pl.BlockSpec((CONV_B, D_B), const2), pl.BlockSpec((1, D_B), const2),
                  pl.BlockSpec((1, D_B), const2), pl.BlockSpec((1, D_B), const2)],
        out_specs=[pl.BlockSpec((Bb, T, D_B), seq3), pl.BlockSpec((Bb, CONV_B - 1, D_B), seq3)],
        out_shape=[_sds((B, T, D_B)), _sds((B, CONV_B - 1, D_B))],
        scratch_shapes=[pltpu.VMEM((T + B_HALO, D_B), F32)],
        compiler_params=_cparams("parallel"), name="convb",
    )(pb3, buf, w, cb, g, b)


def _gla_kernel(p_ref, s0_ref, wg2_ref, bgate_ref, ng_ref, y_ref, snew_ref, s_scr, g_scr, *, T, C, Tb, Tv, Bb):
    ri = lax.broadcasted_iota(jnp.int32, (C, C), 0)
    ci_ = lax.broadcasted_iota(jnp.int32, (C, C), 1)
    tril = ri >= ci_
    lane_k = lax.broadcasted_iota(jnp.int32, (1, D_QK_C), 1)
    lane_v = lax.broadcasted_iota(jnp.int32, (1, D_C), 1)
    hm_k = [(lane_k >= h * DK_C) & (lane_k < (h + 1) * DK_C) for h in range(N_HEADS_C)]
    hm_v = [(lane_v >= h * DV_C) & (lane_v < (h + 1) * DV_C) for h in range(N_HEADS_C)]
    rs = lax.broadcasted_iota(jnp.int32, (D_C, D_QK_C), 0)
    cs = lax.broadcasted_iota(jnp.int32, (D_C, D_QK_C), 1)
    bd_t = (rs >= 0) & (rs < 0)
    for h in range(N_HEADS_C):
        bd_t = bd_t | ((rs >= h * DV_C) & (rs < (h + 1) * DV_C) & (cs >= h * DK_C) & (cs < (h + 1) * DK_C))
    rm = lax.broadcasted_iota(jnp.int32, (D_C, D_C), 0)
    cm = lax.broadcasted_iota(jnp.int32, (D_C, D_C), 1)
    seg = (rm >= 0) & (rm < 0)
    for h in range(N_HEADS_C):
        seg = seg | ((rm >= h * DV_C) & (rm < (h + 1) * DV_C) & (cm >= h * DV_C) & (cm < (h + 1) * DV_C))
    mseg = jnp.where(seg, 1.0, 0.0).astype(BF16)
    wg2 = wg2_ref[...].astype(BF16)
    bgate = bgate_ref[...]
    ng = ng_ref[...]
    rowi = lax.broadcasted_iota(jnp.int32, (C, 1), 0)
    tdims = (((0,), (0,)), ((), ()))

    n_chunks = T // C
    nt_dims = (((1,), (1,)), ((), ()))
    tril4 = jnp.concatenate([tril] * N_HEADS_C, axis=0)
    scan_shifts = [s for s in (1, 2, 4, 8, 16, 32) if s < C]
    scan_keep = [rowi >= s for s in scan_shifts]

    def tile_rows(ti, tb, base=0):
        r0 = ti * tb
        return pl.ds(base + (r0 if isinstance(ti, int) else pl.multiple_of(r0, tb)), tb)

    def gates(ti, carry):
        rows = tile_rows(ti, Tb)
        z = p_ref[rows, PC_Z:PC_Z + GATE_RANK]
        pre = _dot(z.astype(BF16), wg2) + bgate
        g = (jnp.minimum(pre, 0.0) - jnp.log1p(jnp.exp(-jnp.abs(pre)))) * (1.0 / GATE_TAU)
        rid = ti * Tb + lax.broadcasted_iota(jnp.int32, (Tb, 1), 0)
        if Bb > 1:
            rid = rid & (T - 1)
        g_scr[rows, :] = jnp.where(rid < Tv, g, 0.0)
        return carry
    lax.fori_loop(0, Bb * T // Tb, gates, 0)

    for b in range(Bb):
        def chunk(ci):
            rows = tile_rows(ci, C, b * T)
            q = p_ref[rows, PC_Q:PC_Q + D_QK_C] * (DK_C ** -0.5)
            k = p_ref[rows, PC_K:PC_K + D_QK_C]
            v = p_ref[rows, PC_V:PC_V + D_C]
            k = jnp.where((ci * C + rowi) < Tv, k, 0.0)
            gcum = g_scr[rows, :]
            for s, keep in zip(scan_shifts, scan_keep):
                gcum = gcum + jnp.where(keep, pltpu.roll(gcum, s, 0), 0.0)
            g_last = gcum[C - 1:C, :]
            g_mid = gcum[C // 2 - 1:C // 2, :]
            vb = v.astype(BF16)
            qt = q * jnp.exp(gcum - g_mid)
            ktb = (k * jnp.exp(g_mid - gcum)).astype(BF16)
            q4 = jnp.concatenate([jnp.where(hm_k[h], qt, 0.0) for h in range(N_HEADS_C)], axis=0).astype(BF16)
            sc = lax.dot_general(q4, ktb, nt_dims, preferred_element_type=F32)
            r4 = _dot(jnp.where(tril4, sc, 0.0).astype(BF16), vb)
            o = jnp.where(hm_v[0], r4[0:C], 0.0)
            for h in range(1, N_HEADS_C):
                o = o + jnp.where(hm_v[h], r4[h * C:(h + 1) * C], 0.0)
            kd = (k * jnp.exp(g_last - gcum)).astype(BF16)
            upd_t = lax.dot_general(vb, kd, tdims, preferred_element_type=F32)
            return rows, o, (q * jnp.exp(gcum)).astype(BF16), jnp.exp(g_last), jnp.where(bd_t, upd_t, 0.0)

        def apply_state(rows, o, qg, dec, upd):
            s_in = s_scr[...]
            y_ref[rows, :] = o + lax.dot_general(qg, s_in.astype(BF16), nt_dims, preferred_element_type=F32)
            s_scr[...] = s_in * dec + upd

        def trip(ti, carry):
            parts = [chunk(peel + GLA_TRIP * ti + u) for u in range(GLA_TRIP)]
            for part in parts:
                apply_state(*part)
            return carry

        s_scr[...] = jnp.zeros((D_C, D_QK_C), F32)
        for h in range(N_HEADS_C):
            s_scr[h * DV_C:(h + 1) * DV_C, h * DK_C:(h + 1) * DK_C] = s0_ref[b, h]
        peel = n_chunks % GLA_TRIP
        for ci in range(peel):
            apply_state(*chunk(ci))
        lax.fori_loop(0, n_chunks // GLA_TRIP, trip, 0)
        for h in range(N_HEADS_C):
            snew_ref[b, h] = s_scr[h * DV_C:(h + 1) * DV_C, h * DK_C:(h + 1) * DK_C]

    def finish(ti, carry):
        rows = tile_rows(ti, Tb)
        o = y_ref[rows, :]
        rg = p_ref[rows, PC_R:PC_R + D_C]
        o2_hi, o2_lo = _split_bf16(o * o)
        ms = (_dot(o2_hi, mseg) + _dot(o2_lo, mseg)) * (1.0 / DV_C)
        y_ref[rows, :] = o * lax.rsqrt(ms + LN_EPS) * ng * (rg * _sigmoid(rg))
        return carry
    lax.fori_loop(0, Bb * T // Tb, finish, 0)


def _gla(pc, B, T, s0, wg2, bgate, ng, Tv, Bb):
    assert Bb == 1 or T & (T - 1) == 0
    C = _pick(T, (48, 32, 16, 8))
    Tb = _pick(Bb * T, (344, 256, 128, 64, 48, 32, 16, 8))
    const2 = lambda i: (0, 0)
    row2 = lambda i: (i, 0)
    seq4 = lambda i: (i, 0, 0, 0)
    st = (Bb, N_HEADS_C, DV_C, DK_C)
    y, s_new_t = pl.pallas_call(
        functools.partial(_gla_kernel, T=T, C=C, Tb=Tb, Tv=Tv, Bb=Bb),
        grid=(B // Bb,),
        in_specs=[pl.BlockSpec((Bb * T, PC_W), row2), pl.BlockSpec(st, seq4),
                  pl.BlockSpec((GATE_RANK, D_QK_C), const2), pl.BlockSpec((1, D_QK_C), const2),
                  pl.BlockSpec((1, D_C), const2)],
        out_specs=[pl.BlockSpec((Bb * T, D_C), row2), pl.BlockSpec(st, seq4)],
        out_shape=[_sds((B * T, D_C)), _sds((B, N_HEADS_C, DV_C, DK_C))],
        scratch_shapes=[pltpu.VMEM((D_C, D_QK_C), F32), pltpu.VMEM((Bb * T, D_QK_C), F32)],
        compiler_params=_cparams("parallel"), name="gla",
    )(pc, jnp.swapaxes(s0, 2, 3), wg2, bgate, ng)
    return y, jnp.swapaxes(s_new_t, 2, 3)


def _tiles_to_rows(ref, lead, rows):
    return jnp.concatenate([ref[lead + (pl.ds(s, rows, stride=ROW_TILE), slice(None))]
                            for s in range(ROW_TILE)], axis=1)


def _rows_to_tiles(ref, lead, rows, val):
    for s in range(ROW_TILE):
        ref[lead + (pl.ds(s, rows, stride=ROW_TILE), slice(None))] = val[:, s * LANES:(s + 1) * LANES]


def _outproj_kernel(ya_ref, yb_ref, yc_ref, x_ref, w_ref, g_ref, b_ref, *rest, alpha, n_blocks):
    h_ref = rest[-1]

    @pl.when(pl.program_id(0) < n_blocks)
    def _():
        y_cat = jnp.concatenate([ya_ref[...].astype(BF16), yb_ref[...].astype(BF16), yc_ref[...].astype(BF16)],
                                axis=1)
        y = _dot(y_cat, w_ref[...])
        h = _ln(alpha * x_ref[...] + y, g_ref[...], b_ref[...])
        _rows_to_tiles(h_ref, (), h.shape[0], h)

    @pl.when(pl.program_id(0) >= n_blocks)
    def _():
        h_ref[...] = jnp.zeros(h_ref.shape, F32)


def _outproj(ya, yb, yc, x, row0, w, g, b, alpha, tm, out_rows, out_row0=0, into=None):
    n = ya.shape[0]
    nb = n // tm
    assert n % tm == 0 and row0 % tm == 0 and out_row0 % tm == 0 and out_rows % tm == 0
    off, out_off = row0 // tm, out_row0 // tm
    grid = nb if into is not None else out_rows // tm
    row = lambda i: (jnp.minimum(i, nb - 1), 0)
    const = lambda i: (0, 0)
    in_specs = [pl.BlockSpec((tm, D_A), row), pl.BlockSpec((tm, D_B), row), pl.BlockSpec((tm, D_C), row),
                pl.BlockSpec((tm, D_MODEL), lambda i: (jnp.minimum(i, nb - 1) + off, 0)),
                pl.BlockSpec((D_MODEL, D_MODEL), const), pl.BlockSpec((1, D_MODEL), const),
                pl.BlockSpec((1, D_MODEL), const)]
    args = [ya, yb, yc, x, w, g, b]
    aliases = {}
    if into is not None:
        in_specs.append(pl.BlockSpec(memory_space=pl.ANY))
        args.append(into)
        aliases = {len(args) - 1: 0}
    return pl.pallas_call(
        functools.partial(_outproj_kernel, alpha=alpha, n_blocks=nb),
        grid=(grid,),
        in_specs=in_specs,
        out_specs=pl.BlockSpec((tm * ROW_TILE, LANES), lambda i: (i + out_off, 0)),
        out_shape=_sds((out_rows * ROW_TILE, LANES)),
        input_output_aliases=aliases,
        compiler_params=_cparams("arbitrary"), name="outproj",
    )(*args)


def _router_kernel(h_ref, wt_ref, b_ref, idx_ref, gate_ref, cnt_ref):
    nt = (((1,), (1,)), ((), ()))
    hh, hl = _split_bf16(_tiles_to_rows(h_ref, (), idx_ref.shape[1]))
    wh, wl = _split_bf16(wt_ref[...])
    logits = (lax.dot_general(wh, hh, nt, preferred_element_type=F32)
              + lax.dot_general(wh, hl, nt, preferred_element_type=F32)
              + lax.dot_general(wl, hh, nt, preferred_element_type=F32)) + b_ref[...]
    eid = lax.broadcasted_iota(jnp.int32, logits.shape, 0)
    vals = []
    member = jnp.zeros(logits.shape, F32)
    for k in range(TOP_K):
        m = jnp.max(logits, axis=0, keepdims=True)
        sel = jnp.min(jnp.where(logits == m, eid, N_EXPERTS), axis=0, keepdims=True)
        idx_ref[k:k + 1, :] = sel
        vals.append(m)
        hit = eid == sel
        member = jnp.where(hit, 1.0, member)
        logits = jnp.where(hit, -jnp.inf, logits)
    es = [jnp.exp(v - vals[0]) for v in vals]
    tot = es[0] + es[1] + es[2] + es[3]
    for k in range(TOP_K):
        gate_ref[k:k + 1, :] = es[k] / tot
    cnt_ref[...] = jnp.broadcast_to(jnp.sum(member, axis=1, keepdims=True), cnt_ref.shape)


def _router_tile(n):
    return _pick(n, (896, 640, 512, 384, 256, 128))


def _router(h, n, wt, b):
    tm = _router_tile(n)
    nt = n // tm
    return pl.pallas_call(
        _router_kernel,
        grid=(nt,),
        in_specs=[pl.BlockSpec((tm * ROW_TILE, LANES), lambda i: (i, 0)),
                  pl.BlockSpec((N_EXPERTS, D_MODEL), lambda i: (0, 0)),
                  pl.BlockSpec((N_EXPERTS, 1), lambda i: (0, 0))],
        out_specs=[pl.BlockSpec((TOP_K, tm), lambda i: (0, i)), pl.BlockSpec((TOP_K, tm), lambda i: (0, i)),
                   pl.BlockSpec((N_EXPERTS, LANES), lambda i: (0, i))],
        out_shape=[_sds((TOP_K, n), jnp.int32), _sds((TOP_K, n)), _sds((N_EXPERTS, nt * LANES))],
        compiler_params=_cparams("parallel"), name="router",
    )(h, wt, b)


def _dest_kernel(idx_ref, base_ref, dest_ref):
    tm = idx_ref.shape[1]
    eid = lax.broadcasted_iota(jnp.int32, (N_EXPERTS, tm), 0)
    hits = [eid == idx_ref[k:k + 1, :] for k in range(TOP_K)]
    member = jnp.zeros((N_EXPERTS, tm), F32)
    for k in range(TOP_K):
        member = jnp.where(hits[k], 1.0, member)
    earlier = (lax.broadcasted_iota(jnp.int32, (tm, tm), 0) < lax.broadcasted_iota(jnp.int32, (tm, tm), 1))
    rank = _dot(member.astype(BF16), jnp.where(earlier, 1.0, 0.0).astype(BF16))
    pos = base_ref[...] + rank
    for k in range(TOP_K):
        dest_ref[k:k + 1, :] = jnp.sum(jnp.where(hits[k], pos, 0.0), axis=0, keepdims=True).astype(jnp.int32)


def _dest(top_idx, base):
    n = top_idx.shape[1]
    tm = _router_tile(n)
    return pl.pallas_call(
        _dest_kernel,
        grid=(n // tm,),
        in_specs=[pl.BlockSpec((TOP_K, tm), lambda i: (0, i)),
                  pl.BlockSpec((None, N_EXPERTS, 1), lambda i: (i, 0, 0))],
        out_specs=pl.BlockSpec((TOP_K, tm), lambda i: (0, i)),
        out_shape=_sds((TOP_K, n), jnp.int32),
        compiler_params=_cparams("parallel"), name="moe_dest",
    )(top_idx, base)


def _invert(pad_shift, dest, bm):
    k, n = dest.shape
    n_rows = pad_shift.shape[0] * bm
    ch = next(c for c in range(min(n, 4096) // SC_LANES * SC_LANES, 0, -SC_LANES) if n % c == 0)
    fill = (jnp.arange(n_rows, dtype=jnp.int32).reshape(-1, bm) + pad_shift[:, None]).reshape(-1)
    mesh = plsc.VectorSubcoreMesh(core_axis_name="c", subcore_axis_name="s", num_cores=SC_CORES,
                                  num_subcores=SC_SUBCORES)

    @functools.partial(
        pl.kernel, mesh=mesh, out_type=_sds((n_rows,), jnp.int32),
        scratch_types=[pltpu.VMEM((n_rows,), jnp.int32), pltpu.VMEM((ch,), jnp.int32)],
        compiler_params=pltpu.CompilerParams(needs_layout_passes=False), name="moe_invert_sc")
    def run(fill_hbm, dest_hbm, inv_hbm, inv_v, d_v):
        first = jnp.logical_and(lax.axis_index("c") == 0, lax.axis_index("s") == 0)

        @pl.when(first)
        def _():
            pltpu.sync_copy(fill_hbm, inv_v)
            lanes = lax.iota(jnp.int32, SC_LANES)
            for slot in range(k):
                def chunk(c, carry):
                    pltpu.sync_copy(dest_hbm.at[pl.ds(slot * n + c * ch, ch)], d_v)

                    def vec(v, carry):
                        idx = d_v[pl.ds(v * SC_LANES, SC_LANES)]
                        plsc.store_scatter(inv_v, [idx], (c * ch + v * SC_LANES + lanes) * k + slot)
                        return carry
                    return lax.fori_loop(0, ch // SC_LANES, vec, carry)
                lax.fori_loop(0, n // ch, chunk, 0)
            pltpu.sync_copy(inv_v, inv_hbm)

    return run(fill, dest.reshape(-1))


def _moe_kernel(be_ref, nu_ref, inv_ref, h_hbm, w1f_ref, b1_ref, w2f_ref, b2_ref, y4_hbm,
                xbuf, xb, obuf, w1_ref, w2_ref, gsem, ssem, *, bm):
    i = pl.program_id(0)
    n_used = nu_ref[0]
    last_blk = pl.num_programs(0) - 1
    n_ff = D_FF // FF_CHUNK

    def tile(r):
        start = r * ROW_TILE
        return pl.ds(start if isinstance(r, int) else pl.multiple_of(start, ROW_TILE), ROW_TILE)

    def gather_row(blk, slot, j, r=None):
        r = inv_ref[blk * bm + j] if r is None else r
        t = lax.shift_right_logical(r, 2)
        pltpu.make_async_copy(h_hbm.at[tile(t), :], xbuf.at[slot, tile(j), :], gsem.at[slot]).start()

    def scatter_row(blk, slot, j, r=None):
        r = inv_ref[blk * bm + j] if r is None else r
        pltpu.make_async_copy(obuf.at[slot, tile(j), :], y4_hbm.at[tile(r), :], ssem.at[slot]).start()

    def wait_gather(slot):
        pltpu.make_async_copy(h_hbm.at[pl.ds(0, bm * ROW_TILE), :], xbuf.at[slot], gsem.at[slot]).wait()

    def wait_scatter(slot):
        pltpu.make_async_copy(obuf.at[slot], y4_hbm.at[pl.ds(0, bm * ROW_TILE), :], ssem.at[slot]).wait()

    def loop_rows(fn, blk, slot):
        def body(j, c):
            fn(blk, slot, j)
            return c
        lax.fori_loop(0, bm, body, 0)

    def issue_rows(fn, blk, slot, j_lo, j_hi):
        for j0 in range(j_lo, j_hi, DMA_GROUP):
            js = range(j0, j0 + DMA_GROUP)
            ids = [inv_ref[blk * bm + j] for j in js]
            for j, r in zip(js, ids):
                fn(blk, slot, j, r)

    def step():
        slot = i % 2

        @pl.when(i + 1 < n_used)
        def _():
            issue_rows(gather_row, i + 1, 1 - slot, 0, bm)
        wait_gather(slot)
        xb[...] = _tiles_to_rows(xbuf, (slot,), bm).astype(BF16)
        acc = jnp.zeros((bm, D_MODEL), F32)
        for c in range(n_ff):
            lo = c * FF_CHUNK
            x = xb[...]
            hg = _dot(x, w1_ref[:, lo:lo + FF_CHUNK]) + b1_ref[:, lo:lo + FF_CHUNK]
            hl = _dot(x, w1_ref[:, D_FF + lo:D_FF + lo + FF_CHUNK]) + b1_ref[:, D_FF + lo:D_FF + lo + FF_CHUNK]
            gate = jnp.minimum(hg, SWIGLU_LIMIT)
            lin = jnp.clip(hl, -SWIGLU_LIMIT, SWIGLU_LIMIT)
            act = gate * _sigmoid(SWIGLU_ALPHA * gate) * (lin + 1.0)
            acc = acc + _dot(act.astype(BF16), w2_ref[lo:lo + FF_CHUNK, :])

        @pl.when(i >= 2)
        def _():
            wait_scatter(slot)
        _rows_to_tiles(obuf, (slot,), bm, acc + b2_ref[...])
        issue_rows(scatter_row, i, slot, 0, bm)

    new_expert = jnp.logical_or(i == 0, be_ref[i] != be_ref[jnp.maximum(i - 1, 0)])

    @pl.when(jnp.logical_and(new_expert, i < n_used))
    def _():
        def cast_rows(r, c):
            rows = pl.ds(pl.multiple_of(r * W_CAST_ROWS, W_CAST_ROWS), W_CAST_ROWS)
            w1_ref[rows, :] = w1f_ref[rows, :].astype(BF16)
            w2_ref[rows, :] = w2f_ref[rows, :].astype(BF16)
            return c
        lax.fori_loop(0, D_MODEL // W_CAST_ROWS, cast_rows, 0)

    @pl.when(jnp.logical_and(i == 0, n_used > 0))
    def _():
        loop_rows(gather_row, 0, 0)

    @pl.when(i < n_used)
    def _():
        step()

    @pl.when(i == n_used - 1)
    def _():
        slot = i % 2
        wait_scatter(slot)

        @pl.when(i >= 1)
        def _():
            wait_scatter(1 - slot)
        obuf[0] = jnp.zeros((bm * ROW_TILE, LANES), F32)

        def zero_block(blk, c):
            rows = bm * ROW_TILE
            cp = pltpu.make_async_copy(obuf.at[0], y4_hbm.at[pl.ds(pl.multiple_of(blk * rows, rows), rows), :],
                                       ssem.at[0])
            cp.start()
            cp.wait()
            return c
        lax.fori_loop(n_used, last_blk + 1, zero_block, 0)


def _moe(block_e, n_used, inv, h, layer, w1, b1, w2, b2, bm):
    nblk = block_e.shape[0]
    wmap = lambda i, be, nu, iv: (layer, be[i], 0, 0)
    return pl.pallas_call(
        functools.partial(_moe_kernel, bm=bm),
        grid_spec=pltpu.PrefetchScalarGridSpec(
            num_scalar_prefetch=3, grid=(nblk,),
            in_specs=[pl.BlockSpec(memory_space=pl.ANY),
                      pl.BlockSpec((None, None, D_MODEL, 2 * D_FF), wmap),
                      pl.BlockSpec((None, None, 1, 2 * D_FF), wmap),
                      pl.BlockSpec((None, None, D_FF, D_MODEL), wmap),
                      pl.BlockSpec((None, None, 1, D_MODEL), wmap)],
            out_specs=pl.BlockSpec(memory_space=pl.ANY),
            scratch_shapes=[pltpu.VMEM((2, bm * ROW_TILE, LANES), F32), pltpu.VMEM((bm, D_MODEL), BF16),
                            pltpu.VMEM((2, bm * ROW_TILE, LANES), F32),
                            pltpu.VMEM((D_MODEL, 2 * D_FF), BF16), pltpu.VMEM((D_FF, D_MODEL), BF16),
                            pltpu.SemaphoreType.DMA((2,)), pltpu.SemaphoreType.DMA((2,))]),
        out_shape=_sds((nblk * bm * ROW_TILE, LANES)),
        compiler_params=_cparams("arbitrary"), name="moe_experts",
    )(block_e, n_used, inv, h, w1, b1, w2, b2)


def _combine_kernel(gates_ref, y4_ref, h_ref, g_ref, b_ref, out_ref, acc_scr, *, alpha, tok_off):
    tm = out_ref.shape[0]
    tok0 = tok_off + pl.program_id(0) * tm

    def group(gi, carry):
        t0 = pl.multiple_of(gi * SUBLANES, SUBLANES)
        tiles = []
        for u in range(SUBLANES):
            t = t0 + u
            tile = alpha * h_ref[pl.ds(pl.multiple_of(t * ROW_TILE, ROW_TILE), ROW_TILE), :]
            for k in range(TOP_K):
                r = (t * TOP_K + k) * ROW_TILE
                tile = tile + (gates_ref[(tok0 + t) * TOP_K + k]
                               * y4_ref[pl.ds(pl.multiple_of(r, ROW_TILE), ROW_TILE), :])
            tiles.append(tile.reshape(1, ROW_TILE, LANES))
        rows = jnp.swapaxes(jnp.concatenate(tiles, axis=0), 0, 1)
        for s in range(ROW_TILE):
            acc_scr[pl.ds(t0, SUBLANES), s * LANES:(s + 1) * LANES] = rows[s]
        return carry
    lax.fori_loop(0, tm // SUBLANES, group, 0)
    out_ref[...] = _ln(acc_scr[...], g_ref[...], b_ref[...])


def _combine(y4, gates_flat, h, row0, n, g, b, alpha):
    tm = _pick(math.gcd(n, row0) if row0 else n, (448, 384, 224, 128, 64, 32, 16, 8))
    off = row0 // tm
    row = lambda i, gt: (i + off, 0)
    const = lambda i, gt: (0, 0)
    return pl.pallas_call(
        functools.partial(_combine_kernel, alpha=alpha, tok_off=row0),
        grid_spec=pltpu.PrefetchScalarGridSpec(
            num_scalar_prefetch=1, grid=(n // tm,),
            in_specs=[pl.BlockSpec((TOP_K * tm * ROW_TILE, LANES), row),
                      pl.BlockSpec((tm * ROW_TILE, LANES), row),
                      pl.BlockSpec((1, D_MODEL), const), pl.BlockSpec((1, D_MODEL), const)],
            out_specs=pl.BlockSpec((tm, D_MODEL), lambda i, gt: (i, 0)),
            scratch_shapes=[pltpu.VMEM((tm, D_MODEL), F32)]),
        out_shape=_sds((n, D_MODEL)),
        compiler_params=_cparams("arbitrary"), name="moe_combine",
    )(gates_flat, y4, h, g, b)


def _group_layout(cnt_tiles, nblk, bm):
    cnt = cnt_tiles.astype(jnp.int32)
    counts = jnp.sum(cnt, axis=1)
    padded = (counts + bm - 1) // bm * bm
    pad_end = jnp.cumsum(padded)
    pad_start = pad_end - padded
    base = pad_start[:, None] + jnp.cumsum(cnt, axis=1) - cnt
    starts = jnp.arange(nblk, dtype=jnp.int32) * bm
    groups_before = jnp.sum((pad_end[None, :] <= starts[:, None]).astype(jnp.int32), axis=1)
    block_e = jnp.minimum(groups_before, N_EXPERTS - 1).astype(jnp.int32)
    n_used = (pad_end[-1] // bm).astype(jnp.int32).reshape(1)
    total = jnp.sum(counts)
    real_before = jnp.concatenate([jnp.cumsum(counts), total.reshape(1)])[groups_before]
    pad_shift = (total - real_before).astype(jnp.int32)
    return base.T.astype(F32)[:, :, None], pad_shift, block_e, n_used


def _pack_w_in(w):
    o = 0
    xa, o = w[:, o:o + D_A], o + D_A
    ga, o = w[:, o:o + D_A], o + D_A
    vb, o = w[:, o:o + D_B], o + D_B
    gb, o = w[:, o:o + D_B], o + D_B
    q, o = w[:, o:o + D_QK_C], o + D_QK_C
    k, o = w[:, o:o + D_QK_C], o + D_QK_C
    v, o = w[:, o:o + D_C], o + D_C
    r, o = w[:, o:o + D_C], o + D_C
    z = w[:, o:o + GATE_RANK]
    zq = jnp.zeros((w.shape[0], PC_K - PC_Z - GATE_RANK), w.dtype)
    zk = jnp.zeros((w.shape[0], PC_V - PC_K - D_QK_C), w.dtype)
    return jnp.concatenate([xa, ga, vb, gb, q, z, zq, k, zk, v, r], axis=1).astype(BF16)


def _block_diag(w):
    eye = jnp.eye(N_BLK_A, dtype=w.dtype)
    return jnp.einsum("hij,hg->higj", w, eye).reshape(D_A, D_A)


def kernel(x_prompt, x_sample, state_conv_a, state_rglru, state_conv_b, state_gla, meta_tokens, ln0_g, ln0_b,
           w_in, conv_a_w, conv_a_b, w_rg, b_rg, w_ig, b_ig, lru_lambda, conv_b_w, conv_b_b, ln_b_g, ln_b_b,
           w_gate2, b_gate, gla_norm_g, w_out, ln1_g, ln1_b, router_w, router_b, moe_w1, moe_b1, moe_w2, moe_b2,
           ln2_g, ln2_b):
    bp, seq, _ = x_prompt.shape
    bs, dseq, _ = x_sample.shape
    depth = w_in.shape[0]
    tp = N_META + seq
    ts = SAMPLE_PAD_T
    np_rows = bp * tp
    ns_rows = bs * dseq
    n = np_rows + ns_rows
    alpha = (2 * depth) ** 0.25
    row = lambda a: a.reshape(1, -1)

    meta = jnp.broadcast_to(meta_tokens[None], (bp, N_META, D_MODEL))
    xp_in = jnp.concatenate([meta, x_prompt], axis=1).reshape(np_rows, D_MODEL)
    xs_in = jnp.pad(x_sample, ((0, 0), (0, ts - dseq), (0, 0))).reshape(bs * ts, D_MODEL)
    zeros_p = (jnp.zeros((bp, CONV_A - 1, D_A), F32), jnp.zeros((bp, 1, D_A), F32),
               jnp.zeros((bp, CONV_B - 1, D_B), F32), jnp.zeros((bp, N_HEADS_C, DK_C, DV_C), F32))
    sb = _pick(bs, (8, 4, 2, 1))
    tm_p = _pick(np_rows, (384, 512, 256, 128, 64, 32, 16, 8))
    tm_s = _pick(math.gcd(ns_rows, np_rows), (128, 64, 32, 16, 8))
    nblk = -(-(TOP_K * n) // MOE_BM) + N_EXPERTS
    while (nblk * MOE_BM // TOP_K) % math.lcm(tm_p, tm_s):
        nblk += 1

    new_p = ([], [], [], [])
    new_s = ([], [], [], [])
    x_all = None
    for l in range(depth):
        w_packed = _pack_w_in(w_in[l])
        wg = jnp.concatenate([_block_diag(w_rg[l]), _block_diag(w_ig[l])], axis=1).astype(BF16)
        bg = jnp.concatenate([b_rg[l], b_ig[l]]).reshape(1, -1)
        w_out_b = w_out[l].astype(BF16)
        ng = row(jnp.tile(gla_norm_g[l], N_HEADS_C))
        first = l == 0
        if first:
            pa_p, pb_p, pc_p, xn_p = _inproj(xp_in, 0, np_rows, row(ln0_g), row(ln0_b), w_packed, True)
            pa_s, pb_s, pc_s, xn_s = _inproj(xs_in, 0, bs * ts, row(ln0_g), row(ln0_b), w_packed, True)
            res_p, res_p_row0 = xn_p, 0
        else:
            pa_p, pb_p, pc_p = _inproj(x_all, 0, np_rows, row(ln0_g), row(ln0_b), w_packed, False)
            xn_s = jnp.pad(x_all[np_rows:].reshape(bs, dseq, D_MODEL),
                           ((0, 0), (0, ts - dseq), (0, 0))).reshape(bs * ts, D_MODEL)
            pa_s, pb_s, pc_s = _inproj(xn_s, 0, bs * ts, row(ln0_g), row(ln0_b), w_packed, False)
            res_p, res_p_row0 = x_all, 0

        outs = []
        for (pa, pb, pc, nb, t, tv, bb, st) in (
                (pa_p, pb_p, pc_p, bp, tp, tp, 1, zeros_p),
                (pa_s, pb_s, pc_s, bs, ts, dseq, sb,
                 (state_conv_a[l], state_rglru[l].reshape(bs, 1, D_A), state_conv_b[l], state_gla[l]))):
            ya, ca_new, h_last = _rglru(pa.reshape(nb, t, PA_W), st[0], st[1], conv_a_w[l], row(conv_a_b[l]),
                                        wg, bg, row(lru_lambda[l]), tv, bb)
            yb, cb_new = _convb(pb.reshape(nb, t, PB_W), st[2], conv_b_w[l], row(conv_b_b[l]),
                                row(ln_b_g[l]), row(ln_b_b[l]), tv, bb)
            yc, s_new = _gla(pc, nb, t, st[3], w_gate2[l], row(b_gate[l]), ng, tv, bb)
            outs.append((ya.reshape(nb * t, D_A), yb.reshape(nb * t, D_B), yc,
                         ca_new, h_last.reshape(nb, D_A), cb_new, s_new))
        (ya_p, yb_p, yc_p, *st_p), (ya_s, yb_s, yc_s, *st_s) = outs
        for j in range(4):
            new_p[j].append(st_p[j])
            new_s[j].append(st_s[j])

        valid = lambda a: a.reshape(bs, ts, -1)[:, :dseq].reshape(ns_rows, -1)
        h_all = _outproj(ya_p, yb_p, yc_p, res_p, res_p_row0, w_out_b, row(ln1_g[l]), row(ln1_b[l]), alpha,
                         tm_p, nblk * MOE_BM // TOP_K)
        h_all = _outproj(valid(ya_s), valid(yb_s), valid(yc_s), valid(xn_s), 0, w_out_b, row(ln1_g[l]),
                         row(ln1_b[l]), alpha, tm_s, nblk * MOE_BM // TOP_K, out_row0=np_rows, into=h_all)

        top_idx, gates, cnt = _router(h_all, n, router_w[l].T, router_b[l].reshape(-1, 1))
        base, pad_shift, block_e, n_used = _group_layout(cnt[:, ::LANES], nblk, MOE_BM)
        dest = _dest(top_idx, base)
        inv = _invert(pad_shift, dest, MOE_BM)
        y4 = _moe(block_e, n_used, inv, h_all, l, moe_w1, moe_b1[:, :, None, :], moe_w2, moe_b2[:, :, None, :],
                  MOE_BM)
        comb = functools.partial(_combine, y4, gates.T.reshape(-1), h_all, g=row(ln2_g[l]), b=row(ln2_b[l]),
                                 alpha=alpha)
        if l + 1 < depth:
            x_all = comb(0, n)
        else:
            out_p, out_s = comb(0, np_rows), comb(np_rows, ns_rows)

    y_p = out_p.reshape(bp, tp, D_MODEL)[:, N_META:]
    y_s = out_s.reshape(bs, dseq, D_MODEL)
    return (y_p, y_s,
            jnp.stack(new_p[0]), jnp.stack(new_p[1]), jnp.stack(new_p[2]), jnp.stack(new_p[3]),
            jnp.stack(new_s[0]), jnp.stack(new_s[1]), jnp.stack(new_s[2]), jnp.stack(new_s[3]))
```

```python
import functools
import math

import jax
import jax.numpy as jnp
from jax import lax
from jax.experimental import pallas as pl
from jax.experimental.pallas import tpu as pltpu
from jax.experimental.pallas import tpu_sc as plsc

F32 = jnp.float32
BF16 = jnp.bfloat16

D_MODEL = 1024
N_META = 16
D_A = 384
D_B = 256
D_C = 384
N_BLK_A = 8
BLK_A = D_A // N_BLK_A
CONV_A = 4
RG_C = 8.0
CONV_B = 31
N_HEADS_C = 4
DV_C = D_C // N_HEADS_C
DK_C = DV_C // 2
D_QK_C = N_HEADS_C * DK_C
GATE_RANK = 16
GATE_TAU = 16.0
N_EXPERTS = 32
TOP_K = 4
D_FF = D_MODEL
SWIGLU_LIMIT = 7.0
SWIGLU_ALPHA = 1.702
LN_EPS = 1e-5

PA_W = 2 * D_A
PB_W = 2 * D_B
PC_Q, PC_Z, PC_K, PC_V, PC_R, PC_W = 0, 192, 256, 512, 896, 1280
P_W = PA_W + PB_W + PC_W

SUBLANES = 8
LANES = 128
SC_LANES = 16
SC_CORES, SC_SUBCORES = 2, 16
ROW_TILE = D_MODEL // LANES
VMEM_LIMIT_BYTES = 56 * 1024 * 1024
MOE_BM = 512
FF_CHUNK = 512
DMA_GROUP = 8
W_CAST_ROWS = 128
GLA_TRIP = 14
CONV_TRIP = 7
SAMPLE_PAD_T = 8


def _cparams(*sem):
    return pltpu.CompilerParams(dimension_semantics=sem, vmem_limit_bytes=VMEM_LIMIT_BYTES)


def _sds(shape, dtype=F32):
    return jax.ShapeDtypeStruct(shape, dtype)


def _pick(n, prefs):
    for p in prefs:
        if n % p == 0:
            return p
    raise ValueError(f"no tile for {n} in {prefs}")


def _ln(x, g, b):
    mu = jnp.mean(x, axis=-1, keepdims=True)
    xc = x - mu
    var = jnp.mean(xc * xc, axis=-1, keepdims=True)
    return xc * lax.rsqrt(var + LN_EPS) * g + b


def _sigmoid(x):
    return 1.0 / (1.0 + jnp.exp(-x))


def _split_bf16(x):
    hi = x.astype(BF16)
    lo = (x - hi.astype(F32)).astype(BF16)
    return hi, lo


def _dot(a, b):
    return jnp.dot(a, b, preferred_element_type=F32)


def _inproj_kernel(x_ref, g_ref, b_ref, w_ref, pa_ref, pb_ref, pc_ref, *maybe_xn, apply_ln):
    x = x_ref[...]
    if apply_ln:
        x = _ln(x, g_ref[...], b_ref[...])
        maybe_xn[0][...] = x
    xb = x.astype(BF16)
    pa_ref[...] = _dot(xb, w_ref[:, 0:PA_W])
    pb_ref[...] = _dot(xb, w_ref[:, PA_W:PA_W + PB_W])
    pc_ref[...] = _dot(xb, w_ref[:, PA_W + PB_W:P_W])


def _inproj(x, row0, nrows, ln_g, ln_b, w_packed, apply_ln):
    tm = _pick(nrows, (384, 512, 256, 128, 64, 32, 16, 8))
    while row0 % tm:
        tm //= 2
    off = row0 // tm
    const = lambda i: (0, 0)
    row = lambda i: (i, 0)
    out_shape = [_sds((nrows, PA_W)), _sds((nrows, PB_W)), _sds((nrows, PC_W))]
    out_specs = [pl.BlockSpec((tm, PA_W), row), pl.BlockSpec((tm, PB_W), row), pl.BlockSpec((tm, PC_W), row)]
    if apply_ln:
        out_shape.append(_sds((nrows, D_MODEL)))
        out_specs.append(pl.BlockSpec((tm, D_MODEL), row))
    return pl.pallas_call(
        functools.partial(_inproj_kernel, apply_ln=apply_ln),
        grid=(nrows // tm,),
        in_specs=[pl.BlockSpec((tm, D_MODEL), lambda i: (i + off, 0)),
                  pl.BlockSpec((1, D_MODEL), const), pl.BlockSpec((1, D_MODEL), const),
                  pl.BlockSpec((D_MODEL, P_W), const)],
        out_specs=out_specs, out_shape=out_shape,
        compiler_params=_cparams("parallel"), name="inproj",
    )(x, ln_g, ln_b, w_packed)


def _rglru_kernel(p_ref, cbuf_ref, h0_ref, cw_ref, cb_ref, wg_ref, bg_ref, lam_ref,
                  y_ref, cnew_ref, hlast_ref, xp_scr, a_scr, h_scr, *, T, Tc, Tv, Bb):
    lam = lam_ref[...]
    softplus_neg = jnp.maximum(-lam, 0.0) + jnp.log1p(jnp.exp(-jnp.abs(lam)))
    c_decay = -RG_C * softplus_neg
    cw = cw_ref[...]
    cb = cb_ref[...]
    bg = bg_ref[...]
    sub = lax.broadcasted_iota(jnp.int32, (Tc, D_A), 0) % SUBLANES
    halo = SUBLANES - (CONV_A - 1)
    for b in range(Bb):
        xp_scr[halo:SUBLANES, :] = cbuf_ref[b]
        xp_scr[SUBLANES:SUBLANES + T, :] = p_ref[b, :, 0:D_A]
        cnew_ref[b] = xp_scr[halo + Tv:SUBLANES + Tv, :]

        def chunk(ci, h_b):
            r0 = pl.multiple_of(ci * Tc, SUBLANES)
            win = xp_scr[pl.ds(r0, Tc + SUBLANES), :]
            xc = cb + cw[0:1] * win[halo:halo + Tc]
            for j in range(1, CONV_A):
                xc = xc + cw[j:j + 1] * win[halo + j:halo + j + Tc]
            gates = _dot(xc.astype(BF16), wg_ref[...]) + bg
            r = _sigmoid(gates[:, 0:D_A])
            i = _sigmoid(gates[:, D_A:2 * D_A])
            log_a = c_decay * r
            a = jnp.exp(log_a)
            u = jnp.sqrt(1.0 - a * a) * (i * xc)
            for s in (1, 2, 4):
                keep = sub >= s
                a_prev = pltpu.roll(a, s, 0)
                u_prev = pltpu.roll(u, s, 0)
                u = jnp.where(keep, a * u_prev + u, u)
                a = jnp.where(keep, a * a_prev, a)
            a_scr[...] = a
            h_scr[pl.ds(r0, Tc), :] = u

            def group(gi, h_b):
                c0 = pl.multiple_of(gi * SUBLANES, SUBLANES)
                g0 = pl.multiple_of(r0 + gi * SUBLANES, SUBLANES)
                h8 = a_scr[pl.ds(c0, SUBLANES), :] * h_b + h_scr[pl.ds(g0, SUBLANES), :]
                h_scr[pl.ds(g0, SUBLANES), :] = h8
                return jnp.broadcast_to(h8[SUBLANES - 1:SUBLANES, :], (SUBLANES, D_A))

            h_b = lax.fori_loop(0, Tc // SUBLANES, group, h_b)
            ga = p_ref[b, pl.ds(r0, Tc), D_A:2 * D_A]
            gelu = 0.5 * ga * (1.0 + jnp.tanh(0.7978845608028654 * (ga + 0.044715 * ga * ga * ga)))
            y_ref[b, pl.ds(r0, Tc), :] = h_scr[pl.ds(r0, Tc), :] * gelu
            return h_b

        h_b = jnp.broadcast_to(h0_ref[b], (SUBLANES, D_A))
        lax.fori_loop(0, T // Tc, chunk, h_b)
        hlast_ref[b] = h_scr[Tv - 1:Tv, :]


def _rglru(pa3, cbuf, h0, cw, cb, wg, bg, lam, Tv, Bb):
    B, T, _ = pa3.shape
    Tc = _pick(T, (344, 256, 128, 64, 48, 32, 16, 8))
    const2 = lambda i: (0, 0)
    seq3 = lambda i: (i, 0, 0)
    return pl.pallas_call(
        functools.partial(_rglru_kernel, T=T, Tc=Tc, Tv=Tv, Bb=Bb),
        grid=(B // Bb,),
        in_specs=[pl.BlockSpec((Bb, T, PA_W), seq3), pl.BlockSpec((Bb, CONV_A - 1, D_A), seq3),
                  pl.BlockSpec((Bb, 1, D_A), seq3), pl.BlockSpec((CONV_A, D_A), const2),
                  pl.BlockSpec((1, D_A), const2), pl.BlockSpec((D_A, 2 * D_A), const2),
                  pl.BlockSpec((1, 2 * D_A), const2), pl.BlockSpec((1, D_A), const2)],
        out_specs=[pl.BlockSpec((Bb, T, D_A), seq3), pl.BlockSpec((Bb, CONV_A - 1, D_A), seq3),
                   pl.BlockSpec((Bb, 1, D_A), seq3)],
        out_shape=[_sds((B, T, D_A)), _sds((B, CONV_A - 1, D_A)), _sds((B, 1, D_A))],
        scratch_shapes=[pltpu.VMEM((T + 2 * SUBLANES, D_A), F32), pltpu.VMEM((Tc, D_A), F32),
                        pltpu.VMEM((T, D_A), F32)],
        compiler_params=_cparams("parallel"), name="rglru",
    )(pa3, cbuf, h0, cw, cb, wg, bg, lam)


B_HALO = 32


def _convb_kernel(p_ref, buf_ref, w_ref, cb_ref, g_ref, b_ref, y_ref, bnew_ref, u_scr, *, T, Tc, Tv, Bb):
    w = w_ref[...]
    cb = cb_ref[...]
    g = g_ref[...]
    bb = b_ref[...]
    first = B_HALO - (CONV_B - 1)
    for b in range(Bb):
        u_scr[0:first, :] = jnp.zeros((first, D_B), F32)
        u_scr[first:B_HALO, :] = buf_ref[b]
        u_scr[B_HALO:B_HALO + T, :] = p_ref[b, :, 0:D_B] * _sigmoid(p_ref[b, :, D_B:2 * D_B])
        bnew_ref[b] = u_scr[first + Tv:B_HALO + Tv, :]

        def chunk(ci):
            r0 = ci * Tc if isinstance(ci, int) else pl.multiple_of(ci * Tc, SUBLANES)
            win = u_scr[pl.ds(r0, Tc + B_HALO), :]
            shifted = [win] + [pltpu.roll(win, Tc + B_HALO - s, 0) for s in range(1, SUBLANES)]
            acc = cb
            for j in range(CONV_B):
                a, s = divmod(first + j, SUBLANES)
                acc = acc + w[j:j + 1] * shifted[s][a * SUBLANES:a * SUBLANES + Tc]
            yn = _ln(acc, g, bb)
            y_ref[b, pl.ds(r0, Tc), :] = yn * _sigmoid(yn)

        def trip(ti, carry):
            for u in range(CONV_TRIP):
                chunk(peel + CONV_TRIP * ti + u)
            return carry

        peel = (T // Tc) % CONV_TRIP
        for ci in range(peel):
            chunk(ci)
        lax.fori_loop(0, (T // Tc) // CONV_TRIP, trip, 0)


def _convb(pb3, buf, w, cb, g, b, Tv, Bb):
    B, T, _ = pb3.shape
    Tc = _pick(T, (48, 32, 16, 8))
    const2 = lambda i: (0, 0)
    seq3 = lambda i: (i, 0, 0)
    return pl.pallas_call(
        functools.partial(_convb_kernel, T=T, Tc=Tc, Tv=Tv, Bb=Bb),
        grid=(B // Bb,),
        in_specs=[pl.BlockSpec((Bb, T, PB_W), seq3), pl.BlockSpec((Bb, CONV_B - 1, D_B), seq3),
                  pl.BlockSpec((CONV_B, D_B), const2), pl.BlockSpec((1, D_B), const2),
                  pl.BlockSpec((1, D_B), const2), pl.BlockSpec((1, D_B), const2)],
        out_specs=[pl.BlockSpec((Bb, T, D_B), seq3), pl.BlockSpec((Bb, CONV_B - 1, D_B), seq3)],
        out_shape=[_sds((B, T, D_B)), _sds((B, CONV_B - 1, D_B))],
        scratch_shapes=[pltpu.VMEM((T + B_HALO, D_B), F32)],
        compiler_params=_cparams("parallel"), name="convb",
    )(pb3, buf, w, cb, g, b)


def _gla_kernel(p_ref, s0_ref, wg2_ref, bgate_ref, ng_ref, y_ref, snew_ref, s_scr, g_scr, *, T, C, Tb, Tv, Bb):
    ri = lax.broadcasted_iota(jnp.int32, (C, C), 0)
    ci_ = lax.broadcasted_iota(jnp.int32, (C, C), 1)
    tril = ri >= ci_
    lane_k = lax.broadcasted_iota(jnp.int32, (1, D_QK_C), 1)
    lane_v = lax.broadcasted_iota(jnp.int32, (1, D_C), 1)
    hm_k = [(lane_k >= h * DK_C) & (lane_k < (h + 1) * DK_C) for h in range(N_HEADS_C)]
    hm_v = [(lane_v >= h * DV_C) & (lane_v < (h + 1) * DV_C) for h in range(N_HEADS_C)]
    rs = lax.broadcasted_iota(jnp.int32, (D_C, D_QK_C), 0)
    cs = lax.broadcasted_iota(jnp.int32, (D_C, D_QK_C), 1)
    bd_t = (rs >= 0) & (rs < 0)
    for h in range(N_HEADS_C):
        bd_t = bd_t | ((rs >= h * DV_C) & (rs < (h + 1) * DV_C) & (cs >= h * DK_C) & (cs < (h + 1) * DK_C))
    rm = lax.broadcasted_iota(jnp.int32, (D_C, D_C), 0)
    cm = lax.broadcasted_iota(jnp.int32, (D_C, D_C), 1)
    seg = (rm >= 0) & (rm < 0)
    for h in range(N_HEADS_C):
        seg = seg | ((rm >= h * DV_C) & (rm < (h + 1) * DV_C) & (cm >= h * DV_C) & (cm < (h + 1) * DV_C))
    mseg = jnp.where(seg, 1.0, 0.0).astype(BF16)
    wg2 = wg2_ref[...].astype(BF16)
    bgate = bgate_ref[...]
    ng = ng_ref[...]
    rowi = lax.broadcasted_iota(jnp.int32, (C, 1), 0)
    tdims = (((0,), (0,)), ((), ()))

    n_chunks = T // C
    nt_dims = (((1,), (1,)), ((), ()))
    tril4 = jnp.concatenate([tril] * N_HEADS_C, axis=0)
    scan_shifts = [s for s in (1, 2, 4, 8, 16, 32) if s < C]
    scan_keep = [rowi >= s for s in scan_shifts]

    def tile_rows(ti, tb, base=0):
        r0 = ti * tb
        return pl.ds(base + (r0 if isinstance(ti, int) else pl.multiple_of(r0, tb)), tb)

    def gates(ti, carry):
        rows = tile_rows(ti, Tb)
        z = p_ref[rows, PC_Z:PC_Z + GATE_RANK]
        pre = _dot(z.astype(BF16), wg2) + bgate
        g = (jnp.minimum(pre, 0.0) - jnp.log1p(jnp.exp(-jnp.abs(pre)))) * (1.0 / GATE_TAU)
        rid = ti * Tb + lax.broadcasted_iota(jnp.int32, (Tb, 1), 0)
        if Bb > 1:
            rid = rid & (T - 1)
        g_scr[rows, :] = jnp.where(rid < Tv, g, 0.0)
        return carry
    lax.fori_loop(0, Bb * T // Tb, gates, 0)

    for b in range(Bb):
        def chunk(ci):
            rows = tile_rows(ci, C, b * T)
            q = p_ref[rows, PC_Q:PC_Q + D_QK_C] * (DK_C ** -0.5)
            k = p_ref[rows, PC_K:PC_K + D_QK_C]
            v = p_ref[rows, PC_V:PC_V + D_C]
            k = jnp.where((ci * C + rowi) < Tv, k, 0.0)
            gcum = g_scr[rows, :]
            for s, keep in zip(scan_shifts, scan_keep):
                gcum = gcum + jnp.where(keep, pltpu.roll(gcum, s, 0), 0.0)
            g_last = gcum[C - 1:C, :]
            g_mid = gcum[C // 2 - 1:C // 2, :]
            vb = v.astype(BF16)
            qt = q * jnp.exp(gcum - g_mid)
            ktb = (k * jnp.exp(g_mid - gcum)).astype(BF16)
            q4 = jnp.concatenate([jnp.where(hm_k[h], qt, 0.0) for h in range(N_HEADS_C)], axis=0).astype(BF16)
            sc = lax.dot_general(q4, ktb, nt_dims, preferred_element_type=F32)
            r4 = _dot(jnp.where(tril4, sc, 0.0).astype(BF16), vb)
            o = jnp.where(hm_v[0], r4[0:C], 0.0)
            for h in range(1, N_HEADS_C):
                o = o + jnp.where(hm_v[h], r4[h * C:(h + 1) * C], 0.0)
            kd = (k * jnp.exp(g_last - gcum)).astype(BF16)
            upd_t = lax.dot_general(vb, kd, tdims, preferred_element_type=F32)
            return rows, o, (q * jnp.exp(gcum)).astype(BF16), jnp.exp(g_last), jnp.where(bd_t, upd_t, 0.0)

        def apply_state(rows, o, qg, dec, upd):
            s_in = s_scr[...]
            y_ref[rows, :] = o + lax.dot_general(qg, s_in.astype(BF16), nt_dims, preferred_element_type=F32)
            s_scr[...] = s_in * dec + upd

        def trip(ti, carry):
            parts = [chunk(peel + GLA_TRIP * ti + u) for u in range(GLA_TRIP)]
            for part in parts:
                apply_state(*part)
            return carry

        s_scr[...] = jnp.zeros((D_C, D_QK_C), F32)
        for h in range(N_HEADS_C):
            s_scr[h * DV_C:(h + 1) * DV_C, h * DK_C:(h + 1) * DK_C] = s0_ref[b, h]
        peel = n_chunks % GLA_TRIP
        for ci in range(peel):
            apply_state(*chunk(ci))
        lax.fori_loop(0, n_chunks // GLA_TRIP, trip, 0)
        for h in range(N_HEADS_C):
            snew_ref[b, h] = s_scr[h * DV_C:(h + 1) * DV_C, h * DK_C:(h + 1) * DK_C]

    def finish(ti, carry):
        rows = tile_rows(ti, Tb)
        o = y_ref[rows, :]
        rg = p_ref[rows, PC_R:PC_R + D_C]
        o2_hi, o2_lo = _split_bf16(o * o)
        ms = (_dot(o2_hi, mseg) + _dot(o2_lo, mseg)) * (1.0 / DV_C)
        y_ref[rows, :] = o * lax.rsqrt(ms + LN_EPS) * ng * (rg * _sigmoid(rg))
        return carry
    lax.fori_loop(0, Bb * T // Tb, finish, 0)


def _gla(pc, B, T, s0, wg2, bgate, ng, Tv, Bb):
    assert Bb == 1 or T & (T - 1) == 0
    C = _pick(T, (48, 32, 16, 8))
    Tb = _pick(Bb * T, (344, 256, 128, 64, 48, 32, 16, 8))
    const2 = lambda i: (0, 0)
    row2 = lambda i: (i, 0)
    seq4 = lambda i: (i, 0, 0, 0)
    st = (Bb, N_HEADS_C, DV_C, DK_C)
    y, s_new_t = pl.pallas_call(
        functools.partial(_gla_kernel, T=T, C=C, Tb=Tb, Tv=Tv, Bb=Bb),
        grid=(B // Bb,),
        in_specs=[pl.BlockSpec((Bb * T, PC_W), row2), pl.BlockSpec(st, seq4),
                  pl.BlockSpec((GATE_RANK, D_QK_C), const2), pl.BlockSpec((1, D_QK_C), const2),
                  pl.BlockSpec((1, D_C), const2)],
        out_specs=[pl.BlockSpec((Bb * T, D_C), row2), pl.BlockSpec(st, seq4)],
        out_shape=[_sds((B * T, D_C)), _sds((B, N_HEADS_C, DV_C, DK_C))],
        scratch_shapes=[pltpu.VMEM((D_C, D_QK_C), F32), pltpu.VMEM((Bb * T, D_QK_C), F32)],
        compiler_params=_cparams("parallel"), name="gla",
    )(pc, jnp.swapaxes(s0, 2, 3), wg2, bgate, ng)
    return y, jnp.swapaxes(s_new_t, 2, 3)


def _tiles_to_rows(ref, lead, rows):
    return jnp.concatenate([ref[lead + (pl.ds(s, rows, stride=ROW_TILE), slice(None))]
                            for s in range(ROW_TILE)], axis=1)


def _rows_to_tiles(ref, lead, rows, val):
    for s in range(ROW_TILE):
        ref[lead + (pl.ds(s, rows, stride=ROW_TILE), slice(None))] = val[:, s * LANES:(s + 1) * LANES]


def _outproj_kernel(ya_ref, yb_ref, yc_ref, x_ref, w_ref, g_ref, b_ref, *rest, alpha, n_blocks):
    h_ref = rest[-1]

    @pl.when(pl.program_id(0) < n_blocks)
    def _():
        y_cat = jnp.concatenate([ya_ref[...].astype(BF16), yb_ref[...].astype(BF16), yc_ref[...].astype(BF16)],
                                axis=1)
        y = _dot(y_cat, w_ref[...])
        h = _ln(alpha * x_ref[...] + y, g_ref[...], b_ref[...])
        _rows_to_tiles(h_ref, (), h.shape[0], h)

    @pl.when(pl.program_id(0) >= n_blocks)
    def _():
        h_ref[...] = jnp.zeros(h_ref.shape, F32)


def _outproj(ya, yb, yc, x, row0, w, g, b, alpha, tm, out_rows, out_row0=0, into=None):
    n = ya.shape[0]
    nb = n // tm
    assert n % tm == 0 and row0 % tm == 0 and out_row0 % tm == 0 and out_rows % tm == 0
    off, out_off = row0 // tm, out_row0 // tm
    grid = nb if into is not None else out_rows // tm
    row = lambda i: (jnp.minimum(i, nb - 1), 0)
    const = lambda i: (0, 0)
    in_specs = [pl.BlockSpec((tm, D_A), row), pl.BlockSpec((tm, D_B), row), pl.BlockSpec((tm, D_C), row),
                pl.BlockSpec((tm, D_MODEL), lambda i: (jnp.minimum(i, nb - 1) + off, 0)),
                pl.BlockSpec((D_MODEL, D_MODEL), const), pl.BlockSpec((1, D_MODEL), const),
                pl.BlockSpec((1, D_MODEL), const)]
    args = [ya, yb, yc, x, w, g, b]
    aliases = {}
    if into is not None:
        in_specs.append(pl.BlockSpec(memory_space=pl.ANY))
        args.append(into)
        aliases = {len(args) - 1: 0}
    return pl.pallas_call(
        functools.partial(_outproj_kernel, alpha=alpha, n_blocks=nb),
        grid=(grid,),
        in_specs=in_specs,
        out_specs=pl.BlockSpec((tm * ROW_TILE, LANES), lambda i: (i + out_off, 0)),
        out_shape=_sds((out_rows * ROW_TILE, LANES)),
        input_output_aliases=aliases,
        compiler_params=_cparams("arbitrary"), name="outproj",
    )(*args)


def _router_kernel(h_ref, wt_ref, b_ref, idx_ref, gate_ref, cnt_ref):
    nt = (((1,), (1,)), ((), ()))
    hh, hl = _split_bf16(_tiles_to_rows(h_ref, (), idx_ref.shape[1]))
    wh, wl = _split_bf16(wt_ref[...])
    logits = (lax.dot_general(wh, hh, nt, preferred_element_type=F32)
              + lax.dot_general(wh, hl, nt, preferred_element_type=F32)
              + lax.dot_general(wl, hh, nt, preferred_element_type=F32)) + b_ref[...]
    eid = lax.broadcasted_iota(jnp.int32, logits.shape, 0)
    vals = []
    member = jnp.zeros(logits.shape, F32)
    for k in range(TOP_K):
        m = jnp.max(logits, axis=0, keepdims=True)
        sel = jnp.min(jnp.where(logits == m, eid, N_EXPERTS), axis=0, keepdims=True)
        idx_ref[k:k + 1, :] = sel
        vals.append(m)
        hit = eid == sel
        member = jnp.where(hit, 1.0, member)
        logits = jnp.where(hit, -jnp.inf, logits)
    es = [jnp.exp(v - vals[0]) for v in vals]
    tot = es[0] + es[1] + es[2] + es[3]
    for k in range(TOP_K):
        gate_ref[k:k + 1, :] = es[k] / tot
    cnt_ref[...] = jnp.broadcast_to(jnp.sum(member, axis=1, keepdims=True), cnt_ref.shape)


def _router_tile(n):
    return _pick(n, (896, 640, 512, 384, 256, 128))


def _router(h, n, wt, b):
    tm = _router_tile(n)
    nt = n // tm
    return pl.pallas_call(
        _router_kernel,
        grid=(nt,),
        in_specs=[pl.BlockSpec((tm * ROW_TILE, LANES), lambda i: (i, 0)),
                  pl.BlockSpec((N_EXPERTS, D_MODEL), lambda i: (0, 0)),
                  pl.BlockSpec((N_EXPERTS, 1), lambda i: (0, 0))],
        out_specs=[pl.BlockSpec((TOP_K, tm), lambda i: (0, i)), pl.BlockSpec((TOP_K, tm), lambda i: (0, i)),
                   pl.BlockSpec((N_EXPERTS, LANES), lambda i: (0, i))],
        out_shape=[_sds((TOP_K, n), jnp.int32), _sds((TOP_K, n)), _sds((N_EXPERTS, nt * LANES))],
        compiler_params=_cparams("parallel"), name="router",
    )(h, wt, b)


def _dest_kernel(idx_ref, base_ref, dest_ref):
    tm = idx_ref.shape[1]
    eid = lax.broadcasted_iota(jnp.int32, (N_EXPERTS, tm), 0)
    hits = [eid == idx_ref[k:k + 1, :] for k in range(TOP_K)]
    member = jnp.zeros((N_EXPERTS, tm), F32)
    for k in range(TOP_K):
        member = jnp.where(hits[k], 1.0, member)
    earlier = (lax.broadcasted_iota(jnp.int32, (tm, tm), 0) < lax.broadcasted_iota(jnp.int32, (tm, tm), 1))
    rank = _dot(member.astype(BF16), jnp.where(earlier, 1.0, 0.0).astype(BF16))
    pos = base_ref[...] + rank
    for k in range(TOP_K):
        dest_ref[k:k + 1, :] = jnp.sum(jnp.where(hits[k], pos, 0.0), axis=0, keepdims=True).astype(jnp.int32)


def _dest(top_idx, base):
    n = top_idx.shape[1]
    tm = _router_tile(n)
    return pl.pallas_call(
        _dest_kernel,
        grid=(n // tm,),
        in_specs=[pl.BlockSpec((TOP_K, tm), lambda i: (0, i)),
                  pl.BlockSpec((None, N_EXPERTS, 1), lambda i: (i, 0, 0))],
        out_specs=pl.BlockSpec((TOP_K, tm), lambda i: (0, i)),
        out_shape=_sds((TOP_K, n), jnp.int32),
        compiler_params=_cparams("parallel"), name="moe_dest",
    )(top_idx, base)


def _invert(pad_shift, dest, bm):
    k, n = dest.shape
    n_rows = pad_shift.shape[0] * bm
    ch = next(c for c in range(min(n, 4096) // SC_LANES * SC_LANES, 0, -SC_LANES) if n % c == 0)
    fill = (jnp.arange(n_rows, dtype=jnp.int32).reshape(-1, bm) + pad_shift[:, None]).reshape(-1)
    mesh = plsc.VectorSubcoreMesh(core_axis_name="c", subcore_axis_name="s", num_cores=SC_CORES,
                                  num_subcores=SC_SUBCORES)

    @functools.partial(
        pl.kernel, mesh=mesh, out_type=_sds((n_rows,), jnp.int32),
        scratch_types=[pltpu.VMEM((n_rows,), jnp.int32), pltpu.VMEM((ch,), jnp.int32)],
        compiler_params=pltpu.CompilerParams(needs_layout_passes=False), name="moe_invert_sc")
    def run(fill_hbm, dest_hbm, inv_hbm, inv_v, d_v):
        first = jnp.logical_and(lax.axis_index("c") == 0, lax.axis_index("s") == 0)

        @pl.when(first)
        def _():
            pltpu.sync_copy(fill_hbm, inv_v)
            lanes = lax.iota(jnp.int32, SC_LANES)
            for slot in range(k):
                def chunk(c, carry):
                    pltpu.sync_copy(dest_hbm.at[pl.ds(slot * n + c * ch, ch)], d_v)

                    def vec(v, carry):
                        idx = d_v[pl.ds(v * SC_LANES, SC_LANES)]
                        plsc.store_scatter(inv_v, [idx], (c * ch + v * SC_LANES + lanes) * k + slot)
                        return carry
                    return lax.fori_loop(0, ch // SC_LANES, vec, carry)
                lax.fori_loop(0, n // ch, chunk, 0)
            pltpu.sync_copy(inv_v, inv_hbm)

    return run(fill, dest.reshape(-1))


def _moe_kernel(be_ref, nu_ref, inv_ref, h_hbm, w1f_ref, b1_ref, w2f_ref, b2_ref, y4_hbm,
                xbuf, xb, obuf, w1_ref, w2_ref, gsem, ssem, *, bm):
    i = pl.program_id(0)
    n_used = nu_ref[0]
    last_blk = pl.num_programs(0) - 1
    n_ff = D_FF // FF_CHUNK

    def tile(r):
        start = r * ROW_TILE
        return pl.ds(start if isinstance(r, int) else pl.multiple_of(start, ROW_TILE), ROW_TILE)

    def gather_row(blk, slot, j, r=None):
        r = inv_ref[blk * bm + j] if r is None else r
        t = lax.shift_right_logical(r, 2)
        pltpu.make_async_copy(h_hbm.at[tile(t), :], xbuf.at[slot, tile(j), :], gsem.at[slot]).start()

    def scatter_row(blk, slot, j, r=None):
        r = inv_ref[blk * bm + j] if r is None else r
        pltpu.make_async_copy(obuf.at[slot, tile(j), :], y4_hbm.at[tile(r), :], ssem.at[slot]).start()

    def wait_gather(slot):
        pltpu.make_async_copy(h_hbm.at[pl.ds(0, bm * ROW_TILE), :], xbuf.at[slot], gsem.at[slot]).wait()

    def wait_scatter(slot):
        pltpu.make_async_copy(obuf.at[slot], y4_hbm.at[pl.ds(0, bm * ROW_TILE), :], ssem.at[slot]).wait()

    def loop_rows(fn, blk, slot):
        def body(j, c):
            fn(blk, slot, j)
            return c
        lax.fori_loop(0, bm, body, 0)

    def issue_rows(fn, blk, slot, j_lo, j_hi):
        for j0 in range(j_lo, j_hi, DMA_GROUP):
            js = range(j0, j0 + DMA_GROUP)
            ids = [inv_ref[blk * bm + j] for j in js]
            for j, r in zip(js, ids):
                fn(blk, slot, j, r)

    def step():
        slot = i % 2

        @pl.when(i + 1 < n_used)
        def _():
            issue_rows(gather_row, i + 1, 1 - slot, 0, bm)
        wait_gather(slot)
        xb[...] = _tiles_to_rows(xbuf, (slot,), bm).astype(BF16)
        acc = jnp.zeros((bm, D_MODEL), F32)
        for c in range(n_ff):
            lo = c * FF_CHUNK
            x = xb[...]
            hg = _dot(x, w1_ref[:, lo:lo + FF_CHUNK]) + b1_ref[:, lo:lo + FF_CHUNK]
            hl = _dot(x, w1_ref[:, D_FF + lo:D_FF + lo + FF_CHUNK]) + b1_ref[:, D_FF + lo:D_FF + lo + FF_CHUNK]
            gate = jnp.minimum(hg, SWIGLU_LIMIT)
            lin = jnp.clip(hl, -SWIGLU_LIMIT, SWIGLU_LIMIT)
            act = gate * _sigmoid(SWIGLU_ALPHA * gate) * (lin + 1.0)
            acc = acc + _dot(act.astype(BF16), w2_ref[lo:lo + FF_CHUNK, :])

        @pl.when(i >= 2)
        def _():
            wait_scatter(slot)
        _rows_to_tiles(obuf, (slot,), bm, acc + b2_ref[...])
        issue_rows(scatter_row, i, slot, 0, bm)

    new_expert = jnp.logical_or(i == 0, be_ref[i] != be_ref[jnp.maximum(i - 1, 0)])

    @pl.when(jnp.logical_and(new_expert, i < n_used))
    def _():
        def cast_rows(r, c):
            rows = pl.ds(pl.multiple_of(r * W_CAST_ROWS, W_CAST_ROWS), W_CAST_ROWS)
            w1_ref[rows, :] = w1f_ref[rows, :].astype(BF16)
            w2_ref[rows, :] = w2f_ref[rows, :].astype(BF16)
            return c
        lax.fori_loop(0, D_MODEL // W_CAST_ROWS, cast_rows, 0)

    @pl.when(jnp.logical_and(i == 0, n_used > 0))
    def _():
        loop_rows(gather_row, 0, 0)

    @pl.when(i < n_used)
    def _():
        step()

    @pl.when(i == n_used - 1)
    def _():
        slot = i % 2
        wait_scatter(slot)

        @pl.when(i >= 1)
        def _():
            wait_scatter(1 - slot)
        obuf[0] = jnp.zeros((bm * ROW_TILE, LANES), F32)

        def zero_block(blk, c):
            rows = bm * ROW_TILE
            cp = pltpu.make_async_copy(obuf.at[0], y4_hbm.at[pl.ds(pl.multiple_of(blk * rows, rows), rows), :],
                                       ssem.at[0])
            cp.start()
            cp.wait()
            return c
        lax.fori_loop(n_used, last_blk + 1, zero_block, 0)


def _moe(block_e, n_used, inv, h, layer, w1, b1, w2, b2, bm):
    nblk = block_e.shape[0]
    wmap = lambda i, be, nu, iv: (layer, be[i], 0, 0)
    return pl.pallas_call(
        functools.partial(_moe_kernel, bm=bm),
        grid_spec=pltpu.PrefetchScalarGridSpec(
            num_scalar_prefetch=3, grid=(nblk,),
            in_specs=[pl.BlockSpec(memory_space=pl.ANY),
                      pl.BlockSpec((None, None, D_MODEL, 2 * D_FF), wmap),
                      pl.BlockSpec((None, None, 1, 2 * D_FF), wmap),
                      pl.BlockSpec((None, None, D_FF, D_MODEL), wmap),
                      pl.BlockSpec((None, None, 1, D_MODEL), wmap)],
            out_specs=pl.BlockSpec(memory_space=pl.ANY),
            scratch_shapes=[pltpu.VMEM((2, bm * ROW_TILE, LANES), F32), pltpu.VMEM((bm, D_MODEL), BF16),
                            pltpu.VMEM((2, bm * ROW_TILE, LANES), F32),
                            pltpu.VMEM((D_MODEL, 2 * D_FF), BF16), pltpu.VMEM((D_FF, D_MODEL), BF16),
                            pltpu.SemaphoreType.DMA((2,)), pltpu.SemaphoreType.DMA((2,))]),
        out_shape=_sds((nblk * bm * ROW_TILE, LANES)),
        compiler_params=_cparams("arbitrary"), name="moe_experts",
    )(block_e, n_used, inv, h, w1, b1, w2, b2)


def _combine_kernel(gates_ref, y4_ref, h_ref, g_ref, b_ref, out_ref, acc_scr, *, alpha, tok_off):
    tm = out_ref.shape[0]
    tok0 = tok_off + pl.program_id(0) * tm

    def group(gi, carry):
        t0 = pl.multiple_of(gi * SUBLANES, SUBLANES)
        tiles = []
        for u in range(SUBLANES):
            t = t0 + u
            tile = alpha * h_ref[pl.ds(pl.multiple_of(t * ROW_TILE, ROW_TILE), ROW_TILE), :]
            for k in range(TOP_K):
                r = (t * TOP_K + k) * ROW_TILE
                tile = tile + (gates_ref[(tok0 + t) * TOP_K + k]
                               * y4_ref[pl.ds(pl.multiple_of(r, ROW_TILE), ROW_TILE), :])
            tiles.append(tile.reshape(1, ROW_TILE, LANES))
        rows = jnp.swapaxes(jnp.concatenate(tiles, axis=0), 0, 1)
        for s in range(ROW_TILE):
            acc_scr[pl.ds(t0, SUBLANES), s * LANES:(s + 1) * LANES] = rows[s]
        return carry
    lax.fori_loop(0, tm // SUBLANES, group, 0)
    out_ref[...] = _ln(acc_scr[...], g_ref[...], b_ref[...])


def _combine(y4, gates_flat, h, row0, n, g, b, alpha):
    tm = _pick(math.gcd(n, row0) if row0 else n, (448, 384, 224, 128, 64, 32, 16, 8))
    off = row0 // tm
    row = lambda i, gt: (i + off, 0)
    const = lambda i, gt: (0, 0)
    return pl.pallas_call(
        functools.partial(_combine_kernel, alpha=alpha, tok_off=row0),
        grid_spec=pltpu.PrefetchScalarGridSpec(
            num_scalar_prefetch=1, grid=(n // tm,),
            in_specs=[pl.BlockSpec((TOP_K * tm * ROW_TILE, LANES), row),
                      pl.BlockSpec((tm * ROW_TILE, LANES), row),
                      pl.BlockSpec((1, D_MODEL), const), pl.BlockSpec((1, D_MODEL), const)],
            out_specs=pl.BlockSpec((tm, D_MODEL), lambda i, gt: (i, 0)),
            scratch_shapes=[pltpu.VMEM((tm, D_MODEL), F32)]),
        out_shape=_sds((n, D_MODEL)),
        compiler_params=_cparams("arbitrary"), name="moe_combine",
    )(gates_flat, y4, h, g, b)


def _group_layout(cnt_tiles, nblk, bm):
    cnt = cnt_tiles.astype(jnp.int32)
    counts = jnp.sum(cnt, axis=1)
    padded = (counts + bm - 1) // bm * bm
    pad_end = jnp.cumsum(padded)
    pad_start = pad_end - padded
    base = pad_start[:, None] + jnp.cumsum(cnt, axis=1) - cnt
    starts = jnp.arange(nblk, dtype=jnp.int32) * bm
    groups_before = jnp.sum((pad_end[None, :] <= starts[:, None]).astype(jnp.int32), axis=1)
    block_e = jnp.minimum(groups_before, N_EXPERTS - 1).astype(jnp.int32)
    n_used = (pad_end[-1] // bm).astype(jnp.int32).reshape(1)
    total = jnp.sum(counts)
    real_before = jnp.concatenate([jnp.cumsum(counts), total.reshape(1)])[groups_before]
    pad_shift = (total - real_before).astype(jnp.int32)
    return base.T.astype(F32)[:, :, None], pad_shift, block_e, n_used


def _pack_w_in(w):
    o = 0
    xa, o = w[:, o:o + D_A], o + D_A
    ga, o = w[:, o:o + D_A], o + D_A
    vb, o = w[:, o:o + D_B], o + D_B
    gb, o = w[:, o:o + D_B], o + D_B
    q, o = w[:, o:o + D_QK_C], o + D_QK_C
    k, o = w[:, o:o + D_QK_C], o + D_QK_C
    v, o = w[:, o:o + D_C], o + D_C
    r, o = w[:, o:o + D_C], o + D_C
    z = w[:, o:o + GATE_RANK]
    zq = jnp.zeros((w.shape[0], PC_K - PC_Z - GATE_RANK), w.dtype)
    zk = jnp.zeros((w.shape[0], PC_V - PC_K - D_QK_C), w.dtype)
    return jnp.concatenate([xa, ga, vb, gb, q, z, zq, k, zk, v, r], axis=1).astype(BF16)


def _block_diag(w):
    eye = jnp.eye(N_BLK_A, dtype=w.dtype)
    return jnp.einsum("hij,hg->higj", w, eye).reshape(D_A, D_A)


def kernel(x_prompt, x_sample, state_conv_a, state_rglru, state_conv_b, state_gla, meta_tokens, ln0_g, ln0_b,
           w_in, conv_a_w, conv_a_b, w_rg, b_rg, w_ig, b_ig, lru_lambda, conv_b_w, conv_b_b, ln_b_g, ln_b_b,
           w_gate2, b_gate, gla_norm_g, w_out, ln1_g, ln1_b, router_w, router_b, moe_w1, moe_b1, moe_w2, moe_b2,
           ln2_g, ln2_b):
    bp, seq, _ = x_prompt.shape
    bs, dseq, _ = x_sample.shape
    depth = w_in.shape[0]
    tp = N_META + seq
    ts = SAMPLE_PAD_T
    np_rows = bp * tp
    ns_rows = bs * dseq
    n = np_rows + ns_rows
    alpha = (2 * depth) ** 0.25
    row = lambda a: a.reshape(1, -1)

    meta = jnp.broadcast_to(meta_tokens[None], (bp, N_META, D_MODEL))
    xp_in = jnp.concatenate([meta, x_prompt], axis=1).reshape(np_rows, D_MODEL)
    xs_in = jnp.pad(x_sample, ((0, 0), (0, ts - dseq), (0, 0))).reshape(bs * ts, D_MODEL)
    zeros_p = (jnp.zeros((bp, CONV_A - 1, D_A), F32), jnp.zeros((bp, 1, D_A), F32),
               jnp.zeros((bp, CONV_B - 1, D_B), F32), jnp.zeros((bp, N_HEADS_C, DK_C, DV_C), F32))
    sb = _pick(bs, (16, 8, 4, 2, 1))
    tm_p = _pick(np_rows, (384, 512, 256, 128, 64, 32, 16, 8))
    tm_s = _pick(math.gcd(ns_rows, np_rows), (128, 64, 32, 16, 8))
    nblk = -(-(TOP_K * n) // MOE_BM) + N_EXPERTS
    while (nblk * MOE_BM // TOP_K) % math.lcm(tm_p, tm_s):
        nblk += 1

    new_p = ([], [], [], [])
    new_s = ([], [], [], [])
    x_all = None
    for l in range(depth):
        w_packed = _pack_w_in(w_in[l])
        wg = jnp.concatenate([_block_diag(w_rg[l]), _block_diag(w_ig[l])], axis=1).astype(BF16)
        bg = jnp.concatenate([b_rg[l], b_ig[l]]).reshape(1, -1)
        w_out_b = w_out[l].astype(BF16)
        ng = row(jnp.tile(gla_norm_g[l], N_HEADS_C))
        if l == 0:
            pa_p, pb_p, pc_p, xn_p = _inproj(xp_in, 0, np_rows, row(ln0_g), row(ln0_b), w_packed, True)
            pa_s, pb_s, pc_s, xn_s = _inproj(xs_in, 0, bs * ts, row(ln0_g), row(ln0_b), w_packed, True)
            res_p = xn_p
        else:
            pa_p, pb_p, pc_p = _inproj(x_all, 0, np_rows, row(ln0_g), row(ln0_b), w_packed, False)
            xn_s = jnp.pad(x_all[np_rows:].reshape(bs, dseq, D_MODEL),
                           ((0, 0), (0, ts - dseq), (0, 0))).reshape(bs * ts, D_MODEL)
            pa_s, pb_s, pc_s = _inproj(xn_s, 0, bs * ts, row(ln0_g), row(ln0_b), w_packed, False)
            res_p = x_all

        outs = []
        for (pa, pb, pc, nb, t, tv, bb, st) in (
                (pa_p, pb_p, pc_p, bp, tp, tp, 1, zeros_p),
                (pa_s, pb_s, pc_s, bs, ts, dseq, sb,
                 (state_conv_a[l], state_rglru[l].reshape(bs, 1, D_A), state_conv_b[l], state_gla[l]))):
            ya, ca_new, h_last = _rglru(pa.reshape(nb, t, PA_W), st[0], st[1], conv_a_w[l], row(conv_a_b[l]),
                                        wg, bg, row(lru_lambda[l]), tv, bb)
            yb, cb_new = _convb(pb.reshape(nb, t, PB_W), st[2], conv_b_w[l], row(conv_b_b[l]),
                                row(ln_b_g[l]), row(ln_b_b[l]), tv, bb)
            yc, s_new = _gla(pc, nb, t, st[3], w_gate2[l], row(b_gate[l]), ng, tv, bb)
            outs.append((ya.reshape(nb * t, D_A), yb.reshape(nb * t, D_B), yc,
                         ca_new, h_last.reshape(nb, D_A), cb_new, s_new))
        (ya_p, yb_p, yc_p, *st_p), (ya_s, yb_s, yc_s, *st_s) = outs
        for j in range(4):
            new_p[j].append(st_p[j])
            new_s[j].append(st_s[j])

        valid = lambda a: a.reshape(bs, ts, -1)[:, :dseq].reshape(ns_rows, -1)
        h_all = _outproj(ya_p, yb_p, yc_p, res_p, 0, w_out_b, row(ln1_g[l]), row(ln1_b[l]), alpha,
                         tm_p, nblk * MOE_BM // TOP_K)
        h_all = _outproj(valid(ya_s), valid(yb_s), valid(yc_s), valid(xn_s), 0, w_out_b, row(ln1_g[l]),
                         row(ln1_b[l]), alpha, tm_s, nblk * MOE_BM // TOP_K, out_row0=np_rows, into=h_all)

        top_idx, gates, cnt = _router(h_all, n, router_w[l].T, router_b[l].reshape(-1, 1))
        base, pad_shift, block_e, n_used = _group_layout(cnt[:, ::LANES], nblk, MOE_BM)
        dest = _dest(top_idx, base)
        inv = _invert(pad_shift, dest, MOE_BM)
        y4 = _moe(block_e, n_used, inv, h_all, l, moe_w1, moe_b1[:, :, None, :], moe_w2, moe_b2[:, :, None, :],
                  MOE_BM)
        comb = functools.partial(_combine, y4, gates.T.reshape(-1), h_all, g=row(ln2_g[l]), b=row(ln2_b[l]),
                                 alpha=alpha)
        if l + 1 < depth:
            x_all = comb(0, n)
        else:
            out_p, out_s = comb(0, np_rows), comb(np_rows, ns_rows)

    y_p = out_p.reshape(bp, tp, D_MODEL)[:, N_META:]
    y_s = out_s.reshape(bs, dseq, D_MODEL)
    return (y_p, y_s,
            jnp.stack(new_p[0]), jnp.stack(new_p[1]), jnp.stack(new_p[2]), jnp.stack(new_p[3]),
            jnp.stack(new_s[0]), jnp.stack(new_s[1]), jnp.stack(new_s[2]), jnp.stack(new_s[3]))
```

```python
import functools
import math

import jax
import jax.numpy as jnp
from jax import lax
from jax.experimental import pallas as pl
from jax.experimental.pallas import tpu as pltpu
from jax.experimental.pallas import tpu_sc as plsc

F32 = jnp.float32
BF16 = jnp.bfloat16

D_MODEL = 1024
N_META = 16
D_A = 384
D_B = 256
D_C = 384
N_BLK_A = 8
BLK_A = D_A // N_BLK_A
CONV_A = 4
RG_C = 8.0
CONV_B = 31
N_HEADS_C = 4
DV_C = D_C // N_HEADS_C
DK_C = DV_C // 2
D_QK_C = N_HEADS_C * DK_C
GATE_RANK = 16
GATE_TAU = 16.0
N_EXPERTS = 32
TOP_K = 4
D_FF = D_MODEL
SWIGLU_LIMIT = 7.0
SWIGLU_ALPHA = 1.702
LN_EPS = 1e-5

PA_W = 2 * D_A
PB_W = 2 * D_B
PC_Q, PC_Z, PC_K, PC_V, PC_R, PC_W = 0, 192, 256, 512, 896, 1280
P_W = PA_W + PB_W + PC_W

SUBLANES = 8
LANES = 128
SC_LANES = 16
SC_CORES, SC_SUBCORES = 2, 16
ROW_TILE = D_MODEL // LANES
VMEM_LIMIT_BYTES = 56 * 1024 * 1024
MOE_BM = 512
FF_CHUNK = 512
DMA_GROUP = 8
W_CAST_ROWS = 128
GLA_TRIP = 14
CONV_TRIP = 7
SAMPLE_PAD_T = 8


def _cparams(*sem):
    return pltpu.CompilerParams(dimension_semantics=sem, vmem_limit_bytes=VMEM_LIMIT_BYTES)


def _sds(shape, dtype=F32):
    return jax.ShapeDtypeStruct(shape, dtype)


def _pick(n, prefs):
    for p in prefs:
        if n % p == 0:
            return p
    raise ValueError(f"no tile for {n} in {prefs}")


def _ln(x, g, b):
    mu = jnp.mean(x, axis=-1, keepdims=True)
    xc = x - mu
    var = jnp.mean(xc * xc, axis=-1, keepdims=True)
    return xc * lax.rsqrt(var + LN_EPS) * g + b


def _sigmoid(x):
    return 1.0 / (1.0 + jnp.exp(-x))


def _split_bf16(x):
    hi = x.astype(BF16)
    lo = (x - hi.astype(F32)).astype(BF16)
    return hi, lo


def _dot(a, b):
    return jnp.dot(a, b, preferred_element_type=F32)


def _inproj_kernel(x_ref, g_ref, b_ref, w_ref, pa_ref, pb_ref, pc_ref, *maybe_xn, apply_ln):
    x = x_ref[...]
    if apply_ln:
        x = _ln(x, g_ref[...], b_ref[...])
        maybe_xn[0][...] = x
    xb = x.astype(BF16)
    pa_ref[...] = _dot(xb, w_ref[:, 0:PA_W])
    pb_ref[...] = _dot(xb, w_ref[:, PA_W:PA_W + PB_W])
    pc_ref[...] = _dot(xb, w_ref[:, PA_W + PB_W:P_W])


def _inproj(x, row0, nrows, ln_g, ln_b, w_packed, apply_ln):
    tm = _pick(nrows, (384, 512, 256, 128, 64, 32, 16, 8))
    while row0 % tm:
        tm //= 2
    off = row0 // tm
    const = lambda i: (0, 0)
    row = lambda i: (i, 0)
    out_shape = [_sds((nrows, PA_W)), _sds((nrows, PB_W)), _sds((nrows, PC_W))]
    out_specs = [pl.BlockSpec((tm, PA_W), row), pl.BlockSpec((tm, PB_W), row), pl.BlockSpec((tm, PC_W), row)]
    if apply_ln:
        out_shape.append(_sds((nrows, D_MODEL)))
        out_specs.append(pl.BlockSpec((tm, D_MODEL), row))
    return pl.pallas_call(
        functools.partial(_inproj_kernel, apply_ln=apply_ln),
        grid=(nrows // tm,),
        in_specs=[pl.BlockSpec((tm, D_MODEL), lambda i: (i + off, 0)),
                  pl.BlockSpec((1, D_MODEL), const), pl.BlockSpec((1, D_MODEL), const),
                  pl.BlockSpec((D_MODEL, P_W), const)],
        out_specs=out_specs, out_shape=out_shape,
        compiler_params=_cparams("parallel"), name="inproj",
    )(x, ln_g, ln_b, w_packed)


def _rglru_kernel(p_ref, cbuf_ref, h0_ref, cw_ref, cb_ref, wg_ref, bg_ref, lam_ref,
                  y_ref, cnew_ref, hlast_ref, xp_scr, a_scr, h_scr, *, T, Tc, Tv, Bb):
    lam = lam_ref[...]
    softplus_neg = jnp.maximum(-lam, 0.0) + jnp.log1p(jnp.exp(-jnp.abs(lam)))
    c_decay = -RG_C * softplus_neg
    cw = cw_ref[...]
    cb = cb_ref[...]
    bg = bg_ref[...]
    sub = lax.broadcasted_iota(jnp.int32, (Tc, D_A), 0) % SUBLANES
    halo = SUBLANES - (CONV_A - 1)
    for b in range(Bb):
        xp_scr[halo:SUBLANES, :] = cbuf_ref[b]
        xp_scr[SUBLANES:SUBLANES + T, :] = p_ref[b, :, 0:D_A]
        cnew_ref[b] = xp_scr[halo + Tv:SUBLANES + Tv, :]

        def chunk(ci, h_b):
            r0 = pl.multiple_of(ci * Tc, SUBLANES)
            win = xp_scr[pl.ds(r0, Tc + SUBLANES), :]
            xc = cb + cw[0:1] * win[halo:halo + Tc]
            for j in range(1, CONV_A):
                xc = xc + cw[j:j + 1] * win[halo + j:halo + j + Tc]
            gates = _dot(xc.astype(BF16), wg_ref[...]) + bg
            r = _sigmoid(gates[:, 0:D_A])
            i = _sigmoid(gates[:, D_A:2 * D_A])
            log_a = c_decay * r
            a = jnp.exp(log_a)
            u = jnp.sqrt(1.0 - a * a) * (i * xc)
            for s in (1, 2, 4):
                keep = sub >= s
                a_prev = pltpu.roll(a, s, 0)
                u_prev = pltpu.roll(u, s, 0)
                u = jnp.where(keep, a * u_prev + u, u)
                a = jnp.where(keep, a * a_prev, a)
            a_scr[...] = a
            h_scr[pl.ds(r0, Tc), :] = u

            def group(gi, h_b):
                c0 = pl.multiple_of(gi * SUBLANES, SUBLANES)
                g0 = pl.multiple_of(r0 + gi * SUBLANES, SUBLANES)
                h8 = a_scr[pl.ds(c0, SUBLANES), :] * h_b + h_scr[pl.ds(g0, SUBLANES), :]
                h_scr[pl.ds(g0, SUBLANES), :] = h8
                return jnp.broadcast_to(h8[SUBLANES - 1:SUBLANES, :], (SUBLANES, D_A))

            h_b = lax.fori_loop(0, Tc // SUBLANES, group, h_b)
            ga = p_ref[b, pl.ds(r0, Tc), D_A:2 * D_A]
            gelu = 0.5 * ga * (1.0 + jnp.tanh(0.7978845608028654 * (ga + 0.044715 * ga * ga * ga)))
            y_ref[b, pl.ds(r0, Tc), :] = h_scr[pl.ds(r0, Tc), :] * gelu
            return h_b

        h_b = jnp.broadcast_to(h0_ref[b], (SUBLANES, D_A))
        lax.fori_loop(0, T // Tc, chunk, h_b)
        hlast_ref[b] = h_scr[Tv - 1:Tv, :]


def _rglru(pa3, cbuf, h0, cw, cb, wg, bg, lam, Tv, Bb):
    B, T, _ = pa3.shape
    Tc = _pick(T, (344, 256, 128, 64, 48, 32, 16, 8))
    const2 = lambda i: (0, 0)
    seq3 = lambda i: (i, 0, 0)
    return pl.pallas_call(
        functools.partial(_rglru_kernel, T=T, Tc=Tc, Tv=Tv, Bb=Bb),
        grid=(B // Bb,),
        in_specs=[pl.BlockSpec((Bb, T, PA_W), seq3), pl.BlockSpec((Bb, CONV_A - 1, D_A), seq3),
                  pl.BlockSpec((Bb, 1, D_A), seq3), pl.BlockSpec((CONV_A, D_A), const2),
                  pl.BlockSpec((1, D_A), const2), pl.BlockSpec((D_A, 2 * D_A), const2),
                  pl.BlockSpec((1, 2 * D_A), const2), pl.BlockSpec((1, D_A), const2)],
        out_specs=[pl.BlockSpec((Bb, T, D_A), seq3), pl.BlockSpec((Bb, CONV_A - 1, D_A), seq3),
                   pl.BlockSpec((Bb, 1, D_A), seq3)],
        out_shape=[_sds((B, T, D_A)), _sds((B, CONV_A - 1, D_A)), _sds((B, 1, D_A))],
        scratch_shapes=[pltpu.VMEM((T + 2 * SUBLANES, D_A), F32), pltpu.VMEM((Tc, D_A), F32),
                        pltpu.VMEM((T, D_A), F32)],
        compiler_params=_cparams("parallel"), name="rglru",
    )(pa3, cbuf, h0, cw, cb, wg, bg, lam)


B_HALO = 32


def _convb_kernel(p_ref, buf_ref, w_ref, cb_ref, g_ref, b_ref, y_ref, bnew_ref, u_scr, *, T, Tc, Tv, Bb):
    w = w_ref[...]
    cb = cb_ref[...]
    g = g_ref[...]
    bb = b_ref[...]
    first = B_HALO - (CONV_B - 1)
    for b in range(Bb):
        u_scr[0:first, :] = jnp.zeros((first, D_B), F32)
        u_scr[first:B_HALO, :] = buf_ref[b]
        u_scr[B_HALO:B_HALO + T, :] = p_ref[b, :, 0:D_B] * _sigmoid(p_ref[b, :, D_B:2 * D_B])
        bnew_ref[b] = u_scr[first + Tv:B_HALO + Tv, :]

        def chunk(ci):
            r0 = ci * Tc if isinstance(ci, int) else pl.multiple_of(ci * Tc, SUBLANES)
            win = u_scr[pl.ds(r0, Tc + B_HALO), :]
            shifted = [win] + [pltpu.roll(win, Tc + B_HALO - s, 0) for s in range(1, SUBLANES)]
            acc = cb
            for j in range(CONV_B):
                a, s = divmod(first + j, SUBLANES)
                acc = acc + w[j:j + 1] * shifted[s][a * SUBLANES:a * SUBLANES + Tc]
            yn = _ln(acc, g, bb)
            y_ref[b, pl.ds(r0, Tc), :] = yn * _sigmoid(yn)

        def trip(ti, carry):
            for u in range(CONV_TRIP):
                chunk(peel + CONV_TRIP * ti + u)
            return carry

        peel = (T // Tc) % CONV_TRIP
        for ci in range(peel):
            chunk(ci)
        lax.fori_loop(0, (T // Tc) // CONV_TRIP, trip, 0)


def _convb(pb3, buf, w, cb, g, b, Tv, Bb):
    B, T, _ = pb3.shape
    Tc = _pick(T, (48, 32, 16, 8))
    const2 = lambda i: (0, 0)
    seq3 = lambda i: (i, 0, 0)
    return pl.pallas_call(
        functools.partial(_convb_kernel, T=T, Tc=Tc, Tv=Tv, Bb=Bb),
        grid=(B // Bb,),
        in_specs=[pl.BlockSpec((Bb, T, PB_W), seq3), pl.BlockSpec((Bb, CONV_B - 1, D_B), seq3),
                  pl.BlockSpec((CONV_B, D_B), const2), pl.BlockSpec((1, D_B), const2),
                  pl.BlockSpec((1, D_B), const2), pl.BlockSpec((1, D_B), const2)],
        out_specs=[pl.BlockSpec((Bb, T, D_B), seq3), pl.BlockSpec((Bb, CONV_B - 1, D_B), seq3)],
        out_shape=[_sds((B, T, D_B)), _sds((B, CONV_B - 1, D_B))],
        scratch_shapes=[pltpu.VMEM((T + B_HALO, D_B), F32)],
        compiler_params=_cparams("parallel"), name="convb",
    )(pb3, buf, w, cb, g, b)


def _gla_kernel(p_ref, s0_ref, wg2_ref, bgate_ref, ng_ref, y_ref, snew_ref, s_scr, g_scr, *, T, C, Tb, Tv, Bb):
    ri = lax.broadcasted_iota(jnp.int32, (C, C), 0)
    ci_ = lax.broadcasted_iota(jnp.int32, (C, C), 1)
    tril = ri >= ci_
    lane_k = lax.broadcasted_iota(jnp.int32, (1, D_QK_C), 1)
    lane_v = lax.broadcasted_iota(jnp.int32, (1, D_C), 1)
    hm_k = [(lane_k >= h * DK_C) & (lane_k < (h + 1) * DK_C) for h in range(N_HEADS_C)]
    hm_v = [(lane_v >= h * DV_C) & (lane_v < (h + 1) * DV_C) for h in range(N_HEADS_C)]
    rs = lax.broadcasted_iota(jnp.int32, (D_C, D_QK_C), 0)
    cs = lax.broadcasted_iota(jnp.int32, (D_C, D_QK_C), 1)
    bd_t = (rs >= 0) & (rs < 0)
    for h in range(N_HEADS_C):
        bd_t = bd_t | ((rs >= h * DV_C) & (rs < (h + 1) * DV_C) & (cs >= h * DK_C) & (cs < (h + 1) * DK_C))
    rm = lax.broadcasted_iota(jnp.int32, (D_C, D_C), 0)
    cm = lax.broadcasted_iota(jnp.int32, (D_C, D_C), 1)
    seg = (rm >= 0) & (rm < 0)
    for h in range(N_HEADS_C):
        seg = seg | ((rm >= h * DV_C) & (rm < (h + 1) * DV_C) & (cm >= h * DV_C) & (cm < (h + 1) * DV_C))
    mseg = jnp.where(seg, 1.0, 0.0).astype(BF16)
    wg2 = wg2_ref[...].astype(BF16)
    bgate = bgate_ref[...]
    ng = ng_ref[...]
    rowi = lax.broadcasted_iota(jnp.int32, (C, 1), 0)
    tdims = (((0,), (0,)), ((), ()))

    n_chunks = T // C
    nt_dims = (((1,), (1,)), ((), ()))
    tril4 = jnp.concatenate([tril] * N_HEADS_C, axis=0)
    scan_shifts = [s for s in (1, 2, 4, 8, 16, 32) if s < C]
    scan_keep = [rowi >= s for s in scan_shifts]

    def tile_rows(ti, tb, base=0):
        r0 = ti * tb
        return pl.ds(base + (r0 if isinstance(ti, int) else pl.multiple_of(r0, tb)), tb)

    def gates(ti, carry):
        rows = tile_rows(ti, Tb)
        z = p_ref[rows, PC_Z:PC_Z + GATE_RANK]
        pre = _dot(z.astype(BF16), wg2) + bgate
        g = (jnp.minimum(pre, 0.0) - jnp.log1p(jnp.exp(-jnp.abs(pre)))) * (1.0 / GATE_TAU)
        rid = ti * Tb + lax.broadcasted_iota(jnp.int32, (Tb, 1), 0)
        if Bb > 1:
            rid = rid & (T - 1)
        g_scr[rows, :] = jnp.where(rid < Tv, g, 0.0)
        return carry
    lax.fori_loop(0, Bb * T // Tb, gates, 0)

    for b in range(Bb):
        def chunk(ci):
            rows = tile_rows(ci, C, b * T)
            q = p_ref[rows, PC_Q:PC_Q + D_QK_C] * (DK_C ** -0.5)
            k = p_ref[rows, PC_K:PC_K + D_QK_C]
            v = p_ref[rows, PC_V:PC_V + D_C]
            k = jnp.where((ci * C + rowi) < Tv, k, 0.0)
            gcum = g_scr[rows, :]
            for s, keep in zip(scan_shifts, scan_keep):
                gcum = gcum + jnp.where(keep, pltpu.roll(gcum, s, 0), 0.0)
            g_last = gcum[C - 1:C, :]
            g_mid = gcum[C // 2 - 1:C // 2, :]
            vb = v.astype(BF16)
            qt = q * jnp.exp(gcum - g_mid)
            ktb = (k * jnp.exp(g_mid - gcum)).astype(BF16)
            q4 = jnp.concatenate([jnp.where(hm_k[h], qt, 0.0) for h in range(N_HEADS_C)], axis=0).astype(BF16)
            sc = lax.dot_general(q4, ktb, nt_dims, preferred_element_type=F32)
            r4 = _dot(jnp.where(tril4, sc, 0.0).astype(BF16), vb)
            o = jnp.where(hm_v[0], r4[0:C], 0.0)
            for h in range(1, N_HEADS_C):
                o = o + jnp.where(hm_v[h], r4[h * C:(h + 1) * C], 0.0)
            kd = (k * jnp.exp(g_last - gcum)).astype(BF16)
            upd_t = lax.dot_general(vb, kd, tdims, preferred_element_type=F32)
            return rows, o, (q * jnp.exp(gcum)).astype(BF16), jnp.exp(g_last), jnp.where(bd_t, upd_t, 0.0)

        def apply_state(rows, o, qg, dec, upd):
            s_in = s_scr[...]
            y_ref[rows, :] = o + lax.dot_general(qg, s_in.astype(BF16), nt_dims, preferred_element_type=F32)
            s_scr[...] = s_in * dec + upd

        def trip(ti, carry):
            parts = [chunk(peel + GLA_TRIP * ti + u) for u in range(GLA_TRIP)]
            for part in parts:
                apply_state(*part)
            return carry

        s_scr[...] = jnp.zeros((D_C, D_QK_C), F32)
        for h in range(N_HEADS_C):
            s_scr[h * DV_C:(h + 1) * DV_C, h * DK_C:(h + 1) * DK_C] = s0_ref[b, h]
        peel = n_chunks % GLA_TRIP
        for ci in range(peel):
            apply_state(*chunk(ci))
        lax.fori_loop(0, n_chunks // GLA_TRIP, trip, 0)
        for h in range(N_HEADS_C):
            snew_ref[b, h] = s_scr[h * DV_C:(h + 1) * DV_C, h * DK_C:(h + 1) * DK_C]

    def finish(ti, carry):
        rows = tile_rows(ti, Tb)
        o = y_ref[rows, :]
        rg = p_ref[rows, PC_R:PC_R + D_C]
        o2_hi, o2_lo = _split_bf16(o * o)
        ms = (_dot(o2_hi, mseg) + _dot(o2_lo, mseg)) * (1.0 / DV_C)
        y_ref[rows, :] = o * lax.rsqrt(ms + LN_EPS) * ng * (rg * _sigmoid(rg))
        return carry
    lax.fori_loop(0, Bb * T // Tb, finish, 0)


def _gla(pc, B, T, s0, wg2, bgate, ng, Tv, Bb):
    assert Bb == 1 or T & (T - 1) == 0
    C = _pick(T, (48, 32, 16, 8))
    Tb = _pick(Bb * T, (344, 256, 128, 64, 48, 32, 16, 8))
    const2 = lambda i: (0, 0)
    row2 = lambda i: (i, 0)
    seq4 = lambda i: (i, 0, 0, 0)
    st = (Bb, N_HEADS_C, DV_C, DK_C)
    y, s_new_t = pl.pallas_call(
        functools.partial(_gla_kernel, T=T, C=C, Tb=Tb, Tv=Tv, Bb=Bb),
        grid=(B // Bb,),
        in_specs=[pl.BlockSpec((Bb * T, PC_W), row2), pl.BlockSpec(st, seq4),
                  pl.BlockSpec((GATE_RANK, D_QK_C), const2), pl.BlockSpec((1, D_QK_C), const2),
                  pl.BlockSpec((1, D_C), const2)],
        out_specs=[pl.BlockSpec((Bb * T, D_C), row2), pl.BlockSpec(st, seq4)],
        out_shape=[_sds((B * T, D_C)), _sds((B, N_HEADS_C, DV_C, DK_C))],
        scratch_shapes=[pltpu.VMEM((D_C, D_QK_C), F32), pltpu.VMEM((Bb * T, D_QK_C), F32)],
        compiler_params=_cparams("parallel"), name="gla",
    )(pc, jnp.swapaxes(s0, 2, 3), wg2, bgate, ng)
    return y, jnp.swapaxes(s_new_t, 2, 3)


def _tiles_to_rows(ref, lead, rows):
    return jnp.concatenate([ref[lead + (pl.ds(s, rows, stride=ROW_TILE), slice(None))]
                            for s in range(ROW_TILE)], axis=1)


def _rows_to_tiles(ref, lead, rows, val):
    for s in range(ROW_TILE):
        ref[lead + (pl.ds(s, rows, stride=ROW_TILE), slice(None))] = val[:, s * LANES:(s + 1) * LANES]


def _outproj_kernel(ya_ref, yb_ref, yc_ref, x_ref, w_ref, g_ref, b_ref, *rest, alpha, n_blocks):
    h_ref = rest[-1]

    @pl.when(pl.program_id(0) < n_blocks)
    def _():
        y_cat = jnp.concatenate([ya_ref[...].astype(BF16), yb_ref[...].astype(BF16), yc_ref[...].astype(BF16)],
                                axis=1)
        y = _dot(y_cat, w_ref[...])
        h = _ln(alpha * x_ref[...] + y, g_ref[...], b_ref[...])
        _rows_to_tiles(h_ref, (), h.shape[0], h)

    @pl.when(pl.program_id(0) >= n_blocks)
    def _():
        h_ref[...] = jnp.zeros(h_ref.shape, F32)


def _outproj(ya, yb, yc, x, row0, w, g, b, alpha, tm, out_rows, out_row0=0, into=None):
    n = ya.shape[0]
    nb = n // tm
    assert n % tm == 0 and row0 % tm == 0 and out_row0 % tm == 0 and out_rows % tm == 0
    off, out_off = row0 // tm, out_row0 // tm
    grid = nb if into is not None else out_rows // tm
    row = lambda i: (jnp.minimum(i, nb - 1), 0)
    const = lambda i: (0, 0)
    in_specs = [pl.BlockSpec((tm, D_A), row), pl.BlockSpec((tm, D_B), row), pl.BlockSpec((tm, D_C), row),
                pl.BlockSpec((tm, D_MODEL), lambda i: (jnp.minimum(i, nb - 1) + off, 0)),
                pl.BlockSpec((D_MODEL, D_MODEL), const), pl.BlockSpec((1, D_MODEL), const),
                pl.BlockSpec((1, D_MODEL), const)]
    args = [ya, yb, yc, x, w, g, b]
    aliases = {}
    if into is not None:
        in_specs.append(pl.BlockSpec(memory_space=pl.ANY))
        args.append(into)
        aliases = {len(args) - 1: 0}
    return pl.pallas_call(
        functools.partial(_outproj_kernel, alpha=alpha, n_blocks=nb),
        grid=(grid,),
        in_specs=in_specs,
        out_specs=pl.BlockSpec((tm * ROW_TILE, LANES), lambda i: (i + out_off, 0)),
        out_shape=_sds((out_rows * ROW_TILE, LANES)),
        input_output_aliases=aliases,
        compiler_params=_cparams("arbitrary"), name="outproj",
    )(*args)


def _router_kernel(h_ref, wt_ref, b_ref, idx_ref, gate_ref, cnt_ref):
    nt = (((1,), (1,)), ((), ()))
    hh, hl = _split_bf16(_tiles_to_rows(h_ref, (), idx_ref.shape[1]))
    wh, wl = _split_bf16(wt_ref[...])
    logits = (lax.dot_general(wh, hh, nt, preferred_element_type=F32)
              + lax.dot_general(wh, hl, nt, preferred_element_type=F32)
              + lax.dot_general(wl, hh, nt, preferred_element_type=F32)) + b_ref[...]
    eid = lax.broadcasted_iota(jnp.int32, logits.shape, 0)
    vals = []
    member = jnp.zeros(logits.shape, F32)
    for k in range(TOP_K):
        m = jnp.max(logits, axis=0, keepdims=True)
        sel = jnp.min(jnp.where(logits == m, eid, N_EXPERTS), axis=0, keepdims=True)
        idx_ref[k:k + 1, :] = sel
        vals.append(m)
        hit = eid == sel
        member = jnp.where(hit, 1.0, member)
        logits = jnp.where(hit, -jnp.inf, logits)
    es = [jnp.exp(v - vals[0]) for v in vals]
    tot = es[0] + es[1] + es[2] + es[3]
    for k in range(TOP_K):
        gate_ref[k:k + 1, :] = es[k] / tot
    cnt_ref[...] = jnp.broadcast_to(jnp.sum(member, axis=1, keepdims=True), cnt_ref.shape)


def _router_tile(n):
    return _pick(n, (896, 640, 512, 384, 256, 128))


def _router(h, n, wt, b):
    tm = _router_tile(n)
    nt = n // tm
    return pl.pallas_call(
        _router_kernel,
        grid=(nt,),
        in_specs=[pl.BlockSpec((tm * ROW_TILE, LANES), lambda i: (i, 0)),
                  pl.BlockSpec((N_EXPERTS, D_MODEL), lambda i: (0, 0)),
                  pl.BlockSpec((N_EXPERTS, 1), lambda i: (0, 0))],
        out_specs=[pl.BlockSpec((TOP_K, tm), lambda i: (0, i)), pl.BlockSpec((TOP_K, tm), lambda i: (0, i)),
                   pl.BlockSpec((N_EXPERTS, LANES), lambda i: (0, i))],
        out_shape=[_sds((TOP_K, n), jnp.int32), _sds((TOP_K, n)), _sds((N_EXPERTS, nt * LANES))],
        compiler_params=_cparams("parallel"), name="router",
    )(h, wt, b)


def _dest_kernel(idx_ref, base_ref, dest_ref):
    tm = idx_ref.shape[1]
    eid = lax.broadcasted_iota(jnp.int32, (N_EXPERTS, tm), 0)
    hits = [eid == idx_ref[k:k + 1, :] for k in range(TOP_K)]
    member = jnp.zeros((N_EXPERTS, tm), F32)
    for k in range(TOP_K):
        member = jnp.where(hits[k], 1.0, member)
    earlier = (lax.broadcasted_iota(jnp.int32, (tm, tm), 0) < lax.broadcasted_iota(jnp.int32, (tm, tm), 1))
    rank = _dot(member.astype(BF16), jnp.where(earlier, 1.0, 0.0).astype(BF16))
    pos = base_ref[...] + rank
    for k in range(TOP_K):
        dest_ref[k:k + 1, :] = jnp.sum(jnp.where(hits[k], pos, 0.0), axis=0, keepdims=True).astype(jnp.int32)


def _dest(top_idx, base):
    n = top_idx.shape[1]
    tm = _router_tile(n)
    return pl.pallas_call(
        _dest_kernel,
        grid=(n // tm,),
        in_specs=[pl.BlockSpec((TOP_K, tm), lambda i: (0, i)),
                  pl.BlockSpec((None, N_EXPERTS, 1), lambda i: (i, 0, 0))],
        out_specs=pl.BlockSpec((TOP_K, tm), lambda i: (0, i)),
        out_shape=_sds((TOP_K, n), jnp.int32),
        compiler_params=_cparams("parallel"), name="moe_dest",
    )(top_idx, base)


def _invert(pad_shift, dest, bm):
    k, n = dest.shape
    n_rows = pad_shift.shape[0] * bm
    ch = next(c for c in range(min(n, 4096) // SC_LANES * SC_LANES, 0, -SC_LANES) if n % c == 0)
    fill = (jnp.arange(n_rows, dtype=jnp.int32).reshape(-1, bm) + pad_shift[:, None]).reshape(-1)
    mesh = plsc.VectorSubcoreMesh(core_axis_name="c", subcore_axis_name="s", num_cores=SC_CORES,
                                  num_subcores=SC_SUBCORES)

    @functools.partial(
        pl.kernel, mesh=mesh, out_type=_sds((n_rows,), jnp.int32),
        scratch_types=[pltpu.VMEM((n_rows,), jnp.int32), pltpu.VMEM((ch,), jnp.int32)],
        compiler_params=pltpu.CompilerParams(needs_layout_passes=False), name="moe_invert_sc")
    def run(fill_hbm, dest_hbm, inv_hbm, inv_v, d_v):
        first = jnp.logical_and(lax.axis_index("c") == 0, lax.axis_index("s") == 0)

        @pl.when(first)
        def _():
            pltpu.sync_copy(fill_hbm, inv_v)
            lanes = lax.iota(jnp.int32, SC_LANES)
            for slot in range(k):
                def chunk(c, carry):
                    pltpu.sync_copy(dest_hbm.at[pl.ds(slot * n + c * ch, ch)], d_v)

                    def vec(v, carry):
                        idx = d_v[pl.ds(v * SC_LANES, SC_LANES)]
                        plsc.store_scatter(inv_v, [idx], (c * ch + v * SC_LANES + lanes) * k + slot)
                        return carry
                    return lax.fori_loop(0, ch // SC_LANES, vec, carry)
                lax.fori_loop(0, n // ch, chunk, 0)
            pltpu.sync_copy(inv_v, inv_hbm)

    return run(fill, dest.reshape(-1))


def _moe_kernel(be_ref, nu_ref, inv_ref, h_hbm, w1f_ref, b1_ref, w2f_ref, b2_ref, y4_hbm,
                xbuf, xb, obuf, w1_ref, w2_ref, gsem, ssem, *, bm):
    i = pl.program_id(0)
    n_used = nu_ref[0]
    last_blk = pl.num_programs(0) - 1
    n_ff = D_FF // FF_CHUNK

    def tile(r):
        start = r * ROW_TILE
        return pl.ds(start if isinstance(r, int) else pl.multiple_of(start, ROW_TILE), ROW_TILE)

    def gather_row(blk, slot, j, r=None):
        r = inv_ref[blk * bm + j] if r is None else r
        t = lax.shift_right_logical(r, 2)
        pltpu.make_async_copy(h_hbm.at[tile(t), :], xbuf.at[slot, tile(j), :], gsem.at[slot]).start()

    def scatter_row(blk, slot, j, r=None):
        r = inv_ref[blk * bm + j] if r is None else r
        pltpu.make_async_copy(obuf.at[slot, tile(j), :], y4_hbm.at[tile(r), :], ssem.at[slot]).start()

    def wait_gather(slot):
        pltpu.make_async_copy(h_hbm.at[pl.ds(0, bm * ROW_TILE), :], xbuf.at[slot], gsem.at[slot]).wait()

    def wait_scatter(slot):
        pltpu.make_async_copy(obuf.at[slot], y4_hbm.at[pl.ds(0, bm * ROW_TILE), :], ssem.at[slot]).wait()

    def loop_rows(fn, blk, slot):
        def body(j, c):
            fn(blk, slot, j)
            return c
        lax.fori_loop(0, bm, body, 0)

    def issue_rows(fn, blk, slot, j_lo, j_hi):
        for j0 in range(j_lo, j_hi, DMA_GROUP):
            js = range(j0, j0 + DMA_GROUP)
            ids = [inv_ref[blk * bm + j] for j in js]
            for j, r in zip(js, ids):
                fn(blk, slot, j, r)

    def step():
        slot = i % 2

        @pl.when(i + 1 < n_used)
        def _():
            issue_rows(gather_row, i + 1, 1 - slot, 0, bm)
        wait_gather(slot)
        xb[...] = _tiles_to_rows(xbuf, (slot,), bm).astype(BF16)
        acc = jnp.zeros((bm, D_MODEL), F32)
        for c in range(n_ff):
            lo = c * FF_CHUNK
            x = xb[...]
            hg = _dot(x, w1_ref[:, lo:lo + FF_CHUNK]) + b1_ref[:, lo:lo + FF_CHUNK]
            hl = _dot(x, w1_ref[:, D_FF + lo:D_FF + lo + FF_CHUNK]) + b1_ref[:, D_FF + lo:D_FF + lo + FF_CHUNK]
            gate = jnp.minimum(hg, SWIGLU_LIMIT)
            lin = jnp.clip(hl, -SWIGLU_LIMIT, SWIGLU_LIMIT)
            act = gate * _sigmoid(SWIGLU_ALPHA * gate) * (lin + 1.0)
            acc = acc + _dot(act.astype(BF16), w2_ref[lo:lo + FF_CHUNK, :])

        @pl.when(i >= 2)
        def _():
            wait_scatter(slot)
        _rows_to_tiles(obuf, (slot,), bm, acc + b2_ref[...])
        issue_rows(scatter_row, i, slot, 0, bm)

    new_expert = jnp.logical_or(i == 0, be_ref[i] != be_ref[jnp.maximum(i - 1, 0)])

    @pl.when(jnp.logical_and(new_expert, i < n_used))
    def _():
        def cast_rows(r, c):
            rows = pl.ds(pl.multiple_of(r * W_CAST_ROWS, W_CAST_ROWS), W_CAST_ROWS)
            w1_ref[rows, :] = w1f_ref[rows, :].astype(BF16)
            w2_ref[rows, :] = w2f_ref[rows, :].astype(BF16)
            return c
        lax.fori_loop(0, D_MODEL // W_CAST_ROWS, cast_rows, 0)

    @pl.when(jnp.logical_and(i == 0, n_used > 0))
    def _():
        loop_rows(gather_row, 0, 0)

    @pl.when(i < n_used)
    def _():
        step()

    @pl.when(i == n_used - 1)
    def _():
        slot = i % 2
        wait_scatter(slot)

        @pl.when(i >= 1)
        def _():
            wait_scatter(1 - slot)
        obuf[0] = jnp.zeros((bm * ROW_TILE, LANES), F32)

        def zero_block(blk, c):
            rows = bm * ROW_TILE
            cp = pltpu.make_async_copy(obuf.at[0], y4_hbm.at[pl.ds(pl.multiple_of(blk * rows, rows), rows), :],
                                       ssem.at[0])
            cp.start()
            cp.wait()
            return c
        lax.fori_loop(n_used, last_blk + 1, zero_block, 0)


def _moe(block_e, n_used, inv, h, layer, w1, b1, w2, b2, bm):
    nblk = block_e.shape[0]
    wmap = lambda i, be, nu, iv: (layer, be[i], 0, 0)
    return pl.pallas_call(
        functools.partial(_moe_kernel, bm=bm),
        grid_spec=pltpu.PrefetchScalarGridSpec(
            num_scalar_prefetch=3, grid=(nblk,),
            in_specs=[pl.BlockSpec(memory_space=pl.ANY),
                      pl.BlockSpec((None, None, D_MODEL, 2 * D_FF), wmap),
                      pl.BlockSpec((None, None, 1, 2 * D_FF), wmap),
                      pl.BlockSpec((None, None, D_FF, D_MODEL), wmap),
                      pl.BlockSpec((None, None, 1, D_MODEL), wmap)],
            out_specs=pl.BlockSpec(memory_space=pl.ANY),
            scratch_shapes=[pltpu.VMEM((2, bm * ROW_TILE, LANES), F32), pltpu.VMEM((bm, D_MODEL), BF16),
                            pltpu.VMEM((2, bm * ROW_TILE, LANES), F32),
                            pltpu.VMEM((D_MODEL, 2 * D_FF), BF16), pltpu.VMEM((D_FF, D_MODEL), BF16),
                            pltpu.SemaphoreType.DMA((2,)), pltpu.SemaphoreType.DMA((2,))]),
        out_shape=_sds((nblk * bm * ROW_TILE, LANES)),
        compiler_params=_cparams("arbitrary"), name="moe_experts",
    )(block_e, n_used, inv, h, w1, b1, w2, b2)


def _combine_kernel(gates_ref, y4_ref, h_ref, g_ref, b_ref, out_ref, acc_scr, *, alpha, tok_off):
    tm = out_ref.shape[0]
    tok0 = tok_off + pl.program_id(0) * tm

    def group(gi, carry):
        t0 = pl.multiple_of(gi * SUBLANES, SUBLANES)
        tiles = []
        for u in range(SUBLANES):
            t = t0 + u
            tile = alpha * h_ref[pl.ds(pl.multiple_of(t * ROW_TILE, ROW_TILE), ROW_TILE), :]
            for k in range(TOP_K):
                r = (t * TOP_K + k) * ROW_TILE
                tile = tile + (gates_ref[(tok0 + t) * TOP_K + k]
                               * y4_ref[pl.ds(pl.multiple_of(r, ROW_TILE), ROW_TILE), :])
            tiles.append(tile.reshape(1, ROW_TILE, LANES))
        rows = jnp.swapaxes(jnp.concatenate(tiles, axis=0), 0, 1)
        for s in range(ROW_TILE):
            acc_scr[pl.ds(t0, SUBLANES), s * LANES:(s + 1) * LANES] = rows[s]
        return carry
    lax.fori_loop(0, tm // SUBLANES, group, 0)
    out_ref[...] = _ln(acc_scr[...], g_ref[...], b_ref[...])


def _combine(y4, gates_flat, h, row0, n, g, b, alpha):
    tm = _pick(math.gcd(n, row0) if row0 else n, (448, 384, 224, 128, 64, 32, 16, 8))
    off = row0 // tm
    row = lambda i, gt: (i + off, 0)
    const = lambda i, gt: (0, 0)
    return pl.pallas_call(
        functools.partial(_combine_kernel, alpha=alpha, tok_off=row0),
        grid_spec=pltpu.PrefetchScalarGridSpec(
            num_scalar_prefetch=1, grid=(n // tm,),
            in_specs=[pl.BlockSpec((TOP_K * tm * ROW_TILE, LANES), row),
                      pl.BlockSpec((tm * ROW_TILE, LANES), row),
                      pl.BlockSpec((1, D_MODEL), const), pl.BlockSpec((1, D_MODEL), const)],
            out_specs=pl.BlockSpec((tm, D_MODEL), lambda i, gt: (i, 0)),
            scratch_shapes=[pltpu.VMEM((tm, D_MODEL), F32)]),
        out_shape=_sds((n, D_MODEL)),
        compiler_params=_cparams("arbitrary"), name="moe_combine",
    )(gates_flat, y4, h, g, b)


def _group_layout(cnt_tiles, nblk, bm):
    cnt = cnt_tiles.astype(jnp.int32)
    counts = jnp.sum(cnt, axis=1)
    padded = (counts + bm - 1) // bm * bm
    pad_end = jnp.cumsum(padded)
    pad_start = pad_end - padded
    base = pad_start[:, None] + jnp.cumsum(cnt, axis=1) - cnt
    starts = jnp.arange(nblk, dtype=jnp.int32) * bm
    groups_before = jnp.sum((pad_end[None, :] <= starts[:, None]).astype(jnp.int32), axis=1)
    block_e = jnp.minimum(groups_before, N_EXPERTS - 1).astype(jnp.int32)
    n_used = (pad_end[-1] // bm).astype(jnp.int32).reshape(1)
    real_before = jnp.sum(jnp.where(pad_start[None, :] <= starts[:, None], counts[None, :], 0), axis=1)
    pad_shift = (jnp.sum(counts) - real_before).astype(jnp.int32)
    return base.T.astype(F32)[:, :, None], pad_shift, block_e, n_used


def _pack_w_in(w):
    o = 0
    xa, o = w[:, o:o + D_A], o + D_A
    ga, o = w[:, o:o + D_A], o + D_A
    vb, o = w[:, o:o + D_B], o + D_B
    gb, o = w[:, o:o + D_B], o + D_B
    q, o = w[:, o:o + D_QK_C], o + D_QK_C
    k, o = w[:, o:o + D_QK_C], o + D_QK_C
    v, o = w[:, o:o + D_C], o + D_C
    r, o = w[:, o:o + D_C], o + D_C
    z = w[:, o:o + GATE_RANK]
    zq = jnp.zeros((w.shape[0], PC_K - PC_Z - GATE_RANK), w.dtype)
    zk = jnp.zeros((w.shape[0], PC_V - PC_K - D_QK_C), w.dtype)
    return jnp.concatenate([xa, ga, vb, gb, q, z, zq, k, zk, v, r], axis=1).astype(BF16)


def _block_diag(w):
    eye = jnp.eye(N_BLK_A, dtype=w.dtype)
    return jnp.einsum("hij,hg->higj", w, eye).reshape(D_A, D_A)


def kernel(x_prompt, x_sample, state_conv_a, state_rglru, state_conv_b, state_gla, meta_tokens, ln0_g, ln0_b,
           w_in, conv_a_w, conv_a_b, w_rg, b_rg, w_ig, b_ig, lru_lambda, conv_b_w, conv_b_b, ln_b_g, ln_b_b,
           w_gate2, b_gate, gla_norm_g, w_out, ln1_g, ln1_b, router_w, router_b, moe_w1, moe_b1, moe_w2, moe_b2,
           ln2_g, ln2_b):
    bp, seq, _ = x_prompt.shape
    bs, dseq, _ = x_sample.shape
    depth = w_in.shape[0]
    tp = N_META + seq
    ts = SAMPLE_PAD_T
    np_rows = bp * tp
    ns_rows = bs * dseq
    n = np_rows + ns_rows
    alpha = (2 * depth) ** 0.25
    row = lambda a: a.reshape(1, -1)

    meta = jnp.broadcast_to(meta_tokens[None], (bp, N_META, D_MODEL))
    xp_in = jnp.concatenate([meta, x_prompt], axis=1).reshape(np_rows, D_MODEL)
    xs_in = jnp.pad(x_sample, ((0, 0), (0, ts - dseq), (0, 0))).reshape(bs * ts, D_MODEL)
    zeros_p = (jnp.zeros((bp, CONV_A - 1, D_A), F32), jnp.zeros((bp, 1, D_A), F32),
               jnp.zeros((bp, CONV_B - 1, D_B), F32), jnp.zeros((bp, N_HEADS_C, DK_C, DV_C), F32))
    sb = _pick(bs, (16, 8, 4, 2, 1))
    tm_p = _pick(np_rows, (384, 512, 256, 128, 64, 32, 16, 8))
    tm_s = _pick(math.gcd(ns_rows, np_rows), (128, 64, 32, 16, 8))
    nblk = -(-(TOP_K * n) // MOE_BM) + N_EXPERTS
    while (nblk * MOE_BM // TOP_K) % math.lcm(tm_p, tm_s):
        nblk += 1

    new_p = ([], [], [], [])
    new_s = ([], [], [], [])
    x_all = None
    for l in range(depth):
        w_packed = _pack_w_in(w_in[l])
        wg = jnp.concatenate([_block_diag(w_rg[l]), _block_diag(w_ig[l])], axis=1).astype(BF16)
        bg = jnp.concatenate([b_rg[l], b_ig[l]]).reshape(1, -1)
        w_out_b = w_out[l].astype(BF16)
        ng = row(jnp.tile(gla_norm_g[l], N_HEADS_C))
        if l == 0:
            pa_p, pb_p, pc_p, xn_p = _inproj(xp_in, 0, np_rows, row(ln0_g), row(ln0_b), w_packed, True)
            pa_s, pb_s, pc_s, xn_s = _inproj(xs_in, 0, bs * ts, row(ln0_g), row(ln0_b), w_packed, True)
            res_p = xn_p
        else:
            pa_p, pb_p, pc_p = _inproj(x_all, 0, np_rows, row(ln0_g), row(ln0_b), w_packed, False)
            xn_s = jnp.pad(x_all[np_rows:].reshape(bs, dseq, D_MODEL),
                           ((0, 0), (0, ts - dseq), (0, 0))).reshape(bs * ts, D_MODEL)
            pa_s, pb_s, pc_s = _inproj(xn_s, 0, bs * ts, row(ln0_g), row(ln0_b), w_packed, False)
            res_p = x_all

        outs = []
        for (pa, pb, pc, nb, t, tv, bb, st) in (
                (pa_p, pb_p, pc_p, bp, tp, tp, 1, zeros_p),
                (pa_s, pb_s, pc_s, bs, ts, dseq, sb,
                 (state_conv_a[l], state_rglru[l].reshape(bs, 1, D_A), state_conv_b[l], state_gla[l]))):
            ya, ca_new, h_last = _rglru(pa.reshape(nb, t, PA_W), st[0], st[1], conv_a_w[l], row(conv_a_b[l]),
                                        wg, bg, row(lru_lambda[l]), tv, bb)
            yb, cb_new = _convb(pb.reshape(nb, t, PB_W), st[2], conv_b_w[l], row(conv_b_b[l]),
                                row(ln_b_g[l]), row(ln_b_b[l]), tv, bb)
            yc, s_new = _gla(pc, nb, t, st[3], w_gate2[l], row(b_gate[l]), ng, tv, bb)
            outs.append((ya.reshape(nb * t, D_A), yb.reshape(nb * t, D_B), yc,
                         ca_new, h_last.reshape(nb, D_A), cb_new, s_new))
        (ya_p, yb_p, yc_p, *st_p), (ya_s, yb_s, yc_s, *st_s) = outs
        for j in range(4):
            new_p[j].append(st_p[j])
            new_s[j].append(st_s[j])

        valid = lambda a: a.reshape(bs, ts, -1)[:, :dseq].reshape(ns_rows, -1)
        h_all = _outproj(ya_p, yb_p, yc_p, res_p, 0, w_out_b, row(ln1_g[l]), row(ln1_b[l]), alpha,
                         tm_p, nblk * MOE_BM // TOP_K)
        h_all = _outproj(valid(ya_s), valid(yb_s), valid(yc_s), valid(xn_s), 0, w_out_b, row(ln1_g[l]),
                         row(ln1_b[l]), alpha, tm_s, nblk * MOE_BM // TOP_K, out_row0=np_rows, into=h_all)

        top_idx, gates, cnt = _router(h_all, n, router_w[l].T, router_b[l].reshape(-1, 1))
        base, pad_shift, block_e, n_used = _group_layout(cnt[:, ::LANES], nblk, MOE_BM)
        dest = _dest(top_idx, base)
        inv = _invert(pad_shift, dest, MOE_BM)
        y4 = _moe(block_e, n_used, inv, h_all, l, moe_w1, moe_b1[:, :, None, :], moe_w2, moe_b2[:, :, None, :],
                  MOE_BM)
        comb = functools.partial(_combine, y4, gates.T.reshape(-1), h_all, g=row(ln2_g[l]), b=row(ln2_b[l]),
                                 alpha=alpha)
        if l + 1 < depth:
            x_all = comb(0, n)
        else:
            out_p, out_s = comb(0, np_rows), comb(np_rows, ns_rows)

    y_p = out_p.reshape(bp, tp, D_MODEL)[:, N_META:]
    y_s = out_s.reshape(bs, dseq, D_MODEL)
    return (y_p, y_s,
            jnp.stack(new_p[0]), jnp.stack(new_p[1]), jnp.stack(new_p[2]), jnp.stack(new_p[3]),
            jnp.stack(new_s[0]), jnp.stack(new_s[1]), jnp.stack(new_s[2]), jnp.stack(new_s[3]))
```

```python
import functools
import math

import jax
import jax.numpy as jnp
from jax import lax
from jax.experimental import pallas as pl
from jax.experimental.pallas import tpu as pltpu
from jax.experimental.pallas import tpu_sc as plsc

F32 = jnp.float32
BF16 = jnp.bfloat16

D_MODEL = 1024
N_META = 16
D_A = 384
D_B = 256
D_C = 384
N_BLK_A = 8
BLK_A = D_A // N_BLK_A
CONV_A = 4
RG_C = 8.0
CONV_B = 31
N_HEADS_C = 4
DV_C = D_C // N_HEADS_C
DK_C = DV_C // 2
D_QK_C = N_HEADS_C * DK_C
GATE_RANK = 16
GATE_TAU = 16.0
N_EXPERTS = 32
TOP_K = 4
D_FF = D_MODEL
SWIGLU_LIMIT = 7.0
SWIGLU_ALPHA = 1.702
LN_EPS = 1e-5

PA_W = 2 * D_A
PB_W = 2 * D_B
PC_Q, PC_Z, PC_K, PC_V, PC_R, PC_W = 0, 192, 256, 512, 896, 1280
P_W = PA_W + PB_W + PC_W

SUBLANES = 8
LANES = 128
SC_LANES = 16
SC_CORES, SC_SUBCORES = 2, 16
ROW_TILE = D_MODEL // LANES
VMEM_LIMIT_BYTES = 56 * 1024 * 1024
MOE_BM = 512
FF_CHUNK = 512
DMA_GROUP = 8
W_CAST_ROWS = 128
GLA_TRIP = 14
CONV_TRIP = 7
GLA_PASS_TRIP = 3
SAMPLE_PAD_T = 8


def _cparams(*sem):
    return pltpu.CompilerParams(dimension_semantics=sem, vmem_limit_bytes=VMEM_LIMIT_BYTES)


def _sds(shape, dtype=F32):
    return jax.ShapeDtypeStruct(shape, dtype)


def _pick(n, prefs):
    for p in prefs:
        if n % p == 0:
            return p
    raise ValueError(f"no tile for {n} in {prefs}")


def _ln(x, g, b):
    mu = jnp.mean(x, axis=-1, keepdims=True)
    xc = x - mu
    var = jnp.mean(xc * xc, axis=-1, keepdims=True)
    return xc * lax.rsqrt(var + LN_EPS) * g + b


def _sigmoid(x):
    return 1.0 / (1.0 + jnp.exp(-x))


def _split_bf16(x):
    hi = x.astype(BF16)
    lo = (x - hi.astype(F32)).astype(BF16)
    return hi, lo


def _dot(a, b):
    return jnp.dot(a, b, preferred_element_type=F32)


def _for_each(n, per_trip, fn):
    peel = n % per_trip
    for i in range(peel):
        fn(i)

    def trip(t, carry):
        for u in range(per_trip):
            fn(peel + per_trip * t + u)
        return carry
    lax.fori_loop(0, n // per_trip, trip, 0)


def _inproj_kernel(x_ref, g_ref, b_ref, w_ref, pa_ref, pb_ref, pc_ref, *maybe_xn, apply_ln):
    x = x_ref[...]
    if apply_ln:
        x = _ln(x, g_ref[...], b_ref[...])
        maybe_xn[0][...] = x
    xb = x.astype(BF16)
    pa_ref[...] = _dot(xb, w_ref[:, 0:PA_W])
    pb_ref[...] = _dot(xb, w_ref[:, PA_W:PA_W + PB_W])
    pc_ref[...] = _dot(xb, w_ref[:, PA_W + PB_W:P_W])


def _inproj(x, row0, nrows, ln_g, ln_b, w_packed, apply_ln):
    tm = _pick(nrows, (384, 512, 256, 128, 64, 32, 16, 8))
    while row0 % tm:
        tm //= 2
    off = row0 // tm
    const = lambda i: (0, 0)
    row = lambda i: (i, 0)
    out_shape = [_sds((nrows, PA_W)), _sds((nrows, PB_W)), _sds((nrows, PC_W))]
    out_specs = [pl.BlockSpec((tm, PA_W), row), pl.BlockSpec((tm, PB_W), row), pl.BlockSpec((tm, PC_W), row)]
    if apply_ln:
        out_shape.append(_sds((nrows, D_MODEL)))
        out_specs.append(pl.BlockSpec((tm, D_MODEL), row))
    return pl.pallas_call(
        functools.partial(_inproj_kernel, apply_ln=apply_ln),
        grid=(nrows // tm,),
        in_specs=[pl.BlockSpec((tm, D_MODEL), lambda i: (i + off, 0)),
                  pl.BlockSpec((1, D_MODEL), const), pl.BlockSpec((1, D_MODEL), const),
                  pl.BlockSpec((D_MODEL, P_W), const)],
        out_specs=out_specs, out_shape=out_shape,
        compiler_params=_cparams("parallel"), name="inproj",
    )(x, ln_g, ln_b, w_packed)


def _rglru_kernel(p_ref, cbuf_ref, h0_ref, cw_ref, cb_ref, wg_ref, bg_ref, lam_ref,
                  y_ref, cnew_ref, hlast_ref, xp_scr, a_scr, h_scr, *, T, Tc, Tv, Bb):
    lam = lam_ref[...]
    softplus_neg = jnp.maximum(-lam, 0.0) + jnp.log1p(jnp.exp(-jnp.abs(lam)))
    c_decay = -RG_C * softplus_neg
    cw = cw_ref[...]
    cb = cb_ref[...]
    bg = bg_ref[...]
    sub = lax.broadcasted_iota(jnp.int32, (Tc, D_A), 0) % SUBLANES
    halo = SUBLANES - (CONV_A - 1)
    for b in range(Bb):
        xp_scr[halo:SUBLANES, :] = cbuf_ref[b]
        xp_scr[SUBLANES:SUBLANES + T, :] = p_ref[b, :, 0:D_A]
        cnew_ref[b] = xp_scr[halo + Tv:SUBLANES + Tv, :]

        def chunk(ci, h_b):
            r0 = pl.multiple_of(ci * Tc, SUBLANES)
            win = xp_scr[pl.ds(r0, Tc + SUBLANES), :]
            xc = cb + cw[0:1] * win[halo:halo + Tc]
            for j in range(1, CONV_A):
                xc = xc + cw[j:j + 1] * win[halo + j:halo + j + Tc]
            gates = _dot(xc.astype(BF16), wg_ref[...]) + bg
            r = _sigmoid(gates[:, 0:D_A])
            i = _sigmoid(gates[:, D_A:2 * D_A])
            log_a = c_decay * r
            a = jnp.exp(log_a)
            u = jnp.sqrt(1.0 - a * a) * (i * xc)
            for s in (1, 2, 4):
                keep = sub >= s
                a_prev = pltpu.roll(a, s, 0)
                u_prev = pltpu.roll(u, s, 0)
                u = jnp.where(keep, a * u_prev + u, u)
                a = jnp.where(keep, a * a_prev, a)
            a_scr[...] = a
            h_scr[pl.ds(r0, Tc), :] = u

            def group(gi, h_b):
                c0 = pl.multiple_of(gi * SUBLANES, SUBLANES)
                g0 = pl.multiple_of(r0 + gi * SUBLANES, SUBLANES)
                h8 = a_scr[pl.ds(c0, SUBLANES), :] * h_b + h_scr[pl.ds(g0, SUBLANES), :]
                h_scr[pl.ds(g0, SUBLANES), :] = h8
                return jnp.broadcast_to(h8[SUBLANES - 1:SUBLANES, :], (SUBLANES, D_A))

            h_b = lax.fori_loop(0, Tc // SUBLANES, group, h_b)
            ga = p_ref[b, pl.ds(r0, Tc), D_A:2 * D_A]
            gelu = 0.5 * ga * (1.0 + jnp.tanh(0.7978845608028654 * (ga + 0.044715 * ga * ga * ga)))
            y_ref[b, pl.ds(r0, Tc), :] = h_scr[pl.ds(r0, Tc), :] * gelu
            return h_b

        h_b = jnp.broadcast_to(h0_ref[b], (SUBLANES, D_A))
        lax.fori_loop(0, T // Tc, chunk, h_b)
        hlast_ref[b] = h_scr[Tv - 1:Tv, :]


def _rglru(pa3, cbuf, h0, cw, cb, wg, bg, lam, Tv, Bb):
    B, T, _ = pa3.shape
    Tc = _pick(T, (344, 256, 128, 64, 48, 32, 16, 8))
    const2 = lambda i: (0, 0)
    seq3 = lambda i: (i, 0, 0)
    return pl.pallas_call(
        functools.partial(_rglru_kernel, T=T, Tc=Tc, Tv=Tv, Bb=Bb),
        grid=(B // Bb,),
        in_specs=[pl.BlockSpec((Bb, T, PA_W), seq3), pl.BlockSpec((Bb, CONV_A - 1, D_A), seq3),
                  pl.BlockSpec((Bb, 1, D_A), seq3), pl.BlockSpec((CONV_A, D_A), const2),
                  pl.BlockSpec((1, D_A), const2), pl.BlockSpec((D_A, 2 * D_A), const2),
                  pl.BlockSpec((1, 2 * D_A), const2), pl.BlockSpec((1, D_A), const2)],
        out_specs=[pl.BlockSpec((Bb, T, D_A), seq3), pl.BlockSpec((Bb, CONV_A - 1, D_A), seq3),
                   pl.BlockSpec((Bb, 1, D_A), seq3)],
        out_shape=[_sds((B, T, D_A)), _sds((B, CONV_A - 1, D_A)), _sds((B, 1, D_A))],
        scratch_shapes=[pltpu.VMEM((T + 2 * SUBLANES, D_A), F32), pltpu.VMEM((Tc, D_A), F32),
                        pltpu.VMEM((T, D_A), F32)],
        compiler_params=_cparams("parallel"), name="rglru",
    )(pa3, cbuf, h0, cw, cb, wg, bg, lam)


B_HALO = 32


def _convb_kernel(p_ref, buf_ref, w_ref, cb_ref, g_ref, b_ref, y_ref, bnew_ref, u_scr, *, T, Tc, Tv, Bb):
    w = w_ref[...]
    cb = cb_ref[...]
    g = g_ref[...]
    bb = b_ref[...]
    first = B_HALO - (CONV_B - 1)
    for b in range(Bb):
        u_scr[0:first, :] = jnp.zeros((first, D_B), F32)
        u_scr[first:B_HALO, :] = buf_ref[b]
        u_scr[B_HALO:B_HALO + T, :] = p_ref[b, :, 0:D_B] * _sigmoid(p_ref[b, :, D_B:2 * D_B])
        bnew_ref[b] = u_scr[first + Tv:B_HALO + Tv, :]

        def chunk(ci):
            r0 = ci * Tc if isinstance(ci, int) else pl.multiple_of(ci * Tc, SUBLANES)
            win = u_scr[pl.ds(r0, Tc + B_HALO), :]
            shifted = [win] + [pltpu.roll(win, Tc + B_HALO - s, 0) for s in range(1, SUBLANES)]
            acc = cb
            for j in range(CONV_B):
                a, s = divmod(first + j, SUBLANES)
                acc = acc + w[j:j + 1] * shifted[s][a * SUBLANES:a * SUBLANES + Tc]
            yn = _ln(acc, g, bb)
            y_ref[b, pl.ds(r0, Tc), :] = yn * _sigmoid(yn)

        _for_each(T // Tc, CONV_TRIP, chunk)


def _convb(pb3, buf, w, cb, g, b, Tv, Bb):
    B, T, _ = pb3.shape
    Tc = _pick(T, (48, 32, 16, 8))
    const2 = lambda i: (0, 0)
    seq3 = lambda i: (i, 0, 0)
    return pl.pallas_call(
        functools.partial(_convb_kernel, T=T, Tc=Tc, Tv=Tv, Bb=Bb),
        grid=(B // Bb,),
        in_specs=[pl.BlockSpec((Bb, T, PB_W), seq3), pl.BlockSpec((Bb, CONV_B - 1, D_B), seq3),
                  pl.BlockSpec((CONV_B, D_B), const2), pl.BlockSpec((1, D_B), const2),
                  pl.BlockSpec((1, D_B), const2), pl.BlockSpec((1, D_B), const2)],
        out_specs=[pl.BlockSpec((Bb, T, D_B), seq3), pl.BlockSpec((Bb, CONV_B - 1, D_B), seq3)],
        out_shape=[_sds((B, T, D_B)), _sds((B, CONV_B - 1, D_B))],
        scratch_shapes=[pltpu.VMEM((T + B_HALO, D_B), F32)],
        compiler_params=_cparams("parallel"), name="convb",
    )(pb3, buf, w, cb, g, b)


def _gla_kernel(p_ref, s0_ref, wg2_ref, bgate_ref, ng_ref, y_ref, snew_ref, s_scr, g_scr, *, T, C, Tb, Tv, Bb):
    ri = lax.broadcasted_iota(jnp.int32, (C, C), 0)
    ci_ = lax.broadcasted_iota(jnp.int32, (C, C), 1)
    tril = ri >= ci_
    lane_k = lax.broadcasted_iota(jnp.int32, (1, D_QK_C), 1)
    lane_v = lax.broadcasted_iota(jnp.int32, (1, D_C), 1)
    hm_k = [(lane_k >= h * DK_C) & (lane_k < (h + 1) * DK_C) for h in range(N_HEADS_C)]
    hm_v = [(lane_v >= h * DV_C) & (lane_v < (h + 1) * DV_C) for h in range(N_HEADS_C)]
    rs = lax.broadcasted_iota(jnp.int32, (D_C, D_QK_C), 0)
    cs = lax.broadcasted_iota(jnp.int32, (D_C, D_QK_C), 1)
    bd_t = (rs >= 0) & (rs < 0)
    for h in range(N_HEADS_C):
        bd_t = bd_t | ((rs >= h * DV_C) & (rs < (h + 1) * DV_C) & (cs >= h * DK_C) & (cs < (h + 1) * DK_C))
    rm = lax.broadcasted_iota(jnp.int32, (D_C, D_C), 0)
    cm = lax.broadcasted_iota(jnp.int32, (D_C, D_C), 1)
    seg = (rm >= 0) & (rm < 0)
    for h in range(N_HEADS_C):
        seg = seg | ((rm >= h * DV_C) & (rm < (h + 1) * DV_C) & (cm >= h * DV_C) & (cm < (h + 1) * DV_C))
    mseg = jnp.where(seg, 1.0, 0.0).astype(BF16)
    wg2 = wg2_ref[...].astype(BF16)
    bgate = bgate_ref[...]
    ng = ng_ref[...]
    rowi = lax.broadcasted_iota(jnp.int32, (C, 1), 0)
    tdims = (((0,), (0,)), ((), ()))

    n_chunks = T // C
    nt_dims = (((1,), (1,)), ((), ()))
    tril4 = jnp.concatenate([tril] * N_HEADS_C, axis=0)
    scan_shifts = [s for s in (1, 2, 4, 8, 16, 32) if s < C]
    scan_keep = [rowi >= s for s in scan_shifts]

    def tile_rows(ti, tb, base=0):
        r0 = ti * tb
        return pl.ds(base + (r0 if isinstance(ti, int) else pl.multiple_of(r0, tb)), tb)

    def gates(ti):
        rows = tile_rows(ti, Tb)
        z = p_ref[rows, PC_Z:PC_Z + GATE_RANK]
        pre = _dot(z.astype(BF16), wg2) + bgate
        g = (jnp.minimum(pre, 0.0) - jnp.log1p(jnp.exp(-jnp.abs(pre)))) * (1.0 / GATE_TAU)
        rid = ti * Tb + lax.broadcasted_iota(jnp.int32, (Tb, 1), 0)
        if Bb > 1:
            rid = rid & (T - 1)
        g_scr[rows, :] = jnp.where(rid < Tv, g, 0.0)
    _for_each(Bb * T // Tb, GLA_PASS_TRIP, gates)

    for b in range(Bb):
        def chunk(ci):
            rows = tile_rows(ci, C, b * T)
            q = p_ref[rows, PC_Q:PC_Q + D_QK_C] * (DK_C ** -0.5)
            k = p_ref[rows, PC_K:PC_K + D_QK_C]
            v = p_ref[rows, PC_V:PC_V + D_C]
            k = jnp.where((ci * C + rowi) < Tv, k, 0.0)
            gcum = g_scr[rows, :]
            for s, keep in zip(scan_shifts, scan_keep):
                gcum = gcum + jnp.where(keep, pltpu.roll(gcum, s, 0), 0.0)
            g_last = gcum[C - 1:C, :]
            g_mid = gcum[C // 2 - 1:C // 2, :]
            vb = v.astype(BF16)
            qt = q * jnp.exp(gcum - g_mid)
            ktb = (k * jnp.exp(g_mid - gcum)).astype(BF16)
            q4 = jnp.concatenate([jnp.where(hm_k[h], qt, 0.0) for h in range(N_HEADS_C)], axis=0).astype(BF16)
            sc = lax.dot_general(q4, ktb, nt_dims, preferred_element_type=F32)
            r4 = _dot(jnp.where(tril4, sc, 0.0).astype(BF16), vb)
            o = jnp.where(hm_v[0], r4[0:C], 0.0)
            for h in range(1, N_HEADS_C):
                o = o + jnp.where(hm_v[h], r4[h * C:(h + 1) * C], 0.0)
            kd = (k * jnp.exp(g_last - gcum)).astype(BF16)
            upd_t = lax.dot_general(vb, kd, tdims, preferred_element_type=F32)
            return rows, o, (q * jnp.exp(gcum)).astype(BF16), jnp.exp(g_last), jnp.where(bd_t, upd_t, 0.0)

        def apply_state(rows, o, qg, dec, upd):
            s_in = s_scr[...]
            y_ref[rows, :] = o + lax.dot_general(qg, s_in.astype(BF16), nt_dims, preferred_element_type=F32)
            s_scr[...] = s_in * dec + upd

        def trip(ti, carry):
            parts = [chunk(peel + GLA_TRIP * ti + u) for u in range(GLA_TRIP)]
            for part in parts:
                apply_state(*part)
            return carry

        s_scr[...] = jnp.zeros((D_C, D_QK_C), F32)
        for h in range(N_HEADS_C):
            s_scr[h * DV_C:(h + 1) * DV_C, h * DK_C:(h + 1) * DK_C] = s0_ref[b, h]
        peel = n_chunks % GLA_TRIP
        for ci in range(peel):
            apply_state(*chunk(ci))
        lax.fori_loop(0, n_chunks // GLA_TRIP, trip, 0)
        for h in range(N_HEADS_C):
            snew_ref[b, h] = s_scr[h * DV_C:(h + 1) * DV_C, h * DK_C:(h + 1) * DK_C]

    def finish(ti):
        rows = tile_rows(ti, Tb)
        o = y_ref[rows, :]
        rg = p_ref[rows, PC_R:PC_R + D_C]
        o2_hi, o2_lo = _split_bf16(o * o)
        ms = (_dot(o2_hi, mseg) + _dot(o2_lo, mseg)) * (1.0 / DV_C)
        y_ref[rows, :] = o * lax.rsqrt(ms + LN_EPS) * ng * (rg * _sigmoid(rg))
    _for_each(Bb * T // Tb, GLA_PASS_TRIP, finish)


def _gla(pc, B, T, s0, wg2, bgate, ng, Tv, Bb):
    assert Bb == 1 or T & (T - 1) == 0
    C = _pick(T, (48, 32, 16, 8))
    Tb = _pick(Bb * T, (344, 256, 128, 64, 48, 32, 16, 8))
    const2 = lambda i: (0, 0)
    row2 = lambda i: (i, 0)
    seq4 = lambda i: (i, 0, 0, 0)
    st = (Bb, N_HEADS_C, DV_C, DK_C)
    y, s_new_t = pl.pallas_call(
        functools.partial(_gla_kernel, T=T, C=C, Tb=Tb, Tv=Tv, Bb=Bb),
        grid=(B // Bb,),
        in_specs=[pl.BlockSpec((Bb * T, PC_W), row2), pl.BlockSpec(st, seq4),
                  pl.BlockSpec((GATE_RANK, D_QK_C), const2), pl.BlockSpec((1, D_QK_C), const2),
                  pl.BlockSpec((1, D_C), const2)],
        out_specs=[pl.BlockSpec((Bb * T, D_C), row2), pl.BlockSpec(st, seq4)],
        out_shape=[_sds((B * T, D_C)), _sds((B, N_HEADS_C, DV_C, DK_C))],
        scratch_shapes=[pltpu.VMEM((D_C, D_QK_C), F32), pltpu.VMEM((Bb * T, D_QK_C), F32)],
        compiler_params=_cparams("parallel"), name="gla",
    )(pc, jnp.swapaxes(s0, 2, 3), wg2, bgate, ng)
    return y, jnp.swapaxes(s_new_t, 2, 3)


def _tiles_to_rows(ref, lead, rows):
    return jnp.concatenate([ref[lead + (pl.ds(s, rows, stride=ROW_TILE), slice(None))]
                            for s in range(ROW_TILE)], axis=1)


def _rows_to_tiles(ref, lead, rows, val):
    for s in range(ROW_TILE):
        ref[lead + (pl.ds(s, rows, stride=ROW_TILE), slice(None))] = val[:, s * LANES:(s + 1) * LANES]


def _outproj_kernel(ya_ref, yb_ref, yc_ref, x_ref, w_ref, g_ref, b_ref, *rest, alpha, n_blocks):
    h_ref = rest[-1]

    @pl.when(pl.program_id(0) < n_blocks)
    def _():
        y_cat = jnp.concatenate([ya_ref[...].astype(BF16), yb_ref[...].astype(BF16), yc_ref[...].astype(BF16)],
                                axis=1)
        y = _dot(y_cat, w_ref[...])
        h = _ln(alpha * x_ref[...] + y, g_ref[...], b_ref[...])
        _rows_to_tiles(h_ref, (), h.shape[0], h)

    @pl.when(pl.program_id(0) >= n_blocks)
    def _():
        h_ref[...] = jnp.zeros(h_ref.shape, F32)


def _outproj(ya, yb, yc, x, row0, w, g, b, alpha, tm, out_rows, out_row0=0, into=None):
    n = ya.shape[0]
    nb = n // tm
    assert n % tm == 0 and row0 % tm == 0 and out_row0 % tm == 0 and out_rows % tm == 0
    off, out_off = row0 // tm, out_row0 // tm
    grid = nb if into is not None else out_rows // tm
    row = lambda i: (jnp.minimum(i, nb - 1), 0)
    const = lambda i: (0, 0)
    in_specs = [pl.BlockSpec((tm, D_A), row), pl.BlockSpec((tm, D_B), row), pl.BlockSpec((tm, D_C), row),
                pl.BlockSpec((tm, D_MODEL), lambda i: (jnp.minimum(i, nb - 1) + off, 0)),
                pl.BlockSpec((D_MODEL, D_MODEL), const), pl.BlockSpec((1, D_MODEL), const),
                pl.BlockSpec((1, D_MODEL), const)]
    args = [ya, yb, yc, x, w, g, b]
    aliases = {}
    if into is not None:
        in_specs.append(pl.BlockSpec(memory_space=pl.ANY))
        args.append(into)
        aliases = {len(args) - 1: 0}
    return pl.pallas_call(
        functools.partial(_outproj_kernel, alpha=alpha, n_blocks=nb),
        grid=(grid,),
        in_specs=in_specs,
        out_specs=pl.BlockSpec((tm * ROW_TILE, LANES), lambda i: (i + out_off, 0)),
        out_shape=_sds((out_rows * ROW_TILE, LANES)),
        input_output_aliases=aliases,
        compiler_params=_cparams("arbitrary"), name="outproj",
    )(*args)


def _router_kernel(h_ref, wt_ref, b_ref, idx_ref, gate_ref, cnt_ref):
    nt = (((1,), (1,)), ((), ()))
    hh, hl = _split_bf16(_tiles_to_rows(h_ref, (), idx_ref.shape[1]))
    wh, wl = _split_bf16(wt_ref[...])
    logits = (lax.dot_general(wh, hh, nt, preferred_element_type=F32)
              + lax.dot_general(wh, hl, nt, preferred_element_type=F32)
              + lax.dot_general(wl, hh, nt, preferred_element_type=F32)) + b_ref[...]
    eid = lax.broadcasted_iota(jnp.int32, logits.shape, 0)
    vals = []
    member = jnp.zeros(logits.shape, F32)
    for k in range(TOP_K):
        m = jnp.max(logits, axis=0, keepdims=True)
        sel = jnp.min(jnp.where(logits == m, eid, N_EXPERTS), axis=0, keepdims=True)
        idx_ref[k:k + 1, :] = sel
        vals.append(m)
        hit = eid == sel
        member = jnp.where(hit, 1.0, member)
        logits = jnp.where(hit, -jnp.inf, logits)
    es = [jnp.exp(v - vals[0]) for v in vals]
    tot = es[0] + es[1] + es[2] + es[3]
    for k in range(TOP_K):
        gate_ref[k:k + 1, :] = es[k] / tot
    cnt_ref[...] = jnp.broadcast_to(jnp.sum(member, axis=1, keepdims=True), cnt_ref.shape)


def _router_tile(n):
    return _pick(n, (896, 640, 512, 384, 256, 128))


def _router(h, n, wt, b):
    tm = _router_tile(n)
    nt = n // tm
    return pl.pallas_call(
        _router_kernel,
        grid=(nt,),
        in_specs=[pl.BlockSpec((tm * ROW_TILE, LANES), lambda i: (i, 0)),
                  pl.BlockSpec((N_EXPERTS, D_MODEL), lambda i: (0, 0)),
                  pl.BlockSpec((N_EXPERTS, 1), lambda i: (0, 0))],
        out_specs=[pl.BlockSpec((TOP_K, tm), lambda i: (0, i)), pl.BlockSpec((TOP_K, tm), lambda i: (0, i)),
                   pl.BlockSpec((N_EXPERTS, LANES), lambda i: (0, i))],
        out_shape=[_sds((TOP_K, n), jnp.int32), _sds((TOP_K, n)), _sds((N_EXPERTS, nt * LANES))],
        compiler_params=_cparams("parallel"), name="router",
    )(h, wt, b)


def _dest_kernel(idx_ref, base_ref, dest_ref):
    tm = idx_ref.shape[1]
    eid = lax.broadcasted_iota(jnp.int32, (N_EXPERTS, tm), 0)
    hits = [eid == idx_ref[k:k + 1, :] for k in range(TOP_K)]
    member = jnp.zeros((N_EXPERTS, tm), F32)
    for k in range(TOP_K):
        member = jnp.where(hits[k], 1.0, member)
    earlier = (lax.broadcasted_iota(jnp.int32, (tm, tm), 0) < lax.broadcasted_iota(jnp.int32, (tm, tm), 1))
    rank = _dot(member.astype(BF16), jnp.where(earlier, 1.0, 0.0).astype(BF16))
    pos = base_ref[...] + rank
    for k in range(TOP_K):
        dest_ref[k:k + 1, :] = jnp.sum(jnp.where(hits[k], pos, 0.0), axis=0, keepdims=True).astype(jnp.int32)


def _dest(top_idx, base):
    n = top_idx.shape[1]
    tm = _router_tile(n)
    return pl.pallas_call(
        _dest_kernel,
        grid=(n // tm,),
        in_specs=[pl.BlockSpec((TOP_K, tm), lambda i: (0, i)),
                  pl.BlockSpec((None, N_EXPERTS, 1), lambda i: (i, 0, 0))],
        out_specs=pl.BlockSpec((TOP_K, tm), lambda i: (0, i)),
        out_shape=_sds((TOP_K, n), jnp.int32),
        compiler_params=_cparams("parallel"), name="moe_dest",
    )(top_idx, base)


def _invert(pad_shift, dest, bm):
    k, n = dest.shape
    n_rows = pad_shift.shape[0] * bm
    ch = next(c for c in range(min(n, 4096) // SC_LANES * SC_LANES, 0, -SC_LANES) if n % c == 0)
    fill = (jnp.arange(n_rows, dtype=jnp.int32).reshape(-1, bm) + pad_shift[:, None]).reshape(-1)
    mesh = plsc.VectorSubcoreMesh(core_axis_name="c", subcore_axis_name="s", num_cores=SC_CORES,
                                  num_subcores=SC_SUBCORES)

    @functools.partial(
        pl.kernel, mesh=mesh, out_type=_sds((n_rows,), jnp.int32),
        scratch_types=[pltpu.VMEM((n_rows,), jnp.int32), pltpu.VMEM((ch,), jnp.int32)],
        compiler_params=pltpu.CompilerParams(needs_layout_passes=False), name="moe_invert_sc")
    def run(fill_hbm, dest_hbm, inv_hbm, inv_v, d_v):
        first = jnp.logical_and(lax.axis_index("c") == 0, lax.axis_index("s") == 0)

        @pl.when(first)
        def _():
            pltpu.sync_copy(fill_hbm, inv_v)
            lanes = lax.iota(jnp.int32, SC_LANES)
            for slot in range(k):
                def chunk(c, carry):
                    pltpu.sync_copy(dest_hbm.at[pl.ds(slot * n + c * ch, ch)], d_v)

                    def vec(v, carry):
                        idx = d_v[pl.ds(v * SC_LANES, SC_LANES)]
                        plsc.store_scatter(inv_v, [idx], (c * ch + v * SC_LANES + lanes) * k + slot)
                        return carry
                    return lax.fori_loop(0, ch // SC_LANES, vec, carry)
                lax.fori_loop(0, n // ch, chunk, 0)
            pltpu.sync_copy(inv_v, inv_hbm)

    return run(fill, dest.reshape(-1))


def _moe_kernel(be_ref, nu_ref, inv_ref, h_hbm, w1f_ref, b1_ref, w2f_ref, b2_ref, y4_hbm,
                xbuf, xb, obuf, w1_ref, w2_ref, gsem, ssem, *, bm):
    i = pl.program_id(0)
    n_used = nu_ref[0]
    last_blk = pl.num_programs(0) - 1
    n_ff = D_FF // FF_CHUNK

    def tile(r):
        start = r * ROW_TILE
        return pl.ds(start if isinstance(r, int) else pl.multiple_of(start, ROW_TILE), ROW_TILE)

    def gather_row(blk, slot, j, r=None):
        r = inv_ref[blk * bm + j] if r is None else r
        t = lax.shift_right_logical(r, 2)
        pltpu.make_async_copy(h_hbm.at[tile(t), :], xbuf.at[slot, tile(j), :], gsem.at[slot]).start()

    def scatter_row(blk, slot, j, r=None):
        r = inv_ref[blk * bm + j] if r is None else r
        pltpu.make_async_copy(obuf.at[slot, tile(j), :], y4_hbm.at[tile(r), :], ssem.at[slot]).start()

    def wait_gather(slot):
        pltpu.make_async_copy(h_hbm.at[pl.ds(0, bm * ROW_TILE), :], xbuf.at[slot], gsem.at[slot]).wait()

    def wait_scatter(slot):
        pltpu.make_async_copy(obuf.at[slot], y4_hbm.at[pl.ds(0, bm * ROW_TILE), :], ssem.at[slot]).wait()

    def loop_rows(fn, blk, slot):
        def body(j, c):
            fn(blk, slot, j)
            return c
        lax.fori_loop(0, bm, body, 0)

    def issue_rows(fn, blk, slot, j_lo, j_hi):
        for j0 in range(j_lo, j_hi, DMA_GROUP):
            js = range(j0, j0 + DMA_GROUP)
            ids = [inv_ref[blk * bm + j] for j in js]
            for j, r in zip(js, ids):
                fn(blk, slot, j, r)

    def step():
        slot = i % 2

        @pl.when(i + 1 < n_used)
        def _():
            issue_rows(gather_row, i + 1, 1 - slot, 0, bm)
        wait_gather(slot)
        xb[...] = _tiles_to_rows(xbuf, (slot,), bm).astype(BF16)
        acc = jnp.zeros((bm, D_MODEL), F32)
        for c in range(n_ff):
            lo = c * FF_CHUNK
            x = xb[...]
            hg = _dot(x, w1_ref[:, lo:lo + FF_CHUNK]) + b1_ref[:, lo:lo + FF_CHUNK]
            hl = _dot(x, w1_ref[:, D_FF + lo:D_FF + lo + FF_CHUNK]) + b1_ref[:, D_FF + lo:D_FF + lo + FF_CHUNK]
            gate = jnp.minimum(hg, SWIGLU_LIMIT)
            lin = jnp.clip(hl, -SWIGLU_LIMIT, SWIGLU_LIMIT)
            act = gate * _sigmoid(SWIGLU_ALPHA * gate) * (lin + 1.0)
            acc = acc + _dot(act.astype(BF16), w2_ref[lo:lo + FF_CHUNK, :])

        @pl.when(i >= 2)
        def _():
            wait_scatter(slot)
        _rows_to_tiles(obuf, (slot,), bm, acc + b2_ref[...])
        issue_rows(scatter_row, i, slot, 0, bm)

    new_expert = jnp.logical_or(i == 0, be_ref[i] != be_ref[jnp.maximum(i - 1, 0)])

    @pl.when(jnp.logical_and(new_expert, i < n_used))
    def _():
        def cast_rows(r, c):
            rows = pl.ds(pl.multiple_of(r * W_CAST_ROWS, W_CAST_ROWS), W_CAST_ROWS)
            w1_ref[rows, :] = w1f_ref[rows, :].astype(BF16)
            w2_ref[rows, :] = w2f_ref[rows, :].astype(BF16)
            return c
        lax.fori_loop(0, D_MODEL // W_CAST_ROWS, cast_rows, 0)

    @pl.when(jnp.logical_and(i == 0, n_used > 0))
    def _():
        loop_rows(gather_row, 0, 0)

    @pl.when(i < n_used)
    def _():
        step()

    @pl.when(i == n_used - 1)
    def _():
        slot = i % 2
        wait_scatter(slot)

        @pl.when(i >= 1)
        def _():
            wait_scatter(1 - slot)
        obuf[0] = jnp.zeros((bm * ROW_TILE, LANES), F32)

        def zero_block(blk, c):
            rows = bm * ROW_TILE
            cp = pltpu.make_async_copy(obuf.at[0], y4_hbm.at[pl.ds(pl.multiple_of(blk * rows, rows), rows), :],
                                       ssem.at[0])
            cp.start()
            cp.wait()
            return c
        lax.fori_loop(n_used, last_blk + 1, zero_block, 0)


def _moe(block_e, n_used, inv, h, layer, w1, b1, w2, b2, bm):
    nblk = block_e.shape[0]
    wmap = lambda i, be, nu, iv: (layer, be[i], 0, 0)
    return pl.pallas_call(
        functools.partial(_moe_kernel, bm=bm),
        grid_spec=pltpu.PrefetchScalarGridSpec(
            num_scalar_prefetch=3, grid=(nblk,),
            in_specs=[pl.BlockSpec(memory_space=pl.ANY),
                      pl.BlockSpec((None, None, D_MODEL, 2 * D_FF), wmap),
                      pl.BlockSpec((None, None, 1, 2 * D_FF), wmap),
                      pl.BlockSpec((None, None, D_FF, D_MODEL), wmap),
                      pl.BlockSpec((None, None, 1, D_MODEL), wmap)],
            out_specs=pl.BlockSpec(memory_space=pl.ANY),
            scratch_shapes=[pltpu.VMEM((2, bm * ROW_TILE, LANES), F32), pltpu.VMEM((bm, D_MODEL), BF16),
                            pltpu.VMEM((2, bm * ROW_TILE, LANES), F32),
                            pltpu.VMEM((D_MODEL, 2 * D_FF), BF16), pltpu.VMEM((D_FF, D_MODEL), BF16),
                            pltpu.SemaphoreType.DMA((2,)), pltpu.SemaphoreType.DMA((2,))]),
        out_shape=_sds((nblk * bm * ROW_TILE, LANES)),
        compiler_params=_cparams("arbitrary"), name="moe_experts",
    )(block_e, n_used, inv, h, w1, b1, w2, b2)


def _combine_kernel(gates_ref, y4_ref, h_ref, g_ref, b_ref, out_ref, acc_scr, *, alpha, tok_off):
    tm = out_ref.shape[0]
    tok0 = tok_off + pl.program_id(0) * tm

    def group(gi, carry):
        t0 = pl.multiple_of(gi * SUBLANES, SUBLANES)
        tiles = []
        for u in range(SUBLANES):
            t = t0 + u
            tile = alpha * h_ref[pl.ds(pl.multiple_of(t * ROW_TILE, ROW_TILE), ROW_TILE), :]
            for k in range(TOP_K):
                r = (t * TOP_K + k) * ROW_TILE
                tile = tile + (gates_ref[(tok0 + t) * TOP_K + k]
                               * y4_ref[pl.ds(pl.multiple_of(r, ROW_TILE), ROW_TILE), :])
            tiles.append(tile.reshape(1, ROW_TILE, LANES))
        rows = jnp.swapaxes(jnp.concatenate(tiles, axis=0), 0, 1)
        for s in range(ROW_TILE):
            acc_scr[pl.ds(t0, SUBLANES), s * LANES:(s + 1) * LANES] = rows[s]
        return carry
    lax.fori_loop(0, tm // SUBLANES, group, 0)
    out_ref[...] = _ln(acc_scr[...], g_ref[...], b_ref[...])


def _combine(y4, gates_flat, h, row0, n, g, b, alpha):
    tm = _pick(math.gcd(n, row0) if row0 else n, (448, 384, 224, 128, 64, 32, 16, 8))
    off = row0 // tm
    row = lambda i, gt: (i + off, 0)
    const = lambda i, gt: (0, 0)
    return pl.pallas_call(
        functools.partial(_combine_kernel, alpha=alpha, tok_off=row0),
        grid_spec=pltpu.PrefetchScalarGridSpec(
            num_scalar_prefetch=1, grid=(n // tm,),
            in_specs=[pl.BlockSpec((TOP_K * tm * ROW_TILE, LANES), row),
                      pl.BlockSpec((tm * ROW_TILE, LANES), row),
                      pl.BlockSpec((1, D_MODEL), const), pl.BlockSpec((1, D_MODEL), const)],
            out_specs=pl.BlockSpec((tm, D_MODEL), lambda i, gt: (i, 0)),
            scratch_shapes=[pltpu.VMEM((tm, D_MODEL), F32)]),
        out_shape=_sds((n, D_MODEL)),
        compiler_params=_cparams("arbitrary"), name="moe_combine",
    )(gates_flat, y4, h, g, b)


def _group_layout(cnt_tiles, nblk, bm):
    cnt = cnt_tiles.astype(jnp.int32)
    counts = jnp.sum(cnt, axis=1)
    padded = (counts + bm - 1) // bm * bm
    pad_end = jnp.cumsum(padded)
    pad_start = pad_end - padded
    base = pad_start[:, None] + jnp.cumsum(cnt, axis=1) - cnt
    starts = jnp.arange(nblk, dtype=jnp.int32) * bm
    groups_before = jnp.sum((pad_end[None, :] <= starts[:, None]).astype(jnp.int32), axis=1)
    block_e = jnp.minimum(groups_before, N_EXPERTS - 1).astype(jnp.int32)
    n_used = (pad_end[-1] // bm).astype(jnp.int32).reshape(1)
    real_before = jnp.sum(jnp.where(pad_start[None, :] <= starts[:, None], counts[None, :], 0), axis=1)
    pad_shift = (jnp.sum(counts) - real_before).astype(jnp.int32)
    return base.T.astype(F32)[:, :, None], pad_shift, block_e, n_used


def _pack_w_in(w):
    o = 0
    xa, o = w[:, o:o + D_A], o + D_A
    ga, o = w[:, o:o + D_A], o + D_A
    vb, o = w[:, o:o + D_B], o + D_B
    gb, o = w[:, o:o + D_B], o + D_B
    q, o = w[:, o:o + D_QK_C], o + D_QK_C
    k, o = w[:, o:o + D_QK_C], o + D_QK_C
    v, o = w[:, o:o + D_C], o + D_C
    r, o = w[:, o:o + D_C], o + D_C
    z = w[:, o:o + GATE_RANK]
    zq = jnp.zeros((w.shape[0], PC_K - PC_Z - GATE_RANK), w.dtype)
    zk = jnp.zeros((w.shape[0], PC_V - PC_K - D_QK_C), w.dtype)
    return jnp.concatenate([xa, ga, vb, gb, q, z, zq, k, zk, v, r], axis=1).astype(BF16)


def _block_diag(w):
    eye = jnp.eye(N_BLK_A, dtype=w.dtype)
    return jnp.einsum("hij,hg->higj", w, eye).reshape(D_A, D_A)


def kernel(x_prompt, x_sample, state_conv_a, state_rglru, state_conv_b, state_gla, meta_tokens, ln0_g, ln0_b,
           w_in, conv_a_w, conv_a_b, w_rg, b_rg, w_ig, b_ig, lru_lambda, conv_b_w, conv_b_b, ln_b_g, ln_b_b,
           w_gate2, b_gate, gla_norm_g, w_out, ln1_g, ln1_b, router_w, router_b, moe_w1, moe_b1, moe_w2, moe_b2,
           ln2_g, ln2_b):
    bp, seq, _ = x_prompt.shape
    bs, dseq, _ = x_sample.shape
    depth = w_in.shape[0]
    tp = N_META + seq
    ts = SAMPLE_PAD_T
    np_rows = bp * tp
    ns_rows = bs * dseq
    n = np_rows + ns_rows
    alpha = (2 * depth) ** 0.25
    row = lambda a: a.reshape(1, -1)

    meta = jnp.broadcast_to(meta_tokens[None], (bp, N_META, D_MODEL))
    xp_in = jnp.concatenate([meta, x_prompt], axis=1).reshape(np_rows, D_MODEL)
    xs_in = jnp.pad(x_sample, ((0, 0), (0, ts - dseq), (0, 0))).reshape(bs * ts, D_MODEL)
    zeros_p = (jnp.zeros((bp, CONV_A - 1, D_A), F32), jnp.zeros((bp, 1, D_A), F32),
               jnp.zeros((bp, CONV_B - 1, D_B), F32), jnp.zeros((bp, N_HEADS_C, DK_C, DV_C), F32))
    sb = _pick(bs, (16, 8, 4, 2, 1))
    tm_p = _pick(np_rows, (384, 512, 256, 128, 64, 32, 16, 8))
    tm_s = _pick(math.gcd(ns_rows, np_rows), (128, 64, 32, 16, 8))
    nblk = -(-(TOP_K * n) // MOE_BM) + N_EXPERTS
    while (nblk * MOE_BM // TOP_K) % math.lcm(tm_p, tm_s):
        nblk += 1

    new_p = ([], [], [], [])
    new_s = ([], [], [], [])
    x_all = None
    for l in range(depth):
        w_packed = _pack_w_in(w_in[l])
        wg = jnp.concatenate([_block_diag(w_rg[l]), _block_diag(w_ig[l])], axis=1).astype(BF16)
        bg = jnp.concatenate([b_rg[l], b_ig[l]]).reshape(1, -1)
        w_out_b = w_out[l].astype(BF16)
        ng = row(jnp.tile(gla_norm_g[l], N_HEADS_C))
        if l == 0:
            pa_p, pb_p, pc_p, xn_p = _inproj(xp_in, 0, np_rows, row(ln0_g), row(ln0_b), w_packed, True)
            pa_s, pb_s, pc_s, xn_s = _inproj(xs_in, 0, bs * ts, row(ln0_g), row(ln0_b), w_packed, True)
            res_p = xn_p
        else:
            pa_p, pb_p, pc_p = _inproj(x_all, 0, np_rows, row(ln0_g), row(ln0_b), w_packed, False)
            xn_s = jnp.pad(x_all[np_rows:].reshape(bs, dseq, D_MODEL),
                           ((0, 0), (0, ts - dseq), (0, 0))).reshape(bs * ts, D_MODEL)
            pa_s, pb_s, pc_s = _inproj(xn_s, 0, bs * ts, row(ln0_g), row(ln0_b), w_packed, False)
            res_p = x_all

        outs = []
        for (pa, pb, pc, nb, t, tv, bb, st) in (
                (pa_p, pb_p, pc_p, bp, tp, tp, 1, zeros_p),
                (pa_s, pb_s, pc_s, bs, ts, dseq, sb,
                 (state_conv_a[l], state_rglru[l].reshape(bs, 1, D_A), state_conv_b[l], state_gla[l]))):
            ya, ca_new, h_last = _rglru(pa.reshape(nb, t, PA_W), st[0], st[1], conv_a_w[l], row(conv_a_b[l]),
                                        wg, bg, row(lru_lambda[l]), tv, bb)
            yb, cb_new = _convb(pb.reshape(nb, t, PB_W), st[2], conv_b_w[l], row(conv_b_b[l]),
                                row(ln_b_g[l]), row(ln_b_b[l]), tv, bb)
            yc, s_new = _gla(pc, nb, t, st[3], w_gate2[l], row(b_gate[l]), ng, tv, bb)
            outs.append((ya.reshape(nb * t, D_A), yb.reshape(nb * t, D_B), yc,
                         ca_new, h_last.reshape(nb, D_A), cb_new, s_new))
        (ya_p, yb_p, yc_p, *st_p), (ya_s, yb_s, yc_s, *st_s) = outs
        for j in range(4):
            new_p[j].append(st_p[j])
            new_s[j].append(st_s[j])

        valid = lambda a: a.reshape(bs, ts, -1)[:, :dseq].reshape(ns_rows, -1)
        h_all = _outproj(ya_p, yb_p, yc_p, res_p, 0, w_out_b, row(ln1_g[l]), row(ln1_b[l]), alpha,
                         tm_p, nblk * MOE_BM // TOP_K)
        h_all = _outproj(valid(ya_s), valid(yb_s), valid(yc_s), valid(xn_s), 0, w_out_b, row(ln1_g[l]),
                         row(ln1_b[l]), alpha, tm_s, nblk * MOE_BM // TOP_K, out_row0=np_rows, into=h_all)

        top_idx, gates, cnt = _router(h_all, n, router_w[l].T, router_b[l].reshape(-1, 1))
        base, pad_shift, block_e, n_used = _group_layout(cnt[:, ::LANES], nblk, MOE_BM)
        dest = _dest(top_idx, base)
        inv = _invert(pad_shift, dest, MOE_BM)
        y4 = _moe(block_e, n_used, inv, h_all, l, moe_w1, moe_b1[:, :, None, :], moe_w2, moe_b2[:, :, None, :],
                  MOE_BM)
        comb = functools.partial(_combine, y4, gates.T.reshape(-1), h_all, g=row(ln2_g[l]), b=row(ln2_b[l]),
                                 alpha=alpha)
        if l + 1 < depth:
            x_all = comb(0, n)
        else:
            out_p, out_s = comb(0, np_rows), comb(np_rows, ns_rows)

    y_p = out_p.reshape(bp, tp, D_MODEL)[:, N_META:]
    y_s = out_s.reshape(bs, dseq, D_MODEL)
    return (y_p, y_s,
            jnp.stack(new_p[0]), jnp.stack(new_p[1]), jnp.stack(new_p[2]), jnp.stack(new_p[3]),
            jnp.stack(new_s[0]), jnp.stack(new_s[1]), jnp.stack(new_s[2]), jnp.stack(new_s[3]))
```

```python
import functools
import math

import jax
import jax.numpy as jnp
from jax import lax
from jax.experimental import pallas as pl
from jax.experimental.pallas import tpu as pltpu
from jax.experimental.pallas import tpu_sc as plsc

F32 = jnp.float32
BF16 = jnp.bfloat16

D_MODEL = 1024
N_META = 16
D_A = 384
D_B = 256
D_C = 384
N_BLK_A = 8
BLK_A = D_A // N_BLK_A
CONV_A = 4
RG_C = 8.0
CONV_B = 31
N_HEADS_C = 4
DV_C = D_C // N_HEADS_C
DK_C = DV_C // 2
D_QK_C = N_HEADS_C * DK_C
GATE_RANK = 16
GATE_TAU = 16.0
N_EXPERTS = 32
TOP_K = 4
D_FF = D_MODEL
SWIGLU_LIMIT = 7.0
SWIGLU_ALPHA = 1.702
LN_EPS = 1e-5

PA_W = 2 * D_A
PB_W = 2 * D_B
PC_Q, PC_Z, PC_K, PC_V, PC_R, PC_W = 0, 192, 256, 512, 896, 1280
P_W = PA_W + PB_W + PC_W

SUBLANES = 8
LANES = 128
SC_LANES = 16
SC_CORES, SC_SUBCORES = 2, 16
ROW_TILE = D_MODEL // LANES
VMEM_LIMIT_BYTES = 56 * 1024 * 1024
MOE_BM = 512
FF_CHUNK = 512
DMA_GROUP = 8
W_CAST_ROWS = 128
GLA_TRIP = 14
CONV_TRIP = 7
GLA_PASS_TRIP = 3
SAMPLE_PAD_T = 8


def _cparams(*sem):
    return pltpu.CompilerParams(dimension_semantics=sem, vmem_limit_bytes=VMEM_LIMIT_BYTES)


def _sds(shape, dtype=F32):
    return jax.ShapeDtypeStruct(shape, dtype)


def _pick(n, prefs):
    for p in prefs:
        if n % p == 0:
            return p
    raise ValueError(f"no tile for {n} in {prefs}")


def _ln(x, g, b):
    mu = jnp.mean(x, axis=-1, keepdims=True)
    xc = x - mu
    var = jnp.mean(xc * xc, axis=-1, keepdims=True)
    return xc * lax.rsqrt(var + LN_EPS) * g + b


def _sigmoid(x):
    return 1.0 / (1.0 + jnp.exp(-x))


def _split_bf16(x):
    hi = x.astype(BF16)
    lo = (x - hi.astype(F32)).astype(BF16)
    return hi, lo


def _dot(a, b):
    return jnp.dot(a, b, preferred_element_type=F32)


def _for_each(n, per_trip, fn):
    peel = n % per_trip
    for i in range(peel):
        fn(i)

    def trip(t, carry):
        for u in range(per_trip):
            fn(peel + per_trip * t + u)
        return carry
    lax.fori_loop(0, n // per_trip, trip, 0)


def _inproj_kernel(x_ref, g_ref, b_ref, w_ref, pa_ref, pb_ref, pc_ref, *maybe_xn, apply_ln):
    x = x_ref[...]
    if apply_ln:
        x = _ln(x, g_ref[...], b_ref[...])
        maybe_xn[0][...] = x
    xb = x.astype(BF16)
    pa_ref[...] = _dot(xb, w_ref[:, 0:PA_W])
    pb_ref[...] = _dot(xb, w_ref[:, PA_W:PA_W + PB_W])
    pc_ref[...] = _dot(xb, w_ref[:, PA_W + PB_W:P_W])


def _inproj(x, row0, nrows, ln_g, ln_b, w_packed, apply_ln):
    tm = _pick(nrows, (384, 512, 256, 128, 64, 32, 16, 8))
    while row0 % tm:
        tm //= 2
    off = row0 // tm
    const = lambda i: (0, 0)
    row = lambda i: (i, 0)
    out_shape = [_sds((nrows, PA_W)), _sds((nrows, PB_W)), _sds((nrows, PC_W))]
    out_specs = [pl.BlockSpec((tm, PA_W), row), pl.BlockSpec((tm, PB_W), row), pl.BlockSpec((tm, PC_W), row)]
    if apply_ln:
        out_shape.append(_sds((nrows, D_MODEL)))
        out_specs.append(pl.BlockSpec((tm, D_MODEL), row))
    return pl.pallas_call(
        functools.partial(_inproj_kernel, apply_ln=apply_ln),
        grid=(nrows // tm,),
        in_specs=[pl.BlockSpec((tm, D_MODEL), lambda i: (i + off, 0)),
                  pl.BlockSpec((1, D_MODEL), const), pl.BlockSpec((1, D_MODEL), const),
                  pl.BlockSpec((D_MODEL, P_W), const)],
        out_specs=out_specs, out_shape=out_shape,
        compiler_params=_cparams("parallel"), name="inproj",
    )(x, ln_g, ln_b, w_packed)


def _rglru_kernel(p_ref, cbuf_ref, h0_ref, cw_ref, cb_ref, wg_ref, bg_ref, lam_ref,
                  y_ref, cnew_ref, hlast_ref, xp_scr, a_scr, h_scr, *, T, Tc, Tv, Bb):
    lam = lam_ref[...]
    softplus_neg = jnp.maximum(-lam, 0.0) + jnp.log1p(jnp.exp(-jnp.abs(lam)))
    c_decay = -RG_C * softplus_neg
    cw = cw_ref[...]
    cb = cb_ref[...]
    bg = bg_ref[...]
    sub = lax.broadcasted_iota(jnp.int32, (Tc // SUBLANES, SUBLANES, D_A), 1)
    halo = SUBLANES - (CONV_A - 1)
    for b in range(Bb):
        xp_scr[halo:SUBLANES, :] = cbuf_ref[b]
        xp_scr[SUBLANES:SUBLANES + T, :] = p_ref[b, :, 0:D_A]
        cnew_ref[b] = xp_scr[halo + Tv:SUBLANES + Tv, :]

        def chunk(ci, h_b):
            r0 = pl.multiple_of(ci * Tc, SUBLANES)
            win = xp_scr[pl.ds(r0, Tc + SUBLANES), :]
            xc = cb + cw[CONV_A - 1:CONV_A] * win[SUBLANES:SUBLANES + Tc]
            for j in range(CONV_A - 1):
                xc = xc + cw[j:j + 1] * pltpu.roll(win, Tc + SUBLANES - (halo + j), 0)[0:Tc]
            gates = _dot(xc.astype(BF16), wg_ref[...]) + bg
            r = _sigmoid(gates[:, 0:D_A])
            i = _sigmoid(gates[:, D_A:2 * D_A])
            log_a = c_decay * r
            a = jnp.exp(log_a)
            u = jnp.sqrt(1.0 - a * a) * (i * xc)
            a = a.reshape(Tc // SUBLANES, SUBLANES, D_A)
            u = u.reshape(Tc // SUBLANES, SUBLANES, D_A)
            for s in (1, 2, 4):
                keep = sub >= s
                a_prev = pltpu.roll(a, s, 1)
                u_prev = pltpu.roll(u, s, 1)
                u = jnp.where(keep, a * u_prev + u, u)
                a = jnp.where(keep, a * a_prev, a)
            a_scr[...] = a.reshape(Tc, D_A)
            h_scr[pl.ds(r0, Tc), :] = u.reshape(Tc, D_A)

            def group(gi, h_b):
                c0 = pl.multiple_of(gi * SUBLANES, SUBLANES)
                g0 = pl.multiple_of(r0 + gi * SUBLANES, SUBLANES)
                h8 = a_scr[pl.ds(c0, SUBLANES), :] * h_b + h_scr[pl.ds(g0, SUBLANES), :]
                h_scr[pl.ds(g0, SUBLANES), :] = h8
                return jnp.broadcast_to(h8[SUBLANES - 1:SUBLANES, :], (SUBLANES, D_A))

            h_b = lax.fori_loop(0, Tc // SUBLANES, group, h_b)
            ga = p_ref[b, pl.ds(r0, Tc), D_A:2 * D_A]
            gelu = 0.5 * ga * (1.0 + jnp.tanh(0.7978845608028654 * (ga + 0.044715 * ga * ga * ga)))
            y_ref[b, pl.ds(r0, Tc), :] = h_scr[pl.ds(r0, Tc), :] * gelu
            return h_b

        h_b = jnp.broadcast_to(h0_ref[b], (SUBLANES, D_A))
        lax.fori_loop(0, T // Tc, chunk, h_b)
        hlast_ref[b] = h_scr[Tv - 1:Tv, :]


def _rglru(pa3, cbuf, h0, cw, cb, wg, bg, lam, Tv, Bb):
    B, T, _ = pa3.shape
    Tc = _pick(T, (344, 256, 128, 64, 48, 32, 16, 8))
    const2 = lambda i: (0, 0)
    seq3 = lambda i: (i, 0, 0)
    return pl.pallas_call(
        functools.partial(_rglru_kernel, T=T, Tc=Tc, Tv=Tv, Bb=Bb),
        grid=(B // Bb,),
        in_specs=[pl.BlockSpec((Bb, T, PA_W), seq3), pl.BlockSpec((Bb, CONV_A - 1, D_A), seq3),
                  pl.BlockSpec((Bb, 1, D_A), seq3), pl.BlockSpec((CONV_A, D_A), const2),
                  pl.BlockSpec((1, D_A), const2), pl.BlockSpec((D_A, 2 * D_A), const2),
                  pl.BlockSpec((1, 2 * D_A), const2), pl.BlockSpec((1, D_A), const2)],
        out_specs=[pl.BlockSpec((Bb, T, D_A), seq3), pl.BlockSpec((Bb, CONV_A - 1, D_A), seq3),
                   pl.BlockSpec((Bb, 1, D_A), seq3)],
        out_shape=[_sds((B, T, D_A)), _sds((B, CONV_A - 1, D_A)), _sds((B, 1, D_A))],
        scratch_shapes=[pltpu.VMEM((T + 2 * SUBLANES, D_A), F32), pltpu.VMEM((Tc, D_A), F32),
                        pltpu.VMEM((T, D_A), F32)],
        compiler_params=_cparams("parallel"), name="rglru",
    )(pa3, cbuf, h0, cw, cb, wg, bg, lam)


B_HALO = 32


def _convb_kernel(p_ref, buf_ref, w_ref, cb_ref, g_ref, b_ref, y_ref, bnew_ref, u_scr, *, T, Tc, Tv, Bb):
    w = w_ref[...]
    cb = cb_ref[...]
    g = g_ref[...]
    bb = b_ref[...]
    first = B_HALO - (CONV_B - 1)
    for b in range(Bb):
        u_scr[0:first, :] = jnp.zeros((first, D_B), F32)
        u_scr[first:B_HALO, :] = buf_ref[b]
        u_scr[B_HALO:B_HALO + T, :] = p_ref[b, :, 0:D_B] * _sigmoid(p_ref[b, :, D_B:2 * D_B])
        bnew_ref[b] = u_scr[first + Tv:B_HALO + Tv, :]

        def chunk(ci):
            r0 = ci * Tc if isinstance(ci, int) else pl.multiple_of(ci * Tc, SUBLANES)
            win = u_scr[pl.ds(r0, Tc + B_HALO), :]
            shifted = [win] + [pltpu.roll(win, Tc + B_HALO - s, 0) for s in range(1, SUBLANES)]
            acc = cb
            for j in range(CONV_B):
                a, s = divmod(first + j, SUBLANES)
                acc = acc + w[j:j + 1] * shifted[s][a * SUBLANES:a * SUBLANES + Tc]
            yn = _ln(acc, g, bb)
            y_ref[b, pl.ds(r0, Tc), :] = yn * _sigmoid(yn)

        _for_each(T // Tc, CONV_TRIP, chunk)


def _convb(pb3, buf, w, cb, g, b, Tv, Bb):
    B, T, _ = pb3.shape
    Tc = _pick(T, (48, 32, 16, 8))
    const2 = lambda i: (0, 0)
    seq3 = lambda i: (i, 0, 0)
    return pl.pallas_call(
        functools.partial(_convb_kernel, T=T, Tc=Tc, Tv=Tv, Bb=Bb),
        grid=(B // Bb,),
        in_specs=[pl.BlockSpec((Bb, T, PB_W), seq3), pl.BlockSpec((Bb, CONV_B - 1, D_B), seq3),
                  pl.BlockSpec((CONV_B, D_B), const2), pl.BlockSpec((1, D_B), const2),
                  pl.BlockSpec((1, D_B), const2), pl.BlockSpec((1, D_B), const2)],
        out_specs=[pl.BlockSpec((Bb, T, D_B), seq3), pl.BlockSpec((Bb, CONV_B - 1, D_B), seq3)],
        out_shape=[_sds((B, T, D_B)), _sds((B, CONV_B - 1, D_B))],
        scratch_shapes=[pltpu.VMEM((T + B_HALO, D_B), F32)],
        compiler_params=_cparams("parallel"), name="convb",
    )(pb3, buf, w, cb, g, b)


def _gla_kernel(p_ref, s0_ref, wg2_ref, bgate_ref, ng_ref, y_ref, snew_ref, s_scr, g_scr, *, T, C, Tb, Tv, Bb):
    ri = lax.broadcasted_iota(jnp.int32, (C, C), 0)
    ci_ = lax.broadcasted_iota(jnp.int32, (C, C), 1)
    tril = ri >= ci_
    lane_k = lax.broadcasted_iota(jnp.int32, (1, D_QK_C), 1)
    lane_v = lax.broadcasted_iota(jnp.int32, (1, D_C), 1)
    hm_k = [(lane_k >= h * DK_C) & (lane_k < (h + 1) * DK_C) for h in range(N_HEADS_C)]
    hm_v = [(lane_v >= h * DV_C) & (lane_v < (h + 1) * DV_C) for h in range(N_HEADS_C)]
    rs = lax.broadcasted_iota(jnp.int32, (D_C, D_QK_C), 0)
    cs = lax.broadcasted_iota(jnp.int32, (D_C, D_QK_C), 1)
    bd_t = (rs >= 0) & (rs < 0)
    for h in range(N_HEADS_C):
        bd_t = bd_t | ((rs >= h * DV_C) & (rs < (h + 1) * DV_C) & (cs >= h * DK_C) & (cs < (h + 1) * DK_C))
    rm = lax.broadcasted_iota(jnp.int32, (D_C, D_C), 0)
    cm = lax.broadcasted_iota(jnp.int32, (D_C, D_C), 1)
    seg = (rm >= 0) & (rm < 0)
    for h in range(N_HEADS_C):
        seg = seg | ((rm >= h * DV_C) & (rm < (h + 1) * DV_C) & (cm >= h * DV_C) & (cm < (h + 1) * DV_C))
    mseg = jnp.where(seg, 1.0, 0.0).astype(BF16)
    wg2 = wg2_ref[...].astype(BF16)
    bgate = bgate_ref[...]
    ng = ng_ref[...]
    rowi = lax.broadcasted_iota(jnp.int32, (C, 1), 0)
    tdims = (((0,), (0,)), ((), ()))

    n_chunks = T // C
    nt_dims = (((1,), (1,)), ((), ()))
    tril4 = jnp.concatenate([tril] * N_HEADS_C, axis=0)
    scan_shifts = [s for s in (1, 2, 4, 8, 16, 32) if s < C]
    scan_keep = [rowi >= s for s in scan_shifts]

    def tile_rows(ti, tb, base=0):
        r0 = ti * tb
        return pl.ds(base + (r0 if isinstance(ti, int) else pl.multiple_of(r0, tb)), tb)

    def gates(ti):
        rows = tile_rows(ti, Tb)
        z = p_ref[rows, PC_Z:PC_Z + GATE_RANK]
        pre = _dot(z.astype(BF16), wg2) + bgate
        g = (jnp.minimum(pre, 0.0) - jnp.log1p(jnp.exp(-jnp.abs(pre)))) * (1.0 / GATE_TAU)
        rid = ti * Tb + lax.broadcasted_iota(jnp.int32, (Tb, 1), 0)
        if Bb > 1:
            rid = rid & (T - 1)
        g_scr[rows, :] = jnp.where(rid < Tv, g, 0.0)
    _for_each(Bb * T // Tb, GLA_PASS_TRIP, gates)

    for b in range(Bb):
        def chunk(ci):
            rows = tile_rows(ci, C, b * T)
            q = p_ref[rows, PC_Q:PC_Q + D_QK_C] * (DK_C ** -0.5)
            k = p_ref[rows, PC_K:PC_K + D_QK_C]
            v = p_ref[rows, PC_V:PC_V + D_C]
            k = jnp.where((ci * C + rowi) < Tv, k, 0.0)
            gcum = g_scr[rows, :]
            for s, keep in zip(scan_shifts, scan_keep):
                gcum = gcum + jnp.where(keep, pltpu.roll(gcum, s, 0), 0.0)
            g_last = gcum[C - 1:C, :]
            g_mid = gcum[C // 2 - 1:C // 2, :]
            vb = v.astype(BF16)
            qt = q * jnp.exp(gcum - g_mid)
            ktb = (k * jnp.exp(g_mid - gcum)).astype(BF16)
            q4 = jnp.concatenate([jnp.where(hm_k[h], qt, 0.0) for h in range(N_HEADS_C)], axis=0).astype(BF16)
            sc = lax.dot_general(q4, ktb, nt_dims, preferred_element_type=F32)
            r4 = _dot(jnp.where(tril4, sc, 0.0).astype(BF16), vb)
            o = jnp.where(hm_v[0], r4[0:C], 0.0)
            for h in range(1, N_HEADS_C):
                o = o + jnp.where(hm_v[h], r4[h * C:(h + 1) * C], 0.0)
            kd = (k * jnp.exp(g_last - gcum)).astype(BF16)
            upd_t = lax.dot_general(vb, kd, tdims, preferred_element_type=F32)
            return rows, o, (q * jnp.exp(gcum)).astype(BF16), jnp.exp(g_last), jnp.where(bd_t, upd_t, 0.0)

        def apply_state(rows, o, qg, dec, upd):
            s_in = s_scr[...]
            y_ref[rows, :] = o + lax.dot_general(qg, s_in.astype(BF16), nt_dims, preferred_element_type=F32)
            s_scr[...] = s_in * dec + upd

        def trip(ti, carry):
            parts = [chunk(peel + GLA_TRIP * ti + u) for u in range(GLA_TRIP)]
            for part in parts:
                apply_state(*part)
            return carry

        s_scr[...] = jnp.zeros((D_C, D_QK_C), F32)
        for h in range(N_HEADS_C):
            s_scr[h * DV_C:(h + 1) * DV_C, h * DK_C:(h + 1) * DK_C] = s0_ref[b, h]
        peel = n_chunks % GLA_TRIP
        for ci in range(peel):
            apply_state(*chunk(ci))
        lax.fori_loop(0, n_chunks // GLA_TRIP, trip, 0)
        for h in range(N_HEADS_C):
            snew_ref[b, h] = s_scr[h * DV_C:(h + 1) * DV_C, h * DK_C:(h + 1) * DK_C]

    def finish(ti):
        rows = tile_rows(ti, Tb)
        o = y_ref[rows, :]
        rg = p_ref[rows, PC_R:PC_R + D_C]
        o2_hi, o2_lo = _split_bf16(o * o)
        ms = (_dot(o2_hi, mseg) + _dot(o2_lo, mseg)) * (1.0 / DV_C)
        y_ref[rows, :] = o * lax.rsqrt(ms + LN_EPS) * ng * (rg * _sigmoid(rg))
    _for_each(Bb * T // Tb, GLA_PASS_TRIP, finish)


def _gla(pc, B, T, s0, wg2, bgate, ng, Tv, Bb):
    assert Bb == 1 or T & (T - 1) == 0
    C = _pick(T, (48, 32, 16, 8))
    Tb = _pick(Bb * T, (344, 256, 128, 64, 48, 32, 16, 8))
    const2 = lambda i: (0, 0)
    row2 = lambda i: (i, 0)
    seq4 = lambda i: (i, 0, 0, 0)
    st = (Bb, N_HEADS_C, DV_C, DK_C)
    y, s_new_t = pl.pallas_call(
        functools.partial(_gla_kernel, T=T, C=C, Tb=Tb, Tv=Tv, Bb=Bb),
        grid=(B // Bb,),
        in_specs=[pl.BlockSpec((Bb * T, PC_W), row2), pl.BlockSpec(st, seq4),
                  pl.BlockSpec((GATE_RANK, D_QK_C), const2), pl.BlockSpec((1, D_QK_C), const2),
                  pl.BlockSpec((1, D_C), const2)],
        out_specs=[pl.BlockSpec((Bb * T, D_C), row2), pl.BlockSpec(st, seq4)],
        out_shape=[_sds((B * T, D_C)), _sds((B, N_HEADS_C, DV_C, DK_C))],
        scratch_shapes=[pltpu.VMEM((D_C, D_QK_C), F32), pltpu.VMEM((Bb * T, D_QK_C), F32)],
        compiler_params=_cparams("parallel"), name="gla",
    )(pc, jnp.swapaxes(s0, 2, 3), wg2, bgate, ng)
    return y, jnp.swapaxes(s_new_t, 2, 3)


def _tiles_to_rows(ref, lead, rows):
    return jnp.concatenate([ref[lead + (pl.ds(s, rows, stride=ROW_TILE), slice(None))]
                            for s in range(ROW_TILE)], axis=1)


def _rows_to_tiles(ref, lead, rows, val):
    for s in range(ROW_TILE):
        ref[lead + (pl.ds(s, rows, stride=ROW_TILE), slice(None))] = val[:, s * LANES:(s + 1) * LANES]


def _outproj_kernel(ya_ref, yb_ref, yc_ref, x_ref, w_ref, g_ref, b_ref, *rest, alpha, n_blocks):
    h_ref = rest[-1]

    @pl.when(pl.program_id(0) < n_blocks)
    def _():
        y_cat = jnp.concatenate([ya_ref[...].astype(BF16), yb_ref[...].astype(BF16), yc_ref[...].astype(BF16)],
                                axis=1)
        y = _dot(y_cat, w_ref[...])
        h = _ln(alpha * x_ref[...] + y, g_ref[...], b_ref[...])
        _rows_to_tiles(h_ref, (), h.shape[0], h)

    @pl.when(pl.program_id(0) >= n_blocks)
    def _():
        h_ref[...] = jnp.zeros(h_ref.shape, F32)


def _outproj(ya, yb, yc, x, row0, w, g, b, alpha, tm, out_rows, out_row0=0, into=None):
    n = ya.shape[0]
    nb = n // tm
    assert n % tm == 0 and row0 % tm == 0 and out_row0 % tm == 0 and out_rows % tm == 0
    off, out_off = row0 // tm, out_row0 // tm
    grid = nb if into is not None else out_rows // tm
    row = lambda i: (jnp.minimum(i, nb - 1), 0)
    const = lambda i: (0, 0)
    in_specs = [pl.BlockSpec((tm, D_A), row), pl.BlockSpec((tm, D_B), row), pl.BlockSpec((tm, D_C), row),
                pl.BlockSpec((tm, D_MODEL), lambda i: (jnp.minimum(i, nb - 1) + off, 0)),
                pl.BlockSpec((D_MODEL, D_MODEL), const), pl.BlockSpec((1, D_MODEL), const),
                pl.BlockSpec((1, D_MODEL), const)]
    args = [ya, yb, yc, x, w, g, b]
    aliases = {}
    if into is not None:
        in_specs.append(pl.BlockSpec(memory_space=pl.ANY))
        args.append(into)
        aliases = {len(args) - 1: 0}
    return pl.pallas_call(
        functools.partial(_outproj_kernel, alpha=alpha, n_blocks=nb),
        grid=(grid,),
        in_specs=in_specs,
        out_specs=pl.BlockSpec((tm * ROW_TILE, LANES), lambda i: (i + out_off, 0)),
        out_shape=_sds((out_rows * ROW_TILE, LANES)),
        input_output_aliases=aliases,
        compiler_params=_cparams("arbitrary"), name="outproj",
    )(*args)


def _router_kernel(h_ref, wt_ref, b_ref, idx_ref, gate_ref, cnt_ref):
    nt = (((1,), (1,)), ((), ()))
    hh, hl = _split_bf16(_tiles_to_rows(h_ref, (), idx_ref.shape[1]))
    wh, wl = _split_bf16(wt_ref[...])
    logits = (lax.dot_general(wh, hh, nt, preferred_element_type=F32)
              + lax.dot_general(wh, hl, nt, preferred_element_type=F32)
              + lax.dot_general(wl, hh, nt, preferred_element_type=F32)) + b_ref[...]
    eid = lax.broadcasted_iota(jnp.int32, logits.shape, 0)
    vals = []
    member = jnp.zeros(logits.shape, F32)
    for k in range(TOP_K):
        m = jnp.max(logits, axis=0, keepdims=True)
        sel = jnp.min(jnp.where(logits == m, eid, N_EXPERTS), axis=0, keepdims=True)
        idx_ref[k:k + 1, :] = sel
        vals.append(m)
        hit = eid == sel
        member = jnp.where(hit, 1.0, member)
        logits = jnp.where(hit, -jnp.inf, logits)
    es = [jnp.exp(v - vals[0]) for v in vals]
    tot = es[0] + es[1] + es[2] + es[3]
    for k in range(TOP_K):
        gate_ref[k:k + 1, :] = es[k] / tot
    cnt_ref[...] = jnp.broadcast_to(jnp.sum(member, axis=1, keepdims=True), cnt_ref.shape)


def _router_tile(n):
    return _pick(n, (896, 640, 512, 384, 256, 128))


def _router(h, n, wt, b):
    tm = _router_tile(n)
    nt = n // tm
    return pl.pallas_call(
        _router_kernel,
        grid=(nt,),
        in_specs=[pl.BlockSpec((tm * ROW_TILE, LANES), lambda i: (i, 0)),
                  pl.BlockSpec((N_EXPERTS, D_MODEL), lambda i: (0, 0)),
                  pl.BlockSpec((N_EXPERTS, 1), lambda i: (0, 0))],
        out_specs=[pl.BlockSpec((TOP_K, tm), lambda i: (0, i)), pl.BlockSpec((TOP_K, tm), lambda i: (0, i)),
                   pl.BlockSpec((N_EXPERTS, LANES), lambda i: (0, i))],
        out_shape=[_sds((TOP_K, n), jnp.int32), _sds((TOP_K, n)), _sds((N_EXPERTS, nt * LANES))],
        compiler_params=_cparams("parallel"), name="router",
    )(h, wt, b)


def _dest_kernel(idx_ref, base_ref, dest_ref):
    tm = idx_ref.shape[1]
    eid = lax.broadcasted_iota(jnp.int32, (N_EXPERTS, tm), 0)
    hits = [eid == idx_ref[k:k + 1, :] for k in range(TOP_K)]
    member = jnp.zeros((N_EXPERTS, tm), F32)
    for k in range(TOP_K):
        member = jnp.where(hits[k], 1.0, member)
    earlier = (lax.broadcasted_iota(jnp.int32, (tm, tm), 0) < lax.broadcasted_iota(jnp.int32, (tm, tm), 1))
    rank = _dot(member.astype(BF16), jnp.where(earlier, 1.0, 0.0).astype(BF16))
    pos = base_ref[...] + rank
    for k in range(TOP_K):
        dest_ref[k:k + 1, :] = jnp.sum(jnp.where(hits[k], pos, 0.0), axis=0, keepdims=True).astype(jnp.int32)


def _dest(top_idx, base):
    n = top_idx.shape[1]
    tm = _router_tile(n)
    return pl.pallas_call(
        _dest_kernel,
        grid=(n // tm,),
        in_specs=[pl.BlockSpec((TOP_K, tm), lambda i: (0, i)),
                  pl.BlockSpec((None, N_EXPERTS, 1), lambda i: (i, 0, 0))],
        out_specs=pl.BlockSpec((TOP_K, tm), lambda i: (0, i)),
        out_shape=_sds((TOP_K, n), jnp.int32),
        compiler_params=_cparams("parallel"), name="moe_dest",
    )(top_idx, base)


def _invert(pad_shift, dest, bm):
    k, n = dest.shape
    n_rows = pad_shift.shape[0] * bm
    ch = next(c for c in range(min(n, 4096) // SC_LANES * SC_LANES, 0, -SC_LANES) if n % c == 0)
    fill = (jnp.arange(n_rows, dtype=jnp.int32).reshape(-1, bm) + pad_shift[:, None]).reshape(-1)
    mesh = plsc.VectorSubcoreMesh(core_axis_name="c", subcore_axis_name="s", num_cores=SC_CORES,
                                  num_subcores=SC_SUBCORES)

    @functools.partial(
        pl.kernel, mesh=mesh, out_type=_sds((n_rows,), jnp.int32),
        scratch_types=[pltpu.VMEM((n_rows,), jnp.int32), pltpu.VMEM((ch,), jnp.int32)],
        compiler_params=pltpu.CompilerParams(needs_layout_passes=False), name="moe_invert_sc")
    def run(fill_hbm, dest_hbm, inv_hbm, inv_v, d_v):
        first = jnp.logical_and(lax.axis_index("c") == 0, lax.axis_index("s") == 0)

        @pl.when(first)
        def _():
            pltpu.sync_copy(fill_hbm, inv_v)
            lanes = lax.iota(jnp.int32, SC_LANES)
            for slot in range(k):
                def chunk(c, carry):
                    pltpu.sync_copy(dest_hbm.at[pl.ds(slot * n + c * ch, ch)], d_v)

                    def vec(v, carry):
                        idx = d_v[pl.ds(v * SC_LANES, SC_LANES)]
                        plsc.store_scatter(inv_v, [idx], (c * ch + v * SC_LANES + lanes) * k + slot)
                        return carry
                    return lax.fori_loop(0, ch // SC_LANES, vec, carry)
                lax.fori_loop(0, n // ch, chunk, 0)
            pltpu.sync_copy(inv_v, inv_hbm)

    return run(fill, dest.reshape(-1))


def _moe_kernel(be_ref, nu_ref, inv_ref, h_hbm, w1f_ref, b1_ref, w2f_ref, b2_ref, y4_hbm,
                xbuf, xb, obuf, w1_ref, w2_ref, gsem, ssem, *, bm):
    i = pl.program_id(0)
    n_used = nu_ref[0]
    last_blk = pl.num_programs(0) - 1
    n_ff = D_FF // FF_CHUNK

    def tile(r):
        start = r * ROW_TILE
        return pl.ds(start if isinstance(r, int) else pl.multiple_of(start, ROW_TILE), ROW_TILE)

    def gather_row(blk, slot, j, r=None):
        r = inv_ref[blk * bm + j] if r is None else r
        t = lax.shift_right_logical(r, 2)
        pltpu.make_async_copy(h_hbm.at[tile(t), :], xbuf.at[slot, tile(j), :], gsem.at[slot]).start()

    def scatter_row(blk, slot, j, r=None):
        r = inv_ref[blk * bm + j] if r is None else r
        pltpu.make_async_copy(obuf.at[slot, tile(j), :], y4_hbm.at[tile(r), :], ssem.at[slot]).start()

    def wait_gather(slot):
        pltpu.make_async_copy(h_hbm.at[pl.ds(0, bm * ROW_TILE), :], xbuf.at[slot], gsem.at[slot]).wait()

    def wait_scatter(slot):
        pltpu.make_async_copy(obuf.at[slot], y4_hbm.at[pl.ds(0, bm * ROW_TILE), :], ssem.at[slot]).wait()

    def loop_rows(fn, blk, slot):
        def body(j, c):
            fn(blk, slot, j)
            return c
        lax.fori_loop(0, bm, body, 0)

    def issue_rows(fn, blk, slot, j_lo, j_hi):
        for j0 in range(j_lo, j_hi, DMA_GROUP):
            js = range(j0, j0 + DMA_GROUP)
            ids = [inv_ref[blk * bm + j] for j in js]
            for j, r in zip(js, ids):
                fn(blk, slot, j, r)

    def step():
        slot = i % 2

        @pl.when(i + 1 < n_used)
        def _():
            issue_rows(gather_row, i + 1, 1 - slot, 0, bm)
        wait_gather(slot)
        xb[...] = _tiles_to_rows(xbuf, (slot,), bm).astype(BF16)
        acc = jnp.zeros((bm, D_MODEL), F32)
        for c in range(n_ff):
            lo = c * FF_CHUNK
            x = xb[...]
            hg = _dot(x, w1_ref[:, lo:lo + FF_CHUNK]) + b1_ref[:, lo:lo + FF_CHUNK]
            hl = _dot(x, w1_ref[:, D_FF + lo:D_FF + lo + FF_CHUNK]) + b1_ref[:, D_FF + lo:D_FF + lo + FF_CHUNK]
            gate = jnp.minimum(hg, SWIGLU_LIMIT)
            lin = jnp.clip(hl, -SWIGLU_LIMIT, SWIGLU_LIMIT)
            act = gate * _sigmoid(SWIGLU_ALPHA * gate) * (lin + 1.0)
            acc = acc + _dot(act.astype(BF16), w2_ref[lo:lo + FF_CHUNK, :])

        @pl.when(i >= 2)
        def _():
            wait_scatter(slot)
        _rows_to_tiles(obuf, (slot,), bm, acc + b2_ref[...])
        issue_rows(scatter_row, i, slot, 0, bm)

    new_expert = jnp.logical_or(i == 0, be_ref[i] != be_ref[jnp.maximum(i - 1, 0)])

    @pl.when(jnp.logical_and(new_expert, i < n_used))
    def _():
        def cast_rows(r, c):
            rows = pl.ds(pl.multiple_of(r * W_CAST_ROWS, W_CAST_ROWS), W_CAST_ROWS)
            w1_ref[rows, :] = w1f_ref[rows, :].astype(BF16)
            w2_ref[rows, :] = w2f_ref[rows, :].astype(BF16)
            return c
        lax.fori_loop(0, D_MODEL // W_CAST_ROWS, cast_rows, 0)

    @pl.when(jnp.logical_and(i == 0, n_used > 0))
    def _():
        loop_rows(gather_row, 0, 0)

    @pl.when(i < n_used)
    def _():
        step()

    @pl.when(i == n_used - 1)
    def _():
        slot = i % 2
        wait_scatter(slot)

        @pl.when(i >= 1)
        def _():
            wait_scatter(1 - slot)
        obuf[0] = jnp.zeros((bm * ROW_TILE, LANES), F32)

        def zero_block(blk, c):
            rows = bm * ROW_TILE
            cp = pltpu.make_async_copy(obuf.at[0], y4_hbm.at[pl.ds(pl.multiple_of(blk * rows, rows), rows), :],
                                       ssem.at[0])
            cp.start()
            cp.wait()
            return c
        lax.fori_loop(n_used, last_blk + 1, zero_block, 0)


def _moe(block_e, n_used, inv, h, layer, w1, b1, w2, b2, bm):
    nblk = block_e.shape[0]
    wmap = lambda i, be, nu, iv: (layer, be[i], 0, 0)
    return pl.pallas_call(
        functools.partial(_moe_kernel, bm=bm),
        grid_spec=pltpu.PrefetchScalarGridSpec(
            num_scalar_prefetch=3, grid=(nblk,),
            in_specs=[pl.BlockSpec(memory_space=pl.ANY),
                      pl.BlockSpec((None, None, D_MODEL, 2 * D_FF), wmap),
                      pl.BlockSpec((None, None, 1, 2 * D_FF), wmap),
                      pl.BlockSpec((None, None, D_FF, D_MODEL), wmap),
                      pl.BlockSpec((None, None, 1, D_MODEL), wmap)],
            out_specs=pl.BlockSpec(memory_space=pl.ANY),
            scratch_shapes=[pltpu.VMEM((2, bm * ROW_TILE, LANES), F32), pltpu.VMEM((bm, D_MODEL), BF16),
                            pltpu.VMEM((2, bm * ROW_TILE, LANES), F32),
                            pltpu.VMEM((D_MODEL, 2 * D_FF), BF16), pltpu.VMEM((D_FF, D_MODEL), BF16),
                            pltpu.SemaphoreType.DMA((2,)), pltpu.SemaphoreType.DMA((2,))]),
        out_shape=_sds((nblk * bm * ROW_TILE, LANES)),
        compiler_params=_cparams("arbitrary"), name="moe_experts",
    )(block_e, n_used, inv, h, w1, b1, w2, b2)


def _combine_kernel(gates_ref, y4_ref, h_ref, g_ref, b_ref, out_ref, acc_scr, *, alpha, tok_off):
    tm = out_ref.shape[0]
    tok0 = tok_off + pl.program_id(0) * tm

    def group(gi, carry):
        t0 = pl.multiple_of(gi * SUBLANES, SUBLANES)
        tiles = []
        for u in range(SUBLANES):
            t = t0 + u
            tile = alpha * h_ref[pl.ds(pl.multiple_of(t * ROW_TILE, ROW_TILE), ROW_TILE), :]
            for k in range(TOP_K):
                r = (t * TOP_K + k) * ROW_TILE
                tile = tile + (gates_ref[(tok0 + t) * TOP_K + k]
                               * y4_ref[pl.ds(pl.multiple_of(r, ROW_TILE), ROW_TILE), :])
            tiles.append(tile.reshape(1, ROW_TILE, LANES))
        rows = jnp.swapaxes(jnp.concatenate(tiles, axis=0), 0, 1)
        for s in range(ROW_TILE):
            acc_scr[pl.ds(t0, SUBLANES), s * LANES:(s + 1) * LANES] = rows[s]
        return carry
    lax.fori_loop(0, tm // SUBLANES, group, 0)
    out_ref[...] = _ln(acc_scr[...], g_ref[...], b_ref[...])


def _combine(y4, gates_flat, h, row0, n, g, b, alpha):
    tm = _pick(math.gcd(n, row0) if row0 else n, (448, 384, 224, 128, 64, 32, 16, 8))
    off = row0 // tm
    row = lambda i, gt: (i + off, 0)
    const = lambda i, gt: (0, 0)
    return pl.pallas_call(
        functools.partial(_combine_kernel, alpha=alpha, tok_off=row0),
        grid_spec=pltpu.PrefetchScalarGridSpec(
            num_scalar_prefetch=1, grid=(n // tm,),
            in_specs=[pl.BlockSpec((TOP_K * tm * ROW_TILE, LANES), row),
                      pl.BlockSpec((tm * ROW_TILE, LANES), row),
                      pl.BlockSpec((1, D_MODEL), const), pl.BlockSpec((1, D_MODEL), const)],
            out_specs=pl.BlockSpec((tm, D_MODEL), lambda i, gt: (i, 0)),
            scratch_shapes=[pltpu.VMEM((tm, D_MODEL), F32)]),
        out_shape=_sds((n, D_MODEL)),
        compiler_params=_cparams("arbitrary"), name="moe_combine",
    )(gates_flat, y4, h, g, b)


def _group_layout(cnt_tiles, nblk, bm):
    cnt = cnt_tiles.astype(jnp.int32)
    counts = jnp.sum(cnt, axis=1)
    padded = (counts + bm - 1) // bm * bm
    pad_end = jnp.cumsum(padded)
    pad_start = pad_end - padded
    base = pad_start[:, None] + jnp.cumsum(cnt, axis=1) - cnt
    starts = jnp.arange(nblk, dtype=jnp.int32) * bm
    groups_before = jnp.sum((pad_end[None, :] <= starts[:, None]).astype(jnp.int32), axis=1)
    block_e = jnp.minimum(groups_before, N_EXPERTS - 1).astype(jnp.int32)
    n_used = (pad_end[-1] // bm).astype(jnp.int32).reshape(1)
    real_before = jnp.sum(jnp.where(pad_start[None, :] <= starts[:, None], counts[None, :], 0), axis=1)
    pad_shift = (jnp.sum(counts) - real_before).astype(jnp.int32)
    return base.T.astype(F32)[:, :, None], pad_shift, block_e, n_used


def _pack_w_in(w):
    o = 0
    xa, o = w[:, o:o + D_A], o + D_A
    ga, o = w[:, o:o + D_A], o + D_A
    vb, o = w[:, o:o + D_B], o + D_B
    gb, o = w[:, o:o + D_B], o + D_B
    q, o = w[:, o:o + D_QK_C], o + D_QK_C
    k, o = w[:, o:o + D_QK_C], o + D_QK_C
    v, o = w[:, o:o + D_C], o + D_C
    r, o = w[:, o:o + D_C], o + D_C
    z = w[:, o:o + GATE_RANK]
    zq = jnp.zeros((w.shape[0], PC_K - PC_Z - GATE_RANK), w.dtype)
    zk = jnp.zeros((w.shape[0], PC_V - PC_K - D_QK_C), w.dtype)
    return jnp.concatenate([xa, ga, vb, gb, q, z, zq, k, zk, v, r], axis=1).astype(BF16)


def _block_diag(w):
    eye = jnp.eye(N_BLK_A, dtype=w.dtype)
    return jnp.einsum("hij,hg->higj", w, eye).reshape(D_A, D_A)


def kernel(x_prompt, x_sample, state_conv_a, state_rglru, state_conv_b, state_gla, meta_tokens, ln0_g, ln0_b,
           w_in, conv_a_w, conv_a_b, w_rg, b_rg, w_ig, b_ig, lru_lambda, conv_b_w, conv_b_b, ln_b_g, ln_b_b,
           w_gate2, b_gate, gla_norm_g, w_out, ln1_g, ln1_b, router_w, router_b, moe_w1, moe_b1, moe_w2, moe_b2,
           ln2_g, ln2_b):
    bp, seq, _ = x_prompt.shape
    bs, dseq, _ = x_sample.shape
    depth = w_in.shape[0]
    tp = N_META + seq
    ts = SAMPLE_PAD_T
    np_rows = bp * tp
    ns_rows = bs * dseq
    n = np_rows + ns_rows
    alpha = (2 * depth) ** 0.25
    row = lambda a: a.reshape(1, -1)

    meta = jnp.broadcast_to(meta_tokens[None], (bp, N_META, D_MODEL))
    xp_in = jnp.concatenate([meta, x_prompt], axis=1).reshape(np_rows, D_MODEL)
    xs_in = jnp.pad(x_sample, ((0, 0), (0, ts - dseq), (0, 0))).reshape(bs * ts, D_MODEL)
    zeros_p = (jnp.zeros((bp, CONV_A - 1, D_A), F32), jnp.zeros((bp, 1, D_A), F32),
               jnp.zeros((bp, CONV_B - 1, D_B), F32), jnp.zeros((bp, N_HEADS_C, DK_C, DV_C), F32))
    sb = _pick(bs, (16, 8, 4, 2, 1))
    tm_p = _pick(np_rows, (384, 512, 256, 128, 64, 32, 16, 8))
    tm_s = _pick(math.gcd(ns_rows, np_rows), (128, 64, 32, 16, 8))
    nblk = -(-(TOP_K * n) // MOE_BM) + N_EXPERTS
    while (nblk * MOE_BM // TOP_K) % math.lcm(tm_p, tm_s):
        nblk += 1

    new_p = ([], [], [], [])
    new_s = ([], [], [], [])
    x_all = None
    for l in range(depth):
        w_packed = _pack_w_in(w_in[l])
        wg = jnp.concatenate([_block_diag(w_rg[l]), _block_diag(w_ig[l])], axis=1).astype(BF16)
        bg = jnp.concatenate([b_rg[l], b_ig[l]]).reshape(1, -1)
        w_out_b = w_out[l].astype(BF16)
        ng = row(jnp.tile(gla_norm_g[l], N_HEADS_C))
        if l == 0:
            pa_p, pb_p, pc_p, xn_p = _inproj(xp_in, 0, np_rows, row(ln0_g), row(ln0_b), w_packed, True)
            pa_s, pb_s, pc_s, xn_s = _inproj(xs_in, 0, bs * ts, row(ln0_g), row(ln0_b), w_packed, True)
            res_p = xn_p
        else:
            pa_p, pb_p, pc_p = _inproj(x_all, 0, np_rows, row(ln0_g), row(ln0_b), w_packed, False)
            xn_s = jnp.pad(x_all[np_rows:].reshape(bs, dseq, D_MODEL),
                           ((0, 0), (0, ts - dseq), (0, 0))).reshape(bs * ts, D_MODEL)
            pa_s, pb_s, pc_s = _inproj(xn_s, 0, bs * ts, row(ln0_g), row(ln0_b), w_packed, False)
            res_p = x_all

        outs = []
        for (pa, pb, pc, nb, t, tv, bb, st) in (
                (pa_p, pb_p, pc_p, bp, tp, tp, 1, zeros_p),
                (pa_s, pb_s, pc_s, bs, ts, dseq, sb,
                 (state_conv_a[l], state_rglru[l].reshape(bs, 1, D_A), state_conv_b[l], state_gla[l]))):
            ya, ca_new, h_last = _rglru(pa.reshape(nb, t, PA_W), st[0], st[1], conv_a_w[l], row(conv_a_b[l]),
                                        wg, bg, row(lru_lambda[l]), tv, bb)
            yb, cb_new = _convb(pb.reshape(nb, t, PB_W), st[2], conv_b_w[l], row(conv_b_b[l]),
                                row(ln_b_g[l]), row(ln_b_b[l]), tv, bb)
            yc, s_new = _gla(pc, nb, t, st[3], w_gate2[l], row(b_gate[l]), ng, tv, bb)
            outs.append((ya.reshape(nb * t, D_A), yb.reshape(nb * t, D_B), yc,
                         ca_new, h_last.reshape(nb, D_A), cb_new, s_new))
        (ya_p, yb_p, yc_p, *st_p), (ya_s, yb_s, yc_s, *st_s) = outs
        for j in range(4):
            new_p[j].append(st_p[j])
            new_s[j].append(st_s[j])

        valid = lambda a: a.reshape(bs, ts, -1)[:, :dseq].reshape(ns_rows, -1)
        h_all = _outproj(ya_p, yb_p, yc_p, res_p, 0, w_out_b, row(ln1_g[l]), row(ln1_b[l]), alpha,
                         tm_p, nblk * MOE_BM // TOP_K)
        h_all = _outproj(valid(ya_s), valid(yb_s), valid(yc_s), valid(xn_s), 0, w_out_b, row(ln1_g[l]),
                         row(ln1_b[l]), alpha, tm_s, nblk * MOE_BM // TOP_K, out_row0=np_rows, into=h_all)

        top_idx, gates, cnt = _router(h_all, n, router_w[l].T, router_b[l].reshape(-1, 1))
        base, pad_shift, block_e, n_used = _group_layout(cnt[:, ::LANES], nblk, MOE_BM)
        dest = _dest(top_idx, base)
        inv = _invert(pad_shift, dest, MOE_BM)
        y4 = _moe(block_e, n_used, inv, h_all, l, moe_w1, moe_b1[:, :, None, :], moe_w2, moe_b2[:, :, None, :],
                  MOE_BM)
        comb = functools.partial(_combine, y4, gates.T.reshape(-1), h_all, g=row(ln2_g[l]), b=row(ln2_b[l]),
                                 alpha=alpha)
        if l + 1 < depth:
            x_all = comb(0, n)
        else:
            out_p, out_s = comb(0, np_rows), comb(np_rows, ns_rows)

    y_p = out_p.reshape(bp, tp, D_MODEL)[:, N_META:]
    y_s = out_s.reshape(bs, dseq, D_MODEL)
    return (y_p, y_s,
            jnp.stack(new_p[0]), jnp.stack(new_p[1]), jnp.stack(new_p[2]), jnp.stack(new_p[3]),
            jnp.stack(new_s[0]), jnp.stack(new_s[1]), jnp.stack(new_s[2]), jnp.stack(new_s[3]))
```

```python
import functools
import math

import jax
import jax.numpy as jnp
from jax import lax
from jax.experimental import pallas as pl
from jax.experimental.pallas import tpu as pltpu
from jax.experimental.pallas import tpu_sc as plsc

F32 = jnp.float32
BF16 = jnp.bfloat16

D_MODEL = 1024
N_META = 16
D_A = 384
D_B = 256
D_C = 384
N_BLK_A = 8
BLK_A = D_A // N_BLK_A
CONV_A = 4
RG_C = 8.0
CONV_B = 31
N_HEADS_C = 4
DV_C = D_C // N_HEADS_C
DK_C = DV_C // 2
D_QK_C = N_HEADS_C * DK_C
GATE_RANK = 16
GATE_TAU = 16.0
N_EXPERTS = 32
TOP_K = 4
D_FF = D_MODEL
SWIGLU_LIMIT = 7.0
SWIGLU_ALPHA = 1.702
LN_EPS = 1e-5

PA_W = 2 * D_A
PB_W = 2 * D_B
PC_Q, PC_Z, PC_K, PC_V, PC_R, PC_W = 0, 192, 256, 512, 896, 1280
P_W = PA_W + PB_W + PC_W

SUBLANES = 8
LANES = 128
SC_LANES = 16
SC_CORES, SC_SUBCORES = 2, 16
ROW_TILE = D_MODEL // LANES
VMEM_LIMIT_BYTES = 56 * 1024 * 1024
MOE_BM = 512
FF_CHUNK = 512
DMA_GROUP = 8
W_CAST_ROWS = 128
GLA_TRIP = 14
CONV_TRIP = 7
GLA_PASS_TRIP = 3
SAMPLE_PAD_T = 8


def _cparams(*sem):
    return pltpu.CompilerParams(dimension_semantics=sem, vmem_limit_bytes=VMEM_LIMIT_BYTES)


def _sds(shape, dtype=F32):
    return jax.ShapeDtypeStruct(shape, dtype)


def _pick(n, prefs):
    for p in prefs:
        if n % p == 0:
            return p
    raise ValueError(f"no tile for {n} in {prefs}")


def _ln(x, g, b):
    mu = jnp.mean(x, axis=-1, keepdims=True)
    xc = x - mu
    var = jnp.mean(xc * xc, axis=-1, keepdims=True)
    return xc * lax.rsqrt(var + LN_EPS) * g + b


def _sigmoid(x):
    return 1.0 / (1.0 + jnp.exp(-x))


def _split_bf16(x):
    hi = x.astype(BF16)
    lo = (x - hi.astype(F32)).astype(BF16)
    return hi, lo


def _dot(a, b):
    return jnp.dot(a, b, preferred_element_type=F32)


def _for_each(n, per_trip, fn):
    peel = n % per_trip
    for i in range(peel):
        fn(i)

    def trip(t, carry):
        for u in range(per_trip):
            fn(peel + per_trip * t + u)
        return carry
    lax.fori_loop(0, n // per_trip, trip, 0)


def _inproj_kernel(x_ref, g_ref, b_ref, w_ref, pa_ref, pb_ref, pc_ref, *maybe_xn, apply_ln):
    x = x_ref[...]
    if apply_ln:
        x = _ln(x, g_ref[...], b_ref[...])
        maybe_xn[0][...] = x
    xb = x.astype(BF16)
    pa_ref[...] = _dot(xb, w_ref[:, 0:PA_W])
    pb_ref[...] = _dot(xb, w_ref[:, PA_W:PA_W + PB_W])
    pc_ref[...] = _dot(xb, w_ref[:, PA_W + PB_W:P_W])


def _inproj(x, row0, nrows, ln_g, ln_b, w_packed, apply_ln):
    tm = _pick(nrows, (384, 512, 256, 128, 64, 32, 16, 8))
    while row0 % tm:
        tm //= 2
    off = row0 // tm
    const = lambda i: (0, 0)
    row = lambda i: (i, 0)
    out_shape = [_sds((nrows, PA_W)), _sds((nrows, PB_W)), _sds((nrows, PC_W))]
    out_specs = [pl.BlockSpec((tm, PA_W), row), pl.BlockSpec((tm, PB_W), row), pl.BlockSpec((tm, PC_W), row)]
    if apply_ln:
        out_shape.append(_sds((nrows, D_MODEL)))
        out_specs.append(pl.BlockSpec((tm, D_MODEL), row))
    return pl.pallas_call(
        functools.partial(_inproj_kernel, apply_ln=apply_ln),
        grid=(nrows // tm,),
        in_specs=[pl.BlockSpec((tm, D_MODEL), lambda i: (i + off, 0)),
                  pl.BlockSpec((1, D_MODEL), const), pl.BlockSpec((1, D_MODEL), const),
                  pl.BlockSpec((D_MODEL, P_W), const)],
        out_specs=out_specs, out_shape=out_shape,
        compiler_params=_cparams("parallel"), name="inproj",
    )(x, ln_g, ln_b, w_packed)


def _rglru_kernel(p_ref, cbuf_ref, h0_ref, cw_ref, cb_ref, wg_ref, bg_ref, lam_ref,
                  y_ref, cnew_ref, hlast_ref, xp_scr, a_scr, h_scr, *, T, Tc, Tv, Bb):
    lam = lam_ref[...]
    softplus_neg = jnp.maximum(-lam, 0.0) + jnp.log1p(jnp.exp(-jnp.abs(lam)))
    c_decay = -RG_C * softplus_neg
    cw = cw_ref[...]
    cb = cb_ref[...]
    bg = bg_ref[...]
    sub = lax.broadcasted_iota(jnp.int32, (Tc // SUBLANES, SUBLANES, D_A), 1)
    halo = SUBLANES - (CONV_A - 1)
    for b in range(Bb):
        xp_scr[halo:SUBLANES, :] = cbuf_ref[b]
        xp_scr[SUBLANES:SUBLANES + T, :] = p_ref[b, :, 0:D_A]
        cnew_ref[b] = xp_scr[halo + Tv:SUBLANES + Tv, :]

        def chunk(ci, h_b):
            r0 = pl.multiple_of(ci * Tc, SUBLANES)
            win = xp_scr[pl.ds(r0, Tc + SUBLANES), :]
            xc = cb + cw[CONV_A - 1:CONV_A] * win[SUBLANES:SUBLANES + Tc]
            for j in range(CONV_A - 1):
                xc = xc + cw[j:j + 1] * pltpu.roll(win, Tc + SUBLANES - (halo + j), 0)[0:Tc]
            gates = _dot(xc.astype(BF16), wg_ref[...]) + bg
            r = _sigmoid(gates[:, 0:D_A])
            i = _sigmoid(gates[:, D_A:2 * D_A])
            log_a = c_decay * r
            a = jnp.exp(log_a)
            u = jnp.sqrt(1.0 - a * a) * (i * xc)
            a = a.reshape(Tc // SUBLANES, SUBLANES, D_A)
            u = u.reshape(Tc // SUBLANES, SUBLANES, D_A)
            for s in (1, 2, 4):
                keep = sub >= s
                a_prev = pltpu.roll(a, s, 1)
                u_prev = pltpu.roll(u, s, 1)
                u = jnp.where(keep, a * u_prev + u, u)
                a = jnp.where(keep, a * a_prev, a)
            a_scr[...] = a.reshape(Tc, D_A)
            h_scr[pl.ds(r0, Tc), :] = u.reshape(Tc, D_A)

            def group(gi, h_b):
                c0 = pl.multiple_of(gi * SUBLANES, SUBLANES)
                g0 = pl.multiple_of(r0 + gi * SUBLANES, SUBLANES)
                h8 = a_scr[pl.ds(c0, SUBLANES), :] * h_b + h_scr[pl.ds(g0, SUBLANES), :]
                h_scr[pl.ds(g0, SUBLANES), :] = h8
                return jnp.broadcast_to(h8[SUBLANES - 1:SUBLANES, :], (SUBLANES, D_A))

            h_b = lax.fori_loop(0, Tc // SUBLANES, group, h_b)
            ga = p_ref[b, pl.ds(r0, Tc), D_A:2 * D_A]
            gelu = 0.5 * ga * (1.0 + jnp.tanh(0.7978845608028654 * (ga + 0.044715 * ga * ga * ga)))
            y_ref[b, pl.ds(r0, Tc), :] = h_scr[pl.ds(r0, Tc), :] * gelu
            return h_b

        h_b = jnp.broadcast_to(h0_ref[b], (SUBLANES, D_A))
        lax.fori_loop(0, T // Tc, chunk, h_b)
        hlast_ref[b] = h_scr[Tv - 1:Tv, :]


def _rglru(pa3, cbuf, h0, cw, cb, wg, bg, lam, Tv, Bb):
    B, T, _ = pa3.shape
    Tc = _pick(T, (344, 256, 128, 64, 48, 32, 16, 8))
    const2 = lambda i: (0, 0)
    seq3 = lambda i: (i, 0, 0)
    return pl.pallas_call(
        functools.partial(_rglru_kernel, T=T, Tc=Tc, Tv=Tv, Bb=Bb),
        grid=(B // Bb,),
        in_specs=[pl.BlockSpec((Bb, T, PA_W), seq3), pl.BlockSpec((Bb, CONV_A - 1, D_A), seq3),
                  pl.BlockSpec((Bb, 1, D_A), seq3), pl.BlockSpec((CONV_A, D_A), const2),
                  pl.BlockSpec((1, D_A), const2), pl.BlockSpec((D_A, 2 * D_A), const2),
                  pl.BlockSpec((1, 2 * D_A), const2), pl.BlockSpec((1, D_A), const2)],
        out_specs=[pl.BlockSpec((Bb, T, D_A), seq3), pl.BlockSpec((Bb, CONV_A - 1, D_A), seq3),
                   pl.BlockSpec((Bb, 1, D_A), seq3)],
        out_shape=[_sds((B, T, D_A)), _sds((B, CONV_A - 1, D_A)), _sds((B, 1, D_A))],
        scratch_shapes=[pltpu.VMEM((T + 2 * SUBLANES, D_A), F32), pltpu.VMEM((Tc, D_A), F32),
                        pltpu.VMEM((T, D_A), F32)],
        compiler_params=_cparams("parallel"), name="rglru",
    )(pa3, cbuf, h0, cw, cb, wg, bg, lam)


B_HALO = 32


def _convb_kernel(p_ref, buf_ref, w_ref, cb_ref, g_ref, b_ref, y_ref, bnew_ref, u_scr, *, T, Tc, Tv, Bb):
    w = w_ref[...]
    cb = cb_ref[...]
    g = g_ref[...]
    bb = b_ref[...]
    first = B_HALO - (CONV_B - 1)
    for b in range(Bb):
        u_scr[0:first, :] = jnp.zeros((first, D_B), F32)
        u_scr[first:B_HALO, :] = buf_ref[b]
        u_scr[B_HALO:B_HALO + T, :] = p_ref[b, :, 0:D_B] * _sigmoid(p_ref[b, :, D_B:2 * D_B])
        bnew_ref[b] = u_scr[first + Tv:B_HALO + Tv, :]

        def chunk(ci):
            r0 = ci * Tc if isinstance(ci, int) else pl.multiple_of(ci * Tc, SUBLANES)
            win = u_scr[pl.ds(r0, Tc + B_HALO), :]
            shifted = [win] + [pltpu.roll(win, Tc + B_HALO - s, 0) for s in range(1, SUBLANES)]
            acc = cb
            for j in range(CONV_B):
                a, s = divmod(first + j, SUBLANES)
                acc = acc + w[j:j + 1] * shifted[s][a * SUBLANES:a * SUBLANES + Tc]
            yn = _ln(acc, g, bb)
            y_ref[b, pl.ds(r0, Tc), :] = yn * _sigmoid(yn)

        _for_each(T // Tc, CONV_TRIP, chunk)


def _convb(pb3, buf, w, cb, g, b, Tv, Bb):
    B, T, _ = pb3.shape
    Tc = _pick(T, (48, 32, 16, 8))
    const2 = lambda i: (0, 0)
    seq3 = lambda i: (i, 0, 0)
    return pl.pallas_call(
        functools.partial(_convb_kernel, T=T, Tc=Tc, Tv=Tv, Bb=Bb),
        grid=(B // Bb,),
        in_specs=[pl.BlockSpec((Bb, T, PB_W), seq3), pl.BlockSpec((Bb, CONV_B - 1, D_B), seq3),
                  pl.BlockSpec((CONV_B, D_B), const2), pl.BlockSpec((1, D_B), const2),
                  pl.BlockSpec((1, D_B), const2), pl.BlockSpec((1, D_B), const2)],
        out_specs=[pl.BlockSpec((Bb, T, D_B), seq3), pl.BlockSpec((Bb, CONV_B - 1, D_B), seq3)],
        out_shape=[_sds((B, T, D_B)), _sds((B, CONV_B - 1, D_B))],
        scratch_shapes=[pltpu.VMEM((T + B_HALO, D_B), F32)],
        compiler_params=_cparams("parallel"), name="convb",
    )(pb3, buf, w, cb, g, b)


def _gla_kernel(p_ref, s0_ref, wg2_ref, bgate_ref, ng_ref, y_ref, snew_ref, s_scr, g_scr, *, T, C, Tb, Tv, Bb):
    ri = lax.broadcasted_iota(jnp.int32, (C, C), 0)
    ci_ = lax.broadcasted_iota(jnp.int32, (C, C), 1)
    tril = ri >= ci_
    lane_k = lax.broadcasted_iota(jnp.int32, (1, D_QK_C), 1)
    lane_v = lax.broadcasted_iota(jnp.int32, (1, D_C), 1)
    hm_k = [(lane_k >= h * DK_C) & (lane_k < (h + 1) * DK_C) for h in range(N_HEADS_C)]
    hm_v = [(lane_v >= h * DV_C) & (lane_v < (h + 1) * DV_C) for h in range(N_HEADS_C)]
    rs = lax.broadcasted_iota(jnp.int32, (D_C, D_QK_C), 0)
    cs = lax.broadcasted_iota(jnp.int32, (D_C, D_QK_C), 1)
    bd_t = (rs >= 0) & (rs < 0)
    for h in range(N_HEADS_C):
        bd_t = bd_t | ((rs >= h * DV_C) & (rs < (h + 1) * DV_C) & (cs >= h * DK_C) & (cs < (h + 1) * DK_C))
    rm = lax.broadcasted_iota(jnp.int32, (D_C, D_C), 0)
    cm = lax.broadcasted_iota(jnp.int32, (D_C, D_C), 1)
    seg = (rm >= 0) & (rm < 0)
    for h in range(N_HEADS_C):
        seg = seg | ((rm >= h * DV_C) & (rm < (h + 1) * DV_C) & (cm >= h * DV_C) & (cm < (h + 1) * DV_C))
    mseg = jnp.where(seg, 1.0, 0.0).astype(BF16)
    wg2 = wg2_ref[...].astype(BF16)
    bgate = bgate_ref[...]
    ng = ng_ref[...]
    rowi = lax.broadcasted_iota(jnp.int32, (C, 1), 0)
    tdims = (((0,), (0,)), ((), ()))

    n_chunks = T // C
    nt_dims = (((1,), (1,)), ((), ()))
    tril4 = jnp.concatenate([tril] * N_HEADS_C, axis=0)
    scan_shifts = [s for s in (1, 2, 4, 8, 16, 32) if s < C]
    scan_keep = [rowi >= s for s in scan_shifts]

    def tile_rows(ti, tb, base=0):
        r0 = ti * tb
        return pl.ds(base + (r0 if isinstance(ti, int) else pl.multiple_of(r0, tb)), tb)

    def gates(ti):
        rows = tile_rows(ti, Tb)
        z = p_ref[rows, PC_Z:PC_Z + GATE_RANK]
        pre = _dot(z.astype(BF16), wg2) + bgate
        g = (jnp.minimum(pre, 0.0) - jnp.log1p(jnp.exp(-jnp.abs(pre)))) * (1.0 / GATE_TAU)
        rid = ti * Tb + lax.broadcasted_iota(jnp.int32, (Tb, 1), 0)
        if Bb > 1:
            rid = rid & (T - 1)
        g_scr[rows, :] = jnp.where(rid < Tv, g, 0.0)
    _for_each(Bb * T // Tb, GLA_PASS_TRIP, gates)

    for b in range(Bb):
        def chunk(ci):
            rows = tile_rows(ci, C, b * T)
            q = p_ref[rows, PC_Q:PC_Q + D_QK_C] * (DK_C ** -0.5)
            k = p_ref[rows, PC_K:PC_K + D_QK_C]
            v = p_ref[rows, PC_V:PC_V + D_C]
            k = jnp.where((ci * C + rowi) < Tv, k, 0.0)
            gcum = g_scr[rows, :]
            for s, keep in zip(scan_shifts, scan_keep):
                gcum = gcum + jnp.where(keep, pltpu.roll(gcum, s, 0), 0.0)
            g_last = gcum[C - 1:C, :]
            g_mid = gcum[C // 2 - 1:C // 2, :]
            vb = v.astype(BF16)
            qt = q * jnp.exp(gcum - g_mid)
            ktb = (k * jnp.exp(g_mid - gcum)).astype(BF16)
            q4 = jnp.concatenate([jnp.where(hm_k[h], qt, 0.0) for h in range(N_HEADS_C)], axis=0).astype(BF16)
            sc = lax.dot_general(q4, ktb, nt_dims, preferred_element_type=F32)
            r4 = _dot(jnp.where(tril4, sc, 0.0).astype(BF16), vb)
            o = jnp.where(hm_v[0], r4[0:C], 0.0)
            for h in range(1, N_HEADS_C):
                o = o + jnp.where(hm_v[h], r4[h * C:(h + 1) * C], 0.0)
            kd = (k * jnp.exp(g_last - gcum)).astype(BF16)
            upd_t = lax.dot_general(vb, kd, tdims, preferred_element_type=F32)
            return rows, o, (q * jnp.exp(gcum)).astype(BF16), jnp.exp(g_last), jnp.where(bd_t, upd_t, 0.0)

        def apply_state(rows, o, qg, dec, upd):
            s_in = s_scr[...]
            y_ref[rows, :] = o + lax.dot_general(qg, s_in.astype(BF16), nt_dims, preferred_element_type=F32)
            s_scr[...] = s_in * dec + upd

        def trip(ti, carry):
            parts = [chunk(peel + GLA_TRIP * ti + u) for u in range(GLA_TRIP)]
            for part in parts:
                apply_state(*part)
            return carry

        s_scr[...] = jnp.zeros((D_C, D_QK_C), F32)
        for h in range(N_HEADS_C):
            s_scr[h * DV_C:(h + 1) * DV_C, h * DK_C:(h + 1) * DK_C] = s0_ref[b, h]
        peel = n_chunks % GLA_TRIP
        for ci in range(peel):
            apply_state(*chunk(ci))
        lax.fori_loop(0, n_chunks // GLA_TRIP, trip, 0)
        for h in range(N_HEADS_C):
            snew_ref[b, h] = s_scr[h * DV_C:(h + 1) * DV_C, h * DK_C:(h + 1) * DK_C]

    def finish(ti):
        rows = tile_rows(ti, Tb)
        o = y_ref[rows, :]
        rg = p_ref[rows, PC_R:PC_R + D_C]
        o2_hi, o2_lo = _split_bf16(o * o)
        ms = (_dot(o2_hi, mseg) + _dot(o2_lo, mseg)) * (1.0 / DV_C)
        y_ref[rows, :] = o * lax.rsqrt(ms + LN_EPS) * ng * (rg * _sigmoid(rg))
    _for_each(Bb * T // Tb, GLA_PASS_TRIP, finish)


def _gla(pc, B, T, s0, wg2, bgate, ng, Tv, Bb):
    assert Bb == 1 or T & (T - 1) == 0
    C = _pick(T, (48, 32, 16, 8))
    Tb = _pick(Bb * T, (344, 256, 128, 64, 48, 32, 16, 8))
    const2 = lambda i: (0, 0)
    row2 = lambda i: (i, 0)
    seq4 = lambda i: (i, 0, 0, 0)
    st = (Bb, N_HEADS_C, DV_C, DK_C)
    y, s_new_t = pl.pallas_call(
        functools.partial(_gla_kernel, T=T, C=C, Tb=Tb, Tv=Tv, Bb=Bb),
        grid=(B // Bb,),
        in_specs=[pl.BlockSpec((Bb * T, PC_W), row2), pl.BlockSpec(st, seq4),
                  pl.BlockSpec((GATE_RANK, D_QK_C), const2), pl.BlockSpec((1, D_QK_C), const2),
                  pl.BlockSpec((1, D_C), const2)],
        out_specs=[pl.BlockSpec((Bb * T, D_C), row2), pl.BlockSpec(st, seq4)],
        out_shape=[_sds((B * T, D_C)), _sds((B, N_HEADS_C, DV_C, DK_C))],
        scratch_shapes=[pltpu.VMEM((D_C, D_QK_C), F32), pltpu.VMEM((Bb * T, D_QK_C), F32)],
        compiler_params=_cparams("parallel"), name="gla",
    )(pc, jnp.swapaxes(s0, 2, 3), wg2, bgate, ng)
    return y, jnp.swapaxes(s_new_t, 2, 3)


def _tiles_to_rows(ref, lead, rows):
    return jnp.concatenate([ref[lead + (pl.ds(s, rows, stride=ROW_TILE), slice(None))]
                            for s in range(ROW_TILE)], axis=1)


def _rows_to_tiles(ref, lead, rows, val):
    for s in range(ROW_TILE):
        ref[lead + (pl.ds(s, rows, stride=ROW_TILE), slice(None))] = val[:, s * LANES:(s + 1) * LANES]


def _outproj_kernel(ya_ref, yb_ref, yc_ref, x_ref, w_ref, g_ref, b_ref, *rest, alpha, n_blocks):
    h_ref = rest[-1]

    @pl.when(pl.program_id(0) < n_blocks)
    def _():
        y_cat = jnp.concatenate([ya_ref[...].astype(BF16), yb_ref[...].astype(BF16), yc_ref[...].astype(BF16)],
                                axis=1)
        y = _dot(y_cat, w_ref[...])
        h = _ln(alpha * x_ref[...] + y, g_ref[...], b_ref[...])
        _rows_to_tiles(h_ref, (), h.shape[0], h)

    @pl.when(pl.program_id(0) >= n_blocks)
    def _():
        h_ref[...] = jnp.zeros(h_ref.shape, F32)


def _outproj(ya, yb, yc, x, row0, w, g, b, alpha, tm, out_rows, out_row0=0, into=None):
    n = ya.shape[0]
    nb = n // tm
    assert n % tm == 0 and row0 % tm == 0 and out_row0 % tm == 0 and out_rows % tm == 0
    off, out_off = row0 // tm, out_row0 // tm
    grid = nb if into is not None else out_rows // tm
    row = lambda i: (jnp.minimum(i, nb - 1), 0)
    const = lambda i: (0, 0)
    in_specs = [pl.BlockSpec((tm, D_A), row), pl.BlockSpec((tm, D_B), row), pl.BlockSpec((tm, D_C), row),
                pl.BlockSpec((tm, D_MODEL), lambda i: (jnp.minimum(i, nb - 1) + off, 0)),
                pl.BlockSpec((D_MODEL, D_MODEL), const), pl.BlockSpec((1, D_MODEL), const),
                pl.BlockSpec((1, D_MODEL), const)]
    args = [ya, yb, yc, x, w, g, b]
    aliases = {}
    if into is not None:
        in_specs.append(pl.BlockSpec(memory_space=pl.ANY))
        args.append(into)
        aliases = {len(args) - 1: 0}
    return pl.pallas_call(
        functools.partial(_outproj_kernel, alpha=alpha, n_blocks=nb),
        grid=(grid,),
        in_specs=in_specs,
        out_specs=pl.BlockSpec((tm * ROW_TILE, LANES), lambda i: (i + out_off, 0)),
        out_shape=_sds((out_rows * ROW_TILE, LANES)),
        input_output_aliases=aliases,
        compiler_params=_cparams("arbitrary"), name="outproj",
    )(*args)


def _router_kernel(h_ref, wt_ref, b_ref, idx_ref, gate_ref, cnt_ref):
    nt = (((1,), (1,)), ((), ()))
    hh, hl = _split_bf16(_tiles_to_rows(h_ref, (), idx_ref.shape[1]))
    wh, wl = _split_bf16(wt_ref[...])
    logits = (lax.dot_general(wh, hh, nt, preferred_element_type=F32)
              + lax.dot_general(wh, hl, nt, preferred_element_type=F32)
              + lax.dot_general(wl, hh, nt, preferred_element_type=F32)) + b_ref[...]
    eid = lax.broadcasted_iota(jnp.int32, logits.shape, 0)
    vals = []
    member = jnp.zeros(logits.shape, F32)
    for k in range(TOP_K):
        m = jnp.max(logits, axis=0, keepdims=True)
        sel = jnp.min(jnp.where(logits == m, eid, N_EXPERTS), axis=0, keepdims=True)
        idx_ref[k:k + 1, :] = sel
        vals.append(m)
        hit = eid == sel
        member = jnp.where(hit, 1.0, member)
        logits = jnp.where(hit, -jnp.inf, logits)
    es = [jnp.exp(v - vals[0]) for v in vals]
    tot = es[0] + es[1] + es[2] + es[3]
    for k in range(TOP_K):
        gate_ref[k:k + 1, :] = es[k] / tot
    cnt_ref[...] = jnp.broadcast_to(jnp.sum(member, axis=1, keepdims=True), cnt_ref.shape)


def _router_tile(n):
    return _pick(n, (896, 640, 512, 384, 256, 128))


def _router(h, n, wt, b):
    tm = _router_tile(n)
    nt = n // tm
    return pl.pallas_call(
        _router_kernel,
        grid=(nt,),
        in_specs=[pl.BlockSpec((tm * ROW_TILE, LANES), lambda i: (i, 0)),
                  pl.BlockSpec((N_EXPERTS, D_MODEL), lambda i: (0, 0)),
                  pl.BlockSpec((N_EXPERTS, 1), lambda i: (0, 0))],
        out_specs=[pl.BlockSpec((TOP_K, tm), lambda i: (0, i)), pl.BlockSpec((TOP_K, tm), lambda i: (0, i)),
                   pl.BlockSpec((N_EXPERTS, LANES), lambda i: (0, i))],
        out_shape=[_sds((TOP_K, n), jnp.int32), _sds((TOP_K, n)), _sds((N_EXPERTS, nt * LANES))],
        compiler_params=_cparams("parallel"), name="router",
    )(h, wt, b)


def _dest_kernel(idx_ref, base_ref, dest_ref):
    tm = idx_ref.shape[1]
    eid = lax.broadcasted_iota(jnp.int32, (N_EXPERTS, tm), 0)
    hits = [eid == idx_ref[k:k + 1, :] for k in range(TOP_K)]
    member = jnp.zeros((N_EXPERTS, tm), F32)
    for k in range(TOP_K):
        member = jnp.where(hits[k], 1.0, member)
    earlier = (lax.broadcasted_iota(jnp.int32, (tm, tm), 0) < lax.broadcasted_iota(jnp.int32, (tm, tm), 1))
    rank = _dot(member.astype(BF16), jnp.where(earlier, 1.0, 0.0).astype(BF16))
    pos = base_ref[...] + rank
    for k in range(TOP_K):
        dest_ref[k:k + 1, :] = jnp.sum(jnp.where(hits[k], pos, 0.0), axis=0, keepdims=True).astype(jnp.int32)


def _dest(top_idx, base):
    n = top_idx.shape[1]
    tm = _router_tile(n)
    return pl.pallas_call(
        _dest_kernel,
        grid=(n // tm,),
        in_specs=[pl.BlockSpec((TOP_K, tm), lambda i: (0, i)),
                  pl.BlockSpec((None, N_EXPERTS, 1), lambda i: (i, 0, 0))],
        out_specs=pl.BlockSpec((TOP_K, tm), lambda i: (0, i)),
        out_shape=_sds((TOP_K, n), jnp.int32),
        compiler_params=_cparams("parallel"), name="moe_dest",
    )(top_idx, base)


def _invert(pad_shift, dest, bm):
    k, n = dest.shape
    n_rows = pad_shift.shape[0] * bm
    ch = next(c for c in range(min(n, 4096) // SC_LANES * SC_LANES, 0, -SC_LANES) if n % c == 0)
    fill = (jnp.arange(n_rows, dtype=jnp.int32).reshape(-1, bm) + pad_shift[:, None]).reshape(-1)
    mesh = plsc.VectorSubcoreMesh(core_axis_name="c", subcore_axis_name="s", num_cores=SC_CORES,
                                  num_subcores=SC_SUBCORES)

    @functools.partial(
        pl.kernel, mesh=mesh, out_type=_sds((n_rows,), jnp.int32),
        scratch_types=[pltpu.VMEM((n_rows,), jnp.int32), pltpu.VMEM((ch,), jnp.int32)],
        compiler_params=pltpu.CompilerParams(needs_layout_passes=False), name="moe_invert_sc")
    def run(fill_hbm, dest_hbm, inv_hbm, inv_v, d_v):
        first = jnp.logical_and(lax.axis_index("c") == 0, lax.axis_index("s") == 0)

        @pl.when(first)
        def _():
            pltpu.sync_copy(fill_hbm, inv_v)
            lanes = lax.iota(jnp.int32, SC_LANES)
            for slot in range(k):
                def chunk(c, carry):
                    pltpu.sync_copy(dest_hbm.at[pl.ds(slot * n + c * ch, ch)], d_v)

                    def vec(v, carry):
                        idx = d_v[pl.ds(v * SC_LANES, SC_LANES)]
                        plsc.store_scatter(inv_v, [idx], (c * ch + v * SC_LANES + lanes) * k + slot)
                        return carry
                    return lax.fori_loop(0, ch // SC_LANES, vec, carry)
                lax.fori_loop(0, n // ch, chunk, 0)
            pltpu.sync_copy(inv_v, inv_hbm)

    return run(fill, dest.reshape(-1))


def _moe_kernel(be_ref, nu_ref, inv_ref, h_hbm, w1f_ref, b1_ref, w2f_ref, b2_ref, y4_hbm,
                xbuf, xb, obuf, w1_ref, w2_ref, gsem, ssem, *, bm):
    i = pl.program_id(0)
    n_used = nu_ref[0]
    last_blk = pl.num_programs(0) - 1
    n_ff = D_FF // FF_CHUNK

    def tile(r):
        start = r * ROW_TILE
        return pl.ds(start if isinstance(r, int) else pl.multiple_of(start, ROW_TILE), ROW_TILE)

    def gather_row(blk, slot, j, r=None):
        r = inv_ref[blk * bm + j] if r is None else r
        t = lax.shift_right_logical(r, 2)
        pltpu.make_async_copy(h_hbm.at[tile(t), :], xbuf.at[slot, tile(j), :], gsem.at[slot]).start()

    def scatter_row(blk, slot, j, r=None):
        r = inv_ref[blk * bm + j] if r is None else r
        pltpu.make_async_copy(obuf.at[slot, tile(j), :], y4_hbm.at[tile(r), :], ssem.at[slot]).start()

    def wait_gather(slot):
        pltpu.make_async_copy(h_hbm.at[pl.ds(0, bm * ROW_TILE), :], xbuf.at[slot], gsem.at[slot]).wait()

    def wait_scatter(slot):
        pltpu.make_async_copy(obuf.at[slot], y4_hbm.at[pl.ds(0, bm * ROW_TILE), :], ssem.at[slot]).wait()

    def loop_rows(fn, blk, slot):
        def body(j, c):
            fn(blk, slot, j)
            return c
        lax.fori_loop(0, bm, body, 0)

    def issue_rows(fn, blk, slot, j_lo, j_hi):
        for j0 in range(j_lo, j_hi, DMA_GROUP):
            js = range(j0, j0 + DMA_GROUP)
            ids = [inv_ref[blk * bm + j] for j in js]
            for j, r in zip(js, ids):
                fn(blk, slot, j, r)

    def step():
        slot = i % 2

        @pl.when(i + 1 < n_used)
        def _():
            issue_rows(gather_row, i + 1, 1 - slot, 0, bm)
        wait_gather(slot)
        xb[...] = _tiles_to_rows(xbuf, (slot,), bm).astype(BF16)
        acc = jnp.zeros((bm, D_MODEL), F32)
        for c in range(n_ff):
            lo = c * FF_CHUNK
            x = xb[...]
            hg = _dot(x, w1_ref[:, lo:lo + FF_CHUNK]) + b1_ref[:, lo:lo + FF_CHUNK]
            hl = _dot(x, w1_ref[:, D_FF + lo:D_FF + lo + FF_CHUNK]) + b1_ref[:, D_FF + lo:D_FF + lo + FF_CHUNK]
            gate = jnp.minimum(hg, SWIGLU_LIMIT)
            lin = jnp.clip(hl, -SWIGLU_LIMIT, SWIGLU_LIMIT)
            act = gate * _sigmoid(SWIGLU_ALPHA * gate) * (lin + 1.0)
            acc = acc + _dot(act.astype(BF16), w2_ref[lo:lo + FF_CHUNK, :])

        @pl.when(i >= 2)
        def _():
            wait_scatter(slot)
        _rows_to_tiles(obuf, (slot,), bm, acc + b2_ref[...])
        issue_rows(scatter_row, i, slot, 0, bm)

    new_expert = jnp.logical_or(i == 0, be_ref[i] != be_ref[jnp.maximum(i - 1, 0)])

    @pl.when(jnp.logical_and(new_expert, i < n_used))
    def _():
        def cast_rows(r, c):
            rows = pl.ds(pl.multiple_of(r * W_CAST_ROWS, W_CAST_ROWS), W_CAST_ROWS)
            w1_ref[rows, :] = w1f_ref[rows, :].astype(BF16)
            w2_ref[rows, :] = w2f_ref[rows, :].astype(BF16)
            return c
        lax.fori_loop(0, D_MODEL // W_CAST_ROWS, cast_rows, 0)

    @pl.when(jnp.logical_and(i == 0, n_used > 0))
    def _():
        loop_rows(gather_row, 0, 0)

    @pl.when(i < n_used)
    def _():
        step()

    @pl.when(i == n_used - 1)
    def _():
        slot = i % 2
        wait_scatter(slot)

        @pl.when(i >= 1)
        def _():
            wait_scatter(1 - slot)
        obuf[0] = jnp.zeros((bm * ROW_TILE, LANES), F32)

        def zero_block(blk, c):
            rows = bm * ROW_TILE
            cp = pltpu.make_async_copy(obuf.at[0], y4_hbm.at[pl.ds(pl.multiple_of(blk * rows, rows), rows), :],
                                       ssem.at[0])
            cp.start()
            cp.wait()
            return c
        lax.fori_loop(n_used, last_blk + 1, zero_block, 0)


def _moe(block_e, n_used, inv, h, layer, w1, b1, w2, b2, bm):
    nblk = block_e.shape[0]
    wmap = lambda i, be, nu, iv: (layer, be[i], 0, 0)
    return pl.pallas_call(
        functools.partial(_moe_kernel, bm=bm),
        grid_spec=pltpu.PrefetchScalarGridSpec(
            num_scalar_prefetch=3, grid=(nblk,),
            in_specs=[pl.BlockSpec(memory_space=pl.ANY),
                      pl.BlockSpec((None, None, D_MODEL, 2 * D_FF), wmap),
                      pl.BlockSpec((None, None, 1, 2 * D_FF), wmap),
                      pl.BlockSpec((None, None, D_FF, D_MODEL), wmap),
                      pl.BlockSpec((None, None, 1, D_MODEL), wmap)],
            out_specs=pl.BlockSpec(memory_space=pl.ANY),
            scratch_shapes=[pltpu.VMEM((2, bm * ROW_TILE, LANES), F32), pltpu.VMEM((bm, D_MODEL), BF16),
                            pltpu.VMEM((2, bm * ROW_TILE, LANES), F32),
                            pltpu.VMEM((D_MODEL, 2 * D_FF), BF16), pltpu.VMEM((D_FF, D_MODEL), BF16),
                            pltpu.SemaphoreType.DMA((2,)), pltpu.SemaphoreType.DMA((2,))]),
        out_shape=_sds((nblk * bm * ROW_TILE, LANES)),
        compiler_params=_cparams("arbitrary"), name="moe_experts",
    )(block_e, n_used, inv, h, w1, b1, w2, b2)


def _combine_kernel(gates_ref, y4_ref, h_ref, g_ref, b_ref, out_ref, acc_scr, *, alpha, tok_off):
    tm = out_ref.shape[0]
    tok0 = tok_off + pl.program_id(0) * tm

    def group(gi, carry):
        t0 = pl.multiple_of(gi * SUBLANES, SUBLANES)
        tiles = []
        for u in range(SUBLANES):
            t = t0 + u
            tile = alpha * h_ref[pl.ds(pl.multiple_of(t * ROW_TILE, ROW_TILE), ROW_TILE), :]
            for k in range(TOP_K):
                r = (t * TOP_K + k) * ROW_TILE
                tile = tile + (gates_ref[(tok0 + t) * TOP_K + k]
                               * y4_ref[pl.ds(pl.multiple_of(r, ROW_TILE), ROW_TILE), :])
            tiles.append(tile.reshape(1, ROW_TILE, LANES))
        rows = jnp.swapaxes(jnp.concatenate(tiles, axis=0), 0, 1)
        for s in range(ROW_TILE):
            acc_scr[pl.ds(t0, SUBLANES), s * LANES:(s + 1) * LANES] = rows[s]
        return carry
    lax.fori_loop(0, tm // SUBLANES, group, 0)
    out_ref[...] = _ln(acc_scr[...], g_ref[...], b_ref[...])


def _combine(y4, gates_flat, h, row0, n, g, b, alpha):
    tm = _pick(math.gcd(n, row0) if row0 else n, (448, 384, 224, 128, 64, 32, 16, 8))
    off = row0 // tm
    row = lambda i, gt: (i + off, 0)
    const = lambda i, gt: (0, 0)
    return pl.pallas_call(
        functools.partial(_combine_kernel, alpha=alpha, tok_off=row0),
        grid_spec=pltpu.PrefetchScalarGridSpec(
            num_scalar_prefetch=1, grid=(n // tm,),
            in_specs=[pl.BlockSpec((TOP_K * tm * ROW_TILE, LANES), row),
                      pl.BlockSpec((tm * ROW_TILE, LANES), row),
                      pl.BlockSpec((1, D_MODEL), const), pl.BlockSpec((1, D_MODEL), const)],
            out_specs=pl.BlockSpec((tm, D_MODEL), lambda i, gt: (i, 0)),
            scratch_shapes=[pltpu.VMEM((tm, D_MODEL), F32)]),
        out_shape=_sds((n, D_MODEL)),
        compiler_params=_cparams("arbitrary"), name="moe_combine",
    )(gates_flat, y4, h, g, b)


def _group_layout(cnt_tiles, nblk, bm):
    cnt = cnt_tiles.astype(jnp.int32)
    counts = jnp.sum(cnt, axis=1)
    padded = (counts + bm - 1) // bm * bm
    pad_end = jnp.cumsum(padded)
    pad_start = pad_end - padded
    base = pad_start[:, None] + jnp.cumsum(cnt, axis=1) - cnt
    starts = jnp.arange(nblk, dtype=jnp.int32) * bm
    groups_before = jnp.sum((pad_end[None, :] <= starts[:, None]).astype(jnp.int32), axis=1)
    block_e = jnp.minimum(groups_before, N_EXPERTS - 1).astype(jnp.int32)
    n_used = (pad_end[-1] // bm).astype(jnp.int32).reshape(1)
    real_before = jnp.sum(jnp.where(pad_start[None, :] <= starts[:, None], counts[None, :], 0), axis=1)
    pad_shift = (jnp.sum(counts) - real_before).astype(jnp.int32)
    return base.T.astype(F32)[:, :, None], pad_shift, block_e, n_used


def _pack_w_in(w):
    o = 0
    xa, o = w[:, o:o + D_A], o + D_A
    ga, o = w[:, o:o + D_A], o + D_A
    vb, o = w[:, o:o + D_B], o + D_B
    gb, o = w[:, o:o + D_B], o + D_B
    q, o = w[:, o:o + D_QK_C], o + D_QK_C
    k, o = w[:, o:o + D_QK_C], o + D_QK_C
    v, o = w[:, o:o + D_C], o + D_C
    r, o = w[:, o:o + D_C], o + D_C
    z = w[:, o:o + GATE_RANK]
    zq = jnp.zeros((w.shape[0], PC_K - PC_Z - GATE_RANK), w.dtype)
    zk = jnp.zeros((w.shape[0], PC_V - PC_K - D_QK_C), w.dtype)
    return jnp.concatenate([xa, ga, vb, gb, q, z, zq, k, zk, v, r], axis=1).astype(BF16)


def _block_diag(w):
    eye = jnp.eye(N_BLK_A, dtype=w.dtype)
    return jnp.einsum("hij,hg->higj", w, eye).reshape(D_A, D_A)


def kernel(x_prompt, x_sample, state_conv_a, state_rglru, state_conv_b, state_gla, meta_tokens, ln0_g, ln0_b,
           w_in, conv_a_w, conv_a_b, w_rg, b_rg, w_ig, b_ig, lru_lambda, conv_b_w, conv_b_b, ln_b_g, ln_b_b,
           w_gate2, b_gate, gla_norm_g, w_out, ln1_g, ln1_b, router_w, router_b, moe_w1, moe_b1, moe_w2, moe_b2,
           ln2_g, ln2_b):
    bp, seq, _ = x_prompt.shape
    bs, dseq, _ = x_sample.shape
    depth = w_in.shape[0]
    tp = N_META + seq
    ts = SAMPLE_PAD_T
    np_rows = bp * tp
    ns_rows = bs * dseq
    n = np_rows + ns_rows
    alpha = (2 * depth) ** 0.25
    row = lambda a: a.reshape(1, -1)

    meta = jnp.broadcast_to(meta_tokens[None], (bp, N_META, D_MODEL))
    xp_in = jnp.concatenate([meta, x_prompt], axis=1).reshape(np_rows, D_MODEL)
    xs_in = jnp.pad(x_sample, ((0, 0), (0, ts - dseq), (0, 0))).reshape(bs * ts, D_MODEL)
    zeros_p = (jnp.zeros((bp, CONV_A - 1, D_A), F32), jnp.zeros((bp, 1, D_A), F32),
               jnp.zeros((bp, CONV_B - 1, D_B), F32), jnp.zeros((bp, N_HEADS_C, DK_C, DV_C), F32))
    sb = _pick(bs, (32, 16, 8, 4, 2, 1))
    tm_p = _pick(np_rows, (384, 512, 256, 128, 64, 32, 16, 8))
    tm_s = _pick(math.gcd(ns_rows, np_rows), (128, 64, 32, 16, 8))
    nblk = -(-(TOP_K * n) // MOE_BM) + N_EXPERTS
    while (nblk * MOE_BM // TOP_K) % math.lcm(tm_p, tm_s):
        nblk += 1

    new_p = ([], [], [], [])
    new_s = ([], [], [], [])
    x_all = None
    for l in range(depth):
        w_packed = _pack_w_in(w_in[l])
        wg = jnp.concatenate([_block_diag(w_rg[l]), _block_diag(w_ig[l])], axis=1).astype(BF16)
        bg = jnp.concatenate([b_rg[l], b_ig[l]]).reshape(1, -1)
        w_out_b = w_out[l].astype(BF16)
        ng = row(jnp.tile(gla_norm_g[l], N_HEADS_C))
        if l == 0:
            pa_p, pb_p, pc_p, xn_p = _inproj(xp_in, 0, np_rows, row(ln0_g), row(ln0_b), w_packed, True)
            pa_s, pb_s, pc_s, xn_s = _inproj(xs_in, 0, bs * ts, row(ln0_g), row(ln0_b), w_packed, True)
            res_p = xn_p
        else:
            pa_p, pb_p, pc_p = _inproj(x_all, 0, np_rows, row(ln0_g), row(ln0_b), w_packed, False)
            xn_s = jnp.pad(x_all[np_rows:].reshape(bs, dseq, D_MODEL),
                           ((0, 0), (0, ts - dseq), (0, 0))).reshape(bs * ts, D_MODEL)
            pa_s, pb_s, pc_s = _inproj(xn_s, 0, bs * ts, row(ln0_g), row(ln0_b), w_packed, False)
            res_p = x_all

        outs = []
        for (pa, pb, pc, nb, t, tv, bb, st) in (
                (pa_p, pb_p, pc_p, bp, tp, tp, 1, zeros_p),
                (pa_s, pb_s, pc_s, bs, ts, dseq, sb,
                 (state_conv_a[l], state_rglru[l].reshape(bs, 1, D_A), state_conv_b[l], state_gla[l]))):
            ya, ca_new, h_last = _rglru(pa.reshape(nb, t, PA_W), st[0], st[1], conv_a_w[l], row(conv_a_b[l]),
                                        wg, bg, row(lru_lambda[l]), tv, bb)
            yb, cb_new = _convb(pb.reshape(nb, t, PB_W), st[2], conv_b_w[l], row(conv_b_b[l]),
                                row(ln_b_g[l]), row(ln_b_b[l]), tv, bb)
            yc, s_new = _gla(pc, nb, t, st[3], w_gate2[l], row(b_gate[l]), ng, tv, bb)
            outs.append((ya.reshape(nb * t, D_A), yb.reshape(nb * t, D_B), yc,
                         ca_new, h_last.reshape(nb, D_A), cb_new, s_new))
        (ya_p, yb_p, yc_p, *st_p), (ya_s, yb_s, yc_s, *st_s) = outs
        for j in range(4):
            new_p[j].append(st_p[j])
            new_s[j].append(st_s[j])

        valid = lambda a: a.reshape(bs, ts, -1)[:, :dseq].reshape(ns_rows, -1)
        h_all = _outproj(ya_p, yb_p, yc_p, res_p, 0, w_out_b, row(ln1_g[l]), row(ln1_b[l]), alpha,
                         tm_p, nblk * MOE_BM // TOP_K)
        h_all = _outproj(valid(ya_s), valid(yb_s), valid(yc_s), valid(xn_s), 0, w_out_b, row(ln1_g[l]),
                         row(ln1_b[l]), alpha, tm_s, nblk * MOE_BM // TOP_K, out_row0=np_rows, into=h_all)

        top_idx, gates, cnt = _router(h_all, n, router_w[l].T, router_b[l].reshape(-1, 1))
        base, pad_shift, block_e, n_used = _group_layout(cnt[:, ::LANES], nblk, MOE_BM)
        dest = _dest(top_idx, base)
        inv = _invert(pad_shift, dest, MOE_BM)
        y4 = _moe(block_e, n_used, inv, h_all, l, moe_w1, moe_b1[:, :, None, :], moe_w2, moe_b2[:, :, None, :],
                  MOE_BM)
        comb = functools.partial(_combine, y4, gates.T.reshape(-1), h_all, g=row(ln2_g[l]), b=row(ln2_b[l]),
                                 alpha=alpha)
        if l + 1 < depth:
            x_all = comb(0, n)
        else:
            out_p, out_s = comb(0, np_rows), comb(np_rows, ns_rows)

    y_p = out_p.reshape(bp, tp, D_MODEL)[:, N_META:]
    y_s = out_s.reshape(bs, dseq, D_MODEL)
    return (y_p, y_s,
            jnp.stack(new_p[0]), jnp.stack(new_p[1]), jnp.stack(new_p[2]), jnp.stack(new_p[3]),
            jnp.stack(new_s[0]), jnp.stack(new_s[1]), jnp.stack(new_s[2]), jnp.stack(new_s[3]))
```
